```python
import jax
import jax.numpy as jnp
from jax import lax
import numpy as np

D_MODEL = 1024
BATCH = 2
SEQ = 16384
DEPTH = 2

HEAD_DIM = 64
N_GROUPS = 4
MIX_W = N_GROUPS * HEAD_DIM
N_BRANCH = 4
CONV_W = 3
Q_LORA = 256
KV_LORA = 128
QK_NOPE = 64
QK_ROPE = 32
QK_HEAD = QK_NOPE + QK_ROPE
V_HEAD = 64
Q_BLOCK = 128
SG_CHUNK = 128
RET_CHUNK = 128
D_FF = 2816
N_EXPERTS = 8
TOP_K = 2
D_FF_EXPERT = 3584
N_DENSE = (DEPTH + 1) // 2
N_MOE = DEPTH // 2
ROPE_THETA = 10000.0
EPS = 1e-6
IN_SIZES = (MIX_W, MIX_W, MIX_W, Q_LORA, KV_LORA, QK_ROPE, MIX_W, MIX_W, MIX_W, MIX_W, MIX_W, MIX_W, N_BRANCH * D_MODEL)
D_IN = 9 * MIX_W + Q_LORA + KV_LORA + QK_ROPE + N_BRANCH * D_MODEL

kernel_name = 'hybrid_gated_mixers_moe_trunk'


def rms_norm(x, g):
    xf = x.astype(jnp.float32)
    y = xf * lax.rsqrt(jnp.mean(xf * xf, axis=-1, keepdims=True) + EPS)
    return (y * g.astype(jnp.float32)).astype(x.dtype)


def head_layer_norm(x, g):
    xf = x.astype(jnp.float32)
    xc = xf - jnp.mean(xf, axis=-1, keepdims=True)
    var = jnp.mean(xc * xc, axis=-1, keepdims=True)
    return (xc * lax.rsqrt(var + EPS) * g.astype(jnp.float32)).astype(x.dtype)


def rotary(x, positions):
    half = x.shape[-1] // 2
    inv_freq = ROPE_THETA ** (-jnp.arange(half, dtype=jnp.float32) / half)
    ang = positions.astype(jnp.float32)[:, :, None] * inv_freq
    cos = jnp.cos(ang)[:, :, None, :]
    sin = jnp.sin(ang)[:, :, None, :]
    xf = x.astype(jnp.float32)
    x1, x2 = xf[..., :half], xf[..., half:]
    return jnp.concatenate([x1 * cos - x2 * sin, x2 * cos + x1 * sin], axis=-1).astype(x.dtype)


def modulate(h, shift, scale):
    return h * (1.0 + scale) + shift


def short_conv_mixer(b_gate, c_gate, z, conv_w):
    u = c_gate * z
    y = lax.conv_general_dilated(u, conv_w[:, None, :].astype(u.dtype), window_strides=(1,),
                                 padding=[(CONV_W - 1, 0)], dimension_numbers=('NWC', 'WIO', 'NWC'),
                                 feature_group_count=MIX_W)
    return b_gate * y


def blocked_causal_attention(q, k, v):
    B, S, H, Dq = q.shape
    nb = S // Q_BLOCK
    qb = q.reshape(B, nb, Q_BLOCK, H, Dq).transpose(1, 0, 2, 3, 4)
    k_idx = jnp.arange(S)
    scale = Dq ** -0.5

    def one_block(args):
        q_blk, i = args
        s = jnp.einsum('bqhd,bkhd->bhqk', q_blk, k).astype(jnp.float32) * scale
        q_idx = i * Q_BLOCK + jnp.arange(Q_BLOCK)
        s = jnp.where(k_idx[None, :] <= q_idx[:, None], s, -jnp.inf)
        p = jax.nn.softmax(s, axis=-1).astype(v.dtype)
        return jnp.einsum('bhqk,bkhd->bqhd', p, v)

    out = lax.map(one_block, (qb, jnp.arange(nb)))
    return out.transpose(1, 0, 2, 3, 4).reshape(B, S, H, v.shape[-1])


def mla_mixer(c_q, c_kv, k_rope, positions, cq_g, w_uq, ckv_g, w_ukv, qn_g, kn_g):
    B, S, _ = c_q.shape
    q = (rms_norm(c_q, cq_g) @ w_uq).reshape(B, S, N_GROUPS, QK_HEAD)
    kv = (rms_norm(c_kv, ckv_g) @ w_ukv).reshape(B, S, N_GROUPS, QK_NOPE + V_HEAD)
    k_nope, v = kv[..., :QK_NOPE], kv[..., QK_NOPE:]
    k_pe = jnp.broadcast_to(k_rope[:, :, None, :], (B, S, N_GROUPS, QK_ROPE))
    k = jnp.concatenate([k_nope, k_pe], axis=-1)
    q = rms_norm(q, qn_g)
    k = rms_norm(k, kn_g)
    q = jnp.concatenate([q[..., :QK_NOPE], rotary(q[..., QK_NOPE:], positions)], axis=-1)
    k = jnp.concatenate([k[..., :QK_NOPE], rotary(k[..., QK_NOPE:], positions)], axis=-1)
    return blocked_causal_attention(q, k, v).reshape(B, S, N_GROUPS * V_HEAD)


def spatial_gating_mixer(u, v, gv_g, w_s, b_s):
    B, S, _ = u.shape
    nc = S // SG_CHUNK
    u = jax.nn.gelu(u).reshape(B, nc, SG_CHUNK, N_GROUPS, HEAD_DIM)
    v = rms_norm(jax.nn.gelu(v).reshape(B, S, N_GROUPS, HEAD_DIM), gv_g)
    v = v.reshape(B, nc, SG_CHUNK, N_GROUPS, HEAD_DIM)
    ws = jnp.tril(w_s)
    mixed = jnp.einsum('gts,bnsgc->bntgc', ws, v) + b_s.T[:, :, None]
    return (u * mixed).reshape(B, S, MIX_W)


def retention_mixer(q, k, v, g, positions, ret_g):
    B, S, _ = q.shape
    H, dk, C = N_GROUPS, HEAD_DIM, RET_CHUNK
    nc = S // C
    f32 = jnp.float32
    q = rotary(q.reshape(B, S, H, dk), positions).astype(f32)
    k = rotary(k.reshape(B, S, H, dk), positions).astype(f32) * dk ** -0.5
    v = v.reshape(B, S, H, dk).astype(f32)
    log_gamma = jnp.log1p(-(2.0 ** (-5.0 - jnp.arange(H, dtype=f32))))
    qc = q.reshape(B, nc, C, H, dk)
    kc = k.reshape(B, nc, C, H, dk)
    vc = v.reshape(B, nc, C, H, dk)
    pos = jnp.arange(C, dtype=f32)
    rel = pos[:, None] - pos[None, :]
    decay = jnp.where(rel >= 0, jnp.exp(log_gamma[:, None, None] * jnp.maximum(rel, 0.0)), 0.0)
    scores = jnp.einsum('bnihd,bnjhd->bnhij', qc, kc) * decay
    o_inner = jnp.einsum('bnhij,bnjhd->bnihd', scores, vc)
    k_decay = jnp.exp(log_gamma[None, :] * (C - 1.0 - pos)[:, None])
    kv_chunk = jnp.einsum('bnjhd,bnjhe->bnhde', kc * k_decay[:, :, None], vc)
    chunk_decay = jnp.exp(log_gamma * C)[None, :, None, None]

    def step(state, kv_n):
        return state * chunk_decay + kv_n, state

    init = jnp.zeros((B, H, dk, dk), f32)
    _, prev = lax.scan(step, init, kv_chunk.transpose(1, 0, 2, 3, 4))
    prev = prev.transpose(1, 0, 2, 3, 4)
    q_decay = jnp.exp(log_gamma[None, :] * (pos + 1.0)[:, None])
    o_cross = jnp.einsum('bnihd,bnhde->bnihe', qc * q_decay[:, :, None], prev)
    o = head_layer_norm((o_inner + o_cross).reshape(B, S, H, dk), ret_g).reshape(B, S, MIX_W)
    return (jax.nn.silu(g.astype(f32)) * o).astype(g.dtype)


def mixer_sublayer(h, positions, w_in, conv_w, cq_g, w_uq, ckv_g, w_ukv, qn_g, kn_g,
                   gv_g, w_s, b_s, ret_g, w_branch, w_o):
    B, S, _ = h.shape
    points = np.cumsum(IN_SIZES)[:-1].tolist()
    (a_b, a_c, a_x, m_cq, m_ckv, m_kr, s_u, s_v, r_q, r_k, r_v, r_g, gate_logits) = jnp.split(h @ w_in, points, axis=-1)
    ys = (
        short_conv_mixer(a_b, a_c, a_x, conv_w),
        mla_mixer(m_cq, m_ckv, m_kr, positions, cq_g, w_uq, ckv_g, w_ukv, qn_g, kn_g),
        spatial_gating_mixer(s_u, s_v, gv_g, w_s, b_s),
        retention_mixer(r_q, r_k, r_v, r_g, positions, ret_g),
    )
    merged = jnp.zeros((B, S, D_MODEL), h.dtype)
    for n in range(N_BRANCH):
        gate = jax.nn.sigmoid(gate_logits[..., n * D_MODEL:(n + 1) * D_MODEL])
        merged = merged + gate * (ys[n] @ w_branch[n])
    return merged @ w_o


def swiglu(h, w1, w3, w2):
    return (jax.nn.silu(h @ w1) * (h @ w3)) @ w2


def moe_swiglu(h, router_w, router_b, w1, w3, w2):
    B, S, D = h.shape
    t = h.reshape(B * S, D)
    n_tok = t.shape[0]
    probs = jax.nn.softmax((t @ router_w + router_b).astype(jnp.float32), axis=-1)
    top_p, top_i = lax.top_k(probs, TOP_K)
    top_p = top_p / jnp.sum(top_p, axis=-1, keepdims=True)
    flat_e = top_i.reshape(-1)
    order = jnp.argsort(flat_e)
    tok = order // TOP_K
    sizes = jnp.bincount(flat_e, length=N_EXPERTS).astype(jnp.int32)
    xs = t[tok]
    hid = jax.nn.silu(lax.ragged_dot(xs, w1, sizes)) * lax.ragged_dot(xs, w3, sizes)
    y = lax.ragged_dot(hid, w2, sizes) * top_p.reshape(-1)[order][:, None].astype(h.dtype)
    out = jnp.zeros((n_tok, D), h.dtype).at[tok].add(y)
    return out.reshape(B, S, D)


def setup_inputs(seed: int = 0) -> dict:
    key = jax.random.key(seed)
    ks = jax.random.split(key, 32)
    f32 = jnp.float32
    L = DEPTH

    def nrm(k, shape, scale):
        return jax.random.normal(k, shape, f32) * scale

    def gain(k, shape):
        return 1.0 + 0.02 * jax.random.normal(k, shape, f32)

    start = jax.random.randint(ks[2], (BATCH, 1), 0, 4096, dtype=jnp.int32)
    return {
        'x': nrm(ks[0], (BATCH, SEQ, D_MODEL), 1.0),
        'c': nrm(ks[1], (BATCH, D_MODEL), 1.0),
        'positions': start + jnp.arange(SEQ, dtype=jnp.int32)[None, :],
        'norm1_g': gain(ks[3], (L, D_MODEL)),
        'norm2_g': gain(ks[4], (L, D_MODEL)),
        'ada_w': nrm(ks[5], (L, D_MODEL, 6 * D_MODEL), D_MODEL ** -0.5),
        'ada_b': nrm(ks[6], (L, 6 * D_MODEL), 0.02),
        'w_in': nrm(ks[7], (L, D_MODEL, D_IN), D_MODEL ** -0.5),
        'conv_w': nrm(ks[8], (L, CONV_W, MIX_W), CONV_W ** -0.5),
        'cq_g': gain(ks[9], (L, Q_LORA)),
        'w_uq': nrm(ks[10], (L, Q_LORA, N_GROUPS * QK_HEAD), Q_LORA ** -0.5),
        'ckv_g': gain(ks[11], (L, KV_LORA)),
        'w_ukv': nrm(ks[12], (L, KV_LORA, N_GROUPS * (QK_NOPE + V_HEAD)), KV_LORA ** -0.5),
        'qn_g': gain(ks[13], (L, QK_HEAD)),
        'kn_g': gain(ks[14], (L, QK_HEAD)),
        'gv_g': gain(ks[15], (L, N_GROUPS, HEAD_DIM)),
        'w_s': nrm(ks[16], (L, N_GROUPS, SG_CHUNK, SG_CHUNK), SG_CHUNK ** -0.5),
        'b_s': gain(ks[17], (L, N_GROUPS, SG_CHUNK)),
        'ret_g': gain(ks[18], (L, N_GROUPS, HEAD_DIM)),
        'w_branch': nrm(ks[19], (L, N_BRANCH, MIX_W, D_MODEL), MIX_W ** -0.5),
        'w_o': nrm(ks[20], (L, D_MODEL, D_MODEL), D_MODEL ** -0.5),
        'ffn_w1': nrm(ks[21], (N_DENSE, D_MODEL, D_FF), D_MODEL ** -0.5),
        'ffn_w3': nrm(ks[22], (N_DENSE, D_MODEL, D_FF), D_MODEL ** -0.5),
        'ffn_w2': nrm(ks[23], (N_DENSE, D_FF, D_MODEL), D_FF ** -0.5),
        'router_w': nrm(ks[24], (N_MOE, D_MODEL, N_EXPERTS), D_MODEL ** -0.5),
        'router_b': nrm(ks[25], (N_MOE, N_EXPERTS), 0.01),
        'moe_w1': nrm(ks[26], (N_MOE, N_EXPERTS, D_MODEL, D_FF_EXPERT), D_MODEL ** -0.5),
        'moe_w3': nrm(ks[27], (N_MOE, N_EXPERTS, D_MODEL, D_FF_EXPERT), D_MODEL ** -0.5),
        'moe_w2': nrm(ks[28], (N_MOE, N_EXPERTS, D_FF_EXPERT, D_MODEL), D_FF_EXPERT ** -0.5),
    }


def reference(x, c, positions, norm1_g, norm2_g, ada_w, ada_b, w_in, conv_w, cq_g, w_uq, ckv_g, w_ukv,
              qn_g, kn_g, gv_g, w_s, b_s, ret_g, w_branch, w_o, ffn_w1, ffn_w3, ffn_w2,
              router_w, router_b, moe_w1, moe_w3, moe_w2):
    cond = jax.nn.silu(c)
    for l in range(DEPTH):
        ada = (cond @ ada_w[l] + ada_b[l])[:, None, :]
        sh1, sc1, g1, sh2, sc2, g2 = jnp.split(ada, 6, axis=-1)
        h = modulate(rms_norm(x, norm1_g[l]), sh1, sc1)
        x = x + g1 * mixer_sublayer(h, positions, w_in[l], conv_w[l], cq_g[l], w_uq[l], ckv_g[l], w_ukv[l],
                                    qn_g[l], kn_g[l], gv_g[l], w_s[l], b_s[l], ret_g[l], w_branch[l], w_o[l])
        h = modulate(rms_norm(x, norm2_g[l]), sh2, sc2)
        if l % 2 == 0:
            f = swiglu(h, ffn_w1[l // 2], ffn_w3[l // 2], ffn_w2[l // 2])
        else:
            f = moe_swiglu(h, router_w[l // 2], router_b[l // 2], moe_w1[l // 2], moe_w3[l // 2], moe_w2[l // 2])
        x = x + g2 * f
    return x
```

```python
import functools

import jax
import jax.numpy as jnp
import numpy as np
from jax import lax
from jax.experimental import pallas as pl
from jax.experimental.pallas import tpu as pltpu

F32 = jnp.float32
BF16 = jnp.bfloat16
HIGHEST = lax.Precision.HIGHEST

HEAD_DIM = 64
N_GROUPS = 4
MIX_W = N_GROUPS * HEAD_DIM
N_BRANCH = 4
CONV_W = 3
Q_LORA = 256
KV_LORA = 128
QK_NOPE = 64
QK_ROPE = 32
QK_HEAD = QK_NOPE + QK_ROPE
V_HEAD = 64
CHUNK = 128
N_EXPERTS = 8
TOP_K = 2
ROPE_THETA = 10000.0
EPS = 1e-6

LANES = 128
VMEM_LIMIT_BYTES = 56 * 1024 * 1024

COL_GATES = 0
COL_A = 4096
COL_CQ = COL_A + 3 * MIX_W
COL_R = 5120
COL_SU = 6144
COL_CKV = 6656
COL_KRA = 6784
COL_KRB = 6912
N_IN = 7168

TM_PROJ = 1024
TN_PROJ = 1024
TM_PREP = 1024
TQ_ATT = 1024
TM_MIX = 512
TM_FFN = 1024
TF_FFN = 256
TM_ROUTE = 1024
TM_DISP = 512
TG_MOE = 1024
TF_MOE = 512
TM_COMB = 512


def _cparams(*sem):
    return pltpu.CompilerParams(dimension_semantics=sem, vmem_limit_bytes=VMEM_LIMIT_BYTES)


def _sigmoid(x):
    return jax.nn.sigmoid(x)


def _pack_bf16_pair(lo, hi):
    lo_bits = lax.bitcast_convert_type(lo.astype(BF16).astype(F32), jnp.uint32)
    hi_bits = lax.bitcast_convert_type(hi.astype(BF16).astype(F32), jnp.uint32)
    return (lo_bits >> 16) | (hi_bits & jnp.uint32(0xFFFF0000))


def _unpack_bf16_pair(p):
    lo = lax.bitcast_convert_type(p << 16, F32)
    hi = lax.bitcast_convert_type(p & jnp.uint32(0xFFFF0000), F32)
    return lo, hi


def _norm_mod(x, g, shift, scale):
    y = x * lax.rsqrt(jnp.mean(x * x, axis=-1, keepdims=True) + EPS)
    return (y * g) * (1.0 + scale) + shift


def _ada_kernel(c_ref, w_ref, b_ref, o_ref):
    c = c_ref[...]
    cond = c * _sigmoid(c)
    o_ref[...] = jnp.dot(cond, w_ref[...], precision=HIGHEST, preferred_element_type=F32) + b_ref[...]


def _ada(c_pad, ada_w, ada_b):
    n_layer, d, d6 = ada_w.shape
    rows = c_pad.shape[0]
    tn = 1024
    return pl.pallas_call(
        _ada_kernel,
        out_shape=jax.ShapeDtypeStruct((n_layer, rows, d6), F32),
        grid=(n_layer, d6 // tn),
        in_specs=[
            pl.BlockSpec((rows, d), lambda l, j: (0, 0)),
            pl.BlockSpec((None, d, tn), lambda l, j: (l, 0, j)),
            pl.BlockSpec((None, 1, tn), lambda l, j: (l, 0, j)),
        ],
        out_specs=pl.BlockSpec((None, rows, tn), lambda l, j: (l, 0, j)),
        compiler_params=_cparams("parallel", "parallel"),
        name="ada_mod",
    )(c_pad, ada_w, ada_b.reshape(n_layer, 1, d6))


def _rope_kernel(pos_ref, invr_ref, invm_ref, signm_ref, cr_ref, sr_ref, cm_ref, sm_ref):
    pos = pos_ref[...]
    ang_r = pos * invr_ref[...]
    cr_ref[...] = jnp.cos(ang_r)
    sr_ref[...] = jnp.sin(ang_r)
    ang_m = pos * invm_ref[...]
    cm_ref[...] = jnp.cos(ang_m)
    sm_ref[...] = jnp.sin(ang_m) * signm_ref[...]


def _rope_tables(pos_f):
    n = pos_f.shape[0]
    tm = 1024
    half_r = HEAD_DIM // 2
    inv_r = ROPE_THETA ** (-jnp.arange(half_r, dtype=F32) / half_r)
    inv_r = jnp.tile(inv_r, LANES // half_r)[None, :]
    half_m = QK_ROPE // 2
    inv_m1 = ROPE_THETA ** (-jnp.arange(half_m, dtype=F32) / half_m)
    zeros = lambda k: jnp.zeros((k,), F32)
    inv_m = jnp.concatenate([zeros(QK_NOPE), inv_m1, inv_m1, zeros(LANES - QK_HEAD)])[None, :]
    sign_m = jnp.concatenate([zeros(QK_NOPE), -jnp.ones((half_m,), F32), jnp.ones((half_m,), F32),
                              zeros(LANES - QK_HEAD)])[None, :]
    row = pl.BlockSpec((1, LANES), lambda i: (0, 0))
    tab = pl.BlockSpec((tm, LANES), lambda i: (i, 0))
    shape = jax.ShapeDtypeStruct((n, LANES), F32)
    return pl.pallas_call(
        _rope_kernel,
        out_shape=(shape, shape, shape, shape),
        grid=(n // tm,),
        in_specs=[pl.BlockSpec((tm, 1), lambda i: (i, 0)), row, row, row],
        out_specs=(tab, tab, tab, tab),
        compiler_params=_cparams("parallel"),
        name="rope_tables",
    )(pos_f, inv_r, inv_m, sign_m)


def _inproj_kernel(x_ref, mod_ref, g_ref, w_ref, o_ref, h_scr):
    @pl.when(pl.program_id(1) == 0)
    def _():
        h = _norm_mod(x_ref[...], g_ref[...], mod_ref[0:1, :], mod_ref[1:2, :])
        h_scr[...] = h.astype(BF16)

    o_ref[...] = jnp.dot(h_scr[...], w_ref[...], preferred_element_type=F32).astype(BF16)


def _inproj(x, mod, g, w, seq):
    n, d = x.shape
    tm, tn = TM_PROJ, TN_PROJ
    tpb = seq // tm
    return pl.pallas_call(
        _inproj_kernel,
        out_shape=jax.ShapeDtypeStruct((n, N_IN), BF16),
        grid=(n // tm, N_IN // tn),
        in_specs=[
            pl.BlockSpec((tm, d), lambda i, j: (i, 0)),
            pl.BlockSpec((None, 6, d), lambda i, j: (i // tpb, 0, 0)),
            pl.BlockSpec((1, d), lambda i, j: (0, 0)),
            pl.BlockSpec((d, tn), lambda i, j: (0, j)),
        ],
        out_specs=pl.BlockSpec((tm, tn), lambda i, j: (i, j)),
        scratch_shapes=[pltpu.VMEM((tm, d), BF16)],
        compiler_params=_cparams("parallel", "arbitrary"),
        name="in_proj",
    )(x, mod, g, w)


def _mla_prep_kernel(cq_ref, ckv_ref, kra_ref, krb_ref, cm_ref, sm_ref, cqg_ref, wqa_ref, wqb_ref,
                     ckvg_ref, wk_ref, wv_ref, qga_ref, qgb_ref, kga_ref, kgb_ref,
                     q_ref, k_ref, v_ref):
    cq = cq_ref[...].astype(F32)
    cqn = (cq * lax.rsqrt(jnp.mean(cq * cq, axis=-1, keepdims=True) + EPS) * cqg_ref[...]).astype(BF16)
    qa = jnp.dot(cqn, wqa_ref[...], preferred_element_type=F32)
    qb = jnp.dot(cqn, wqb_ref[...], preferred_element_type=F32)
    ckv = ckv_ref[...].astype(F32)
    ckvn = (ckv * lax.rsqrt(jnp.mean(ckv * ckv, axis=-1, keepdims=True) + EPS) * ckvg_ref[...]).astype(BF16)
    ka = jnp.dot(ckvn, wk_ref[...], preferred_element_type=F32)
    v_ref[...] = jnp.dot(ckvn, wv_ref[...], preferred_element_type=F32).astype(BF16)
    kra = kra_ref[...].astype(F32)
    krb = krb_ref[...].astype(F32)
    cm = cm_ref[...]
    sm = sm_ref[...]
    scale = QK_HEAD ** -0.5
    for h in range(N_GROUPS):
        sl = slice(h * LANES, (h + 1) * LANES)
        qah, qbh = qa[:, sl], qb[:, sl]
        r = lax.rsqrt(jnp.sum(qah * qah, axis=-1, keepdims=True) * (1.0 / QK_HEAD) + EPS)
        q_rot = (qah * r) * qga_ref[...] * cm + (qbh * r) * qgb_ref[...] * sm
        q_ref[:, sl] = (q_rot * scale).astype(BF16)
        kah = ka[:, sl] + kra
        kbh = ka[:, sl] + krb
        r = lax.rsqrt(jnp.sum(kah * kah, axis=-1, keepdims=True) * (1.0 / QK_HEAD) + EPS)
        k_rot = (kah * r) * kga_ref[...] * cm + (kbh * r) * kgb_ref[...] * sm
        k_ref[:, sl] = k_rot.astype(BF16)


def _mla_prep(proj, cm, sm, p):
    n = proj.shape[0]
    tm = TM_PREP
    hw = N_GROUPS * LANES

    def col(width, offset):
        return pl.BlockSpec((tm, width), lambda i: (i, offset // width))

    def full(a):
        return pl.BlockSpec(a.shape, lambda i: (0,) * a.ndim)

    weights = [p["cq_g"], p["wqa"], p["wqb"], p["ckv_g"], p["wk"], p["wv"],
               p["qga"], p["qgb"], p["kga"], p["kgb"]]
    return pl.pallas_call(
        _mla_prep_kernel,
        out_shape=(jax.ShapeDtypeStruct((n, hw), BF16), jax.ShapeDtypeStruct((n, hw), BF16),
                   jax.ShapeDtypeStruct((n, MIX_W), BF16)),
        grid=(n // tm,),
        in_specs=[col(Q_LORA, COL_CQ), col(KV_LORA, COL_CKV), col(LANES, COL_KRA), col(LANES, COL_KRB),
                  pl.BlockSpec((tm, LANES), lambda i: (i, 0)), pl.BlockSpec((tm, LANES), lambda i: (i, 0))]
                 + [full(w) for w in weights],
        out_specs=(pl.BlockSpec((tm, hw), lambda i: (i, 0)), pl.BlockSpec((tm, hw), lambda i: (i, 0)),
                   pl.BlockSpec((tm, MIX_W), lambda i: (i, 0))),
        compiler_params=_cparams("parallel"),
        name="mla_prep",
    )(proj, proj, proj, proj, cm, sm, *weights)


def _flash_kernel(q_ref, k_ref, v_ref, o_ref, m_scr, l_scr, acc_scr, *, tq):
    i = pl.program_id(1)
    j = pl.program_id(2)

    @pl.when(j == 0)
    def _():
        m_scr[...] = jnp.full(m_scr.shape, -jnp.inf, F32)
        l_scr[...] = jnp.zeros(l_scr.shape, F32)
        acc_scr[...] = jnp.zeros(acc_scr.shape, F32)

    def step(masked):
        if masked:
            row = lax.broadcasted_iota(jnp.int32, (tq, tq), 0)
            col = lax.broadcasted_iota(jnp.int32, (tq, tq), 1)
            keep = col <= row
        for h in range(N_GROUPS):
            sl = slice(h * LANES, (h + 1) * LANES)
            s = lax.dot_general(q_ref[:, sl], k_ref[:, sl], (((1,), (1,)), ((), ())),
                                preferred_element_type=F32)
            if masked:
                s = jnp.where(keep, s, -jnp.inf)
            m_prev = m_scr[h]
            m_new = jnp.maximum(m_prev, jnp.max(s, axis=-1, keepdims=True))
            alpha = jnp.exp(m_prev - m_new)
            p = jnp.exp(s - m_new)
            l_scr[h] = alpha * l_scr[h] + jnp.sum(p, axis=-1, keepdims=True)
            pair = slice((h // 2) * LANES, (h // 2 + 1) * LANES)
            acc_scr[h] = alpha * acc_scr[h] + jnp.dot(p.astype(BF16), v_ref[:, pair],
                                                      preferred_element_type=F32)
            m_scr[h] = m_new

    @pl.when(j < i)
    def _():
        step(False)

    @pl.when(j == i)
    def _():
        step(True)
        lane = lax.broadcasted_iota(jnp.int32, (tq, LANES), 1)
        for pr in range(N_GROUPS // 2):
            lo = acc_scr[2 * pr] / l_scr[2 * pr]
            hi = acc_scr[2 * pr + 1] / l_scr[2 * pr + 1]
            o_ref[:, pr * LANES:(pr + 1) * LANES] = jnp.where(lane < V_HEAD, lo, hi).astype(BF16)


def _flash(q, k, v, batch, seq):
    n = q.shape[0]
    tq = TQ_ATT
    nq = seq // tq
    hw = N_GROUPS * LANES
    return pl.pallas_call(
        functools.partial(_flash_kernel, tq=tq),
        out_shape=jax.ShapeDtypeStruct((n, MIX_W), BF16),
        grid=(batch, nq, nq),
        in_specs=[
            pl.BlockSpec((tq, hw), lambda b, i, j: (b * nq + i, 0)),
            pl.BlockSpec((tq, hw), lambda b, i, j: (b * nq + jnp.minimum(j, i), 0)),
            pl.BlockSpec((tq, MIX_W), lambda b, i, j: (b * nq + jnp.minimum(j, i), 0)),
        ],
        out_specs=pl.BlockSpec((tq, MIX_W), lambda b, i, j: (b * nq + i, 0)),
        scratch_shapes=[pltpu.VMEM((N_GROUPS, tq, 1), F32), pltpu.VMEM((N_GROUPS, tq, 1), F32),
                        pltpu.VMEM((N_GROUPS, tq, LANES), F32)],
        compiler_params=_cparams("parallel", "parallel", "arbitrary"),
        name="mla_flash",
    )(q, k, v)


def _gelu_tanh(x):
    return jax.nn.gelu(x, approximate=True)


def _mix_kernel(gates_ref, a_ref, r_ref, su_ref, ymla_ref, x_ref, cos_ref, sin_ref, mod_ref,
                convw_ref, gvg_ref, wscat_ref, bsmat_ref, retg_ref, dec_ref, kdec_ref, qdec_ref,
                cdec_ref, bd_ref, gmat_ref, mk_ref, mv_ref, wb_ref, wo_ref,
                o_ref, carry_scr, state_scr, ysg_scr, yret_scr, *, tm, tpb):
    i = pl.program_id(0)

    @pl.when(i % tpb == 0)
    def _():
        carry_scr[...] = jnp.zeros(carry_scr.shape, F32)
        state_scr[...] = jnp.zeros(state_scr.shape, F32)

    w = MIX_W
    a_b = a_ref[:, 0:w].astype(F32)
    u = a_ref[:, w:2 * w].astype(F32) * a_ref[:, 2 * w:3 * w].astype(F32)
    rowi = lax.broadcasted_iota(jnp.int32, (tm, w), 0)
    prev1 = carry_scr[0:1, :]
    prev2 = carry_scr[1:2, :]
    u1 = jnp.where(rowi == 0, prev1, pltpu.roll(u, 1, axis=0))
    u2 = jnp.where(rowi == 0, prev2, jnp.where(rowi == 1, prev1, pltpu.roll(u, 2, axis=0)))
    carry_scr[0:1, :] = u[tm - 1:tm, :]
    carry_scr[1:2, :] = u[tm - 2:tm - 1, :]
    y_conv = a_b * (convw_ref[0:1, :] * u2 + convw_ref[1:2, :] * u1 + convw_ref[2:3, :] * u)

    gmat = gmat_ref[...]
    s_u = _gelu_tanh(su_ref[:, 0:w].astype(F32))
    s_v = _gelu_tanh(su_ref[:, w:2 * w].astype(F32))
    ms = jnp.dot(s_v * s_v, gmat, precision=HIGHEST, preferred_element_type=F32)
    vn = (s_v * lax.rsqrt(ms + EPS) * gvg_ref[...]).astype(BF16)

    cosr = cos_ref[...]
    sinr = sin_ref[...]

    def rot(t):
        t1, t2 = t[:, 0:LANES], t[:, LANES:2 * LANES]
        return jnp.concatenate([t1 * cosr - t2 * sinr, t2 * cosr + t1 * sinr], axis=-1)

    rq = rot(r_ref[:, 0:w].astype(F32))
    rk = rot(r_ref[:, w:2 * w].astype(F32)) * (HEAD_DIM ** -0.5)

    for c in range(tm // CHUNK):
        rows = slice(c * CHUNK, (c + 1) * CHUNK)
        vc = vn[rows, :]
        vbd = jnp.concatenate([vc * mv_ref[g:g + 1, :].astype(BF16) for g in range(N_GROUPS)], axis=0)
        mixed = jnp.dot(wscat_ref[...], vbd, preferred_element_type=F32) + bsmat_ref[...]
        ysg_scr[rows, :] = s_u[rows, :] * mixed

        qc = rq[rows, :]
        kc = rk[rows, :]
        kcb = kc.astype(BF16)
        vcb = r_ref[rows, 2 * w:3 * w]
        qstack = jnp.concatenate([(qc * mk_ref[h:h + 1, :]).astype(BF16) for h in range(N_GROUPS)], axis=0)
        sc = lax.dot_general(qstack, kcb, (((1,), (1,)), ((), ())), preferred_element_type=F32)
        sc = (sc * dec_ref[...]).astype(BF16)
        scat = jnp.concatenate([sc[h * CHUNK:(h + 1) * CHUNK, :] for h in range(N_GROUPS)], axis=1)
        vstack = jnp.concatenate([vcb * mv_ref[h:h + 1, :].astype(BF16) for h in range(N_GROUPS)], axis=0)
        o_c = jnp.dot(scat, vstack, preferred_element_type=F32)
        state = state_scr[...]
        o_c = o_c + jnp.dot((qc * qdec_ref[...]).astype(BF16), state.astype(BF16),
                            preferred_element_type=F32)
        kd_t = jnp.transpose(kc * kdec_ref[...]).astype(BF16)
        kv = jnp.dot(kd_t, vcb, preferred_element_type=F32)
        state_scr[...] = state * cdec_ref[...] + kv * bd_ref[...]
        yret_scr[rows, :] = o_c

    o_all = yret_scr[...]
    mean = jnp.dot(o_all, gmat, precision=HIGHEST, preferred_element_type=F32)
    xc = o_all - mean
    var = jnp.dot(xc * xc, gmat, precision=HIGHEST, preferred_element_type=F32)
    r_g = r_ref[:, 3 * w:4 * w].astype(F32)
    y_ret = (r_g * _sigmoid(r_g)) * (xc * lax.rsqrt(var + EPS) * retg_ref[...])

    d = x_ref.shape[1]
    ys = (y_conv, ymla_ref[...], ysg_scr[...], y_ret)
    merged = jnp.zeros((tm, d), F32)
    for n in range(N_BRANCH):
        gate = _sigmoid(gates_ref[:, n * d:(n + 1) * d].astype(F32))
        merged = merged + gate * jnp.dot(ys[n].astype(BF16), wb_ref[n], preferred_element_type=F32)
    out = jnp.dot(merged.astype(BF16), wo_ref[...], preferred_element_type=F32)
    o_ref[...] = x_ref[...] + mod_ref[2:3, :] * out


def _mixers(proj, ymla, x, cosr, sinr, mod, p, seq):
    n, d = x.shape
    tm = TM_MIX
    tpb = seq // tm

    def col(width, offset):
        return pl.BlockSpec((tm, width), lambda i: (i, offset // width))

    def full(a):
        return pl.BlockSpec(a.shape, lambda i: (0,) * a.ndim)

    consts = [p["conv_w"], p["gv_g"], p["ws_cat"], p["bs_mat"], p["ret_g"], p["dec"], p["kdec"],
              p["qdec"], p["cdec"], p["bd"], p["gmat"], p["mk"], p["mv"], p["w_branch"], p["w_o"]]
    return pl.pallas_call(
        functools.partial(_mix_kernel, tm=tm, tpb=tpb),
        out_shape=jax.ShapeDtypeStruct((n, d), F32),
        grid=(n // tm,),
        in_specs=[col(N_BRANCH * d, COL_GATES), col(4 * MIX_W, COL_A), col(4 * MIX_W, COL_R),
                  col(2 * MIX_W, COL_SU),
                  pl.BlockSpec((tm, MIX_W), lambda i: (i, 0)),
                  pl.BlockSpec((tm, d), lambda i: (i, 0)),
                  pl.BlockSpec((tm, LANES), lambda i: (i, 0)),
                  pl.BlockSpec((tm, LANES), lambda i: (i, 0)),
                  pl.BlockSpec((None, 6, d), lambda i: (i // tpb, 0, 0))]
                 + [full(c) for c in consts],
        out_specs=pl.BlockSpec((tm, d), lambda i: (i, 0)),
        scratch_shapes=[pltpu.VMEM((8, MIX_W), F32), pltpu.VMEM((MIX_W, MIX_W), F32),
                        pltpu.VMEM((tm, MIX_W), F32), pltpu.VMEM((tm, MIX_W), F32)],
        compiler_params=_cparams("arbitrary"),
        name="mixers_merge",
    )(proj, proj, proj, proj, ymla, x, cosr, sinr, mod, *consts)


def _ffn_kernel(x_ref, mod_ref, g_ref, w1_ref, w3_ref, w2_ref, o_ref, h_scr, acc_scr):
    j = pl.program_id(1)

    @pl.when(j == 0)
    def _():
        h = _norm_mod(x_ref[...], g_ref[...], mod_ref[3:4, :], mod_ref[4:5, :])
        h_scr[...] = h.astype(BF16)
        acc_scr[...] = jnp.zeros(acc_scr.shape, F32)

    h = h_scr[...]
    a = jnp.dot(h, w1_ref[...], preferred_element_type=F32)
    b = jnp.dot(h, w3_ref[...], preferred_element_type=F32)
    hid = ((a * _sigmoid(a)) * b).astype(BF16)
    acc_scr[...] += jnp.dot(hid, w2_ref[...], preferred_element_type=F32)

    @pl.when(j == pl.num_programs(1) - 1)
    def _():
        o_ref[...] = x_ref[...] + mod_ref[5:6, :] * acc_scr[...]


def _dense_ffn(x, mod, g, w1, w3, w2, seq):
    n, d = x.shape
    dff = w1.shape[1]
    tm, tf = TM_FFN, TF_FFN
    tpb = seq // tm
    return pl.pallas_call(
        _ffn_kernel,
        out_shape=jax.ShapeDtypeStruct((n, d), F32),
        grid=(n // tm, dff // tf),
        in_specs=[
            pl.BlockSpec((tm, d), lambda i, j: (i, 0)),
            pl.BlockSpec((None, 6, d), lambda i, j: (i // tpb, 0, 0)),
            pl.BlockSpec((1, d), lambda i, j: (0, 0)),
            pl.BlockSpec((d, tf), lambda i, j: (0, j)),
            pl.BlockSpec((d, tf), lambda i, j: (0, j)),
            pl.BlockSpec((tf, d), lambda i, j: (j, 0)),
        ],
        out_specs=pl.BlockSpec((tm, d), lambda i, j: (i, 0)),
        scratch_shapes=[pltpu.VMEM((tm, d), BF16), pltpu.VMEM((tm, d), F32)],
        compiler_params=_cparams("parallel", "arbitrary"),
        name="dense_swiglu",
    )(x, mod, g, w1, w3, w2)


def _router_kernel(x_ref, mod_ref, g_ref, rw_ref, rb_ref, hp_ref, ei_ref, pw_ref, cnt_ref, carry_scr, *, tm):
    i = pl.program_id(0)

    @pl.when(i == 0)
    def _():
        carry_scr[...] = jnp.zeros(carry_scr.shape, F32)

    h = _norm_mod(x_ref[...], g_ref[...], mod_ref[3:4, :], mod_ref[4:5, :])
    half = h.shape[1] // 2
    hp_ref[...] = _pack_bf16_pair(h[:, :half], h[:, half:])

    logits = jnp.dot(h, rw_ref[...], precision=HIGHEST, preferred_element_type=F32) + rb_ref[...]
    mx = jnp.max(logits, axis=-1, keepdims=True)
    ex = jnp.exp(logits - mx)
    probs = ex / jnp.sum(ex, axis=-1, keepdims=True)
    lane = lax.broadcasted_iota(jnp.int32, (tm, LANES), 1)
    valid = lane < N_EXPERTS
    probs = jnp.where(valid, probs, -1.0)
    m1 = jnp.max(probs, axis=-1, keepdims=True)
    i1 = jnp.min(jnp.where(probs == m1, lane, LANES), axis=-1, keepdims=True)
    rest = jnp.where(lane == i1, -1.0, probs)
    m2 = jnp.max(rest, axis=-1, keepdims=True)
    i2 = jnp.min(jnp.where(rest == m2, lane, LANES), axis=-1, keepdims=True)
    den = m1 + m2
    pw_ref[...] = jnp.where(lane == 0, m1 / den, jnp.where(lane == 1, m2 / den, 0.0))

    sel1 = lane == i1
    sel2 = lane == i2
    onehot = jnp.where(sel1, 1.0, 0.0) + jnp.where(sel2, 1.0, 0.0)
    r_i = lax.broadcasted_iota(jnp.int32, (tm, tm), 0)
    c_i = lax.broadcasted_iota(jnp.int32, (tm, tm), 1)
    tri = jnp.where(c_i < r_i, 1.0, 0.0).astype(BF16)
    before = jnp.dot(tri, onehot.astype(BF16), preferred_element_type=F32) + carry_scr[0:1, :]
    rank1 = jnp.sum(jnp.where(sel1, before, 0.0), axis=-1, keepdims=True)
    rank2 = jnp.sum(jnp.where(sel2, before, 0.0), axis=-1, keepdims=True)
    ei = jnp.where(lane == 0, i1, jnp.where(lane == 1, i2, 0))
    ei = jnp.where(lane == 2, rank1.astype(jnp.int32), jnp.where(lane == 3, rank2.astype(jnp.int32), ei))
    ei_ref[...] = ei
    total = carry_scr[0:1, :] + jnp.sum(onehot, axis=0, keepdims=True)
    carry_scr[0:1, :] = total
    cnt_ref[...] = jnp.broadcast_to(total, cnt_ref.shape)


def _router(x, mod, g, rw_pad, rb_pad, seq):
    n, d = x.shape
    tm = TM_ROUTE
    tpb = seq // tm
    return pl.pallas_call(
        functools.partial(_router_kernel, tm=tm),
        out_shape=(jax.ShapeDtypeStruct((n, d // 2), jnp.uint32),
                   jax.ShapeDtypeStruct((n, LANES), jnp.int32),
                   jax.ShapeDtypeStruct((n, LANES), F32),
                   jax.ShapeDtypeStruct((8, LANES), F32)),
        grid=(n // tm,),
        in_specs=[
            pl.BlockSpec((tm, d), lambda i: (i, 0)),
            pl.BlockSpec((None, 6, d), lambda i: (i // tpb, 0, 0)),
            pl.BlockSpec((1, d), lambda i: (0, 0)),
            pl.BlockSpec((d, LANES), lambda i: (0, 0)),
            pl.BlockSpec((1, LANES), lambda i: (0, 0)),
        ],
        out_specs=(pl.BlockSpec((tm, d // 2), lambda i: (i, 0)),
                   pl.BlockSpec((tm, LANES), lambda i: (i, 0)),
                   pl.BlockSpec((tm, LANES), lambda i: (i, 0)),
                   pl.BlockSpec((8, LANES), lambda i: (0, 0))),
        scratch_shapes=[pltpu.VMEM((8, LANES), F32)],
        compiler_params=_cparams("arbitrary"),
        name="router_top2",
    )(x, mod, g, rw_pad, rb_pad)


def _dispatch_kernel(pos_ref, hp_ref, xs_in_ref, xs_ref, sem, *, tm):
    del xs_in_ref
    base = pl.program_id(0) * (2 * tm)

    def issue(r, carry):
        for k in range(TOP_K):
            dst = pos_ref[base + 2 * r + k]
            pltpu.make_async_copy(hp_ref.at[pl.ds(r, 1)], xs_ref.at[pl.ds(dst, 1)], sem).start()
        return carry

    lax.fori_loop(0, tm, issue, 0)
    for k in range(TOP_K):
        pltpu.make_async_copy(hp_ref, xs_ref.at[pl.ds(0, tm)], sem).wait()


def _dispatch(pos_flat, hp, xs_init):
    n, half = hp.shape
    tm = TM_DISP
    return pl.pallas_call(
        functools.partial(_dispatch_kernel, tm=tm),
        out_shape=jax.ShapeDtypeStruct(xs_init.shape, xs_init.dtype),
        grid_spec=pltpu.PrefetchScalarGridSpec(
            num_scalar_prefetch=1,
            grid=(n // tm,),
            in_specs=[pl.BlockSpec((tm, half), lambda i, pos: (i, 0)),
                      pl.BlockSpec(memory_space=pl.ANY)],
            out_specs=pl.BlockSpec(memory_space=pl.ANY),
            scratch_shapes=[pltpu.SemaphoreType.DMA],
        ),
        input_output_aliases={2: 0},
        compiler_params=_cparams("arbitrary"),
        name="moe_dispatch",
    )(pos_flat, hp, xs_init)


def _expert_kernel(te_ref, used_ref, xs_ref, w1_ref, w3_ref, w2_ref, y_ref, h_scr, acc_scr):
    t = pl.program_id(0)
    j = pl.program_id(1)
    del te_ref

    @pl.when(used_ref[t] == 1)
    def _():
        @pl.when(j == 0)
        def _():
            lo, hi = _unpack_bf16_pair(xs_ref[...])
            h_scr[...] = jnp.concatenate([lo.astype(BF16), hi.astype(BF16)], axis=1)
            acc_scr[...] = jnp.zeros(acc_scr.shape, F32)

        h = h_scr[...]
        a = jnp.dot(h, w1_ref[...], preferred_element_type=F32)
        b = jnp.dot(h, w3_ref[...], preferred_element_type=F32)
        hid = ((a * _sigmoid(a)) * b).astype(BF16)
        acc_scr[...] += jnp.dot(hid, w2_ref[...], preferred_element_type=F32)

        @pl.when(j == pl.num_programs(1) - 1)
        def _():
            acc = acc_scr[...]
            half = acc.shape[1] // 2
            y_ref[...] = _pack_bf16_pair(acc[:, :half], acc[:, half:])

    @pl.when(jnp.logical_and(used_ref[t] == 0, j == 0))
    def _():
        y_ref[...] = jnp.zeros(y_ref.shape, y_ref.dtype)


def _expert_ffn(tile_e, tile_used, xs, w1, w3, w2):
    rows, half = xs.shape
    d = 2 * half
    dff = w1.shape[2]
    tg, tf = TG_MOE, TF_MOE
    return pl.pallas_call(
        _expert_kernel,
        out_shape=jax.ShapeDtypeStruct((rows, half), jnp.uint32),
        grid_spec=pltpu.PrefetchScalarGridSpec(
            num_scalar_prefetch=2,
            grid=(rows // tg, dff // tf),
            in_specs=[
                pl.BlockSpec((tg, half), lambda t, j, te, us: (t, 0)),
                pl.BlockSpec((None, d, tf), lambda t, j, te, us: (te[t], 0, j * us[t])),
                pl.BlockSpec((None, d, tf), lambda t, j, te, us: (te[t], 0, j * us[t])),
                pl.BlockSpec((None, tf, d), lambda t, j, te, us: (te[t], j * us[t], 0)),
            ],
            out_specs=pl.BlockSpec((tg, half), lambda t, j, te, us: (t, 0)),
            scratch_shapes=[pltpu.VMEM((tg, d), BF16), pltpu.VMEM((tg, d), F32)],
        ),
        compiler_params=_cparams("arbitrary", "arbitrary"),
        name="expert_swiglu",
    )(tile_e, tile_used, xs, w1, w3, w2)


def _combine_kernel(pos_ref, x_ref, pw_ref, mod_ref, y_ref, o_ref, buf0, buf1, sem, *, tm):
    base = pl.program_id(0) * (2 * tm)
    bufs = (buf0, buf1)

    def issue(r, carry):
        for k in range(TOP_K):
            src = pos_ref[base + 2 * r + k]
            pltpu.make_async_copy(y_ref.at[pl.ds(src, 1)], bufs[k].at[pl.ds(r, 1)], sem).start()
        return carry

    lax.fori_loop(0, tm, issue, 0)
    for k in range(TOP_K):
        pltpu.make_async_copy(y_ref.at[pl.ds(0, tm)], bufs[k], sem).wait()

    p0 = pw_ref[:, 0:1]
    p1 = pw_ref[:, 1:2]
    lo0, hi0 = _unpack_bf16_pair(buf0[...])
    lo1, hi1 = _unpack_bf16_pair(buf1[...])
    half = lo0.shape[1]
    g2 = mod_ref[5:6, :]
    o_ref[:, :half] = x_ref[:, :half] + g2[:, :half] * (lo0 * p0 + lo1 * p1)
    o_ref[:, half:] = x_ref[:, half:] + g2[:, half:] * (hi0 * p0 + hi1 * p1)


def _combine(pos_flat, x, pw, mod, y, seq):
    n, d = x.shape
    tm = TM_COMB
    tpb = seq // tm
    return pl.pallas_call(
        functools.partial(_combine_kernel, tm=tm),
        out_shape=jax.ShapeDtypeStruct((n, d), F32),
        grid_spec=pltpu.PrefetchScalarGridSpec(
            num_scalar_prefetch=1,
            grid=(n // tm,),
            in_specs=[pl.BlockSpec((tm, d), lambda i, pos: (i, 0)),
                      pl.BlockSpec((tm, LANES), lambda i, pos: (i, 0)),
                      pl.BlockSpec((None, 6, d), lambda i, pos: (i // tpb, 0, 0)),
                      pl.BlockSpec(memory_space=pl.ANY)],
            out_specs=pl.BlockSpec((tm, d), lambda i, pos: (i, 0)),
            scratch_shapes=[pltpu.VMEM((tm, d // 2), jnp.uint32), pltpu.VMEM((tm, d // 2), jnp.uint32),
                            pltpu.SemaphoreType.DMA],
        ),
        compiler_params=_cparams("arbitrary"),
        name="moe_combine",
    )(pos_flat, x, pw, mod, y)


def _pack_w_in(w_in):
    d = w_in.shape[0]
    w = MIX_W
    o_ckv = 3 * w + Q_LORA
    o_kr = o_ckv + KV_LORA
    o_su = o_kr + QK_ROPE
    o_rq = o_su + 2 * w
    o_gate = o_rq + 4 * w
    half = HEAD_DIM // 2
    perm = np.array([h * HEAD_DIM + part * half + i
                     for part in range(2) for h in range(N_GROUPS) for i in range(half)])
    kr = w_in[:, o_kr:o_kr + QK_ROPE]
    hr = QK_ROPE // 2
    z = lambda k: jnp.zeros((d, k), w_in.dtype)
    cols = [
        w_in[:, o_gate:o_gate + N_BRANCH * d],
        w_in[:, 0:3 * w + Q_LORA],
        w_in[:, o_rq:o_rq + w][:, perm], w_in[:, o_rq + w:o_rq + 2 * w][:, perm],
        w_in[:, o_rq + 2 * w:o_rq + 4 * w],
        w_in[:, o_su:o_su + 2 * w],
        w_in[:, o_ckv:o_ckv + KV_LORA],
        z(QK_NOPE), kr, z(LANES - QK_HEAD),
        z(QK_NOPE), kr[:, hr:], kr[:, :hr], z(LANES - QK_HEAD),
        z(N_IN - COL_KRB - LANES),
    ]
    return jnp.concatenate(cols, axis=1).astype(BF16)


def _swap_rope_halves(a):
    hr = QK_ROPE // 2
    return jnp.concatenate([a[..., :QK_NOPE], a[..., QK_NOPE + hr:QK_HEAD], a[..., QK_NOPE:QK_NOPE + hr],
                            a[..., QK_HEAD:]], axis=-1)


def _mla_params(cq_g, w_uq, ckv_g, w_ukv, qn_g, kn_g):
    pad = LANES - QK_HEAD
    wq = w_uq.reshape(Q_LORA, N_GROUPS, QK_HEAD)
    wq = jnp.pad(wq, ((0, 0), (0, 0), (0, pad)))
    wkv = w_ukv.reshape(KV_LORA, N_GROUPS, QK_NOPE + V_HEAD)
    wk = jnp.pad(wkv[:, :, :QK_NOPE], ((0, 0), (0, 0), (0, LANES - QK_NOPE)))
    qg = jnp.pad(qn_g, (0, pad))[None, :]
    kg = jnp.pad(kn_g, (0, pad))[None, :]
    return {
        "cq_g": cq_g[None, :], "ckv_g": ckv_g[None, :],
        "wqa": wq.reshape(Q_LORA, -1).astype(BF16),
        "wqb": _swap_rope_halves(wq).reshape(Q_LORA, -1).astype(BF16),
        "wk": wk.reshape(KV_LORA, -1).astype(BF16),
        "wv": wkv[:, :, QK_NOPE:].reshape(KV_LORA, -1).astype(BF16),
        "qga": qg, "qgb": _swap_rope_halves(qg), "kga": kg, "kgb": _swap_rope_halves(kg),
    }


def _mixer_consts():
    h = jnp.arange(N_GROUPS, dtype=F32)
    log_gamma = jnp.log1p(-(2.0 ** (-5.0 - h)))
    pos = jnp.arange(CHUNK, dtype=F32)
    rel = pos[:, None] - pos[None, :]
    dec = jnp.where(rel >= 0, jnp.exp(log_gamma[:, None, None] * jnp.maximum(rel, 0.0)), 0.0)
    lane = np.arange(MIX_W)
    head_k = (lane % LANES) // (HEAD_DIM // 2)
    head_v = lane // HEAD_DIM
    lg_k = log_gamma[head_k]
    return {
        "dec": dec.reshape(N_GROUPS * CHUNK, CHUNK),
        "kdec": jnp.exp(lg_k[None, :] * (CHUNK - 1.0 - pos)[:, None]),
        "qdec": jnp.exp(lg_k[None, :] * (pos + 1.0)[:, None]),
        "cdec": jnp.broadcast_to(jnp.exp(lg_k * CHUNK)[:, None], (MIX_W, MIX_W)),
        "bd": jnp.asarray((head_k[:, None] == head_v[None, :]).astype(np.float32)),
        "gmat": jnp.asarray((head_v[:, None] == head_v[None, :]).astype(np.float32) / HEAD_DIM),
        "mk": jnp.asarray((head_k[None, :] == np.arange(N_GROUPS)[:, None]).astype(np.float32)),
        "mv": jnp.asarray((head_v[None, :] == np.arange(N_GROUPS)[:, None]).astype(np.float32)),
    }


def _mixer_params(conv_w, gv_g, w_s, b_s, ret_g, w_branch, w_o):
    p = dict(_mixer_consts())
    ws = jnp.tril(w_s)
    p.update({
        "conv_w": conv_w,
        "gv_g": gv_g.reshape(1, MIX_W),
        "ws_cat": jnp.transpose(ws, (1, 0, 2)).reshape(CHUNK, N_GROUPS * CHUNK).astype(BF16),
        "bs_mat": jnp.repeat(b_s.T, HEAD_DIM, axis=1),
        "ret_g": ret_g.reshape(1, MIX_W),
        "w_branch": w_branch.astype(BF16),
        "w_o": w_o.astype(BF16),
    })
    return p


def _moe_layout(ei, cnt, n_tok, n_tiles):
    counts = cnt[0, :N_EXPERTS].astype(jnp.int32)
    padded = ((counts + TG_MOE - 1) // TG_MOE) * TG_MOE
    ends = jnp.cumsum(padded)
    starts = ends - padded
    e01 = ei[:, 0:TOP_K]
    rank = ei[:, TOP_K:2 * TOP_K]
    onehot = (e01[:, :, None] == jnp.arange(N_EXPERTS)[None, None, :]).astype(jnp.int32)
    pos = jnp.sum(onehot * starts[None, None, :], axis=-1) + rank
    tile_start = jnp.arange(n_tiles, dtype=jnp.int32) * TG_MOE
    tile_e = jnp.sum((tile_start[:, None] >= ends[None, :]).astype(jnp.int32), axis=1)
    used = (tile_start < ends[-1]).astype(jnp.int32)
    last_e = jnp.sum((ends[-1] - 1 >= ends).astype(jnp.int32))
    tile_e = jnp.where(used == 1, tile_e, last_e)
    return pos.reshape(n_tok * TOP_K), jnp.minimum(tile_e, N_EXPERTS - 1), used


def kernel(x, c, positions, norm1_g, norm2_g, ada_w, ada_b, w_in, conv_w, cq_g, w_uq, ckv_g, w_ukv, qn_g, kn_g, gv_g, w_s, b_s, ret_g, w_branch, w_o, ffn_w1, ffn_w3, ffn_w2, router_w, router_b, moe_w1, moe_w3, moe_w2):
    batch, seq, d = x.shape
    depth = ada_w.shape[0]
    n = batch * seq
    assert seq % max(TM_PROJ, TM_PREP, TQ_ATT, TM_MIX, TM_FFN, TM_ROUTE, TM_COMB, TM_DISP) == 0
    assert d // 2 % LANES == 0

    c_pad = jnp.pad(c, ((0, 8 - batch), (0, 0)))
    ada = _ada(c_pad, ada_w, ada_b)[:, :batch].reshape(depth, batch, 6, d)
    cosr, sinr, cm, sm = _rope_tables(positions.astype(F32).reshape(n, 1))

    xt = x.reshape(n, d)
    for l in range(depth):
        mod = ada[l]
        proj = _inproj(xt, mod, norm1_g[l][None, :], _pack_w_in(w_in[l]), seq)
        q, k, v = _mla_prep(proj, cm, sm, _mla_params(cq_g[l], w_uq[l], ckv_g[l], w_ukv[l], qn_g[l], kn_g[l]))
        y_mla = _flash(q, k, v, batch, seq)
        mp = _mixer_params(conv_w[l], gv_g[l], w_s[l], b_s[l], ret_g[l], w_branch[l], w_o[l])
        xt = _mixers(proj, y_mla, xt, cosr, sinr, mod, mp, seq)
        g2n = norm2_g[l][None, :]
        if l % 2 == 0:
            i = l // 2
            xt = _dense_ffn(xt, mod, g2n, ffn_w1[i].astype(BF16), ffn_w3[i].astype(BF16),
                            ffn_w2[i].astype(BF16), seq)
        else:
            i = l // 2
            rw_pad = jnp.pad(router_w[i], ((0, 0), (0, LANES - N_EXPERTS)))
            rb_pad = jnp.pad(router_b[i], (0, LANES - N_EXPERTS), constant_values=-1e30)[None, :]
            hp, ei, pw, cnt = _router(xt, mod, g2n, rw_pad, rb_pad, seq)
            n_tiles = (n * TOP_K) // TG_MOE + N_EXPERTS
            pos_flat, tile_e, used = _moe_layout(ei, cnt, n, n_tiles)
            xs = _dispatch(pos_flat, hp, jnp.zeros((n_tiles * TG_MOE, d // 2), jnp.uint32))
            y = _expert_ffn(tile_e, used, xs, moe_w1[i].astype(BF16), moe_w3[i].astype(BF16),
                            moe_w2[i].astype(BF16))
            xt = _combine(pos_flat, xt, pw, mod, y, seq)
    return xt.reshape(batch, seq, d)
```

```python
import functools

import jax
import jax.numpy as jnp
import numpy as np
from jax import lax
from jax.experimental import pallas as pl
from jax.experimental.pallas import tpu as pltpu

F32 = jnp.float32
BF16 = jnp.bfloat16
HIGHEST = lax.Precision.HIGHEST

HEAD_DIM = 64
N_GROUPS = 4
MIX_W = N_GROUPS * HEAD_DIM
N_BRANCH = 4
CONV_W = 3
Q_LORA = 256
KV_LORA = 128
QK_NOPE = 64
QK_ROPE = 32
QK_HEAD = QK_NOPE + QK_ROPE
V_HEAD = 64
CHUNK = 128
N_EXPERTS = 8
TOP_K = 2
ROPE_THETA = 10000.0
EPS = 1e-6
LOG2_E = 1.4426950408889634
MAX_STATIC_SHIFT = 50.0

LANES = 128
VMEM_LIMIT_BYTES = 56 * 1024 * 1024

COL_GATES = 0
COL_A = 4096
COL_CQ = COL_A + 3 * MIX_W
COL_R = 5120
COL_SU = 6144
COL_CKV = 6656
COL_KRA = 6784
COL_KRB = 6912
N_IN = 7168

TM_PROJ = 512
TN_PROJ = 1024
TM_PREP = 1024
TQ_ATT = 1024
TM_MIX = 512
TM_FFN = 512
TM_ROUTE = 1024
TM_DISP = 512
TG_MOE = 1024
TF_MOE = 896
TM_COMB = 512


def _cparams(*sem):
    return pltpu.CompilerParams(dimension_semantics=sem, vmem_limit_bytes=VMEM_LIMIT_BYTES)


def _sigmoid(x):
    return jax.nn.sigmoid(x)


def _pack_bf16_pair(lo, hi):
    lo_bits = lax.bitcast_convert_type(lo.astype(BF16).astype(F32), jnp.uint32)
    hi_bits = lax.bitcast_convert_type(hi.astype(BF16).astype(F32), jnp.uint32)
    return (lo_bits >> 16) | (hi_bits & jnp.uint32(0xFFFF0000))


def _unpack_bf16_pair(p):
    lo = lax.bitcast_convert_type(p << 16, F32)
    hi = lax.bitcast_convert_type(p & jnp.uint32(0xFFFF0000), F32)
    return lo, hi


def _norm_mod(x, g, shift, scale):
    y = x * lax.rsqrt(jnp.mean(x * x, axis=-1, keepdims=True) + EPS)
    return (y * g) * (1.0 + scale) + shift


def _ada_kernel(c_ref, w_ref, b_ref, o_ref):
    c = c_ref[...]
    cond = c * _sigmoid(c)
    o_ref[...] = jnp.dot(cond, w_ref[...], precision=HIGHEST, preferred_element_type=F32) + b_ref[...]


def _ada(c_pad, ada_w, ada_b):
    n_layer, d, d6 = ada_w.shape
    rows = c_pad.shape[0]
    tn = 1024
    return pl.pallas_call(
        _ada_kernel,
        out_shape=jax.ShapeDtypeStruct((n_layer, rows, d6), F32),
        grid=(n_layer, d6 // tn),
        in_specs=[
            pl.BlockSpec((rows, d), lambda l, j: (0, 0)),
            pl.BlockSpec((None, d, tn), lambda l, j: (l, 0, j)),
            pl.BlockSpec((None, 1, tn), lambda l, j: (l, 0, j)),
        ],
        out_specs=pl.BlockSpec((None, rows, tn), lambda l, j: (l, 0, j)),
        compiler_params=_cparams("parallel", "parallel"),
        name="ada_mod",
    )(c_pad, ada_w, ada_b.reshape(n_layer, 1, d6))


def _rope_kernel(pos_ref, invr_ref, invm_ref, signm_ref, cr_ref, sr_ref, cm_ref, sm_ref):
    pos = pos_ref[...]
    ang_r = pos * invr_ref[...]
    cr_ref[...] = jnp.cos(ang_r)
    sr_ref[...] = jnp.sin(ang_r)
    ang_m = pos * invm_ref[...]
    cm_ref[...] = jnp.cos(ang_m)
    sm_ref[...] = jnp.sin(ang_m) * signm_ref[...]


def _rope_tables(pos_f):
    n = pos_f.shape[0]
    tm = 1024
    half_r = HEAD_DIM // 2
    inv_r = ROPE_THETA ** (-jnp.arange(half_r, dtype=F32) / half_r)
    inv_r = jnp.tile(inv_r, LANES // half_r)[None, :]
    half_m = QK_ROPE // 2
    inv_m1 = ROPE_THETA ** (-jnp.arange(half_m, dtype=F32) / half_m)
    zeros = lambda k: jnp.zeros((k,), F32)
    inv_m = jnp.concatenate([zeros(QK_NOPE), inv_m1, inv_m1, zeros(LANES - QK_HEAD)])[None, :]
    sign_m = jnp.concatenate([zeros(QK_NOPE), -jnp.ones((half_m,), F32), jnp.ones((half_m,), F32),
                              zeros(LANES - QK_HEAD)])[None, :]
    row = pl.BlockSpec((1, LANES), lambda i: (0, 0))
    tab = pl.BlockSpec((tm, LANES), lambda i: (i, 0))
    shape = jax.ShapeDtypeStruct((n, LANES), F32)
    return pl.pallas_call(
        _rope_kernel,
        out_shape=(shape, shape, shape, shape),
        grid=(n // tm,),
        in_specs=[pl.BlockSpec((tm, 1), lambda i: (i, 0)), row, row, row],
        out_specs=(tab, tab, tab, tab),
        compiler_params=_cparams("parallel"),
        name="rope_tables",
    )(pos_f, inv_r, inv_m, sign_m)


def _inproj_kernel(x_ref, mod_ref, g_ref, w_ref, o_ref):
    h = _norm_mod(x_ref[...], g_ref[...], mod_ref[0:1, :], mod_ref[1:2, :]).astype(BF16)
    for c in range(N_IN // TN_PROJ):
        cols = slice(c * TN_PROJ, (c + 1) * TN_PROJ)
        o_ref[:, cols] = jnp.dot(h, w_ref[:, cols], preferred_element_type=F32).astype(BF16)


def _resident(shape):
    return pl.BlockSpec(shape, lambda *_: (0,) * len(shape), pipeline_mode=pl.Buffered(1))


def _inproj(x, mod, g, w, seq):
    n, d = x.shape
    tm = TM_PROJ
    tpb = seq // tm
    return pl.pallas_call(
        _inproj_kernel,
        out_shape=jax.ShapeDtypeStruct((n, N_IN), BF16),
        grid=(n // tm,),
        in_specs=[
            pl.BlockSpec((tm, d), lambda i: (i, 0)),
            pl.BlockSpec((None, 6, d), lambda i: (i // tpb, 0, 0)),
            pl.BlockSpec((1, d), lambda i: (0, 0)),
            _resident((d, N_IN)),
        ],
        out_specs=pl.BlockSpec((tm, N_IN), lambda i: (i, 0)),
        compiler_params=_cparams("parallel"),
        name="in_proj",
    )(x, mod, g, w)


def _mla_prep_kernel(cq_ref, ckv_ref, kra_ref, krb_ref, cm_ref, sm_ref, cqg_ref, wqa_ref, wqb_ref,
                     ckvg_ref, wk_ref, wv_ref, qga_ref, qgb_ref, kga_ref, kgb_ref,
                     qaug_ref, kaug_ref, vaug_ref, q_ref, k_ref, v_ref):
    cq = cq_ref[...].astype(F32)
    cqn = (cq * lax.rsqrt(jnp.mean(cq * cq, axis=-1, keepdims=True) + EPS) * cqg_ref[...]).astype(BF16)
    qa = jnp.dot(cqn, wqa_ref[...], preferred_element_type=F32)
    qb = jnp.dot(cqn, wqb_ref[...], preferred_element_type=F32)
    ckv = ckv_ref[...].astype(F32)
    ckvn = (ckv * lax.rsqrt(jnp.mean(ckv * ckv, axis=-1, keepdims=True) + EPS) * ckvg_ref[...]).astype(BF16)
    ka = jnp.dot(ckvn, wk_ref[...], preferred_element_type=F32)
    v_ref[...] = (jnp.dot(ckvn, wv_ref[...], preferred_element_type=F32) + vaug_ref[...]).astype(BF16)
    kra = kra_ref[...].astype(F32)
    krb = krb_ref[...].astype(F32)
    cm = cm_ref[...]
    sm = sm_ref[...]
    scale = QK_HEAD ** -0.5 * LOG2_E
    for h in range(N_GROUPS):
        sl = slice(h * LANES, (h + 1) * LANES)
        qah, qbh = qa[:, sl], qb[:, sl]
        r = lax.rsqrt(jnp.sum(qah * qah, axis=-1, keepdims=True) * (1.0 / QK_HEAD) + EPS)
        q_rot = (qah * r) * qga_ref[...] * cm + (qbh * r) * qgb_ref[...] * sm
        q_ref[:, sl] = (q_rot * scale + qaug_ref[...]).astype(BF16)
        kah = ka[:, sl] + kra
        kbh = ka[:, sl] + krb
        r = lax.rsqrt(jnp.sum(kah * kah, axis=-1, keepdims=True) * (1.0 / QK_HEAD) + EPS)
        k_rot = (kah * r) * kga_ref[...] * cm + (kbh * r) * kgb_ref[...] * sm
        k_ref[:, sl] = (k_rot + kaug_ref[...]).astype(BF16)


def _mla_prep(proj, cm, sm, p):
    n = proj.shape[0]
    tm = TM_PREP
    hw = N_GROUPS * LANES

    def col(width, offset):
        return pl.BlockSpec((tm, width), lambda i: (i, offset // width))

    def full(a):
        return pl.BlockSpec(a.shape, lambda i: (0,) * a.ndim)

    weights = [p["cq_g"], p["wqa"], p["wqb"], p["ckv_g"], p["wk"], p["wv"],
               p["qga"], p["qgb"], p["kga"], p["kgb"], p["qaug"], p["kaug"], p["vaug"]]
    head_tile = pl.BlockSpec((tm, hw), lambda i: (i, 0))
    out = jax.ShapeDtypeStruct((n, hw), BF16)
    return pl.pallas_call(
        _mla_prep_kernel,
        out_shape=(out, out, out),
        grid=(n // tm,),
        in_specs=[col(Q_LORA, COL_CQ), col(KV_LORA, COL_CKV), col(LANES, COL_KRA), col(LANES, COL_KRB),
                  pl.BlockSpec((tm, LANES), lambda i: (i, 0)), pl.BlockSpec((tm, LANES), lambda i: (i, 0))]
                 + [full(w) for w in weights],
        out_specs=(head_tile, head_tile, head_tile),
        compiler_params=_cparams("parallel"),
        name="mla_prep",
    )(proj, proj, proj, proj, cm, sm, *weights)


def _flash_kernel(q_ref, k_ref, v_ref, o_ref, acc_scr, *rest, tq, online_max):
    i = pl.program_id(1)
    j = pl.program_id(2)

    @pl.when(j == 0)
    def _():
        acc_scr[...] = jnp.zeros(acc_scr.shape, F32)
        if online_max:
            rest[0][...] = jnp.full(rest[0].shape, -jnp.inf, F32)

    def step(masked):
        if masked:
            row = lax.broadcasted_iota(jnp.int32, (tq, tq), 0)
            col = lax.broadcasted_iota(jnp.int32, (tq, tq), 1)
            keep = col <= row
        for h in range(N_GROUPS):
            sl = slice(h * LANES, (h + 1) * LANES)
            s = lax.dot_general(q_ref[:, sl], k_ref[:, sl], (((1,), (1,)), ((), ())),
                                preferred_element_type=F32)
            if masked:
                s = jnp.where(keep, s, -jnp.inf)
            if online_max:
                m_scr = rest[0]
                m_prev = m_scr[h]
                m_new = jnp.maximum(m_prev, jnp.max(s, axis=-1, keepdims=True))
                p = jnp.exp2(s - m_new).astype(BF16)
                acc_scr[h] = jnp.exp2(m_prev - m_new) * acc_scr[h] + jnp.dot(
                    p, v_ref[:, sl], preferred_element_type=F32)
                m_scr[h] = m_new
            else:
                acc_scr[h] += jnp.dot(jnp.exp2(s).astype(BF16), v_ref[:, sl], preferred_element_type=F32)

    @pl.when(j < i)
    def _():
        step(False)

    @pl.when(j == i)
    def _():
        step(True)
        lane = lax.broadcasted_iota(jnp.int32, (tq, LANES), 1)
        for pr in range(N_GROUPS // 2):
            lo = acc_scr[2 * pr]
            hi = acc_scr[2 * pr + 1]
            lo = lo / lo[:, V_HEAD:V_HEAD + 1]
            hi = hi / hi[:, V_HEAD:V_HEAD + 1]
            both = jnp.where(lane < V_HEAD, lo, pltpu.roll(hi, V_HEAD, axis=1))
            o_ref[:, pr * LANES:(pr + 1) * LANES] = both.astype(BF16)


def _flash(q, k, v, batch, seq, online_max):
    n = q.shape[0]
    tq = TQ_ATT
    nq = seq // tq
    hw = N_GROUPS * LANES
    scratch = [pltpu.VMEM((N_GROUPS, tq, LANES), F32)]
    if online_max:
        scratch.append(pltpu.VMEM((N_GROUPS, tq, 1), F32))
    return pl.pallas_call(
        functools.partial(_flash_kernel, tq=tq, online_max=online_max),
        out_shape=jax.ShapeDtypeStruct((n, MIX_W), BF16),
        grid=(batch, nq, nq),
        in_specs=[
            pl.BlockSpec((tq, hw), lambda b, i, j: (b * nq + i, 0)),
            pl.BlockSpec((tq, hw), lambda b, i, j: (b * nq + jnp.minimum(j, i), 0)),
            pl.BlockSpec((tq, hw), lambda b, i, j: (b * nq + jnp.minimum(j, i), 0)),
        ],
        out_specs=pl.BlockSpec((tq, MIX_W), lambda b, i, j: (b * nq + i, 0)),
        scratch_shapes=scratch,
        compiler_params=_cparams("parallel", "parallel", "arbitrary"),
        name="mla_flash_online" if online_max else "mla_flash",
    )(q, k, v)


def _gelu_tanh(x):
    return jax.nn.gelu(x, approximate=True)


def _mix_kernel(gates_ref, a_ref, r_ref, su_ref, ymla_ref, x_ref, cos_ref, sin_ref, mod_ref,
                convw_ref, gvg_ref, wscat_ref, bsmat_ref, retg_ref, dec_ref, kdec_ref, qdec_ref,
                cdec_ref, bd_ref, gmat_ref, mk_ref, mv_ref, wb_ref, wo_ref,
                o_ref, carry_scr, state_scr, ysg_scr, yret_scr, *, tm, tpb):
    i = pl.program_id(0)

    @pl.when(i % tpb == 0)
    def _():
        carry_scr[...] = jnp.zeros(carry_scr.shape, F32)
        state_scr[...] = jnp.zeros(state_scr.shape, F32)

    w = MIX_W
    a_b = a_ref[:, 0:w].astype(F32)
    u = a_ref[:, w:2 * w].astype(F32) * a_ref[:, 2 * w:3 * w].astype(F32)
    rowi = lax.broadcasted_iota(jnp.int32, (tm, w), 0)
    prev1 = carry_scr[0:1, :]
    prev2 = carry_scr[1:2, :]
    u1 = jnp.where(rowi == 0, prev1, pltpu.roll(u, 1, axis=0))
    u2 = jnp.where(rowi == 0, prev2, jnp.where(rowi == 1, prev1, pltpu.roll(u, 2, axis=0)))
    carry_scr[0:1, :] = u[tm - 1:tm, :]
    carry_scr[1:2, :] = u[tm - 2:tm - 1, :]
    y_conv = a_b * (convw_ref[0:1, :] * u2 + convw_ref[1:2, :] * u1 + convw_ref[2:3, :] * u)

    gmat = gmat_ref[...]
    s_u = _gelu_tanh(su_ref[:, 0:w].astype(F32))
    s_v = _gelu_tanh(su_ref[:, w:2 * w].astype(F32))
    ms = jnp.dot(s_v * s_v, gmat, precision=HIGHEST, preferred_element_type=F32)
    vn = (s_v * lax.rsqrt(ms + EPS) * gvg_ref[...]).astype(BF16)

    cosr = cos_ref[...]
    sinr = sin_ref[...]

    def rot(t):
        t1, t2 = t[:, 0:LANES], t[:, LANES:2 * LANES]
        return jnp.concatenate([t1 * cosr - t2 * sinr, t2 * cosr + t1 * sinr], axis=-1)

    rq = rot(r_ref[:, 0:w].astype(F32))
    rk = rot(r_ref[:, w:2 * w].astype(F32)) * (HEAD_DIM ** -0.5)

    for c in range(tm // CHUNK):
        rows = slice(c * CHUNK, (c + 1) * CHUNK)
        vc = vn[rows, :]
        vbd = jnp.concatenate([vc * mv_ref[g:g + 1, :].astype(BF16) for g in range(N_GROUPS)], axis=0)
        mixed = jnp.dot(wscat_ref[...], vbd, preferred_element_type=F32) + bsmat_ref[...]
        ysg_scr[rows, :] = s_u[rows, :] * mixed

        qc = rq[rows, :]
        kc = rk[rows, :]
        kcb = kc.astype(BF16)
        vcb = r_ref[rows, 2 * w:3 * w]
        qstack = jnp.concatenate([(qc * mk_ref[h:h + 1, :]).astype(BF16) for h in range(N_GROUPS)], axis=0)
        sc = lax.dot_general(qstack, kcb, (((1,), (1,)), ((), ())), preferred_element_type=F32)
        sc = (sc * dec_ref[...]).astype(BF16)
        scat = jnp.concatenate([sc[h * CHUNK:(h + 1) * CHUNK, :] for h in range(N_GROUPS)], axis=1)
        vstack = jnp.concatenate([vcb * mv_ref[h:h + 1, :].astype(BF16) for h in range(N_GROUPS)], axis=0)
        o_c = jnp.dot(scat, vstack, preferred_element_type=F32)
        state = state_scr[...]
        o_c = o_c + jnp.dot((qc * qdec_ref[...]).astype(BF16), state.astype(BF16),
                            preferred_element_type=F32)
        kd_t = jnp.transpose(kc * kdec_ref[...]).astype(BF16)
        kv = jnp.dot(kd_t, vcb, preferred_element_type=F32)
        state_scr[...] = state * cdec_ref[...] + kv * bd_ref[...]
        yret_scr[rows, :] = o_c

    o_all = yret_scr[...]
    mean = jnp.dot(o_all, gmat, precision=HIGHEST, preferred_element_type=F32)
    xc = o_all - mean
    var = jnp.dot(xc * xc, gmat, precision=HIGHEST, preferred_element_type=F32)
    r_g = r_ref[:, 3 * w:4 * w].astype(F32)
    y_ret = (r_g * _sigmoid(r_g)) * (xc * lax.rsqrt(var + EPS) * retg_ref[...])

    d = x_ref.shape[1]
    ys = (y_conv, ymla_ref[...], ysg_scr[...], y_ret)
    merged = jnp.zeros((tm, d), F32)
    for n in range(N_BRANCH):
        gate = _sigmoid(gates_ref[:, n * d:(n + 1) * d].astype(F32))
        merged = merged + gate * jnp.dot(ys[n].astype(BF16), wb_ref[n], preferred_element_type=F32)
    out = jnp.dot(merged.astype(BF16), wo_ref[...], preferred_element_type=F32)
    o_ref[...] = x_ref[...] + mod_ref[2:3, :] * out


def _mixers(proj, ymla, x, cosr, sinr, mod, p, seq):
    n, d = x.shape
    tm = TM_MIX
    tpb = seq // tm

    def col(width, offset):
        return pl.BlockSpec((tm, width), lambda i: (i, offset // width))

    def full(a):
        return pl.BlockSpec(a.shape, lambda i: (0,) * a.ndim)

    consts = [p["conv_w"], p["gv_g"], p["ws_cat"], p["bs_mat"], p["ret_g"], p["dec"], p["kdec"],
              p["qdec"], p["cdec"], p["bd"], p["gmat"], p["mk"], p["mv"], p["w_branch"], p["w_o"]]
    return pl.pallas_call(
        functools.partial(_mix_kernel, tm=tm, tpb=tpb),
        out_shape=jax.ShapeDtypeStruct((n, d), F32),
        grid=(n // tm,),
        in_specs=[col(N_BRANCH * d, COL_GATES), col(4 * MIX_W, COL_A), col(4 * MIX_W, COL_R),
                  col(2 * MIX_W, COL_SU),
                  pl.BlockSpec((tm, MIX_W), lambda i: (i, 0)),
                  pl.BlockSpec((tm, d), lambda i: (i, 0)),
                  pl.BlockSpec((tm, LANES), lambda i: (i, 0)),
                  pl.BlockSpec((tm, LANES), lambda i: (i, 0)),
                  pl.BlockSpec((None, 6, d), lambda i: (i // tpb, 0, 0))]
                 + [full(c) for c in consts],
        out_specs=pl.BlockSpec((tm, d), lambda i: (i, 0)),
        scratch_shapes=[pltpu.VMEM((8, MIX_W), F32), pltpu.VMEM((MIX_W, MIX_W), F32),
                        pltpu.VMEM((tm, MIX_W), F32), pltpu.VMEM((tm, MIX_W), F32)],
        compiler_params=_cparams("arbitrary"),
        name="mixers_merge",
    )(proj, proj, proj, proj, ymla, x, cosr, sinr, mod, *consts)


def _ffn_kernel(x_ref, mod_ref, g_ref, w1_ref, w3_ref, w2_ref, o_ref):
    x = x_ref[...]
    h = _norm_mod(x, g_ref[...], mod_ref[3:4, :], mod_ref[4:5, :]).astype(BF16)
    a = jnp.dot(h, w1_ref[...], preferred_element_type=F32)
    b = jnp.dot(h, w3_ref[...], preferred_element_type=F32)
    hid = ((a * _sigmoid(a)) * b).astype(BF16)
    o_ref[...] = x + mod_ref[5:6, :] * jnp.dot(hid, w2_ref[...], preferred_element_type=F32)


def _dense_ffn(x, mod, g, w1, w3, w2, seq):
    n, d = x.shape
    dff = w1.shape[1]
    tm = TM_FFN
    tpb = seq // tm
    return pl.pallas_call(
        _ffn_kernel,
        out_shape=jax.ShapeDtypeStruct((n, d), F32),
        grid=(n // tm,),
        in_specs=[
            pl.BlockSpec((tm, d), lambda i: (i, 0)),
            pl.BlockSpec((None, 6, d), lambda i: (i // tpb, 0, 0)),
            pl.BlockSpec((1, d), lambda i: (0, 0)),
            _resident((d, dff)), _resident((d, dff)), _resident((dff, d)),
        ],
        out_specs=pl.BlockSpec((tm, d), lambda i: (i, 0)),
        compiler_params=_cparams("parallel"),
        name="dense_swiglu",
    )(x, mod, g, w1, w3, w2)


def _router_kernel(x_ref, mod_ref, g_ref, rw_ref, rb_ref, hp_ref, ei_ref, pw_ref, cnt_ref, carry_scr, *, tm):
    i = pl.program_id(0)

    @pl.when(i == 0)
    def _():
        carry_scr[...] = jnp.zeros(carry_scr.shape, F32)

    h = _norm_mod(x_ref[...], g_ref[...], mod_ref[3:4, :], mod_ref[4:5, :])
    half = h.shape[1] // 2
    hp_ref[...] = _pack_bf16_pair(h[:, :half], h[:, half:])

    logits = jnp.dot(h, rw_ref[...], precision=HIGHEST, preferred_element_type=F32) + rb_ref[...]
    mx = jnp.max(logits, axis=-1, keepdims=True)
    ex = jnp.exp(logits - mx)
    probs = ex / jnp.sum(ex, axis=-1, keepdims=True)
    lane = lax.broadcasted_iota(jnp.int32, (tm, LANES), 1)
    valid = lane < N_EXPERTS
    probs = jnp.where(valid, probs, -1.0)
    m1 = jnp.max(probs, axis=-1, keepdims=True)
    i1 = jnp.min(jnp.where(probs == m1, lane, LANES), axis=-1, keepdims=True)
    rest = jnp.where(lane == i1, -1.0, probs)
    m2 = jnp.max(rest, axis=-1, keepdims=True)
    i2 = jnp.min(jnp.where(rest == m2, lane, LANES), axis=-1, keepdims=True)
    den = m1 + m2
    pw_ref[...] = jnp.where(lane == 0, m1 / den, jnp.where(lane == 1, m2 / den, 0.0))

    sel1 = lane == i1
    sel2 = lane == i2
    onehot = jnp.where(sel1, 1.0, 0.0) + jnp.where(sel2, 1.0, 0.0)
    r_i = lax.broadcasted_iota(jnp.int32, (tm, tm), 0)
    c_i = lax.broadcasted_iota(jnp.int32, (tm, tm), 1)
    tri = jnp.where(c_i < r_i, 1.0, 0.0).astype(BF16)
    before = jnp.dot(tri, onehot.astype(BF16), preferred_element_type=F32) + carry_scr[0:1, :]
    rank1 = jnp.sum(jnp.where(sel1, before, 0.0), axis=-1, keepdims=True)
    rank2 = jnp.sum(jnp.where(sel2, before, 0.0), axis=-1, keepdims=True)
    ei = jnp.where(lane == 0, i1, jnp.where(lane == 1, i2, 0))
    ei = jnp.where(lane == 2, rank1.astype(jnp.int32), jnp.where(lane == 3, rank2.astype(jnp.int32), ei))
    ei_ref[...] = ei
    total = carry_scr[0:1, :] + jnp.sum(onehot, axis=0, keepdims=True)
    carry_scr[0:1, :] = total
    cnt_ref[...] = jnp.broadcast_to(total, cnt_ref.shape)


def _router(x, mod, g, rw_pad, rb_pad, seq):
    n, d = x.shape
    tm = TM_ROUTE
    tpb = seq // tm
    return pl.pallas_call(
        functools.partial(_router_kernel, tm=tm),
        out_shape=(jax.ShapeDtypeStruct((n, d // 2), jnp.uint32),
                   jax.ShapeDtypeStruct((n, LANES), jnp.int32),
                   jax.ShapeDtypeStruct((n, LANES), F32),
                   jax.ShapeDtypeStruct((8, LANES), F32)),
        grid=(n // tm,),
        in_specs=[
            pl.BlockSpec((tm, d), lambda i: (i, 0)),
            pl.BlockSpec((None, 6, d), lambda i: (i // tpb, 0, 0)),
            pl.BlockSpec((1, d), lambda i: (0, 0)),
            pl.BlockSpec((d, LANES), lambda i: (0, 0)),
            pl.BlockSpec((1, LANES), lambda i: (0, 0)),
        ],
        out_specs=(pl.BlockSpec((tm, d // 2), lambda i: (i, 0)),
                   pl.BlockSpec((tm, LANES), lambda i: (i, 0)),
                   pl.BlockSpec((tm, LANES), lambda i: (i, 0)),
                   pl.BlockSpec((8, LANES), lambda i: (0, 0))),
        scratch_shapes=[pltpu.VMEM((8, LANES), F32)],
        compiler_params=_cparams("arbitrary"),
        name="router_top2",
    )(x, mod, g, rw_pad, rb_pad)


def _dispatch_kernel(pos_ref, hp_ref, xs_in_ref, xs_ref, sem, *, tm):
    del xs_in_ref
    base = pl.program_id(0) * (2 * tm)

    def issue(r, carry):
        for k in range(TOP_K):
            dst = pos_ref[base + 2 * r + k]
            pltpu.make_async_copy(hp_ref.at[pl.ds(r, 1)], xs_ref.at[pl.ds(dst, 1)], sem).start(priority=k)
        return carry

    lax.fori_loop(0, tm, issue, 0)
    for k in range(TOP_K):
        pltpu.make_async_copy(hp_ref, xs_ref.at[pl.ds(0, tm)], sem).wait()


def _dispatch(pos_flat, hp, xs_init):
    n, half = hp.shape
    tm = TM_DISP
    return pl.pallas_call(
        functools.partial(_dispatch_kernel, tm=tm),
        out_shape=jax.ShapeDtypeStruct(xs_init.shape, xs_init.dtype),
        grid_spec=pltpu.PrefetchScalarGridSpec(
            num_scalar_prefetch=1,
            grid=(n // tm,),
            in_specs=[pl.BlockSpec((tm, half), lambda i, pos: (i, 0)),
                      pl.BlockSpec(memory_space=pl.ANY)],
            out_specs=pl.BlockSpec(memory_space=pl.ANY),
            scratch_shapes=[pltpu.SemaphoreType.DMA],
        ),
        input_output_aliases={2: 0},
        compiler_params=_cparams("arbitrary"),
        name="moe_dispatch",
    )(pos_flat, hp, xs_init)


def _expert_kernel(te_ref, used_ref, xs_ref, w1_ref, w3_ref, w2_ref, y_ref, h_scr, acc_scr):
    t = pl.program_id(0)
    j = pl.program_id(1)
    del te_ref

    @pl.when(used_ref[t] == 1)
    def _():
        @pl.when(j == 0)
        def _():
            lo, hi = _unpack_bf16_pair(xs_ref[...])
            h_scr[...] = jnp.concatenate([lo.astype(BF16), hi.astype(BF16)], axis=1)
            acc_scr[...] = jnp.zeros(acc_scr.shape, F32)

        h = h_scr[...]
        a = jnp.dot(h, w1_ref[...], preferred_element_type=F32)
        b = jnp.dot(h, w3_ref[...], preferred_element_type=F32)
        hid = ((a * _sigmoid(a)) * b).astype(BF16)
        acc_scr[...] += jnp.dot(hid, w2_ref[...], preferred_element_type=F32)

        @pl.when(j == pl.num_programs(1) - 1)
        def _():
            acc = acc_scr[...]
            half = acc.shape[1] // 2
            y_ref[...] = _pack_bf16_pair(acc[:, :half], acc[:, half:])

    @pl.when(jnp.logical_and(used_ref[t] == 0, j == 0))
    def _():
        y_ref[...] = jnp.zeros(y_ref.shape, y_ref.dtype)


def _expert_ffn(tile_e, tile_used, xs, w1, w3, w2):
    rows, half = xs.shape
    d = 2 * half
    dff = w1.shape[2]
    tg, tf = TG_MOE, TF_MOE
    return pl.pallas_call(
        _expert_kernel,
        out_shape=jax.ShapeDtypeStruct((rows, half), jnp.uint32),
        grid_spec=pltpu.PrefetchScalarGridSpec(
            num_scalar_prefetch=2,
            grid=(rows // tg, dff // tf),
            in_specs=[
                pl.BlockSpec((tg, half), lambda t, j, te, us: (t, 0)),
                pl.BlockSpec((None, d, tf), lambda t, j, te, us: (te[t], 0, j * us[t])),
                pl.BlockSpec((None, d, tf), lambda t, j, te, us: (te[t], 0, j * us[t])),
                pl.BlockSpec((None, tf, d), lambda t, j, te, us: (te[t], j * us[t], 0)),
            ],
            out_specs=pl.BlockSpec((tg, half), lambda t, j, te, us: (t, 0)),
            scratch_shapes=[pltpu.VMEM((tg, d), BF16), pltpu.VMEM((tg, d), F32)],
        ),
        compiler_params=_cparams("arbitrary", "arbitrary"),
        name="expert_swiglu",
    )(tile_e, tile_used, xs, w1, w3, w2)


def _combine_kernel(pos_ref, x_ref, pw_ref, mod_ref, y_ref, o_ref, buf0, buf1, sem, *, tm):
    base = pl.program_id(0) * (2 * tm)
    bufs = (buf0, buf1)

    def issue(r, carry):
        for k in range(TOP_K):
            src = pos_ref[base + 2 * r + k]
            pltpu.make_async_copy(y_ref.at[pl.ds(src, 1)], bufs[k].at[pl.ds(r, 1)], sem).start(priority=k)
        return carry

    lax.fori_loop(0, tm, issue, 0)
    for k in range(TOP_K):
        pltpu.make_async_copy(y_ref.at[pl.ds(0, tm)], bufs[k], sem).wait()

    p0 = pw_ref[:, 0:1]
    p1 = pw_ref[:, 1:2]
    lo0, hi0 = _unpack_bf16_pair(buf0[...])
    lo1, hi1 = _unpack_bf16_pair(buf1[...])
    half = lo0.shape[1]
    g2 = mod_ref[5:6, :]
    o_ref[:, :half] = x_ref[:, :half] + g2[:, :half] * (lo0 * p0 + lo1 * p1)
    o_ref[:, half:] = x_ref[:, half:] + g2[:, half:] * (hi0 * p0 + hi1 * p1)


def _combine(pos_flat, x, pw, mod, y, seq):
    n, d = x.shape
    tm = TM_COMB
    tpb = seq // tm
    return pl.pallas_call(
        functools.partial(_combine_kernel, tm=tm),
        out_shape=jax.ShapeDtypeStruct((n, d), F32),
        grid_spec=pltpu.PrefetchScalarGridSpec(
            num_scalar_prefetch=1,
            grid=(n // tm,),
            in_specs=[pl.BlockSpec((tm, d), lambda i, pos: (i, 0)),
                      pl.BlockSpec((tm, LANES), lambda i, pos: (i, 0)),
                      pl.BlockSpec((None, 6, d), lambda i, pos: (i // tpb, 0, 0)),
                      pl.BlockSpec(memory_space=pl.ANY)],
            out_specs=pl.BlockSpec((tm, d), lambda i, pos: (i, 0)),
            scratch_shapes=[pltpu.VMEM((tm, d // 2), jnp.uint32), pltpu.VMEM((tm, d // 2), jnp.uint32),
                            pltpu.SemaphoreType.DMA],
        ),
        compiler_params=_cparams("arbitrary"),
        name="moe_combine",
    )(pos_flat, x, pw, mod, y)


def _pack_w_in(w_in):
    d = w_in.shape[0]
    w = MIX_W
    o_ckv = 3 * w + Q_LORA
    o_kr = o_ckv + KV_LORA
    o_su = o_kr + QK_ROPE
    o_rq = o_su + 2 * w
    o_gate = o_rq + 4 * w
    half = HEAD_DIM // 2
    perm = np.array([h * HEAD_DIM + part * half + i
                     for part in range(2) for h in range(N_GROUPS) for i in range(half)])
    kr = w_in[:, o_kr:o_kr + QK_ROPE]
    hr = QK_ROPE // 2
    z = lambda k: jnp.zeros((d, k), w_in.dtype)
    cols = [
        w_in[:, o_gate:o_gate + N_BRANCH * d],
        w_in[:, 0:3 * w + Q_LORA],
        w_in[:, o_rq:o_rq + w][:, perm], w_in[:, o_rq + w:o_rq + 2 * w][:, perm],
        w_in[:, o_rq + 2 * w:o_rq + 4 * w],
        w_in[:, o_su:o_su + 2 * w],
        w_in[:, o_ckv:o_ckv + KV_LORA],
        z(QK_NOPE), kr, z(LANES - QK_HEAD),
        z(QK_NOPE), kr[:, hr:], kr[:, :hr], z(LANES - QK_HEAD),
        z(N_IN - COL_KRB - LANES),
    ]
    return jnp.concatenate(cols, axis=1).astype(BF16)


def _swap_rope_halves(a):
    hr = QK_ROPE // 2
    return jnp.concatenate([a[..., :QK_NOPE], a[..., QK_NOPE + hr:QK_HEAD], a[..., QK_NOPE:QK_NOPE + hr],
                            a[..., QK_HEAD:]], axis=-1)


def _mla_params(cq_g, w_uq, ckv_g, w_ukv, qn_g, kn_g):
    pad = LANES - QK_HEAD
    wq = w_uq.reshape(Q_LORA, N_GROUPS, QK_HEAD)
    wq = jnp.pad(wq, ((0, 0), (0, 0), (0, pad)))
    wkv = w_ukv.reshape(KV_LORA, N_GROUPS, QK_NOPE + V_HEAD)
    wk = jnp.pad(wkv[:, :, :QK_NOPE], ((0, 0), (0, 0), (0, LANES - QK_NOPE)))
    wv = jnp.pad(wkv[:, :, QK_NOPE:], ((0, 0), (0, 0), (0, LANES - V_HEAD)))
    qg = jnp.pad(qn_g, (0, pad))[None, :]
    kg = jnp.pad(kn_g, (0, pad))[None, :]
    bound = (QK_HEAD ** 0.5 * LOG2_E) * jnp.max(jnp.abs(qn_g)) * jnp.max(jnp.abs(kn_g))
    static_shift = bound <= MAX_STATIC_SHIFT
    lane = jnp.arange(LANES)
    qaug = (lane == QK_HEAD).astype(F32)[None, :]
    kaug = qaug * jnp.where(static_shift, -bound, 0.0)
    vaug = jnp.tile((lane == V_HEAD).astype(F32), N_GROUPS)[None, :]
    params = {
        "cq_g": cq_g[None, :], "ckv_g": ckv_g[None, :],
        "wqa": wq.reshape(Q_LORA, -1).astype(BF16),
        "wqb": _swap_rope_halves(wq).reshape(Q_LORA, -1).astype(BF16),
        "wk": wk.reshape(KV_LORA, -1).astype(BF16),
        "wv": wv.reshape(KV_LORA, -1).astype(BF16),
        "qga": qg, "qgb": _swap_rope_halves(qg), "kga": kg, "kgb": _swap_rope_halves(kg),
        "qaug": qaug, "kaug": kaug, "vaug": vaug,
    }
    return params, static_shift


def _mixer_consts():
    h = jnp.arange(N_GROUPS, dtype=F32)
    log_gamma = jnp.log1p(-(2.0 ** (-5.0 - h)))
    pos = jnp.arange(CHUNK, dtype=F32)
    rel = pos[:, None] - pos[None, :]
    dec = jnp.where(rel >= 0, jnp.exp(log_gamma[:, None, None] * jnp.maximum(rel, 0.0)), 0.0)
    lane = np.arange(MIX_W)
    head_k = (lane % LANES) // (HEAD_DIM // 2)
    head_v = lane // HEAD_DIM
    lg_k = log_gamma[head_k]
    return {
        "dec": dec.reshape(N_GROUPS * CHUNK, CHUNK),
        "kdec": jnp.exp(lg_k[None, :] * (CHUNK - 1.0 - pos)[:, None]),
        "qdec": jnp.exp(lg_k[None, :] * (pos + 1.0)[:, None]),
        "cdec": jnp.broadcast_to(jnp.exp(lg_k * CHUNK)[:, None], (MIX_W, MIX_W)),
        "bd": jnp.asarray((head_k[:, None] == head_v[None, :]).astype(np.float32)),
        "gmat": jnp.asarray((head_v[:, None] == head_v[None, :]).astype(np.float32) / HEAD_DIM),
        "mk": jnp.asarray((head_k[None, :] == np.arange(N_GROUPS)[:, None]).astype(np.float32)),
        "mv": jnp.asarray((head_v[None, :] == np.arange(N_GROUPS)[:, None]).astype(np.float32)),
    }


def _mixer_params(conv_w, gv_g, w_s, b_s, ret_g, w_branch, w_o):
    p = dict(_mixer_consts())
    ws = jnp.tril(w_s)
    p.update({
        "conv_w": conv_w,
        "gv_g": gv_g.reshape(1, MIX_W),
        "ws_cat": jnp.transpose(ws, (1, 0, 2)).reshape(CHUNK, N_GROUPS * CHUNK).astype(BF16),
        "bs_mat": jnp.repeat(b_s.T, HEAD_DIM, axis=1),
        "ret_g": ret_g.reshape(1, MIX_W),
        "w_branch": w_branch.astype(BF16),
        "w_o": w_o.astype(BF16),
    })
    return p


def _moe_layout(ei, cnt, n_tok, n_tiles):
    counts = cnt[0, :N_EXPERTS].astype(jnp.int32)
    padded = ((counts + TG_MOE - 1) // TG_MOE) * TG_MOE
    ends = jnp.cumsum(padded)
    starts = ends - padded
    e01 = ei[:, 0:TOP_K]
    rank = ei[:, TOP_K:2 * TOP_K]
    onehot = (e01[:, :, None] == jnp.arange(N_EXPERTS)[None, None, :]).astype(jnp.int32)
    pos = jnp.sum(onehot * starts[None, None, :], axis=-1) + rank
    tile_start = jnp.arange(n_tiles, dtype=jnp.int32) * TG_MOE
    tile_e = jnp.sum((tile_start[:, None] >= ends[None, :]).astype(jnp.int32), axis=1)
    used = (tile_start < ends[-1]).astype(jnp.int32)
    last_e = jnp.sum((ends[-1] - 1 >= ends).astype(jnp.int32))
    tile_e = jnp.where(used == 1, tile_e, last_e)
    return pos.reshape(n_tok * TOP_K), jnp.minimum(tile_e, N_EXPERTS - 1), used


def kernel(x, c, positions, norm1_g, norm2_g, ada_w, ada_b, w_in, conv_w, cq_g, w_uq, ckv_g, w_ukv, qn_g, kn_g, gv_g, w_s, b_s, ret_g, w_branch, w_o, ffn_w1, ffn_w3, ffn_w2, router_w, router_b, moe_w1, moe_w3, moe_w2):
    batch, seq, d = x.shape
    depth = ada_w.shape[0]
    n = batch * seq
    assert seq % max(TM_PROJ, TM_PREP, TQ_ATT, TM_MIX, TM_FFN, TM_ROUTE, TM_COMB, TM_DISP) == 0
    assert d // 2 % LANES == 0

    c_pad = jnp.pad(c, ((0, 8 - batch), (0, 0)))
    ada = _ada(c_pad, ada_w, ada_b)[:, :batch].reshape(depth, batch, 6, d)
    cosr, sinr, cm, sm = _rope_tables(positions.astype(F32).reshape(n, 1))

    xt = x.reshape(n, d)
    for l in range(depth):
        mod = ada[l]
        proj = _inproj(xt, mod, norm1_g[l][None, :], _pack_w_in(w_in[l]), seq)
        mla_p, static_shift = _mla_params(cq_g[l], w_uq[l], ckv_g[l], w_ukv[l], qn_g[l], kn_g[l])
        q, k, v = _mla_prep(proj, cm, sm, mla_p)
        y_mla = lax.cond(static_shift,
                         functools.partial(_flash, batch=batch, seq=seq, online_max=False),
                         functools.partial(_flash, batch=batch, seq=seq, online_max=True), q, k, v)
        mp = _mixer_params(conv_w[l], gv_g[l], w_s[l], b_s[l], ret_g[l], w_branch[l], w_o[l])
        xt = _mixers(proj, y_mla, xt, cosr, sinr, mod, mp, seq)
        g2n = norm2_g[l][None, :]
        if l % 2 == 0:
            i = l // 2
            xt = _dense_ffn(xt, mod, g2n, ffn_w1[i].astype(BF16), ffn_w3[i].astype(BF16),
                            ffn_w2[i].astype(BF16), seq)
        else:
            i = l // 2
            rw_pad = jnp.pad(router_w[i], ((0, 0), (0, LANES - N_EXPERTS)))
            rb_pad = jnp.pad(router_b[i], (0, LANES - N_EXPERTS), constant_values=-1e30)[None, :]
            hp, ei, pw, cnt = _router(xt, mod, g2n, rw_pad, rb_pad, seq)
            n_tiles = (n * TOP_K) // TG_MOE + N_EXPERTS
            pos_flat, tile_e, used = _moe_layout(ei, cnt, n, n_tiles)
            xs = _dispatch(pos_flat, hp, jnp.zeros((n_tiles * TG_MOE, d // 2), jnp.uint32))
            y = _expert_ffn(tile_e, used, xs, moe_w1[i].astype(BF16), moe_w3[i].astype(BF16),
                            moe_w2[i].astype(BF16))
            xt = _combine(pos_flat, xt, pw, mod, y, seq)
    return xt.reshape(batch, seq, d)
```

```python
import functools

import jax
import jax.numpy as jnp
import numpy as np
from jax import lax
from jax.experimental import pallas as pl
from jax.experimental.pallas import tpu as pltpu

F32 = jnp.float32
BF16 = jnp.bfloat16
HIGHEST = lax.Precision.HIGHEST

HEAD_DIM = 64
N_GROUPS = 4
MIX_W = N_GROUPS * HEAD_DIM
N_BRANCH = 4
CONV_W = 3
Q_LORA = 256
KV_LORA = 128
QK_NOPE = 64
QK_ROPE = 32
QK_HEAD = QK_NOPE + QK_ROPE
V_HEAD = 64
CHUNK = 128
N_EXPERTS = 8
TOP_K = 2
ROPE_THETA = 10000.0
EPS = 1e-6
LOG2_E = 1.4426950408889634
MAX_STATIC_SHIFT = 50.0

LANES = 128
VMEM_LIMIT_BYTES = 56 * 1024 * 1024

COL_GATES = 0
COL_A = 4096
COL_CQ = COL_A + 3 * MIX_W
COL_R = 5120
COL_SU = 6144
COL_CKV = 6656
COL_KRA = 6784
COL_KRB = 6912
N_IN = 7168

TM_PROJ = 512
TN_PROJ = 1024
TM_PREP = 1024
TQ_ATT = 1024
TM_MIX = 512
TM_FFN = 512
TM_ROUTE = 512
ROW_ALIGN = 8
SORT_ROWS = TOP_K * TM_ROUTE + N_EXPERTS * ROW_ALIGN
TG_MOE = 512


def _cparams(*sem):
    return pltpu.CompilerParams(dimension_semantics=sem, vmem_limit_bytes=VMEM_LIMIT_BYTES)


def _sigmoid(x):
    return jnp.tanh(x * 0.5) * 0.5 + 0.5


def _group_mean(x, gmat_bf16):
    hi = x.astype(BF16)
    lo = (x - hi.astype(F32)).astype(BF16)
    return (jnp.dot(hi, gmat_bf16, preferred_element_type=F32)
            + jnp.dot(lo, gmat_bf16, preferred_element_type=F32))


def _pack_bf16_pair(lo, hi):
    lo_bits = lax.bitcast_convert_type(lo.astype(BF16).astype(F32), jnp.uint32)
    hi_bits = lax.bitcast_convert_type(hi.astype(BF16).astype(F32), jnp.uint32)
    return (lo_bits >> 16) | (hi_bits & jnp.uint32(0xFFFF0000))


def _unpack_bf16_pair(p):
    lo = lax.bitcast_convert_type(p << 16, F32)
    hi = lax.bitcast_convert_type(p & jnp.uint32(0xFFFF0000), F32)
    return lo, hi


def _norm_mod(x, g, shift, scale):
    y = x * lax.rsqrt(jnp.mean(x * x, axis=-1, keepdims=True) + EPS)
    return (y * g) * (1.0 + scale) + shift


def _ada_kernel(c_ref, w_ref, b_ref, o_ref):
    c = c_ref[...]
    cond = c * _sigmoid(c)
    o_ref[...] = jnp.dot(cond, w_ref[...], precision=HIGHEST, preferred_element_type=F32) + b_ref[...]


def _ada(c_pad, ada_w, ada_b):
    n_layer, d, d6 = ada_w.shape
    rows = c_pad.shape[0]
    tn = 1024
    return pl.pallas_call(
        _ada_kernel,
        out_shape=jax.ShapeDtypeStruct((n_layer, rows, d6), F32),
        grid=(n_layer, d6 // tn),
        in_specs=[
            pl.BlockSpec((rows, d), lambda l, j: (0, 0)),
            pl.BlockSpec((None, d, tn), lambda l, j: (l, 0, j)),
            pl.BlockSpec((None, 1, tn), lambda l, j: (l, 0, j)),
        ],
        out_specs=pl.BlockSpec((None, rows, tn), lambda l, j: (l, 0, j)),
        compiler_params=_cparams("parallel", "parallel"),
        name="ada_mod",
    )(c_pad, ada_w, ada_b.reshape(n_layer, 1, d6))


def _rope_kernel(pos_ref, invr_ref, invm_ref, signm_ref, cr_ref, sr_ref, cm_ref, sm_ref):
    pos = pos_ref[...]
    ang_r = pos * invr_ref[...]
    cr_ref[...] = jnp.cos(ang_r)
    sr_ref[...] = jnp.sin(ang_r)
    ang_m = pos * invm_ref[...]
    cm_ref[...] = jnp.cos(ang_m)
    sm_ref[...] = jnp.sin(ang_m) * signm_ref[...]


def _rope_tables(pos_f):
    n = pos_f.shape[0]
    tm = 1024
    half_r = HEAD_DIM // 2
    inv_r = ROPE_THETA ** (-jnp.arange(half_r, dtype=F32) / half_r)
    inv_r = jnp.tile(inv_r, LANES // half_r)[None, :]
    half_m = QK_ROPE // 2
    inv_m1 = ROPE_THETA ** (-jnp.arange(half_m, dtype=F32) / half_m)
    zeros = lambda k: jnp.zeros((k,), F32)
    inv_m = jnp.concatenate([zeros(QK_NOPE), inv_m1, inv_m1, zeros(LANES - QK_HEAD)])[None, :]
    sign_m = jnp.concatenate([zeros(QK_NOPE), -jnp.ones((half_m,), F32), jnp.ones((half_m,), F32),
                              zeros(LANES - QK_HEAD)])[None, :]
    row = pl.BlockSpec((1, LANES), lambda i: (0, 0))
    tab = pl.BlockSpec((tm, LANES), lambda i: (i, 0))
    shape = jax.ShapeDtypeStruct((n, LANES), F32)
    return pl.pallas_call(
        _rope_kernel,
        out_shape=(shape, shape, shape, shape),
        grid=(n // tm,),
        in_specs=[pl.BlockSpec((tm, 1), lambda i: (i, 0)), row, row, row],
        out_specs=(tab, tab, tab, tab),
        compiler_params=_cparams("parallel"),
        name="rope_tables",
    )(pos_f, inv_r, inv_m, sign_m)


def _inproj_kernel(x_ref, mod_ref, g_ref, w_ref, o_ref):
    h = _norm_mod(x_ref[...], g_ref[...], mod_ref[0:1, :], mod_ref[1:2, :]).astype(BF16)
    for c in range(N_IN // TN_PROJ):
        cols = slice(c * TN_PROJ, (c + 1) * TN_PROJ)
        o_ref[:, cols] = jnp.dot(h, w_ref[:, cols], preferred_element_type=F32).astype(BF16)


def _resident(shape):
    return pl.BlockSpec(shape, lambda *_: (0,) * len(shape), pipeline_mode=pl.Buffered(1))


def _inproj(x, mod, g, w, seq):
    n, d = x.shape
    tm = TM_PROJ
    tpb = seq // tm
    return pl.pallas_call(
        _inproj_kernel,
        out_shape=jax.ShapeDtypeStruct((n, N_IN), BF16),
        grid=(n // tm,),
        in_specs=[
            pl.BlockSpec((tm, d), lambda i: (i, 0)),
            pl.BlockSpec((None, 6, d), lambda i: (i // tpb, 0, 0)),
            pl.BlockSpec((1, d), lambda i: (0, 0)),
            _resident((d, N_IN)),
        ],
        out_specs=pl.BlockSpec((tm, N_IN), lambda i: (i, 0)),
        compiler_params=_cparams("parallel"),
        name="in_proj",
    )(x, mod, g, w)


def _mla_prep_kernel(cq_ref, ckv_ref, kra_ref, krb_ref, cm_ref, sm_ref, cqg_ref, wqa_ref, wqb_ref,
                     ckvg_ref, wk_ref, wv_ref, qga_ref, qgb_ref, kga_ref, kgb_ref,
                     qaug_ref, kaug_ref, vaug_ref, q_ref, k_ref, v_ref):
    cq = cq_ref[...].astype(F32)
    cqn = (cq * lax.rsqrt(jnp.mean(cq * cq, axis=-1, keepdims=True) + EPS) * cqg_ref[...]).astype(BF16)
    qa = jnp.dot(cqn, wqa_ref[...], preferred_element_type=F32)
    qb = jnp.dot(cqn, wqb_ref[...], preferred_element_type=F32)
    ckv = ckv_ref[...].astype(F32)
    ckvn = (ckv * lax.rsqrt(jnp.mean(ckv * ckv, axis=-1, keepdims=True) + EPS) * ckvg_ref[...]).astype(BF16)
    ka = jnp.dot(ckvn, wk_ref[...], preferred_element_type=F32)
    v_ref[...] = (jnp.dot(ckvn, wv_ref[...], preferred_element_type=F32) + vaug_ref[...]).astype(BF16)
    kra = kra_ref[...].astype(F32)
    krb = krb_ref[...].astype(F32)
    cm = cm_ref[...]
    sm = sm_ref[...]
    scale = QK_HEAD ** -0.5 * LOG2_E
    for h in range(N_GROUPS):
        sl = slice(h * LANES, (h + 1) * LANES)
        qah, qbh = qa[:, sl], qb[:, sl]
        r = lax.rsqrt(jnp.sum(qah * qah, axis=-1, keepdims=True) * (1.0 / QK_HEAD) + EPS)
        q_rot = (qah * r) * qga_ref[...] * cm + (qbh * r) * qgb_ref[...] * sm
        q_ref[:, sl] = (q_rot * scale + qaug_ref[...]).astype(BF16)
        kah = ka[:, sl] + kra
        kbh = ka[:, sl] + krb
        r = lax.rsqrt(jnp.sum(kah * kah, axis=-1, keepdims=True) * (1.0 / QK_HEAD) + EPS)
        k_rot = (kah * r) * kga_ref[...] * cm + (kbh * r) * kgb_ref[...] * sm
        k_ref[:, sl] = (k_rot + kaug_ref[...]).astype(BF16)


def _mla_prep(proj, cm, sm, p):
    n = proj.shape[0]
    tm = TM_PREP
    hw = N_GROUPS * LANES

    def col(width, offset):
        return pl.BlockSpec((tm, width), lambda i: (i, offset // width))

    def full(a):
        return pl.BlockSpec(a.shape, lambda i: (0,) * a.ndim)

    weights = [p["cq_g"], p["wqa"], p["wqb"], p["ckv_g"], p["wk"], p["wv"],
               p["qga"], p["qgb"], p["kga"], p["kgb"], p["qaug"], p["kaug"], p["vaug"]]
    head_tile = pl.BlockSpec((tm, hw), lambda i: (i, 0))
    out = jax.ShapeDtypeStruct((n, hw), BF16)
    return pl.pallas_call(
        _mla_prep_kernel,
        out_shape=(out, out, out),
        grid=(n // tm,),
        in_specs=[col(Q_LORA, COL_CQ), col(KV_LORA, COL_CKV), col(LANES, COL_KRA), col(LANES, COL_KRB),
                  pl.BlockSpec((tm, LANES), lambda i: (i, 0)), pl.BlockSpec((tm, LANES), lambda i: (i, 0))]
                 + [full(w) for w in weights],
        out_specs=(head_tile, head_tile, head_tile),
        compiler_params=_cparams("parallel"),
        name="mla_prep",
    )(proj, proj, proj, proj, cm, sm, *weights)


def _flash_kernel(q_ref, k_ref, v_ref, o_ref, acc_scr, *rest, tq, online_max):
    i = pl.program_id(1)
    j = pl.program_id(2)

    @pl.when(j == 0)
    def _():
        acc_scr[...] = jnp.zeros(acc_scr.shape, F32)
        if online_max:
            rest[0][...] = jnp.full(rest[0].shape, -jnp.inf, F32)

    def step(masked):
        if masked:
            row = lax.broadcasted_iota(jnp.int32, (tq, tq), 0)
            col = lax.broadcasted_iota(jnp.int32, (tq, tq), 1)
            keep = col <= row
        for h in range(N_GROUPS):
            sl = slice(h * LANES, (h + 1) * LANES)
            s = lax.dot_general(q_ref[:, sl], k_ref[:, sl], (((1,), (1,)), ((), ())),
                                preferred_element_type=F32)
            if masked:
                s = jnp.where(keep, s, -jnp.inf)
            if online_max:
                m_scr = rest[0]
                m_prev = m_scr[h]
                m_new = jnp.maximum(m_prev, jnp.max(s, axis=-1, keepdims=True))
                p = jnp.exp2(s - m_new).astype(BF16)
                acc_scr[h] = jnp.exp2(m_prev - m_new) * acc_scr[h] + jnp.dot(
                    p, v_ref[:, sl], preferred_element_type=F32)
                m_scr[h] = m_new
            else:
                acc_scr[h] += jnp.dot(jnp.exp2(s).astype(BF16), v_ref[:, sl], preferred_element_type=F32)

    @pl.when(j < i)
    def _():
        step(False)

    @pl.when(j == i)
    def _():
        step(True)
        lane = lax.broadcasted_iota(jnp.int32, (tq, LANES), 1)
        for pr in range(N_GROUPS // 2):
            lo = acc_scr[2 * pr]
            hi = acc_scr[2 * pr + 1]
            lo = lo / lo[:, V_HEAD:V_HEAD + 1]
            hi = hi / hi[:, V_HEAD:V_HEAD + 1]
            both = jnp.where(lane < V_HEAD, lo, pltpu.roll(hi, V_HEAD, axis=1))
            o_ref[:, pr * LANES:(pr + 1) * LANES] = both.astype(BF16)


def _flash(q, k, v, batch, seq, online_max):
    n = q.shape[0]
    tq = TQ_ATT
    nq = seq // tq
    hw = N_GROUPS * LANES
    scratch = [pltpu.VMEM((N_GROUPS, tq, LANES), F32)]
    if online_max:
        scratch.append(pltpu.VMEM((N_GROUPS, tq, 1), F32))
    return pl.pallas_call(
        functools.partial(_flash_kernel, tq=tq, online_max=online_max),
        out_shape=jax.ShapeDtypeStruct((n, MIX_W), BF16),
        grid=(batch, nq, nq),
        in_specs=[
            pl.BlockSpec((tq, hw), lambda b, i, j: (b * nq + i, 0)),
            pl.BlockSpec((tq, hw), lambda b, i, j: (b * nq + jnp.minimum(j, i), 0)),
            pl.BlockSpec((tq, hw), lambda b, i, j: (b * nq + jnp.minimum(j, i), 0)),
        ],
        out_specs=pl.BlockSpec((tq, MIX_W), lambda b, i, j: (b * nq + i, 0)),
        scratch_shapes=scratch,
        compiler_params=_cparams("parallel", "parallel", "arbitrary"),
        name="mla_flash_online" if online_max else "mla_flash",
    )(q, k, v)


def _gelu_tanh(x):
    return jax.nn.gelu(x, approximate=True)


def _mix_kernel(gates_ref, a_ref, r_ref, su_ref, ymla_ref, x_ref, cos_ref, sin_ref, mod_ref,
                convw_ref, gvg_ref, wscat_ref, bsmat_ref, retg_ref, dec_ref, kdec_ref, qdec_ref,
                cdec_ref, bd_ref, gmat_ref, mk_ref, mv_ref, wb_ref, wo_ref,
                o_ref, carry_scr, state_scr, ysg_scr, yret_scr, *, tm, tpb):
    i = pl.program_id(0)

    @pl.when(i % tpb == 0)
    def _():
        carry_scr[...] = jnp.zeros(carry_scr.shape, F32)
        state_scr[...] = jnp.zeros(state_scr.shape, F32)

    w = MIX_W
    a_b = a_ref[:, 0:w].astype(F32)
    u = a_ref[:, w:2 * w].astype(F32) * a_ref[:, 2 * w:3 * w].astype(F32)
    rowi = lax.broadcasted_iota(jnp.int32, (tm, w), 0)
    prev1 = carry_scr[0:1, :]
    prev2 = carry_scr[1:2, :]
    u1 = jnp.where(rowi == 0, prev1, pltpu.roll(u, 1, axis=0))
    u2 = jnp.where(rowi == 0, prev2, jnp.where(rowi == 1, prev1, pltpu.roll(u, 2, axis=0)))
    carry_scr[0:1, :] = u[tm - 1:tm, :]
    carry_scr[1:2, :] = u[tm - 2:tm - 1, :]
    y_conv = a_b * (convw_ref[0:1, :] * u2 + convw_ref[1:2, :] * u1 + convw_ref[2:3, :] * u)

    gmat = gmat_ref[...]
    s_u = _gelu_tanh(su_ref[:, 0:w].astype(F32))
    s_v = _gelu_tanh(su_ref[:, w:2 * w].astype(F32))
    ms = _group_mean(s_v * s_v, gmat)
    vn = (s_v * lax.rsqrt(ms + EPS) * gvg_ref[...]).astype(BF16)

    cosr = cos_ref[...]
    sinr = sin_ref[...]

    def rot(t):
        t1, t2 = t[:, 0:LANES], t[:, LANES:2 * LANES]
        return jnp.concatenate([t1 * cosr - t2 * sinr, t2 * cosr + t1 * sinr], axis=-1)

    rq = rot(r_ref[:, 0:w].astype(F32))
    rk = rot(r_ref[:, w:2 * w].astype(F32)) * (HEAD_DIM ** -0.5)

    for c in range(tm // CHUNK):
        rows = slice(c * CHUNK, (c + 1) * CHUNK)
        vc = vn[rows, :]
        vbd = jnp.concatenate([vc * mv_ref[g:g + 1, :].astype(BF16) for g in range(N_GROUPS)], axis=0)
        mixed = jnp.dot(wscat_ref[...], vbd, preferred_element_type=F32) + bsmat_ref[...]
        ysg_scr[rows, :] = s_u[rows, :] * mixed

        qc = rq[rows, :]
        kc = rk[rows, :]
        kcb = kc.astype(BF16)
        vcb = r_ref[rows, 2 * w:3 * w]
        qstack = jnp.concatenate([(qc * mk_ref[h:h + 1, :]).astype(BF16) for h in range(N_GROUPS)], axis=0)
        sc = lax.dot_general(qstack, kcb, (((1,), (1,)), ((), ())), preferred_element_type=F32)
        sc = (sc * dec_ref[...]).astype(BF16)
        scat = jnp.concatenate([sc[h * CHUNK:(h + 1) * CHUNK, :] for h in range(N_GROUPS)], axis=1)
        vstack = jnp.concatenate([vcb * mv_ref[h:h + 1, :].astype(BF16) for h in range(N_GROUPS)], axis=0)
        o_c = jnp.dot(scat, vstack, preferred_element_type=F32)
        state = state_scr[...]
        o_c = o_c + jnp.dot((qc * qdec_ref[...]).astype(BF16), state.astype(BF16),
                            preferred_element_type=F32)
        kd_t = jnp.transpose(kc * kdec_ref[...]).astype(BF16)
        kv = jnp.dot(kd_t, vcb, preferred_element_type=F32)
        state_scr[...] = state * cdec_ref[...] + kv * bd_ref[...]
        yret_scr[rows, :] = o_c

    o_all = yret_scr[...]
    xc = o_all - _group_mean(o_all, gmat)
    var = _group_mean(xc * xc, gmat)
    r_g = r_ref[:, 3 * w:4 * w].astype(F32)
    y_ret = (r_g * _sigmoid(r_g)) * (xc * lax.rsqrt(var + EPS) * retg_ref[...])

    d = x_ref.shape[1]
    ys = (y_conv, ymla_ref[...], ysg_scr[...], y_ret)
    merged = jnp.zeros((tm, d), F32)
    for n in range(N_BRANCH):
        gate = _sigmoid(gates_ref[:, n * d:(n + 1) * d]).astype(F32)
        merged = merged + gate * jnp.dot(ys[n].astype(BF16), wb_ref[n], preferred_element_type=F32)
    out = jnp.dot(merged.astype(BF16), wo_ref[...], preferred_element_type=F32)
    o_ref[...] = x_ref[...] + mod_ref[2:3, :] * out


def _mixers(proj, ymla, x, cosr, sinr, mod, p, seq):
    n, d = x.shape
    tm = TM_MIX
    tpb = seq // tm

    def col(width, offset):
        return pl.BlockSpec((tm, width), lambda i: (i, offset // width))

    def full(a):
        return pl.BlockSpec(a.shape, lambda i: (0,) * a.ndim)

    consts = [p["conv_w"], p["gv_g"], p["ws_cat"], p["bs_mat"], p["ret_g"], p["dec"], p["kdec"],
              p["qdec"], p["cdec"], p["bd"], p["gmat"], p["mk"], p["mv"], p["w_branch"], p["w_o"]]
    return pl.pallas_call(
        functools.partial(_mix_kernel, tm=tm, tpb=tpb),
        out_shape=jax.ShapeDtypeStruct((n, d), F32),
        grid=(n // tm,),
        in_specs=[col(N_BRANCH * d, COL_GATES), col(4 * MIX_W, COL_A), col(4 * MIX_W, COL_R),
                  col(2 * MIX_W, COL_SU),
                  pl.BlockSpec((tm, MIX_W), lambda i: (i, 0)),
                  pl.BlockSpec((tm, d), lambda i: (i, 0)),
                  pl.BlockSpec((tm, LANES), lambda i: (i, 0)),
                  pl.BlockSpec((tm, LANES), lambda i: (i, 0)),
                  pl.BlockSpec((None, 6, d), lambda i: (i // tpb, 0, 0))]
                 + [full(c) for c in consts],
        out_specs=pl.BlockSpec((tm, d), lambda i: (i, 0)),
        scratch_shapes=[pltpu.VMEM((8, MIX_W), F32), pltpu.VMEM((MIX_W, MIX_W), F32),
                        pltpu.VMEM((tm, MIX_W), F32), pltpu.VMEM((tm, MIX_W), F32)],
        compiler_params=_cparams("arbitrary"),
        name="mixers_merge",
    )(proj, proj, proj, proj, ymla, x, cosr, sinr, mod, *consts)


def _ffn_kernel(x_ref, mod_ref, g_ref, w1_ref, w3_ref, w2_ref, o_ref):
    x = x_ref[...]
    h = _norm_mod(x, g_ref[...], mod_ref[3:4, :], mod_ref[4:5, :]).astype(BF16)
    a = jnp.dot(h, w1_ref[...], preferred_element_type=F32)
    b = jnp.dot(h, w3_ref[...], preferred_element_type=F32)
    hid = ((a * _sigmoid(a)) * b).astype(BF16)
    o_ref[...] = x + mod_ref[5:6, :] * jnp.dot(hid, w2_ref[...], preferred_element_type=F32)


def _dense_ffn(x, mod, g, w1, w3, w2, seq):
    n, d = x.shape
    dff = w1.shape[1]
    tm = TM_FFN
    tpb = seq // tm
    return pl.pallas_call(
        _ffn_kernel,
        out_shape=jax.ShapeDtypeStruct((n, d), F32),
        grid=(n // tm,),
        in_specs=[
            pl.BlockSpec((tm, d), lambda i: (i, 0)),
            pl.BlockSpec((None, 6, d), lambda i: (i // tpb, 0, 0)),
            pl.BlockSpec((1, d), lambda i: (0, 0)),
            _resident((d, dff)), _resident((d, dff)), _resident((dff, d)),
        ],
        out_specs=pl.BlockSpec((tm, d), lambda i: (i, 0)),
        compiler_params=_cparams("parallel"),
        name="dense_swiglu",
    )(x, mod, g, w1, w3, w2)


def _router_kernel(x_ref, mod_ref, g_ref, rw_ref, rb_ref, hs_ref, ei_ref, pw_ref, meta_ref, tot_ref,
                   carry_scr, *, tm, srows):
    i = pl.program_id(0)

    @pl.when(i == 0)
    def _():
        carry_scr[...] = jnp.zeros(carry_scr.shape, F32)

    h = _norm_mod(x_ref[...], g_ref[...], mod_ref[3:4, :], mod_ref[4:5, :])

    h_hi = h.astype(BF16)
    h_lo = (h - h_hi.astype(F32)).astype(BF16)
    hw = jnp.dot(h_hi, rw_ref[...], preferred_element_type=F32)
    logits = (hw[:, :LANES] + hw[:, LANES:] + jnp.dot(h_lo, rw_ref[:, :LANES], preferred_element_type=F32)
              + rb_ref[...])
    mx = jnp.max(logits, axis=-1, keepdims=True)
    ex = jnp.exp(logits - mx)
    probs = ex / jnp.sum(ex, axis=-1, keepdims=True)
    lane = lax.broadcasted_iota(jnp.int32, (tm, LANES), 1)
    valid = lane < N_EXPERTS
    probs = jnp.where(valid, probs, -1.0)
    m1 = jnp.max(probs, axis=-1, keepdims=True)
    i1 = jnp.min(jnp.where(probs == m1, lane, LANES), axis=-1, keepdims=True)
    rest = jnp.where(lane == i1, -1.0, probs)
    m2 = jnp.max(rest, axis=-1, keepdims=True)
    i2 = jnp.min(jnp.where(rest == m2, lane, LANES), axis=-1, keepdims=True)
    den = m1 + m2
    pw_ref[...] = jnp.where(lane == 0, m1 / den, jnp.where(lane == 1, m2 / den, 0.0))

    sel1 = lane == i1
    sel2 = lane == i2
    onehot = jnp.where(sel1, 1.0, 0.0) + jnp.where(sel2, 1.0, 0.0)
    r_i = lax.broadcasted_iota(jnp.int32, (tm, tm), 0)
    c_i = lax.broadcasted_iota(jnp.int32, (tm, tm), 1)
    tri = jnp.where(c_i < r_i, 1.0, 0.0).astype(BF16)
    before = jnp.dot(tri, onehot.astype(BF16), preferred_element_type=F32)
    cnt = jnp.sum(onehot, axis=0, keepdims=True)
    cnt_al = jnp.floor((cnt + (ROW_ALIGN - 1)) * (1.0 / ROW_ALIGN)) * ROW_ALIGN
    e_r = lax.broadcasted_iota(jnp.int32, (LANES, LANES), 0)
    e_c = lax.broadcasted_iota(jnp.int32, (LANES, LANES), 1)
    upper = jnp.where(e_r < e_c, 1.0, 0.0)
    loff = jnp.dot(jnp.broadcast_to(cnt_al, (8, LANES)), upper, precision=HIGHEST,
                   preferred_element_type=F32)[0:1, :]
    slot = loff + before
    slot1 = jnp.sum(jnp.where(sel1, slot, 0.0), axis=-1, keepdims=True).astype(jnp.int32)
    slot2 = jnp.sum(jnp.where(sel2, slot, 0.0), axis=-1, keepdims=True).astype(jnp.int32)
    ei = jnp.where(lane == 0, i1, jnp.where(lane == 1, i2, 0))
    ei_ref[...] = jnp.where(lane == 2, slot1, jnp.where(lane == 3, slot2, ei))

    r_idx = lax.broadcasted_iota(jnp.int32, (tm, srows), 1)
    place = jnp.where(r_idx == slot1, 1.0, jnp.where(r_idx == slot2, 1.0, 0.0)).astype(BF16)
    hs = lax.dot_general(place, h.astype(BF16), (((0,), (0,)), ((), ())), preferred_element_type=F32)
    half = hs.shape[1] // 2
    hs_ref[...] = _pack_bf16_pair(hs[:, :half], hs[:, half:])

    carry = carry_scr[0:1, :]
    mrow = lax.broadcasted_iota(jnp.int32, (8, LANES), 0)
    meta = jnp.where(mrow == 0, cnt_al, jnp.where(mrow == 1, carry, jnp.where(mrow == 2, loff, 0.0)))
    meta_ref[...] = meta.astype(jnp.int32)
    carry_scr[0:1, :] = carry + cnt_al
    tot_ref[...] = jnp.broadcast_to(carry + cnt_al, tot_ref.shape).astype(jnp.int32)


def _router(x, mod, g, rw_pad, rb_pad, seq):
    n, d = x.shape
    tm = TM_ROUTE
    tpb = seq // tm
    nt = n // tm
    return pl.pallas_call(
        functools.partial(_router_kernel, tm=tm, srows=SORT_ROWS),
        out_shape=(jax.ShapeDtypeStruct((nt * SORT_ROWS, d // 2), jnp.uint32),
                   jax.ShapeDtypeStruct((n, LANES), jnp.int32),
                   jax.ShapeDtypeStruct((n, LANES), F32),
                   jax.ShapeDtypeStruct((nt, 8, LANES), jnp.int32),
                   jax.ShapeDtypeStruct((8, LANES), jnp.int32)),
        grid=(nt,),
        in_specs=[
            pl.BlockSpec((tm, d), lambda i: (i, 0)),
            pl.BlockSpec((None, 6, d), lambda i: (i // tpb, 0, 0)),
            pl.BlockSpec((1, d), lambda i: (0, 0)),
            pl.BlockSpec((d, 2 * LANES), lambda i: (0, 0)),
            pl.BlockSpec((1, LANES), lambda i: (0, 0)),
        ],
        out_specs=(pl.BlockSpec((SORT_ROWS, d // 2), lambda i: (i, 0)),
                   pl.BlockSpec((tm, LANES), lambda i: (i, 0)),
                   pl.BlockSpec((tm, LANES), lambda i: (i, 0)),
                   pl.BlockSpec((None, 8, LANES), lambda i: (i, 0, 0)),
                   pl.BlockSpec((8, LANES), lambda i: (0, 0))),
        scratch_shapes=[pltpu.VMEM((8, LANES), F32)],
        compiler_params=_cparams("arbitrary"),
        name="router_top2",
    )(x, mod, g, rw_pad, rb_pad)


def _segment_copy(src_hbm, dst_hbm, src_row, dst_row, n_rows, sem):
    src_row = pl.multiple_of(src_row, ROW_ALIGN)
    dst_row = pl.multiple_of(dst_row, ROW_ALIGN)
    n_rows = pl.multiple_of(n_rows, ROW_ALIGN)
    return pltpu.make_async_copy(src_hbm.at[pl.ds(src_row, n_rows)], dst_hbm.at[pl.ds(dst_row, n_rows)], sem)


def _dispatch_kernel(src_ref, dst_ref, cnt_ref, hs_ref, xs_in_ref, xs_ref, sem, *, n_seg):
    del xs_in_ref

    def issue(s, total):
        n_rows = cnt_ref[s]

        @pl.when(n_rows > 0)
        def _():
            _segment_copy(hs_ref, xs_ref, src_ref[s], dst_ref[s], n_rows, sem).start()

        return total + n_rows

    total = lax.fori_loop(0, n_seg, issue, 0)

    @pl.when(total > 0)
    def _():
        _segment_copy(hs_ref, xs_ref, 0, 0, total, sem).wait()


def _dispatch(seg_src, seg_dst, seg_cnt, hs, xs_init):
    n_seg = seg_cnt.shape[0]
    return pl.pallas_call(
        functools.partial(_dispatch_kernel, n_seg=n_seg),
        out_shape=jax.ShapeDtypeStruct(xs_init.shape, xs_init.dtype),
        grid_spec=pltpu.PrefetchScalarGridSpec(
            num_scalar_prefetch=3,
            grid=(1,),
            in_specs=[pl.BlockSpec(memory_space=pl.ANY), pl.BlockSpec(memory_space=pl.ANY)],
            out_specs=pl.BlockSpec(memory_space=pl.ANY),
            scratch_shapes=[pltpu.SemaphoreType.DMA],
        ),
        input_output_aliases={4: 0},
        compiler_params=_cparams("arbitrary"),
        name="moe_dispatch",
    )(seg_src, seg_dst, seg_cnt, hs, xs_init)


def _expert_kernel(te_ref, used_ref, xs_ref, w1_ref, w3_ref, w2_ref, y_ref):
    t = pl.program_id(0)
    del te_ref

    @pl.when(used_ref[t] == 1)
    def _():
        lo, hi = _unpack_bf16_pair(xs_ref[...])
        h = jnp.concatenate([lo.astype(BF16), hi.astype(BF16)], axis=1)
        a = jnp.dot(h, w1_ref[...], preferred_element_type=F32)
        b = jnp.dot(h, w3_ref[...], preferred_element_type=F32)
        hid = ((a * _sigmoid(a)) * b).astype(BF16)
        acc = jnp.dot(hid, w2_ref[...], preferred_element_type=F32)
        half = acc.shape[1] // 2
        y_ref[...] = _pack_bf16_pair(acc[:, :half], acc[:, half:])

    @pl.when(used_ref[t] == 0)
    def _():
        y_ref[...] = jnp.zeros(y_ref.shape, y_ref.dtype)


def _expert_ffn(tile_e, tile_used, xs, w1, w3, w2):
    rows, half = xs.shape
    d = 2 * half
    dff = w1.shape[2]
    tg = TG_MOE

    def weight(shape):
        return pl.BlockSpec((None,) + shape, lambda t, te, us: (te[t], 0, 0), pipeline_mode=pl.Buffered(1))

    return pl.pallas_call(
        _expert_kernel,
        out_shape=jax.ShapeDtypeStruct((rows, half), jnp.uint32),
        grid_spec=pltpu.PrefetchScalarGridSpec(
            num_scalar_prefetch=2,
            grid=(rows // tg,),
            in_specs=[pl.BlockSpec((tg, half), lambda t, te, us: (t, 0)),
                      weight((d, dff)), weight((d, dff)), weight((dff, d))],
            out_specs=pl.BlockSpec((tg, half), lambda t, te, us: (t, 0)),
        ),
        compiler_params=_cparams("arbitrary"),
        name="expert_swiglu",
    )(tile_e, tile_used, xs, w1, w3, w2)


def _combine_kernel(src_ref, loff_ref, cnt_ref, x_ref, ei_ref, pw_ref, mod_ref, y_ref, o_ref, ybuf, sem,
                    *, tm, srows):
    i = pl.program_id(0)
    ybuf[...] = jnp.zeros(ybuf.shape, ybuf.dtype)
    total = 0
    for e in range(N_EXPERTS):
        s = i * N_EXPERTS + e
        n_rows = cnt_ref[s]

        @pl.when(n_rows > 0)
        def _():
            _segment_copy(y_ref, ybuf, src_ref[s], loff_ref[s], n_rows, sem).start()

        total = total + n_rows

    @pl.when(total > 0)
    def _():
        _segment_copy(y_ref, ybuf, 0, 0, total, sem).wait()

    lo, hi = _unpack_bf16_pair(ybuf[...])
    ys = jnp.concatenate([lo.astype(BF16), hi.astype(BF16)], axis=1)
    r_idx = lax.broadcasted_iota(jnp.int32, (tm, srows), 1)
    mix = jnp.zeros(x_ref.shape, F32)
    for k in range(TOP_K):
        pick = jnp.where(r_idx == ei_ref[:, TOP_K + k:TOP_K + k + 1], 1.0, 0.0).astype(BF16)
        mix = mix + pw_ref[:, k:k + 1] * jnp.dot(pick, ys, preferred_element_type=F32)
    o_ref[...] = x_ref[...] + mod_ref[5:6, :] * mix


def _combine(seg_src, seg_loff, seg_cnt, x, ei, pw, mod, y, seq):
    n, d = x.shape
    tm = TM_ROUTE
    tpb = seq // tm
    tok = lambda width: pl.BlockSpec((tm, width), lambda i, *_: (i, 0))
    return pl.pallas_call(
        functools.partial(_combine_kernel, tm=tm, srows=SORT_ROWS),
        out_shape=jax.ShapeDtypeStruct((n, d), F32),
        grid_spec=pltpu.PrefetchScalarGridSpec(
            num_scalar_prefetch=3,
            grid=(n // tm,),
            in_specs=[tok(d), tok(LANES), tok(LANES),
                      pl.BlockSpec((None, 6, d), lambda i, *_: (i // tpb, 0, 0)),
                      pl.BlockSpec(memory_space=pl.ANY)],
            out_specs=tok(d),
            scratch_shapes=[pltpu.VMEM((SORT_ROWS, d // 2), jnp.uint32), pltpu.SemaphoreType.DMA],
        ),
        compiler_params=_cparams("arbitrary"),
        name="moe_combine",
    )(seg_src, seg_loff, seg_cnt, x, ei, pw, mod, y)


def _pack_w_in(w_in):
    d = w_in.shape[0]
    w = MIX_W
    o_ckv = 3 * w + Q_LORA
    o_kr = o_ckv + KV_LORA
    o_su = o_kr + QK_ROPE
    o_rq = o_su + 2 * w
    o_gate = o_rq + 4 * w
    half = HEAD_DIM // 2
    perm = np.array([h * HEAD_DIM + part * half + i
                     for part in range(2) for h in range(N_GROUPS) for i in range(half)])
    kr = w_in[:, o_kr:o_kr + QK_ROPE]
    hr = QK_ROPE // 2
    z = lambda k: jnp.zeros((d, k), w_in.dtype)
    cols = [
        w_in[:, o_gate:o_gate + N_BRANCH * d],
        w_in[:, 0:3 * w + Q_LORA],
        w_in[:, o_rq:o_rq + w][:, perm], w_in[:, o_rq + w:o_rq + 2 * w][:, perm],
        w_in[:, o_rq + 2 * w:o_rq + 4 * w],
        w_in[:, o_su:o_su + 2 * w],
        w_in[:, o_ckv:o_ckv + KV_LORA],
        z(QK_NOPE), kr, z(LANES - QK_HEAD),
        z(QK_NOPE), kr[:, hr:], kr[:, :hr], z(LANES - QK_HEAD),
        z(N_IN - COL_KRB - LANES),
    ]
    return jnp.concatenate(cols, axis=1).astype(BF16)


def _swap_rope_halves(a):
    hr = QK_ROPE // 2
    return jnp.concatenate([a[..., :QK_NOPE], a[..., QK_NOPE + hr:QK_HEAD], a[..., QK_NOPE:QK_NOPE + hr],
                            a[..., QK_HEAD:]], axis=-1)


def _mla_params(cq_g, w_uq, ckv_g, w_ukv, qn_g, kn_g):
    pad = LANES - QK_HEAD
    wq = w_uq.reshape(Q_LORA, N_GROUPS, QK_HEAD)
    wq = jnp.pad(wq, ((0, 0), (0, 0), (0, pad)))
    wkv = w_ukv.reshape(KV_LORA, N_GROUPS, QK_NOPE + V_HEAD)
    wk = jnp.pad(wkv[:, :, :QK_NOPE], ((0, 0), (0, 0), (0, LANES - QK_NOPE)))
    wv = jnp.pad(wkv[:, :, QK_NOPE:], ((0, 0), (0, 0), (0, LANES - V_HEAD)))
    qg = jnp.pad(qn_g, (0, pad))[None, :]
    kg = jnp.pad(kn_g, (0, pad))[None, :]
    bound = (QK_HEAD ** 0.5 * LOG2_E) * jnp.max(jnp.abs(qn_g)) * jnp.max(jnp.abs(kn_g))
    static_shift = bound <= MAX_STATIC_SHIFT
    lane = jnp.arange(LANES)
    qaug = (lane == QK_HEAD).astype(F32)[None, :]
    kaug = qaug * jnp.where(static_shift, -bound, 0.0)
    vaug = jnp.tile((lane == V_HEAD).astype(F32), N_GROUPS)[None, :]
    params = {
        "cq_g": cq_g[None, :], "ckv_g": ckv_g[None, :],
        "wqa": wq.reshape(Q_LORA, -1).astype(BF16),
        "wqb": _swap_rope_halves(wq).reshape(Q_LORA, -1).astype(BF16),
        "wk": wk.reshape(KV_LORA, -1).astype(BF16),
        "wv": wv.reshape(KV_LORA, -1).astype(BF16),
        "qga": qg, "qgb": _swap_rope_halves(qg), "kga": kg, "kgb": _swap_rope_halves(kg),
        "qaug": qaug, "kaug": kaug, "vaug": vaug,
    }
    return params, static_shift


def _mixer_consts():
    h = jnp.arange(N_GROUPS, dtype=F32)
    log_gamma = jnp.log1p(-(2.0 ** (-5.0 - h)))
    pos = jnp.arange(CHUNK, dtype=F32)
    rel = pos[:, None] - pos[None, :]
    dec = jnp.where(rel >= 0, jnp.exp(log_gamma[:, None, None] * jnp.maximum(rel, 0.0)), 0.0)
    lane = np.arange(MIX_W)
    head_k = (lane % LANES) // (HEAD_DIM // 2)
    head_v = lane // HEAD_DIM
    lg_k = log_gamma[head_k]
    return {
        "dec": dec.reshape(N_GROUPS * CHUNK, CHUNK),
        "kdec": jnp.exp(lg_k[None, :] * (CHUNK - 1.0 - pos)[:, None]),
        "qdec": jnp.exp(lg_k[None, :] * (pos + 1.0)[:, None]),
        "cdec": jnp.broadcast_to(jnp.exp(lg_k * CHUNK)[:, None], (MIX_W, MIX_W)),
        "bd": jnp.asarray((head_k[:, None] == head_v[None, :]).astype(np.float32)),
        "gmat": jnp.asarray((head_v[:, None] == head_v[None, :]).astype(np.float32) / HEAD_DIM).astype(BF16),
        "mk": jnp.asarray((head_k[None, :] == np.arange(N_GROUPS)[:, None]).astype(np.float32)),
        "mv": jnp.asarray((head_v[None, :] == np.arange(N_GROUPS)[:, None]).astype(np.float32)),
    }


def _mixer_params(conv_w, gv_g, w_s, b_s, ret_g, w_branch, w_o):
    p = dict(_mixer_consts())
    ws = jnp.tril(w_s)
    p.update({
        "conv_w": conv_w,
        "gv_g": gv_g.reshape(1, MIX_W),
        "ws_cat": jnp.transpose(ws, (1, 0, 2)).reshape(CHUNK, N_GROUPS * CHUNK).astype(BF16),
        "bs_mat": jnp.repeat(b_s.T, HEAD_DIM, axis=1),
        "ret_g": ret_g.reshape(1, MIX_W),
        "w_branch": w_branch.astype(BF16),
        "w_o": w_o.astype(BF16),
    })
    return p


def _moe_layout(meta, tot, n_tiles):
    totals = tot[0, :N_EXPERTS]
    padded = ((totals + TG_MOE - 1) // TG_MOE) * TG_MOE
    ends = jnp.cumsum(padded)
    starts = ends - padded
    nt = meta.shape[0]
    seg_cnt = meta[:, 0, :N_EXPERTS]
    seg_loff = meta[:, 2, :N_EXPERTS]
    seg_grouped = starts[None, :] + meta[:, 1, :N_EXPERTS]
    seg_sorted = jnp.arange(nt, dtype=jnp.int32)[:, None] * SORT_ROWS + seg_loff
    tile_start = jnp.arange(n_tiles, dtype=jnp.int32) * TG_MOE
    tile_e = jnp.sum((tile_start[:, None] >= ends[None, :]).astype(jnp.int32), axis=1)
    used = (tile_start < ends[-1]).astype(jnp.int32)
    last_e = jnp.sum((ends[-1] - 1 >= ends).astype(jnp.int32))
    tile_e = jnp.minimum(jnp.where(used == 1, tile_e, last_e), N_EXPERTS - 1)
    flat = lambda a: a.reshape(-1).astype(jnp.int32)
    return flat(seg_sorted), flat(seg_grouped), flat(seg_loff), flat(seg_cnt), tile_e, used


def kernel(x, c, positions, norm1_g, norm2_g, ada_w, ada_b, w_in, conv_w, cq_g, w_uq, ckv_g, w_ukv, qn_g, kn_g, gv_g, w_s, b_s, ret_g, w_branch, w_o, ffn_w1, ffn_w3, ffn_w2, router_w, router_b, moe_w1, moe_w3, moe_w2):
    batch, seq, d = x.shape
    depth = ada_w.shape[0]
    n = batch * seq
    assert seq % max(TM_PROJ, TM_PREP, TQ_ATT, TM_MIX, TM_FFN, TM_ROUTE) == 0
    assert d // 2 % LANES == 0

    c_pad = jnp.pad(c, ((0, 8 - batch), (0, 0)))
    ada = _ada(c_pad, ada_w, ada_b)[:, :batch].reshape(depth, batch, 6, d)
    cosr, sinr, cm, sm = _rope_tables(positions.astype(F32).reshape(n, 1))

    xt = x.reshape(n, d)
    for l in range(depth):
        mod = ada[l]
        proj = _inproj(xt, mod, norm1_g[l][None, :], _pack_w_in(w_in[l]), seq)
        mla_p, static_shift = _mla_params(cq_g[l], w_uq[l], ckv_g[l], w_ukv[l], qn_g[l], kn_g[l])
        q, k, v = _mla_prep(proj, cm, sm, mla_p)
        y_mla = lax.cond(static_shift,
                         functools.partial(_flash, batch=batch, seq=seq, online_max=False),
                         functools.partial(_flash, batch=batch, seq=seq, online_max=True), q, k, v)
        mp = _mixer_params(conv_w[l], gv_g[l], w_s[l], b_s[l], ret_g[l], w_branch[l], w_o[l])
        xt = _mixers(proj, y_mla, xt, cosr, sinr, mod, mp, seq)
        g2n = norm2_g[l][None, :]
        if l % 2 == 0:
            i = l // 2
            xt = _dense_ffn(xt, mod, g2n, ffn_w1[i].astype(BF16), ffn_w3[i].astype(BF16),
                            ffn_w2[i].astype(BF16), seq)
        else:
            i = l // 2
            rw = jnp.pad(router_w[i], ((0, 0), (0, LANES - N_EXPERTS)))
            rw_hi = rw.astype(BF16)
            rw_pad = jnp.concatenate([rw_hi, (rw - rw_hi.astype(F32)).astype(BF16)], axis=1)
            rb_pad = jnp.pad(router_b[i], (0, LANES - N_EXPERTS), constant_values=-1e30)[None, :]
            hs, ei, pw, meta, tot = _router(xt, mod, g2n, rw_pad, rb_pad, seq)
            max_rows = n * TOP_K + N_EXPERTS * (n // TM_ROUTE) * (ROW_ALIGN - 1)
            n_tiles = -(-max_rows // TG_MOE) + N_EXPERTS
            seg_sorted, seg_grouped, seg_loff, seg_cnt, tile_e, used = _moe_layout(meta, tot, n_tiles)
            xs = _dispatch(seg_sorted, seg_grouped, seg_cnt, hs,
                           jnp.zeros((n_tiles * TG_MOE, d // 2), jnp.uint32))
            y = _expert_ffn(tile_e, used, xs, moe_w1[i].astype(BF16), moe_w3[i].astype(BF16),
                            moe_w2[i].astype(BF16))
            xt = _combine(seg_grouped, seg_loff, seg_cnt, xt, ei, pw, mod, y, seq)
    return xt.reshape(batch, seq, d)
```

```python
import functools

import jax
import jax.numpy as jnp
import numpy as np
from jax import lax
from jax.experimental import pallas as pl
from jax.experimental.pallas import tpu as pltpu

F32 = jnp.float32
BF16 = jnp.bfloat16
HIGHEST = lax.Precision.HIGHEST

HEAD_DIM = 64
N_GROUPS = 4
MIX_W = N_GROUPS * HEAD_DIM
N_BRANCH = 4
CONV_W = 3
Q_LORA = 256
KV_LORA = 128
QK_NOPE = 64
QK_ROPE = 32
QK_HEAD = QK_NOPE + QK_ROPE
V_HEAD = 64
CHUNK = 128
N_EXPERTS = 8
TOP_K = 2
ROPE_THETA = 10000.0
EPS = 1e-6
LOG2_E = 1.4426950408889634
MAX_STATIC_SHIFT = 50.0

LANES = 128
VMEM_LIMIT_BYTES = 56 * 1024 * 1024

COL_GATES = 0
COL_A = 4096
COL_CQ = COL_A + 3 * MIX_W
COL_R = 5120
COL_SU = 6144
COL_CKV = 6656
COL_KRA = 6784
COL_KRB = 6912
N_IN = 7168

TM_PROJ = 512
TN_PROJ = 1024
TM_PREP = 1024
TQ_ATT = 1024
TM_MIX = 512
TM_FFN = 512
TM_ROUTE = 512
ROW_ALIGN = 8
SORT_ROWS = TOP_K * TM_ROUTE + N_EXPERTS * ROW_ALIGN
TG_MOE = 512
DISPATCH_SLOTS = 4
DISPATCH_AHEAD = 2


def _cparams(*sem):
    return pltpu.CompilerParams(dimension_semantics=sem, vmem_limit_bytes=VMEM_LIMIT_BYTES)


def _sigmoid(x):
    return jnp.tanh(x * 0.5) * 0.5 + 0.5


def _group_mean(x, gmat_bf16):
    hi = x.astype(BF16)
    lo = (x - hi.astype(F32)).astype(BF16)
    return (jnp.dot(hi, gmat_bf16, preferred_element_type=F32)
            + jnp.dot(lo, gmat_bf16, preferred_element_type=F32))


def _pack_bf16_pair(lo, hi):
    lo_bits = lax.bitcast_convert_type(lo.astype(BF16).astype(F32), jnp.uint32)
    hi_bits = lax.bitcast_convert_type(hi.astype(BF16).astype(F32), jnp.uint32)
    return (lo_bits >> 16) | (hi_bits & jnp.uint32(0xFFFF0000))


def _unpack_bf16_pair(p):
    lo = lax.bitcast_convert_type(p << 16, F32)
    hi = lax.bitcast_convert_type(p & jnp.uint32(0xFFFF0000), F32)
    return lo, hi


def _norm_mod(x, g, shift, scale):
    y = x * lax.rsqrt(jnp.mean(x * x, axis=-1, keepdims=True) + EPS)
    return (y * g) * (1.0 + scale) + shift


def _ada_kernel(c_ref, w_ref, b_ref, o_ref):
    c = c_ref[...]
    cond = c * _sigmoid(c)
    o_ref[...] = jnp.dot(cond, w_ref[...], precision=HIGHEST, preferred_element_type=F32) + b_ref[...]


def _ada(c_pad, ada_w, ada_b):
    n_layer, d, d6 = ada_w.shape
    rows = c_pad.shape[0]
    tn = 1024
    return pl.pallas_call(
        _ada_kernel,
        out_shape=jax.ShapeDtypeStruct((n_layer, rows, d6), F32),
        grid=(n_layer, d6 // tn),
        in_specs=[
            pl.BlockSpec((rows, d), lambda l, j: (0, 0)),
            pl.BlockSpec((None, d, tn), lambda l, j: (l, 0, j)),
            pl.BlockSpec((None, 1, tn), lambda l, j: (l, 0, j)),
        ],
        out_specs=pl.BlockSpec((None, rows, tn), lambda l, j: (l, 0, j)),
        compiler_params=_cparams("parallel", "parallel"),
        name="ada_mod",
    )(c_pad, ada_w, ada_b.reshape(n_layer, 1, d6))


def _rope_kernel(pos_ref, invr_ref, invm_ref, signm_ref, cr_ref, sr_ref, cm_ref, sm_ref):
    pos = pos_ref[...]
    ang_r = pos * invr_ref[...]
    cr_ref[...] = jnp.cos(ang_r)
    sr_ref[...] = jnp.sin(ang_r)
    ang_m = pos * invm_ref[...]
    cm_ref[...] = jnp.cos(ang_m)
    sm_ref[...] = jnp.sin(ang_m) * signm_ref[...]


def _rope_tables(pos_f):
    n = pos_f.shape[0]
    tm = 1024
    half_r = HEAD_DIM // 2
    inv_r = ROPE_THETA ** (-jnp.arange(half_r, dtype=F32) / half_r)
    inv_r = jnp.tile(inv_r, LANES // half_r)[None, :]
    half_m = QK_ROPE // 2
    inv_m1 = ROPE_THETA ** (-jnp.arange(half_m, dtype=F32) / half_m)
    zeros = lambda k: jnp.zeros((k,), F32)
    inv_m = jnp.concatenate([zeros(QK_NOPE), inv_m1, inv_m1, zeros(LANES - QK_HEAD)])[None, :]
    sign_m = jnp.concatenate([zeros(QK_NOPE), -jnp.ones((half_m,), F32), jnp.ones((half_m,), F32),
                              zeros(LANES - QK_HEAD)])[None, :]
    row = pl.BlockSpec((1, LANES), lambda i: (0, 0))
    tab = pl.BlockSpec((tm, LANES), lambda i: (i, 0))
    shape = jax.ShapeDtypeStruct((n, LANES), F32)
    return pl.pallas_call(
        _rope_kernel,
        out_shape=(shape, shape, shape, shape),
        grid=(n // tm,),
        in_specs=[pl.BlockSpec((tm, 1), lambda i: (i, 0)), row, row, row],
        out_specs=(tab, tab, tab, tab),
        compiler_params=_cparams("parallel"),
        name="rope_tables",
    )(pos_f, inv_r, inv_m, sign_m)


def _inproj_kernel(x_ref, mod_ref, g_ref, w_ref, o_ref):
    h = _norm_mod(x_ref[...], g_ref[...], mod_ref[0:1, :], mod_ref[1:2, :]).astype(BF16)
    for c in range(N_IN // TN_PROJ):
        cols = slice(c * TN_PROJ, (c + 1) * TN_PROJ)
        o_ref[:, cols] = jnp.dot(h, w_ref[:, cols], preferred_element_type=F32).astype(BF16)


def _resident(shape):
    return pl.BlockSpec(shape, lambda *_: (0,) * len(shape), pipeline_mode=pl.Buffered(1))


def _inproj(x, mod, g, w, seq):
    n, d = x.shape
    tm = TM_PROJ
    tpb = seq // tm
    return pl.pallas_call(
        _inproj_kernel,
        out_shape=jax.ShapeDtypeStruct((n, N_IN), BF16),
        grid=(n // tm,),
        in_specs=[
            pl.BlockSpec((tm, d), lambda i: (i, 0)),
            pl.BlockSpec((None, 6, d), lambda i: (i // tpb, 0, 0)),
            pl.BlockSpec((1, d), lambda i: (0, 0)),
            _resident((d, N_IN)),
        ],
        out_specs=pl.BlockSpec((tm, N_IN), lambda i: (i, 0)),
        compiler_params=_cparams("parallel"),
        name="in_proj",
    )(x, mod, g, w)


def _mla_prep_kernel(cq_ref, ckv_ref, kra_ref, krb_ref, cm_ref, sm_ref, cqg_ref, wqa_ref, wqb_ref,
                     ckvg_ref, wk_ref, wv_ref, qga_ref, qgb_ref, kga_ref, kgb_ref,
                     qaug_ref, kaug_ref, vaug_ref, q_ref, k_ref, v_ref):
    cq = cq_ref[...].astype(F32)
    cqn = (cq * lax.rsqrt(jnp.mean(cq * cq, axis=-1, keepdims=True) + EPS) * cqg_ref[...]).astype(BF16)
    qa = jnp.dot(cqn, wqa_ref[...], preferred_element_type=F32)
    qb = jnp.dot(cqn, wqb_ref[...], preferred_element_type=F32)
    ckv = ckv_ref[...].astype(F32)
    ckvn = (ckv * lax.rsqrt(jnp.mean(ckv * ckv, axis=-1, keepdims=True) + EPS) * ckvg_ref[...]).astype(BF16)
    ka = jnp.dot(ckvn, wk_ref[...], preferred_element_type=F32)
    v_ref[...] = (jnp.dot(ckvn, wv_ref[...], preferred_element_type=F32) + vaug_ref[...]).astype(BF16)
    kra = kra_ref[...].astype(F32)
    krb = krb_ref[...].astype(F32)
    cm = cm_ref[...]
    sm = sm_ref[...]
    scale = QK_HEAD ** -0.5 * LOG2_E
    for h in range(N_GROUPS):
        sl = slice(h * LANES, (h + 1) * LANES)
        qah, qbh = qa[:, sl], qb[:, sl]
        r = lax.rsqrt(jnp.sum(qah * qah, axis=-1, keepdims=True) * (1.0 / QK_HEAD) + EPS)
        q_rot = (qah * r) * qga_ref[...] * cm + (qbh * r) * qgb_ref[...] * sm
        q_ref[:, sl] = (q_rot * scale + qaug_ref[...]).astype(BF16)
        kah = ka[:, sl] + kra
        kbh = ka[:, sl] + krb
        r = lax.rsqrt(jnp.sum(kah * kah, axis=-1, keepdims=True) * (1.0 / QK_HEAD) + EPS)
        k_rot = (kah * r) * kga_ref[...] * cm + (kbh * r) * kgb_ref[...] * sm
        k_ref[:, sl] = (k_rot + kaug_ref[...]).astype(BF16)


def _mla_prep(proj, cm, sm, p):
    n = proj.shape[0]
    tm = TM_PREP
    hw = N_GROUPS * LANES

    def col(width, offset):
        return pl.BlockSpec((tm, width), lambda i: (i, offset // width))

    def full(a):
        return pl.BlockSpec(a.shape, lambda i: (0,) * a.ndim)

    weights = [p["cq_g"], p["wqa"], p["wqb"], p["ckv_g"], p["wk"], p["wv"],
               p["qga"], p["qgb"], p["kga"], p["kgb"], p["qaug"], p["kaug"], p["vaug"]]
    head_tile = pl.BlockSpec((tm, hw), lambda i: (i, 0))
    out = jax.ShapeDtypeStruct((n, hw), BF16)
    return pl.pallas_call(
        _mla_prep_kernel,
        out_shape=(out, out, out),
        grid=(n // tm,),
        in_specs=[col(Q_LORA, COL_CQ), col(KV_LORA, COL_CKV), col(LANES, COL_KRA), col(LANES, COL_KRB),
                  pl.BlockSpec((tm, LANES), lambda i: (i, 0)), pl.BlockSpec((tm, LANES), lambda i: (i, 0))]
                 + [full(w) for w in weights],
        out_specs=(head_tile, head_tile, head_tile),
        compiler_params=_cparams("parallel"),
        name="mla_prep",
    )(proj, proj, proj, proj, cm, sm, *weights)


def _flash_kernel(q_ref, k_ref, v_ref, o_ref, acc_scr, *rest, tq, online_max):
    i = pl.program_id(1)
    j = pl.program_id(2)

    @pl.when(j == 0)
    def _():
        acc_scr[...] = jnp.zeros(acc_scr.shape, F32)
        if online_max:
            rest[0][...] = jnp.full(rest[0].shape, -jnp.inf, F32)

    def step(masked):
        if masked:
            row = lax.broadcasted_iota(jnp.int32, (tq, tq), 0)
            col = lax.broadcasted_iota(jnp.int32, (tq, tq), 1)
            keep = col <= row
        for h in range(N_GROUPS):
            sl = slice(h * LANES, (h + 1) * LANES)
            s = lax.dot_general(q_ref[:, sl], k_ref[:, sl], (((1,), (1,)), ((), ())),
                                preferred_element_type=F32)
            if masked:
                s = jnp.where(keep, s, -jnp.inf)
            if online_max:
                m_scr = rest[0]
                m_prev = m_scr[h]
                m_new = jnp.maximum(m_prev, jnp.max(s, axis=-1, keepdims=True))
                p = jnp.exp2(s - m_new).astype(BF16)
                acc_scr[h] = jnp.exp2(m_prev - m_new) * acc_scr[h] + jnp.dot(
                    p, v_ref[:, sl], preferred_element_type=F32)
                m_scr[h] = m_new
            else:
                acc_scr[h] += jnp.dot(jnp.exp2(s).astype(BF16), v_ref[:, sl], preferred_element_type=F32)

    @pl.when(j < i)
    def _():
        step(False)

    @pl.when(j == i)
    def _():
        step(True)
        lane = lax.broadcasted_iota(jnp.int32, (tq, LANES), 1)
        for pr in range(N_GROUPS // 2):
            lo = acc_scr[2 * pr]
            hi = acc_scr[2 * pr + 1]
            lo = lo / lo[:, V_HEAD:V_HEAD + 1]
            hi = hi / hi[:, V_HEAD:V_HEAD + 1]
            both = jnp.where(lane < V_HEAD, lo, pltpu.roll(hi, V_HEAD, axis=1))
            o_ref[:, pr * LANES:(pr + 1) * LANES] = both.astype(BF16)


def _flash(q, k, v, batch, seq, online_max):
    n = q.shape[0]
    tq = TQ_ATT
    nq = seq // tq
    hw = N_GROUPS * LANES
    scratch = [pltpu.VMEM((N_GROUPS, tq, LANES), F32)]
    if online_max:
        scratch.append(pltpu.VMEM((N_GROUPS, tq, 1), F32))
    return pl.pallas_call(
        functools.partial(_flash_kernel, tq=tq, online_max=online_max),
        out_shape=jax.ShapeDtypeStruct((n, MIX_W), BF16),
        grid=(batch, nq, nq),
        in_specs=[
            pl.BlockSpec((tq, hw), lambda b, i, j: (b * nq + i, 0)),
            pl.BlockSpec((tq, hw), lambda b, i, j: (b * nq + jnp.minimum(j, i), 0)),
            pl.BlockSpec((tq, hw), lambda b, i, j: (b * nq + jnp.minimum(j, i), 0)),
        ],
        out_specs=pl.BlockSpec((tq, MIX_W), lambda b, i, j: (b * nq + i, 0)),
        scratch_shapes=scratch,
        compiler_params=_cparams("parallel", "parallel", "arbitrary"),
        name="mla_flash_online" if online_max else "mla_flash",
    )(q, k, v)


def _gelu_tanh(x):
    return jax.nn.gelu(x, approximate=True)


def _mix_kernel(gates_ref, a_ref, r_ref, su_ref, ymla_ref, x_ref, cos_ref, sin_ref, mod_ref,
                convw_ref, gvg_ref, wscat_ref, bsmat_ref, retg_ref, dec_ref, kdec_ref, qdec_ref,
                cdec_ref, bd_ref, gmat_ref, mk_ref, mv_ref, wb_ref, wo_ref,
                o_ref, carry_scr, state_scr, ysg_scr, yret_scr, *, tm, tpb):
    i = pl.program_id(0)

    @pl.when(i % tpb == 0)
    def _():
        carry_scr[...] = jnp.zeros(carry_scr.shape, F32)
        state_scr[...] = jnp.zeros(state_scr.shape, F32)

    w = MIX_W
    a_b = a_ref[:, 0:w].astype(F32)
    u = a_ref[:, w:2 * w].astype(F32) * a_ref[:, 2 * w:3 * w].astype(F32)
    rowi = lax.broadcasted_iota(jnp.int32, (tm, w), 0)
    prev1 = carry_scr[0:1, :]
    prev2 = carry_scr[1:2, :]
    u1 = jnp.where(rowi == 0, prev1, pltpu.roll(u, 1, axis=0))
    u2 = jnp.where(rowi == 0, prev2, jnp.where(rowi == 1, prev1, pltpu.roll(u, 2, axis=0)))
    carry_scr[0:1, :] = u[tm - 1:tm, :]
    carry_scr[1:2, :] = u[tm - 2:tm - 1, :]
    y_conv = a_b * (convw_ref[0:1, :] * u2 + convw_ref[1:2, :] * u1 + convw_ref[2:3, :] * u)

    gmat = gmat_ref[...]
    s_u = _gelu_tanh(su_ref[:, 0:w].astype(F32))
    s_v = _gelu_tanh(su_ref[:, w:2 * w].astype(F32))
    ms = _group_mean(s_v * s_v, gmat)
    vn = (s_v * lax.rsqrt(ms + EPS) * gvg_ref[...]).astype(BF16)

    cosr = cos_ref[...]
    sinr = sin_ref[...]

    def rot(t):
        t1, t2 = t[:, 0:LANES], t[:, LANES:2 * LANES]
        return jnp.concatenate([t1 * cosr - t2 * sinr, t2 * cosr + t1 * sinr], axis=-1)

    rq = rot(r_ref[:, 0:w].astype(F32))
    rk = rot(r_ref[:, w:2 * w].astype(F32)) * (HEAD_DIM ** -0.5)

    for c in range(tm // CHUNK):
        rows = slice(c * CHUNK, (c + 1) * CHUNK)
        vc = vn[rows, :]
        vbd = jnp.concatenate([vc * mv_ref[g:g + 1, :].astype(BF16) for g in range(N_GROUPS)], axis=0)
        mixed = jnp.dot(wscat_ref[...], vbd, preferred_element_type=F32) + bsmat_ref[...]
        ysg_scr[rows, :] = s_u[rows, :] * mixed

        qc = rq[rows, :]
        kc = rk[rows, :]
        kcb = kc.astype(BF16)
        vcb = r_ref[rows, 2 * w:3 * w]
        qstack = jnp.concatenate([(qc * mk_ref[h:h + 1, :]).astype(BF16) for h in range(N_GROUPS)], axis=0)
        sc = lax.dot_general(qstack, kcb, (((1,), (1,)), ((), ())), preferred_element_type=F32)
        sc = (sc * dec_ref[...]).astype(BF16)
        scat = jnp.concatenate([sc[h * CHUNK:(h + 1) * CHUNK, :] for h in range(N_GROUPS)], axis=1)
        vstack = jnp.concatenate([vcb * mv_ref[h:h + 1, :].astype(BF16) for h in range(N_GROUPS)], axis=0)
        o_c = jnp.dot(scat, vstack, preferred_element_type=F32)
        state = state_scr[...]
        o_c = o_c + jnp.dot((qc * qdec_ref[...]).astype(BF16), state.astype(BF16),
                            preferred_element_type=F32)
        kd_t = jnp.transpose(kc * kdec_ref[...]).astype(BF16)
        kv = jnp.dot(kd_t, vcb, preferred_element_type=F32)
        state_scr[...] = state * cdec_ref[...] + kv * bd_ref[...]
        yret_scr[rows, :] = o_c

    o_all = yret_scr[...]
    xc = o_all - _group_mean(o_all, gmat)
    var = _group_mean(xc * xc, gmat)
    r_g = r_ref[:, 3 * w:4 * w].astype(F32)
    y_ret = (r_g * _sigmoid(r_g)) * (xc * lax.rsqrt(var + EPS) * retg_ref[...])

    d = x_ref.shape[1]
    ys = (y_conv, ymla_ref[...], ysg_scr[...], y_ret)
    merged = jnp.zeros((tm, d), F32)
    for n in range(N_BRANCH):
        gate = _sigmoid(gates_ref[:, n * d:(n + 1) * d]).astype(F32)
        merged = merged + gate * jnp.dot(ys[n].astype(BF16), wb_ref[n], preferred_element_type=F32)
    out = jnp.dot(merged.astype(BF16), wo_ref[...], preferred_element_type=F32)
    o_ref[...] = x_ref[...] + mod_ref[2:3, :] * out


def _mixers(proj, ymla, x, cosr, sinr, mod, p, seq):
    n, d = x.shape
    tm = TM_MIX
    tpb = seq // tm

    def col(width, offset):
        return pl.BlockSpec((tm, width), lambda i: (i, offset // width))

    def full(a):
        return pl.BlockSpec(a.shape, lambda i: (0,) * a.ndim)

    consts = [p["conv_w"], p["gv_g"], p["ws_cat"], p["bs_mat"], p["ret_g"], p["dec"], p["kdec"],
              p["qdec"], p["cdec"], p["bd"], p["gmat"], p["mk"], p["mv"], p["w_branch"], p["w_o"]]
    return pl.pallas_call(
        functools.partial(_mix_kernel, tm=tm, tpb=tpb),
        out_shape=jax.ShapeDtypeStruct((n, d), F32),
        grid=(n // tm,),
        in_specs=[col(N_BRANCH * d, COL_GATES), col(4 * MIX_W, COL_A), col(4 * MIX_W, COL_R),
                  col(2 * MIX_W, COL_SU),
                  pl.BlockSpec((tm, MIX_W), lambda i: (i, 0)),
                  pl.BlockSpec((tm, d), lambda i: (i, 0)),
                  pl.BlockSpec((tm, LANES), lambda i: (i, 0)),
                  pl.BlockSpec((tm, LANES), lambda i: (i, 0)),
                  pl.BlockSpec((None, 6, d), lambda i: (i // tpb, 0, 0))]
                 + [full(c) for c in consts],
        out_specs=pl.BlockSpec((tm, d), lambda i: (i, 0)),
        scratch_shapes=[pltpu.VMEM((8, MIX_W), F32), pltpu.VMEM((MIX_W, MIX_W), F32),
                        pltpu.VMEM((tm, MIX_W), F32), pltpu.VMEM((tm, MIX_W), F32)],
        compiler_params=_cparams("arbitrary"),
        name="mixers_merge",
    )(proj, proj, proj, proj, ymla, x, cosr, sinr, mod, *consts)


def _ffn_kernel(x_ref, mod_ref, g_ref, w1_ref, w3_ref, w2_ref, o_ref):
    x = x_ref[...]
    h = _norm_mod(x, g_ref[...], mod_ref[3:4, :], mod_ref[4:5, :]).astype(BF16)
    a = jnp.dot(h, w1_ref[...], preferred_element_type=F32)
    b = jnp.dot(h, w3_ref[...], preferred_element_type=F32)
    hid = ((a * _sigmoid(a)) * b).astype(BF16)
    o_ref[...] = x + mod_ref[5:6, :] * jnp.dot(hid, w2_ref[...], preferred_element_type=F32)


def _dense_ffn(x, mod, g, w1, w3, w2, seq):
    n, d = x.shape
    dff = w1.shape[1]
    tm = TM_FFN
    tpb = seq // tm
    return pl.pallas_call(
        _ffn_kernel,
        out_shape=jax.ShapeDtypeStruct((n, d), F32),
        grid=(n // tm,),
        in_specs=[
            pl.BlockSpec((tm, d), lambda i: (i, 0)),
            pl.BlockSpec((None, 6, d), lambda i: (i // tpb, 0, 0)),
            pl.BlockSpec((1, d), lambda i: (0, 0)),
            _resident((d, dff)), _resident((d, dff)), _resident((dff, d)),
        ],
        out_specs=pl.BlockSpec((tm, d), lambda i: (i, 0)),
        compiler_params=_cparams("parallel"),
        name="dense_swiglu",
    )(x, mod, g, w1, w3, w2)


def _router_kernel(x_ref, mod_ref, g_ref, rw_ref, rb_ref, hs_ref, ei_ref, pw_ref, meta_ref, tot_ref,
                   carry_scr, *, tm, srows):
    i = pl.program_id(0)

    @pl.when(i == 0)
    def _():
        carry_scr[...] = jnp.zeros(carry_scr.shape, F32)

    h = _norm_mod(x_ref[...], g_ref[...], mod_ref[3:4, :], mod_ref[4:5, :])

    h_hi = h.astype(BF16)
    h_lo = (h - h_hi.astype(F32)).astype(BF16)
    hw = jnp.dot(h_hi, rw_ref[...], preferred_element_type=F32)
    logits = (hw[:, :LANES] + hw[:, LANES:] + jnp.dot(h_lo, rw_ref[:, :LANES], preferred_element_type=F32)
              + rb_ref[...])
    mx = jnp.max(logits, axis=-1, keepdims=True)
    ex = jnp.exp(logits - mx)
    probs = ex / jnp.sum(ex, axis=-1, keepdims=True)
    lane = lax.broadcasted_iota(jnp.int32, (tm, LANES), 1)
    valid = lane < N_EXPERTS
    probs = jnp.where(valid, probs, -1.0)
    m1 = jnp.max(probs, axis=-1, keepdims=True)
    i1 = jnp.min(jnp.where(probs == m1, lane, LANES), axis=-1, keepdims=True)
    rest = jnp.where(lane == i1, -1.0, probs)
    m2 = jnp.max(rest, axis=-1, keepdims=True)
    i2 = jnp.min(jnp.where(rest == m2, lane, LANES), axis=-1, keepdims=True)
    den = m1 + m2
    pw_ref[...] = jnp.where(lane == 0, m1 / den, jnp.where(lane == 1, m2 / den, 0.0))

    sel1 = lane == i1
    sel2 = lane == i2
    onehot = jnp.where(sel1, 1.0, 0.0) + jnp.where(sel2, 1.0, 0.0)
    r_i = lax.broadcasted_iota(jnp.int32, (tm, tm), 0)
    c_i = lax.broadcasted_iota(jnp.int32, (tm, tm), 1)
    tri = jnp.where(c_i < r_i, 1.0, 0.0).astype(BF16)
    before = jnp.dot(tri, onehot.astype(BF16), preferred_element_type=F32)
    cnt = jnp.sum(onehot, axis=0, keepdims=True)
    cnt_al = jnp.floor((cnt + (ROW_ALIGN - 1)) * (1.0 / ROW_ALIGN)) * ROW_ALIGN
    e_r = lax.broadcasted_iota(jnp.int32, (LANES, LANES), 0)
    e_c = lax.broadcasted_iota(jnp.int32, (LANES, LANES), 1)
    upper = jnp.where(e_r < e_c, 1.0, 0.0)
    loff = jnp.dot(jnp.broadcast_to(cnt_al, (8, LANES)), upper, precision=HIGHEST,
                   preferred_element_type=F32)[0:1, :]
    slot = loff + before
    slot1 = jnp.sum(jnp.where(sel1, slot, 0.0), axis=-1, keepdims=True).astype(jnp.int32)
    slot2 = jnp.sum(jnp.where(sel2, slot, 0.0), axis=-1, keepdims=True).astype(jnp.int32)
    ei = jnp.where(lane == 0, i1, jnp.where(lane == 1, i2, 0))
    ei_ref[...] = jnp.where(lane == 2, slot1, jnp.where(lane == 3, slot2, ei))

    r_idx = lax.broadcasted_iota(jnp.int32, (tm, srows), 1)
    place = jnp.where(r_idx == slot1, 1.0, jnp.where(r_idx == slot2, 1.0, 0.0)).astype(BF16)
    hs = lax.dot_general(place, h.astype(BF16), (((0,), (0,)), ((), ())), preferred_element_type=F32)
    half = hs.shape[1] // 2
    hs_ref[...] = _pack_bf16_pair(hs[:, :half], hs[:, half:])

    carry = carry_scr[0:1, :]
    mrow = lax.broadcasted_iota(jnp.int32, (8, LANES), 0)
    meta = jnp.where(mrow == 0, cnt_al, jnp.where(mrow == 1, carry, jnp.where(mrow == 2, loff, 0.0)))
    meta_ref[...] = meta.astype(jnp.int32)
    carry_scr[0:1, :] = carry + cnt_al
    tot_ref[...] = jnp.broadcast_to(carry + cnt_al, tot_ref.shape).astype(jnp.int32)


def _router(x, mod, g, rw_pad, rb_pad, seq):
    n, d = x.shape
    tm = TM_ROUTE
    tpb = seq // tm
    nt = n // tm
    return pl.pallas_call(
        functools.partial(_router_kernel, tm=tm, srows=SORT_ROWS),
        out_shape=(jax.ShapeDtypeStruct((nt * SORT_ROWS, d // 2), jnp.uint32),
                   jax.ShapeDtypeStruct((n, LANES), jnp.int32),
                   jax.ShapeDtypeStruct((n, LANES), F32),
                   jax.ShapeDtypeStruct((nt, 8, LANES), jnp.int32),
                   jax.ShapeDtypeStruct((8, LANES), jnp.int32)),
        grid=(nt,),
        in_specs=[
            pl.BlockSpec((tm, d), lambda i: (i, 0)),
            pl.BlockSpec((None, 6, d), lambda i: (i // tpb, 0, 0)),
            pl.BlockSpec((1, d), lambda i: (0, 0)),
            pl.BlockSpec((d, 2 * LANES), lambda i: (0, 0)),
            pl.BlockSpec((1, LANES), lambda i: (0, 0)),
        ],
        out_specs=(pl.BlockSpec((SORT_ROWS, d // 2), lambda i: (i, 0)),
                   pl.BlockSpec((tm, LANES), lambda i: (i, 0)),
                   pl.BlockSpec((tm, LANES), lambda i: (i, 0)),
                   pl.BlockSpec((None, 8, LANES), lambda i: (i, 0, 0)),
                   pl.BlockSpec((8, LANES), lambda i: (0, 0))),
        scratch_shapes=[pltpu.VMEM((8, LANES), F32)],
        compiler_params=_cparams("arbitrary"),
        name="router_top2",
    )(x, mod, g, rw_pad, rb_pad)


def _segment_copy(src_hbm, dst_hbm, src_row, dst_row, n_rows, sem):
    src_row = pl.multiple_of(src_row, ROW_ALIGN)
    dst_row = pl.multiple_of(dst_row, ROW_ALIGN)
    n_rows = pl.multiple_of(n_rows, ROW_ALIGN)
    return pltpu.make_async_copy(src_hbm.at[pl.ds(src_row, n_rows)], dst_hbm.at[pl.ds(dst_row, n_rows)], sem)


def _dispatch_kernel(loff_ref, dst_ref, cnt_ref, rows_ref, hs_ref, xs_in_ref, xs_ref, buf, sem_in, sem_out,
                     *, n_tiles, srows):
    del xs_in_ref

    def fetch(t):
        slot = t % DISPATCH_SLOTS
        return pltpu.make_async_copy(hs_ref.at[pl.ds(pl.multiple_of(t * srows, ROW_ALIGN), srows)],
                                     buf.at[slot], sem_in.at[slot])

    def drain(t):
        slot = t % DISPATCH_SLOTS
        n_rows = rows_ref[t]

        @pl.when(n_rows > 0)
        def _():
            _segment_copy(buf.at[slot], xs_ref, 0, 0, n_rows, sem_out.at[slot]).wait()

    for t in range(DISPATCH_AHEAD):
        fetch(t).start()

    def body(t, carry):
        slot = t % DISPATCH_SLOTS
        fetch(t).wait()
        for e in range(N_EXPERTS):
            s = t * N_EXPERTS + e
            n_rows = cnt_ref[s]

            @pl.when(n_rows > 0)
            def _():
                _segment_copy(buf.at[slot], xs_ref, loff_ref[s], dst_ref[s], n_rows, sem_out.at[slot]).start()

        @pl.when(t + DISPATCH_AHEAD < n_tiles)
        def _():
            @pl.when(t + DISPATCH_AHEAD >= DISPATCH_SLOTS)
            def _():
                drain(t + DISPATCH_AHEAD - DISPATCH_SLOTS)

            fetch(t + DISPATCH_AHEAD).start()

        return carry

    lax.fori_loop(0, n_tiles, body, 0)
    for t in range(max(n_tiles - DISPATCH_SLOTS, 0), n_tiles):
        drain(t)


def _dispatch(seg_loff, seg_dst, seg_cnt, tile_rows, hs, xs_init):
    n_tiles = tile_rows.shape[0]
    srows = hs.shape[0] // n_tiles
    assert n_tiles >= DISPATCH_SLOTS
    return pl.pallas_call(
        functools.partial(_dispatch_kernel, n_tiles=n_tiles, srows=srows),
        out_shape=jax.ShapeDtypeStruct(xs_init.shape, xs_init.dtype),
        grid_spec=pltpu.PrefetchScalarGridSpec(
            num_scalar_prefetch=4,
            grid=(1,),
            in_specs=[pl.BlockSpec(memory_space=pl.ANY), pl.BlockSpec(memory_space=pl.ANY)],
            out_specs=pl.BlockSpec(memory_space=pl.ANY),
            scratch_shapes=[pltpu.VMEM((DISPATCH_SLOTS, srows, hs.shape[1]), hs.dtype),
                            pltpu.SemaphoreType.DMA((DISPATCH_SLOTS,)),
                            pltpu.SemaphoreType.DMA((DISPATCH_SLOTS,))],
        ),
        input_output_aliases={5: 0},
        compiler_params=_cparams("arbitrary"),
        name="moe_dispatch",
    )(seg_loff, seg_dst, seg_cnt, tile_rows, hs, xs_init)


def _expert_kernel(te_ref, used_ref, xs_ref, w1_ref, w3_ref, w2_ref, y_ref):
    t = pl.program_id(0)
    del te_ref

    @pl.when(used_ref[t] == 1)
    def _():
        lo, hi = _unpack_bf16_pair(xs_ref[...])
        h = jnp.concatenate([lo.astype(BF16), hi.astype(BF16)], axis=1)
        a = jnp.dot(h, w1_ref[...], preferred_element_type=F32)
        b = jnp.dot(h, w3_ref[...], preferred_element_type=F32)
        hid = ((a * _sigmoid(a)) * b).astype(BF16)
        acc = jnp.dot(hid, w2_ref[...], preferred_element_type=F32)
        half = acc.shape[1] // 2
        y_ref[...] = _pack_bf16_pair(acc[:, :half], acc[:, half:])

    @pl.when(used_ref[t] == 0)
    def _():
        y_ref[...] = jnp.zeros(y_ref.shape, y_ref.dtype)


def _expert_ffn(tile_e, tile_used, xs, w1, w3, w2):
    rows, half = xs.shape
    d = 2 * half
    dff = w1.shape[2]
    tg = TG_MOE

    def weight(shape):
        return pl.BlockSpec((None,) + shape, lambda t, te, us: (te[t], 0, 0), pipeline_mode=pl.Buffered(1))

    return pl.pallas_call(
        _expert_kernel,
        out_shape=jax.ShapeDtypeStruct((rows, half), jnp.uint32),
        grid_spec=pltpu.PrefetchScalarGridSpec(
            num_scalar_prefetch=2,
            grid=(rows // tg,),
            in_specs=[pl.BlockSpec((tg, half), lambda t, te, us: (t, 0)),
                      weight((d, dff)), weight((d, dff)), weight((dff, d))],
            out_specs=pl.BlockSpec((tg, half), lambda t, te, us: (t, 0)),
        ),
        compiler_params=_cparams("arbitrary"),
        name="expert_swiglu",
    )(tile_e, tile_used, xs, w1, w3, w2)


def _combine_kernel(src_ref, loff_ref, cnt_ref, x_ref, ei_ref, pw_ref, mod_ref, y_ref, o_ref, ybuf, sem,
                    *, tm, srows):
    i = pl.program_id(0)
    ybuf[...] = jnp.zeros(ybuf.shape, ybuf.dtype)
    total = 0
    for e in range(N_EXPERTS):
        s = i * N_EXPERTS + e
        n_rows = cnt_ref[s]

        @pl.when(n_rows > 0)
        def _():
            _segment_copy(y_ref, ybuf, src_ref[s], loff_ref[s], n_rows, sem).start()

        total = total + n_rows

    @pl.when(total > 0)
    def _():
        _segment_copy(y_ref, ybuf, 0, 0, total, sem).wait()

    lo, hi = _unpack_bf16_pair(ybuf[...])
    ys = jnp.concatenate([lo.astype(BF16), hi.astype(BF16)], axis=1)
    r_idx = lax.broadcasted_iota(jnp.int32, (tm, srows), 1)
    mix = jnp.zeros(x_ref.shape, F32)
    for k in range(TOP_K):
        pick = jnp.where(r_idx == ei_ref[:, TOP_K + k:TOP_K + k + 1], 1.0, 0.0).astype(BF16)
        mix = mix + pw_ref[:, k:k + 1] * jnp.dot(pick, ys, preferred_element_type=F32)
    o_ref[...] = x_ref[...] + mod_ref[5:6, :] * mix


def _combine(seg_src, seg_loff, seg_cnt, x, ei, pw, mod, y, seq):
    n, d = x.shape
    tm = TM_ROUTE
    tpb = seq // tm
    tok = lambda width: pl.BlockSpec((tm, width), lambda i, *_: (i, 0))
    return pl.pallas_call(
        functools.partial(_combine_kernel, tm=tm, srows=SORT_ROWS),
        out_shape=jax.ShapeDtypeStruct((n, d), F32),
        grid_spec=pltpu.PrefetchScalarGridSpec(
            num_scalar_prefetch=3,
            grid=(n // tm,),
            in_specs=[tok(d), tok(LANES), tok(LANES),
                      pl.BlockSpec((None, 6, d), lambda i, *_: (i // tpb, 0, 0)),
                      pl.BlockSpec(memory_space=pl.ANY)],
            out_specs=tok(d),
            scratch_shapes=[pltpu.VMEM((SORT_ROWS, d // 2), jnp.uint32), pltpu.SemaphoreType.DMA],
        ),
        compiler_params=_cparams("arbitrary"),
        name="moe_combine",
    )(seg_src, seg_loff, seg_cnt, x, ei, pw, mod, y)


def _pack_w_in(w_in):
    d = w_in.shape[0]
    w = MIX_W
    o_ckv = 3 * w + Q_LORA
    o_kr = o_ckv + KV_LORA
    o_su = o_kr + QK_ROPE
    o_rq = o_su + 2 * w
    o_gate = o_rq + 4 * w
    half = HEAD_DIM // 2
    perm = np.array([h * HEAD_DIM + part * half + i
                     for part in range(2) for h in range(N_GROUPS) for i in range(half)])
    kr = w_in[:, o_kr:o_kr + QK_ROPE]
    hr = QK_ROPE // 2
    z = lambda k: jnp.zeros((d, k), w_in.dtype)
    cols = [
        w_in[:, o_gate:o_gate + N_BRANCH * d],
        w_in[:, 0:3 * w + Q_LORA],
        w_in[:, o_rq:o_rq + w][:, perm], w_in[:, o_rq + w:o_rq + 2 * w][:, perm],
        w_in[:, o_rq + 2 * w:o_rq + 4 * w],
        w_in[:, o_su:o_su + 2 * w],
        w_in[:, o_ckv:o_ckv + KV_LORA],
        z(QK_NOPE), kr, z(LANES - QK_HEAD),
        z(QK_NOPE), kr[:, hr:], kr[:, :hr], z(LANES - QK_HEAD),
        z(N_IN - COL_KRB - LANES),
    ]
    return jnp.concatenate(cols, axis=1).astype(BF16)


def _swap_rope_halves(a):
    hr = QK_ROPE // 2
    return jnp.concatenate([a[..., :QK_NOPE], a[..., QK_NOPE + hr:QK_HEAD], a[..., QK_NOPE:QK_NOPE + hr],
                            a[..., QK_HEAD:]], axis=-1)


def _mla_params(cq_g, w_uq, ckv_g, w_ukv, qn_g, kn_g):
    pad = LANES - QK_HEAD
    wq = w_uq.reshape(Q_LORA, N_GROUPS, QK_HEAD)
    wq = jnp.pad(wq, ((0, 0), (0, 0), (0, pad)))
    wkv = w_ukv.reshape(KV_LORA, N_GROUPS, QK_NOPE + V_HEAD)
    wk = jnp.pad(wkv[:, :, :QK_NOPE], ((0, 0), (0, 0), (0, LANES - QK_NOPE)))
    wv = jnp.pad(wkv[:, :, QK_NOPE:], ((0, 0), (0, 0), (0, LANES - V_HEAD)))
    qg = jnp.pad(qn_g, (0, pad))[None, :]
    kg = jnp.pad(kn_g, (0, pad))[None, :]
    bound = (QK_HEAD ** 0.5 * LOG2_E) * jnp.max(jnp.abs(qn_g)) * jnp.max(jnp.abs(kn_g))
    static_shift = bound <= MAX_STATIC_SHIFT
    lane = jnp.arange(LANES)
    qaug = (lane == QK_HEAD).astype(F32)[None, :]
    kaug = qaug * jnp.where(static_shift, -bound, 0.0)
    vaug = jnp.tile((lane == V_HEAD).astype(F32), N_GROUPS)[None, :]
    params = {
        "cq_g": cq_g[None, :], "ckv_g": ckv_g[None, :],
        "wqa": wq.reshape(Q_LORA, -1).astype(BF16),
        "wqb": _swap_rope_halves(wq).reshape(Q_LORA, -1).astype(BF16),
        "wk": wk.reshape(KV_LORA, -1).astype(BF16),
        "wv": wv.reshape(KV_LORA, -1).astype(BF16),
        "qga": qg, "qgb": _swap_rope_halves(qg), "kga": kg, "kgb": _swap_rope_halves(kg),
        "qaug": qaug, "kaug": kaug, "vaug": vaug,
    }
    return params, static_shift


def _mixer_consts():
    h = jnp.arange(N_GROUPS, dtype=F32)
    log_gamma = jnp.log1p(-(2.0 ** (-5.0 - h)))
    pos = jnp.arange(CHUNK, dtype=F32)
    rel = pos[:, None] - pos[None, :]
    dec = jnp.where(rel >= 0, jnp.exp(log_gamma[:, None, None] * jnp.maximum(rel, 0.0)), 0.0)
    lane = np.arange(MIX_W)
    head_k = (lane % LANES) // (HEAD_DIM // 2)
    head_v = lane // HEAD_DIM
    lg_k = log_gamma[head_k]
    return {
        "dec": dec.reshape(N_GROUPS * CHUNK, CHUNK),
        "kdec": jnp.exp(lg_k[None, :] * (CHUNK - 1.0 - pos)[:, None]),
        "qdec": jnp.exp(lg_k[None, :] * (pos + 1.0)[:, None]),
        "cdec": jnp.broadcast_to(jnp.exp(lg_k * CHUNK)[:, None], (MIX_W, MIX_W)),
        "bd": jnp.asarray((head_k[:, None] == head_v[None, :]).astype(np.float32)),
        "gmat": jnp.asarray((head_v[:, None] == head_v[None, :]).astype(np.float32) / HEAD_DIM).astype(BF16),
        "mk": jnp.asarray((head_k[None, :] == np.arange(N_GROUPS)[:, None]).astype(np.float32)),
        "mv": jnp.asarray((head_v[None, :] == np.arange(N_GROUPS)[:, None]).astype(np.float32)),
    }


def _mixer_params(conv_w, gv_g, w_s, b_s, ret_g, w_branch, w_o):
    p = dict(_mixer_consts())
    ws = jnp.tril(w_s)
    p.update({
        "conv_w": conv_w,
        "gv_g": gv_g.reshape(1, MIX_W),
        "ws_cat": jnp.transpose(ws, (1, 0, 2)).reshape(CHUNK, N_GROUPS * CHUNK).astype(BF16),
        "bs_mat": jnp.repeat(b_s.T, HEAD_DIM, axis=1),
        "ret_g": ret_g.reshape(1, MIX_W),
        "w_branch": w_branch.astype(BF16),
        "w_o": w_o.astype(BF16),
    })
    return p


def _moe_layout(meta, tot, n_tiles):
    totals = tot[0, :N_EXPERTS]
    padded = ((totals + TG_MOE - 1) // TG_MOE) * TG_MOE
    ends = jnp.cumsum(padded)
    starts = ends - padded
    seg_cnt = meta[:, 0, :N_EXPERTS]
    seg_loff = meta[:, 2, :N_EXPERTS]
    seg_grouped = starts[None, :] + meta[:, 1, :N_EXPERTS]
    tile_start = jnp.arange(n_tiles, dtype=jnp.int32) * TG_MOE
    tile_e = jnp.sum((tile_start[:, None] >= ends[None, :]).astype(jnp.int32), axis=1)
    used = (tile_start < ends[-1]).astype(jnp.int32)
    last_e = jnp.sum((ends[-1] - 1 >= ends).astype(jnp.int32))
    tile_e = jnp.minimum(jnp.where(used == 1, tile_e, last_e), N_EXPERTS - 1)
    flat = lambda a: a.reshape(-1).astype(jnp.int32)
    return flat(seg_grouped), flat(seg_loff), flat(seg_cnt), tile_e, used


def kernel(x, c, positions, norm1_g, norm2_g, ada_w, ada_b, w_in, conv_w, cq_g, w_uq, ckv_g, w_ukv, qn_g, kn_g, gv_g, w_s, b_s, ret_g, w_branch, w_o, ffn_w1, ffn_w3, ffn_w2, router_w, router_b, moe_w1, moe_w3, moe_w2):
    batch, seq, d = x.shape
    depth = ada_w.shape[0]
    n = batch * seq
    assert seq % max(TM_PROJ, TM_PREP, TQ_ATT, TM_MIX, TM_FFN, TM_ROUTE) == 0
    assert d // 2 % LANES == 0

    c_pad = jnp.pad(c, ((0, 8 - batch), (0, 0)))
    ada = _ada(c_pad, ada_w, ada_b)[:, :batch].reshape(depth, batch, 6, d)
    cosr, sinr, cm, sm = _rope_tables(positions.astype(F32).reshape(n, 1))

    xt = x.reshape(n, d)
    for l in range(depth):
        mod = ada[l]
        proj = _inproj(xt, mod, norm1_g[l][None, :], _pack_w_in(w_in[l]), seq)
        mla_p, static_shift = _mla_params(cq_g[l], w_uq[l], ckv_g[l], w_ukv[l], qn_g[l], kn_g[l])
        q, k, v = _mla_prep(proj, cm, sm, mla_p)
        y_mla = lax.cond(static_shift,
                         functools.partial(_flash, batch=batch, seq=seq, online_max=False),
                         functools.partial(_flash, batch=batch, seq=seq, online_max=True), q, k, v)
        mp = _mixer_params(conv_w[l], gv_g[l], w_s[l], b_s[l], ret_g[l], w_branch[l], w_o[l])
        xt = _mixers(proj, y_mla, xt, cosr, sinr, mod, mp, seq)
        g2n = norm2_g[l][None, :]
        if l % 2 == 0:
            i = l // 2
            xt = _dense_ffn(xt, mod, g2n, ffn_w1[i].astype(BF16), ffn_w3[i].astype(BF16),
                            ffn_w2[i].astype(BF16), seq)
        else:
            i = l // 2
            rw = jnp.pad(router_w[i], ((0, 0), (0, LANES - N_EXPERTS)))
            rw_hi = rw.astype(BF16)
            rw_pad = jnp.concatenate([rw_hi, (rw - rw_hi.astype(F32)).astype(BF16)], axis=1)
            rb_pad = jnp.pad(router_b[i], (0, LANES - N_EXPERTS), constant_values=-1e30)[None, :]
            hs, ei, pw, meta, tot = _router(xt, mod, g2n, rw_pad, rb_pad, seq)
            max_rows = n * TOP_K + N_EXPERTS * (n // TM_ROUTE) * (ROW_ALIGN - 1)
            n_tiles = -(-max_rows // TG_MOE) + N_EXPERTS
            seg_grouped, seg_loff, seg_cnt, tile_e, used = _moe_layout(meta, tot, n_tiles)
            tile_rows = jnp.sum(seg_cnt.reshape(-1, N_EXPERTS), axis=1)
            xs = _dispatch(seg_loff, seg_grouped, seg_cnt, tile_rows, hs,
                           jnp.zeros((n_tiles * TG_MOE, d // 2), jnp.uint32))
            y = _expert_ffn(tile_e, used, xs, moe_w1[i].astype(BF16), moe_w3[i].astype(BF16),
                            moe_w2[i].astype(BF16))
            xt = _combine(seg_grouped, seg_loff, seg_cnt, xt, ei, pw, mod, y, seq)
    return xt.reshape(batch, seq, d)
```

```python
import functools

import jax
import jax.numpy as jnp
import numpy as np
from jax import lax
from jax.experimental import pallas as pl
from jax.experimental.pallas import tpu as pltpu

F32 = jnp.float32
BF16 = jnp.bfloat16
HIGHEST = lax.Precision.HIGHEST

HEAD_DIM = 64
N_GROUPS = 4
MIX_W = N_GROUPS * HEAD_DIM
N_BRANCH = 4
CONV_W = 3
Q_LORA = 256
KV_LORA = 128
QK_NOPE = 64
QK_ROPE = 32
QK_HEAD = QK_NOPE + QK_ROPE
V_HEAD = 64
CHUNK = 128
N_EXPERTS = 8
TOP_K = 2
ROPE_THETA = 10000.0
EPS = 1e-6
LOG2_E = 1.4426950408889634
MAX_STATIC_SHIFT = 50.0

LANES = 128
VMEM_LIMIT_BYTES = 56 * 1024 * 1024

COL_GATES = 0
COL_A = 4096
COL_CQ = COL_A + 3 * MIX_W
COL_R = 5120
COL_SU = 6144
COL_CKV = 6656
COL_KRA = 6784
COL_KRB = 6912
N_IN = 7168

TM_PROJ = 512
TN_PROJ = 1024
TM_PREP = 1024
TQ_ATT = 1024
TM_MIX = 512
TM_FFN = 512
TM_ROUTE = 512
ROW_ALIGN = 8
SORT_ROWS = TOP_K * TM_ROUTE + N_EXPERTS * ROW_ALIGN
TG_MOE = 512
DISPATCH_SLOTS = 4
DISPATCH_AHEAD = 2


def _cparams(*sem):
    return pltpu.CompilerParams(dimension_semantics=sem, vmem_limit_bytes=VMEM_LIMIT_BYTES)


def _sigmoid(x):
    return jnp.tanh(x * 0.5) * 0.5 + 0.5


def _group_mean(x, gmat_bf16):
    hi = x.astype(BF16)
    lo = (x - hi.astype(F32)).astype(BF16)
    return (jnp.dot(hi, gmat_bf16, preferred_element_type=F32)
            + jnp.dot(lo, gmat_bf16, preferred_element_type=F32))


def _pack_bf16_pair(lo, hi):
    lo_bits = lax.bitcast_convert_type(lo.astype(BF16).astype(F32), jnp.uint32)
    hi_bits = lax.bitcast_convert_type(hi.astype(BF16).astype(F32), jnp.uint32)
    return (lo_bits >> 16) | (hi_bits & jnp.uint32(0xFFFF0000))


def _unpack_bf16_pair(p):
    lo = lax.bitcast_convert_type(p << 16, F32)
    hi = lax.bitcast_convert_type(p & jnp.uint32(0xFFFF0000), F32)
    return lo, hi


def _norm_mod(x, g, shift, scale):
    y = x * lax.rsqrt(jnp.mean(x * x, axis=-1, keepdims=True) + EPS)
    return (y * g) * (1.0 + scale) + shift


def _ada_kernel(c_ref, w_ref, b_ref, o_ref):
    c = c_ref[...]
    cond = c * _sigmoid(c)
    o_ref[...] = jnp.dot(cond, w_ref[...], precision=HIGHEST, preferred_element_type=F32) + b_ref[...]


def _ada(c_pad, ada_w, ada_b):
    n_layer, d, d6 = ada_w.shape
    rows = c_pad.shape[0]
    tn = 1024
    return pl.pallas_call(
        _ada_kernel,
        out_shape=jax.ShapeDtypeStruct((n_layer, rows, d6), F32),
        grid=(n_layer, d6 // tn),
        in_specs=[
            pl.BlockSpec((rows, d), lambda l, j: (0, 0)),
            pl.BlockSpec((None, d, tn), lambda l, j: (l, 0, j)),
            pl.BlockSpec((None, 1, tn), lambda l, j: (l, 0, j)),
        ],
        out_specs=pl.BlockSpec((None, rows, tn), lambda l, j: (l, 0, j)),
        compiler_params=_cparams("parallel", "parallel"),
        name="ada_mod",
    )(c_pad, ada_w, ada_b.reshape(n_layer, 1, d6))


def _rope_kernel(pos_ref, inv_ref, cr_ref, sr_ref, cm_ref, sm_ref):
    half_r = HEAD_DIM // 2
    half_m = QK_ROPE // 2
    ang = pos_ref[...] * inv_ref[...]
    c = jnp.cos(ang)
    s = jnp.sin(ang)
    lane = lax.broadcasted_iota(jnp.int32, c.shape, 1)

    def tile_r(t):
        t = jnp.where(lane < half_r, t, 0.0)
        out = t
        for k in range(1, LANES // half_r):
            out = out + pltpu.roll(t, k * half_r, axis=1)
        return out

    cr_ref[...] = tile_r(c)
    sr_ref[...] = tile_r(s)
    first = jnp.logical_and(lane >= QK_NOPE, lane < QK_NOPE + half_m)
    second = jnp.logical_and(lane >= QK_NOPE + half_m, lane < QK_HEAD)
    c1, c2 = pltpu.roll(c, QK_NOPE - half_r, axis=1), pltpu.roll(c, QK_NOPE + half_m - half_r, axis=1)
    s1, s2 = pltpu.roll(s, QK_NOPE - half_r, axis=1), pltpu.roll(s, QK_NOPE + half_m - half_r, axis=1)
    cm_ref[...] = jnp.where(first, c1, jnp.where(second, c2, 1.0))
    sm_ref[...] = jnp.where(first, -s1, jnp.where(second, s2, 0.0))


def _rope_tables(pos_f):
    n = pos_f.shape[0]
    tm = 1024
    half_r = HEAD_DIM // 2
    half_m = QK_ROPE // 2
    inv_r = ROPE_THETA ** (-jnp.arange(half_r, dtype=F32) / half_r)
    inv_m = ROPE_THETA ** (-jnp.arange(half_m, dtype=F32) / half_m)
    inv = jnp.concatenate([inv_r, inv_m, jnp.zeros((LANES - half_r - half_m,), F32)])[None, :]
    tab = pl.BlockSpec((tm, LANES), lambda i: (i, 0))
    shape = jax.ShapeDtypeStruct((n, LANES), F32)
    return pl.pallas_call(
        _rope_kernel,
        out_shape=(shape, shape, shape, shape),
        grid=(n // tm,),
        in_specs=[pl.BlockSpec((tm, 1), lambda i: (i, 0)), pl.BlockSpec((1, LANES), lambda i: (0, 0))],
        out_specs=(tab, tab, tab, tab),
        compiler_params=_cparams("parallel"),
        name="rope_tables",
    )(pos_f, inv)


def _inproj_kernel(x_ref, mod_ref, g_ref, w_ref, o_ref):
    h = _norm_mod(x_ref[...], g_ref[...], mod_ref[0:1, :], mod_ref[1:2, :]).astype(BF16)
    for c in range(N_IN // TN_PROJ):
        cols = slice(c * TN_PROJ, (c + 1) * TN_PROJ)
        o_ref[:, cols] = jnp.dot(h, w_ref[:, cols], preferred_element_type=F32).astype(BF16)


def _resident(shape):
    return pl.BlockSpec(shape, lambda *_: (0,) * len(shape), pipeline_mode=pl.Buffered(1))


def _inproj(x, mod, g, w, seq):
    n, d = x.shape
    tm = TM_PROJ
    tpb = seq // tm
    return pl.pallas_call(
        _inproj_kernel,
        out_shape=jax.ShapeDtypeStruct((n, N_IN), BF16),
        grid=(n // tm,),
        in_specs=[
            pl.BlockSpec((tm, d), lambda i: (i, 0)),
            pl.BlockSpec((None, 6, d), lambda i: (i // tpb, 0, 0)),
            pl.BlockSpec((1, d), lambda i: (0, 0)),
            _resident((d, N_IN)),
        ],
        out_specs=pl.BlockSpec((tm, N_IN), lambda i: (i, 0)),
        compiler_params=_cparams("parallel"),
        name="in_proj",
    )(x, mod, g, w)


def _mla_prep_kernel(cq_ref, ckv_ref, kra_ref, krb_ref, cm_ref, sm_ref, cqg_ref, wqa_ref, wqb_ref,
                     ckvg_ref, wk_ref, wv_ref, qga_ref, qgb_ref, kga_ref, kgb_ref,
                     qaug_ref, kaug_ref, vaug_ref, q_ref, k_ref, v_ref):
    cq = cq_ref[...].astype(F32)
    cqn = (cq * lax.rsqrt(jnp.mean(cq * cq, axis=-1, keepdims=True) + EPS) * cqg_ref[...]).astype(BF16)
    qa = jnp.dot(cqn, wqa_ref[...], preferred_element_type=F32)
    qb = jnp.dot(cqn, wqb_ref[...], preferred_element_type=F32)
    ckv = ckv_ref[...].astype(F32)
    ckvn = (ckv * lax.rsqrt(jnp.mean(ckv * ckv, axis=-1, keepdims=True) + EPS) * ckvg_ref[...]).astype(BF16)
    ka = jnp.dot(ckvn, wk_ref[...], preferred_element_type=F32)
    v_ref[...] = (jnp.dot(ckvn, wv_ref[...], preferred_element_type=F32) + vaug_ref[...]).astype(BF16)
    kra = kra_ref[...].astype(F32)
    krb = krb_ref[...].astype(F32)
    cm = cm_ref[...]
    sm = sm_ref[...]
    scale = QK_HEAD ** -0.5 * LOG2_E
    for h in range(N_GROUPS):
        sl = slice(h * LANES, (h + 1) * LANES)
        qah, qbh = qa[:, sl], qb[:, sl]
        r = lax.rsqrt(jnp.sum(qah * qah, axis=-1, keepdims=True) * (1.0 / QK_HEAD) + EPS)
        q_rot = (qah * r) * qga_ref[...] * cm + (qbh * r) * qgb_ref[...] * sm
        q_ref[:, sl] = (q_rot * scale + qaug_ref[...]).astype(BF16)
        kah = ka[:, sl] + kra
        kbh = ka[:, sl] + krb
        r = lax.rsqrt(jnp.sum(kah * kah, axis=-1, keepdims=True) * (1.0 / QK_HEAD) + EPS)
        k_rot = (kah * r) * kga_ref[...] * cm + (kbh * r) * kgb_ref[...] * sm
        k_ref[:, sl] = (k_rot + kaug_ref[...]).astype(BF16)


def _mla_prep(proj, cm, sm, p):
    n = proj.shape[0]
    tm = TM_PREP
    hw = N_GROUPS * LANES

    def col(width, offset):
        return pl.BlockSpec((tm, width), lambda i: (i, offset // width))

    def full(a):
        return pl.BlockSpec(a.shape, lambda i: (0,) * a.ndim)

    weights = [p["cq_g"], p["wqa"], p["wqb"], p["ckv_g"], p["wk"], p["wv"],
               p["qga"], p["qgb"], p["kga"], p["kgb"], p["qaug"], p["kaug"], p["vaug"]]
    head_tile = pl.BlockSpec((tm, hw), lambda i: (i, 0))
    out = jax.ShapeDtypeStruct((n, hw), BF16)
    return pl.pallas_call(
        _mla_prep_kernel,
        out_shape=(out, out, out),
        grid=(n // tm,),
        in_specs=[col(Q_LORA, COL_CQ), col(KV_LORA, COL_CKV), col(LANES, COL_KRA), col(LANES, COL_KRB),
                  pl.BlockSpec((tm, LANES), lambda i: (i, 0)), pl.BlockSpec((tm, LANES), lambda i: (i, 0))]
                 + [full(w) for w in weights],
        out_specs=(head_tile, head_tile, head_tile),
        compiler_params=_cparams("parallel"),
        name="mla_prep",
    )(proj, proj, proj, proj, cm, sm, *weights)


def _flash_kernel(q_ref, k_ref, v_ref, o_ref, acc_scr, *rest, tq, online_max):
    i = pl.program_id(1)
    j = pl.program_id(2)

    @pl.when(j == 0)
    def _():
        acc_scr[...] = jnp.zeros(acc_scr.shape, F32)
        if online_max:
            rest[0][...] = jnp.full(rest[0].shape, -jnp.inf, F32)

    def step(masked):
        if masked:
            row = lax.broadcasted_iota(jnp.int32, (tq, tq), 0)
            col = lax.broadcasted_iota(jnp.int32, (tq, tq), 1)
            keep = col <= row
        for h in range(N_GROUPS):
            sl = slice(h * LANES, (h + 1) * LANES)
            s = lax.dot_general(q_ref[:, sl], k_ref[:, sl], (((1,), (1,)), ((), ())),
                                preferred_element_type=F32)
            if masked:
                s = jnp.where(keep, s, -jnp.inf)
            if online_max:
                m_scr = rest[0]
                m_prev = m_scr[h]
                m_new = jnp.maximum(m_prev, jnp.max(s, axis=-1, keepdims=True))
                p = jnp.exp2(s - m_new).astype(BF16)
                acc_scr[h] = jnp.exp2(m_prev - m_new) * acc_scr[h] + jnp.dot(
                    p, v_ref[:, sl], preferred_element_type=F32)
                m_scr[h] = m_new
            else:
                acc_scr[h] += jnp.dot(jnp.exp2(s).astype(BF16), v_ref[:, sl], preferred_element_type=F32)

    @pl.when(j < i)
    def _():
        step(False)

    @pl.when(j == i)
    def _():
        step(True)
        lane = lax.broadcasted_iota(jnp.int32, (tq, LANES), 1)
        for pr in range(N_GROUPS // 2):
            lo = acc_scr[2 * pr]
            hi = acc_scr[2 * pr + 1]
            lo = lo / lo[:, V_HEAD:V_HEAD + 1]
            hi = hi / hi[:, V_HEAD:V_HEAD + 1]
            both = jnp.where(lane < V_HEAD, lo, pltpu.roll(hi, V_HEAD, axis=1))
            o_ref[:, pr * LANES:(pr + 1) * LANES] = both.astype(BF16)


def _flash(q, k, v, batch, seq, online_max):
    n = q.shape[0]
    tq = TQ_ATT
    nq = seq // tq
    hw = N_GROUPS * LANES
    scratch = [pltpu.VMEM((N_GROUPS, tq, LANES), F32)]
    if online_max:
        scratch.append(pltpu.VMEM((N_GROUPS, tq, 1), F32))
    return pl.pallas_call(
        functools.partial(_flash_kernel, tq=tq, online_max=online_max),
        out_shape=jax.ShapeDtypeStruct((n, MIX_W), BF16),
        grid=(batch, nq, nq),
        in_specs=[
            pl.BlockSpec((tq, hw), lambda b, i, j: (b * nq + i, 0)),
            pl.BlockSpec((tq, hw), lambda b, i, j: (b * nq + jnp.minimum(j, i), 0)),
            pl.BlockSpec((tq, hw), lambda b, i, j: (b * nq + jnp.minimum(j, i), 0)),
        ],
        out_specs=pl.BlockSpec((tq, MIX_W), lambda b, i, j: (b * nq + i, 0)),
        scratch_shapes=scratch,
        compiler_params=_cparams("parallel", "parallel", "arbitrary"),
        name="mla_flash_online" if online_max else "mla_flash",
    )(q, k, v)


def _gelu_tanh(x):
    return jax.nn.gelu(x, approximate=True)


def _mix_kernel(gates_ref, a_ref, r_ref, su_ref, ymla_ref, x_ref, cos_ref, sin_ref, mod_ref,
                convw_ref, gvg_ref, wscat_ref, bsmat_ref, retg_ref, dec_ref, kdec_ref, qdec_ref,
                cdec_ref, bd_ref, gmat_ref, mk_ref, mv_ref, wb_ref, wo_ref,
                o_ref, carry_scr, state_scr, ysg_scr, yret_scr, *, tm, tpb):
    i = pl.program_id(0)

    @pl.when(i % tpb == 0)
    def _():
        carry_scr[...] = jnp.zeros(carry_scr.shape, F32)
        state_scr[...] = jnp.zeros(state_scr.shape, F32)

    w = MIX_W
    a_b = a_ref[:, 0:w].astype(F32)
    u = a_ref[:, w:2 * w].astype(F32) * a_ref[:, 2 * w:3 * w].astype(F32)
    rowi = lax.broadcasted_iota(jnp.int32, (tm, w), 0)
    prev1 = carry_scr[0:1, :]
    prev2 = carry_scr[1:2, :]
    u1 = jnp.where(rowi == 0, prev1, pltpu.roll(u, 1, axis=0))
    u2 = jnp.where(rowi == 0, prev2, jnp.where(rowi == 1, prev1, pltpu.roll(u, 2, axis=0)))
    carry_scr[0:1, :] = u[tm - 1:tm, :]
    carry_scr[1:2, :] = u[tm - 2:tm - 1, :]
    y_conv = a_b * (convw_ref[0:1, :] * u2 + convw_ref[1:2, :] * u1 + convw_ref[2:3, :] * u)

    gmat = gmat_ref[...]
    s_u = _gelu_tanh(su_ref[:, 0:w].astype(F32))
    s_v = _gelu_tanh(su_ref[:, w:2 * w].astype(F32))
    ms = _group_mean(s_v * s_v, gmat)
    vn = (s_v * lax.rsqrt(ms + EPS) * gvg_ref[...]).astype(BF16)

    cosr = cos_ref[...]
    sinr = sin_ref[...]

    def rot(t):
        t1, t2 = t[:, 0:LANES], t[:, LANES:2 * LANES]
        return jnp.concatenate([t1 * cosr - t2 * sinr, t2 * cosr + t1 * sinr], axis=-1)

    rq = rot(r_ref[:, 0:w].astype(F32))
    rk = rot(r_ref[:, w:2 * w].astype(F32)) * (HEAD_DIM ** -0.5)

    for c in range(tm // CHUNK):
        rows = slice(c * CHUNK, (c + 1) * CHUNK)
        vc = vn[rows, :]
        vbd = jnp.concatenate([vc * mv_ref[g:g + 1, :].astype(BF16) for g in range(N_GROUPS)], axis=0)
        mixed = jnp.dot(wscat_ref[...], vbd, preferred_element_type=F32) + bsmat_ref[...]
        ysg_scr[rows, :] = s_u[rows, :] * mixed

        qc = rq[rows, :]
        kc = rk[rows, :]
        kcb = kc.astype(BF16)
        vcb = r_ref[rows, 2 * w:3 * w]
        qstack = jnp.concatenate([(qc * mk_ref[h:h + 1, :]).astype(BF16) for h in range(N_GROUPS)], axis=0)
        sc = lax.dot_general(qstack, kcb, (((1,), (1,)), ((), ())), preferred_element_type=F32)
        sc = (sc * dec_ref[...]).astype(BF16)
        scat = jnp.concatenate([sc[h * CHUNK:(h + 1) * CHUNK, :] for h in range(N_GROUPS)], axis=1)
        vstack = jnp.concatenate([vcb * mv_ref[h:h + 1, :].astype(BF16) for h in range(N_GROUPS)], axis=0)
        o_c = jnp.dot(scat, vstack, preferred_element_type=F32)
        state = state_scr[...]
        o_c = o_c + jnp.dot((qc * qdec_ref[...]).astype(BF16), state.astype(BF16),
                            preferred_element_type=F32)
        kd_t = jnp.transpose(kc * kdec_ref[...]).astype(BF16)
        kv = jnp.dot(kd_t, vcb, preferred_element_type=F32)
        state_scr[...] = state * cdec_ref[...] + kv * bd_ref[...]
        yret_scr[rows, :] = o_c

    o_all = yret_scr[...]
    xc = o_all - _group_mean(o_all, gmat)
    var = _group_mean(xc * xc, gmat)
    r_g = r_ref[:, 3 * w:4 * w].astype(F32)
    y_ret = (r_g * _sigmoid(r_g)) * (xc * lax.rsqrt(var + EPS) * retg_ref[...])

    d = x_ref.shape[1]
    ys = (y_conv, ymla_ref[...], ysg_scr[...], y_ret)
    merged = jnp.zeros((tm, d), F32)
    for n in range(N_BRANCH):
        gate = _sigmoid(gates_ref[:, n * d:(n + 1) * d]).astype(F32)
        merged = merged + gate * jnp.dot(ys[n].astype(BF16), wb_ref[n], preferred_element_type=F32)
    out = jnp.dot(merged.astype(BF16), wo_ref[...], preferred_element_type=F32)
    o_ref[...] = x_ref[...] + mod_ref[2:3, :] * out


def _mixers(proj, ymla, x, cosr, sinr, mod, p, seq):
    n, d = x.shape
    tm = TM_MIX
    tpb = seq // tm

    def col(width, offset):
        return pl.BlockSpec((tm, width), lambda i: (i, offset // width))

    def full(a):
        return pl.BlockSpec(a.shape, lambda i: (0,) * a.ndim)

    consts = [p["conv_w"], p["gv_g"], p["ws_cat"], p["bs_mat"], p["ret_g"], p["dec"], p["kdec"],
              p["qdec"], p["cdec"], p["bd"], p["gmat"], p["mk"], p["mv"], p["w_branch"], p["w_o"]]
    return pl.pallas_call(
        functools.partial(_mix_kernel, tm=tm, tpb=tpb),
        out_shape=jax.ShapeDtypeStruct((n, d), F32),
        grid=(n // tm,),
        in_specs=[col(N_BRANCH * d, COL_GATES), col(4 * MIX_W, COL_A), col(4 * MIX_W, COL_R),
                  col(2 * MIX_W, COL_SU),
                  pl.BlockSpec((tm, MIX_W), lambda i: (i, 0)),
                  pl.BlockSpec((tm, d), lambda i: (i, 0)),
                  pl.BlockSpec((tm, LANES), lambda i: (i, 0)),
                  pl.BlockSpec((tm, LANES), lambda i: (i, 0)),
                  pl.BlockSpec((None, 6, d), lambda i: (i // tpb, 0, 0))]
                 + [full(c) for c in consts],
        out_specs=pl.BlockSpec((tm, d), lambda i: (i, 0)),
        scratch_shapes=[pltpu.VMEM((8, MIX_W), F32), pltpu.VMEM((MIX_W, MIX_W), F32),
                        pltpu.VMEM((tm, MIX_W), F32), pltpu.VMEM((tm, MIX_W), F32)],
        compiler_params=_cparams("arbitrary"),
        name="mixers_merge",
    )(proj, proj, proj, proj, ymla, x, cosr, sinr, mod, *consts)


def _ffn_kernel(x_ref, mod_ref, g_ref, w1_ref, w3_ref, w2_ref, o_ref):
    x = x_ref[...]
    h = _norm_mod(x, g_ref[...], mod_ref[3:4, :], mod_ref[4:5, :]).astype(BF16)
    a = jnp.dot(h, w1_ref[...], preferred_element_type=F32)
    b = jnp.dot(h, w3_ref[...], preferred_element_type=F32)
    hid = ((a * _sigmoid(a)) * b).astype(BF16)
    o_ref[...] = x + mod_ref[5:6, :] * jnp.dot(hid, w2_ref[...], preferred_element_type=F32)


def _dense_ffn(x, mod, g, w1, w3, w2, seq):
    n, d = x.shape
    dff = w1.shape[1]
    tm = TM_FFN
    tpb = seq // tm
    return pl.pallas_call(
        _ffn_kernel,
        out_shape=jax.ShapeDtypeStruct((n, d), F32),
        grid=(n // tm,),
        in_specs=[
            pl.BlockSpec((tm, d), lambda i: (i, 0)),
            pl.BlockSpec((None, 6, d), lambda i: (i // tpb, 0, 0)),
            pl.BlockSpec((1, d), lambda i: (0, 0)),
            _resident((d, dff)), _resident((d, dff)), _resident((dff, d)),
        ],
        out_specs=pl.BlockSpec((tm, d), lambda i: (i, 0)),
        compiler_params=_cparams("parallel"),
        name="dense_swiglu",
    )(x, mod, g, w1, w3, w2)


def _router_kernel(x_ref, mod_ref, g_ref, rw_ref, rb_ref, hs_ref, ei_ref, pw_ref, meta_ref, tot_ref,
                   carry_scr, *, tm, srows):
    i = pl.program_id(0)

    @pl.when(i == 0)
    def _():
        carry_scr[...] = jnp.zeros(carry_scr.shape, F32)

    h = _norm_mod(x_ref[...], g_ref[...], mod_ref[3:4, :], mod_ref[4:5, :])

    h_hi = h.astype(BF16)
    h_lo = (h - h_hi.astype(F32)).astype(BF16)
    hw = jnp.dot(h_hi, rw_ref[...], preferred_element_type=F32)
    logits = (hw[:, :LANES] + hw[:, LANES:] + jnp.dot(h_lo, rw_ref[:, :LANES], preferred_element_type=F32)
              + rb_ref[...])
    mx = jnp.max(logits, axis=-1, keepdims=True)
    ex = jnp.exp(logits - mx)
    probs = ex / jnp.sum(ex, axis=-1, keepdims=True)
    lane = lax.broadcasted_iota(jnp.int32, (tm, LANES), 1)
    valid = lane < N_EXPERTS
    probs = jnp.where(valid, probs, -1.0)
    m1 = jnp.max(probs, axis=-1, keepdims=True)
    i1 = jnp.min(jnp.where(probs == m1, lane, LANES), axis=-1, keepdims=True)
    rest = jnp.where(lane == i1, -1.0, probs)
    m2 = jnp.max(rest, axis=-1, keepdims=True)
    i2 = jnp.min(jnp.where(rest == m2, lane, LANES), axis=-1, keepdims=True)
    den = m1 + m2
    pw_ref[...] = jnp.where(lane == 0, m1 / den, jnp.where(lane == 1, m2 / den, 0.0))

    sel1 = lane == i1
    sel2 = lane == i2
    onehot = jnp.where(sel1, 1.0, 0.0) + jnp.where(sel2, 1.0, 0.0)
    r_i = lax.broadcasted_iota(jnp.int32, (tm, tm), 0)
    c_i = lax.broadcasted_iota(jnp.int32, (tm, tm), 1)
    tri = jnp.where(c_i < r_i, 1.0, 0.0).astype(BF16)
    before = jnp.dot(tri, onehot.astype(BF16), preferred_element_type=F32)
    cnt = jnp.sum(onehot, axis=0, keepdims=True)
    cnt_al = jnp.floor((cnt + (ROW_ALIGN - 1)) * (1.0 / ROW_ALIGN)) * ROW_ALIGN
    e_r = lax.broadcasted_iota(jnp.int32, (LANES, LANES), 0)
    e_c = lax.broadcasted_iota(jnp.int32, (LANES, LANES), 1)
    upper = jnp.where(e_r < e_c, 1.0, 0.0)
    loff = jnp.dot(jnp.broadcast_to(cnt_al, (8, LANES)), upper, precision=HIGHEST,
                   preferred_element_type=F32)[0:1, :]
    slot = loff + before
    slot1 = jnp.sum(jnp.where(sel1, slot, 0.0), axis=-1, keepdims=True).astype(jnp.int32)
    slot2 = jnp.sum(jnp.where(sel2, slot, 0.0), axis=-1, keepdims=True).astype(jnp.int32)
    ei = jnp.where(lane == 0, i1, jnp.where(lane == 1, i2, 0))
    ei_ref[...] = jnp.where(lane == 2, slot1, jnp.where(lane == 3, slot2, ei))

    r_idx = lax.broadcasted_iota(jnp.int32, (tm, srows), 1)
    place = jnp.where(r_idx == slot1, 1.0, jnp.where(r_idx == slot2, 1.0, 0.0)).astype(BF16)
    hs = lax.dot_general(place, h.astype(BF16), (((0,), (0,)), ((), ())), preferred_element_type=F32)
    half = hs.shape[1] // 2
    hs_ref[...] = _pack_bf16_pair(hs[:, :half], hs[:, half:])

    carry = carry_scr[0:1, :]
    mrow = lax.broadcasted_iota(jnp.int32, (8, LANES), 0)
    meta = jnp.where(mrow == 0, cnt_al, jnp.where(mrow == 1, carry, jnp.where(mrow == 2, loff, 0.0)))
    meta_ref[...] = meta.astype(jnp.int32)
    carry_scr[0:1, :] = carry + cnt_al
    tot_ref[...] = jnp.broadcast_to(carry + cnt_al, tot_ref.shape).astype(jnp.int32)


def _router(x, mod, g, rw_pad, rb_pad, seq):
    n, d = x.shape
    tm = TM_ROUTE
    tpb = seq // tm
    nt = n // tm
    return pl.pallas_call(
        functools.partial(_router_kernel, tm=tm, srows=SORT_ROWS),
        out_shape=(jax.ShapeDtypeStruct((nt * SORT_ROWS, d // 2), jnp.uint32),
                   jax.ShapeDtypeStruct((n, LANES), jnp.int32),
                   jax.ShapeDtypeStruct((n, LANES), F32),
                   jax.ShapeDtypeStruct((nt, 8, LANES), jnp.int32),
                   jax.ShapeDtypeStruct((8, LANES), jnp.int32)),
        grid=(nt,),
        in_specs=[
            pl.BlockSpec((tm, d), lambda i: (i, 0)),
            pl.BlockSpec((None, 6, d), lambda i: (i // tpb, 0, 0)),
            pl.BlockSpec((1, d), lambda i: (0, 0)),
            pl.BlockSpec((d, 2 * LANES), lambda i: (0, 0)),
            pl.BlockSpec((1, LANES), lambda i: (0, 0)),
        ],
        out_specs=(pl.BlockSpec((SORT_ROWS, d // 2), lambda i: (i, 0)),
                   pl.BlockSpec((tm, LANES), lambda i: (i, 0)),
                   pl.BlockSpec((tm, LANES), lambda i: (i, 0)),
                   pl.BlockSpec((None, 8, LANES), lambda i: (i, 0, 0)),
                   pl.BlockSpec((8, LANES), lambda i: (0, 0))),
        scratch_shapes=[pltpu.VMEM((8, LANES), F32)],
        compiler_params=_cparams("arbitrary"),
        name="router_top2",
    )(x, mod, g, rw_pad, rb_pad)


def _segment_copy(src_hbm, dst_hbm, src_row, dst_row, n_rows, sem):
    src_row = pl.multiple_of(src_row, ROW_ALIGN)
    dst_row = pl.multiple_of(dst_row, ROW_ALIGN)
    n_rows = pl.multiple_of(n_rows, ROW_ALIGN)
    return pltpu.make_async_copy(src_hbm.at[pl.ds(src_row, n_rows)], dst_hbm.at[pl.ds(dst_row, n_rows)], sem)


def _dispatch_kernel(loff_ref, dst_ref, cnt_ref, rows_ref, gap_ref, hs_ref, xs_ref, buf, zbuf, sem_in, sem_out,
                     sem_zero, *, n_tiles, srows, n_out_tiles):
    zbuf[...] = jnp.zeros(zbuf.shape, zbuf.dtype)
    tg = zbuf.shape[0]
    first_unused = gap_ref[2 * N_EXPERTS]

    def zero_gap(e):
        return _segment_copy(zbuf, xs_ref, 0, gap_ref[2 * e], gap_ref[2 * e + 1], sem_zero)

    def zero_tile(t):
        return pltpu.make_async_copy(zbuf, xs_ref.at[pl.ds(pl.multiple_of(t * tg, ROW_ALIGN), tg)], sem_zero)

    def for_each_zero_copy(action):
        for e in range(N_EXPERTS):
            @pl.when(gap_ref[2 * e + 1] > 0)
            def _():
                action(zero_gap(e))

        def tail(t, carry):
            action(zero_tile(t))
            return carry

        lax.fori_loop(first_unused, n_out_tiles, tail, 0)

    for_each_zero_copy(lambda copy: copy.start())

    def fetch(t):
        slot = t % DISPATCH_SLOTS
        return pltpu.make_async_copy(hs_ref.at[pl.ds(pl.multiple_of(t * srows, ROW_ALIGN), srows)],
                                     buf.at[slot], sem_in.at[slot])

    def drain(t):
        slot = t % DISPATCH_SLOTS
        n_rows = rows_ref[t]

        @pl.when(n_rows > 0)
        def _():
            _segment_copy(buf.at[slot], xs_ref, 0, 0, n_rows, sem_out.at[slot]).wait()

    for t in range(DISPATCH_AHEAD):
        fetch(t).start()

    def body(t, carry):
        slot = t % DISPATCH_SLOTS
        fetch(t).wait()
        for e in range(N_EXPERTS):
            s = t * N_EXPERTS + e
            n_rows = cnt_ref[s]

            @pl.when(n_rows > 0)
            def _():
                _segment_copy(buf.at[slot], xs_ref, loff_ref[s], dst_ref[s], n_rows, sem_out.at[slot]).start()

        @pl.when(t + DISPATCH_AHEAD < n_tiles)
        def _():
            @pl.when(t + DISPATCH_AHEAD >= DISPATCH_SLOTS)
            def _():
                drain(t + DISPATCH_AHEAD - DISPATCH_SLOTS)

            fetch(t + DISPATCH_AHEAD).start()

        return carry

    lax.fori_loop(0, n_tiles, body, 0)
    for t in range(max(n_tiles - DISPATCH_SLOTS, 0), n_tiles):
        drain(t)
    for_each_zero_copy(lambda copy: copy.wait())


def _dispatch(seg_loff, seg_dst, seg_cnt, tile_rows, gaps, hs, n_out_tiles):
    n_tiles = tile_rows.shape[0]
    srows = hs.shape[0] // n_tiles
    assert n_tiles >= DISPATCH_SLOTS
    return pl.pallas_call(
        functools.partial(_dispatch_kernel, n_tiles=n_tiles, srows=srows, n_out_tiles=n_out_tiles),
        out_shape=jax.ShapeDtypeStruct((n_out_tiles * TG_MOE, hs.shape[1]), hs.dtype),
        grid_spec=pltpu.PrefetchScalarGridSpec(
            num_scalar_prefetch=5,
            grid=(1,),
            in_specs=[pl.BlockSpec(memory_space=pl.ANY)],
            out_specs=pl.BlockSpec(memory_space=pl.ANY),
            scratch_shapes=[pltpu.VMEM((DISPATCH_SLOTS, srows, hs.shape[1]), hs.dtype),
                            pltpu.VMEM((TG_MOE, hs.shape[1]), hs.dtype),
                            pltpu.SemaphoreType.DMA((DISPATCH_SLOTS,)),
                            pltpu.SemaphoreType.DMA((DISPATCH_SLOTS,)),
                            pltpu.SemaphoreType.DMA],
        ),
        compiler_params=_cparams("arbitrary"),
        name="moe_dispatch",
    )(seg_loff, seg_dst, seg_cnt, tile_rows, gaps, hs)


def _expert_kernel(te_ref, used_ref, xs_ref, w1_ref, w3_ref, w2_ref, y_ref):
    t = pl.program_id(0)
    del te_ref

    @pl.when(used_ref[t] == 1)
    def _():
        lo, hi = _unpack_bf16_pair(xs_ref[...])
        h = jnp.concatenate([lo.astype(BF16), hi.astype(BF16)], axis=1)
        a = jnp.dot(h, w1_ref[...], preferred_element_type=F32)
        b = jnp.dot(h, w3_ref[...], preferred_element_type=F32)
        hid = ((a * _sigmoid(a)) * b).astype(BF16)
        acc = jnp.dot(hid, w2_ref[...], preferred_element_type=F32)
        half = acc.shape[1] // 2
        y_ref[...] = _pack_bf16_pair(acc[:, :half], acc[:, half:])

    @pl.when(used_ref[t] == 0)
    def _():
        y_ref[...] = jnp.zeros(y_ref.shape, y_ref.dtype)


def _expert_ffn(tile_e, tile_used, xs, w1, w3, w2):
    rows, half = xs.shape
    d = 2 * half
    dff = w1.shape[2]
    tg = TG_MOE

    def weight(shape):
        return pl.BlockSpec((None,) + shape, lambda t, te, us: (te[t], 0, 0), pipeline_mode=pl.Buffered(1))

    return pl.pallas_call(
        _expert_kernel,
        out_shape=jax.ShapeDtypeStruct((rows, half), jnp.uint32),
        grid_spec=pltpu.PrefetchScalarGridSpec(
            num_scalar_prefetch=2,
            grid=(rows // tg,),
            in_specs=[pl.BlockSpec((tg, half), lambda t, te, us: (t, 0)),
                      weight((d, dff)), weight((d, dff)), weight((dff, d))],
            out_specs=pl.BlockSpec((tg, half), lambda t, te, us: (t, 0)),
        ),
        compiler_params=_cparams("arbitrary"),
        name="expert_swiglu",
    )(tile_e, tile_used, xs, w1, w3, w2)


def _combine_kernel(src_ref, loff_ref, cnt_ref, rows_ref, x_ref, ei_ref, pw_ref, mod_ref, y_ref, o_ref,
                    ybuf, sem, *, tm, srows):
    i = pl.program_id(0)
    slot = i % 2

    def fetch(tile, into):
        ybuf[into] = jnp.zeros(ybuf.shape[1:], ybuf.dtype)
        for e in range(N_EXPERTS):
            s = tile * N_EXPERTS + e
            n_rows = cnt_ref[s]

            @pl.when(n_rows > 0)
            def _():
                _segment_copy(y_ref, ybuf.at[into], src_ref[s], loff_ref[s], n_rows, sem.at[into]).start()

    @pl.when(i == 0)
    def _():
        fetch(i, slot)

    @pl.when(i + 1 < pl.num_programs(0))
    def _():
        fetch(i + 1, 1 - slot)

    @pl.when(rows_ref[i] > 0)
    def _():
        _segment_copy(y_ref, ybuf.at[slot], 0, 0, rows_ref[i], sem.at[slot]).wait()

    lo, hi = _unpack_bf16_pair(ybuf[slot])
    ys = jnp.concatenate([lo.astype(BF16), hi.astype(BF16)], axis=1)
    r_idx = lax.broadcasted_iota(jnp.int32, (tm, srows), 1)
    mix = jnp.zeros(x_ref.shape, F32)
    for k in range(TOP_K):
        pick = jnp.where(r_idx == ei_ref[:, TOP_K + k:TOP_K + k + 1], 1.0, 0.0).astype(BF16)
        mix = mix + pw_ref[:, k:k + 1] * jnp.dot(pick, ys, preferred_element_type=F32)
    o_ref[...] = x_ref[...] + mod_ref[5:6, :] * mix


def _combine(seg_src, seg_loff, seg_cnt, tile_rows, x, ei, pw, mod, y, seq):
    n, d = x.shape
    tm = TM_ROUTE
    tpb = seq // tm
    tok = lambda width: pl.BlockSpec((tm, width), lambda i, *_: (i, 0))
    return pl.pallas_call(
        functools.partial(_combine_kernel, tm=tm, srows=SORT_ROWS),
        out_shape=jax.ShapeDtypeStruct((n, d), F32),
        grid_spec=pltpu.PrefetchScalarGridSpec(
            num_scalar_prefetch=4,
            grid=(n // tm,),
            in_specs=[tok(d), tok(LANES), tok(LANES),
                      pl.BlockSpec((None, 6, d), lambda i, *_: (i // tpb, 0, 0)),
                      pl.BlockSpec(memory_space=pl.ANY)],
            out_specs=tok(d),
            scratch_shapes=[pltpu.VMEM((2, SORT_ROWS, d // 2), jnp.uint32), pltpu.SemaphoreType.DMA((2,))],
        ),
        compiler_params=_cparams("arbitrary"),
        name="moe_combine",
    )(seg_src, seg_loff, seg_cnt, tile_rows, x, ei, pw, mod, y)


def _pack_w_in(w_in):
    d = w_in.shape[0]
    w = MIX_W
    o_ckv = 3 * w + Q_LORA
    o_kr = o_ckv + KV_LORA
    o_su = o_kr + QK_ROPE
    o_rq = o_su + 2 * w
    o_gate = o_rq + 4 * w
    half = HEAD_DIM // 2
    perm = np.array([h * HEAD_DIM + part * half + i
                     for part in range(2) for h in range(N_GROUPS) for i in range(half)])
    kr = w_in[:, o_kr:o_kr + QK_ROPE]
    hr = QK_ROPE // 2
    z = lambda k: jnp.zeros((d, k), w_in.dtype)
    cols = [
        w_in[:, o_gate:o_gate + N_BRANCH * d],
        w_in[:, 0:3 * w + Q_LORA],
        w_in[:, o_rq:o_rq + w][:, perm], w_in[:, o_rq + w:o_rq + 2 * w][:, perm],
        w_in[:, o_rq + 2 * w:o_rq + 4 * w],
        w_in[:, o_su:o_su + 2 * w],
        w_in[:, o_ckv:o_ckv + KV_LORA],
        z(QK_NOPE), kr, z(LANES - QK_HEAD),
        z(QK_NOPE), kr[:, hr:], kr[:, :hr], z(LANES - QK_HEAD),
        z(N_IN - COL_KRB - LANES),
    ]
    return jnp.concatenate(cols, axis=1).astype(BF16)


def _swap_rope_halves(a):
    hr = QK_ROPE // 2
    return jnp.concatenate([a[..., :QK_NOPE], a[..., QK_NOPE + hr:QK_HEAD], a[..., QK_NOPE:QK_NOPE + hr],
                            a[..., QK_HEAD:]], axis=-1)


def _mla_params(cq_g, w_uq, ckv_g, w_ukv, qn_g, kn_g):
    pad = LANES - QK_HEAD
    wq = w_uq.reshape(Q_LORA, N_GROUPS, QK_HEAD)
    wq = jnp.pad(wq, ((0, 0), (0, 0), (0, pad)))
    wkv = w_ukv.reshape(KV_LORA, N_GROUPS, QK_NOPE + V_HEAD)
    wk = jnp.pad(wkv[:, :, :QK_NOPE], ((0, 0), (0, 0), (0, LANES - QK_NOPE)))
    wv = jnp.pad(wkv[:, :, QK_NOPE:], ((0, 0), (0, 0), (0, LANES - V_HEAD)))
    qg = jnp.pad(qn_g, (0, pad))[None, :]
    kg = jnp.pad(kn_g, (0, pad))[None, :]
    bound = (QK_HEAD ** 0.5 * LOG2_E) * jnp.max(jnp.abs(qn_g)) * jnp.max(jnp.abs(kn_g))
    static_shift = bound <= MAX_STATIC_SHIFT
    lane = jnp.arange(LANES)
    qaug = (lane == QK_HEAD).astype(F32)[None, :]
    kaug = qaug * jnp.where(static_shift, -bound, 0.0)
    vaug = jnp.tile((lane == V_HEAD).astype(F32), N_GROUPS)[None, :]
    params = {
        "cq_g": cq_g[None, :], "ckv_g": ckv_g[None, :],
        "wqa": wq.reshape(Q_LORA, -1).astype(BF16),
        "wqb": _swap_rope_halves(wq).reshape(Q_LORA, -1).astype(BF16),
        "wk": wk.reshape(KV_LORA, -1).astype(BF16),
        "wv": wv.reshape(KV_LORA, -1).astype(BF16),
        "qga": qg, "qgb": _swap_rope_halves(qg), "kga": kg, "kgb": _swap_rope_halves(kg),
        "qaug": qaug, "kaug": kaug, "vaug": vaug,
    }
    return params, static_shift


def _mixer_consts():
    h = jnp.arange(N_GROUPS, dtype=F32)
    log_gamma = jnp.log1p(-(2.0 ** (-5.0 - h)))
    pos = jnp.arange(CHUNK, dtype=F32)
    rel = pos[:, None] - pos[None, :]
    dec = jnp.where(rel >= 0, jnp.exp(log_gamma[:, None, None] * jnp.maximum(rel, 0.0)), 0.0)
    lane = np.arange(MIX_W)
    head_k = (lane % LANES) // (HEAD_DIM // 2)
    head_v = lane // HEAD_DIM
    lg_k = log_gamma[head_k]
    return {
        "dec": dec.reshape(N_GROUPS * CHUNK, CHUNK),
        "kdec": jnp.exp(lg_k[None, :] * (CHUNK - 1.0 - pos)[:, None]),
        "qdec": jnp.exp(lg_k[None, :] * (pos + 1.0)[:, None]),
        "cdec": jnp.broadcast_to(jnp.exp(lg_k * CHUNK)[:, None], (MIX_W, MIX_W)),
        "bd": jnp.asarray((head_k[:, None] == head_v[None, :]).astype(np.float32)),
        "gmat": jnp.asarray((head_v[:, None] == head_v[None, :]).astype(np.float32) / HEAD_DIM).astype(BF16),
        "mk": jnp.asarray((head_k[None, :] == np.arange(N_GROUPS)[:, None]).astype(np.float32)),
        "mv": jnp.asarray((head_v[None, :] == np.arange(N_GROUPS)[:, None]).astype(np.float32)),
    }


def _mixer_params(conv_w, gv_g, w_s, b_s, ret_g, w_branch, w_o):
    p = dict(_mixer_consts())
    ws = jnp.tril(w_s)
    p.update({
        "conv_w": conv_w,
        "gv_g": gv_g.reshape(1, MIX_W),
        "ws_cat": jnp.transpose(ws, (1, 0, 2)).reshape(CHUNK, N_GROUPS * CHUNK).astype(BF16),
        "bs_mat": jnp.repeat(b_s.T, HEAD_DIM, axis=1),
        "ret_g": ret_g.reshape(1, MIX_W),
        "w_branch": w_branch.astype(BF16),
        "w_o": w_o.astype(BF16),
    })
    return p


def _moe_layout(meta, tot, n_tiles):
    totals = tot[0, :N_EXPERTS]
    padded = ((totals + TG_MOE - 1) // TG_MOE) * TG_MOE
    ends = jnp.cumsum(padded)
    starts = ends - padded
    seg_cnt = meta[:, 0, :N_EXPERTS]
    seg_loff = meta[:, 2, :N_EXPERTS]
    seg_grouped = starts[None, :] + meta[:, 1, :N_EXPERTS]
    tile_start = jnp.arange(n_tiles, dtype=jnp.int32) * TG_MOE
    tile_e = jnp.sum((tile_start[:, None] >= ends[None, :]).astype(jnp.int32), axis=1)
    used = (tile_start < ends[-1]).astype(jnp.int32)
    last_e = jnp.sum((ends[-1] - 1 >= ends).astype(jnp.int32))
    tile_e = jnp.minimum(jnp.where(used == 1, tile_e, last_e), N_EXPERTS - 1)
    flat = lambda a: a.reshape(-1).astype(jnp.int32)
    gaps = jnp.concatenate([flat(jnp.stack([starts + totals, padded - totals], axis=1)),
                            flat(ends[-1:] // TG_MOE)])
    return flat(seg_grouped), flat(seg_loff), flat(seg_cnt), gaps, tile_e, used


def kernel(x, c, positions, norm1_g, norm2_g, ada_w, ada_b, w_in, conv_w, cq_g, w_uq, ckv_g, w_ukv, qn_g, kn_g, gv_g, w_s, b_s, ret_g, w_branch, w_o, ffn_w1, ffn_w3, ffn_w2, router_w, router_b, moe_w1, moe_w3, moe_w2):
    batch, seq, d = x.shape
    depth = ada_w.shape[0]
    n = batch * seq
    assert seq % max(TM_PROJ, TM_PREP, TQ_ATT, TM_MIX, TM_FFN, TM_ROUTE) == 0
    assert d // 2 % LANES == 0

    c_pad = jnp.pad(c, ((0, 8 - batch), (0, 0)))
    ada = _ada(c_pad, ada_w, ada_b)[:, :batch].reshape(depth, batch, 6, d)
    cosr, sinr, cm, sm = _rope_tables(positions.astype(F32).reshape(n, 1))

    xt = x.reshape(n, d)
    for l in range(depth):
        mod = ada[l]
        proj = _inproj(xt, mod, norm1_g[l][None, :], _pack_w_in(w_in[l]), seq)
        mla_p, static_shift = _mla_params(cq_g[l], w_uq[l], ckv_g[l], w_ukv[l], qn_g[l], kn_g[l])
        q, k, v = _mla_prep(proj, cm, sm, mla_p)
        y_mla = lax.cond(static_shift,
                         functools.partial(_flash, batch=batch, seq=seq, online_max=False),
                         functools.partial(_flash, batch=batch, seq=seq, online_max=True), q, k, v)
        mp = _mixer_params(conv_w[l], gv_g[l], w_s[l], b_s[l], ret_g[l], w_branch[l], w_o[l])
        xt = _mixers(proj, y_mla, xt, cosr, sinr, mod, mp, seq)
        g2n = norm2_g[l][None, :]
        if l % 2 == 0:
            i = l // 2
            xt = _dense_ffn(xt, mod, g2n, ffn_w1[i].astype(BF16), ffn_w3[i].astype(BF16),
                            ffn_w2[i].astype(BF16), seq)
        else:
            i = l // 2
            rw = jnp.pad(router_w[i], ((0, 0), (0, LANES - N_EXPERTS)))
            rw_hi = rw.astype(BF16)
            rw_pad = jnp.concatenate([rw_hi, (rw - rw_hi.astype(F32)).astype(BF16)], axis=1)
            rb_pad = jnp.pad(router_b[i], (0, LANES - N_EXPERTS), constant_values=-1e30)[None, :]
            hs, ei, pw, meta, tot = _router(xt, mod, g2n, rw_pad, rb_pad, seq)
            max_rows = n * TOP_K + N_EXPERTS * (n // TM_ROUTE) * (ROW_ALIGN - 1)
            n_tiles = -(-max_rows // TG_MOE) + N_EXPERTS
            seg_grouped, seg_loff, seg_cnt, gaps, tile_e, used = _moe_layout(meta, tot, n_tiles)
            tile_rows = jnp.sum(seg_cnt.reshape(-1, N_EXPERTS), axis=1)
            xs = _dispatch(seg_loff, seg_grouped, seg_cnt, tile_rows, gaps, hs, n_tiles)
            y = _expert_ffn(tile_e, used, xs, moe_w1[i].astype(BF16), moe_w3[i].astype(BF16),
                            moe_w2[i].astype(BF16))
            xt = _combine(seg_grouped, seg_loff, seg_cnt, tile_rows, xt, ei, pw, mod, y, seq)
    return xt.reshape(batch, seq, d)
```

```python
import functools

import jax
import jax.numpy as jnp
import numpy as np
from jax import lax
from jax.experimental import pallas as pl
from jax.experimental.pallas import tpu as pltpu

F32 = jnp.float32
BF16 = jnp.bfloat16
HIGHEST = lax.Precision.HIGHEST

HEAD_DIM = 64
N_GROUPS = 4
MIX_W = N_GROUPS * HEAD_DIM
N_BRANCH = 4
CONV_W = 3
Q_LORA = 256
KV_LORA = 128
QK_NOPE = 64
QK_ROPE = 32
QK_HEAD = QK_NOPE + QK_ROPE
V_HEAD = 64
CHUNK = 128
N_EXPERTS = 8
TOP_K = 2
ROPE_THETA = 10000.0
EPS = 1e-6
LOG2_E = 1.4426950408889634
MAX_STATIC_SHIFT = 50.0

LANES = 128
VMEM_LIMIT_BYTES = 56 * 1024 * 1024

COL_GATES = 0
COL_A = 4096
COL_CQ = COL_A + 3 * MIX_W
COL_R = 5120
COL_SU = 6144
COL_CKV = 6656
COL_KRA = 6784
N_IN = 6912

TM_PROJ = 512
TN_PROJ = 768
TM_PREP = 1024
TQ_ATT = 1024
TM_MIX = 512
TM_FFN = 512
TM_ROUTE = 512
ROW_ALIGN = 8
SORT_ROWS = TOP_K * TM_ROUTE + N_EXPERTS * ROW_ALIGN
TG_MOE = 512
DISPATCH_SLOTS = 4
DISPATCH_AHEAD = 2


def _cparams(*sem):
    return pltpu.CompilerParams(dimension_semantics=sem, vmem_limit_bytes=VMEM_LIMIT_BYTES)


def _sigmoid(x):
    return jnp.tanh(x * 0.5) * 0.5 + 0.5


def _group_mean(x, gmat_bf16):
    hi = x.astype(BF16)
    lo = (x - hi.astype(F32)).astype(BF16)
    return (jnp.dot(hi, gmat_bf16, preferred_element_type=F32)
            + jnp.dot(lo, gmat_bf16, preferred_element_type=F32))


def _pack_bf16_pair(lo, hi):
    lo_bits = lax.bitcast_convert_type(lo.astype(BF16).astype(F32), jnp.uint32)
    hi_bits = lax.bitcast_convert_type(hi.astype(BF16).astype(F32), jnp.uint32)
    return (lo_bits >> 16) | (hi_bits & jnp.uint32(0xFFFF0000))


def _unpack_bf16_pair(p):
    lo = lax.bitcast_convert_type(p << 16, F32)
    hi = lax.bitcast_convert_type(p & jnp.uint32(0xFFFF0000), F32)
    return lo, hi


def _norm_mod(x, g, shift, scale):
    y = x * lax.rsqrt(jnp.mean(x * x, axis=-1, keepdims=True) + EPS)
    return (y * g) * (1.0 + scale) + shift


def _ada_kernel(ct_ref, w_ref, b_ref, o_ref, *, batch):
    ct = ct_ref[...]
    cond = ct * _sigmoid(ct)
    w = w_ref[...]
    o_ref[...] = jnp.zeros(o_ref.shape, F32)
    for b in range(batch):
        o_ref[b:b + 1, :] = jnp.sum(w * cond[:, b:b + 1], axis=0, keepdims=True) + b_ref[...]


def _ada(c_t, ada_w, ada_b, batch):
    n_layer, d, d6 = ada_w.shape
    rows = c_t.shape[1]
    tn = 1024
    return pl.pallas_call(
        functools.partial(_ada_kernel, batch=batch),
        out_shape=jax.ShapeDtypeStruct((n_layer, rows, d6), F32),
        grid=(n_layer, d6 // tn),
        in_specs=[
            pl.BlockSpec((d, rows), lambda l, j: (0, 0)),
            pl.BlockSpec((None, d, tn), lambda l, j: (l, 0, j)),
            pl.BlockSpec((None, 1, tn), lambda l, j: (l, 0, j)),
        ],
        out_specs=pl.BlockSpec((None, rows, tn), lambda l, j: (l, 0, j)),
        compiler_params=_cparams("parallel", "parallel"),
        name="ada_mod",
    )(c_t, ada_w, ada_b.reshape(n_layer, 1, d6))


def _rope_kernel(pos_ref, inv_ref, cr_ref, sr_ref, cm_ref, sm_ref):
    half_r = HEAD_DIM // 2
    half_m = QK_ROPE // 2
    ang = pos_ref[...] * inv_ref[...]
    c = jnp.cos(ang)
    s = jnp.sin(ang)
    lane = lax.broadcasted_iota(jnp.int32, c.shape, 1)

    def tile_r(t):
        t = jnp.where(lane < half_r, t, 0.0)
        out = t
        for k in range(1, LANES // half_r):
            out = out + pltpu.roll(t, k * half_r, axis=1)
        return out

    cr_ref[...] = tile_r(c)
    sr_ref[...] = tile_r(s)
    first = jnp.logical_and(lane >= QK_NOPE, lane < QK_NOPE + half_m)
    second = jnp.logical_and(lane >= QK_NOPE + half_m, lane < QK_HEAD)
    c1, c2 = pltpu.roll(c, QK_NOPE - half_r, axis=1), pltpu.roll(c, QK_NOPE + half_m - half_r, axis=1)
    s1, s2 = pltpu.roll(s, QK_NOPE - half_r, axis=1), pltpu.roll(s, QK_NOPE + half_m - half_r, axis=1)
    cm_ref[...] = jnp.where(first, c1, jnp.where(second, c2, 1.0))
    sm_ref[...] = jnp.where(first, -s1, jnp.where(second, s2, 0.0))


def _rope_tables(pos_f):
    n = pos_f.shape[0]
    tm = 1024
    half_r = HEAD_DIM // 2
    half_m = QK_ROPE // 2
    inv_r = ROPE_THETA ** (-jnp.arange(half_r, dtype=F32) / half_r)
    inv_m = ROPE_THETA ** (-jnp.arange(half_m, dtype=F32) / half_m)
    inv = jnp.concatenate([inv_r, inv_m, jnp.zeros((LANES - half_r - half_m,), F32)])[None, :]
    tab = pl.BlockSpec((tm, LANES), lambda i: (i, 0))
    shape = jax.ShapeDtypeStruct((n, LANES), F32)
    return pl.pallas_call(
        _rope_kernel,
        out_shape=(shape, shape, shape, shape),
        grid=(n // tm,),
        in_specs=[pl.BlockSpec((tm, 1), lambda i: (i, 0)), pl.BlockSpec((1, LANES), lambda i: (0, 0))],
        out_specs=(tab, tab, tab, tab),
        compiler_params=_cparams("parallel"),
        name="rope_tables",
    )(pos_f, inv)


def _inproj_kernel(x_ref, mod_ref, g_ref, w_ref, o_ref):
    h = _norm_mod(x_ref[...], g_ref[...], mod_ref[0:1, :], mod_ref[1:2, :]).astype(BF16)
    for c in range(N_IN // TN_PROJ):
        cols = slice(c * TN_PROJ, (c + 1) * TN_PROJ)
        o_ref[:, cols] = jnp.dot(h, w_ref[:, cols], preferred_element_type=F32).astype(BF16)


def _resident(shape):
    return pl.BlockSpec(shape, lambda *_: (0,) * len(shape), pipeline_mode=pl.Buffered(1))


def _inproj(x, mod, g, w, seq):
    n, d = x.shape
    tm = TM_PROJ
    tpb = seq // tm
    return pl.pallas_call(
        _inproj_kernel,
        out_shape=jax.ShapeDtypeStruct((n, N_IN), BF16),
        grid=(n // tm,),
        in_specs=[
            pl.BlockSpec((tm, d), lambda i: (i, 0)),
            pl.BlockSpec((None, 6, d), lambda i: (i // tpb, 0, 0)),
            pl.BlockSpec((1, d), lambda i: (0, 0)),
            _resident((d, N_IN)),
        ],
        out_specs=pl.BlockSpec((tm, N_IN), lambda i: (i, 0)),
        compiler_params=_cparams("parallel"),
        name="in_proj",
    )(x, mod, g, w)


def _mla_prep_kernel(cq_ref, ckv_ref, kra_ref, cm_ref, sm_ref, cqg_ref, wqa_ref, wqb_ref,
                     ckvg_ref, wk_ref, wv_ref, qga_ref, qgb_ref, kga_ref, kgb_ref,
                     qaug_ref, kaug_ref, vaug_ref, swap_ref, q_ref, k_ref, v_ref):
    cq = cq_ref[...].astype(F32)
    cqn = (cq * lax.rsqrt(jnp.mean(cq * cq, axis=-1, keepdims=True) + EPS) * cqg_ref[...]).astype(BF16)
    qa = jnp.dot(cqn, wqa_ref[...], preferred_element_type=F32)
    qb = jnp.dot(cqn, wqb_ref[...], preferred_element_type=F32)
    ckv = ckv_ref[...].astype(F32)
    ckvn = (ckv * lax.rsqrt(jnp.mean(ckv * ckv, axis=-1, keepdims=True) + EPS) * ckvg_ref[...]).astype(BF16)
    ka = jnp.dot(ckvn, wk_ref[...], preferred_element_type=F32)
    v_ref[...] = (jnp.dot(ckvn, wv_ref[...], preferred_element_type=F32) + vaug_ref[...]).astype(BF16)
    kra = kra_ref[...].astype(F32)
    krb = jnp.dot(kra_ref[...], swap_ref[...], preferred_element_type=F32)
    cm = cm_ref[...]
    sm = sm_ref[...]
    scale = QK_HEAD ** -0.5 * LOG2_E
    for h in range(N_GROUPS):
        sl = slice(h * LANES, (h + 1) * LANES)
        qah, qbh = qa[:, sl], qb[:, sl]
        r = lax.rsqrt(jnp.sum(qah * qah, axis=-1, keepdims=True) * (1.0 / QK_HEAD) + EPS)
        q_rot = (qah * r) * qga_ref[...] * cm + (qbh * r) * qgb_ref[...] * sm
        q_ref[:, sl] = (q_rot * scale + qaug_ref[...]).astype(BF16)
        kah = ka[:, sl] + kra
        kbh = ka[:, sl] + krb
        r = lax.rsqrt(jnp.sum(kah * kah, axis=-1, keepdims=True) * (1.0 / QK_HEAD) + EPS)
        k_rot = (kah * r) * kga_ref[...] * cm + (kbh * r) * kgb_ref[...] * sm
        k_ref[:, sl] = (k_rot + kaug_ref[...]).astype(BF16)


def _mla_prep(proj, cm, sm, p):
    n = proj.shape[0]
    tm = TM_PREP
    hw = N_GROUPS * LANES

    def col(width, offset):
        return pl.BlockSpec((tm, width), lambda i: (i, offset // width))

    def full(a):
        return pl.BlockSpec(a.shape, lambda i: (0,) * a.ndim)

    weights = [p["cq_g"], p["wqa"], p["wqb"], p["ckv_g"], p["wk"], p["wv"],
               p["qga"], p["qgb"], p["kga"], p["kgb"], p["qaug"], p["kaug"], p["vaug"], p["swap"]]
    head_tile = pl.BlockSpec((tm, hw), lambda i: (i, 0))
    out = jax.ShapeDtypeStruct((n, hw), BF16)
    return pl.pallas_call(
        _mla_prep_kernel,
        out_shape=(out, out, out),
        grid=(n // tm,),
        in_specs=[col(Q_LORA, COL_CQ), col(KV_LORA, COL_CKV), col(LANES, COL_KRA),
                  pl.BlockSpec((tm, LANES), lambda i: (i, 0)), pl.BlockSpec((tm, LANES), lambda i: (i, 0))]
                 + [full(w) for w in weights],
        out_specs=(head_tile, head_tile, head_tile),
        compiler_params=_cparams("parallel"),
        name="mla_prep",
    )(proj, proj, proj, cm, sm, *weights)


def _flash_kernel(q_ref, k_ref, v_ref, o_ref, acc_scr, *rest, tq, online_max):
    i = pl.program_id(1)
    j = pl.program_id(2)

    @pl.when(j == 0)
    def _():
        acc_scr[...] = jnp.zeros(acc_scr.shape, F32)
        if online_max:
            rest[0][...] = jnp.full(rest[0].shape, -jnp.inf, F32)

    def block(q0, nq, nk, masked):
        rows = slice(q0, q0 + nq)
        if masked:
            row = lax.broadcasted_iota(jnp.int32, (nq, nk), 0) + q0
            col = lax.broadcasted_iota(jnp.int32, (nq, nk), 1)
            keep = col <= row
        for h in range(N_GROUPS):
            sl = slice(h * LANES, (h + 1) * LANES)
            s = lax.dot_general(q_ref[rows, sl], k_ref[0:nk, sl], (((1,), (1,)), ((), ())),
                                preferred_element_type=F32)
            if masked:
                s = jnp.where(keep, s, -jnp.inf)
            if online_max:
                m_scr = rest[0]
                m_prev = m_scr[h, rows]
                m_new = jnp.maximum(m_prev, jnp.max(s, axis=-1, keepdims=True))
                p = jnp.exp2(s - m_new).astype(BF16)
                acc_scr[h, rows] = jnp.exp2(m_prev - m_new) * acc_scr[h, rows] + jnp.dot(
                    p, v_ref[0:nk, sl], preferred_element_type=F32)
                m_scr[h, rows] = m_new
            else:
                acc_scr[h, rows] += jnp.dot(jnp.exp2(s).astype(BF16), v_ref[0:nk, sl],
                                            preferred_element_type=F32)

    @pl.when(j < i)
    def _():
        block(0, tq, tq, False)

    @pl.when(j == i)
    def _():
        block(0, tq // 2, tq // 2, True)
        block(tq // 2, tq // 2, tq, True)
        lane = lax.broadcasted_iota(jnp.int32, (tq, LANES), 1)
        for pr in range(N_GROUPS // 2):
            lo = acc_scr[2 * pr]
            hi = acc_scr[2 * pr + 1]
            lo = lo / lo[:, V_HEAD:V_HEAD + 1]
            hi = hi / hi[:, V_HEAD:V_HEAD + 1]
            both = jnp.where(lane < V_HEAD, lo, pltpu.roll(hi, V_HEAD, axis=1))
            o_ref[:, pr * LANES:(pr + 1) * LANES] = both.astype(BF16)


def _flash(q, k, v, batch, seq, online_max):
    n = q.shape[0]
    tq = TQ_ATT
    nq = seq // tq
    hw = N_GROUPS * LANES
    scratch = [pltpu.VMEM((N_GROUPS, tq, LANES), F32)]
    if online_max:
        scratch.append(pltpu.VMEM((N_GROUPS, tq, 1), F32))
    return pl.pallas_call(
        functools.partial(_flash_kernel, tq=tq, online_max=online_max),
        out_shape=jax.ShapeDtypeStruct((n, MIX_W), BF16),
        grid=(batch, nq, nq),
        in_specs=[
            pl.BlockSpec((tq, hw), lambda b, i, j: (b * nq + i, 0)),
            pl.BlockSpec((tq, hw), lambda b, i, j: (b * nq + jnp.minimum(j, i), 0)),
            pl.BlockSpec((tq, hw), lambda b, i, j: (b * nq + jnp.minimum(j, i), 0)),
        ],
        out_specs=pl.BlockSpec((tq, MIX_W), lambda b, i, j: (b * nq + i, 0)),
        scratch_shapes=scratch,
        compiler_params=_cparams("parallel", "parallel", "arbitrary"),
        name="mla_flash_online" if online_max else "mla_flash",
    )(q, k, v)


def _gelu_tanh(x):
    return jax.nn.gelu(x, approximate=True)


def _mix_kernel(gates_ref, a_ref, r_ref, su_ref, ymla_ref, x_ref, cos_ref, sin_ref, mod_ref,
                convw_ref, gvg_ref, wscat_ref, bsmat_ref, retg_ref, dec_ref, kdec_ref, qdec_ref,
                cdec_ref, bd_ref, gmat_ref, mk_ref, mv_ref, wb_ref, wo_ref,
                o_ref, carry_scr, state_scr, ysg_scr, yret_scr, *, tm, tpb):
    i = pl.program_id(0)

    @pl.when(i % tpb == 0)
    def _():
        carry_scr[...] = jnp.zeros(carry_scr.shape, F32)
        state_scr[...] = jnp.zeros(state_scr.shape, F32)

    w = MIX_W
    a_b = a_ref[:, 0:w].astype(F32)
    u = a_ref[:, w:2 * w].astype(F32) * a_ref[:, 2 * w:3 * w].astype(F32)
    rowi = lax.broadcasted_iota(jnp.int32, (tm, w), 0)
    prev1 = carry_scr[0:1, :]
    prev2 = carry_scr[1:2, :]
    u1 = jnp.where(rowi == 0, prev1, pltpu.roll(u, 1, axis=0))
    u2 = jnp.where(rowi == 0, prev2, jnp.where(rowi == 1, prev1, pltpu.roll(u, 2, axis=0)))
    carry_scr[0:1, :] = u[tm - 1:tm, :]
    carry_scr[1:2, :] = u[tm - 2:tm - 1, :]
    y_conv = a_b * (convw_ref[0:1, :] * u2 + convw_ref[1:2, :] * u1 + convw_ref[2:3, :] * u)

    gmat = gmat_ref[...]
    s_u = _gelu_tanh(su_ref[:, 0:w].astype(F32))
    s_v = _gelu_tanh(su_ref[:, w:2 * w].astype(F32))
    ms = _group_mean(s_v * s_v, gmat)
    vn = (s_v * lax.rsqrt(ms + EPS) * gvg_ref[...]).astype(BF16)

    cosr = cos_ref[...]
    sinr = sin_ref[...]

    def rot(t):
        t1, t2 = t[:, 0:LANES], t[:, LANES:2 * LANES]
        return jnp.concatenate([t1 * cosr - t2 * sinr, t2 * cosr + t1 * sinr], axis=-1)

    rq = rot(r_ref[:, 0:w].astype(F32))
    rk = rot(r_ref[:, w:2 * w].astype(F32)) * (HEAD_DIM ** -0.5)

    for c in range(tm // CHUNK):
        rows = slice(c * CHUNK, (c + 1) * CHUNK)
        vc = vn[rows, :]
        vbd = jnp.concatenate([vc * mv_ref[g:g + 1, :].astype(BF16) for g in range(N_GROUPS)], axis=0)
        mixed = jnp.dot(wscat_ref[...], vbd, preferred_element_type=F32) + bsmat_ref[...]
        ysg_scr[rows, :] = s_u[rows, :] * mixed

        qc = rq[rows, :]
        kc = rk[rows, :]
        kcb = kc.astype(BF16)
        vcb = r_ref[rows, 2 * w:3 * w]
        qstack = jnp.concatenate([(qc * mk_ref[h:h + 1, :]).astype(BF16) for h in range(N_GROUPS)], axis=0)
        sc = lax.dot_general(qstack, kcb, (((1,), (1,)), ((), ())), preferred_element_type=F32)
        sc = (sc * dec_ref[...]).astype(BF16)
        scat = jnp.concatenate([sc[h * CHUNK:(h + 1) * CHUNK, :] for h in range(N_GROUPS)], axis=1)
        vstack = jnp.concatenate([vcb * mv_ref[h:h + 1, :].astype(BF16) for h in range(N_GROUPS)], axis=0)
        o_c = jnp.dot(scat, vstack, preferred_element_type=F32)
        state = state_scr[...]
        o_c = o_c + jnp.dot((qc * qdec_ref[...]).astype(BF16), state.astype(BF16),
                            preferred_element_type=F32)
        kd_t = jnp.transpose(kc * kdec_ref[...]).astype(BF16)
        kv = jnp.dot(kd_t, vcb, preferred_element_type=F32)
        state_scr[...] = state * cdec_ref[...] + kv * bd_ref[...]
        yret_scr[rows, :] = o_c

    o_all = yret_scr[...]
    xc = o_all - _group_mean(o_all, gmat)
    var = _group_mean(xc * xc, gmat)
    r_g = r_ref[:, 3 * w:4 * w].astype(F32)
    y_ret = (r_g * _sigmoid(r_g)) * (xc * lax.rsqrt(var + EPS) * retg_ref[...])

    d = x_ref.shape[1]
    ys = (y_conv, ymla_ref[...], ysg_scr[...], y_ret)
    merged = jnp.zeros((tm, d), F32)
    for n in range(N_BRANCH):
        gate = _sigmoid(gates_ref[:, n * d:(n + 1) * d]).astype(F32)
        merged = merged + gate * jnp.dot(ys[n].astype(BF16), wb_ref[n], preferred_element_type=F32)
    out = jnp.dot(merged.astype(BF16), wo_ref[...], preferred_element_type=F32)
    o_ref[...] = x_ref[...] + mod_ref[2:3, :] * out


def _mixers(proj, ymla, x, cosr, sinr, mod, p, seq):
    n, d = x.shape
    tm = TM_MIX
    tpb = seq // tm

    def col(width, offset):
        return pl.BlockSpec((tm, width), lambda i: (i, offset // width))

    def full(a):
        return pl.BlockSpec(a.shape, lambda i: (0,) * a.ndim)

    consts = [p["conv_w"], p["gv_g"], p["ws_cat"], p["bs_mat"], p["ret_g"], p["dec"], p["kdec"],
              p["qdec"], p["cdec"], p["bd"], p["gmat"], p["mk"], p["mv"], p["w_branch"], p["w_o"]]
    return pl.pallas_call(
        functools.partial(_mix_kernel, tm=tm, tpb=tpb),
        out_shape=jax.ShapeDtypeStruct((n, d), F32),
        grid=(n // tm,),
        in_specs=[col(N_BRANCH * d, COL_GATES), col(4 * MIX_W, COL_A), col(4 * MIX_W, COL_R),
                  col(2 * MIX_W, COL_SU),
                  pl.BlockSpec((tm, MIX_W), lambda i: (i, 0)),
                  pl.BlockSpec((tm, d), lambda i: (i, 0)),
                  pl.BlockSpec((tm, LANES), lambda i: (i, 0)),
                  pl.BlockSpec((tm, LANES), lambda i: (i, 0)),
                  pl.BlockSpec((None, 6, d), lambda i: (i // tpb, 0, 0))]
                 + [full(c) for c in consts],
        out_specs=pl.BlockSpec((tm, d), lambda i: (i, 0)),
        scratch_shapes=[pltpu.VMEM((8, MIX_W), F32), pltpu.VMEM((MIX_W, MIX_W), F32),
                        pltpu.VMEM((tm, MIX_W), F32), pltpu.VMEM((tm, MIX_W), F32)],
        compiler_params=_cparams("arbitrary"),
        name="mixers_merge",
    )(proj, proj, proj, proj, ymla, x, cosr, sinr, mod, *consts)


def _ffn_kernel(x_ref, mod_ref, g_ref, w1_ref, w3_ref, w2_ref, o_ref):
    x = x_ref[...]
    h = _norm_mod(x, g_ref[...], mod_ref[3:4, :], mod_ref[4:5, :]).astype(BF16)
    a = jnp.dot(h, w1_ref[...], preferred_element_type=F32)
    b = jnp.dot(h, w3_ref[...], preferred_element_type=F32)
    hid = ((a * _sigmoid(a)) * b).astype(BF16)
    o_ref[...] = x + mod_ref[5:6, :] * jnp.dot(hid, w2_ref[...], preferred_element_type=F32)


def _dense_ffn(x, mod, g, w1, w3, w2, seq):
    n, d = x.shape
    dff = w1.shape[1]
    tm = TM_FFN
    tpb = seq // tm
    return pl.pallas_call(
        _ffn_kernel,
        out_shape=jax.ShapeDtypeStruct((n, d), F32),
        grid=(n // tm,),
        in_specs=[
            pl.BlockSpec((tm, d), lambda i: (i, 0)),
            pl.BlockSpec((None, 6, d), lambda i: (i // tpb, 0, 0)),
            pl.BlockSpec((1, d), lambda i: (0, 0)),
            _resident((d, dff)), _resident((d, dff)), _resident((dff, d)),
        ],
        out_specs=pl.BlockSpec((tm, d), lambda i: (i, 0)),
        compiler_params=_cparams("parallel"),
        name="dense_swiglu",
    )(x, mod, g, w1, w3, w2)


def _router_kernel(x_ref, mod_ref, g_ref, rw_ref, rb_ref, hs_ref, ei_ref, pw_ref, meta_ref, tot_ref,
                   carry_scr, *, tm, srows):
    i = pl.program_id(0)

    @pl.when(i == 0)
    def _():
        carry_scr[...] = jnp.zeros(carry_scr.shape, F32)

    h = _norm_mod(x_ref[...], g_ref[...], mod_ref[3:4, :], mod_ref[4:5, :])

    h_hi = h.astype(BF16)
    h_lo = (h - h_hi.astype(F32)).astype(BF16)
    hw = jnp.dot(h_hi, rw_ref[...], preferred_element_type=F32)
    logits = (hw[:, :LANES] + hw[:, LANES:] + jnp.dot(h_lo, rw_ref[:, :LANES], preferred_element_type=F32)
              + rb_ref[...])
    mx = jnp.max(logits, axis=-1, keepdims=True)
    ex = jnp.exp(logits - mx)
    probs = ex / jnp.sum(ex, axis=-1, keepdims=True)
    lane = lax.broadcasted_iota(jnp.int32, (tm, LANES), 1)
    valid = lane < N_EXPERTS
    probs = jnp.where(valid, probs, -1.0)
    m1 = jnp.max(probs, axis=-1, keepdims=True)
    i1 = jnp.min(jnp.where(probs == m1, lane, LANES), axis=-1, keepdims=True)
    rest = jnp.where(lane == i1, -1.0, probs)
    m2 = jnp.max(rest, axis=-1, keepdims=True)
    i2 = jnp.min(jnp.where(rest == m2, lane, LANES), axis=-1, keepdims=True)
    den = m1 + m2
    pw_ref[...] = jnp.where(lane == 0, m1 / den, jnp.where(lane == 1, m2 / den, 0.0))

    sel1 = lane == i1
    sel2 = lane == i2
    onehot = jnp.where(sel1, 1.0, 0.0) + jnp.where(sel2, 1.0, 0.0)
    r_i = lax.broadcasted_iota(jnp.int32, (tm, tm), 0)
    c_i = lax.broadcasted_iota(jnp.int32, (tm, tm), 1)
    tri = jnp.where(c_i < r_i, 1.0, 0.0).astype(BF16)
    before = jnp.dot(tri, onehot.astype(BF16), preferred_element_type=F32)
    cnt = jnp.sum(onehot, axis=0, keepdims=True)
    cnt_al = jnp.floor((cnt + (ROW_ALIGN - 1)) * (1.0 / ROW_ALIGN)) * ROW_ALIGN
    e_r = lax.broadcasted_iota(jnp.int32, (LANES, LANES), 0)
    e_c = lax.broadcasted_iota(jnp.int32, (LANES, LANES), 1)
    upper = jnp.where(e_r < e_c, 1.0, 0.0)
    loff = jnp.dot(jnp.broadcast_to(cnt_al, (8, LANES)), upper, precision=HIGHEST,
                   preferred_element_type=F32)[0:1, :]
    slot = loff + before
    slot1 = jnp.sum(jnp.where(sel1, slot, 0.0), axis=-1, keepdims=True).astype(jnp.int32)
    slot2 = jnp.sum(jnp.where(sel2, slot, 0.0), axis=-1, keepdims=True).astype(jnp.int32)
    ei = jnp.where(lane == 0, i1, jnp.where(lane == 1, i2, 0))
    ei_ref[...] = jnp.where(lane == 2, slot1, jnp.where(lane == 3, slot2, ei))

    r_idx = lax.broadcasted_iota(jnp.int32, (tm, srows), 1)
    place = jnp.where(r_idx == slot1, 1.0, jnp.where(r_idx == slot2, 1.0, 0.0)).astype(BF16)
    hs = lax.dot_general(place, h.astype(BF16), (((0,), (0,)), ((), ())), preferred_element_type=F32)
    half = hs.shape[1] // 2
    hs_ref[...] = _pack_bf16_pair(hs[:, :half], hs[:, half:])

    carry = carry_scr[0:1, :]
    mrow = lax.broadcasted_iota(jnp.int32, (8, LANES), 0)
    meta = jnp.where(mrow == 0, cnt_al, jnp.where(mrow == 1, carry, jnp.where(mrow == 2, loff, 0.0)))
    meta_ref[...] = meta.astype(jnp.int32)
    carry_scr[0:1, :] = carry + cnt_al
    tot_ref[...] = jnp.broadcast_to(carry + cnt_al, tot_ref.shape).astype(jnp.int32)


def _router(x, mod, g, rw_pad, rb_pad, seq):
    n, d = x.shape
    tm = TM_ROUTE
    tpb = seq // tm
    nt = n // tm
    return pl.pallas_call(
        functools.partial(_router_kernel, tm=tm, srows=SORT_ROWS),
        out_shape=(jax.ShapeDtypeStruct((nt * SORT_ROWS, d // 2), jnp.uint32),
                   jax.ShapeDtypeStruct((n, LANES), jnp.int32),
                   jax.ShapeDtypeStruct((n, LANES), F32),
                   jax.ShapeDtypeStruct((nt, 8, LANES), jnp.int32),
                   jax.ShapeDtypeStruct((8, LANES), jnp.int32)),
        grid=(nt,),
        in_specs=[
            pl.BlockSpec((tm, d), lambda i: (i, 0)),
            pl.BlockSpec((None, 6, d), lambda i: (i // tpb, 0, 0)),
            pl.BlockSpec((1, d), lambda i: (0, 0)),
            pl.BlockSpec((d, 2 * LANES), lambda i: (0, 0)),
            pl.BlockSpec((1, LANES), lambda i: (0, 0)),
        ],
        out_specs=(pl.BlockSpec((SORT_ROWS, d // 2), lambda i: (i, 0)),
                   pl.BlockSpec((tm, LANES), lambda i: (i, 0)),
                   pl.BlockSpec((tm, LANES), lambda i: (i, 0)),
                   pl.BlockSpec((None, 8, LANES), lambda i: (i, 0, 0)),
                   pl.BlockSpec((8, LANES), lambda i: (0, 0))),
        scratch_shapes=[pltpu.VMEM((8, LANES), F32)],
        compiler_params=_cparams("arbitrary"),
        name="router_top2",
    )(x, mod, g, rw_pad, rb_pad)


def _segment_copy(src_hbm, dst_hbm, src_row, dst_row, n_rows, sem):
    src_row = pl.multiple_of(src_row, ROW_ALIGN)
    dst_row = pl.multiple_of(dst_row, ROW_ALIGN)
    n_rows = pl.multiple_of(n_rows, ROW_ALIGN)
    return pltpu.make_async_copy(src_hbm.at[pl.ds(src_row, n_rows)], dst_hbm.at[pl.ds(dst_row, n_rows)], sem)


def _dispatch_kernel(loff_ref, dst_ref, cnt_ref, rows_ref, gap_ref, hs_ref, xs_ref, buf, zbuf, sem_in, sem_out,
                     sem_zero, *, n_tiles, srows, n_out_tiles):
    zbuf[...] = jnp.zeros(zbuf.shape, zbuf.dtype)
    tg = zbuf.shape[0]
    first_unused = gap_ref[2 * N_EXPERTS]

    def zero_gap(e):
        return _segment_copy(zbuf, xs_ref, 0, gap_ref[2 * e], gap_ref[2 * e + 1], sem_zero)

    def zero_tile(t):
        return pltpu.make_async_copy(zbuf, xs_ref.at[pl.ds(pl.multiple_of(t * tg, ROW_ALIGN), tg)], sem_zero)

    def for_each_zero_copy(action):
        for e in range(N_EXPERTS):
            @pl.when(gap_ref[2 * e + 1] > 0)
            def _():
                action(zero_gap(e))

        def tail(t, carry):
            action(zero_tile(t))
            return carry

        lax.fori_loop(first_unused, n_out_tiles, tail, 0)

    for_each_zero_copy(lambda copy: copy.start())

    def fetch(t):
        slot = t % DISPATCH_SLOTS
        return pltpu.make_async_copy(hs_ref.at[pl.ds(pl.multiple_of(t * srows, ROW_ALIGN), srows)],
                                     buf.at[slot], sem_in.at[slot])

    def drain(t):
        slot = t % DISPATCH_SLOTS
        n_rows = rows_ref[t]

        @pl.when(n_rows > 0)
        def _():
            _segment_copy(buf.at[slot], xs_ref, 0, 0, n_rows, sem_out.at[slot]).wait()

    for t in range(DISPATCH_AHEAD):
        fetch(t).start()

    def body(t, carry):
        slot = t % DISPATCH_SLOTS
        fetch(t).wait()
        for e in range(N_EXPERTS):
            s = t * N_EXPERTS + e
            n_rows = cnt_ref[s]

            @pl.when(n_rows > 0)
            def _():
                _segment_copy(buf.at[slot], xs_ref, loff_ref[s], dst_ref[s], n_rows, sem_out.at[slot]).start()

        @pl.when(t + DISPATCH_AHEAD < n_tiles)
        def _():
            @pl.when(t + DISPATCH_AHEAD >= DISPATCH_SLOTS)
            def _():
                drain(t + DISPATCH_AHEAD - DISPATCH_SLOTS)

            fetch(t + DISPATCH_AHEAD).start()

        return carry

    lax.fori_loop(0, n_tiles, body, 0)
    for t in range(max(n_tiles - DISPATCH_SLOTS, 0), n_tiles):
        drain(t)
    for_each_zero_copy(lambda copy: copy.wait())


def _dispatch(seg_loff, seg_dst, seg_cnt, tile_rows, gaps, hs, n_out_tiles):
    n_tiles = tile_rows.shape[0]
    srows = hs.shape[0] // n_tiles
    assert n_tiles >= DISPATCH_SLOTS
    return pl.pallas_call(
        functools.partial(_dispatch_kernel, n_tiles=n_tiles, srows=srows, n_out_tiles=n_out_tiles),
        out_shape=jax.ShapeDtypeStruct((n_out_tiles * TG_MOE, hs.shape[1]), hs.dtype),
        grid_spec=pltpu.PrefetchScalarGridSpec(
            num_scalar_prefetch=5,
            grid=(1,),
            in_specs=[pl.BlockSpec(memory_space=pl.ANY)],
            out_specs=pl.BlockSpec(memory_space=pl.ANY),
            scratch_shapes=[pltpu.VMEM((DISPATCH_SLOTS, srows, hs.shape[1]), hs.dtype),
                            pltpu.VMEM((TG_MOE, hs.shape[1]), hs.dtype),
                            pltpu.SemaphoreType.DMA((DISPATCH_SLOTS,)),
                            pltpu.SemaphoreType.DMA((DISPATCH_SLOTS,)),
                            pltpu.SemaphoreType.DMA],
        ),
        compiler_params=_cparams("arbitrary"),
        name="moe_dispatch",
    )(seg_loff, seg_dst, seg_cnt, tile_rows, gaps, hs)


def _expert_kernel(te_ref, used_ref, xs_ref, w1_ref, w3_ref, w2_ref, y_ref):
    t = pl.program_id(0)
    del te_ref

    @pl.when(used_ref[t] == 1)
    def _():
        lo, hi = _unpack_bf16_pair(xs_ref[...])
        h = jnp.concatenate([lo.astype(BF16), hi.astype(BF16)], axis=1)
        a = jnp.dot(h, w1_ref[...], preferred_element_type=F32)
        b = jnp.dot(h, w3_ref[...], preferred_element_type=F32)
        hid = ((a * _sigmoid(a)) * b).astype(BF16)
        acc = jnp.dot(hid, w2_ref[...], preferred_element_type=F32)
        half = acc.shape[1] // 2
        y_ref[...] = _pack_bf16_pair(acc[:, :half], acc[:, half:])

    @pl.when(used_ref[t] == 0)
    def _():
        y_ref[...] = jnp.zeros(y_ref.shape, y_ref.dtype)


def _expert_ffn(tile_e, tile_used, xs, w1, w3, w2):
    rows, half = xs.shape
    d = 2 * half
    dff = w1.shape[2]
    tg = TG_MOE

    def weight(shape):
        return pl.BlockSpec((None,) + shape, lambda t, te, us: (te[t], 0, 0), pipeline_mode=pl.Buffered(1))

    return pl.pallas_call(
        _expert_kernel,
        out_shape=jax.ShapeDtypeStruct((rows, half), jnp.uint32),
        grid_spec=pltpu.PrefetchScalarGridSpec(
            num_scalar_prefetch=2,
            grid=(rows // tg,),
            in_specs=[pl.BlockSpec((tg, half), lambda t, te, us: (t, 0)),
                      weight((d, dff)), weight((d, dff)), weight((dff, d))],
            out_specs=pl.BlockSpec((tg, half), lambda t, te, us: (t, 0)),
        ),
        compiler_params=_cparams("arbitrary"),
        name="expert_swiglu",
    )(tile_e, tile_used, xs, w1, w3, w2)


def _combine_kernel(src_ref, loff_ref, cnt_ref, rows_ref, x_ref, ei_ref, pw_ref, mod_ref, y_ref, o_ref,
                    ybuf, sem, *, tm, srows):
    i = pl.program_id(0)
    slot = i % 2

    def fetch(tile, into):
        ybuf[into] = jnp.zeros(ybuf.shape[1:], ybuf.dtype)
        for e in range(N_EXPERTS):
            s = tile * N_EXPERTS + e
            n_rows = cnt_ref[s]

            @pl.when(n_rows > 0)
            def _():
                _segment_copy(y_ref, ybuf.at[into], src_ref[s], loff_ref[s], n_rows, sem.at[into]).start()

    @pl.when(i == 0)
    def _():
        fetch(i, slot)

    @pl.when(i + 1 < pl.num_programs(0))
    def _():
        fetch(i + 1, 1 - slot)

    @pl.when(rows_ref[i] > 0)
    def _():
        _segment_copy(y_ref, ybuf.at[slot], 0, 0, rows_ref[i], sem.at[slot]).wait()

    lo, hi = _unpack_bf16_pair(ybuf[slot])
    ys = jnp.concatenate([lo.astype(BF16), hi.astype(BF16)], axis=1)
    r_idx = lax.broadcasted_iota(jnp.int32, (tm, srows), 1)
    mix = jnp.zeros(x_ref.shape, F32)
    for k in range(TOP_K):
        pick = jnp.where(r_idx == ei_ref[:, TOP_K + k:TOP_K + k + 1], 1.0, 0.0).astype(BF16)
        mix = mix + pw_ref[:, k:k + 1] * jnp.dot(pick, ys, preferred_element_type=F32)
    o_ref[...] = x_ref[...] + mod_ref[5:6, :] * mix


def _combine(seg_src, seg_loff, seg_cnt, tile_rows, x, ei, pw, mod, y, seq):
    n, d = x.shape
    tm = TM_ROUTE
    tpb = seq // tm
    tok = lambda width: pl.BlockSpec((tm, width), lambda i, *_: (i, 0))
    return pl.pallas_call(
        functools.partial(_combine_kernel, tm=tm, srows=SORT_ROWS),
        out_shape=jax.ShapeDtypeStruct((n, d), F32),
        grid_spec=pltpu.PrefetchScalarGridSpec(
            num_scalar_prefetch=4,
            grid=(n // tm,),
            in_specs=[tok(d), tok(LANES), tok(LANES),
                      pl.BlockSpec((None, 6, d), lambda i, *_: (i // tpb, 0, 0)),
                      pl.BlockSpec(memory_space=pl.ANY)],
            out_specs=tok(d),
            scratch_shapes=[pltpu.VMEM((2, SORT_ROWS, d // 2), jnp.uint32), pltpu.SemaphoreType.DMA((2,))],
        ),
        compiler_params=_cparams("arbitrary"),
        name="moe_combine",
    )(seg_src, seg_loff, seg_cnt, tile_rows, x, ei, pw, mod, y)


def _pack_w_in(w_in):
    d = w_in.shape[0]
    w = MIX_W
    o_ckv = 3 * w + Q_LORA
    o_kr = o_ckv + KV_LORA
    o_su = o_kr + QK_ROPE
    o_rq = o_su + 2 * w
    o_gate = o_rq + 4 * w
    half = HEAD_DIM // 2
    perm = np.array([h * HEAD_DIM + part * half + i
                     for part in range(2) for h in range(N_GROUPS) for i in range(half)])
    kr = w_in[:, o_kr:o_kr + QK_ROPE]
    z = lambda k: jnp.zeros((d, k), w_in.dtype)
    cols = [
        w_in[:, o_gate:o_gate + N_BRANCH * d],
        w_in[:, 0:3 * w + Q_LORA],
        w_in[:, o_rq:o_rq + w][:, perm], w_in[:, o_rq + w:o_rq + 2 * w][:, perm],
        w_in[:, o_rq + 2 * w:o_rq + 4 * w],
        w_in[:, o_su:o_su + 2 * w],
        w_in[:, o_ckv:o_ckv + KV_LORA],
        z(QK_NOPE), kr, z(LANES - QK_HEAD),
    ]
    return jnp.concatenate(cols, axis=1).astype(BF16)


def _swap_rope_halves(a):
    hr = QK_ROPE // 2
    return jnp.concatenate([a[..., :QK_NOPE], a[..., QK_NOPE + hr:QK_HEAD], a[..., QK_NOPE:QK_NOPE + hr],
                            a[..., QK_HEAD:]], axis=-1)


def _mla_params(cq_g, w_uq, ckv_g, w_ukv, qn_g, kn_g):
    pad = LANES - QK_HEAD
    wq = w_uq.reshape(Q_LORA, N_GROUPS, QK_HEAD)
    wq = jnp.pad(wq, ((0, 0), (0, 0), (0, pad)))
    wkv = w_ukv.reshape(KV_LORA, N_GROUPS, QK_NOPE + V_HEAD)
    wk = jnp.pad(wkv[:, :, :QK_NOPE], ((0, 0), (0, 0), (0, LANES - QK_NOPE)))
    wv = jnp.pad(wkv[:, :, QK_NOPE:], ((0, 0), (0, 0), (0, LANES - V_HEAD)))
    qg = jnp.pad(qn_g, (0, pad))[None, :]
    kg = jnp.pad(kn_g, (0, pad))[None, :]
    bound = (QK_HEAD ** 0.5 * LOG2_E) * jnp.max(jnp.abs(qn_g)) * jnp.max(jnp.abs(kn_g))
    static_shift = bound <= MAX_STATIC_SHIFT
    lane = jnp.arange(LANES)
    qaug = (lane == QK_HEAD).astype(F32)[None, :]
    kaug = qaug * jnp.where(static_shift, -bound, 0.0)
    vaug = jnp.tile((lane == V_HEAD).astype(F32), N_GROUPS)[None, :]
    params = {
        "cq_g": cq_g[None, :], "ckv_g": ckv_g[None, :],
        "wqa": wq.reshape(Q_LORA, -1).astype(BF16),
        "wqb": _swap_rope_halves(wq).reshape(Q_LORA, -1).astype(BF16),
        "wk": wk.reshape(KV_LORA, -1).astype(BF16),
        "wv": wv.reshape(KV_LORA, -1).astype(BF16),
        "qga": qg, "qgb": _swap_rope_halves(qg), "kga": kg, "kgb": _swap_rope_halves(kg),
        "qaug": qaug, "kaug": kaug, "vaug": vaug,
        "swap": (_swap_rope_halves(lane[None, :])[0][None, :] == lane[:, None]).astype(BF16),
    }
    return params, static_shift


def _mixer_consts():
    h = jnp.arange(N_GROUPS, dtype=F32)
    log_gamma = jnp.log1p(-(2.0 ** (-5.0 - h)))
    pos = jnp.arange(CHUNK, dtype=F32)
    rel = pos[:, None] - pos[None, :]
    dec = jnp.where(rel >= 0, jnp.exp(log_gamma[:, None, None] * jnp.maximum(rel, 0.0)), 0.0)
    lane = np.arange(MIX_W)
    head_k = (lane % LANES) // (HEAD_DIM // 2)
    head_v = lane // HEAD_DIM
    lg_k = log_gamma[head_k]
    return {
        "dec": dec.reshape(N_GROUPS * CHUNK, CHUNK),
        "kdec": jnp.exp(lg_k[None, :] * (CHUNK - 1.0 - pos)[:, None]),
        "qdec": jnp.exp(lg_k[None, :] * (pos + 1.0)[:, None]),
        "cdec": jnp.broadcast_to(jnp.exp(lg_k * CHUNK)[:, None], (MIX_W, MIX_W)),
        "bd": jnp.asarray((head_k[:, None] == head_v[None, :]).astype(np.float32)),
        "gmat": jnp.asarray((head_v[:, None] == head_v[None, :]).astype(np.float32) / HEAD_DIM).astype(BF16),
        "mk": jnp.asarray((head_k[None, :] == np.arange(N_GROUPS)[:, None]).astype(np.float32)),
        "mv": jnp.asarray((head_v[None, :] == np.arange(N_GROUPS)[:, None]).astype(np.float32)),
    }


def _mixer_params(conv_w, gv_g, w_s, b_s, ret_g, w_branch, w_o):
    p = dict(_mixer_consts())
    ws = jnp.tril(w_s)
    p.update({
        "conv_w": conv_w,
        "gv_g": gv_g.reshape(1, MIX_W),
        "ws_cat": jnp.transpose(ws, (1, 0, 2)).reshape(CHUNK, N_GROUPS * CHUNK).astype(BF16),
        "bs_mat": jnp.repeat(b_s.T, HEAD_DIM, axis=1),
        "ret_g": ret_g.reshape(1, MIX_W),
        "w_branch": w_branch.astype(BF16),
        "w_o": w_o.astype(BF16),
    })
    return p


def _moe_layout(meta, tot, n_tiles):
    totals = tot[0, :N_EXPERTS]
    padded = ((totals + TG_MOE - 1) // TG_MOE) * TG_MOE
    ends = jnp.cumsum(padded)
    starts = ends - padded
    seg_cnt = meta[:, 0, :N_EXPERTS]
    seg_loff = meta[:, 2, :N_EXPERTS]
    seg_grouped = starts[None, :] + meta[:, 1, :N_EXPERTS]
    tile_start = jnp.arange(n_tiles, dtype=jnp.int32) * TG_MOE
    tile_e = jnp.sum((tile_start[:, None] >= ends[None, :]).astype(jnp.int32), axis=1)
    used = (tile_start < ends[-1]).astype(jnp.int32)
    last_e = jnp.sum((ends[-1] - 1 >= ends).astype(jnp.int32))
    tile_e = jnp.minimum(jnp.where(used == 1, tile_e, last_e), N_EXPERTS - 1)
    flat = lambda a: a.reshape(-1).astype(jnp.int32)
    gaps = jnp.concatenate([flat(jnp.stack([starts + totals, padded - totals], axis=1)),
                            flat(ends[-1:] // TG_MOE)])
    return flat(seg_grouped), flat(seg_loff), flat(seg_cnt), gaps, tile_e, used


def kernel(x, c, positions, norm1_g, norm2_g, ada_w, ada_b, w_in, conv_w, cq_g, w_uq, ckv_g, w_ukv, qn_g, kn_g, gv_g, w_s, b_s, ret_g, w_branch, w_o, ffn_w1, ffn_w3, ffn_w2, router_w, router_b, moe_w1, moe_w3, moe_w2):
    batch, seq, d = x.shape
    depth = ada_w.shape[0]
    n = batch * seq
    assert seq % max(TM_PROJ, TM_PREP, TQ_ATT, TM_MIX, TM_FFN, TM_ROUTE) == 0
    assert d // 2 % LANES == 0

    c_t = jnp.pad(c, ((0, 8 - batch), (0, 0))).T
    ada = _ada(c_t, ada_w, ada_b, batch)[:, :batch].reshape(depth, batch, 6, d)
    cosr, sinr, cm, sm = _rope_tables(positions.astype(F32).reshape(n, 1))

    xt = x.reshape(n, d)
    for l in range(depth):
        mod = ada[l]
        proj = _inproj(xt, mod, norm1_g[l][None, :], _pack_w_in(w_in[l]), seq)
        mla_p, static_shift = _mla_params(cq_g[l], w_uq[l], ckv_g[l], w_ukv[l], qn_g[l], kn_g[l])
        q, k, v = _mla_prep(proj, cm, sm, mla_p)
        y_mla = lax.cond(static_shift,
                         functools.partial(_flash, batch=batch, seq=seq, online_max=False),
                         functools.partial(_flash, batch=batch, seq=seq, online_max=True), q, k, v)
        mp = _mixer_params(conv_w[l], gv_g[l], w_s[l], b_s[l], ret_g[l], w_branch[l], w_o[l])
        xt = _mixers(proj, y_mla, xt, cosr, sinr, mod, mp, seq)
        g2n = norm2_g[l][None, :]
        if l % 2 == 0:
            i = l // 2
            xt = _dense_ffn(xt, mod, g2n, ffn_w1[i].astype(BF16), ffn_w3[i].astype(BF16),
                            ffn_w2[i].astype(BF16), seq)
        else:
            i = l // 2
            rw = jnp.pad(router_w[i], ((0, 0), (0, LANES - N_EXPERTS)))
            rw_hi = rw.astype(BF16)
            rw_pad = jnp.concatenate([rw_hi, (rw - rw_hi.astype(F32)).astype(BF16)], axis=1)
            rb_pad = jnp.pad(router_b[i], (0, LANES - N_EXPERTS), constant_values=-1e30)[None, :]
            hs, ei, pw, meta, tot = _router(xt, mod, g2n, rw_pad, rb_pad, seq)
            max_rows = n * TOP_K + N_EXPERTS * (n // TM_ROUTE) * (ROW_ALIGN - 1)
            n_tiles = -(-max_rows // TG_MOE) + N_EXPERTS
            seg_grouped, seg_loff, seg_cnt, gaps, tile_e, used = _moe_layout(meta, tot, n_tiles)
            tile_rows = jnp.sum(seg_cnt.reshape(-1, N_EXPERTS), axis=1)
            xs = _dispatch(seg_loff, seg_grouped, seg_cnt, tile_rows, gaps, hs, n_tiles)
            y = _expert_ffn(tile_e, used, xs, moe_w1[i].astype(BF16), moe_w3[i].astype(BF16),
                            moe_w2[i].astype(BF16))
            xt = _combine(seg_grouped, seg_loff, seg_cnt, tile_rows, xt, ei, pw, mod, y, seq)
    return xt.reshape(batch, seq, d)
```

```python
import functools

import jax
import jax.numpy as jnp
import numpy as np
from jax import lax
from jax.experimental import pallas as pl
from jax.experimental.pallas import tpu as pltpu

F32 = jnp.float32
BF16 = jnp.bfloat16
HIGHEST = lax.Precision.HIGHEST

HEAD_DIM = 64
N_GROUPS = 4
MIX_W = N_GROUPS * HEAD_DIM
N_BRANCH = 4
CONV_W = 3
Q_LORA = 256
KV_LORA = 128
QK_NOPE = 64
QK_ROPE = 32
QK_HEAD = QK_NOPE + QK_ROPE
V_HEAD = 64
CHUNK = 128
N_EXPERTS = 8
TOP_K = 2
ROPE_THETA = 10000.0
EPS = 1e-6
LOG2_E = 1.4426950408889634
MAX_STATIC_SHIFT = 50.0

LANES = 128
VMEM_LIMIT_BYTES = 56 * 1024 * 1024

COL_GATES = 0
COL_A = 4096
COL_CQ = COL_A + 3 * MIX_W
COL_R = 5120
COL_SU = 6144
COL_CKV = 6656
COL_KRA = 6784
N_IN = 6912

TM_PROJ = 512
TN_PROJ = 768
TM_PREP = 1024
TQ_ATT = 1024
TM_MIX = 512
TM_FFN = 512
TM_ROUTE = 512
ROW_ALIGN = 8
SORT_ROWS = TOP_K * TM_ROUTE + N_EXPERTS * ROW_ALIGN
TG_MOE = 512
DISPATCH_SLOTS = 4
DISPATCH_AHEAD = 2
W_CHUNK = 256


def _cparams(*sem):
    return pltpu.CompilerParams(dimension_semantics=sem, vmem_limit_bytes=VMEM_LIMIT_BYTES)


def _sigmoid(x):
    return jnp.tanh(x * 0.5) * 0.5 + 0.5


def _group_mean(x, gmat_bf16):
    hi = x.astype(BF16)
    lo = (x - hi.astype(F32)).astype(BF16)
    return (jnp.dot(hi, gmat_bf16, preferred_element_type=F32)
            + jnp.dot(lo, gmat_bf16, preferred_element_type=F32))


def _pack_bf16_pair(lo, hi):
    lo_bits = lax.bitcast_convert_type(lo.astype(BF16).astype(F32), jnp.uint32)
    hi_bits = lax.bitcast_convert_type(hi.astype(BF16).astype(F32), jnp.uint32)
    return (lo_bits >> 16) | (hi_bits & jnp.uint32(0xFFFF0000))


def _unpack_bf16_pair(p):
    lo = lax.bitcast_convert_type(p << 16, F32)
    hi = lax.bitcast_convert_type(p & jnp.uint32(0xFFFF0000), F32)
    return lo, hi


def _norm_mod(x, g, shift, scale):
    y = x * lax.rsqrt(jnp.mean(x * x, axis=-1, keepdims=True) + EPS)
    return (y * g) * (1.0 + scale) + shift


def _ada_kernel(ct_ref, w_ref, b_ref, o_ref, *, batch):
    ct = ct_ref[...]
    cond = ct * _sigmoid(ct)
    w = w_ref[...]
    o_ref[...] = jnp.zeros(o_ref.shape, F32)
    for b in range(batch):
        o_ref[b:b + 1, :] = jnp.sum(w * cond[:, b:b + 1], axis=0, keepdims=True) + b_ref[...]


def _ada(c_t, ada_w, ada_b, batch):
    n_layer, d, d6 = ada_w.shape
    rows = c_t.shape[1]
    tn = 1024
    return pl.pallas_call(
        functools.partial(_ada_kernel, batch=batch),
        out_shape=jax.ShapeDtypeStruct((n_layer, rows, d6), F32),
        grid=(n_layer, d6 // tn),
        in_specs=[
            pl.BlockSpec((d, rows), lambda l, j: (0, 0)),
            pl.BlockSpec((None, d, tn), lambda l, j: (l, 0, j)),
            pl.BlockSpec((None, 1, tn), lambda l, j: (l, 0, j)),
        ],
        out_specs=pl.BlockSpec((None, rows, tn), lambda l, j: (l, 0, j)),
        compiler_params=_cparams("parallel", "parallel"),
        name="ada_mod",
    )(c_t, ada_w, ada_b.reshape(n_layer, 1, d6))


def _rope_kernel(pos_ref, inv_ref, cr_ref, sr_ref, cm_ref, sm_ref):
    half_r = HEAD_DIM // 2
    half_m = QK_ROPE // 2
    ang = pos_ref[...] * inv_ref[...]
    c = jnp.cos(ang)
    s = jnp.sin(ang)
    lane = lax.broadcasted_iota(jnp.int32, c.shape, 1)

    def tile_r(t):
        t = jnp.where(lane < half_r, t, 0.0)
        out = t
        for k in range(1, LANES // half_r):
            out = out + pltpu.roll(t, k * half_r, axis=1)
        return out

    cr_ref[...] = tile_r(c)
    sr_ref[...] = tile_r(s)
    first = jnp.logical_and(lane >= QK_NOPE, lane < QK_NOPE + half_m)
    second = jnp.logical_and(lane >= QK_NOPE + half_m, lane < QK_HEAD)
    c1, c2 = pltpu.roll(c, QK_NOPE - half_r, axis=1), pltpu.roll(c, QK_NOPE + half_m - half_r, axis=1)
    s1, s2 = pltpu.roll(s, QK_NOPE - half_r, axis=1), pltpu.roll(s, QK_NOPE + half_m - half_r, axis=1)
    cm_ref[...] = jnp.where(first, c1, jnp.where(second, c2, 1.0))
    sm_ref[...] = jnp.where(first, -s1, jnp.where(second, s2, 0.0))


def _rope_tables(pos_f):
    n = pos_f.shape[0]
    tm = 1024
    half_r = HEAD_DIM // 2
    half_m = QK_ROPE // 2
    inv_r = ROPE_THETA ** (-jnp.arange(half_r, dtype=F32) / half_r)
    inv_m = ROPE_THETA ** (-jnp.arange(half_m, dtype=F32) / half_m)
    inv = jnp.concatenate([inv_r, inv_m, jnp.zeros((LANES - half_r - half_m,), F32)])[None, :]
    tab = pl.BlockSpec((tm, LANES), lambda i: (i, 0))
    shape = jax.ShapeDtypeStruct((n, LANES), F32)
    return pl.pallas_call(
        _rope_kernel,
        out_shape=(shape, shape, shape, shape),
        grid=(n // tm,),
        in_specs=[pl.BlockSpec((tm, 1), lambda i: (i, 0)), pl.BlockSpec((1, LANES), lambda i: (0, 0))],
        out_specs=(tab, tab, tab, tab),
        compiler_params=_cparams("parallel"),
        name="rope_tables",
    )(pos_f, inv)


def _inproj_kernel(x_ref, mod_ref, g_ref, w_ref, o_ref):
    h = _norm_mod(x_ref[...], g_ref[...], mod_ref[0:1, :], mod_ref[1:2, :]).astype(BF16)
    for c in range(N_IN // TN_PROJ):
        cols = slice(c * TN_PROJ, (c + 1) * TN_PROJ)
        o_ref[:, cols] = jnp.dot(h, w_ref[:, cols], preferred_element_type=F32).astype(BF16)


def _resident(shape):
    return pl.BlockSpec(shape, lambda *_: (0,) * len(shape), pipeline_mode=pl.Buffered(1))


def _inproj(x, mod, g, w, seq):
    n, d = x.shape
    tm = TM_PROJ
    tpb = seq // tm
    return pl.pallas_call(
        _inproj_kernel,
        out_shape=jax.ShapeDtypeStruct((n, N_IN), BF16),
        grid=(n // tm,),
        in_specs=[
            pl.BlockSpec((tm, d), lambda i: (i, 0)),
            pl.BlockSpec((None, 6, d), lambda i: (i // tpb, 0, 0)),
            pl.BlockSpec((1, d), lambda i: (0, 0)),
            _resident((d, N_IN)),
        ],
        out_specs=pl.BlockSpec((tm, N_IN), lambda i: (i, 0)),
        compiler_params=_cparams("parallel"),
        name="in_proj",
    )(x, mod, g, w)


def _mla_prep_kernel(cq_ref, ckv_ref, kra_ref, cm_ref, sm_ref, cqg_ref, wqa_ref, wqb_ref,
                     ckvg_ref, wk_ref, wv_ref, qga_ref, qgb_ref, kga_ref, kgb_ref,
                     qaug_ref, kaug_ref, vaug_ref, swap_ref, q_ref, k_ref, v_ref):
    cq = cq_ref[...].astype(F32)
    cqn = (cq * lax.rsqrt(jnp.mean(cq * cq, axis=-1, keepdims=True) + EPS) * cqg_ref[...]).astype(BF16)
    qa = jnp.dot(cqn, wqa_ref[...], preferred_element_type=F32)
    qb = jnp.dot(cqn, wqb_ref[...], preferred_element_type=F32)
    ckv = ckv_ref[...].astype(F32)
    ckvn = (ckv * lax.rsqrt(jnp.mean(ckv * ckv, axis=-1, keepdims=True) + EPS) * ckvg_ref[...]).astype(BF16)
    ka = jnp.dot(ckvn, wk_ref[...], preferred_element_type=F32)
    v_ref[...] = (jnp.dot(ckvn, wv_ref[...], preferred_element_type=F32) + vaug_ref[...]).astype(BF16)
    kra = kra_ref[...].astype(F32)
    krb = jnp.dot(kra_ref[...], swap_ref[...], preferred_element_type=F32)
    cm = cm_ref[...]
    sm = sm_ref[...]
    scale = QK_HEAD ** -0.5 * LOG2_E
    for h in range(N_GROUPS):
        sl = slice(h * LANES, (h + 1) * LANES)
        qah, qbh = qa[:, sl], qb[:, sl]
        r = lax.rsqrt(jnp.sum(qah * qah, axis=-1, keepdims=True) * (1.0 / QK_HEAD) + EPS)
        q_rot = (qah * r) * qga_ref[...] * cm + (qbh * r) * qgb_ref[...] * sm
        q_ref[:, sl] = (q_rot * scale + qaug_ref[...]).astype(BF16)
        kah = ka[:, sl] + kra
        kbh = ka[:, sl] + krb
        r = lax.rsqrt(jnp.sum(kah * kah, axis=-1, keepdims=True) * (1.0 / QK_HEAD) + EPS)
        k_rot = (kah * r) * kga_ref[...] * cm + (kbh * r) * kgb_ref[...] * sm
        k_ref[:, sl] = (k_rot + kaug_ref[...]).astype(BF16)


def _mla_prep(proj, cm, sm, p):
    n = proj.shape[0]
    tm = TM_PREP
    hw = N_GROUPS * LANES

    def col(width, offset):
        return pl.BlockSpec((tm, width), lambda i: (i, offset // width))

    def full(a):
        return pl.BlockSpec(a.shape, lambda i: (0,) * a.ndim)

    weights = [p["cq_g"], p["wqa"], p["wqb"], p["ckv_g"], p["wk"], p["wv"],
               p["qga"], p["qgb"], p["kga"], p["kgb"], p["qaug"], p["kaug"], p["vaug"], p["swap"]]
    head_tile = pl.BlockSpec((tm, hw), lambda i: (i, 0))
    out = jax.ShapeDtypeStruct((n, hw), BF16)
    return pl.pallas_call(
        _mla_prep_kernel,
        out_shape=(out, out, out),
        grid=(n // tm,),
        in_specs=[col(Q_LORA, COL_CQ), col(KV_LORA, COL_CKV), col(LANES, COL_KRA),
                  pl.BlockSpec((tm, LANES), lambda i: (i, 0)), pl.BlockSpec((tm, LANES), lambda i: (i, 0))]
                 + [full(w) for w in weights],
        out_specs=(head_tile, head_tile, head_tile),
        compiler_params=_cparams("parallel"),
        name="mla_prep",
    )(proj, proj, proj, cm, sm, *weights)


def _flash_kernel(q_ref, k_ref, v_ref, o_ref, acc_scr, *rest, tq, online_max):
    i = pl.program_id(1)
    j = pl.program_id(2)

    @pl.when(j == 0)
    def _():
        acc_scr[...] = jnp.zeros(acc_scr.shape, F32)
        if online_max:
            rest[0][...] = jnp.full(rest[0].shape, -jnp.inf, F32)

    def block(q0, nq, nk, masked):
        rows = slice(q0, q0 + nq)
        if masked:
            row = lax.broadcasted_iota(jnp.int32, (nq, nk), 0) + q0
            col = lax.broadcasted_iota(jnp.int32, (nq, nk), 1)
            keep = col <= row
        for h in range(N_GROUPS):
            sl = slice(h * LANES, (h + 1) * LANES)
            s = lax.dot_general(q_ref[rows, sl], k_ref[0:nk, sl], (((1,), (1,)), ((), ())),
                                preferred_element_type=F32)
            if masked:
                s = jnp.where(keep, s, -jnp.inf)
            if online_max:
                m_scr = rest[0]
                m_prev = m_scr[h, rows]
                m_new = jnp.maximum(m_prev, jnp.max(s, axis=-1, keepdims=True))
                p = jnp.exp2(s - m_new).astype(BF16)
                acc_scr[h, rows] = jnp.exp2(m_prev - m_new) * acc_scr[h, rows] + jnp.dot(
                    p, v_ref[0:nk, sl], preferred_element_type=F32)
                m_scr[h, rows] = m_new
            else:
                acc_scr[h, rows] += jnp.dot(jnp.exp2(s).astype(BF16), v_ref[0:nk, sl],
                                            preferred_element_type=F32)

    @pl.when(j < i)
    def _():
        block(0, tq, tq, False)

    @pl.when(j == i)
    def _():
        block(0, tq // 2, tq // 2, True)
        block(tq // 2, tq // 2, tq, True)
        lane = lax.broadcasted_iota(jnp.int32, (tq, LANES), 1)
        for pr in range(N_GROUPS // 2):
            lo = acc_scr[2 * pr]
            hi = acc_scr[2 * pr + 1]
            lo = lo / lo[:, V_HEAD:V_HEAD + 1]
            hi = hi / hi[:, V_HEAD:V_HEAD + 1]
            both = jnp.where(lane < V_HEAD, lo, pltpu.roll(hi, V_HEAD, axis=1))
            o_ref[:, pr * LANES:(pr + 1) * LANES] = both.astype(BF16)


def _flash(q, k, v, batch, seq, online_max):
    n = q.shape[0]
    tq = TQ_ATT
    nq = seq // tq
    hw = N_GROUPS * LANES
    scratch = [pltpu.VMEM((N_GROUPS, tq, LANES), F32)]
    if online_max:
        scratch.append(pltpu.VMEM((N_GROUPS, tq, 1), F32))
    return pl.pallas_call(
        functools.partial(_flash_kernel, tq=tq, online_max=online_max),
        out_shape=jax.ShapeDtypeStruct((n, MIX_W), BF16),
        grid=(batch, nq, nq),
        in_specs=[
            pl.BlockSpec((tq, hw), lambda b, i, j: (b * nq + i, 0)),
            pl.BlockSpec((tq, hw), lambda b, i, j: (b * nq + jnp.minimum(j, i), 0)),
            pl.BlockSpec((tq, hw), lambda b, i, j: (b * nq + jnp.minimum(j, i), 0)),
        ],
        out_specs=pl.BlockSpec((tq, MIX_W), lambda b, i, j: (b * nq + i, 0)),
        scratch_shapes=scratch,
        compiler_params=_cparams("parallel", "parallel", "arbitrary"),
        name="mla_flash_online" if online_max else "mla_flash",
    )(q, k, v)


def _gelu_tanh(x):
    return jax.nn.gelu(x, approximate=True)


def _mix_kernel(gates_ref, a_ref, r_ref, su_ref, ymla_ref, x_ref, cos_ref, sin_ref, mod_ref,
                convw_ref, gvg_ref, wscat_ref, bsmat_ref, retg_ref, dec_ref, kdec_ref, qdec_ref,
                cdec_ref, bd_ref, gmat_ref, mk_ref, mv_ref, wb_ref, wo_ref,
                o_ref, carry_scr, state_scr, ysg_scr, yret_scr, *, tm, tpb):
    i = pl.program_id(0)

    @pl.when(i % tpb == 0)
    def _():
        carry_scr[...] = jnp.zeros(carry_scr.shape, F32)
        state_scr[...] = jnp.zeros(state_scr.shape, F32)

    w = MIX_W
    a_b = a_ref[:, 0:w].astype(F32)
    u = a_ref[:, w:2 * w].astype(F32) * a_ref[:, 2 * w:3 * w].astype(F32)
    rowi = lax.broadcasted_iota(jnp.int32, (tm, w), 0)
    prev1 = carry_scr[0:1, :]
    prev2 = carry_scr[1:2, :]
    u1 = jnp.where(rowi == 0, prev1, pltpu.roll(u, 1, axis=0))
    u2 = jnp.where(rowi == 0, prev2, jnp.where(rowi == 1, prev1, pltpu.roll(u, 2, axis=0)))
    carry_scr[0:1, :] = u[tm - 1:tm, :]
    carry_scr[1:2, :] = u[tm - 2:tm - 1, :]
    y_conv = a_b * (convw_ref[0:1, :] * u2 + convw_ref[1:2, :] * u1 + convw_ref[2:3, :] * u)

    gmat = gmat_ref[...]
    s_u = _gelu_tanh(su_ref[:, 0:w].astype(F32))
    s_v = _gelu_tanh(su_ref[:, w:2 * w].astype(F32))
    ms = _group_mean(s_v * s_v, gmat)
    vn = (s_v * lax.rsqrt(ms + EPS) * gvg_ref[...]).astype(BF16)

    cosr = cos_ref[...]
    sinr = sin_ref[...]

    def rot(t):
        t1, t2 = t[:, 0:LANES], t[:, LANES:2 * LANES]
        return jnp.concatenate([t1 * cosr - t2 * sinr, t2 * cosr + t1 * sinr], axis=-1)

    rq = rot(r_ref[:, 0:w].astype(F32))
    rk = rot(r_ref[:, w:2 * w].astype(F32)) * (HEAD_DIM ** -0.5)

    for c in range(tm // CHUNK):
        rows = slice(c * CHUNK, (c + 1) * CHUNK)
        vc = vn[rows, :]
        vbd = jnp.concatenate([vc * mv_ref[g:g + 1, :].astype(BF16) for g in range(N_GROUPS)], axis=0)
        mixed = jnp.dot(wscat_ref[...], vbd, preferred_element_type=F32) + bsmat_ref[...]
        ysg_scr[rows, :] = s_u[rows, :] * mixed

        qc = rq[rows, :]
        kc = rk[rows, :]
        kcb = kc.astype(BF16)
        vcb = r_ref[rows, 2 * w:3 * w]
        qstack = jnp.concatenate([(qc * mk_ref[h:h + 1, :]).astype(BF16) for h in range(N_GROUPS)], axis=0)
        sc = lax.dot_general(qstack, kcb, (((1,), (1,)), ((), ())), preferred_element_type=F32)
        sc = (sc * dec_ref[...]).astype(BF16)
        scat = jnp.concatenate([sc[h * CHUNK:(h + 1) * CHUNK, :] for h in range(N_GROUPS)], axis=1)
        vstack = jnp.concatenate([vcb * mv_ref[h:h + 1, :].astype(BF16) for h in range(N_GROUPS)], axis=0)
        o_c = jnp.dot(scat, vstack, preferred_element_type=F32)
        state = state_scr[...]
        o_c = o_c + jnp.dot((qc * qdec_ref[...]).astype(BF16), state.astype(BF16),
                            preferred_element_type=F32)
        kd_t = jnp.transpose(kc * kdec_ref[...]).astype(BF16)
        kv = jnp.dot(kd_t, vcb, preferred_element_type=F32)
        state_scr[...] = state * cdec_ref[...] + kv * bd_ref[...]
        yret_scr[rows, :] = o_c

    o_all = yret_scr[...]
    xc = o_all - _group_mean(o_all, gmat)
    var = _group_mean(xc * xc, gmat)
    r_g = r_ref[:, 3 * w:4 * w].astype(F32)
    y_ret = (r_g * _sigmoid(r_g)) * (xc * lax.rsqrt(var + EPS) * retg_ref[...])

    d = x_ref.shape[1]
    ys = (y_conv, ymla_ref[...], ysg_scr[...], y_ret)
    merged = None
    for n in range(N_BRANCH):
        gate = _sigmoid(gates_ref[:, n * d:(n + 1) * d])
        term = gate * jnp.dot(ys[n].astype(BF16), wb_ref[n], preferred_element_type=F32).astype(BF16)
        merged = term if merged is None else merged + term
    out = jnp.dot(merged, wo_ref[...], preferred_element_type=F32)
    o_ref[...] = x_ref[...] + mod_ref[2:3, :] * out


def _mixers(proj, ymla, x, cosr, sinr, mod, p, seq):
    n, d = x.shape
    tm = TM_MIX
    tpb = seq // tm

    def col(width, offset):
        return pl.BlockSpec((tm, width), lambda i: (i, offset // width))

    def full(a):
        return pl.BlockSpec(a.shape, lambda i: (0,) * a.ndim)

    consts = [p["conv_w"], p["gv_g"], p["ws_cat"], p["bs_mat"], p["ret_g"], p["dec"], p["kdec"],
              p["qdec"], p["cdec"], p["bd"], p["gmat"], p["mk"], p["mv"], p["w_branch"], p["w_o"]]
    return pl.pallas_call(
        functools.partial(_mix_kernel, tm=tm, tpb=tpb),
        out_shape=jax.ShapeDtypeStruct((n, d), F32),
        grid=(n // tm,),
        in_specs=[col(N_BRANCH * d, COL_GATES), col(4 * MIX_W, COL_A), col(4 * MIX_W, COL_R),
                  col(2 * MIX_W, COL_SU),
                  pl.BlockSpec((tm, MIX_W), lambda i: (i, 0)),
                  pl.BlockSpec((tm, d), lambda i: (i, 0)),
                  pl.BlockSpec((tm, LANES), lambda i: (i, 0)),
                  pl.BlockSpec((tm, LANES), lambda i: (i, 0)),
                  pl.BlockSpec((None, 6, d), lambda i: (i // tpb, 0, 0))]
                 + [full(c) for c in consts],
        out_specs=pl.BlockSpec((tm, d), lambda i: (i, 0)),
        scratch_shapes=[pltpu.VMEM((8, MIX_W), F32), pltpu.VMEM((MIX_W, MIX_W), F32),
                        pltpu.VMEM((tm, MIX_W), F32), pltpu.VMEM((tm, MIX_W), F32)],
        compiler_params=_cparams("arbitrary"),
        name="mixers_merge",
    )(proj, proj, proj, proj, ymla, x, cosr, sinr, mod, *consts)


def _ffn_kernel(x_ref, mod_ref, g_ref, w1_ref, w3_ref, w2_ref, o_ref):
    x = x_ref[...]
    h = _norm_mod(x, g_ref[...], mod_ref[3:4, :], mod_ref[4:5, :]).astype(BF16)
    a = jnp.dot(h, w1_ref[...], preferred_element_type=F32)
    b = jnp.dot(h, w3_ref[...], preferred_element_type=F32)
    hid = ((a * _sigmoid(a)) * b).astype(BF16)
    o_ref[...] = x + mod_ref[5:6, :] * jnp.dot(hid, w2_ref[...], preferred_element_type=F32)


def _dense_ffn(x, mod, g, w1, w3, w2, seq):
    n, d = x.shape
    dff = w1.shape[1]
    tm = TM_FFN
    tpb = seq // tm
    return pl.pallas_call(
        _ffn_kernel,
        out_shape=jax.ShapeDtypeStruct((n, d), F32),
        grid=(n // tm,),
        in_specs=[
            pl.BlockSpec((tm, d), lambda i: (i, 0)),
            pl.BlockSpec((None, 6, d), lambda i: (i // tpb, 0, 0)),
            pl.BlockSpec((1, d), lambda i: (0, 0)),
            _resident((d, dff)), _resident((d, dff)), _resident((dff, d)),
        ],
        out_specs=pl.BlockSpec((tm, d), lambda i: (i, 0)),
        compiler_params=_cparams("parallel"),
        name="dense_swiglu",
    )(x, mod, g, w1, w3, w2)


def _router_kernel(x_ref, mod_ref, g_ref, rw_ref, rb_ref, hs_ref, ei_ref, pw_ref, meta_ref, tot_ref,
                   carry_scr, *, tm, srows):
    i = pl.program_id(0)

    @pl.when(i == 0)
    def _():
        carry_scr[...] = jnp.zeros(carry_scr.shape, F32)

    h = _norm_mod(x_ref[...], g_ref[...], mod_ref[3:4, :], mod_ref[4:5, :])

    h_hi = h.astype(BF16)
    h_lo = (h - h_hi.astype(F32)).astype(BF16)
    hw = jnp.dot(h_hi, rw_ref[...], preferred_element_type=F32)
    logits = (hw[:, :LANES] + hw[:, LANES:] + jnp.dot(h_lo, rw_ref[:, :LANES], preferred_element_type=F32)
              + rb_ref[...])
    mx = jnp.max(logits, axis=-1, keepdims=True)
    ex = jnp.exp(logits - mx)
    probs = ex / jnp.sum(ex, axis=-1, keepdims=True)
    lane = lax.broadcasted_iota(jnp.int32, (tm, LANES), 1)
    valid = lane < N_EXPERTS
    probs = jnp.where(valid, probs, -1.0)
    m1 = jnp.max(probs, axis=-1, keepdims=True)
    i1 = jnp.min(jnp.where(probs == m1, lane, LANES), axis=-1, keepdims=True)
    rest = jnp.where(lane == i1, -1.0, probs)
    m2 = jnp.max(rest, axis=-1, keepdims=True)
    i2 = jnp.min(jnp.where(rest == m2, lane, LANES), axis=-1, keepdims=True)
    den = m1 + m2
    pw_ref[...] = jnp.where(lane == 0, m1 / den, jnp.where(lane == 1, m2 / den, 0.0))

    sel1 = lane == i1
    sel2 = lane == i2
    onehot = jnp.where(sel1, 1.0, 0.0) + jnp.where(sel2, 1.0, 0.0)
    r_i = lax.broadcasted_iota(jnp.int32, (tm, tm), 0)
    c_i = lax.broadcasted_iota(jnp.int32, (tm, tm), 1)
    tri = jnp.where(c_i < r_i, 1.0, 0.0).astype(BF16)
    before = jnp.dot(tri, onehot.astype(BF16), preferred_element_type=F32)
    cnt = jnp.sum(onehot, axis=0, keepdims=True)
    cnt_al = jnp.floor((cnt + (ROW_ALIGN - 1)) * (1.0 / ROW_ALIGN)) * ROW_ALIGN
    e_r = lax.broadcasted_iota(jnp.int32, (LANES, LANES), 0)
    e_c = lax.broadcasted_iota(jnp.int32, (LANES, LANES), 1)
    upper = jnp.where(e_r < e_c, 1.0, 0.0)
    loff = jnp.dot(jnp.broadcast_to(cnt_al, (8, LANES)), upper, precision=HIGHEST,
                   preferred_element_type=F32)[0:1, :]
    slot = loff + before
    slot1 = jnp.sum(jnp.where(sel1, slot, 0.0), axis=-1, keepdims=True).astype(jnp.int32)
    slot2 = jnp.sum(jnp.where(sel2, slot, 0.0), axis=-1, keepdims=True).astype(jnp.int32)
    ei = jnp.where(lane == 0, i1, jnp.where(lane == 1, i2, 0))
    ei_ref[...] = jnp.where(lane == 2, slot1, jnp.where(lane == 3, slot2, ei))

    r_idx = lax.broadcasted_iota(jnp.int32, (tm, srows), 1)
    place = jnp.where(r_idx == slot1, 1.0, jnp.where(r_idx == slot2, 1.0, 0.0)).astype(BF16)
    hs = lax.dot_general(place, h.astype(BF16), (((0,), (0,)), ((), ())), preferred_element_type=F32)
    half = hs.shape[1] // 2
    hs_ref[...] = _pack_bf16_pair(hs[:, :half], hs[:, half:])

    carry = carry_scr[0:1, :]
    mrow = lax.broadcasted_iota(jnp.int32, (8, LANES), 0)
    meta = jnp.where(mrow == 0, cnt_al, jnp.where(mrow == 1, carry, jnp.where(mrow == 2, loff, 0.0)))
    meta_ref[...] = meta.astype(jnp.int32)
    carry_scr[0:1, :] = carry + cnt_al
    tot_ref[...] = jnp.broadcast_to(carry + cnt_al, tot_ref.shape).astype(jnp.int32)


def _router(x, mod, g, rw_pad, rb_pad, seq):
    n, d = x.shape
    tm = TM_ROUTE
    tpb = seq // tm
    nt = n // tm
    return pl.pallas_call(
        functools.partial(_router_kernel, tm=tm, srows=SORT_ROWS),
        out_shape=(jax.ShapeDtypeStruct((nt * SORT_ROWS, d // 2), jnp.uint32),
                   jax.ShapeDtypeStruct((n, LANES), jnp.int32),
                   jax.ShapeDtypeStruct((n, LANES), F32),
                   jax.ShapeDtypeStruct((nt, 8, LANES), jnp.int32),
                   jax.ShapeDtypeStruct((8, LANES), jnp.int32)),
        grid=(nt,),
        in_specs=[
            pl.BlockSpec((tm, d), lambda i: (i, 0)),
            pl.BlockSpec((None, 6, d), lambda i: (i // tpb, 0, 0)),
            pl.BlockSpec((1, d), lambda i: (0, 0)),
            pl.BlockSpec((d, 2 * LANES), lambda i: (0, 0)),
            pl.BlockSpec((1, LANES), lambda i: (0, 0)),
        ],
        out_specs=(pl.BlockSpec((SORT_ROWS, d // 2), lambda i: (i, 0)),
                   pl.BlockSpec((tm, LANES), lambda i: (i, 0)),
                   pl.BlockSpec((tm, LANES), lambda i: (i, 0)),
                   pl.BlockSpec((None, 8, LANES), lambda i: (i, 0, 0)),
                   pl.BlockSpec((8, LANES), lambda i: (0, 0))),
        scratch_shapes=[pltpu.VMEM((8, LANES), F32)],
        compiler_params=_cparams("arbitrary"),
        name="router_top2",
    )(x, mod, g, rw_pad, rb_pad)


def _segment_copy(src_hbm, dst_hbm, src_row, dst_row, n_rows, sem):
    src_row = pl.multiple_of(src_row, ROW_ALIGN)
    dst_row = pl.multiple_of(dst_row, ROW_ALIGN)
    n_rows = pl.multiple_of(n_rows, ROW_ALIGN)
    return pltpu.make_async_copy(src_hbm.at[pl.ds(src_row, n_rows)], dst_hbm.at[pl.ds(dst_row, n_rows)], sem)


def _dispatch_kernel(loff_ref, dst_ref, cnt_ref, rows_ref, gap_ref, hs_ref, xs_ref, buf, zbuf, sem_in, sem_out,
                     sem_zero, *, n_tiles, srows, n_out_tiles):
    zbuf[...] = jnp.zeros(zbuf.shape, zbuf.dtype)
    tg = zbuf.shape[0]
    first_unused = gap_ref[2 * N_EXPERTS]

    def zero_gap(e):
        return _segment_copy(zbuf, xs_ref, 0, gap_ref[2 * e], gap_ref[2 * e + 1], sem_zero)

    def zero_tile(t):
        return pltpu.make_async_copy(zbuf, xs_ref.at[pl.ds(pl.multiple_of(t * tg, ROW_ALIGN), tg)], sem_zero)

    def for_each_zero_copy(action):
        for e in range(N_EXPERTS):
            @pl.when(gap_ref[2 * e + 1] > 0)
            def _():
                action(zero_gap(e))

        def tail(t, carry):
            action(zero_tile(t))
            return carry

        lax.fori_loop(first_unused, n_out_tiles, tail, 0)

    for_each_zero_copy(lambda copy: copy.start())

    def fetch(t):
        slot = t % DISPATCH_SLOTS
        return pltpu.make_async_copy(hs_ref.at[pl.ds(pl.multiple_of(t * srows, ROW_ALIGN), srows)],
                                     buf.at[slot], sem_in.at[slot])

    def drain(t):
        slot = t % DISPATCH_SLOTS
        n_rows = rows_ref[t]

        @pl.when(n_rows > 0)
        def _():
            _segment_copy(buf.at[slot], xs_ref, 0, 0, n_rows, sem_out.at[slot]).wait()

    for t in range(DISPATCH_AHEAD):
        fetch(t).start()

    def body(t, carry):
        slot = t % DISPATCH_SLOTS
        fetch(t).wait()
        for e in range(N_EXPERTS):
            s = t * N_EXPERTS + e
            n_rows = cnt_ref[s]

            @pl.when(n_rows > 0)
            def _():
                _segment_copy(buf.at[slot], xs_ref, loff_ref[s], dst_ref[s], n_rows, sem_out.at[slot]).start()

        @pl.when(t + DISPATCH_AHEAD < n_tiles)
        def _():
            @pl.when(t + DISPATCH_AHEAD >= DISPATCH_SLOTS)
            def _():
                drain(t + DISPATCH_AHEAD - DISPATCH_SLOTS)

            fetch(t + DISPATCH_AHEAD).start()

        return carry

    lax.fori_loop(0, n_tiles, body, 0)
    for t in range(max(n_tiles - DISPATCH_SLOTS, 0), n_tiles):
        drain(t)
    for_each_zero_copy(lambda copy: copy.wait())


def _dispatch(seg_loff, seg_dst, seg_cnt, tile_rows, gaps, hs, n_out_tiles):
    n_tiles = tile_rows.shape[0]
    srows = hs.shape[0] // n_tiles
    assert n_tiles >= DISPATCH_SLOTS
    return pl.pallas_call(
        functools.partial(_dispatch_kernel, n_tiles=n_tiles, srows=srows, n_out_tiles=n_out_tiles),
        out_shape=jax.ShapeDtypeStruct((n_out_tiles * TG_MOE, hs.shape[1]), hs.dtype),
        grid_spec=pltpu.PrefetchScalarGridSpec(
            num_scalar_prefetch=5,
            grid=(1,),
            in_specs=[pl.BlockSpec(memory_space=pl.ANY)],
            out_specs=pl.BlockSpec(memory_space=pl.ANY),
            scratch_shapes=[pltpu.VMEM((DISPATCH_SLOTS, srows, hs.shape[1]), hs.dtype),
                            pltpu.VMEM((TG_MOE, hs.shape[1]), hs.dtype),
                            pltpu.SemaphoreType.DMA((DISPATCH_SLOTS,)),
                            pltpu.SemaphoreType.DMA((DISPATCH_SLOTS,)),
                            pltpu.SemaphoreType.DMA],
        ),
        compiler_params=_cparams("arbitrary"),
        name="moe_dispatch",
    )(seg_loff, seg_dst, seg_cnt, tile_rows, gaps, hs)


def _expert_kernel(te_ref, used_ref, xs_ref, w1_hbm, w3_hbm, w2_hbm, y_ref, wb1, wb3, wb2, stage_c, stage_r, sem):
    t = pl.program_id(0)
    e = te_ref[t]
    dff = wb1.shape[1]
    first_of_expert = jnp.logical_or(t == 0, e != te_ref[jnp.maximum(t - 1, 0)])

    @pl.when(jnp.logical_and(used_ref[t] == 1, first_of_expert))
    def _():
        chunks = ([(w1_hbm, wb1, c, True) for c in range(dff // W_CHUNK)]
                  + [(w3_hbm, wb3, c, True) for c in range(dff // W_CHUNK)]
                  + [(w2_hbm, wb2, c, False) for c in range(dff // W_CHUNK)])

        def staged_copy(k):
            src, _, c, by_col = chunks[k]
            slot = k % 2
            if by_col:
                return pltpu.make_async_copy(src.at[e, :, pl.ds(c * W_CHUNK, W_CHUNK)], stage_c.at[slot],
                                             sem.at[slot])
            return pltpu.make_async_copy(src.at[e, pl.ds(c * W_CHUNK, W_CHUNK), :], stage_r.at[slot], sem.at[slot])

        staged_copy(0).start()
        for k, (_, dst, c, by_col) in enumerate(chunks):
            if k + 1 < len(chunks):
                staged_copy(k + 1).start()
            staged_copy(k).wait()
            if by_col:
                dst[:, c * W_CHUNK:(c + 1) * W_CHUNK] = stage_c[k % 2].astype(BF16)
            else:
                dst[c * W_CHUNK:(c + 1) * W_CHUNK, :] = stage_r[k % 2].astype(BF16)

    @pl.when(used_ref[t] == 1)
    def _():
        lo, hi = _unpack_bf16_pair(xs_ref[...])
        h = jnp.concatenate([lo.astype(BF16), hi.astype(BF16)], axis=1)
        a = jnp.dot(h, wb1[...], preferred_element_type=F32)
        b = jnp.dot(h, wb3[...], preferred_element_type=F32)
        hid = ((a * _sigmoid(a)) * b).astype(BF16)
        acc = jnp.dot(hid, wb2[...], preferred_element_type=F32)
        half = acc.shape[1] // 2
        y_ref[...] = _pack_bf16_pair(acc[:, :half], acc[:, half:])

    @pl.when(used_ref[t] == 0)
    def _():
        y_ref[...] = jnp.zeros(y_ref.shape, y_ref.dtype)


def _expert_ffn(tile_e, tile_used, xs, w1, w3, w2):
    rows, half = xs.shape
    d = 2 * half
    dff = w1.shape[2]
    tg = TG_MOE
    assert dff % W_CHUNK == 0
    hbm = pl.BlockSpec(memory_space=pl.ANY)
    return pl.pallas_call(
        _expert_kernel,
        out_shape=jax.ShapeDtypeStruct((rows, half), jnp.uint32),
        grid_spec=pltpu.PrefetchScalarGridSpec(
            num_scalar_prefetch=2,
            grid=(rows // tg,),
            in_specs=[pl.BlockSpec((tg, half), lambda t, te, us: (t, 0)), hbm, hbm, hbm],
            out_specs=pl.BlockSpec((tg, half), lambda t, te, us: (t, 0)),
            scratch_shapes=[pltpu.VMEM((d, dff), BF16), pltpu.VMEM((d, dff), BF16), pltpu.VMEM((dff, d), BF16),
                            pltpu.VMEM((2, d, W_CHUNK), F32), pltpu.VMEM((2, W_CHUNK, d), F32),
                            pltpu.SemaphoreType.DMA((2,))],
        ),
        compiler_params=_cparams("arbitrary"),
        name="expert_swiglu",
    )(tile_e, tile_used, xs, w1, w3, w2)


def _combine_kernel(src_ref, loff_ref, cnt_ref, rows_ref, x_ref, ei_ref, pw_ref, mod_ref, y_ref, o_ref,
                    ybuf, sem, *, tm, srows):
    i = pl.program_id(0)
    slot = i % 2

    def fetch(tile, into):
        ybuf[into] = jnp.zeros(ybuf.shape[1:], ybuf.dtype)
        for e in range(N_EXPERTS):
            s = tile * N_EXPERTS + e
            n_rows = cnt_ref[s]

            @pl.when(n_rows > 0)
            def _():
                _segment_copy(y_ref, ybuf.at[into], src_ref[s], loff_ref[s], n_rows, sem.at[into]).start()

    @pl.when(i == 0)
    def _():
        fetch(i, slot)

    @pl.when(i + 1 < pl.num_programs(0))
    def _():
        fetch(i + 1, 1 - slot)

    @pl.when(rows_ref[i] > 0)
    def _():
        _segment_copy(y_ref, ybuf.at[slot], 0, 0, rows_ref[i], sem.at[slot]).wait()

    lo, hi = _unpack_bf16_pair(ybuf[slot])
    ys = jnp.concatenate([lo.astype(BF16), hi.astype(BF16)], axis=1)
    r_idx = lax.broadcasted_iota(jnp.int32, (tm, srows), 1)
    mix = jnp.zeros(x_ref.shape, F32)
    for k in range(TOP_K):
        pick = jnp.where(r_idx == ei_ref[:, TOP_K + k:TOP_K + k + 1], 1.0, 0.0).astype(BF16)
        mix = mix + pw_ref[:, k:k + 1] * jnp.dot(pick, ys, preferred_element_type=F32)
    o_ref[...] = x_ref[...] + mod_ref[5:6, :] * mix


def _combine(seg_src, seg_loff, seg_cnt, tile_rows, x, ei, pw, mod, y, seq):
    n, d = x.shape
    tm = TM_ROUTE
    tpb = seq // tm
    tok = lambda width: pl.BlockSpec((tm, width), lambda i, *_: (i, 0))
    return pl.pallas_call(
        functools.partial(_combine_kernel, tm=tm, srows=SORT_ROWS),
        out_shape=jax.ShapeDtypeStruct((n, d), F32),
        grid_spec=pltpu.PrefetchScalarGridSpec(
            num_scalar_prefetch=4,
            grid=(n // tm,),
            in_specs=[tok(d), tok(LANES), tok(LANES),
                      pl.BlockSpec((None, 6, d), lambda i, *_: (i // tpb, 0, 0)),
                      pl.BlockSpec(memory_space=pl.ANY)],
            out_specs=tok(d),
            scratch_shapes=[pltpu.VMEM((2, SORT_ROWS, d // 2), jnp.uint32), pltpu.SemaphoreType.DMA((2,))],
        ),
        compiler_params=_cparams("arbitrary"),
        name="moe_combine",
    )(seg_src, seg_loff, seg_cnt, tile_rows, x, ei, pw, mod, y)


def _pack_w_in(w_in):
    d = w_in.shape[0]
    w = MIX_W
    o_ckv = 3 * w + Q_LORA
    o_kr = o_ckv + KV_LORA
    o_su = o_kr + QK_ROPE
    o_rq = o_su + 2 * w
    o_gate = o_rq + 4 * w
    half = HEAD_DIM // 2
    perm = np.array([h * HEAD_DIM + part * half + i
                     for part in range(2) for h in range(N_GROUPS) for i in range(half)])
    kr = w_in[:, o_kr:o_kr + QK_ROPE]
    z = lambda k: jnp.zeros((d, k), w_in.dtype)
    cols = [
        w_in[:, o_gate:o_gate + N_BRANCH * d],
        w_in[:, 0:3 * w + Q_LORA],
        w_in[:, o_rq:o_rq + w][:, perm], w_in[:, o_rq + w:o_rq + 2 * w][:, perm],
        w_in[:, o_rq + 2 * w:o_rq + 4 * w],
        w_in[:, o_su:o_su + 2 * w],
        w_in[:, o_ckv:o_ckv + KV_LORA],
        z(QK_NOPE), kr, z(LANES - QK_HEAD),
    ]
    return jnp.concatenate(cols, axis=1).astype(BF16)


def _swap_rope_halves(a):
    hr = QK_ROPE // 2
    return jnp.concatenate([a[..., :QK_NOPE], a[..., QK_NOPE + hr:QK_HEAD], a[..., QK_NOPE:QK_NOPE + hr],
                            a[..., QK_HEAD:]], axis=-1)


def _mla_params(cq_g, w_uq, ckv_g, w_ukv, qn_g, kn_g):
    pad = LANES - QK_HEAD
    wq = w_uq.reshape(Q_LORA, N_GROUPS, QK_HEAD)
    wq = jnp.pad(wq, ((0, 0), (0, 0), (0, pad)))
    wkv = w_ukv.reshape(KV_LORA, N_GROUPS, QK_NOPE + V_HEAD)
    wk = jnp.pad(wkv[:, :, :QK_NOPE], ((0, 0), (0, 0), (0, LANES - QK_NOPE)))
    wv = jnp.pad(wkv[:, :, QK_NOPE:], ((0, 0), (0, 0), (0, LANES - V_HEAD)))
    qg = jnp.pad(qn_g, (0, pad))[None, :]
    kg = jnp.pad(kn_g, (0, pad))[None, :]
    bound = (QK_HEAD ** 0.5 * LOG2_E) * jnp.max(jnp.abs(qn_g)) * jnp.max(jnp.abs(kn_g))
    static_shift = bound <= MAX_STATIC_SHIFT
    lane = jnp.arange(LANES)
    qaug = (lane == QK_HEAD).astype(F32)[None, :]
    kaug = qaug * jnp.where(static_shift, -bound, 0.0)
    vaug = jnp.tile((lane == V_HEAD).astype(F32), N_GROUPS)[None, :]
    params = {
        "cq_g": cq_g[None, :], "ckv_g": ckv_g[None, :],
        "wqa": wq.reshape(Q_LORA, -1).astype(BF16),
        "wqb": _swap_rope_halves(wq).reshape(Q_LORA, -1).astype(BF16),
        "wk": wk.reshape(KV_LORA, -1).astype(BF16),
        "wv": wv.reshape(KV_LORA, -1).astype(BF16),
        "qga": qg, "qgb": _swap_rope_halves(qg), "kga": kg, "kgb": _swap_rope_halves(kg),
        "qaug": qaug, "kaug": kaug, "vaug": vaug,
        "swap": (_swap_rope_halves(lane[None, :])[0][None, :] == lane[:, None]).astype(BF16),
    }
    return params, static_shift


def _mixer_consts():
    h = jnp.arange(N_GROUPS, dtype=F32)
    log_gamma = jnp.log1p(-(2.0 ** (-5.0 - h)))
    pos = jnp.arange(CHUNK, dtype=F32)
    rel = pos[:, None] - pos[None, :]
    dec = jnp.where(rel >= 0, jnp.exp(log_gamma[:, None, None] * jnp.maximum(rel, 0.0)), 0.0)
    lane = np.arange(MIX_W)
    head_k = (lane % LANES) // (HEAD_DIM // 2)
    head_v = lane // HEAD_DIM
    lg_k = log_gamma[head_k]
    return {
        "dec": dec.reshape(N_GROUPS * CHUNK, CHUNK),
        "kdec": jnp.exp(lg_k[None, :] * (CHUNK - 1.0 - pos)[:, None]),
        "qdec": jnp.exp(lg_k[None, :] * (pos + 1.0)[:, None]),
        "cdec": jnp.broadcast_to(jnp.exp(lg_k * CHUNK)[:, None], (MIX_W, MIX_W)),
        "bd": jnp.asarray((head_k[:, None] == head_v[None, :]).astype(np.float32)),
        "gmat": jnp.asarray((head_v[:, None] == head_v[None, :]).astype(np.float32) / HEAD_DIM).astype(BF16),
        "mk": jnp.asarray((head_k[None, :] == np.arange(N_GROUPS)[:, None]).astype(np.float32)),
        "mv": jnp.asarray((head_v[None, :] == np.arange(N_GROUPS)[:, None]).astype(np.float32)),
    }


def _mixer_params(conv_w, gv_g, w_s, b_s, ret_g, w_branch, w_o):
    p = dict(_mixer_consts())
    ws = jnp.tril(w_s)
    p.update({
        "conv_w": conv_w,
        "gv_g": gv_g.reshape(1, MIX_W),
        "ws_cat": jnp.transpose(ws, (1, 0, 2)).reshape(CHUNK, N_GROUPS * CHUNK).astype(BF16),
        "bs_mat": jnp.repeat(b_s.T, HEAD_DIM, axis=1),
        "ret_g": ret_g.reshape(1, MIX_W),
        "w_branch": w_branch.astype(BF16),
        "w_o": w_o.astype(BF16),
    })
    return p


def _moe_layout(meta, tot, n_tiles):
    totals = tot[0, :N_EXPERTS]
    padded = ((totals + TG_MOE - 1) // TG_MOE) * TG_MOE
    ends = jnp.cumsum(padded)
    starts = ends - padded
    seg_cnt = meta[:, 0, :N_EXPERTS]
    seg_loff = meta[:, 2, :N_EXPERTS]
    seg_grouped = starts[None, :] + meta[:, 1, :N_EXPERTS]
    tile_start = jnp.arange(n_tiles, dtype=jnp.int32) * TG_MOE
    tile_e = jnp.sum((tile_start[:, None] >= ends[None, :]).astype(jnp.int32), axis=1)
    used = (tile_start < ends[-1]).astype(jnp.int32)
    last_e = jnp.sum((ends[-1] - 1 >= ends).astype(jnp.int32))
    tile_e = jnp.minimum(jnp.where(used == 1, tile_e, last_e), N_EXPERTS - 1)
    flat = lambda a: a.reshape(-1).astype(jnp.int32)
    gaps = jnp.concatenate([flat(jnp.stack([starts + totals, padded - totals], axis=1)),
                            flat(ends[-1:] // TG_MOE)])
    return flat(seg_grouped), flat(seg_loff), flat(seg_cnt), gaps, tile_e, used


def kernel(x, c, positions, norm1_g, norm2_g, ada_w, ada_b, w_in, conv_w, cq_g, w_uq, ckv_g, w_ukv, qn_g, kn_g, gv_g, w_s, b_s, ret_g, w_branch, w_o, ffn_w1, ffn_w3, ffn_w2, router_w, router_b, moe_w1, moe_w3, moe_w2):
    batch, seq, d = x.shape
    depth = ada_w.shape[0]
    n = batch * seq
    assert seq % max(TM_PROJ, TM_PREP, TQ_ATT, TM_MIX, TM_FFN, TM_ROUTE) == 0
    assert d // 2 % LANES == 0

    c_t = jnp.pad(c, ((0, 8 - batch), (0, 0))).T
    ada = _ada(c_t, ada_w, ada_b, batch)[:, :batch].reshape(depth, batch, 6, d)
    cosr, sinr, cm, sm = _rope_tables(positions.astype(F32).reshape(n, 1))

    xt = x.reshape(n, d)
    for l in range(depth):
        mod = ada[l]
        proj = _inproj(xt, mod, norm1_g[l][None, :], _pack_w_in(w_in[l]), seq)
        mla_p, static_shift = _mla_params(cq_g[l], w_uq[l], ckv_g[l], w_ukv[l], qn_g[l], kn_g[l])
        q, k, v = _mla_prep(proj, cm, sm, mla_p)
        y_mla = lax.cond(static_shift,
                         functools.partial(_flash, batch=batch, seq=seq, online_max=False),
                         functools.partial(_flash, batch=batch, seq=seq, online_max=True), q, k, v)
        mp = _mixer_params(conv_w[l], gv_g[l], w_s[l], b_s[l], ret_g[l], w_branch[l], w_o[l])
        xt = _mixers(proj, y_mla, xt, cosr, sinr, mod, mp, seq)
        g2n = norm2_g[l][None, :]
        if l % 2 == 0:
            i = l // 2
            xt = _dense_ffn(xt, mod, g2n, ffn_w1[i].astype(BF16), ffn_w3[i].astype(BF16),
                            ffn_w2[i].astype(BF16), seq)
        else:
            i = l // 2
            rw = jnp.pad(router_w[i], ((0, 0), (0, LANES - N_EXPERTS)))
            rw_hi = rw.astype(BF16)
            rw_pad = jnp.concatenate([rw_hi, (rw - rw_hi.astype(F32)).astype(BF16)], axis=1)
            rb_pad = jnp.pad(router_b[i], (0, LANES - N_EXPERTS), constant_values=-1e30)[None, :]
            hs, ei, pw, meta, tot = _router(xt, mod, g2n, rw_pad, rb_pad, seq)
            max_rows = n * TOP_K + N_EXPERTS * (n // TM_ROUTE) * (ROW_ALIGN - 1)
            n_tiles = -(-max_rows // TG_MOE) + N_EXPERTS
            seg_grouped, seg_loff, seg_cnt, gaps, tile_e, used = _moe_layout(meta, tot, n_tiles)
            tile_rows = jnp.sum(seg_cnt.reshape(-1, N_EXPERTS), axis=1)
            xs = _dispatch(seg_loff, seg_grouped, seg_cnt, tile_rows, gaps, hs, n_tiles)
            y = _expert_ffn(tile_e, used, xs, moe_w1[i], moe_w3[i], moe_w2[i])
            xt = _combine(seg_grouped, seg_loff, seg_cnt, tile_rows, xt, ei, pw, mod, y, seq)
    return xt.reshape(batch, seq, d)
```

```python
import functools

import jax
import jax.numpy as jnp
import numpy as np
from jax import lax
from jax.experimental import pallas as pl
from jax.experimental.pallas import tpu as pltpu

F32 = jnp.float32
BF16 = jnp.bfloat16
HIGHEST = lax.Precision.HIGHEST

HEAD_DIM = 64
N_GROUPS = 4
MIX_W = N_GROUPS * HEAD_DIM
N_BRANCH = 4
CONV_W = 3
Q_LORA = 256
KV_LORA = 128
QK_NOPE = 64
QK_ROPE = 32
QK_HEAD = QK_NOPE + QK_ROPE
V_HEAD = 64
CHUNK = 128
N_EXPERTS = 8
TOP_K = 2
ROPE_THETA = 10000.0
EPS = 1e-6
LOG2_E = 1.4426950408889634
MAX_STATIC_SHIFT = 50.0

LANES = 128
VMEM_LIMIT_BYTES = 56 * 1024 * 1024

COL_GATES = 0
COL_A = 4096
COL_CQ = COL_A + 3 * MIX_W
COL_R = 5120
COL_SU = 6144
COL_CKV = 6656
COL_KRA = 6784
N_IN = 6912

TM_PROJ = 512
TN_PROJ = 768
TM_PREP = 1024
TQ_ATT = 1024
TM_MIX = 512
TM_FFN = 512
TM_ROUTE = 512
ROW_ALIGN = 8
SORT_ROWS = TOP_K * TM_ROUTE + N_EXPERTS * ROW_ALIGN
TG_MOE = 512
DISPATCH_SLOTS = 4
DISPATCH_AHEAD = 2
W_CHUNK = 512
W_SLOTS = 4


def _cparams(*sem):
    return pltpu.CompilerParams(dimension_semantics=sem, vmem_limit_bytes=VMEM_LIMIT_BYTES)


def _sigmoid(x):
    return jnp.tanh(x * 0.5) * 0.5 + 0.5


def _group_mean(x, gmat_bf16):
    hi = x.astype(BF16)
    lo = (x - hi.astype(F32)).astype(BF16)
    return (jnp.dot(hi, gmat_bf16, preferred_element_type=F32)
            + jnp.dot(lo, gmat_bf16, preferred_element_type=F32))


def _pack_bf16_pair(lo, hi):
    lo_bits = lax.bitcast_convert_type(lo.astype(BF16).astype(F32), jnp.uint32)
    hi_bits = lax.bitcast_convert_type(hi.astype(BF16).astype(F32), jnp.uint32)
    return (lo_bits >> 16) | (hi_bits & jnp.uint32(0xFFFF0000))


def _unpack_bf16_pair(p):
    lo = lax.bitcast_convert_type(p << 16, F32)
    hi = lax.bitcast_convert_type(p & jnp.uint32(0xFFFF0000), F32)
    return lo, hi


def _norm_mod(x, g, shift, scale):
    y = x * lax.rsqrt(jnp.mean(x * x, axis=-1, keepdims=True) + EPS)
    return (y * g) * (1.0 + scale) + shift


def _ada_kernel(ct_ref, w_ref, b_ref, o_ref, *, batch):
    ct = ct_ref[...]
    cond = ct * _sigmoid(ct)
    w = w_ref[...]
    o_ref[...] = jnp.zeros(o_ref.shape, F32)
    for b in range(batch):
        o_ref[b:b + 1, :] = jnp.sum(w * cond[:, b:b + 1], axis=0, keepdims=True) + b_ref[...]


def _ada(c_t, ada_w, ada_b, batch):
    n_layer, d, d6 = ada_w.shape
    rows = c_t.shape[1]
    tn = 1024
    return pl.pallas_call(
        functools.partial(_ada_kernel, batch=batch),
        out_shape=jax.ShapeDtypeStruct((n_layer, rows, d6), F32),
        grid=(n_layer, d6 // tn),
        in_specs=[
            pl.BlockSpec((d, rows), lambda l, j: (0, 0)),
            pl.BlockSpec((None, d, tn), lambda l, j: (l, 0, j)),
            pl.BlockSpec((None, 1, tn), lambda l, j: (l, 0, j)),
        ],
        out_specs=pl.BlockSpec((None, rows, tn), lambda l, j: (l, 0, j)),
        compiler_params=_cparams("parallel", "parallel"),
        name="ada_mod",
    )(c_t, ada_w, ada_b.reshape(n_layer, 1, d6))


def _rope_kernel(pos_ref, inv_ref, cr_ref, sr_ref, cm_ref, sm_ref):
    half_r = HEAD_DIM // 2
    half_m = QK_ROPE // 2
    ang = pos_ref[...] * inv_ref[...]
    c = jnp.cos(ang)
    s = jnp.sin(ang)
    lane = lax.broadcasted_iota(jnp.int32, c.shape, 1)

    def tile_r(t):
        t = jnp.where(lane < half_r, t, 0.0)
        out = t
        for k in range(1, LANES // half_r):
            out = out + pltpu.roll(t, k * half_r, axis=1)
        return out

    cr_ref[...] = tile_r(c)
    sr_ref[...] = tile_r(s)
    first = jnp.logical_and(lane >= QK_NOPE, lane < QK_NOPE + half_m)
    second = jnp.logical_and(lane >= QK_NOPE + half_m, lane < QK_HEAD)
    c1, c2 = pltpu.roll(c, QK_NOPE - half_r, axis=1), pltpu.roll(c, QK_NOPE + half_m - half_r, axis=1)
    s1, s2 = pltpu.roll(s, QK_NOPE - half_r, axis=1), pltpu.roll(s, QK_NOPE + half_m - half_r, axis=1)
    cm_ref[...] = jnp.where(first, c1, jnp.where(second, c2, 1.0))
    sm_ref[...] = jnp.where(first, -s1, jnp.where(second, s2, 0.0))


def _rope_tables(pos_f):
    n = pos_f.shape[0]
    tm = 1024
    half_r = HEAD_DIM // 2
    half_m = QK_ROPE // 2
    inv_r = ROPE_THETA ** (-jnp.arange(half_r, dtype=F32) / half_r)
    inv_m = ROPE_THETA ** (-jnp.arange(half_m, dtype=F32) / half_m)
    inv = jnp.concatenate([inv_r, inv_m, jnp.zeros((LANES - half_r - half_m,), F32)])[None, :]
    tab = pl.BlockSpec((tm, LANES), lambda i: (i, 0))
    shape = jax.ShapeDtypeStruct((n, LANES), F32)
    return pl.pallas_call(
        _rope_kernel,
        out_shape=(shape, shape, shape, shape),
        grid=(n // tm,),
        in_specs=[pl.BlockSpec((tm, 1), lambda i: (i, 0)), pl.BlockSpec((1, LANES), lambda i: (0, 0))],
        out_specs=(tab, tab, tab, tab),
        compiler_params=_cparams("parallel"),
        name="rope_tables",
    )(pos_f, inv)


def _inproj_kernel(x_ref, mod_ref, g_ref, w_ref, o_ref):
    h = _norm_mod(x_ref[...], g_ref[...], mod_ref[0:1, :], mod_ref[1:2, :]).astype(BF16)
    for c in range(N_IN // TN_PROJ):
        cols = slice(c * TN_PROJ, (c + 1) * TN_PROJ)
        o_ref[:, cols] = jnp.dot(h, w_ref[:, cols], preferred_element_type=F32).astype(BF16)


def _resident(shape):
    return pl.BlockSpec(shape, lambda *_: (0,) * len(shape), pipeline_mode=pl.Buffered(1))


def _inproj(x, mod, g, w, seq):
    n, d = x.shape
    tm = TM_PROJ
    tpb = seq // tm
    return pl.pallas_call(
        _inproj_kernel,
        out_shape=jax.ShapeDtypeStruct((n, N_IN), BF16),
        grid=(n // tm,),
        in_specs=[
            pl.BlockSpec((tm, d), lambda i: (i, 0)),
            pl.BlockSpec((None, 6, d), lambda i: (i // tpb, 0, 0)),
            pl.BlockSpec((1, d), lambda i: (0, 0)),
            _resident((d, N_IN)),
        ],
        out_specs=pl.BlockSpec((tm, N_IN), lambda i: (i, 0)),
        compiler_params=_cparams("parallel"),
        name="in_proj",
    )(x, mod, g, w)


def _mla_prep_kernel(cq_ref, ckv_ref, kra_ref, cm_ref, sm_ref, cqg_ref, wqa_ref, wqb_ref,
                     ckvg_ref, wk_ref, wv_ref, qga_ref, qgb_ref, kga_ref, kgb_ref,
                     qaug_ref, kaug_ref, vaug_ref, swap_ref, q_ref, k_ref, v_ref):
    cq = cq_ref[...].astype(F32)
    cqn = (cq * lax.rsqrt(jnp.mean(cq * cq, axis=-1, keepdims=True) + EPS) * cqg_ref[...]).astype(BF16)
    qa = jnp.dot(cqn, wqa_ref[...], preferred_element_type=F32)
    qb = jnp.dot(cqn, wqb_ref[...], preferred_element_type=F32)
    ckv = ckv_ref[...].astype(F32)
    ckvn = (ckv * lax.rsqrt(jnp.mean(ckv * ckv, axis=-1, keepdims=True) + EPS) * ckvg_ref[...]).astype(BF16)
    ka = jnp.dot(ckvn, wk_ref[...], preferred_element_type=F32)
    v_ref[...] = (jnp.dot(ckvn, wv_ref[...], preferred_element_type=F32) + vaug_ref[...]).astype(BF16)
    kra = kra_ref[...].astype(F32)
    krb = jnp.dot(kra_ref[...], swap_ref[...], preferred_element_type=F32)
    cm = cm_ref[...]
    sm = sm_ref[...]
    scale = QK_HEAD ** -0.5 * LOG2_E
    for h in range(N_GROUPS):
        sl = slice(h * LANES, (h + 1) * LANES)
        qah, qbh = qa[:, sl], qb[:, sl]
        r = lax.rsqrt(jnp.sum(qah * qah, axis=-1, keepdims=True) * (1.0 / QK_HEAD) + EPS)
        q_rot = (qah * r) * qga_ref[...] * cm + (qbh * r) * qgb_ref[...] * sm
        q_ref[:, sl] = (q_rot * scale + qaug_ref[...]).astype(BF16)
        kah = ka[:, sl] + kra
        kbh = ka[:, sl] + krb
        r = lax.rsqrt(jnp.sum(kah * kah, axis=-1, keepdims=True) * (1.0 / QK_HEAD) + EPS)
        k_rot = (kah * r) * kga_ref[...] * cm + (kbh * r) * kgb_ref[...] * sm
        k_ref[:, sl] = (k_rot + kaug_ref[...]).astype(BF16)


def _mla_prep(proj, cm, sm, p):
    n = proj.shape[0]
    tm = TM_PREP
    hw = N_GROUPS * LANES

    def col(width, offset):
        return pl.BlockSpec((tm, width), lambda i: (i, offset // width))

    def full(a):
        return pl.BlockSpec(a.shape, lambda i: (0,) * a.ndim)

    weights = [p["cq_g"], p["wqa"], p["wqb"], p["ckv_g"], p["wk"], p["wv"],
               p["qga"], p["qgb"], p["kga"], p["kgb"], p["qaug"], p["kaug"], p["vaug"], p["swap"]]
    head_tile = pl.BlockSpec((tm, hw), lambda i: (i, 0))
    out = jax.ShapeDtypeStruct((n, hw), BF16)
    return pl.pallas_call(
        _mla_prep_kernel,
        out_shape=(out, out, out),
        grid=(n // tm,),
        in_specs=[col(Q_LORA, COL_CQ), col(KV_LORA, COL_CKV), col(LANES, COL_KRA),
                  pl.BlockSpec((tm, LANES), lambda i: (i, 0)), pl.BlockSpec((tm, LANES), lambda i: (i, 0))]
                 + [full(w) for w in weights],
        out_specs=(head_tile, head_tile, head_tile),
        compiler_params=_cparams("parallel"),
        name="mla_prep",
    )(proj, proj, proj, cm, sm, *weights)


def _flash_kernel(q_ref, k_ref, v_ref, o_ref, acc_scr, *rest, tq, online_max):
    i = pl.program_id(1)
    j = pl.program_id(2)

    @pl.when(j == 0)
    def _():
        acc_scr[...] = jnp.zeros(acc_scr.shape, F32)
        if online_max:
            rest[0][...] = jnp.full(rest[0].shape, -jnp.inf, F32)

    def block(q0, nq, nk, masked):
        rows = slice(q0, q0 + nq)
        if masked:
            row = lax.broadcasted_iota(jnp.int32, (nq, nk), 0) + q0
            col = lax.broadcasted_iota(jnp.int32, (nq, nk), 1)
            keep = col <= row
        for h in range(N_GROUPS):
            sl = slice(h * LANES, (h + 1) * LANES)
            s = lax.dot_general(q_ref[rows, sl], k_ref[0:nk, sl], (((1,), (1,)), ((), ())),
                                preferred_element_type=F32)
            if masked:
                s = jnp.where(keep, s, -jnp.inf)
            if online_max:
                m_scr = rest[0]
                m_prev = m_scr[h, rows]
                m_new = jnp.maximum(m_prev, jnp.max(s, axis=-1, keepdims=True))
                p = jnp.exp2(s - m_new).astype(BF16)
                acc_scr[h, rows] = jnp.exp2(m_prev - m_new) * acc_scr[h, rows] + jnp.dot(
                    p, v_ref[0:nk, sl], preferred_element_type=F32)
                m_scr[h, rows] = m_new
            else:
                acc_scr[h, rows] += jnp.dot(jnp.exp2(s).astype(BF16), v_ref[0:nk, sl],
                                            preferred_element_type=F32)

    @pl.when(j < i)
    def _():
        block(0, tq, tq, False)

    @pl.when(j == i)
    def _():
        block(0, tq // 2, tq // 2, True)
        block(tq // 2, tq // 2, tq, True)
        lane = lax.broadcasted_iota(jnp.int32, (tq, LANES), 1)
        for pr in range(N_GROUPS // 2):
            lo = acc_scr[2 * pr]
            hi = acc_scr[2 * pr + 1]
            lo = lo / lo[:, V_HEAD:V_HEAD + 1]
            hi = hi / hi[:, V_HEAD:V_HEAD + 1]
            both = jnp.where(lane < V_HEAD, lo, pltpu.roll(hi, V_HEAD, axis=1))
            o_ref[:, pr * LANES:(pr + 1) * LANES] = both.astype(BF16)


def _flash(q, k, v, batch, seq, online_max):
    n = q.shape[0]
    tq = TQ_ATT
    nq = seq // tq
    hw = N_GROUPS * LANES
    scratch = [pltpu.VMEM((N_GROUPS, tq, LANES), F32)]
    if online_max:
        scratch.append(pltpu.VMEM((N_GROUPS, tq, 1), F32))
    return pl.pallas_call(
        functools.partial(_flash_kernel, tq=tq, online_max=online_max),
        out_shape=jax.ShapeDtypeStruct((n, MIX_W), BF16),
        grid=(batch, nq, nq),
        in_specs=[
            pl.BlockSpec((tq, hw), lambda b, i, j: (b * nq + i, 0)),
            pl.BlockSpec((tq, hw), lambda b, i, j: (b * nq + jnp.minimum(j, i), 0)),
            pl.BlockSpec((tq, hw), lambda b, i, j: (b * nq + jnp.minimum(j, i), 0)),
        ],
        out_specs=pl.BlockSpec((tq, MIX_W), lambda b, i, j: (b * nq + i, 0)),
        scratch_shapes=scratch,
        compiler_params=_cparams("parallel", "parallel", "arbitrary"),
        name="mla_flash_online" if online_max else "mla_flash",
    )(q, k, v)


def _gelu_tanh(x):
    return jax.nn.gelu(x, approximate=True)


def _mix_kernel(gates_ref, a_ref, r_ref, su_ref, ymla_ref, x_ref, cos_ref, sin_ref, mod_ref,
                convw_ref, gvg_ref, wscat_ref, bsmat_ref, retg_ref, dec_ref, kdec_ref, qdec_ref,
                cdec_ref, bd_ref, gmat_ref, mk_ref, mv_ref, wb_ref, wo_ref,
                o_ref, carry_scr, state_scr, ysg_scr, yret_scr, *, tm, tpb):
    i = pl.program_id(0)

    @pl.when(i % tpb == 0)
    def _():
        carry_scr[...] = jnp.zeros(carry_scr.shape, F32)
        state_scr[...] = jnp.zeros(state_scr.shape, F32)

    w = MIX_W
    a_b = a_ref[:, 0:w].astype(F32)
    u = a_ref[:, w:2 * w].astype(F32) * a_ref[:, 2 * w:3 * w].astype(F32)
    rowi = lax.broadcasted_iota(jnp.int32, (tm, w), 0)
    prev1 = carry_scr[0:1, :]
    prev2 = carry_scr[1:2, :]
    u1 = jnp.where(rowi == 0, prev1, pltpu.roll(u, 1, axis=0))
    u2 = jnp.where(rowi == 0, prev2, jnp.where(rowi == 1, prev1, pltpu.roll(u, 2, axis=0)))
    carry_scr[0:1, :] = u[tm - 1:tm, :]
    carry_scr[1:2, :] = u[tm - 2:tm - 1, :]
    y_conv = a_b * (convw_ref[0:1, :] * u2 + convw_ref[1:2, :] * u1 + convw_ref[2:3, :] * u)

    gmat = gmat_ref[...]
    s_u = _gelu_tanh(su_ref[:, 0:w].astype(F32))
    s_v = _gelu_tanh(su_ref[:, w:2 * w].astype(F32))
    ms = _group_mean(s_v * s_v, gmat)
    vn = (s_v * lax.rsqrt(ms + EPS) * gvg_ref[...]).astype(BF16)

    cosr = cos_ref[...]
    sinr = sin_ref[...]

    def rot(t):
        t1, t2 = t[:, 0:LANES], t[:, LANES:2 * LANES]
        return jnp.concatenate([t1 * cosr - t2 * sinr, t2 * cosr + t1 * sinr], axis=-1)

    rq = rot(r_ref[:, 0:w].astype(F32))
    rk = rot(r_ref[:, w:2 * w].astype(F32)) * (HEAD_DIM ** -0.5)

    for c in range(tm // CHUNK):
        rows = slice(c * CHUNK, (c + 1) * CHUNK)
        vc = vn[rows, :]
        vbd = jnp.concatenate([vc * mv_ref[g:g + 1, :].astype(BF16) for g in range(N_GROUPS)], axis=0)
        mixed = jnp.dot(wscat_ref[...], vbd, preferred_element_type=F32) + bsmat_ref[...]
        ysg_scr[rows, :] = s_u[rows, :] * mixed

        qc = rq[rows, :]
        kc = rk[rows, :]
        kcb = kc.astype(BF16)
        vcb = r_ref[rows, 2 * w:3 * w]
        qstack = jnp.concatenate([(qc * mk_ref[h:h + 1, :]).astype(BF16) for h in range(N_GROUPS)], axis=0)
        sc = lax.dot_general(qstack, kcb, (((1,), (1,)), ((), ())), preferred_element_type=F32)
        sc = (sc * dec_ref[...]).astype(BF16)
        scat = jnp.concatenate([sc[h * CHUNK:(h + 1) * CHUNK, :] for h in range(N_GROUPS)], axis=1)
        vstack = jnp.concatenate([vcb * mv_ref[h:h + 1, :].astype(BF16) for h in range(N_GROUPS)], axis=0)
        o_c = jnp.dot(scat, vstack, preferred_element_type=F32)
        state = state_scr[...]
        o_c = o_c + jnp.dot((qc * qdec_ref[...]).astype(BF16), state.astype(BF16),
                            preferred_element_type=F32)
        kd_t = jnp.transpose(kc * kdec_ref[...]).astype(BF16)
        kv = jnp.dot(kd_t, vcb, preferred_element_type=F32)
        state_scr[...] = state * cdec_ref[...] + kv * bd_ref[...]
        yret_scr[rows, :] = o_c

    o_all = yret_scr[...]
    xc = o_all - _group_mean(o_all, gmat)
    var = _group_mean(xc * xc, gmat)
    r_g = r_ref[:, 3 * w:4 * w].astype(F32)
    y_ret = (r_g * _sigmoid(r_g)) * (xc * lax.rsqrt(var + EPS) * retg_ref[...])

    d = x_ref.shape[1]
    ys = (y_conv, ymla_ref[...], ysg_scr[...], y_ret)
    merged = None
    for n in range(N_BRANCH):
        gate = _sigmoid(gates_ref[:, n * d:(n + 1) * d])
        term = gate * jnp.dot(ys[n].astype(BF16), wb_ref[n], preferred_element_type=F32).astype(BF16)
        merged = term if merged is None else merged + term
    out = jnp.dot(merged, wo_ref[...], preferred_element_type=F32)
    o_ref[...] = x_ref[...] + mod_ref[2:3, :] * out


def _mixers(proj, ymla, x, cosr, sinr, mod, p, seq):
    n, d = x.shape
    tm = TM_MIX
    tpb = seq // tm

    def col(width, offset):
        return pl.BlockSpec((tm, width), lambda i: (i, offset // width))

    def full(a):
        return pl.BlockSpec(a.shape, lambda i: (0,) * a.ndim)

    consts = [p["conv_w"], p["gv_g"], p["ws_cat"], p["bs_mat"], p["ret_g"], p["dec"], p["kdec"],
              p["qdec"], p["cdec"], p["bd"], p["gmat"], p["mk"], p["mv"], p["w_branch"], p["w_o"]]
    return pl.pallas_call(
        functools.partial(_mix_kernel, tm=tm, tpb=tpb),
        out_shape=jax.ShapeDtypeStruct((n, d), F32),
        grid=(n // tm,),
        in_specs=[col(N_BRANCH * d, COL_GATES), col(4 * MIX_W, COL_A), col(4 * MIX_W, COL_R),
                  col(2 * MIX_W, COL_SU),
                  pl.BlockSpec((tm, MIX_W), lambda i: (i, 0)),
                  pl.BlockSpec((tm, d), lambda i: (i, 0)),
                  pl.BlockSpec((tm, LANES), lambda i: (i, 0)),
                  pl.BlockSpec((tm, LANES), lambda i: (i, 0)),
                  pl.BlockSpec((None, 6, d), lambda i: (i // tpb, 0, 0))]
                 + [full(c) for c in consts],
        out_specs=pl.BlockSpec((tm, d), lambda i: (i, 0)),
        scratch_shapes=[pltpu.VMEM((8, MIX_W), F32), pltpu.VMEM((MIX_W, MIX_W), F32),
                        pltpu.VMEM((tm, MIX_W), F32), pltpu.VMEM((tm, MIX_W), F32)],
        compiler_params=_cparams("arbitrary"),
        name="mixers_merge",
    )(proj, proj, proj, proj, ymla, x, cosr, sinr, mod, *consts)


def _ffn_kernel(x_ref, mod_ref, g_ref, w1_ref, w3_ref, w2_ref, o_ref):
    x = x_ref[...]
    h = _norm_mod(x, g_ref[...], mod_ref[3:4, :], mod_ref[4:5, :]).astype(BF16)
    a = jnp.dot(h, w1_ref[...], preferred_element_type=F32)
    b = jnp.dot(h, w3_ref[...], preferred_element_type=F32)
    hid = ((a * _sigmoid(a)) * b).astype(BF16)
    o_ref[...] = x + mod_ref[5:6, :] * jnp.dot(hid, w2_ref[...], preferred_element_type=F32)


def _dense_ffn(x, mod, g, w1, w3, w2, seq):
    n, d = x.shape
    dff = w1.shape[1]
    tm = TM_FFN
    tpb = seq // tm
    return pl.pallas_call(
        _ffn_kernel,
        out_shape=jax.ShapeDtypeStruct((n, d), F32),
        grid=(n // tm,),
        in_specs=[
            pl.BlockSpec((tm, d), lambda i: (i, 0)),
            pl.BlockSpec((None, 6, d), lambda i: (i // tpb, 0, 0)),
            pl.BlockSpec((1, d), lambda i: (0, 0)),
            _resident((d, dff)), _resident((d, dff)), _resident((dff, d)),
        ],
        out_specs=pl.BlockSpec((tm, d), lambda i: (i, 0)),
        compiler_params=_cparams("parallel"),
        name="dense_swiglu",
    )(x, mod, g, w1, w3, w2)


def _router_kernel(x_ref, mod_ref, g_ref, rw_ref, rb_ref, hs_ref, ei_ref, pw_ref, meta_ref, tot_ref,
                   carry_scr, *, tm, srows):
    i = pl.program_id(0)

    @pl.when(i == 0)
    def _():
        carry_scr[...] = jnp.zeros(carry_scr.shape, F32)

    h = _norm_mod(x_ref[...], g_ref[...], mod_ref[3:4, :], mod_ref[4:5, :])

    h_hi = h.astype(BF16)
    h_lo = (h - h_hi.astype(F32)).astype(BF16)
    hw = jnp.dot(h_hi, rw_ref[...], preferred_element_type=F32)
    logits = (hw[:, :LANES] + hw[:, LANES:] + jnp.dot(h_lo, rw_ref[:, :LANES], preferred_element_type=F32)
              + rb_ref[...])
    mx = jnp.max(logits, axis=-1, keepdims=True)
    ex = jnp.exp(logits - mx)
    probs = ex / jnp.sum(ex, axis=-1, keepdims=True)
    lane = lax.broadcasted_iota(jnp.int32, (tm, LANES), 1)
    valid = lane < N_EXPERTS
    probs = jnp.where(valid, probs, -1.0)
    m1 = jnp.max(probs, axis=-1, keepdims=True)
    i1 = jnp.min(jnp.where(probs == m1, lane, LANES), axis=-1, keepdims=True)
    rest = jnp.where(lane == i1, -1.0, probs)
    m2 = jnp.max(rest, axis=-1, keepdims=True)
    i2 = jnp.min(jnp.where(rest == m2, lane, LANES), axis=-1, keepdims=True)
    den = m1 + m2
    pw_ref[...] = jnp.where(lane == 0, m1 / den, jnp.where(lane == 1, m2 / den, 0.0))

    sel1 = lane == i1
    sel2 = lane == i2
    onehot = jnp.where(sel1, 1.0, 0.0) + jnp.where(sel2, 1.0, 0.0)
    r_i = lax.broadcasted_iota(jnp.int32, (tm, tm), 0)
    c_i = lax.broadcasted_iota(jnp.int32, (tm, tm), 1)
    tri = jnp.where(c_i < r_i, 1.0, 0.0).astype(BF16)
    before = jnp.dot(tri, onehot.astype(BF16), preferred_element_type=F32)
    cnt = jnp.sum(onehot, axis=0, keepdims=True)
    cnt_al = jnp.floor((cnt + (ROW_ALIGN - 1)) * (1.0 / ROW_ALIGN)) * ROW_ALIGN
    e_r = lax.broadcasted_iota(jnp.int32, (LANES, LANES), 0)
    e_c = lax.broadcasted_iota(jnp.int32, (LANES, LANES), 1)
    upper = jnp.where(e_r < e_c, 1.0, 0.0)
    loff = jnp.dot(jnp.broadcast_to(cnt_al, (8, LANES)), upper, precision=HIGHEST,
                   preferred_element_type=F32)[0:1, :]
    slot = loff + before
    slot1 = jnp.sum(jnp.where(sel1, slot, 0.0), axis=-1, keepdims=True).astype(jnp.int32)
    slot2 = jnp.sum(jnp.where(sel2, slot, 0.0), axis=-1, keepdims=True).astype(jnp.int32)
    ei = jnp.where(lane == 0, i1, jnp.where(lane == 1, i2, 0))
    ei_ref[...] = jnp.where(lane == 2, slot1, jnp.where(lane == 3, slot2, ei))

    r_idx = lax.broadcasted_iota(jnp.int32, (tm, srows), 1)
    place = jnp.where(r_idx == slot1, 1.0, jnp.where(r_idx == slot2, 1.0, 0.0)).astype(BF16)
    hs = lax.dot_general(place, h.astype(BF16), (((0,), (0,)), ((), ())), preferred_element_type=F32)
    half = hs.shape[1] // 2
    hs_ref[...] = _pack_bf16_pair(hs[:, :half], hs[:, half:])

    carry = carry_scr[0:1, :]
    mrow = lax.broadcasted_iota(jnp.int32, (8, LANES), 0)
    meta = jnp.where(mrow == 0, cnt_al, jnp.where(mrow == 1, carry, jnp.where(mrow == 2, loff, 0.0)))
    meta_ref[...] = meta.astype(jnp.int32)
    carry_scr[0:1, :] = carry + cnt_al
    tot_ref[...] = jnp.broadcast_to(carry + cnt_al, tot_ref.shape).astype(jnp.int32)


def _router(x, mod, g, rw_pad, rb_pad, seq):
    n, d = x.shape
    tm = TM_ROUTE
    tpb = seq // tm
    nt = n // tm
    return pl.pallas_call(
        functools.partial(_router_kernel, tm=tm, srows=SORT_ROWS),
        out_shape=(jax.ShapeDtypeStruct((nt * SORT_ROWS, d // 2), jnp.uint32),
                   jax.ShapeDtypeStruct((n, LANES), jnp.int32),
                   jax.ShapeDtypeStruct((n, LANES), F32),
                   jax.ShapeDtypeStruct((nt, 8, LANES), jnp.int32),
                   jax.ShapeDtypeStruct((8, LANES), jnp.int32)),
        grid=(nt,),
        in_specs=[
            pl.BlockSpec((tm, d), lambda i: (i, 0)),
            pl.BlockSpec((None, 6, d), lambda i: (i // tpb, 0, 0)),
            pl.BlockSpec((1, d), lambda i: (0, 0)),
            pl.BlockSpec((d, 2 * LANES), lambda i: (0, 0)),
            pl.BlockSpec((1, LANES), lambda i: (0, 0)),
        ],
        out_specs=(pl.BlockSpec((SORT_ROWS, d // 2), lambda i: (i, 0)),
                   pl.BlockSpec((tm, LANES), lambda i: (i, 0)),
                   pl.BlockSpec((tm, LANES), lambda i: (i, 0)),
                   pl.BlockSpec((None, 8, LANES), lambda i: (i, 0, 0)),
                   pl.BlockSpec((8, LANES), lambda i: (0, 0))),
        scratch_shapes=[pltpu.VMEM((8, LANES), F32)],
        compiler_params=_cparams("arbitrary"),
        name="router_top2",
    )(x, mod, g, rw_pad, rb_pad)


def _segment_copy(src_hbm, dst_hbm, src_row, dst_row, n_rows, sem):
    src_row = pl.multiple_of(src_row, ROW_ALIGN)
    dst_row = pl.multiple_of(dst_row, ROW_ALIGN)
    n_rows = pl.multiple_of(n_rows, ROW_ALIGN)
    return pltpu.make_async_copy(src_hbm.at[pl.ds(src_row, n_rows)], dst_hbm.at[pl.ds(dst_row, n_rows)], sem)


def _dispatch_kernel(loff_ref, dst_ref, cnt_ref, rows_ref, gap_ref, hs_ref, xs_ref, buf, zbuf, sem_in, sem_out,
                     sem_zero, *, n_tiles, srows, n_out_tiles):
    zbuf[...] = jnp.zeros(zbuf.shape, zbuf.dtype)
    tg = zbuf.shape[0]
    first_unused = gap_ref[2 * N_EXPERTS]

    def zero_gap(e):
        return _segment_copy(zbuf, xs_ref, 0, gap_ref[2 * e], gap_ref[2 * e + 1], sem_zero)

    def zero_tile(t):
        return pltpu.make_async_copy(zbuf, xs_ref.at[pl.ds(pl.multiple_of(t * tg, ROW_ALIGN), tg)], sem_zero)

    def for_each_zero_copy(action):
        for e in range(N_EXPERTS):
            @pl.when(gap_ref[2 * e + 1] > 0)
            def _():
                action(zero_gap(e))

        def tail(t, carry):
            action(zero_tile(t))
            return carry

        lax.fori_loop(first_unused, n_out_tiles, tail, 0)

    for_each_zero_copy(lambda copy: copy.start())

    def fetch(t):
        slot = t % DISPATCH_SLOTS
        return pltpu.make_async_copy(hs_ref.at[pl.ds(pl.multiple_of(t * srows, ROW_ALIGN), srows)],
                                     buf.at[slot], sem_in.at[slot])

    def drain(t):
        slot = t % DISPATCH_SLOTS
        n_rows = rows_ref[t]

        @pl.when(n_rows > 0)
        def _():
            _segment_copy(buf.at[slot], xs_ref, 0, 0, n_rows, sem_out.at[slot]).wait()

    for t in range(DISPATCH_AHEAD):
        fetch(t).start()

    def body(t, carry):
        slot = t % DISPATCH_SLOTS
        fetch(t).wait()
        for e in range(N_EXPERTS):
            s = t * N_EXPERTS + e
            n_rows = cnt_ref[s]

            @pl.when(n_rows > 0)
            def _():
                _segment_copy(buf.at[slot], xs_ref, loff_ref[s], dst_ref[s], n_rows, sem_out.at[slot]).start()

        @pl.when(t + DISPATCH_AHEAD < n_tiles)
        def _():
            @pl.when(t + DISPATCH_AHEAD >= DISPATCH_SLOTS)
            def _():
                drain(t + DISPATCH_AHEAD - DISPATCH_SLOTS)

            fetch(t + DISPATCH_AHEAD).start()

        return carry

    lax.fori_loop(0, n_tiles, body, 0)
    for t in range(max(n_tiles - DISPATCH_SLOTS, 0), n_tiles):
        drain(t)
    for_each_zero_copy(lambda copy: copy.wait())


def _dispatch(seg_loff, seg_dst, seg_cnt, tile_rows, gaps, hs, n_out_tiles):
    n_tiles = tile_rows.shape[0]
    srows = hs.shape[0] // n_tiles
    assert n_tiles >= DISPATCH_SLOTS
    return pl.pallas_call(
        functools.partial(_dispatch_kernel, n_tiles=n_tiles, srows=srows, n_out_tiles=n_out_tiles),
        out_shape=jax.ShapeDtypeStruct((n_out_tiles * TG_MOE, hs.shape[1]), hs.dtype),
        grid_spec=pltpu.PrefetchScalarGridSpec(
            num_scalar_prefetch=5,
            grid=(1,),
            in_specs=[pl.BlockSpec(memory_space=pl.ANY)],
            out_specs=pl.BlockSpec(memory_space=pl.ANY),
            scratch_shapes=[pltpu.VMEM((DISPATCH_SLOTS, srows, hs.shape[1]), hs.dtype),
                            pltpu.VMEM((TG_MOE, hs.shape[1]), hs.dtype),
                            pltpu.SemaphoreType.DMA((DISPATCH_SLOTS,)),
                            pltpu.SemaphoreType.DMA((DISPATCH_SLOTS,)),
                            pltpu.SemaphoreType.DMA],
        ),
        compiler_params=_cparams("arbitrary"),
        name="moe_dispatch",
    )(seg_loff, seg_dst, seg_cnt, tile_rows, gaps, hs)


def _expert_kernel(te_ref, used_ref, xs_ref, w1_hbm, w3_hbm, w2_hbm, y_ref, wb1, wb3, wb2, stage, sem):
    t = pl.program_id(0)
    e = te_ref[t]
    first_of_expert = jnp.logical_or(t == 0, e != te_ref[jnp.maximum(t - 1, 0)])

    @pl.when(jnp.logical_and(used_ref[t] == 1, first_of_expert))
    def _():
        n_slots = stage.shape[0]
        windows = [(src, dst, r, c) for src, dst in ((w1_hbm, wb1), (w3_hbm, wb3), (w2_hbm, wb2))
                   for r in range(dst.shape[0] // W_CHUNK) for c in range(dst.shape[1] // W_CHUNK)]

        def staged_copy(k):
            src, _, r, c = windows[k]
            return pltpu.make_async_copy(src.at[e, pl.ds(r * W_CHUNK, W_CHUNK), pl.ds(c * W_CHUNK, W_CHUNK)],
                                         stage.at[k % n_slots], sem.at[k % n_slots])

        for k in range(n_slots - 1):
            staged_copy(k).start()
        for k, (_, dst, r, c) in enumerate(windows):
            staged_copy(k).wait()
            dst[r * W_CHUNK:(r + 1) * W_CHUNK, c * W_CHUNK:(c + 1) * W_CHUNK] = stage[k % n_slots].astype(BF16)
            if k + n_slots - 1 < len(windows):
                staged_copy(k + n_slots - 1).start()

    @pl.when(used_ref[t] == 1)
    def _():
        lo, hi = _unpack_bf16_pair(xs_ref[...])
        h = jnp.concatenate([lo.astype(BF16), hi.astype(BF16)], axis=1)
        a = jnp.dot(h, wb1[...], preferred_element_type=F32)
        b = jnp.dot(h, wb3[...], preferred_element_type=F32)
        hid = ((a * _sigmoid(a)) * b).astype(BF16)
        acc = jnp.dot(hid, wb2[...], preferred_element_type=F32)
        half = acc.shape[1] // 2
        y_ref[...] = _pack_bf16_pair(acc[:, :half], acc[:, half:])

    @pl.when(used_ref[t] == 0)
    def _():
        y_ref[...] = jnp.zeros(y_ref.shape, y_ref.dtype)


def _expert_ffn(tile_e, tile_used, xs, w1, w3, w2):
    rows, half = xs.shape
    d = 2 * half
    dff = w1.shape[2]
    tg = TG_MOE
    assert dff % W_CHUNK == 0
    hbm = pl.BlockSpec(memory_space=pl.ANY)
    return pl.pallas_call(
        _expert_kernel,
        out_shape=jax.ShapeDtypeStruct((rows, half), jnp.uint32),
        grid_spec=pltpu.PrefetchScalarGridSpec(
            num_scalar_prefetch=2,
            grid=(rows // tg,),
            in_specs=[pl.BlockSpec((tg, half), lambda t, te, us: (t, 0)), hbm, hbm, hbm],
            out_specs=pl.BlockSpec((tg, half), lambda t, te, us: (t, 0)),
            scratch_shapes=[pltpu.VMEM((d, dff), BF16), pltpu.VMEM((d, dff), BF16), pltpu.VMEM((dff, d), BF16),
                            pltpu.VMEM((W_SLOTS, W_CHUNK, W_CHUNK), F32), pltpu.SemaphoreType.DMA((W_SLOTS,))],
        ),
        compiler_params=_cparams("arbitrary"),
        name="expert_swiglu",
    )(tile_e, tile_used, xs, w1, w3, w2)


def _combine_kernel(src_ref, loff_ref, cnt_ref, rows_ref, x_ref, ei_ref, pw_ref, mod_ref, y_ref, o_ref,
                    ybuf, sem, *, tm, srows):
    i = pl.program_id(0)
    slot = i % 2

    def fetch(tile, into):
        ybuf[into] = jnp.zeros(ybuf.shape[1:], ybuf.dtype)
        for e in range(N_EXPERTS):
            s = tile * N_EXPERTS + e
            n_rows = cnt_ref[s]

            @pl.when(n_rows > 0)
            def _():
                _segment_copy(y_ref, ybuf.at[into], src_ref[s], loff_ref[s], n_rows, sem.at[into]).start()

    @pl.when(i == 0)
    def _():
        fetch(i, slot)

    @pl.when(i + 1 < pl.num_programs(0))
    def _():
        fetch(i + 1, 1 - slot)

    @pl.when(rows_ref[i] > 0)
    def _():
        _segment_copy(y_ref, ybuf.at[slot], 0, 0, rows_ref[i], sem.at[slot]).wait()

    lo, hi = _unpack_bf16_pair(ybuf[slot])
    ys = jnp.concatenate([lo.astype(BF16), hi.astype(BF16)], axis=1)
    r_idx = lax.broadcasted_iota(jnp.int32, (tm, srows), 1)
    mix = jnp.zeros(x_ref.shape, F32)
    for k in range(TOP_K):
        pick = jnp.where(r_idx == ei_ref[:, TOP_K + k:TOP_K + k + 1], 1.0, 0.0).astype(BF16)
        mix = mix + pw_ref[:, k:k + 1] * jnp.dot(pick, ys, preferred_element_type=F32)
    o_ref[...] = x_ref[...] + mod_ref[5:6, :] * mix


def _combine(seg_src, seg_loff, seg_cnt, tile_rows, x, ei, pw, mod, y, seq):
    n, d = x.shape
    tm = TM_ROUTE
    tpb = seq // tm
    tok = lambda width: pl.BlockSpec((tm, width), lambda i, *_: (i, 0))
    return pl.pallas_call(
        functools.partial(_combine_kernel, tm=tm, srows=SORT_ROWS),
        out_shape=jax.ShapeDtypeStruct((n, d), F32),
        grid_spec=pltpu.PrefetchScalarGridSpec(
            num_scalar_prefetch=4,
            grid=(n // tm,),
            in_specs=[tok(d), tok(LANES), tok(LANES),
                      pl.BlockSpec((None, 6, d), lambda i, *_: (i // tpb, 0, 0)),
                      pl.BlockSpec(memory_space=pl.ANY)],
            out_specs=tok(d),
            scratch_shapes=[pltpu.VMEM((2, SORT_ROWS, d // 2), jnp.uint32), pltpu.SemaphoreType.DMA((2,))],
        ),
        compiler_params=_cparams("arbitrary"),
        name="moe_combine",
    )(seg_src, seg_loff, seg_cnt, tile_rows, x, ei, pw, mod, y)


def _pack_w_in(w_in):
    d = w_in.shape[0]
    w = MIX_W
    o_ckv = 3 * w + Q_LORA
    o_kr = o_ckv + KV_LORA
    o_su = o_kr + QK_ROPE
    o_rq = o_su + 2 * w
    o_gate = o_rq + 4 * w
    half = HEAD_DIM // 2
    perm = np.array([h * HEAD_DIM + part * half + i
                     for part in range(2) for h in range(N_GROUPS) for i in range(half)])
    kr = w_in[:, o_kr:o_kr + QK_ROPE]
    z = lambda k: jnp.zeros((d, k), w_in.dtype)
    cols = [
        w_in[:, o_gate:o_gate + N_BRANCH * d],
        w_in[:, 0:3 * w + Q_LORA],
        w_in[:, o_rq:o_rq + w][:, perm], w_in[:, o_rq + w:o_rq + 2 * w][:, perm],
        w_in[:, o_rq + 2 * w:o_rq + 4 * w],
        w_in[:, o_su:o_su + 2 * w],
        w_in[:, o_ckv:o_ckv + KV_LORA],
        z(QK_NOPE), kr, z(LANES - QK_HEAD),
    ]
    return jnp.concatenate(cols, axis=1).astype(BF16)


def _swap_rope_halves(a):
    hr = QK_ROPE // 2
    return jnp.concatenate([a[..., :QK_NOPE], a[..., QK_NOPE + hr:QK_HEAD], a[..., QK_NOPE:QK_NOPE + hr],
                            a[..., QK_HEAD:]], axis=-1)


def _mla_params(cq_g, w_uq, ckv_g, w_ukv, qn_g, kn_g):
    pad = LANES - QK_HEAD
    wq = w_uq.reshape(Q_LORA, N_GROUPS, QK_HEAD)
    wq = jnp.pad(wq, ((0, 0), (0, 0), (0, pad)))
    wkv = w_ukv.reshape(KV_LORA, N_GROUPS, QK_NOPE + V_HEAD)
    wk = jnp.pad(wkv[:, :, :QK_NOPE], ((0, 0), (0, 0), (0, LANES - QK_NOPE)))
    wv = jnp.pad(wkv[:, :, QK_NOPE:], ((0, 0), (0, 0), (0, LANES - V_HEAD)))
    qg = jnp.pad(qn_g, (0, pad))[None, :]
    kg = jnp.pad(kn_g, (0, pad))[None, :]
    bound = (QK_HEAD ** 0.5 * LOG2_E) * jnp.max(jnp.abs(qn_g)) * jnp.max(jnp.abs(kn_g))
    static_shift = bound <= MAX_STATIC_SHIFT
    lane = jnp.arange(LANES)
    qaug = (lane == QK_HEAD).astype(F32)[None, :]
    kaug = qaug * jnp.where(static_shift, -bound, 0.0)
    vaug = jnp.tile((lane == V_HEAD).astype(F32), N_GROUPS)[None, :]
    params = {
        "cq_g": cq_g[None, :], "ckv_g": ckv_g[None, :],
        "wqa": wq.reshape(Q_LORA, -1).astype(BF16),
        "wqb": _swap_rope_halves(wq).reshape(Q_LORA, -1).astype(BF16),
        "wk": wk.reshape(KV_LORA, -1).astype(BF16),
        "wv": wv.reshape(KV_LORA, -1).astype(BF16),
        "qga": qg, "qgb": _swap_rope_halves(qg), "kga": kg, "kgb": _swap_rope_halves(kg),
        "qaug": qaug, "kaug": kaug, "vaug": vaug,
        "swap": (_swap_rope_halves(lane[None, :])[0][None, :] == lane[:, None]).astype(BF16),
    }
    return params, static_shift


def _mixer_consts():
    h = jnp.arange(N_GROUPS, dtype=F32)
    log_gamma = jnp.log1p(-(2.0 ** (-5.0 - h)))
    pos = jnp.arange(CHUNK, dtype=F32)
    rel = pos[:, None] - pos[None, :]
    dec = jnp.where(rel >= 0, jnp.exp(log_gamma[:, None, None] * jnp.maximum(rel, 0.0)), 0.0)
    lane = np.arange(MIX_W)
    head_k = (lane % LANES) // (HEAD_DIM // 2)
    head_v = lane // HEAD_DIM
    lg_k = log_gamma[head_k]
    return {
        "dec": dec.reshape(N_GROUPS * CHUNK, CHUNK),
        "kdec": jnp.exp(lg_k[None, :] * (CHUNK - 1.0 - pos)[:, None]),
        "qdec": jnp.exp(lg_k[None, :] * (pos + 1.0)[:, None]),
        "cdec": jnp.broadcast_to(jnp.exp(lg_k * CHUNK)[:, None], (MIX_W, MIX_W)),
        "bd": jnp.asarray((head_k[:, None] == head_v[None, :]).astype(np.float32)),
        "gmat": jnp.asarray((head_v[:, None] == head_v[None, :]).astype(np.float32) / HEAD_DIM).astype(BF16),
        "mk": jnp.asarray((head_k[None, :] == np.arange(N_GROUPS)[:, None]).astype(np.float32)),
        "mv": jnp.asarray((head_v[None, :] == np.arange(N_GROUPS)[:, None]).astype(np.float32)),
    }


def _mixer_params(conv_w, gv_g, w_s, b_s, ret_g, w_branch, w_o):
    p = dict(_mixer_consts())
    ws = jnp.tril(w_s)
    p.update({
        "conv_w": conv_w,
        "gv_g": gv_g.reshape(1, MIX_W),
        "ws_cat": jnp.transpose(ws, (1, 0, 2)).reshape(CHUNK, N_GROUPS * CHUNK).astype(BF16),
        "bs_mat": jnp.repeat(b_s.T, HEAD_DIM, axis=1),
        "ret_g": ret_g.reshape(1, MIX_W),
        "w_branch": w_branch.astype(BF16),
        "w_o": w_o.astype(BF16),
    })
    return p


def _moe_layout(meta, tot, n_tiles):
    totals = tot[0, :N_EXPERTS]
    padded = ((totals + TG_MOE - 1) // TG_MOE) * TG_MOE
    ends = jnp.cumsum(padded)
    starts = ends - padded
    seg_cnt = meta[:, 0, :N_EXPERTS]
    seg_loff = meta[:, 2, :N_EXPERTS]
    seg_grouped = starts[None, :] + meta[:, 1, :N_EXPERTS]
    tile_start = jnp.arange(n_tiles, dtype=jnp.int32) * TG_MOE
    tile_e = jnp.sum((tile_start[:, None] >= ends[None, :]).astype(jnp.int32), axis=1)
    used = (tile_start < ends[-1]).astype(jnp.int32)
    last_e = jnp.sum((ends[-1] - 1 >= ends).astype(jnp.int32))
    tile_e = jnp.minimum(jnp.where(used == 1, tile_e, last_e), N_EXPERTS - 1)
    flat = lambda a: a.reshape(-1).astype(jnp.int32)
    gaps = jnp.concatenate([flat(jnp.stack([starts + totals, padded - totals], axis=1)),
                            flat(ends[-1:] // TG_MOE)])
    return flat(seg_grouped), flat(seg_loff), flat(seg_cnt), gaps, tile_e, used


def kernel(x, c, positions, norm1_g, norm2_g, ada_w, ada_b, w_in, conv_w, cq_g, w_uq, ckv_g, w_ukv, qn_g, kn_g, gv_g, w_s, b_s, ret_g, w_branch, w_o, ffn_w1, ffn_w3, ffn_w2, router_w, router_b, moe_w1, moe_w3, moe_w2):
    batch, seq, d = x.shape
    depth = ada_w.shape[0]
    n = batch * seq
    assert seq % max(TM_PROJ, TM_PREP, TQ_ATT, TM_MIX, TM_FFN, TM_ROUTE) == 0
    assert d // 2 % LANES == 0

    c_t = jnp.pad(c, ((0, 8 - batch), (0, 0))).T
    ada = _ada(c_t, ada_w, ada_b, batch)[:, :batch].reshape(depth, batch, 6, d)
    cosr, sinr, cm, sm = _rope_tables(positions.astype(F32).reshape(n, 1))

    xt = x.reshape(n, d)
    for l in range(depth):
        mod = ada[l]
        proj = _inproj(xt, mod, norm1_g[l][None, :], _pack_w_in(w_in[l]), seq)
        mla_p, static_shift = _mla_params(cq_g[l], w_uq[l], ckv_g[l], w_ukv[l], qn_g[l], kn_g[l])
        q, k, v = _mla_prep(proj, cm, sm, mla_p)
        y_mla = lax.cond(static_shift,
                         functools.partial(_flash, batch=batch, seq=seq, online_max=False),
                         functools.partial(_flash, batch=batch, seq=seq, online_max=True), q, k, v)
        mp = _mixer_params(conv_w[l], gv_g[l], w_s[l], b_s[l], ret_g[l], w_branch[l], w_o[l])
        xt = _mixers(proj, y_mla, xt, cosr, sinr, mod, mp, seq)
        g2n = norm2_g[l][None, :]
        if l % 2 == 0:
            i = l // 2
            xt = _dense_ffn(xt, mod, g2n, ffn_w1[i].astype(BF16), ffn_w3[i].astype(BF16),
                            ffn_w2[i].astype(BF16), seq)
        else:
            i = l // 2
            rw = jnp.pad(router_w[i], ((0, 0), (0, LANES - N_EXPERTS)))
            rw_hi = rw.astype(BF16)
            rw_pad = jnp.concatenate([rw_hi, (rw - rw_hi.astype(F32)).astype(BF16)], axis=1)
            rb_pad = jnp.pad(router_b[i], (0, LANES - N_EXPERTS), constant_values=-1e30)[None, :]
            hs, ei, pw, meta, tot = _router(xt, mod, g2n, rw_pad, rb_pad, seq)
            max_rows = n * TOP_K + N_EXPERTS * (n // TM_ROUTE) * (ROW_ALIGN - 1)
            n_tiles = -(-max_rows // TG_MOE) + N_EXPERTS
            seg_grouped, seg_loff, seg_cnt, gaps, tile_e, used = _moe_layout(meta, tot, n_tiles)
            tile_rows = jnp.sum(seg_cnt.reshape(-1, N_EXPERTS), axis=1)
            xs = _dispatch(seg_loff, seg_grouped, seg_cnt, tile_rows, gaps, hs, n_tiles)
            y = _expert_ffn(tile_e, used, xs, moe_w1[i], moe_w3[i], moe_w2[i])
            xt = _combine(seg_grouped, seg_loff, seg_cnt, tile_rows, xt, ei, pw, mod, y, seq)
    return xt.reshape(batch, seq, d)
```

```python
import functools

import jax
import jax.numpy as jnp
import numpy as np
from jax import lax
from jax.experimental import pallas as pl
from jax.experimental.pallas import tpu as pltpu

F32 = jnp.float32
BF16 = jnp.bfloat16
HIGHEST = lax.Precision.HIGHEST

HEAD_DIM = 64
N_GROUPS = 4
MIX_W = N_GROUPS * HEAD_DIM
N_BRANCH = 4
CONV_W = 3
Q_LORA = 256
KV_LORA = 128
QK_NOPE = 64
QK_ROPE = 32
QK_HEAD = QK_NOPE + QK_ROPE
V_HEAD = 64
CHUNK = 128
N_EXPERTS = 8
TOP_K = 2
ROPE_THETA = 10000.0
EPS = 1e-6
LOG2_E = 1.4426950408889634
MAX_STATIC_SHIFT = 50.0

LANES = 128
VMEM_LIMIT_BYTES = 56 * 1024 * 1024

COL_GATES = 0
COL_A = 4096
COL_CQ = COL_A + 3 * MIX_W
COL_R = 5120
COL_SU = 6144
COL_CKV = 6656
COL_KRA = 6784
N_IN = 6912

TM_PROJ = 512
TN_PROJ = 768
TM_PREP = 1024
TQ_ATT = 1024
TM_MIX = 512
TM_FFN = 512
TM_ROUTE = 512
ROW_ALIGN = 8
SORT_ROWS = TOP_K * TM_ROUTE + N_EXPERTS * ROW_ALIGN
TG_MOE = 512
DISPATCH_SLOTS = 4
DISPATCH_AHEAD = 2
W_CHUNK = 512
W_SLOTS = 6


def _cparams(*sem):
    return pltpu.CompilerParams(dimension_semantics=sem, vmem_limit_bytes=VMEM_LIMIT_BYTES)


def _sigmoid(x):
    return jnp.tanh(x * 0.5) * 0.5 + 0.5


def _group_mean(x, gmat_bf16):
    hi = x.astype(BF16)
    lo = (x - hi.astype(F32)).astype(BF16)
    return (jnp.dot(hi, gmat_bf16, preferred_element_type=F32)
            + jnp.dot(lo, gmat_bf16, preferred_element_type=F32))


def _pack_bf16_pair(lo, hi):
    lo_bits = lax.bitcast_convert_type(lo.astype(BF16).astype(F32), jnp.uint32)
    hi_bits = lax.bitcast_convert_type(hi.astype(BF16).astype(F32), jnp.uint32)
    return (lo_bits >> 16) | (hi_bits & jnp.uint32(0xFFFF0000))


def _unpack_bf16_pair(p):
    lo = lax.bitcast_convert_type(p << 16, F32)
    hi = lax.bitcast_convert_type(p & jnp.uint32(0xFFFF0000), F32)
    return lo, hi


def _norm_mod(x, g, shift, scale):
    y = x * lax.rsqrt(jnp.mean(x * x, axis=-1, keepdims=True) + EPS)
    return (y * g) * (1.0 + scale) + shift


def _ada_kernel(ct_ref, w_ref, b_ref, o_ref, *, batch):
    ct = ct_ref[...]
    cond = ct * _sigmoid(ct)
    w = w_ref[...]
    o_ref[...] = jnp.zeros(o_ref.shape, F32)
    for b in range(batch):
        o_ref[b:b + 1, :] = jnp.sum(w * cond[:, b:b + 1], axis=0, keepdims=True) + b_ref[...]


def _ada(c_t, ada_w, ada_b, batch):
    n_layer, d, d6 = ada_w.shape
    rows = c_t.shape[1]
    tn = 1024
    return pl.pallas_call(
        functools.partial(_ada_kernel, batch=batch),
        out_shape=jax.ShapeDtypeStruct((n_layer, rows, d6), F32),
        grid=(n_layer, d6 // tn),
        in_specs=[
            pl.BlockSpec((d, rows), lambda l, j: (0, 0)),
            pl.BlockSpec((None, d, tn), lambda l, j: (l, 0, j)),
            pl.BlockSpec((None, 1, tn), lambda l, j: (l, 0, j)),
        ],
        out_specs=pl.BlockSpec((None, rows, tn), lambda l, j: (l, 0, j)),
        compiler_params=_cparams("parallel", "parallel"),
        name="ada_mod",
    )(c_t, ada_w, ada_b.reshape(n_layer, 1, d6))


def _rope_kernel(pos_ref, inv_ref, cr_ref, sr_ref, cm_ref, sm_ref):
    half_r = HEAD_DIM // 2
    half_m = QK_ROPE // 2
    ang = pos_ref[...] * inv_ref[...]
    c = jnp.cos(ang)
    s = jnp.sin(ang)
    lane = lax.broadcasted_iota(jnp.int32, c.shape, 1)

    def tile_r(t):
        t = jnp.where(lane < half_r, t, 0.0)
        out = t
        for k in range(1, LANES // half_r):
            out = out + pltpu.roll(t, k * half_r, axis=1)
        return out

    cr_ref[...] = tile_r(c)
    sr_ref[...] = tile_r(s)
    first = jnp.logical_and(lane >= QK_NOPE, lane < QK_NOPE + half_m)
    second = jnp.logical_and(lane >= QK_NOPE + half_m, lane < QK_HEAD)
    c1, c2 = pltpu.roll(c, QK_NOPE - half_r, axis=1), pltpu.roll(c, QK_NOPE + half_m - half_r, axis=1)
    s1, s2 = pltpu.roll(s, QK_NOPE - half_r, axis=1), pltpu.roll(s, QK_NOPE + half_m - half_r, axis=1)
    cm_ref[...] = jnp.where(first, c1, jnp.where(second, c2, 1.0))
    sm_ref[...] = jnp.where(first, -s1, jnp.where(second, s2, 0.0))


def _rope_tables(pos_f):
    n = pos_f.shape[0]
    tm = 1024
    half_r = HEAD_DIM // 2
    half_m = QK_ROPE // 2
    inv_r = ROPE_THETA ** (-jnp.arange(half_r, dtype=F32) / half_r)
    inv_m = ROPE_THETA ** (-jnp.arange(half_m, dtype=F32) / half_m)
    inv = jnp.concatenate([inv_r, inv_m, jnp.zeros((LANES - half_r - half_m,), F32)])[None, :]
    tab = pl.BlockSpec((tm, LANES), lambda i: (i, 0))
    shape = jax.ShapeDtypeStruct((n, LANES), F32)
    return pl.pallas_call(
        _rope_kernel,
        out_shape=(shape, shape, shape, shape),
        grid=(n // tm,),
        in_specs=[pl.BlockSpec((tm, 1), lambda i: (i, 0)), pl.BlockSpec((1, LANES), lambda i: (0, 0))],
        out_specs=(tab, tab, tab, tab),
        compiler_params=_cparams("parallel"),
        name="rope_tables",
    )(pos_f, inv)


def _inproj_kernel(x_ref, mod_ref, g_ref, w_ref, o_ref):
    h = _norm_mod(x_ref[...], g_ref[...], mod_ref[0:1, :], mod_ref[1:2, :]).astype(BF16)
    for c in range(N_IN // TN_PROJ):
        cols = slice(c * TN_PROJ, (c + 1) * TN_PROJ)
        o_ref[:, cols] = jnp.dot(h, w_ref[:, cols], preferred_element_type=F32).astype(BF16)


def _resident(shape):
    return pl.BlockSpec(shape, lambda *_: (0,) * len(shape), pipeline_mode=pl.Buffered(1))


def _inproj(x, mod, g, w, seq):
    n, d = x.shape
    tm = TM_PROJ
    tpb = seq // tm
    return pl.pallas_call(
        _inproj_kernel,
        out_shape=jax.ShapeDtypeStruct((n, N_IN), BF16),
        grid=(n // tm,),
        in_specs=[
            pl.BlockSpec((tm, d), lambda i: (i, 0)),
            pl.BlockSpec((None, 6, d), lambda i: (i // tpb, 0, 0)),
            pl.BlockSpec((1, d), lambda i: (0, 0)),
            _resident((d, N_IN)),
        ],
        out_specs=pl.BlockSpec((tm, N_IN), lambda i: (i, 0)),
        compiler_params=_cparams("parallel"),
        name="in_proj",
    )(x, mod, g, w)


def _mla_prep_kernel(cq_ref, ckv_ref, kra_ref, cm_ref, sm_ref, cqg_ref, wqa_ref, wqb_ref,
                     ckvg_ref, wk_ref, wv_ref, qga_ref, qgb_ref, kga_ref, kgb_ref,
                     qaug_ref, kaug_ref, vaug_ref, swap_ref, q_ref, k_ref, v_ref):
    cq = cq_ref[...].astype(F32)
    cqn = (cq * lax.rsqrt(jnp.mean(cq * cq, axis=-1, keepdims=True) + EPS) * cqg_ref[...]).astype(BF16)
    qa = jnp.dot(cqn, wqa_ref[...], preferred_element_type=F32)
    qb = jnp.dot(cqn, wqb_ref[...], preferred_element_type=F32)
    ckv = ckv_ref[...].astype(F32)
    ckvn = (ckv * lax.rsqrt(jnp.mean(ckv * ckv, axis=-1, keepdims=True) + EPS) * ckvg_ref[...]).astype(BF16)
    ka = jnp.dot(ckvn, wk_ref[...], preferred_element_type=F32)
    v_ref[...] = (jnp.dot(ckvn, wv_ref[...], preferred_element_type=F32) + vaug_ref[...]).astype(BF16)
    kra = kra_ref[...].astype(F32)
    krb = jnp.dot(kra_ref[...], swap_ref[...], preferred_element_type=F32)
    cm = cm_ref[...]
    sm = sm_ref[...]
    scale = QK_HEAD ** -0.5 * LOG2_E
    q_cos, q_sin = cm * (qga_ref[...] * scale), sm * (qgb_ref[...] * scale)
    k_cos, k_sin = cm * kga_ref[...], sm * kgb_ref[...]
    for h in range(N_GROUPS):
        sl = slice(h * LANES, (h + 1) * LANES)
        qah, qbh = qa[:, sl], qb[:, sl]
        r = lax.rsqrt(jnp.sum(qah * qah, axis=-1, keepdims=True) * (1.0 / QK_HEAD) + EPS)
        q_ref[:, sl] = ((qah * q_cos + qbh * q_sin) * r + qaug_ref[...]).astype(BF16)
        kah = ka[:, sl] + kra
        kbh = ka[:, sl] + krb
        r = lax.rsqrt(jnp.sum(kah * kah, axis=-1, keepdims=True) * (1.0 / QK_HEAD) + EPS)
        k_ref[:, sl] = ((kah * k_cos + kbh * k_sin) * r + kaug_ref[...]).astype(BF16)


def _mla_prep(proj, cm, sm, p):
    n = proj.shape[0]
    tm = TM_PREP
    hw = N_GROUPS * LANES

    def col(width, offset):
        return pl.BlockSpec((tm, width), lambda i: (i, offset // width))

    def full(a):
        return pl.BlockSpec(a.shape, lambda i: (0,) * a.ndim)

    weights = [p["cq_g"], p["wqa"], p["wqb"], p["ckv_g"], p["wk"], p["wv"],
               p["qga"], p["qgb"], p["kga"], p["kgb"], p["qaug"], p["kaug"], p["vaug"], p["swap"]]
    head_tile = pl.BlockSpec((tm, hw), lambda i: (i, 0))
    out = jax.ShapeDtypeStruct((n, hw), BF16)
    return pl.pallas_call(
        _mla_prep_kernel,
        out_shape=(out, out, out),
        grid=(n // tm,),
        in_specs=[col(Q_LORA, COL_CQ), col(KV_LORA, COL_CKV), col(LANES, COL_KRA),
                  pl.BlockSpec((tm, LANES), lambda i: (i, 0)), pl.BlockSpec((tm, LANES), lambda i: (i, 0))]
                 + [full(w) for w in weights],
        out_specs=(head_tile, head_tile, head_tile),
        compiler_params=_cparams("parallel"),
        name="mla_prep",
    )(proj, proj, proj, cm, sm, *weights)


def _flash_kernel(qi_ref, kj_ref, q_ref, k_ref, v_ref, o_ref, acc_scr, *rest, tq, online_max):
    i = qi_ref[pl.program_id(1)]
    j = kj_ref[pl.program_id(1)]

    @pl.when(j == 0)
    def _():
        acc_scr[...] = jnp.zeros(acc_scr.shape, F32)
        if online_max:
            rest[0][...] = jnp.full(rest[0].shape, -jnp.inf, F32)

    def block(q0, nq, nk, masked):
        rows = slice(q0, q0 + nq)
        if masked:
            row = lax.broadcasted_iota(jnp.int32, (nq, nk), 0) + q0
            col = lax.broadcasted_iota(jnp.int32, (nq, nk), 1)
            keep = col <= row
        for h in range(N_GROUPS):
            sl = slice(h * LANES, (h + 1) * LANES)
            s = lax.dot_general(q_ref[rows, sl], k_ref[0:nk, sl], (((1,), (1,)), ((), ())),
                                preferred_element_type=F32)
            if masked:
                s = jnp.where(keep, s, -jnp.inf)
            if online_max:
                m_scr = rest[0]
                m_prev = m_scr[h, rows]
                m_new = jnp.maximum(m_prev, jnp.max(s, axis=-1, keepdims=True))
                p = jnp.exp2(s - m_new).astype(BF16)
                acc_scr[h, rows] = jnp.exp2(m_prev - m_new) * acc_scr[h, rows] + jnp.dot(
                    p, v_ref[0:nk, sl], preferred_element_type=F32)
                m_scr[h, rows] = m_new
            else:
                acc_scr[h, rows] += jnp.dot(jnp.exp2(s).astype(BF16), v_ref[0:nk, sl],
                                            preferred_element_type=F32)

    @pl.when(j < i)
    def _():
        block(0, tq, tq, False)

    @pl.when(j == i)
    def _():
        block(0, tq // 2, tq // 2, True)
        block(tq // 2, tq // 2, tq, True)
        lane = lax.broadcasted_iota(jnp.int32, (tq, LANES), 1)
        for pr in range(N_GROUPS // 2):
            lo = acc_scr[2 * pr]
            hi = acc_scr[2 * pr + 1]
            lo = lo / lo[:, V_HEAD:V_HEAD + 1]
            hi = hi / hi[:, V_HEAD:V_HEAD + 1]
            both = jnp.where(lane < V_HEAD, lo, pltpu.roll(hi, V_HEAD, axis=1))
            o_ref[:, pr * LANES:(pr + 1) * LANES] = both.astype(BF16)


def _flash(q, k, v, batch, seq, online_max):
    n = q.shape[0]
    tq = TQ_ATT
    nq = seq // tq
    hw = N_GROUPS * LANES
    scratch = [pltpu.VMEM((N_GROUPS, tq, LANES), F32)]
    if online_max:
        scratch.append(pltpu.VMEM((N_GROUPS, tq, 1), F32))
    pairs = [(i, j) for i in range(nq) for j in range(i + 1)]
    qi = jnp.asarray([p[0] for p in pairs], jnp.int32)
    kj = jnp.asarray([p[1] for p in pairs], jnp.int32)
    q_tile = lambda b, s, qi, kj: (b * nq + qi[s], 0)
    k_tile = lambda b, s, qi, kj: (b * nq + kj[s], 0)
    return pl.pallas_call(
        functools.partial(_flash_kernel, tq=tq, online_max=online_max),
        out_shape=jax.ShapeDtypeStruct((n, MIX_W), BF16),
        grid_spec=pltpu.PrefetchScalarGridSpec(
            num_scalar_prefetch=2,
            grid=(batch, len(pairs)),
            in_specs=[pl.BlockSpec((tq, hw), q_tile), pl.BlockSpec((tq, hw), k_tile),
                      pl.BlockSpec((tq, hw), k_tile)],
            out_specs=pl.BlockSpec((tq, MIX_W), q_tile),
            scratch_shapes=scratch,
        ),
        compiler_params=_cparams("parallel", "arbitrary"),
        name="mla_flash_online" if online_max else "mla_flash",
    )(qi, kj, q, k, v)


def _gelu_tanh(x):
    return jax.nn.gelu(x, approximate=True)


def _mix_kernel(gates_ref, a_ref, r_ref, su_ref, ymla_ref, x_ref, cos_ref, sin_ref, mod_ref,
                convw_ref, gvg_ref, wscat_ref, bsmat_ref, retg_ref, dec_ref, kdec_ref, qdec_ref,
                cdec_ref, bd_ref, gmat_ref, mk_ref, mv_ref, wb_ref, wo_ref,
                o_ref, carry_scr, state_scr, ysg_scr, yret_scr, *, tm, tpb):
    i = pl.program_id(0)

    @pl.when(i % tpb == 0)
    def _():
        carry_scr[...] = jnp.zeros(carry_scr.shape, F32)
        state_scr[...] = jnp.zeros(state_scr.shape, F32)

    w = MIX_W
    a_b = a_ref[:, 0:w].astype(F32)
    u = a_ref[:, w:2 * w].astype(F32) * a_ref[:, 2 * w:3 * w].astype(F32)
    rowi = lax.broadcasted_iota(jnp.int32, (tm, w), 0)
    prev1 = carry_scr[0:1, :]
    prev2 = carry_scr[1:2, :]
    u1 = jnp.where(rowi == 0, prev1, pltpu.roll(u, 1, axis=0))
    u2 = jnp.where(rowi == 0, prev2, jnp.where(rowi == 1, prev1, pltpu.roll(u, 2, axis=0)))
    carry_scr[0:1, :] = u[tm - 1:tm, :]
    carry_scr[1:2, :] = u[tm - 2:tm - 1, :]
    y_conv = a_b * (convw_ref[0:1, :] * u2 + convw_ref[1:2, :] * u1 + convw_ref[2:3, :] * u)

    gmat = gmat_ref[...]
    s_u = _gelu_tanh(su_ref[:, 0:w].astype(F32))
    s_v = _gelu_tanh(su_ref[:, w:2 * w].astype(F32))
    ms = _group_mean(s_v * s_v, gmat)
    vn = (s_v * lax.rsqrt(ms + EPS) * gvg_ref[...]).astype(BF16)

    cosr = cos_ref[...]
    sinr = sin_ref[...]

    def rot(t):
        t1, t2 = t[:, 0:LANES], t[:, LANES:2 * LANES]
        return jnp.concatenate([t1 * cosr - t2 * sinr, t2 * cosr + t1 * sinr], axis=-1)

    rq = rot(r_ref[:, 0:w].astype(F32))
    rk = rot(r_ref[:, w:2 * w].astype(F32)) * (HEAD_DIM ** -0.5)

    for c in range(tm // CHUNK):
        rows = slice(c * CHUNK, (c + 1) * CHUNK)
        vc = vn[rows, :]
        vbd = jnp.concatenate([vc * mv_ref[g:g + 1, :].astype(BF16) for g in range(N_GROUPS)], axis=0)
        mixed = jnp.dot(wscat_ref[...], vbd, preferred_element_type=F32) + bsmat_ref[...]
        ysg_scr[rows, :] = s_u[rows, :] * mixed

        qc = rq[rows, :]
        kc = rk[rows, :]
        kcb = kc.astype(BF16)
        vcb = r_ref[rows, 2 * w:3 * w]
        qstack = jnp.concatenate([(qc * mk_ref[h:h + 1, :]).astype(BF16) for h in range(N_GROUPS)], axis=0)
        sc = lax.dot_general(qstack, kcb, (((1,), (1,)), ((), ())), preferred_element_type=F32)
        sc = (sc * dec_ref[...]).astype(BF16)
        scat = jnp.concatenate([sc[h * CHUNK:(h + 1) * CHUNK, :] for h in range(N_GROUPS)], axis=1)
        vstack = jnp.concatenate([vcb * mv_ref[h:h + 1, :].astype(BF16) for h in range(N_GROUPS)], axis=0)
        o_c = jnp.dot(scat, vstack, preferred_element_type=F32)
        state = state_scr[...]
        o_c = o_c + jnp.dot((qc * qdec_ref[...]).astype(BF16), state.astype(BF16),
                            preferred_element_type=F32)
        kd_t = jnp.transpose(kc * kdec_ref[...]).astype(BF16)
        kv = jnp.dot(kd_t, vcb, preferred_element_type=F32)
        state_scr[...] = state * cdec_ref[...] + kv * bd_ref[...]
        yret_scr[rows, :] = o_c

    o_all = yret_scr[...]
    xc = o_all - _group_mean(o_all, gmat)
    var = _group_mean(xc * xc, gmat)
    r_g = r_ref[:, 3 * w:4 * w].astype(F32)
    y_ret = (r_g * _sigmoid(r_g)) * (xc * lax.rsqrt(var + EPS) * retg_ref[...])

    d = x_ref.shape[1]
    ys = (y_conv, ymla_ref[...], ysg_scr[...], y_ret)
    merged = None
    for n in range(N_BRANCH):
        gate = _sigmoid(gates_ref[:, n * d:(n + 1) * d])
        term = gate * jnp.dot(ys[n].astype(BF16), wb_ref[n], preferred_element_type=F32).astype(BF16)
        merged = term if merged is None else merged + term
    out = jnp.dot(merged, wo_ref[...], preferred_element_type=F32)
    o_ref[...] = x_ref[...] + mod_ref[2:3, :] * out


def _mixers(proj, ymla, x, cosr, sinr, mod, p, seq):
    n, d = x.shape
    tm = TM_MIX
    tpb = seq // tm

    def col(width, offset):
        return pl.BlockSpec((tm, width), lambda i: (i, offset // width))

    def full(a):
        return pl.BlockSpec(a.shape, lambda i: (0,) * a.ndim)

    consts = [p["conv_w"], p["gv_g"], p["ws_cat"], p["bs_mat"], p["ret_g"], p["dec"], p["kdec"],
              p["qdec"], p["cdec"], p["bd"], p["gmat"], p["mk"], p["mv"], p["w_branch"], p["w_o"]]
    return pl.pallas_call(
        functools.partial(_mix_kernel, tm=tm, tpb=tpb),
        out_shape=jax.ShapeDtypeStruct((n, d), F32),
        grid=(n // tm,),
        in_specs=[col(N_BRANCH * d, COL_GATES), col(4 * MIX_W, COL_A), col(4 * MIX_W, COL_R),
                  col(2 * MIX_W, COL_SU),
                  pl.BlockSpec((tm, MIX_W), lambda i: (i, 0)),
                  pl.BlockSpec((tm, d), lambda i: (i, 0)),
                  pl.BlockSpec((tm, LANES), lambda i: (i, 0)),
                  pl.BlockSpec((tm, LANES), lambda i: (i, 0)),
                  pl.BlockSpec((None, 6, d), lambda i: (i // tpb, 0, 0))]
                 + [full(c) for c in consts],
        out_specs=pl.BlockSpec((tm, d), lambda i: (i, 0)),
        scratch_shapes=[pltpu.VMEM((8, MIX_W), F32), pltpu.VMEM((MIX_W, MIX_W), F32),
                        pltpu.VMEM((tm, MIX_W), F32), pltpu.VMEM((tm, MIX_W), F32)],
        compiler_params=_cparams("arbitrary"),
        name="mixers_merge",
    )(proj, proj, proj, proj, ymla, x, cosr, sinr, mod, *consts)


def _ffn_kernel(x_ref, mod_ref, g_ref, w1_ref, w3_ref, w2_ref, o_ref):
    x = x_ref[...]
    h = _norm_mod(x, g_ref[...], mod_ref[3:4, :], mod_ref[4:5, :]).astype(BF16)
    a = jnp.dot(h, w1_ref[...], preferred_element_type=F32)
    b = jnp.dot(h, w3_ref[...], preferred_element_type=F32)
    hid = ((a * _sigmoid(a)) * b).astype(BF16)
    o_ref[...] = x + mod_ref[5:6, :] * jnp.dot(hid, w2_ref[...], preferred_element_type=F32)


def _dense_ffn(x, mod, g, w1, w3, w2, seq):
    n, d = x.shape
    dff = w1.shape[1]
    tm = TM_FFN
    tpb = seq // tm
    return pl.pallas_call(
        _ffn_kernel,
        out_shape=jax.ShapeDtypeStruct((n, d), F32),
        grid=(n // tm,),
        in_specs=[
            pl.BlockSpec((tm, d), lambda i: (i, 0)),
            pl.BlockSpec((None, 6, d), lambda i: (i // tpb, 0, 0)),
            pl.BlockSpec((1, d), lambda i: (0, 0)),
            _resident((d, dff)), _resident((d, dff)), _resident((dff, d)),
        ],
        out_specs=pl.BlockSpec((tm, d), lambda i: (i, 0)),
        compiler_params=_cparams("parallel"),
        name="dense_swiglu",
    )(x, mod, g, w1, w3, w2)


def _router_kernel(x_ref, mod_ref, g_ref, rw_ref, rb_ref, hs_ref, ei_ref, pw_ref, meta_ref, tot_ref,
                   carry_scr, *, tm, srows):
    i = pl.program_id(0)

    @pl.when(i == 0)
    def _():
        carry_scr[...] = jnp.zeros(carry_scr.shape, F32)

    h = _norm_mod(x_ref[...], g_ref[...], mod_ref[3:4, :], mod_ref[4:5, :])

    h_hi = h.astype(BF16)
    h_lo = (h - h_hi.astype(F32)).astype(BF16)
    hw = jnp.dot(h_hi, rw_ref[...], preferred_element_type=F32)
    logits = (hw[:, :LANES] + hw[:, LANES:] + jnp.dot(h_lo, rw_ref[:, :LANES], preferred_element_type=F32)
              + rb_ref[...])
    mx = jnp.max(logits, axis=-1, keepdims=True)
    ex = jnp.exp(logits - mx)
    probs = ex / jnp.sum(ex, axis=-1, keepdims=True)
    lane = lax.broadcasted_iota(jnp.int32, (tm, LANES), 1)
    valid = lane < N_EXPERTS
    probs = jnp.where(valid, probs, -1.0)
    m1 = jnp.max(probs, axis=-1, keepdims=True)
    i1 = jnp.min(jnp.where(probs == m1, lane, LANES), axis=-1, keepdims=True)
    rest = jnp.where(lane == i1, -1.0, probs)
    m2 = jnp.max(rest, axis=-1, keepdims=True)
    i2 = jnp.min(jnp.where(rest == m2, lane, LANES), axis=-1, keepdims=True)
    den = m1 + m2
    pw_ref[...] = jnp.where(lane == 0, m1 / den, jnp.where(lane == 1, m2 / den, 0.0))

    sel1 = lane == i1
    sel2 = lane == i2
    onehot = jnp.where(sel1, 1.0, 0.0) + jnp.where(sel2, 1.0, 0.0)
    r_i = lax.broadcasted_iota(jnp.int32, (tm, tm), 0)
    c_i = lax.broadcasted_iota(jnp.int32, (tm, tm), 1)
    tri = jnp.where(c_i < r_i, 1.0, 0.0).astype(BF16)
    before = jnp.dot(tri, onehot.astype(BF16), preferred_element_type=F32)
    cnt = jnp.sum(onehot, axis=0, keepdims=True)
    cnt_al = jnp.floor((cnt + (ROW_ALIGN - 1)) * (1.0 / ROW_ALIGN)) * ROW_ALIGN
    e_r = lax.broadcasted_iota(jnp.int32, (LANES, LANES), 0)
    e_c = lax.broadcasted_iota(jnp.int32, (LANES, LANES), 1)
    upper = jnp.where(e_r < e_c, 1.0, 0.0)
    loff = jnp.dot(jnp.broadcast_to(cnt_al, (8, LANES)), upper, precision=HIGHEST,
                   preferred_element_type=F32)[0:1, :]
    slot = loff + before
    slot1 = jnp.sum(jnp.where(sel1, slot, 0.0), axis=-1, keepdims=True).astype(jnp.int32)
    slot2 = jnp.sum(jnp.where(sel2, slot, 0.0), axis=-1, keepdims=True).astype(jnp.int32)
    ei = jnp.where(lane == 0, i1, jnp.where(lane == 1, i2, 0))
    ei_ref[...] = jnp.where(lane == 2, slot1, jnp.where(lane == 3, slot2, ei))

    r_idx = lax.broadcasted_iota(jnp.int32, (tm, srows), 1)
    place = jnp.where(r_idx == slot1, 1.0, jnp.where(r_idx == slot2, 1.0, 0.0)).astype(BF16)
    hs = lax.dot_general(place, h.astype(BF16), (((0,), (0,)), ((), ())), preferred_element_type=F32)
    half = hs.shape[1] // 2
    hs_ref[...] = _pack_bf16_pair(hs[:, :half], hs[:, half:])

    carry = carry_scr[0:1, :]
    mrow = lax.broadcasted_iota(jnp.int32, (8, LANES), 0)
    meta = jnp.where(mrow == 0, cnt_al, jnp.where(mrow == 1, carry, jnp.where(mrow == 2, loff, 0.0)))
    meta_ref[...] = meta.astype(jnp.int32)
    carry_scr[0:1, :] = carry + cnt_al
    tot_ref[...] = jnp.broadcast_to(carry + cnt_al, tot_ref.shape).astype(jnp.int32)


def _router(x, mod, g, rw_pad, rb_pad, seq):
    n, d = x.shape
    tm = TM_ROUTE
    tpb = seq // tm
    nt = n // tm
    return pl.pallas_call(
        functools.partial(_router_kernel, tm=tm, srows=SORT_ROWS),
        out_shape=(jax.ShapeDtypeStruct((nt * SORT_ROWS, d // 2), jnp.uint32),
                   jax.ShapeDtypeStruct((n, LANES), jnp.int32),
                   jax.ShapeDtypeStruct((n, LANES), F32),
                   jax.ShapeDtypeStruct((nt, 8, LANES), jnp.int32),
                   jax.ShapeDtypeStruct((8, LANES), jnp.int32)),
        grid=(nt,),
        in_specs=[
            pl.BlockSpec((tm, d), lambda i: (i, 0)),
            pl.BlockSpec((None, 6, d), lambda i: (i // tpb, 0, 0)),
            pl.BlockSpec((1, d), lambda i: (0, 0)),
            pl.BlockSpec((d, 2 * LANES), lambda i: (0, 0)),
            pl.BlockSpec((1, LANES), lambda i: (0, 0)),
        ],
        out_specs=(pl.BlockSpec((SORT_ROWS, d // 2), lambda i: (i, 0)),
                   pl.BlockSpec((tm, LANES), lambda i: (i, 0)),
                   pl.BlockSpec((tm, LANES), lambda i: (i, 0)),
                   pl.BlockSpec((None, 8, LANES), lambda i: (i, 0, 0)),
                   pl.BlockSpec((8, LANES), lambda i: (0, 0))),
        scratch_shapes=[pltpu.VMEM((8, LANES), F32)],
        compiler_params=_cparams("arbitrary"),
        name="router_top2",
    )(x, mod, g, rw_pad, rb_pad)


def _segment_copy(src_hbm, dst_hbm, src_row, dst_row, n_rows, sem):
    src_row = pl.multiple_of(src_row, ROW_ALIGN)
    dst_row = pl.multiple_of(dst_row, ROW_ALIGN)
    n_rows = pl.multiple_of(n_rows, ROW_ALIGN)
    return pltpu.make_async_copy(src_hbm.at[pl.ds(src_row, n_rows)], dst_hbm.at[pl.ds(dst_row, n_rows)], sem)


def _dispatch_kernel(loff_ref, dst_ref, cnt_ref, rows_ref, gap_ref, hs_ref, xs_ref, buf, zbuf, sem_in, sem_out,
                     sem_zero, *, n_tiles, srows, n_out_tiles):
    zbuf[...] = jnp.zeros(zbuf.shape, zbuf.dtype)
    tg = zbuf.shape[0]
    first_unused = gap_ref[2 * N_EXPERTS]

    def zero_gap(e):
        return _segment_copy(zbuf, xs_ref, 0, gap_ref[2 * e], gap_ref[2 * e + 1], sem_zero)

    def zero_tile(t):
        return pltpu.make_async_copy(zbuf, xs_ref.at[pl.ds(pl.multiple_of(t * tg, ROW_ALIGN), tg)], sem_zero)

    def for_each_zero_copy(action):
        for e in range(N_EXPERTS):
            @pl.when(gap_ref[2 * e + 1] > 0)
            def _():
                action(zero_gap(e))

        def tail(t, carry):
            action(zero_tile(t))
            return carry

        lax.fori_loop(first_unused, n_out_tiles, tail, 0)

    for_each_zero_copy(lambda copy: copy.start())

    def fetch(t):
        slot = t % DISPATCH_SLOTS
        return pltpu.make_async_copy(hs_ref.at[pl.ds(pl.multiple_of(t * srows, ROW_ALIGN), srows)],
                                     buf.at[slot], sem_in.at[slot])

    def drain(t):
        slot = t % DISPATCH_SLOTS
        n_rows = rows_ref[t]

        @pl.when(n_rows > 0)
        def _():
            _segment_copy(buf.at[slot], xs_ref, 0, 0, n_rows, sem_out.at[slot]).wait()

    for t in range(DISPATCH_AHEAD):
        fetch(t).start()

    def body(t, carry):
        slot = t % DISPATCH_SLOTS
        fetch(t).wait()
        for e in range(N_EXPERTS):
            s = t * N_EXPERTS + e
            n_rows = cnt_ref[s]

            @pl.when(n_rows > 0)
            def _():
                _segment_copy(buf.at[slot], xs_ref, loff_ref[s], dst_ref[s], n_rows, sem_out.at[slot]).start()

        @pl.when(t + DISPATCH_AHEAD < n_tiles)
        def _():
            @pl.when(t + DISPATCH_AHEAD >= DISPATCH_SLOTS)
            def _():
                drain(t + DISPATCH_AHEAD - DISPATCH_SLOTS)

            fetch(t + DISPATCH_AHEAD).start()

        return carry

    lax.fori_loop(0, n_tiles, body, 0)
    for t in range(max(n_tiles - DISPATCH_SLOTS, 0), n_tiles):
        drain(t)
    for_each_zero_copy(lambda copy: copy.wait())


def _dispatch(seg_loff, seg_dst, seg_cnt, tile_rows, gaps, hs, n_out_tiles):
    n_tiles = tile_rows.shape[0]
    srows = hs.shape[0] // n_tiles
    assert n_tiles >= DISPATCH_SLOTS
    return pl.pallas_call(
        functools.partial(_dispatch_kernel, n_tiles=n_tiles, srows=srows, n_out_tiles=n_out_tiles),
        out_shape=jax.ShapeDtypeStruct((n_out_tiles * TG_MOE, hs.shape[1]), hs.dtype),
        grid_spec=pltpu.PrefetchScalarGridSpec(
            num_scalar_prefetch=5,
            grid=(1,),
            in_specs=[pl.BlockSpec(memory_space=pl.ANY)],
            out_specs=pl.BlockSpec(memory_space=pl.ANY),
            scratch_shapes=[pltpu.VMEM((DISPATCH_SLOTS, srows, hs.shape[1]), hs.dtype),
                            pltpu.VMEM((TG_MOE, hs.shape[1]), hs.dtype),
                            pltpu.SemaphoreType.DMA((DISPATCH_SLOTS,)),
                            pltpu.SemaphoreType.DMA((DISPATCH_SLOTS,)),
                            pltpu.SemaphoreType.DMA],
        ),
        compiler_params=_cparams("arbitrary"),
        name="moe_dispatch",
    )(seg_loff, seg_dst, seg_cnt, tile_rows, gaps, hs)


def _expert_kernel(te_ref, used_ref, xs_ref, w1_hbm, w3_hbm, w2_hbm, y_ref, wb1, wb3, wb2, stage, sem):
    t = pl.program_id(0)
    e = te_ref[t]
    first_of_expert = jnp.logical_or(t == 0, e != te_ref[jnp.maximum(t - 1, 0)])

    @pl.when(jnp.logical_and(used_ref[t] == 1, first_of_expert))
    def _():
        n_slots = stage.shape[0]
        windows = [(src, dst, r, c) for src, dst in ((w1_hbm, wb1), (w3_hbm, wb3), (w2_hbm, wb2))
                   for r in range(dst.shape[0] // W_CHUNK) for c in range(dst.shape[1] // W_CHUNK)]

        def staged_copy(k):
            src, _, r, c = windows[k]
            return pltpu.make_async_copy(src.at[e, pl.ds(r * W_CHUNK, W_CHUNK), pl.ds(c * W_CHUNK, W_CHUNK)],
                                         stage.at[k % n_slots], sem.at[k % n_slots])

        for k in range(n_slots - 1):
            staged_copy(k).start()
        for k, (_, dst, r, c) in enumerate(windows):
            staged_copy(k).wait()
            dst[r * W_CHUNK:(r + 1) * W_CHUNK, c * W_CHUNK:(c + 1) * W_CHUNK] = stage[k % n_slots].astype(BF16)
            if k + n_slots - 1 < len(windows):
                staged_copy(k + n_slots - 1).start()

    @pl.when(used_ref[t] == 1)
    def _():
        lo, hi = _unpack_bf16_pair(xs_ref[...])
        h = jnp.concatenate([lo.astype(BF16), hi.astype(BF16)], axis=1)
        a = jnp.dot(h, wb1[...], preferred_element_type=F32)
        b = jnp.dot(h, wb3[...], preferred_element_type=F32)
        hid = ((a * _sigmoid(a)) * b).astype(BF16)
        acc = jnp.dot(hid, wb2[...], preferred_element_type=F32)
        half = acc.shape[1] // 2
        y_ref[...] = _pack_bf16_pair(acc[:, :half], acc[:, half:])

    @pl.when(used_ref[t] == 0)
    def _():
        y_ref[...] = jnp.zeros(y_ref.shape, y_ref.dtype)


def _expert_ffn(tile_e, tile_used, xs, w1, w3, w2):
    rows, half = xs.shape
    d = 2 * half
    dff = w1.shape[2]
    tg = TG_MOE
    assert dff % W_CHUNK == 0
    hbm = pl.BlockSpec(memory_space=pl.ANY)
    return pl.pallas_call(
        _expert_kernel,
        out_shape=jax.ShapeDtypeStruct((rows, half), jnp.uint32),
        grid_spec=pltpu.PrefetchScalarGridSpec(
            num_scalar_prefetch=2,
            grid=(rows // tg,),
            in_specs=[pl.BlockSpec((tg, half), lambda t, te, us: (t, 0)), hbm, hbm, hbm],
            out_specs=pl.BlockSpec((tg, half), lambda t, te, us: (t, 0)),
            scratch_shapes=[pltpu.VMEM((d, dff), BF16), pltpu.VMEM((d, dff), BF16), pltpu.VMEM((dff, d), BF16),
                            pltpu.VMEM((W_SLOTS, W_CHUNK, W_CHUNK), F32), pltpu.SemaphoreType.DMA((W_SLOTS,))],
        ),
        compiler_params=_cparams("arbitrary"),
        name="expert_swiglu",
    )(tile_e, tile_used, xs, w1, w3, w2)


def _combine_kernel(src_ref, loff_ref, cnt_ref, rows_ref, x_ref, ei_ref, pw_ref, mod_ref, y_ref, o_ref,
                    ybuf, sem, *, tm, srows):
    i = pl.program_id(0)
    slot = i % 2

    def fetch(tile, into):
        ybuf[into] = jnp.zeros(ybuf.shape[1:], ybuf.dtype)
        for e in range(N_EXPERTS):
            s = tile * N_EXPERTS + e
            n_rows = cnt_ref[s]

            @pl.when(n_rows > 0)
            def _():
                _segment_copy(y_ref, ybuf.at[into], src_ref[s], loff_ref[s], n_rows, sem.at[into]).start()

    @pl.when(i == 0)
    def _():
        fetch(i, slot)

    @pl.when(i + 1 < pl.num_programs(0))
    def _():
        fetch(i + 1, 1 - slot)

    @pl.when(rows_ref[i] > 0)
    def _():
        _segment_copy(y_ref, ybuf.at[slot], 0, 0, rows_ref[i], sem.at[slot]).wait()

    lo, hi = _unpack_bf16_pair(ybuf[slot])
    ys = jnp.concatenate([lo.astype(BF16), hi.astype(BF16)], axis=1)
    r_idx = lax.broadcasted_iota(jnp.int32, (tm, srows), 1)
    mix = jnp.zeros(x_ref.shape, F32)
    for k in range(TOP_K):
        pick = jnp.where(r_idx == ei_ref[:, TOP_K + k:TOP_K + k + 1], 1.0, 0.0).astype(BF16)
        mix = mix + pw_ref[:, k:k + 1] * jnp.dot(pick, ys, preferred_element_type=F32)
    o_ref[...] = x_ref[...] + mod_ref[5:6, :] * mix


def _combine(seg_src, seg_loff, seg_cnt, tile_rows, x, ei, pw, mod, y, seq):
    n, d = x.shape
    tm = TM_ROUTE
    tpb = seq // tm
    tok = lambda width: pl.BlockSpec((tm, width), lambda i, *_: (i, 0))
    return pl.pallas_call(
        functools.partial(_combine_kernel, tm=tm, srows=SORT_ROWS),
        out_shape=jax.ShapeDtypeStruct((n, d), F32),
        grid_spec=pltpu.PrefetchScalarGridSpec(
            num_scalar_prefetch=4,
            grid=(n // tm,),
            in_specs=[tok(d), tok(LANES), tok(LANES),
                      pl.BlockSpec((None, 6, d), lambda i, *_: (i // tpb, 0, 0)),
                      pl.BlockSpec(memory_space=pl.ANY)],
            out_specs=tok(d),
            scratch_shapes=[pltpu.VMEM((2, SORT_ROWS, d // 2), jnp.uint32), pltpu.SemaphoreType.DMA((2,))],
        ),
        compiler_params=_cparams("arbitrary"),
        name="moe_combine",
    )(seg_src, seg_loff, seg_cnt, tile_rows, x, ei, pw, mod, y)


def _pack_w_in(w_in):
    d = w_in.shape[0]
    w = MIX_W
    o_ckv = 3 * w + Q_LORA
    o_kr = o_ckv + KV_LORA
    o_su = o_kr + QK_ROPE
    o_rq = o_su + 2 * w
    o_gate = o_rq + 4 * w
    half = HEAD_DIM // 2
    perm = np.array([h * HEAD_DIM + part * half + i
                     for part in range(2) for h in range(N_GROUPS) for i in range(half)])
    kr = w_in[:, o_kr:o_kr + QK_ROPE]
    z = lambda k: jnp.zeros((d, k), w_in.dtype)
    cols = [
        w_in[:, o_gate:o_gate + N_BRANCH * d],
        w_in[:, 0:3 * w + Q_LORA],
        w_in[:, o_rq:o_rq + w][:, perm], w_in[:, o_rq + w:o_rq + 2 * w][:, perm],
        w_in[:, o_rq + 2 * w:o_rq + 4 * w],
        w_in[:, o_su:o_su + 2 * w],
        w_in[:, o_ckv:o_ckv + KV_LORA],
        z(QK_NOPE), kr, z(LANES - QK_HEAD),
    ]
    return jnp.concatenate(cols, axis=1).astype(BF16)


def _swap_rope_halves(a):
    hr = QK_ROPE // 2
    return jnp.concatenate([a[..., :QK_NOPE], a[..., QK_NOPE + hr:QK_HEAD], a[..., QK_NOPE:QK_NOPE + hr],
                            a[..., QK_HEAD:]], axis=-1)


def _mla_params(cq_g, w_uq, ckv_g, w_ukv, qn_g, kn_g):
    pad = LANES - QK_HEAD
    wq = w_uq.reshape(Q_LORA, N_GROUPS, QK_HEAD)
    wq = jnp.pad(wq, ((0, 0), (0, 0), (0, pad)))
    wkv = w_ukv.reshape(KV_LORA, N_GROUPS, QK_NOPE + V_HEAD)
    wk = jnp.pad(wkv[:, :, :QK_NOPE], ((0, 0), (0, 0), (0, LANES - QK_NOPE)))
    wv = jnp.pad(wkv[:, :, QK_NOPE:], ((0, 0), (0, 0), (0, LANES - V_HEAD)))
    qg = jnp.pad(qn_g, (0, pad))[None, :]
    kg = jnp.pad(kn_g, (0, pad))[None, :]
    bound = (QK_HEAD ** 0.5 * LOG2_E) * jnp.max(jnp.abs(qn_g)) * jnp.max(jnp.abs(kn_g))
    static_shift = bound <= MAX_STATIC_SHIFT
    lane = jnp.arange(LANES)
    qaug = (lane == QK_HEAD).astype(F32)[None, :]
    kaug = qaug * jnp.where(static_shift, -bound, 0.0)
    vaug = jnp.tile((lane == V_HEAD).astype(F32), N_GROUPS)[None, :]
    params = {
        "cq_g": cq_g[None, :], "ckv_g": ckv_g[None, :],
        "wqa": wq.reshape(Q_LORA, -1).astype(BF16),
        "wqb": _swap_rope_halves(wq).reshape(Q_LORA, -1).astype(BF16),
        "wk": wk.reshape(KV_LORA, -1).astype(BF16),
        "wv": wv.reshape(KV_LORA, -1).astype(BF16),
        "qga": qg, "qgb": _swap_rope_halves(qg), "kga": kg, "kgb": _swap_rope_halves(kg),
        "qaug": qaug, "kaug": kaug, "vaug": vaug,
        "swap": (_swap_rope_halves(lane[None, :])[0][None, :] == lane[:, None]).astype(BF16),
    }
    return params, static_shift


def _mixer_consts():
    h = jnp.arange(N_GROUPS, dtype=F32)
    log_gamma = jnp.log1p(-(2.0 ** (-5.0 - h)))
    pos = jnp.arange(CHUNK, dtype=F32)
    rel = pos[:, None] - pos[None, :]
    dec = jnp.where(rel >= 0, jnp.exp(log_gamma[:, None, None] * jnp.maximum(rel, 0.0)), 0.0)
    lane = np.arange(MIX_W)
    head_k = (lane % LANES) // (HEAD_DIM // 2)
    head_v = lane // HEAD_DIM
    lg_k = log_gamma[head_k]
    return {
        "dec": dec.reshape(N_GROUPS * CHUNK, CHUNK),
        "kdec": jnp.exp(lg_k[None, :] * (CHUNK - 1.0 - pos)[:, None]),
        "qdec": jnp.exp(lg_k[None, :] * (pos + 1.0)[:, None]),
        "cdec": jnp.broadcast_to(jnp.exp(lg_k * CHUNK)[:, None], (MIX_W, MIX_W)),
        "bd": jnp.asarray((head_k[:, None] == head_v[None, :]).astype(np.float32)),
        "gmat": jnp.asarray((head_v[:, None] == head_v[None, :]).astype(np.float32) / HEAD_DIM).astype(BF16),
        "mk": jnp.asarray((head_k[None, :] == np.arange(N_GROUPS)[:, None]).astype(np.float32)),
        "mv": jnp.asarray((head_v[None, :] == np.arange(N_GROUPS)[:, None]).astype(np.float32)),
    }


def _mixer_params(conv_w, gv_g, w_s, b_s, ret_g, w_branch, w_o):
    p = dict(_mixer_consts())
    ws = jnp.tril(w_s)
    p.update({
        "conv_w": conv_w,
        "gv_g": gv_g.reshape(1, MIX_W),
        "ws_cat": jnp.transpose(ws, (1, 0, 2)).reshape(CHUNK, N_GROUPS * CHUNK).astype(BF16),
        "bs_mat": jnp.repeat(b_s.T, HEAD_DIM, axis=1),
        "ret_g": ret_g.reshape(1, MIX_W),
        "w_branch": w_branch.astype(BF16),
        "w_o": w_o.astype(BF16),
    })
    return p


def _moe_layout(meta, tot, n_tiles):
    totals = tot[0, :N_EXPERTS]
    padded = ((totals + TG_MOE - 1) // TG_MOE) * TG_MOE
    ends = jnp.cumsum(padded)
    starts = ends - padded
    seg_cnt = meta[:, 0, :N_EXPERTS]
    seg_loff = meta[:, 2, :N_EXPERTS]
    seg_grouped = starts[None, :] + meta[:, 1, :N_EXPERTS]
    tile_start = jnp.arange(n_tiles, dtype=jnp.int32) * TG_MOE
    tile_e = jnp.sum((tile_start[:, None] >= ends[None, :]).astype(jnp.int32), axis=1)
    used = (tile_start < ends[-1]).astype(jnp.int32)
    last_e = jnp.sum((ends[-1] - 1 >= ends).astype(jnp.int32))
    tile_e = jnp.minimum(jnp.where(used == 1, tile_e, last_e), N_EXPERTS - 1)
    flat = lambda a: a.reshape(-1).astype(jnp.int32)
    gaps = jnp.concatenate([flat(jnp.stack([starts + totals, padded - totals], axis=1)),
                            flat(ends[-1:] // TG_MOE)])
    return flat(seg_grouped), flat(seg_loff), flat(seg_cnt), gaps, tile_e, used


def kernel(x, c, positions, norm1_g, norm2_g, ada_w, ada_b, w_in, conv_w, cq_g, w_uq, ckv_g, w_ukv, qn_g, kn_g, gv_g, w_s, b_s, ret_g, w_branch, w_o, ffn_w1, ffn_w3, ffn_w2, router_w, router_b, moe_w1, moe_w3, moe_w2):
    batch, seq, d = x.shape
    depth = ada_w.shape[0]
    n = batch * seq
    assert seq % max(TM_PROJ, TM_PREP, TQ_ATT, TM_MIX, TM_FFN, TM_ROUTE) == 0
    assert d // 2 % LANES == 0

    c_t = jnp.pad(c, ((0, 8 - batch), (0, 0))).T
    ada = _ada(c_t, ada_w, ada_b, batch)[:, :batch].reshape(depth, batch, 6, d)
    cosr, sinr, cm, sm = _rope_tables(positions.astype(F32).reshape(n, 1))

    xt = x.reshape(n, d)
    for l in range(depth):
        mod = ada[l]
        proj = _inproj(xt, mod, norm1_g[l][None, :], _pack_w_in(w_in[l]), seq)
        mla_p, static_shift = _mla_params(cq_g[l], w_uq[l], ckv_g[l], w_ukv[l], qn_g[l], kn_g[l])
        q, k, v = _mla_prep(proj, cm, sm, mla_p)
        y_mla = lax.cond(static_shift,
                         functools.partial(_flash, batch=batch, seq=seq, online_max=False),
                         functools.partial(_flash, batch=batch, seq=seq, online_max=True), q, k, v)
        mp = _mixer_params(conv_w[l], gv_g[l], w_s[l], b_s[l], ret_g[l], w_branch[l], w_o[l])
        xt = _mixers(proj, y_mla, xt, cosr, sinr, mod, mp, seq)
        g2n = norm2_g[l][None, :]
        if l % 2 == 0:
            i = l // 2
            xt = _dense_ffn(xt, mod, g2n, ffn_w1[i].astype(BF16), ffn_w3[i].astype(BF16),
                            ffn_w2[i].astype(BF16), seq)
        else:
            i = l // 2
            rw = jnp.pad(router_w[i], ((0, 0), (0, LANES - N_EXPERTS)))
            rw_hi = rw.astype(BF16)
            rw_pad = jnp.concatenate([rw_hi, (rw - rw_hi.astype(F32)).astype(BF16)], axis=1)
            rb_pad = jnp.pad(router_b[i], (0, LANES - N_EXPERTS), constant_values=-1e30)[None, :]
            hs, ei, pw, meta, tot = _router(xt, mod, g2n, rw_pad, rb_pad, seq)
            max_rows = n * TOP_K + N_EXPERTS * (n // TM_ROUTE) * (ROW_ALIGN - 1)
            n_tiles = -(-max_rows // TG_MOE) + N_EXPERTS
            seg_grouped, seg_loff, seg_cnt, gaps, tile_e, used = _moe_layout(meta, tot, n_tiles)
            tile_rows = jnp.sum(seg_cnt.reshape(-1, N_EXPERTS), axis=1)
            xs = _dispatch(seg_loff, seg_grouped, seg_cnt, tile_rows, gaps, hs, n_tiles)
            y = _expert_ffn(tile_e, used, xs, moe_w1[i], moe_w3[i], moe_w2[i])
            xt = _combine(seg_grouped, seg_loff, seg_cnt, tile_rows, xt, ei, pw, mod, y, seq)
    return xt.reshape(batch, seq, d)
```

```python
import functools

import jax
import jax.numpy as jnp
import numpy as np
from jax import lax
from jax.experimental import pallas as pl
from jax.experimental.pallas import tpu as pltpu

F32 = jnp.float32
BF16 = jnp.bfloat16
HIGHEST = lax.Precision.HIGHEST

HEAD_DIM = 64
N_GROUPS = 4
MIX_W = N_GROUPS * HEAD_DIM
N_BRANCH = 4
CONV_W = 3
Q_LORA = 256
KV_LORA = 128
QK_NOPE = 64
QK_ROPE = 32
QK_HEAD = QK_NOPE + QK_ROPE
V_HEAD = 64
CHUNK = 128
N_EXPERTS = 8
TOP_K = 2
ROPE_THETA = 10000.0
EPS = 1e-6
LOG2_E = 1.4426950408889634
MAX_STATIC_SHIFT = 50.0

LANES = 128
VMEM_LIMIT_BYTES = 56 * 1024 * 1024

COL_GATES = 0
COL_A = 4096
COL_CQ = COL_A + 3 * MIX_W
COL_R = 5120
COL_SU = 6144
COL_CKV = 6656
COL_KRA = 6784
N_IN = 6912

TM_PROJ = 512
TN_PROJ = 768
TM_PREP = 1024
TQ_ATT = 1024
KV_TILES = 4
TM_MIX = 512
TM_FFN = 512
TM_ROUTE = 512
ROW_ALIGN = 8
SORT_ROWS = TOP_K * TM_ROUTE + N_EXPERTS * ROW_ALIGN
TG_MOE = 512
DISPATCH_SLOTS = 4
DISPATCH_AHEAD = 2
W_CHUNK = 512
W_SLOTS = 6


def _cparams(*sem):
    return pltpu.CompilerParams(dimension_semantics=sem, vmem_limit_bytes=VMEM_LIMIT_BYTES)


def _sigmoid(x):
    return jnp.tanh(x * 0.5) * 0.5 + 0.5


def _group_mean(x, gmat_bf16):
    hi = x.astype(BF16)
    lo = (x - hi.astype(F32)).astype(BF16)
    return (jnp.dot(hi, gmat_bf16, preferred_element_type=F32)
            + jnp.dot(lo, gmat_bf16, preferred_element_type=F32))


def _pack_bf16_pair(lo, hi):
    lo_bits = lax.bitcast_convert_type(lo.astype(BF16).astype(F32), jnp.uint32)
    hi_bits = lax.bitcast_convert_type(hi.astype(BF16).astype(F32), jnp.uint32)
    return (lo_bits >> 16) | (hi_bits & jnp.uint32(0xFFFF0000))


def _unpack_bf16_pair(p):
    lo = lax.bitcast_convert_type(p << 16, F32)
    hi = lax.bitcast_convert_type(p & jnp.uint32(0xFFFF0000), F32)
    return lo, hi


def _norm_mod(x, g, shift, scale):
    y = x * lax.rsqrt(jnp.mean(x * x, axis=-1, keepdims=True) + EPS)
    return (y * g) * (1.0 + scale) + shift


def _ada_kernel(ct_ref, w_ref, b_ref, o_ref, *, batch):
    ct = ct_ref[...]
    cond = ct * _sigmoid(ct)
    w = w_ref[...]
    o_ref[...] = jnp.zeros(o_ref.shape, F32)
    for b in range(batch):
        o_ref[b:b + 1, :] = jnp.sum(w * cond[:, b:b + 1], axis=0, keepdims=True) + b_ref[...]


def _ada(c_t, ada_w, ada_b, batch):
    n_layer, d, d6 = ada_w.shape
    rows = c_t.shape[1]
    tn = 1024
    return pl.pallas_call(
        functools.partial(_ada_kernel, batch=batch),
        out_shape=jax.ShapeDtypeStruct((n_layer, rows, d6), F32),
        grid=(n_layer, d6 // tn),
        in_specs=[
            pl.BlockSpec((d, rows), lambda l, j: (0, 0)),
            pl.BlockSpec((None, d, tn), lambda l, j: (l, 0, j)),
            pl.BlockSpec((None, 1, tn), lambda l, j: (l, 0, j)),
        ],
        out_specs=pl.BlockSpec((None, rows, tn), lambda l, j: (l, 0, j)),
        compiler_params=_cparams("parallel", "parallel"),
        name="ada_mod",
    )(c_t, ada_w, ada_b.reshape(n_layer, 1, d6))


def _rope_kernel(pos_ref, inv_ref, cr_ref, sr_ref, cm_ref, sm_ref):
    half_r = HEAD_DIM // 2
    half_m = QK_ROPE // 2
    ang = pos_ref[...] * inv_ref[...]
    c = jnp.cos(ang)
    s = jnp.sin(ang)
    lane = lax.broadcasted_iota(jnp.int32, c.shape, 1)

    def tile_r(t):
        t = jnp.where(lane < half_r, t, 0.0)
        out = t
        for k in range(1, LANES // half_r):
            out = out + pltpu.roll(t, k * half_r, axis=1)
        return out

    cr_ref[...] = tile_r(c)
    sr_ref[...] = tile_r(s)
    first = jnp.logical_and(lane >= QK_NOPE, lane < QK_NOPE + half_m)
    second = jnp.logical_and(lane >= QK_NOPE + half_m, lane < QK_HEAD)
    c1, c2 = pltpu.roll(c, QK_NOPE - half_r, axis=1), pltpu.roll(c, QK_NOPE + half_m - half_r, axis=1)
    s1, s2 = pltpu.roll(s, QK_NOPE - half_r, axis=1), pltpu.roll(s, QK_NOPE + half_m - half_r, axis=1)
    cm_ref[...] = jnp.where(first, c1, jnp.where(second, c2, 1.0))
    sm_ref[...] = jnp.where(first, -s1, jnp.where(second, s2, 0.0))


def _rope_tables(pos_f):
    n = pos_f.shape[0]
    tm = 1024
    half_r = HEAD_DIM // 2
    half_m = QK_ROPE // 2
    inv_r = ROPE_THETA ** (-jnp.arange(half_r, dtype=F32) / half_r)
    inv_m = ROPE_THETA ** (-jnp.arange(half_m, dtype=F32) / half_m)
    inv = jnp.concatenate([inv_r, inv_m, jnp.zeros((LANES - half_r - half_m,), F32)])[None, :]
    tab = pl.BlockSpec((tm, LANES), lambda i: (i, 0))
    shape = jax.ShapeDtypeStruct((n, LANES), F32)
    return pl.pallas_call(
        _rope_kernel,
        out_shape=(shape, shape, shape, shape),
        grid=(n // tm,),
        in_specs=[pl.BlockSpec((tm, 1), lambda i: (i, 0)), pl.BlockSpec((1, LANES), lambda i: (0, 0))],
        out_specs=(tab, tab, tab, tab),
        compiler_params=_cparams("parallel"),
        name="rope_tables",
    )(pos_f, inv)


def _inproj_kernel(x_ref, mod_ref, g_ref, w_ref, o_ref):
    h = _norm_mod(x_ref[...], g_ref[...], mod_ref[0:1, :], mod_ref[1:2, :]).astype(BF16)
    for c in range(N_IN // TN_PROJ):
        cols = slice(c * TN_PROJ, (c + 1) * TN_PROJ)
        o_ref[:, cols] = jnp.dot(h, w_ref[:, cols], preferred_element_type=F32).astype(BF16)


def _resident(shape):
    return pl.BlockSpec(shape, lambda *_: (0,) * len(shape), pipeline_mode=pl.Buffered(1))


def _inproj(x, mod, g, w, seq):
    n, d = x.shape
    tm = TM_PROJ
    tpb = seq // tm
    return pl.pallas_call(
        _inproj_kernel,
        out_shape=jax.ShapeDtypeStruct((n, N_IN), BF16),
        grid=(n // tm,),
        in_specs=[
            pl.BlockSpec((tm, d), lambda i: (i, 0)),
            pl.BlockSpec((None, 6, d), lambda i: (i // tpb, 0, 0)),
            pl.BlockSpec((1, d), lambda i: (0, 0)),
            _resident((d, N_IN)),
        ],
        out_specs=pl.BlockSpec((tm, N_IN), lambda i: (i, 0)),
        compiler_params=_cparams("parallel"),
        name="in_proj",
    )(x, mod, g, w)


def _mla_prep_kernel(cq_ref, ckv_ref, kra_ref, cm_ref, sm_ref, cqg_ref, wqa_ref, wqb_ref,
                     ckvg_ref, wk_ref, wv_ref, qga_ref, qgb_ref, kga_ref, kgb_ref,
                     qaug_ref, kaug_ref, vaug_ref, swap_ref, q_ref, k_ref, v_ref):
    cq = cq_ref[...].astype(F32)
    cqn = (cq * lax.rsqrt(jnp.mean(cq * cq, axis=-1, keepdims=True) + EPS) * cqg_ref[...]).astype(BF16)
    qa = jnp.dot(cqn, wqa_ref[...], preferred_element_type=F32)
    qb = jnp.dot(cqn, wqb_ref[...], preferred_element_type=F32)
    ckv = ckv_ref[...].astype(F32)
    ckvn = (ckv * lax.rsqrt(jnp.mean(ckv * ckv, axis=-1, keepdims=True) + EPS) * ckvg_ref[...]).astype(BF16)
    ka = jnp.dot(ckvn, wk_ref[...], preferred_element_type=F32)
    v_ref[...] = (jnp.dot(ckvn, wv_ref[...], preferred_element_type=F32) + vaug_ref[...]).astype(BF16)
    kra = kra_ref[...].astype(F32)
    krb = jnp.dot(kra_ref[...], swap_ref[...], preferred_element_type=F32)
    cm = cm_ref[...]
    sm = sm_ref[...]
    scale = QK_HEAD ** -0.5 * LOG2_E
    q_cos, q_sin = cm * (qga_ref[...] * scale), sm * (qgb_ref[...] * scale)
    k_cos, k_sin = cm * kga_ref[...], sm * kgb_ref[...]
    for h in range(N_GROUPS):
        sl = slice(h * LANES, (h + 1) * LANES)
        qah, qbh = qa[:, sl], qb[:, sl]
        r = lax.rsqrt(jnp.sum(qah * qah, axis=-1, keepdims=True) * (1.0 / QK_HEAD) + EPS)
        q_ref[:, sl] = ((qah * q_cos + qbh * q_sin) * r + qaug_ref[...]).astype(BF16)
        kah = ka[:, sl] + kra
        kbh = ka[:, sl] + krb
        r = lax.rsqrt(jnp.sum(kah * kah, axis=-1, keepdims=True) * (1.0 / QK_HEAD) + EPS)
        k_ref[:, sl] = ((kah * k_cos + kbh * k_sin) * r + kaug_ref[...]).astype(BF16)


def _mla_prep(proj, cm, sm, p):
    n = proj.shape[0]
    tm = TM_PREP
    hw = N_GROUPS * LANES

    def col(width, offset):
        return pl.BlockSpec((tm, width), lambda i: (i, offset // width))

    def full(a):
        return pl.BlockSpec(a.shape, lambda i: (0,) * a.ndim)

    weights = [p["cq_g"], p["wqa"], p["wqb"], p["ckv_g"], p["wk"], p["wv"],
               p["qga"], p["qgb"], p["kga"], p["kgb"], p["qaug"], p["kaug"], p["vaug"], p["swap"]]
    head_tile = pl.BlockSpec((tm, hw), lambda i: (i, 0))
    out = jax.ShapeDtypeStruct((n, hw), BF16)
    return pl.pallas_call(
        _mla_prep_kernel,
        out_shape=(out, out, out),
        grid=(n // tm,),
        in_specs=[col(Q_LORA, COL_CQ), col(KV_LORA, COL_CKV), col(LANES, COL_KRA),
                  pl.BlockSpec((tm, LANES), lambda i: (i, 0)), pl.BlockSpec((tm, LANES), lambda i: (i, 0))]
                 + [full(w) for w in weights],
        out_specs=(head_tile, head_tile, head_tile),
        compiler_params=_cparams("parallel"),
        name="mla_prep",
    )(proj, proj, proj, cm, sm, *weights)


def _flash_kernel(qi_ref, kj_ref, q_ref, k_ref, v_ref, o_ref, acc_scr, *rest, tq, online_max):
    i = qi_ref[pl.program_id(1)]
    j = kj_ref[pl.program_id(1)]

    @pl.when(j == 0)
    def _():
        acc_scr[...] = jnp.zeros(acc_scr.shape, F32)
        if online_max:
            rest[0][...] = jnp.full(rest[0].shape, -jnp.inf, F32)

    def block(k0, q0, nq, nk, masked):
        rows = slice(q0, q0 + nq)
        keys = slice(k0, k0 + nk)
        if masked:
            row = lax.broadcasted_iota(jnp.int32, (nq, nk), 0) + q0
            col = lax.broadcasted_iota(jnp.int32, (nq, nk), 1)
            keep = col <= row
        for h in range(N_GROUPS):
            sl = slice(h * LANES, (h + 1) * LANES)
            s = lax.dot_general(q_ref[rows, sl], k_ref[keys, sl], (((1,), (1,)), ((), ())),
                                preferred_element_type=F32)
            if masked:
                s = jnp.where(keep, s, -jnp.inf)
            if online_max:
                m_scr = rest[0]
                m_prev = m_scr[h, rows]
                m_new = jnp.maximum(m_prev, jnp.max(s, axis=-1, keepdims=True))
                p = jnp.exp2(s - m_new).astype(BF16)
                acc_scr[h, rows] = jnp.exp2(m_prev - m_new) * acc_scr[h, rows] + jnp.dot(
                    p, v_ref[keys, sl], preferred_element_type=F32)
                m_scr[h, rows] = m_new
            else:
                acc_scr[h, rows] += jnp.dot(jnp.exp2(s).astype(BF16), v_ref[keys, sl],
                                            preferred_element_type=F32)

    for t in range(KV_TILES):
        key_tile = j * KV_TILES + t

        @pl.when(key_tile < i)
        def _():
            block(t * tq, 0, tq, tq, False)

        @pl.when(key_tile == i)
        def _():
            block(t * tq, 0, tq // 2, tq // 2, True)
            block(t * tq, tq // 2, tq // 2, tq, True)
            lane = lax.broadcasted_iota(jnp.int32, (tq, LANES), 1)
            for pr in range(N_GROUPS // 2):
                lo = acc_scr[2 * pr]
                hi = acc_scr[2 * pr + 1]
                lo = lo / lo[:, V_HEAD:V_HEAD + 1]
                hi = hi / hi[:, V_HEAD:V_HEAD + 1]
                both = jnp.where(lane < V_HEAD, lo, pltpu.roll(hi, V_HEAD, axis=1))
                o_ref[:, pr * LANES:(pr + 1) * LANES] = both.astype(BF16)


def _flash(q, k, v, batch, seq, online_max):
    n = q.shape[0]
    tq = TQ_ATT
    nq = seq // tq
    hw = N_GROUPS * LANES
    scratch = [pltpu.VMEM((N_GROUPS, tq, LANES), F32)]
    if online_max:
        scratch.append(pltpu.VMEM((N_GROUPS, tq, 1), F32))
    assert nq % KV_TILES == 0
    nkb = nq // KV_TILES
    pairs = [(i, j) for i in range(nq) for j in range(i // KV_TILES + 1)]
    qi = jnp.asarray([p[0] for p in pairs], jnp.int32)
    kj = jnp.asarray([p[1] for p in pairs], jnp.int32)
    q_tile = lambda b, s, qi, kj: (b * nq + qi[s], 0)
    k_block = lambda b, s, qi, kj: (b * nkb + kj[s], 0)
    return pl.pallas_call(
        functools.partial(_flash_kernel, tq=tq, online_max=online_max),
        out_shape=jax.ShapeDtypeStruct((n, MIX_W), BF16),
        grid_spec=pltpu.PrefetchScalarGridSpec(
            num_scalar_prefetch=2,
            grid=(batch, len(pairs)),
            in_specs=[pl.BlockSpec((tq, hw), q_tile), pl.BlockSpec((KV_TILES * tq, hw), k_block),
                      pl.BlockSpec((KV_TILES * tq, hw), k_block)],
            out_specs=pl.BlockSpec((tq, MIX_W), q_tile),
            scratch_shapes=scratch,
        ),
        compiler_params=_cparams("parallel", "arbitrary"),
        name="mla_flash_online" if online_max else "mla_flash",
    )(qi, kj, q, k, v)


def _gelu_tanh(x):
    return jax.nn.gelu(x, approximate=True)


def _mix_kernel(gates_ref, a_ref, r_ref, su_ref, ymla_ref, x_ref, cos_ref, sin_ref, mod_ref,
                convw_ref, gvg_ref, wscat_ref, bsmat_ref, retg_ref, dec_ref, kdec_ref, qdec_ref,
                cdec_ref, bd_ref, gmat_ref, mk_ref, mv_ref, wb_ref, wo_ref,
                o_ref, carry_scr, state_scr, ysg_scr, yret_scr, *, tm, tpb):
    i = pl.program_id(0)

    @pl.when(i % tpb == 0)
    def _():
        carry_scr[...] = jnp.zeros(carry_scr.shape, F32)
        state_scr[...] = jnp.zeros(state_scr.shape, F32)

    w = MIX_W
    a_b = a_ref[:, 0:w].astype(F32)
    u = a_ref[:, w:2 * w].astype(F32) * a_ref[:, 2 * w:3 * w].astype(F32)
    rowi = lax.broadcasted_iota(jnp.int32, (tm, w), 0)
    prev1 = carry_scr[0:1, :]
    prev2 = carry_scr[1:2, :]
    u1 = jnp.where(rowi == 0, prev1, pltpu.roll(u, 1, axis=0))
    u2 = jnp.where(rowi == 0, prev2, jnp.where(rowi == 1, prev1, pltpu.roll(u, 2, axis=0)))
    carry_scr[0:1, :] = u[tm - 1:tm, :]
    carry_scr[1:2, :] = u[tm - 2:tm - 1, :]
    y_conv = a_b * (convw_ref[0:1, :] * u2 + convw_ref[1:2, :] * u1 + convw_ref[2:3, :] * u)

    gmat = gmat_ref[...]
    s_u = _gelu_tanh(su_ref[:, 0:w].astype(F32))
    s_v = _gelu_tanh(su_ref[:, w:2 * w].astype(F32))
    ms = _group_mean(s_v * s_v, gmat)
    vn = (s_v * lax.rsqrt(ms + EPS) * gvg_ref[...]).astype(BF16)

    cosr = cos_ref[...]
    sinr = sin_ref[...]

    def rot(t):
        t1, t2 = t[:, 0:LANES], t[:, LANES:2 * LANES]
        return jnp.concatenate([t1 * cosr - t2 * sinr, t2 * cosr + t1 * sinr], axis=-1)

    rq = rot(r_ref[:, 0:w].astype(F32))
    rk = rot(r_ref[:, w:2 * w].astype(F32)) * (HEAD_DIM ** -0.5)

    for c in range(tm // CHUNK):
        rows = slice(c * CHUNK, (c + 1) * CHUNK)
        vc = vn[rows, :]
        vbd = jnp.concatenate([vc * mv_ref[g:g + 1, :].astype(BF16) for g in range(N_GROUPS)], axis=0)
        mixed = jnp.dot(wscat_ref[...], vbd, preferred_element_type=F32) + bsmat_ref[...]
        ysg_scr[rows, :] = s_u[rows, :] * mixed

        qc = rq[rows, :]
        kc = rk[rows, :]
        kcb = kc.astype(BF16)
        vcb = r_ref[rows, 2 * w:3 * w]
        qstack = jnp.concatenate([(qc * mk_ref[h:h + 1, :]).astype(BF16) for h in range(N_GROUPS)], axis=0)
        sc = lax.dot_general(qstack, kcb, (((1,), (1,)), ((), ())), preferred_element_type=F32)
        sc = (sc * dec_ref[...]).astype(BF16)
        scat = jnp.concatenate([sc[h * CHUNK:(h + 1) * CHUNK, :] for h in range(N_GROUPS)], axis=1)
        vstack = jnp.concatenate([vcb * mv_ref[h:h + 1, :].astype(BF16) for h in range(N_GROUPS)], axis=0)
        o_c = jnp.dot(scat, vstack, preferred_element_type=F32)
        state = state_scr[...]
        o_c = o_c + jnp.dot((qc * qdec_ref[...]).astype(BF16), state.astype(BF16),
                            preferred_element_type=F32)
        kd_t = jnp.transpose(kc * kdec_ref[...]).astype(BF16)
        kv = jnp.dot(kd_t, vcb, preferred_element_type=F32)
        state_scr[...] = state * cdec_ref[...] + kv * bd_ref[...]
        yret_scr[rows, :] = o_c

    o_all = yret_scr[...]
    xc = o_all - _group_mean(o_all, gmat)
    var = _group_mean(xc * xc, gmat)
    r_g = r_ref[:, 3 * w:4 * w].astype(F32)
    y_ret = (r_g * _sigmoid(r_g)) * (xc * lax.rsqrt(var + EPS) * retg_ref[...])

    d = x_ref.shape[1]
    ys = (y_conv, ymla_ref[...], ysg_scr[...], y_ret)
    merged = None
    for n in range(N_BRANCH):
        gate = _sigmoid(gates_ref[:, n * d:(n + 1) * d])
        term = gate * jnp.dot(ys[n].astype(BF16), wb_ref[n], preferred_element_type=F32).astype(BF16)
        merged = term if merged is None else merged + term
    out = jnp.dot(merged, wo_ref[...], preferred_element_type=F32)
    o_ref[...] = x_ref[...] + mod_ref[2:3, :] * out


def _mixers(proj, ymla, x, cosr, sinr, mod, p, seq):
    n, d = x.shape
    tm = TM_MIX
    tpb = seq // tm

    def col(width, offset):
        return pl.BlockSpec((tm, width), lambda i: (i, offset // width))

    def full(a):
        return pl.BlockSpec(a.shape, lambda i: (0,) * a.ndim)

    consts = [p["conv_w"], p["gv_g"], p["ws_cat"], p["bs_mat"], p["ret_g"], p["dec"], p["kdec"],
              p["qdec"], p["cdec"], p["bd"], p["gmat"], p["mk"], p["mv"], p["w_branch"], p["w_o"]]
    return pl.pallas_call(
        functools.partial(_mix_kernel, tm=tm, tpb=tpb),
        out_shape=jax.ShapeDtypeStruct((n, d), F32),
        grid=(n // tm,),
        in_specs=[col(N_BRANCH * d, COL_GATES), col(4 * MIX_W, COL_A), col(4 * MIX_W, COL_R),
                  col(2 * MIX_W, COL_SU),
                  pl.BlockSpec((tm, MIX_W), lambda i: (i, 0)),
                  pl.BlockSpec((tm, d), lambda i: (i, 0)),
                  pl.BlockSpec((tm, LANES), lambda i: (i, 0)),
                  pl.BlockSpec((tm, LANES), lambda i: (i, 0)),
                  pl.BlockSpec((None, 6, d), lambda i: (i // tpb, 0, 0))]
                 + [full(c) for c in consts],
        out_specs=pl.BlockSpec((tm, d), lambda i: (i, 0)),
        scratch_shapes=[pltpu.VMEM((8, MIX_W), F32), pltpu.VMEM((MIX_W, MIX_W), F32),
                        pltpu.VMEM((tm, MIX_W), F32), pltpu.VMEM((tm, MIX_W), F32)],
        compiler_params=_cparams("arbitrary"),
        name="mixers_merge",
    )(proj, proj, proj, proj, ymla, x, cosr, sinr, mod, *consts)


def _ffn_kernel(x_ref, mod_ref, g_ref, w1_ref, w3_ref, w2_ref, o_ref):
    x = x_ref[...]
    h = _norm_mod(x, g_ref[...], mod_ref[3:4, :], mod_ref[4:5, :]).astype(BF16)
    a = jnp.dot(h, w1_ref[...], preferred_element_type=F32)
    b = jnp.dot(h, w3_ref[...], preferred_element_type=F32)
    hid = ((a * _sigmoid(a)) * b).astype(BF16)
    o_ref[...] = x + mod_ref[5:6, :] * jnp.dot(hid, w2_ref[...], preferred_element_type=F32)


def _dense_ffn(x, mod, g, w1, w3, w2, seq):
    n, d = x.shape
    dff = w1.shape[1]
    tm = TM_FFN
    tpb = seq // tm
    return pl.pallas_call(
        _ffn_kernel,
        out_shape=jax.ShapeDtypeStruct((n, d), F32),
        grid=(n // tm,),
        in_specs=[
            pl.BlockSpec((tm, d), lambda i: (i, 0)),
            pl.BlockSpec((None, 6, d), lambda i: (i // tpb, 0, 0)),
            pl.BlockSpec((1, d), lambda i: (0, 0)),
            _resident((d, dff)), _resident((d, dff)), _resident((dff, d)),
        ],
        out_specs=pl.BlockSpec((tm, d), lambda i: (i, 0)),
        compiler_params=_cparams("parallel"),
        name="dense_swiglu",
    )(x, mod, g, w1, w3, w2)


def _router_kernel(x_ref, mod_ref, g_ref, rw_ref, rb_ref, hs_ref, ei_ref, pw_ref, meta_ref, tot_ref,
                   carry_scr, *, tm, srows):
    i = pl.program_id(0)

    @pl.when(i == 0)
    def _():
        carry_scr[...] = jnp.zeros(carry_scr.shape, F32)

    h = _norm_mod(x_ref[...], g_ref[...], mod_ref[3:4, :], mod_ref[4:5, :])

    h_hi = h.astype(BF16)
    h_lo = (h - h_hi.astype(F32)).astype(BF16)
    hw = jnp.dot(h_hi, rw_ref[...], preferred_element_type=F32)
    logits = (hw[:, :LANES] + hw[:, LANES:] + jnp.dot(h_lo, rw_ref[:, :LANES], preferred_element_type=F32)
              + rb_ref[...])
    mx = jnp.max(logits, axis=-1, keepdims=True)
    ex = jnp.exp(logits - mx)
    probs = ex / jnp.sum(ex, axis=-1, keepdims=True)
    lane = lax.broadcasted_iota(jnp.int32, (tm, LANES), 1)
    valid = lane < N_EXPERTS
    probs = jnp.where(valid, probs, -1.0)
    m1 = jnp.max(probs, axis=-1, keepdims=True)
    i1 = jnp.min(jnp.where(probs == m1, lane, LANES), axis=-1, keepdims=True)
    rest = jnp.where(lane == i1, -1.0, probs)
    m2 = jnp.max(rest, axis=-1, keepdims=True)
    i2 = jnp.min(jnp.where(rest == m2, lane, LANES), axis=-1, keepdims=True)
    den = m1 + m2
    pw_ref[...] = jnp.where(lane == 0, m1 / den, jnp.where(lane == 1, m2 / den, 0.0))

    sel1 = lane == i1
    sel2 = lane == i2
    onehot = jnp.where(sel1, 1.0, 0.0) + jnp.where(sel2, 1.0, 0.0)
    r_i = lax.broadcasted_iota(jnp.int32, (tm, tm), 0)
    c_i = lax.broadcasted_iota(jnp.int32, (tm, tm), 1)
    tri = jnp.where(c_i < r_i, 1.0, 0.0).astype(BF16)
    before = jnp.dot(tri, onehot.astype(BF16), preferred_element_type=F32)
    cnt = jnp.sum(onehot, axis=0, keepdims=True)
    cnt_al = jnp.floor((cnt + (ROW_ALIGN - 1)) * (1.0 / ROW_ALIGN)) * ROW_ALIGN
    e_r = lax.broadcasted_iota(jnp.int32, (LANES, LANES), 0)
    e_c = lax.broadcasted_iota(jnp.int32, (LANES, LANES), 1)
    upper = jnp.where(e_r < e_c, 1.0, 0.0)
    loff = jnp.dot(jnp.broadcast_to(cnt_al, (8, LANES)), upper, precision=HIGHEST,
                   preferred_element_type=F32)[0:1, :]
    slot = loff + before
    slot1 = jnp.sum(jnp.where(sel1, slot, 0.0), axis=-1, keepdims=True).astype(jnp.int32)
    slot2 = jnp.sum(jnp.where(sel2, slot, 0.0), axis=-1, keepdims=True).astype(jnp.int32)
    ei = jnp.where(lane == 0, i1, jnp.where(lane == 1, i2, 0))
    ei_ref[...] = jnp.where(lane == 2, slot1, jnp.where(lane == 3, slot2, ei))

    r_idx = lax.broadcasted_iota(jnp.int32, (tm, srows), 1)
    place = jnp.where(r_idx == slot1, 1.0, jnp.where(r_idx == slot2, 1.0, 0.0)).astype(BF16)
    hs = lax.dot_general(place, h.astype(BF16), (((0,), (0,)), ((), ())), preferred_element_type=F32)
    half = hs.shape[1] // 2
    hs_ref[...] = _pack_bf16_pair(hs[:, :half], hs[:, half:])

    carry = carry_scr[0:1, :]
    mrow = lax.broadcasted_iota(jnp.int32, (8, LANES), 0)
    meta = jnp.where(mrow == 0, cnt_al, jnp.where(mrow == 1, carry, jnp.where(mrow == 2, loff, 0.0)))
    meta_ref[...] = meta.astype(jnp.int32)
    carry_scr[0:1, :] = carry + cnt_al
    tot_ref[...] = jnp.broadcast_to(carry + cnt_al, tot_ref.shape).astype(jnp.int32)


def _router(x, mod, g, rw_pad, rb_pad, seq):
    n, d = x.shape
    tm = TM_ROUTE
    tpb = seq // tm
    nt = n // tm
    return pl.pallas_call(
        functools.partial(_router_kernel, tm=tm, srows=SORT_ROWS),
        out_shape=(jax.ShapeDtypeStruct((nt * SORT_ROWS, d // 2), jnp.uint32),
                   jax.ShapeDtypeStruct((n, LANES), jnp.int32),
                   jax.ShapeDtypeStruct((n, LANES), F32),
                   jax.ShapeDtypeStruct((nt, 8, LANES), jnp.int32),
                   jax.ShapeDtypeStruct((8, LANES), jnp.int32)),
        grid=(nt,),
        in_specs=[
            pl.BlockSpec((tm, d), lambda i: (i, 0)),
            pl.BlockSpec((None, 6, d), lambda i: (i // tpb, 0, 0)),
            pl.BlockSpec((1, d), lambda i: (0, 0)),
            pl.BlockSpec((d, 2 * LANES), lambda i: (0, 0)),
            pl.BlockSpec((1, LANES), lambda i: (0, 0)),
        ],
        out_specs=(pl.BlockSpec((SORT_ROWS, d // 2), lambda i: (i, 0)),
                   pl.BlockSpec((tm, LANES), lambda i: (i, 0)),
                   pl.BlockSpec((tm, LANES), lambda i: (i, 0)),
                   pl.BlockSpec((None, 8, LANES), lambda i: (i, 0, 0)),
                   pl.BlockSpec((8, LANES), lambda i: (0, 0))),
        scratch_shapes=[pltpu.VMEM((8, LANES), F32)],
        compiler_params=_cparams("arbitrary"),
        name="router_top2",
    )(x, mod, g, rw_pad, rb_pad)


def _segment_copy(src_hbm, dst_hbm, src_row, dst_row, n_rows, sem):
    src_row = pl.multiple_of(src_row, ROW_ALIGN)
    dst_row = pl.multiple_of(dst_row, ROW_ALIGN)
    n_rows = pl.multiple_of(n_rows, ROW_ALIGN)
    return pltpu.make_async_copy(src_hbm.at[pl.ds(src_row, n_rows)], dst_hbm.at[pl.ds(dst_row, n_rows)], sem)


def _dispatch_kernel(loff_ref, dst_ref, cnt_ref, rows_ref, gap_ref, hs_ref, xs_ref, buf, zbuf, sem_in, sem_out,
                     sem_zero, *, n_tiles, srows, n_out_tiles):
    zbuf[...] = jnp.zeros(zbuf.shape, zbuf.dtype)
    tg = zbuf.shape[0]
    first_unused = gap_ref[2 * N_EXPERTS]

    def zero_gap(e):
        return _segment_copy(zbuf, xs_ref, 0, gap_ref[2 * e], gap_ref[2 * e + 1], sem_zero)

    def zero_tile(t):
        return pltpu.make_async_copy(zbuf, xs_ref.at[pl.ds(pl.multiple_of(t * tg, ROW_ALIGN), tg)], sem_zero)

    def for_each_zero_copy(action):
        for e in range(N_EXPERTS):
            @pl.when(gap_ref[2 * e + 1] > 0)
            def _():
                action(zero_gap(e))

        def tail(t, carry):
            action(zero_tile(t))
            return carry

        lax.fori_loop(first_unused, n_out_tiles, tail, 0)

    for_each_zero_copy(lambda copy: copy.start())

    def fetch(t):
        slot = t % DISPATCH_SLOTS
        return pltpu.make_async_copy(hs_ref.at[pl.ds(pl.multiple_of(t * srows, ROW_ALIGN), srows)],
                                     buf.at[slot], sem_in.at[slot])

    def drain(t):
        slot = t % DISPATCH_SLOTS
        n_rows = rows_ref[t]

        @pl.when(n_rows > 0)
        def _():
            _segment_copy(buf.at[slot], xs_ref, 0, 0, n_rows, sem_out.at[slot]).wait()

    for t in range(DISPATCH_AHEAD):
        fetch(t).start()

    def body(t, carry):
        slot = t % DISPATCH_SLOTS
        fetch(t).wait()
        for e in range(N_EXPERTS):
            s = t * N_EXPERTS + e
            n_rows = cnt_ref[s]

            @pl.when(n_rows > 0)
            def _():
                _segment_copy(buf.at[slot], xs_ref, loff_ref[s], dst_ref[s], n_rows, sem_out.at[slot]).start()

        @pl.when(t + DISPATCH_AHEAD < n_tiles)
        def _():
            @pl.when(t + DISPATCH_AHEAD >= DISPATCH_SLOTS)
            def _():
                drain(t + DISPATCH_AHEAD - DISPATCH_SLOTS)

            fetch(t + DISPATCH_AHEAD).start()

        return carry

    lax.fori_loop(0, n_tiles, body, 0)
    for t in range(max(n_tiles - DISPATCH_SLOTS, 0), n_tiles):
        drain(t)
    for_each_zero_copy(lambda copy: copy.wait())


def _dispatch(seg_loff, seg_dst, seg_cnt, tile_rows, gaps, hs, n_out_tiles):
    n_tiles = tile_rows.shape[0]
    srows = hs.shape[0] // n_tiles
    assert n_tiles >= DISPATCH_SLOTS
    return pl.pallas_call(
        functools.partial(_dispatch_kernel, n_tiles=n_tiles, srows=srows, n_out_tiles=n_out_tiles),
        out_shape=jax.ShapeDtypeStruct((n_out_tiles * TG_MOE, hs.shape[1]), hs.dtype),
        grid_spec=pltpu.PrefetchScalarGridSpec(
            num_scalar_prefetch=5,
            grid=(1,),
            in_specs=[pl.BlockSpec(memory_space=pl.ANY)],
            out_specs=pl.BlockSpec(memory_space=pl.ANY),
            scratch_shapes=[pltpu.VMEM((DISPATCH_SLOTS, srows, hs.shape[1]), hs.dtype),
                            pltpu.VMEM((TG_MOE, hs.shape[1]), hs.dtype),
                            pltpu.SemaphoreType.DMA((DISPATCH_SLOTS,)),
                            pltpu.SemaphoreType.DMA((DISPATCH_SLOTS,)),
                            pltpu.SemaphoreType.DMA],
        ),
        compiler_params=_cparams("arbitrary"),
        name="moe_dispatch",
    )(seg_loff, seg_dst, seg_cnt, tile_rows, gaps, hs)


def _expert_kernel(te_ref, used_ref, xs_ref, w1_hbm, w3_hbm, w2_hbm, y_ref, wb1, wb3, wb2, stage, sem):
    t = pl.program_id(0)
    e = te_ref[t]
    first_of_expert = jnp.logical_or(t == 0, e != te_ref[jnp.maximum(t - 1, 0)])

    @pl.when(jnp.logical_and(used_ref[t] == 1, first_of_expert))
    def _():
        n_slots = stage.shape[0]
        windows = [(src, dst, r, c) for src, dst in ((w1_hbm, wb1), (w3_hbm, wb3), (w2_hbm, wb2))
                   for r in range(dst.shape[0] // W_CHUNK) for c in range(dst.shape[1] // W_CHUNK)]

        def staged_copy(k):
            src, _, r, c = windows[k]
            return pltpu.make_async_copy(src.at[e, pl.ds(r * W_CHUNK, W_CHUNK), pl.ds(c * W_CHUNK, W_CHUNK)],
                                         stage.at[k % n_slots], sem.at[k % n_slots])

        for k in range(n_slots - 1):
            staged_copy(k).start()
        for k, (_, dst, r, c) in enumerate(windows):
            staged_copy(k).wait()
            dst[r * W_CHUNK:(r + 1) * W_CHUNK, c * W_CHUNK:(c + 1) * W_CHUNK] = stage[k % n_slots].astype(BF16)
            if k + n_slots - 1 < len(windows):
                staged_copy(k + n_slots - 1).start()

    @pl.when(used_ref[t] == 1)
    def _():
        lo, hi = _unpack_bf16_pair(xs_ref[...])
        h = jnp.concatenate([lo.astype(BF16), hi.astype(BF16)], axis=1)
        a = jnp.dot(h, wb1[...], preferred_element_type=F32)
        b = jnp.dot(h, wb3[...], preferred_element_type=F32)
        hid = ((a * _sigmoid(a)) * b).astype(BF16)
        acc = jnp.dot(hid, wb2[...], preferred_element_type=F32)
        half = acc.shape[1] // 2
        y_ref[...] = _pack_bf16_pair(acc[:, :half], acc[:, half:])

    @pl.when(used_ref[t] == 0)
    def _():
        y_ref[...] = jnp.zeros(y_ref.shape, y_ref.dtype)


def _expert_ffn(tile_e, tile_used, xs, w1, w3, w2):
    rows, half = xs.shape
    d = 2 * half
    dff = w1.shape[2]
    tg = TG_MOE
    assert dff % W_CHUNK == 0
    hbm = pl.BlockSpec(memory_space=pl.ANY)
    return pl.pallas_call(
        _expert_kernel,
        out_shape=jax.ShapeDtypeStruct((rows, half), jnp.uint32),
        grid_spec=pltpu.PrefetchScalarGridSpec(
            num_scalar_prefetch=2,
            grid=(rows // tg,),
            in_specs=[pl.BlockSpec((tg, half), lambda t, te, us: (t, 0)), hbm, hbm, hbm],
            out_specs=pl.BlockSpec((tg, half), lambda t, te, us: (t, 0)),
            scratch_shapes=[pltpu.VMEM((d, dff), BF16), pltpu.VMEM((d, dff), BF16), pltpu.VMEM((dff, d), BF16),
                            pltpu.VMEM((W_SLOTS, W_CHUNK, W_CHUNK), F32), pltpu.SemaphoreType.DMA((W_SLOTS,))],
        ),
        compiler_params=_cparams("arbitrary"),
        name="expert_swiglu",
    )(tile_e, tile_used, xs, w1, w3, w2)


def _combine_kernel(src_ref, loff_ref, cnt_ref, rows_ref, x_ref, ei_ref, pw_ref, mod_ref, y_ref, o_ref,
                    ybuf, sem, *, tm, srows):
    i = pl.program_id(0)
    slot = i % 2

    def fetch(tile, into):
        ybuf[into] = jnp.zeros(ybuf.shape[1:], ybuf.dtype)
        for e in range(N_EXPERTS):
            s = tile * N_EXPERTS + e
            n_rows = cnt_ref[s]

            @pl.when(n_rows > 0)
            def _():
                _segment_copy(y_ref, ybuf.at[into], src_ref[s], loff_ref[s], n_rows, sem.at[into]).start()

    @pl.when(i == 0)
    def _():
        fetch(i, slot)

    @pl.when(i + 1 < pl.num_programs(0))
    def _():
        fetch(i + 1, 1 - slot)

    @pl.when(rows_ref[i] > 0)
    def _():
        _segment_copy(y_ref, ybuf.at[slot], 0, 0, rows_ref[i], sem.at[slot]).wait()

    lo, hi = _unpack_bf16_pair(ybuf[slot])
    ys = jnp.concatenate([lo.astype(BF16), hi.astype(BF16)], axis=1)
    r_idx = lax.broadcasted_iota(jnp.int32, (tm, srows), 1)
    mix = jnp.zeros(x_ref.shape, F32)
    for k in range(TOP_K):
        pick = jnp.where(r_idx == ei_ref[:, TOP_K + k:TOP_K + k + 1], 1.0, 0.0).astype(BF16)
        mix = mix + pw_ref[:, k:k + 1] * jnp.dot(pick, ys, preferred_element_type=F32)
    o_ref[...] = x_ref[...] + mod_ref[5:6, :] * mix


def _combine(seg_src, seg_loff, seg_cnt, tile_rows, x, ei, pw, mod, y, seq):
    n, d = x.shape
    tm = TM_ROUTE
    tpb = seq // tm
    tok = lambda width: pl.BlockSpec((tm, width), lambda i, *_: (i, 0))
    return pl.pallas_call(
        functools.partial(_combine_kernel, tm=tm, srows=SORT_ROWS),
        out_shape=jax.ShapeDtypeStruct((n, d), F32),
        grid_spec=pltpu.PrefetchScalarGridSpec(
            num_scalar_prefetch=4,
            grid=(n // tm,),
            in_specs=[tok(d), tok(LANES), tok(LANES),
                      pl.BlockSpec((None, 6, d), lambda i, *_: (i // tpb, 0, 0)),
                      pl.BlockSpec(memory_space=pl.ANY)],
            out_specs=tok(d),
            scratch_shapes=[pltpu.VMEM((2, SORT_ROWS, d // 2), jnp.uint32), pltpu.SemaphoreType.DMA((2,))],
        ),
        compiler_params=_cparams("arbitrary"),
        name="moe_combine",
    )(seg_src, seg_loff, seg_cnt, tile_rows, x, ei, pw, mod, y)


def _pack_w_in(w_in):
    d = w_in.shape[0]
    w = MIX_W
    o_ckv = 3 * w + Q_LORA
    o_kr = o_ckv + KV_LORA
    o_su = o_kr + QK_ROPE
    o_rq = o_su + 2 * w
    o_gate = o_rq + 4 * w
    half = HEAD_DIM // 2
    perm = np.array([h * HEAD_DIM + part * half + i
                     for part in range(2) for h in range(N_GROUPS) for i in range(half)])
    kr = w_in[:, o_kr:o_kr + QK_ROPE]
    z = lambda k: jnp.zeros((d, k), w_in.dtype)
    cols = [
        w_in[:, o_gate:o_gate + N_BRANCH * d],
        w_in[:, 0:3 * w + Q_LORA],
        w_in[:, o_rq:o_rq + w][:, perm], w_in[:, o_rq + w:o_rq + 2 * w][:, perm],
        w_in[:, o_rq + 2 * w:o_rq + 4 * w],
        w_in[:, o_su:o_su + 2 * w],
        w_in[:, o_ckv:o_ckv + KV_LORA],
        z(QK_NOPE), kr, z(LANES - QK_HEAD),
    ]
    return jnp.concatenate(cols, axis=1).astype(BF16)


def _swap_rope_halves(a):
    hr = QK_ROPE // 2
    return jnp.concatenate([a[..., :QK_NOPE], a[..., QK_NOPE + hr:QK_HEAD], a[..., QK_NOPE:QK_NOPE + hr],
                            a[..., QK_HEAD:]], axis=-1)


def _mla_params(cq_g, w_uq, ckv_g, w_ukv, qn_g, kn_g):
    pad = LANES - QK_HEAD
    wq = w_uq.reshape(Q_LORA, N_GROUPS, QK_HEAD)
    wq = jnp.pad(wq, ((0, 0), (0, 0), (0, pad)))
    wkv = w_ukv.reshape(KV_LORA, N_GROUPS, QK_NOPE + V_HEAD)
    wk = jnp.pad(wkv[:, :, :QK_NOPE], ((0, 0), (0, 0), (0, LANES - QK_NOPE)))
    wv = jnp.pad(wkv[:, :, QK_NOPE:], ((0, 0), (0, 0), (0, LANES - V_HEAD)))
    qg = jnp.pad(qn_g, (0, pad))[None, :]
    kg = jnp.pad(kn_g, (0, pad))[None, :]
    bound = (QK_HEAD ** 0.5 * LOG2_E) * jnp.max(jnp.abs(qn_g)) * jnp.max(jnp.abs(kn_g))
    static_shift = bound <= MAX_STATIC_SHIFT
    lane = jnp.arange(LANES)
    qaug = (lane == QK_HEAD).astype(F32)[None, :]
    kaug = qaug * jnp.where(static_shift, -bound, 0.0)
    vaug = jnp.tile((lane == V_HEAD).astype(F32), N_GROUPS)[None, :]
    params = {
        "cq_g": cq_g[None, :], "ckv_g": ckv_g[None, :],
        "wqa": wq.reshape(Q_LORA, -1).astype(BF16),
        "wqb": _swap_rope_halves(wq).reshape(Q_LORA, -1).astype(BF16),
        "wk": wk.reshape(KV_LORA, -1).astype(BF16),
        "wv": wv.reshape(KV_LORA, -1).astype(BF16),
        "qga": qg, "qgb": _swap_rope_halves(qg), "kga": kg, "kgb": _swap_rope_halves(kg),
        "qaug": qaug, "kaug": kaug, "vaug": vaug,
        "swap": (_swap_rope_halves(lane[None, :])[0][None, :] == lane[:, None]).astype(BF16),
    }
    return params, static_shift


def _mixer_consts():
    h = jnp.arange(N_GROUPS, dtype=F32)
    log_gamma = jnp.log1p(-(2.0 ** (-5.0 - h)))
    pos = jnp.arange(CHUNK, dtype=F32)
    rel = pos[:, None] - pos[None, :]
    dec = jnp.where(rel >= 0, jnp.exp(log_gamma[:, None, None] * jnp.maximum(rel, 0.0)), 0.0)
    lane = np.arange(MIX_W)
    head_k = (lane % LANES) // (HEAD_DIM // 2)
    head_v = lane // HEAD_DIM
    lg_k = log_gamma[head_k]
    return {
        "dec": dec.reshape(N_GROUPS * CHUNK, CHUNK),
        "kdec": jnp.exp(lg_k[None, :] * (CHUNK - 1.0 - pos)[:, None]),
        "qdec": jnp.exp(lg_k[None, :] * (pos + 1.0)[:, None]),
        "cdec": jnp.broadcast_to(jnp.exp(lg_k * CHUNK)[:, None], (MIX_W, MIX_W)),
        "bd": jnp.asarray((head_k[:, None] == head_v[None, :]).astype(np.float32)),
        "gmat": jnp.asarray((head_v[:, None] == head_v[None, :]).astype(np.float32) / HEAD_DIM).astype(BF16),
        "mk": jnp.asarray((head_k[None, :] == np.arange(N_GROUPS)[:, None]).astype(np.float32)),
        "mv": jnp.asarray((head_v[None, :] == np.arange(N_GROUPS)[:, None]).astype(np.float32)),
    }


def _mixer_params(conv_w, gv_g, w_s, b_s, ret_g, w_branch, w_o):
    p = dict(_mixer_consts())
    ws = jnp.tril(w_s)
    p.update({
        "conv_w": conv_w,
        "gv_g": gv_g.reshape(1, MIX_W),
        "ws_cat": jnp.transpose(ws, (1, 0, 2)).reshape(CHUNK, N_GROUPS * CHUNK).astype(BF16),
        "bs_mat": jnp.repeat(b_s.T, HEAD_DIM, axis=1),
        "ret_g": ret_g.reshape(1, MIX_W),
        "w_branch": w_branch.astype(BF16),
        "w_o": w_o.astype(BF16),
    })
    return p


def _moe_layout(meta, tot, n_tiles):
    totals = tot[0, :N_EXPERTS]
    padded = ((totals + TG_MOE - 1) // TG_MOE) * TG_MOE
    ends = jnp.cumsum(padded)
    starts = ends - padded
    seg_cnt = meta[:, 0, :N_EXPERTS]
    seg_loff = meta[:, 2, :N_EXPERTS]
    seg_grouped = starts[None, :] + meta[:, 1, :N_EXPERTS]
    tile_start = jnp.arange(n_tiles, dtype=jnp.int32) * TG_MOE
    tile_e = jnp.sum((tile_start[:, None] >= ends[None, :]).astype(jnp.int32), axis=1)
    used = (tile_start < ends[-1]).astype(jnp.int32)
    last_e = jnp.sum((ends[-1] - 1 >= ends).astype(jnp.int32))
    tile_e = jnp.minimum(jnp.where(used == 1, tile_e, last_e), N_EXPERTS - 1)
    flat = lambda a: a.reshape(-1).astype(jnp.int32)
    gaps = jnp.concatenate([flat(jnp.stack([starts + totals, padded - totals], axis=1)),
                            flat(ends[-1:] // TG_MOE)])
    return flat(seg_grouped), flat(seg_loff), flat(seg_cnt), gaps, tile_e, used


def kernel(x, c, positions, norm1_g, norm2_g, ada_w, ada_b, w_in, conv_w, cq_g, w_uq, ckv_g, w_ukv, qn_g, kn_g, gv_g, w_s, b_s, ret_g, w_branch, w_o, ffn_w1, ffn_w3, ffn_w2, router_w, router_b, moe_w1, moe_w3, moe_w2):
    batch, seq, d = x.shape
    depth = ada_w.shape[0]
    n = batch * seq
    assert seq % max(TM_PROJ, TM_PREP, TQ_ATT, TM_MIX, TM_FFN, TM_ROUTE) == 0
    assert d // 2 % LANES == 0

    c_t = jnp.pad(c, ((0, 8 - batch), (0, 0))).T
    ada = _ada(c_t, ada_w, ada_b, batch)[:, :batch].reshape(depth, batch, 6, d)
    cosr, sinr, cm, sm = _rope_tables(positions.astype(F32).reshape(n, 1))

    xt = x.reshape(n, d)
    for l in range(depth):
        mod = ada[l]
        proj = _inproj(xt, mod, norm1_g[l][None, :], _pack_w_in(w_in[l]), seq)
        mla_p, static_shift = _mla_params(cq_g[l], w_uq[l], ckv_g[l], w_ukv[l], qn_g[l], kn_g[l])
        q, k, v = _mla_prep(proj, cm, sm, mla_p)
        y_mla = lax.cond(static_shift,
                         functools.partial(_flash, batch=batch, seq=seq, online_max=False),
                         functools.partial(_flash, batch=batch, seq=seq, online_max=True), q, k, v)
        mp = _mixer_params(conv_w[l], gv_g[l], w_s[l], b_s[l], ret_g[l], w_branch[l], w_o[l])
        xt = _mixers(proj, y_mla, xt, cosr, sinr, mod, mp, seq)
        g2n = norm2_g[l][None, :]
        if l % 2 == 0:
            i = l // 2
            xt = _dense_ffn(xt, mod, g2n, ffn_w1[i].astype(BF16), ffn_w3[i].astype(BF16),
                            ffn_w2[i].astype(BF16), seq)
        else:
            i = l // 2
            rw = jnp.pad(router_w[i], ((0, 0), (0, LANES - N_EXPERTS)))
            rw_hi = rw.astype(BF16)
            rw_pad = jnp.concatenate([rw_hi, (rw - rw_hi.astype(F32)).astype(BF16)], axis=1)
            rb_pad = jnp.pad(router_b[i], (0, LANES - N_EXPERTS), constant_values=-1e30)[None, :]
            hs, ei, pw, meta, tot = _router(xt, mod, g2n, rw_pad, rb_pad, seq)
            max_rows = n * TOP_K + N_EXPERTS * (n // TM_ROUTE) * (ROW_ALIGN - 1)
            n_tiles = -(-max_rows // TG_MOE) + N_EXPERTS
            seg_grouped, seg_loff, seg_cnt, gaps, tile_e, used = _moe_layout(meta, tot, n_tiles)
            tile_rows = jnp.sum(seg_cnt.reshape(-1, N_EXPERTS), axis=1)
            xs = _dispatch(seg_loff, seg_grouped, seg_cnt, tile_rows, gaps, hs, n_tiles)
            y = _expert_ffn(tile_e, used, xs, moe_w1[i], moe_w3[i], moe_w2[i])
            xt = _combine(seg_grouped, seg_loff, seg_cnt, tile_rows, xt, ei, pw, mod, y, seq)
    return xt.reshape(batch, seq, d)
```

```python
import functools

import jax
import jax.numpy as jnp
import numpy as np
from jax import lax
from jax.experimental import pallas as pl
from jax.experimental.pallas import tpu as pltpu

F32 = jnp.float32
BF16 = jnp.bfloat16
HIGHEST = lax.Precision.HIGHEST

HEAD_DIM = 64
N_GROUPS = 4
MIX_W = N_GROUPS * HEAD_DIM
N_BRANCH = 4
CONV_W = 3
Q_LORA = 256
KV_LORA = 128
QK_NOPE = 64
QK_ROPE = 32
QK_HEAD = QK_NOPE + QK_ROPE
V_HEAD = 64
CHUNK = 128
N_EXPERTS = 8
TOP_K = 2
ROPE_THETA = 10000.0
EPS = 1e-6
LOG2_E = 1.4426950408889634
MAX_STATIC_SHIFT = 50.0

LANES = 128
VMEM_LIMIT_BYTES = 56 * 1024 * 1024

COL_GATES = 0
COL_A = 4096
COL_CQ = COL_A + 3 * MIX_W
COL_R = 5120
COL_SU = 6144
COL_CKV = 6656
COL_KRA = 6784
N_IN = 6912

TM_PROJ = 512
TN_PROJ = 768
TM_PREP = 1024
TQ_ATT = 1024
TM_MIX = 512
TM_FFN = 512
TM_ROUTE = 512
ROW_ALIGN = 8
SORT_ROWS = TOP_K * TM_ROUTE + N_EXPERTS * ROW_ALIGN
TG_MOE = 512
DISPATCH_SLOTS = 4
DISPATCH_AHEAD = 2
W_CHUNK = 512
W_SLOTS = 8


def _cparams(*sem):
    return pltpu.CompilerParams(dimension_semantics=sem, vmem_limit_bytes=VMEM_LIMIT_BYTES)


def _sigmoid(x):
    return jnp.tanh(x * 0.5) * 0.5 + 0.5


def _group_mean(x, gmat_bf16):
    hi = x.astype(BF16)
    lo = (x - hi.astype(F32)).astype(BF16)
    return (jnp.dot(hi, gmat_bf16, preferred_element_type=F32)
            + jnp.dot(lo, gmat_bf16, preferred_element_type=F32))


def _pack_bf16_pair(lo, hi):
    lo_bits = lax.bitcast_convert_type(lo.astype(BF16).astype(F32), jnp.uint32)
    hi_bits = lax.bitcast_convert_type(hi.astype(BF16).astype(F32), jnp.uint32)
    return (lo_bits >> 16) | (hi_bits & jnp.uint32(0xFFFF0000))


def _unpack_bf16_pair(p):
    lo = lax.bitcast_convert_type(p << 16, F32)
    hi = lax.bitcast_convert_type(p & jnp.uint32(0xFFFF0000), F32)
    return lo, hi


def _norm_mod(x, g, shift, scale):
    y = x * lax.rsqrt(jnp.mean(x * x, axis=-1, keepdims=True) + EPS)
    return (y * g) * (1.0 + scale) + shift


def _ada_kernel(ct_ref, w_ref, b_ref, o_ref, *, batch):
    ct = ct_ref[...]
    cond = ct * _sigmoid(ct)
    w = w_ref[...]
    o_ref[...] = jnp.zeros(o_ref.shape, F32)
    for b in range(batch):
        o_ref[b:b + 1, :] = jnp.sum(w * cond[:, b:b + 1], axis=0, keepdims=True) + b_ref[...]


def _ada(c_t, ada_w, ada_b, batch):
    n_layer, d, d6 = ada_w.shape
    rows = c_t.shape[1]
    tn = 1024
    return pl.pallas_call(
        functools.partial(_ada_kernel, batch=batch),
        out_shape=jax.ShapeDtypeStruct((n_layer, rows, d6), F32),
        grid=(n_layer, d6 // tn),
        in_specs=[
            pl.BlockSpec((d, rows), lambda l, j: (0, 0)),
            pl.BlockSpec((None, d, tn), lambda l, j: (l, 0, j)),
            pl.BlockSpec((None, 1, tn), lambda l, j: (l, 0, j)),
        ],
        out_specs=pl.BlockSpec((None, rows, tn), lambda l, j: (l, 0, j)),
        compiler_params=_cparams("parallel", "parallel"),
        name="ada_mod",
    )(c_t, ada_w, ada_b.reshape(n_layer, 1, d6))


def _rope_kernel(pos_ref, inv_ref, cr_ref, sr_ref, cm_ref, sm_ref):
    half_r = HEAD_DIM // 2
    half_m = QK_ROPE // 2
    tm = pos_ref.shape[0]
    low = lax.broadcasted_iota(jnp.int32, (tm // 2, LANES), 1) < LANES // 2
    ang = jnp.where(low, pos_ref[0:tm // 2, :], pos_ref[tm // 2:tm, :]) * inv_ref[...]
    c = jnp.cos(ang)
    s = jnp.sin(ang)
    c = jnp.concatenate([c, pltpu.roll(c, LANES // 2, axis=1)], axis=0)
    s = jnp.concatenate([s, pltpu.roll(s, LANES // 2, axis=1)], axis=0)
    lane = lax.broadcasted_iota(jnp.int32, c.shape, 1)

    def tile_r(t):
        t = jnp.where(lane < half_r, t, 0.0)
        out = t
        for k in range(1, LANES // half_r):
            out = out + pltpu.roll(t, k * half_r, axis=1)
        return out

    cr_ref[...] = tile_r(c)
    sr_ref[...] = tile_r(s)
    first = jnp.logical_and(lane >= QK_NOPE, lane < QK_NOPE + half_m)
    second = jnp.logical_and(lane >= QK_NOPE + half_m, lane < QK_HEAD)
    c1, c2 = pltpu.roll(c, QK_NOPE - half_r, axis=1), pltpu.roll(c, QK_NOPE + half_m - half_r, axis=1)
    s1, s2 = pltpu.roll(s, QK_NOPE - half_r, axis=1), pltpu.roll(s, QK_NOPE + half_m - half_r, axis=1)
    cm_ref[...] = jnp.where(first, c1, jnp.where(second, c2, 1.0))
    sm_ref[...] = jnp.where(first, -s1, jnp.where(second, s2, 0.0))


def _rope_tables(pos_f):
    n = pos_f.shape[0]
    tm = 1024
    half_r = HEAD_DIM // 2
    half_m = QK_ROPE // 2
    inv_r = ROPE_THETA ** (-jnp.arange(half_r, dtype=F32) / half_r)
    inv_m = ROPE_THETA ** (-jnp.arange(half_m, dtype=F32) / half_m)
    inv = jnp.concatenate([inv_r, inv_m, jnp.zeros((LANES // 2 - half_r - half_m,), F32)])
    inv = jnp.tile(inv, 2)[None, :]
    tab = pl.BlockSpec((tm, LANES), lambda i: (i, 0))
    shape = jax.ShapeDtypeStruct((n, LANES), F32)
    return pl.pallas_call(
        _rope_kernel,
        out_shape=(shape, shape, shape, shape),
        grid=(n // tm,),
        in_specs=[pl.BlockSpec((tm, 1), lambda i: (i, 0)), pl.BlockSpec((1, LANES), lambda i: (0, 0))],
        out_specs=(tab, tab, tab, tab),
        compiler_params=_cparams("parallel"),
        name="rope_tables",
    )(pos_f, inv)


def _inproj_kernel(x_ref, mod_ref, g_ref, w_ref, o_ref):
    h = _norm_mod(x_ref[...], g_ref[...], mod_ref[0:1, :], mod_ref[1:2, :]).astype(BF16)
    for c in range(N_IN // TN_PROJ):
        cols = slice(c * TN_PROJ, (c + 1) * TN_PROJ)
        o_ref[:, cols] = jnp.dot(h, w_ref[:, cols], preferred_element_type=F32).astype(BF16)


def _resident(shape):
    return pl.BlockSpec(shape, lambda *_: (0,) * len(shape), pipeline_mode=pl.Buffered(1))


def _inproj(x, mod, g, w, seq):
    n, d = x.shape
    tm = TM_PROJ
    tpb = seq // tm
    return pl.pallas_call(
        _inproj_kernel,
        out_shape=jax.ShapeDtypeStruct((n, N_IN), BF16),
        grid=(n // tm,),
        in_specs=[
            pl.BlockSpec((tm, d), lambda i: (i, 0)),
            pl.BlockSpec((None, 6, d), lambda i: (i // tpb, 0, 0)),
            pl.BlockSpec((1, d), lambda i: (0, 0)),
            _resident((d, N_IN)),
        ],
        out_specs=pl.BlockSpec((tm, N_IN), lambda i: (i, 0)),
        compiler_params=_cparams("parallel"),
        name="in_proj",
    )(x, mod, g, w)


def _mla_prep_kernel(cq_ref, ckv_ref, kra_ref, cm_ref, sm_ref, cqg_ref, wqa_ref, wqb_ref,
                     ckvg_ref, wk_ref, wv_ref, qga_ref, qgb_ref, kga_ref, kgb_ref,
                     qaug_ref, kaug_ref, vaug_ref, swap_ref, q_ref, k_ref, v_ref):
    cq = cq_ref[...].astype(F32)
    cqn = (cq * lax.rsqrt(jnp.mean(cq * cq, axis=-1, keepdims=True) + EPS) * cqg_ref[...]).astype(BF16)
    qa = jnp.dot(cqn, wqa_ref[...], preferred_element_type=F32)
    qb = jnp.dot(cqn, wqb_ref[...], preferred_element_type=F32)
    ckv = ckv_ref[...].astype(F32)
    ckvn = (ckv * lax.rsqrt(jnp.mean(ckv * ckv, axis=-1, keepdims=True) + EPS) * ckvg_ref[...]).astype(BF16)
    ka = jnp.dot(ckvn, wk_ref[...], preferred_element_type=F32)
    v_ref[...] = (jnp.dot(ckvn, wv_ref[...], preferred_element_type=F32) + vaug_ref[...]).astype(BF16)
    kra = kra_ref[...].astype(F32)
    krb = jnp.dot(kra_ref[...], swap_ref[...], preferred_element_type=F32)
    cm = cm_ref[...]
    sm = sm_ref[...]
    scale = QK_HEAD ** -0.5 * LOG2_E
    q_cos, q_sin = cm * (qga_ref[...] * scale), sm * (qgb_ref[...] * scale)
    k_cos, k_sin = cm * kga_ref[...], sm * kgb_ref[...]
    for h in range(N_GROUPS):
        sl = slice(h * LANES, (h + 1) * LANES)
        qah, qbh = qa[:, sl], qb[:, sl]
        r = lax.rsqrt(jnp.sum(qah * qah, axis=-1, keepdims=True) * (1.0 / QK_HEAD) + EPS)
        q_ref[:, sl] = ((qah * q_cos + qbh * q_sin) * r + qaug_ref[...]).astype(BF16)
        kah = ka[:, sl] + kra
        kbh = ka[:, sl] + krb
        r = lax.rsqrt(jnp.sum(kah * kah, axis=-1, keepdims=True) * (1.0 / QK_HEAD) + EPS)
        k_ref[:, sl] = ((kah * k_cos + kbh * k_sin) * r + kaug_ref[...]).astype(BF16)


def _mla_prep(proj, cm, sm, p):
    n = proj.shape[0]
    tm = TM_PREP
    hw = N_GROUPS * LANES

    def col(width, offset):
        return pl.BlockSpec((tm, width), lambda i: (i, offset // width))

    def full(a):
        return pl.BlockSpec(a.shape, lambda i: (0,) * a.ndim)

    weights = [p["cq_g"], p["wqa"], p["wqb"], p["ckv_g"], p["wk"], p["wv"],
               p["qga"], p["qgb"], p["kga"], p["kgb"], p["qaug"], p["kaug"], p["vaug"], p["swap"]]
    head_tile = pl.BlockSpec((tm, hw), lambda i: (i, 0))
    out = jax.ShapeDtypeStruct((n, hw), BF16)
    return pl.pallas_call(
        _mla_prep_kernel,
        out_shape=(out, out, out),
        grid=(n // tm,),
        in_specs=[col(Q_LORA, COL_CQ), col(KV_LORA, COL_CKV), col(LANES, COL_KRA),
                  pl.BlockSpec((tm, LANES), lambda i: (i, 0)), pl.BlockSpec((tm, LANES), lambda i: (i, 0))]
                 + [full(w) for w in weights],
        out_specs=(head_tile, head_tile, head_tile),
        compiler_params=_cparams("parallel"),
        name="mla_prep",
    )(proj, proj, proj, cm, sm, *weights)


def _flash_kernel(qi_ref, kj_ref, q_ref, k_ref, v_ref, o_ref, acc_scr, *rest, tq, online_max):
    i = qi_ref[pl.program_id(1)]
    j = kj_ref[pl.program_id(1)]

    @pl.when(j == 0)
    def _():
        acc_scr[...] = jnp.zeros(acc_scr.shape, F32)
        if online_max:
            rest[0][...] = jnp.full(rest[0].shape, -jnp.inf, F32)

    def block(q0, nq, nk, masked):
        rows = slice(q0, q0 + nq)
        if masked:
            row = lax.broadcasted_iota(jnp.int32, (nq, nk), 0) + q0
            col = lax.broadcasted_iota(jnp.int32, (nq, nk), 1)
            keep = col <= row
        for h in range(N_GROUPS):
            sl = slice(h * LANES, (h + 1) * LANES)
            s = lax.dot_general(q_ref[rows, sl], k_ref[0:nk, sl], (((1,), (1,)), ((), ())),
                                preferred_element_type=F32)
            if masked:
                s = jnp.where(keep, s, -jnp.inf)
            if online_max:
                m_scr = rest[0]
                m_prev = m_scr[h, rows]
                m_new = jnp.maximum(m_prev, jnp.max(s, axis=-1, keepdims=True))
                p = jnp.exp2(s - m_new).astype(BF16)
                acc_scr[h, rows] = jnp.exp2(m_prev - m_new) * acc_scr[h, rows] + jnp.dot(
                    p, v_ref[0:nk, sl], preferred_element_type=F32)
                m_scr[h, rows] = m_new
            else:
                acc_scr[h, rows] += jnp.dot(jnp.exp2(s).astype(BF16), v_ref[0:nk, sl],
                                            preferred_element_type=F32)

    @pl.when(j < i)
    def _():
        block(0, tq, tq, False)

    @pl.when(j == i)
    def _():
        block(0, tq // 2, tq // 2, True)
        block(tq // 2, tq // 2, tq, True)
        lane = lax.broadcasted_iota(jnp.int32, (tq, LANES), 1)
        for pr in range(N_GROUPS // 2):
            lo = acc_scr[2 * pr]
            hi = acc_scr[2 * pr + 1]
            lo = lo / lo[:, V_HEAD:V_HEAD + 1]
            hi = hi / hi[:, V_HEAD:V_HEAD + 1]
            both = jnp.where(lane < V_HEAD, lo, pltpu.roll(hi, V_HEAD, axis=1))
            o_ref[:, pr * LANES:(pr + 1) * LANES] = both.astype(BF16)


def _flash(q, k, v, batch, seq, online_max):
    n = q.shape[0]
    tq = TQ_ATT
    nq = seq // tq
    hw = N_GROUPS * LANES
    scratch = [pltpu.VMEM((N_GROUPS, tq, LANES), F32)]
    if online_max:
        scratch.append(pltpu.VMEM((N_GROUPS, tq, 1), F32))
    pairs = [(i, j) for i in range(nq) for j in range(i + 1)]
    qi = jnp.asarray([p[0] for p in pairs], jnp.int32)
    kj = jnp.asarray([p[1] for p in pairs], jnp.int32)
    q_tile = lambda b, s, qi, kj: (b * nq + qi[s], 0)
    k_tile = lambda b, s, qi, kj: (b * nq + kj[s], 0)
    return pl.pallas_call(
        functools.partial(_flash_kernel, tq=tq, online_max=online_max),
        out_shape=jax.ShapeDtypeStruct((n, MIX_W), BF16),
        grid_spec=pltpu.PrefetchScalarGridSpec(
            num_scalar_prefetch=2,
            grid=(batch, len(pairs)),
            in_specs=[pl.BlockSpec((tq, hw), q_tile), pl.BlockSpec((tq, hw), k_tile),
                      pl.BlockSpec((tq, hw), k_tile)],
            out_specs=pl.BlockSpec((tq, MIX_W), q_tile),
            scratch_shapes=scratch,
        ),
        compiler_params=_cparams("parallel", "arbitrary"),
        name="mla_flash_online" if online_max else "mla_flash",
    )(qi, kj, q, k, v)


def _gelu_tanh(x):
    return jax.nn.gelu(x, approximate=True)


def _mix_kernel(gates_ref, a_ref, r_ref, su_ref, ymla_ref, x_ref, cos_ref, sin_ref, mod_ref,
                convw_ref, gvg_ref, wscat_ref, bsmat_ref, retg_ref, dec_ref, kdec_ref, qdec_ref,
                cdec_ref, bd_ref, gmat_ref, mk_ref, mv_ref, wb_ref, wo_ref,
                o_ref, carry_scr, state_scr, ysg_scr, yret_scr, *, tm, tpb):
    i = pl.program_id(0)

    @pl.when(i % tpb == 0)
    def _():
        carry_scr[...] = jnp.zeros(carry_scr.shape, F32)
        state_scr[...] = jnp.zeros(state_scr.shape, F32)

    w = MIX_W
    a_b = a_ref[:, 0:w].astype(F32)
    u = a_ref[:, w:2 * w].astype(F32) * a_ref[:, 2 * w:3 * w].astype(F32)
    rowi = lax.broadcasted_iota(jnp.int32, (tm, w), 0)
    prev1 = carry_scr[0:1, :]
    prev2 = carry_scr[1:2, :]
    u1 = jnp.where(rowi == 0, prev1, pltpu.roll(u, 1, axis=0))
    u2 = jnp.where(rowi == 0, prev2, jnp.where(rowi == 1, prev1, pltpu.roll(u, 2, axis=0)))
    carry_scr[0:1, :] = u[tm - 1:tm, :]
    carry_scr[1:2, :] = u[tm - 2:tm - 1, :]
    y_conv = a_b * (convw_ref[0:1, :] * u2 + convw_ref[1:2, :] * u1 + convw_ref[2:3, :] * u)

    gmat = gmat_ref[...]
    s_u = _gelu_tanh(su_ref[:, 0:w].astype(F32))
    s_v = _gelu_tanh(su_ref[:, w:2 * w].astype(F32))
    ms = _group_mean(s_v * s_v, gmat)
    vn = (s_v * lax.rsqrt(ms + EPS) * gvg_ref[...]).astype(BF16)

    cosr = cos_ref[...]
    sinr = sin_ref[...]

    def rot(t):
        t1, t2 = t[:, 0:LANES], t[:, LANES:2 * LANES]
        return jnp.concatenate([t1 * cosr - t2 * sinr, t2 * cosr + t1 * sinr], axis=-1)

    rq = rot(r_ref[:, 0:w].astype(F32))
    rk = rot(r_ref[:, w:2 * w].astype(F32)) * (HEAD_DIM ** -0.5)

    for c in range(tm // CHUNK):
        rows = slice(c * CHUNK, (c + 1) * CHUNK)
        vc = vn[rows, :]
        vbd = jnp.concatenate([vc * mv_ref[g:g + 1, :].astype(BF16) for g in range(N_GROUPS)], axis=0)
        mixed = jnp.dot(wscat_ref[...], vbd, preferred_element_type=F32) + bsmat_ref[...]
        ysg_scr[rows, :] = s_u[rows, :] * mixed

        qc = rq[rows, :]
        kc = rk[rows, :]
        kcb = kc.astype(BF16)
        vcb = r_ref[rows, 2 * w:3 * w]
        qstack = jnp.concatenate([(qc * mk_ref[h:h + 1, :]).astype(BF16) for h in range(N_GROUPS)], axis=0)
        sc = lax.dot_general(qstack, kcb, (((1,), (1,)), ((), ())), preferred_element_type=F32)
        sc = (sc * dec_ref[...]).astype(BF16)
        scat = jnp.concatenate([sc[h * CHUNK:(h + 1) * CHUNK, :] for h in range(N_GROUPS)], axis=1)
        vstack = jnp.concatenate([vcb * mv_ref[h:h + 1, :].astype(BF16) for h in range(N_GROUPS)], axis=0)
        o_c = jnp.dot(scat, vstack, preferred_element_type=F32)
        state = state_scr[...]
        o_c = o_c + jnp.dot((qc * qdec_ref[...]).astype(BF16), state.astype(BF16),
                            preferred_element_type=F32)
        kd_t = jnp.transpose(kc * kdec_ref[...]).astype(BF16)
        kv = jnp.dot(kd_t, vcb, preferred_element_type=F32)
        state_scr[...] = state * cdec_ref[...] + kv * bd_ref[...]
        yret_scr[rows, :] = o_c

    o_all = yret_scr[...]
    xc = o_all - _group_mean(o_all, gmat)
    var = _group_mean(xc * xc, gmat)
    r_g = r_ref[:, 3 * w:4 * w].astype(F32)
    y_ret = (r_g * _sigmoid(r_g)) * (xc * lax.rsqrt(var + EPS) * retg_ref[...])

    d = x_ref.shape[1]
    ys = (y_conv, ymla_ref[...], ysg_scr[...], y_ret)
    merged = None
    for n in range(N_BRANCH):
        gate = _sigmoid(gates_ref[:, n * d:(n + 1) * d])
        term = gate * jnp.dot(ys[n].astype(BF16), wb_ref[n], preferred_element_type=F32).astype(BF16)
        merged = term if merged is None else merged + term
    out = jnp.dot(merged, wo_ref[...], preferred_element_type=F32)
    o_ref[...] = x_ref[...] + mod_ref[2:3, :] * out


def _mixers(proj, ymla, x, cosr, sinr, mod, p, seq):
    n, d = x.shape
    tm = TM_MIX
    tpb = seq // tm

    def col(width, offset):
        return pl.BlockSpec((tm, width), lambda i: (i, offset // width))

    def full(a):
        return pl.BlockSpec(a.shape, lambda i: (0,) * a.ndim)

    consts = [p["conv_w"], p["gv_g"], p["ws_cat"], p["bs_mat"], p["ret_g"], p["dec"], p["kdec"],
              p["qdec"], p["cdec"], p["bd"], p["gmat"], p["mk"], p["mv"], p["w_branch"], p["w_o"]]
    return pl.pallas_call(
        functools.partial(_mix_kernel, tm=tm, tpb=tpb),
        out_shape=jax.ShapeDtypeStruct((n, d), F32),
        grid=(n // tm,),
        in_specs=[col(N_BRANCH * d, COL_GATES), col(4 * MIX_W, COL_A), col(4 * MIX_W, COL_R),
                  col(2 * MIX_W, COL_SU),
                  pl.BlockSpec((tm, MIX_W), lambda i: (i, 0)),
                  pl.BlockSpec((tm, d), lambda i: (i, 0)),
                  pl.BlockSpec((tm, LANES), lambda i: (i, 0)),
                  pl.BlockSpec((tm, LANES), lambda i: (i, 0)),
                  pl.BlockSpec((None, 6, d), lambda i: (i // tpb, 0, 0))]
                 + [full(c) for c in consts],
        out_specs=pl.BlockSpec((tm, d), lambda i: (i, 0)),
        scratch_shapes=[pltpu.VMEM((8, MIX_W), F32), pltpu.VMEM((MIX_W, MIX_W), F32),
                        pltpu.VMEM((tm, MIX_W), F32), pltpu.VMEM((tm, MIX_W), F32)],
        compiler_params=_cparams("arbitrary"),
        name="mixers_merge",
    )(proj, proj, proj, proj, ymla, x, cosr, sinr, mod, *consts)


def _ffn_kernel(x_ref, mod_ref, g_ref, w1_ref, w3_ref, w2_ref, o_ref):
    x = x_ref[...]
    h = _norm_mod(x, g_ref[...], mod_ref[3:4, :], mod_ref[4:5, :]).astype(BF16)
    a = jnp.dot(h, w1_ref[...], preferred_element_type=F32)
    b = jnp.dot(h, w3_ref[...], preferred_element_type=F32)
    hid = ((a * _sigmoid(a)) * b).astype(BF16)
    o_ref[...] = x + mod_ref[5:6, :] * jnp.dot(hid, w2_ref[...], preferred_element_type=F32)


def _dense_ffn(x, mod, g, w1, w3, w2, seq):
    n, d = x.shape
    dff = w1.shape[1]
    tm = TM_FFN
    tpb = seq // tm
    return pl.pallas_call(
        _ffn_kernel,
        out_shape=jax.ShapeDtypeStruct((n, d), F32),
        grid=(n // tm,),
        in_specs=[
            pl.BlockSpec((tm, d), lambda i: (i, 0)),
            pl.BlockSpec((None, 6, d), lambda i: (i // tpb, 0, 0)),
            pl.BlockSpec((1, d), lambda i: (0, 0)),
            _resident((d, dff)), _resident((d, dff)), _resident((dff, d)),
        ],
        out_specs=pl.BlockSpec((tm, d), lambda i: (i, 0)),
        compiler_params=_cparams("parallel"),
        name="dense_swiglu",
    )(x, mod, g, w1, w3, w2)


def _router_kernel(x_ref, mod_ref, g_ref, rw_ref, rb_ref, hs_ref, ei_ref, pw_ref, meta_ref, tot_ref,
                   carry_scr, *, tm, srows):
    i = pl.program_id(0)

    @pl.when(i == 0)
    def _():
        carry_scr[...] = jnp.zeros(carry_scr.shape, F32)

    h = _norm_mod(x_ref[...], g_ref[...], mod_ref[3:4, :], mod_ref[4:5, :])

    h_hi = h.astype(BF16)
    h_lo = (h - h_hi.astype(F32)).astype(BF16)
    hw = jnp.dot(h_hi, rw_ref[...], preferred_element_type=F32)
    logits = (hw[:, :LANES] + hw[:, LANES:] + jnp.dot(h_lo, rw_ref[:, :LANES], preferred_element_type=F32)
              + rb_ref[...])
    mx = jnp.max(logits, axis=-1, keepdims=True)
    ex = jnp.exp(logits - mx)
    probs = ex / jnp.sum(ex, axis=-1, keepdims=True)
    lane = lax.broadcasted_iota(jnp.int32, (tm, LANES), 1)
    valid = lane < N_EXPERTS
    probs = jnp.where(valid, probs, -1.0)
    m1 = jnp.max(probs, axis=-1, keepdims=True)
    i1 = jnp.min(jnp.where(probs == m1, lane, LANES), axis=-1, keepdims=True)
    rest = jnp.where(lane == i1, -1.0, probs)
    m2 = jnp.max(rest, axis=-1, keepdims=True)
    i2 = jnp.min(jnp.where(rest == m2, lane, LANES), axis=-1, keepdims=True)
    den = m1 + m2
    pw_ref[...] = jnp.where(lane == 0, m1 / den, jnp.where(lane == 1, m2 / den, 0.0))

    sel1 = lane == i1
    sel2 = lane == i2
    onehot = jnp.where(sel1, 1.0, 0.0) + jnp.where(sel2, 1.0, 0.0)
    r_i = lax.broadcasted_iota(jnp.int32, (tm, tm), 0)
    c_i = lax.broadcasted_iota(jnp.int32, (tm, tm), 1)
    tri = jnp.where(c_i < r_i, 1.0, 0.0).astype(BF16)
    before = jnp.dot(tri, onehot.astype(BF16), preferred_element_type=F32)
    cnt = jnp.sum(onehot, axis=0, keepdims=True)
    cnt_al = jnp.floor((cnt + (ROW_ALIGN - 1)) * (1.0 / ROW_ALIGN)) * ROW_ALIGN
    e_r = lax.broadcasted_iota(jnp.int32, (LANES, LANES), 0)
    e_c = lax.broadcasted_iota(jnp.int32, (LANES, LANES), 1)
    upper = jnp.where(e_r < e_c, 1.0, 0.0)
    loff = jnp.dot(jnp.broadcast_to(cnt_al, (8, LANES)), upper, precision=HIGHEST,
                   preferred_element_type=F32)[0:1, :]
    slot = loff + before
    slot1 = jnp.sum(jnp.where(sel1, slot, 0.0), axis=-1, keepdims=True).astype(jnp.int32)
    slot2 = jnp.sum(jnp.where(sel2, slot, 0.0), axis=-1, keepdims=True).astype(jnp.int32)
    ei = jnp.where(lane == 0, i1, jnp.where(lane == 1, i2, 0))
    ei_ref[...] = jnp.where(lane == 2, slot1, jnp.where(lane == 3, slot2, ei))

    r_idx = lax.broadcasted_iota(jnp.int32, (tm, srows), 1)
    place = jnp.where(r_idx == slot1, 1.0, jnp.where(r_idx == slot2, 1.0, 0.0)).astype(BF16)
    hs = lax.dot_general(place, h.astype(BF16), (((0,), (0,)), ((), ())), preferred_element_type=F32)
    half = hs.shape[1] // 2
    hs_ref[...] = _pack_bf16_pair(hs[:, :half], hs[:, half:])

    carry = carry_scr[0:1, :]
    mrow = lax.broadcasted_iota(jnp.int32, (8, LANES), 0)
    meta = jnp.where(mrow == 0, cnt_al, jnp.where(mrow == 1, carry, jnp.where(mrow == 2, loff, 0.0)))
    meta_ref[...] = meta.astype(jnp.int32)
    carry_scr[0:1, :] = carry + cnt_al
    tot_ref[...] = jnp.broadcast_to(carry + cnt_al, tot_ref.shape).astype(jnp.int32)


def _router(x, mod, g, rw_pad, rb_pad, seq):
    n, d = x.shape
    tm = TM_ROUTE
    tpb = seq // tm
    nt = n // tm
    return pl.pallas_call(
        functools.partial(_router_kernel, tm=tm, srows=SORT_ROWS),
        out_shape=(jax.ShapeDtypeStruct((nt * SORT_ROWS, d // 2), jnp.uint32),
                   jax.ShapeDtypeStruct((n, LANES), jnp.int32),
                   jax.ShapeDtypeStruct((n, LANES), F32),
                   jax.ShapeDtypeStruct((nt, 8, LANES), jnp.int32),
                   jax.ShapeDtypeStruct((8, LANES), jnp.int32)),
        grid=(nt,),
        in_specs=[
            pl.BlockSpec((tm, d), lambda i: (i, 0)),
            pl.BlockSpec((None, 6, d), lambda i: (i // tpb, 0, 0)),
            pl.BlockSpec((1, d), lambda i: (0, 0)),
            pl.BlockSpec((d, 2 * LANES), lambda i: (0, 0)),
            pl.BlockSpec((1, LANES), lambda i: (0, 0)),
        ],
        out_specs=(pl.BlockSpec((SORT_ROWS, d // 2), lambda i: (i, 0)),
                   pl.BlockSpec((tm, LANES), lambda i: (i, 0)),
                   pl.BlockSpec((tm, LANES), lambda i: (i, 0)),
                   pl.BlockSpec((None, 8, LANES), lambda i: (i, 0, 0)),
                   pl.BlockSpec((8, LANES), lambda i: (0, 0))),
        scratch_shapes=[pltpu.VMEM((8, LANES), F32)],
        compiler_params=_cparams("arbitrary"),
        name="router_top2",
    )(x, mod, g, rw_pad, rb_pad)


def _segment_copy(src_hbm, dst_hbm, src_row, dst_row, n_rows, sem):
    src_row = pl.multiple_of(src_row, ROW_ALIGN)
    dst_row = pl.multiple_of(dst_row, ROW_ALIGN)
    n_rows = pl.multiple_of(n_rows, ROW_ALIGN)
    return pltpu.make_async_copy(src_hbm.at[pl.ds(src_row, n_rows)], dst_hbm.at[pl.ds(dst_row, n_rows)], sem)


def _dispatch_kernel(loff_ref, dst_ref, cnt_ref, rows_ref, gap_ref, hs_ref, xs_ref, buf, zbuf, sem_in, sem_out,
                     sem_zero, *, n_tiles, srows, n_out_tiles):
    zbuf[...] = jnp.zeros(zbuf.shape, zbuf.dtype)
    tg = zbuf.shape[0]
    first_unused = gap_ref[2 * N_EXPERTS]

    def zero_gap(e):
        return _segment_copy(zbuf, xs_ref, 0, gap_ref[2 * e], gap_ref[2 * e + 1], sem_zero)

    def zero_tile(t):
        return pltpu.make_async_copy(zbuf, xs_ref.at[pl.ds(pl.multiple_of(t * tg, ROW_ALIGN), tg)], sem_zero)

    def for_each_zero_copy(action):
        for e in range(N_EXPERTS):
            @pl.when(gap_ref[2 * e + 1] > 0)
            def _():
                action(zero_gap(e))

        def tail(t, carry):
            action(zero_tile(t))
            return carry

        lax.fori_loop(first_unused, n_out_tiles, tail, 0)

    for_each_zero_copy(lambda copy: copy.start())

    def fetch(t):
        slot = t % DISPATCH_SLOTS
        return pltpu.make_async_copy(hs_ref.at[pl.ds(pl.multiple_of(t * srows, ROW_ALIGN), srows)],
                                     buf.at[slot], sem_in.at[slot])

    def drain(t):
        slot = t % DISPATCH_SLOTS
        n_rows = rows_ref[t]

        @pl.when(n_rows > 0)
        def _():
            _segment_copy(buf.at[slot], xs_ref, 0, 0, n_rows, sem_out.at[slot]).wait()

    for t in range(DISPATCH_AHEAD):
        fetch(t).start()

    def body(t, carry):
        slot = t % DISPATCH_SLOTS
        fetch(t).wait()
        for e in range(N_EXPERTS):
            s = t * N_EXPERTS + e
            n_rows = cnt_ref[s]

            @pl.when(n_rows > 0)
            def _():
                _segment_copy(buf.at[slot], xs_ref, loff_ref[s], dst_ref[s], n_rows, sem_out.at[slot]).start()

        @pl.when(t + DISPATCH_AHEAD < n_tiles)
        def _():
            @pl.when(t + DISPATCH_AHEAD >= DISPATCH_SLOTS)
            def _():
                drain(t + DISPATCH_AHEAD - DISPATCH_SLOTS)

            fetch(t + DISPATCH_AHEAD).start()

        return carry

    lax.fori_loop(0, n_tiles, body, 0)
    for t in range(max(n_tiles - DISPATCH_SLOTS, 0), n_tiles):
        drain(t)
    for_each_zero_copy(lambda copy: copy.wait())


def _dispatch(seg_loff, seg_dst, seg_cnt, tile_rows, gaps, hs, n_out_tiles):
    n_tiles = tile_rows.shape[0]
    srows = hs.shape[0] // n_tiles
    assert n_tiles >= DISPATCH_SLOTS
    return pl.pallas_call(
        functools.partial(_dispatch_kernel, n_tiles=n_tiles, srows=srows, n_out_tiles=n_out_tiles),
        out_shape=jax.ShapeDtypeStruct((n_out_tiles * TG_MOE, hs.shape[1]), hs.dtype),
        grid_spec=pltpu.PrefetchScalarGridSpec(
            num_scalar_prefetch=5,
            grid=(1,),
            in_specs=[pl.BlockSpec(memory_space=pl.ANY)],
            out_specs=pl.BlockSpec(memory_space=pl.ANY),
            scratch_shapes=[pltpu.VMEM((DISPATCH_SLOTS, srows, hs.shape[1]), hs.dtype),
                            pltpu.VMEM((TG_MOE, hs.shape[1]), hs.dtype),
                            pltpu.SemaphoreType.DMA((DISPATCH_SLOTS,)),
                            pltpu.SemaphoreType.DMA((DISPATCH_SLOTS,)),
                            pltpu.SemaphoreType.DMA],
        ),
        compiler_params=_cparams("arbitrary"),
        name="moe_dispatch",
    )(seg_loff, seg_dst, seg_cnt, tile_rows, gaps, hs)


def _expert_kernel(te_ref, used_ref, xs_ref, w1_hbm, w3_hbm, w2_hbm, y_ref, wb1, wb3, wb2, stage, sem):
    t = pl.program_id(0)
    e = te_ref[t]
    first_of_expert = jnp.logical_or(t == 0, e != te_ref[jnp.maximum(t - 1, 0)])

    @pl.when(jnp.logical_and(used_ref[t] == 1, first_of_expert))
    def _():
        n_slots = stage.shape[0]
        windows = [(src, dst, r, c) for src, dst in ((w1_hbm, wb1), (w3_hbm, wb3), (w2_hbm, wb2))
                   for r in range(dst.shape[0] // W_CHUNK) for c in range(dst.shape[1] // W_CHUNK)]

        def staged_copy(k):
            src, _, r, c = windows[k]
            return pltpu.make_async_copy(src.at[e, pl.ds(r * W_CHUNK, W_CHUNK), pl.ds(c * W_CHUNK, W_CHUNK)],
                                         stage.at[k % n_slots], sem.at[k % n_slots])

        for k in range(n_slots - 1):
            staged_copy(k).start()
        for k, (_, dst, r, c) in enumerate(windows):
            staged_copy(k).wait()
            dst[r * W_CHUNK:(r + 1) * W_CHUNK, c * W_CHUNK:(c + 1) * W_CHUNK] = stage[k % n_slots].astype(BF16)
            if k + n_slots - 1 < len(windows):
                staged_copy(k + n_slots - 1).start()

    @pl.when(used_ref[t] == 1)
    def _():
        lo, hi = _unpack_bf16_pair(xs_ref[...])
        h = jnp.concatenate([lo.astype(BF16), hi.astype(BF16)], axis=1)
        a = jnp.dot(h, wb1[...], preferred_element_type=F32)
        b = jnp.dot(h, wb3[...], preferred_element_type=F32)
        hid = ((a * _sigmoid(a)) * b).astype(BF16)
        acc = jnp.dot(hid, wb2[...], preferred_element_type=F32)
        half = acc.shape[1] // 2
        y_ref[...] = _pack_bf16_pair(acc[:, :half], acc[:, half:])

    @pl.when(used_ref[t] == 0)
    def _():
        y_ref[...] = jnp.zeros(y_ref.shape, y_ref.dtype)


def _expert_ffn(tile_e, tile_used, xs, w1, w3, w2):
    rows, half = xs.shape
    d = 2 * half
    dff = w1.shape[2]
    tg = TG_MOE
    assert dff % W_CHUNK == 0
    hbm = pl.BlockSpec(memory_space=pl.ANY)
    return pl.pallas_call(
        _expert_kernel,
        out_shape=jax.ShapeDtypeStruct((rows, half), jnp.uint32),
        grid_spec=pltpu.PrefetchScalarGridSpec(
            num_scalar_prefetch=2,
            grid=(rows // tg,),
            in_specs=[pl.BlockSpec((tg, half), lambda t, te, us: (t, 0)), hbm, hbm, hbm],
            out_specs=pl.BlockSpec((tg, half), lambda t, te, us: (t, 0)),
            scratch_shapes=[pltpu.VMEM((d, dff), BF16), pltpu.VMEM((d, dff), BF16), pltpu.VMEM((dff, d), BF16),
                            pltpu.VMEM((W_SLOTS, W_CHUNK, W_CHUNK), F32), pltpu.SemaphoreType.DMA((W_SLOTS,))],
        ),
        compiler_params=_cparams("arbitrary"),
        name="expert_swiglu",
    )(tile_e, tile_used, xs, w1, w3, w2)


def _combine_kernel(src_ref, loff_ref, cnt_ref, rows_ref, x_ref, ei_ref, pw_ref, mod_ref, y_ref, o_ref,
                    ybuf, sem, *, tm, srows):
    i = pl.program_id(0)
    slot = i % 2

    def fetch(tile, into):
        ybuf[into] = jnp.zeros(ybuf.shape[1:], ybuf.dtype)
        for e in range(N_EXPERTS):
            s = tile * N_EXPERTS + e
            n_rows = cnt_ref[s]

            @pl.when(n_rows > 0)
            def _():
                _segment_copy(y_ref, ybuf.at[into], src_ref[s], loff_ref[s], n_rows, sem.at[into]).start()

    @pl.when(i == 0)
    def _():
        fetch(i, slot)

    @pl.when(i + 1 < pl.num_programs(0))
    def _():
        fetch(i + 1, 1 - slot)

    @pl.when(rows_ref[i] > 0)
    def _():
        _segment_copy(y_ref, ybuf.at[slot], 0, 0, rows_ref[i], sem.at[slot]).wait()

    lo, hi = _unpack_bf16_pair(ybuf[slot])
    ys = jnp.concatenate([lo.astype(BF16), hi.astype(BF16)], axis=1)
    r_idx = lax.broadcasted_iota(jnp.int32, (tm, srows), 1)
    mix = jnp.zeros(x_ref.shape, F32)
    for k in range(TOP_K):
        pick = jnp.where(r_idx == ei_ref[:, TOP_K + k:TOP_K + k + 1], 1.0, 0.0).astype(BF16)
        mix = mix + pw_ref[:, k:k + 1] * jnp.dot(pick, ys, preferred_element_type=F32)
    o_ref[...] = x_ref[...] + mod_ref[5:6, :] * mix


def _combine(seg_src, seg_loff, seg_cnt, tile_rows, x, ei, pw, mod, y, seq):
    n, d = x.shape
    tm = TM_ROUTE
    tpb = seq // tm
    tok = lambda width: pl.BlockSpec((tm, width), lambda i, *_: (i, 0))
    return pl.pallas_call(
        functools.partial(_combine_kernel, tm=tm, srows=SORT_ROWS),
        out_shape=jax.ShapeDtypeStruct((n, d), F32),
        grid_spec=pltpu.PrefetchScalarGridSpec(
            num_scalar_prefetch=4,
            grid=(n // tm,),
            in_specs=[tok(d), tok(LANES), tok(LANES),
                      pl.BlockSpec((None, 6, d), lambda i, *_: (i // tpb, 0, 0)),
                      pl.BlockSpec(memory_space=pl.ANY)],
            out_specs=tok(d),
            scratch_shapes=[pltpu.VMEM((2, SORT_ROWS, d // 2), jnp.uint32), pltpu.SemaphoreType.DMA((2,))],
        ),
        compiler_params=_cparams("arbitrary"),
        name="moe_combine",
    )(seg_src, seg_loff, seg_cnt, tile_rows, x, ei, pw, mod, y)


def _pack_w_in(w_in):
    d = w_in.shape[0]
    w = MIX_W
    o_ckv = 3 * w + Q_LORA
    o_kr = o_ckv + KV_LORA
    o_su = o_kr + QK_ROPE
    o_rq = o_su + 2 * w
    o_gate = o_rq + 4 * w
    half = HEAD_DIM // 2
    perm = np.array([h * HEAD_DIM + part * half + i
                     for part in range(2) for h in range(N_GROUPS) for i in range(half)])
    kr = w_in[:, o_kr:o_kr + QK_ROPE]
    z = lambda k: jnp.zeros((d, k), w_in.dtype)
    cols = [
        w_in[:, o_gate:o_gate + N_BRANCH * d],
        w_in[:, 0:3 * w + Q_LORA],
        w_in[:, o_rq:o_rq + w][:, perm], w_in[:, o_rq + w:o_rq + 2 * w][:, perm],
        w_in[:, o_rq + 2 * w:o_rq + 4 * w],
        w_in[:, o_su:o_su + 2 * w],
        w_in[:, o_ckv:o_ckv + KV_LORA],
        z(QK_NOPE), kr, z(LANES - QK_HEAD),
    ]
    return jnp.concatenate(cols, axis=1).astype(BF16)


def _swap_rope_halves(a):
    hr = QK_ROPE // 2
    return jnp.concatenate([a[..., :QK_NOPE], a[..., QK_NOPE + hr:QK_HEAD], a[..., QK_NOPE:QK_NOPE + hr],
                            a[..., QK_HEAD:]], axis=-1)


def _mla_params(cq_g, w_uq, ckv_g, w_ukv, qn_g, kn_g):
    pad = LANES - QK_HEAD
    wq = w_uq.reshape(Q_LORA, N_GROUPS, QK_HEAD)
    wq = jnp.pad(wq, ((0, 0), (0, 0), (0, pad)))
    wkv = w_ukv.reshape(KV_LORA, N_GROUPS, QK_NOPE + V_HEAD)
    wk = jnp.pad(wkv[:, :, :QK_NOPE], ((0, 0), (0, 0), (0, LANES - QK_NOPE)))
    wv = jnp.pad(wkv[:, :, QK_NOPE:], ((0, 0), (0, 0), (0, LANES - V_HEAD)))
    qg = jnp.pad(qn_g, (0, pad))[None, :]
    kg = jnp.pad(kn_g, (0, pad))[None, :]
    bound = (QK_HEAD ** 0.5 * LOG2_E) * jnp.max(jnp.abs(qn_g)) * jnp.max(jnp.abs(kn_g))
    static_shift = bound <= MAX_STATIC_SHIFT
    lane = jnp.arange(LANES)
    qaug = (lane == QK_HEAD).astype(F32)[None, :]
    kaug = qaug * jnp.where(static_shift, -bound, 0.0)
    vaug = jnp.tile((lane == V_HEAD).astype(F32), N_GROUPS)[None, :]
    params = {
        "cq_g": cq_g[None, :], "ckv_g": ckv_g[None, :],
        "wqa": wq.reshape(Q_LORA, -1).astype(BF16),
        "wqb": _swap_rope_halves(wq).reshape(Q_LORA, -1).astype(BF16),
        "wk": wk.reshape(KV_LORA, -1).astype(BF16),
        "wv": wv.reshape(KV_LORA, -1).astype(BF16),
        "qga": qg, "qgb": _swap_rope_halves(qg), "kga": kg, "kgb": _swap_rope_halves(kg),
        "qaug": qaug, "kaug": kaug, "vaug": vaug,
        "swap": (_swap_rope_halves(lane[None, :])[0][None, :] == lane[:, None]).astype(BF16),
    }
    return params, static_shift


def _mixer_consts():
    h = jnp.arange(N_GROUPS, dtype=F32)
    log_gamma = jnp.log1p(-(2.0 ** (-5.0 - h)))
    pos = jnp.arange(CHUNK, dtype=F32)
    rel = pos[:, None] - pos[None, :]
    dec = jnp.where(rel >= 0, jnp.exp(log_gamma[:, None, None] * jnp.maximum(rel, 0.0)), 0.0)
    lane = np.arange(MIX_W)
    head_k = (lane % LANES) // (HEAD_DIM // 2)
    head_v = lane // HEAD_DIM
    lg_k = log_gamma[head_k]
    return {
        "dec": dec.reshape(N_GROUPS * CHUNK, CHUNK),
        "kdec": jnp.exp(lg_k[None, :] * (CHUNK - 1.0 - pos)[:, None]),
        "qdec": jnp.exp(lg_k[None, :] * (pos + 1.0)[:, None]),
        "cdec": jnp.broadcast_to(jnp.exp(lg_k * CHUNK)[:, None], (MIX_W, MIX_W)),
        "bd": jnp.asarray((head_k[:, None] == head_v[None, :]).astype(np.float32)),
        "gmat": jnp.asarray((head_v[:, None] == head_v[None, :]).astype(np.float32) / HEAD_DIM).astype(BF16),
        "mk": jnp.asarray((head_k[None, :] == np.arange(N_GROUPS)[:, None]).astype(np.float32)),
        "mv": jnp.asarray((head_v[None, :] == np.arange(N_GROUPS)[:, None]).astype(np.float32)),
    }


def _mixer_params(conv_w, gv_g, w_s, b_s, ret_g, w_branch, w_o):
    p = dict(_mixer_consts())
    ws = jnp.tril(w_s)
    p.update({
        "conv_w": conv_w,
        "gv_g": gv_g.reshape(1, MIX_W),
        "ws_cat": jnp.transpose(ws, (1, 0, 2)).reshape(CHUNK, N_GROUPS * CHUNK).astype(BF16),
        "bs_mat": jnp.repeat(b_s.T, HEAD_DIM, axis=1),
        "ret_g": ret_g.reshape(1, MIX_W),
        "w_branch": w_branch.astype(BF16),
        "w_o": w_o.astype(BF16),
    })
    return p


def _moe_layout(meta, tot, n_tiles):
    totals = tot[0, :N_EXPERTS]
    padded = ((totals + TG_MOE - 1) // TG_MOE) * TG_MOE
    ends = jnp.cumsum(padded)
    starts = ends - padded
    seg_cnt = meta[:, 0, :N_EXPERTS]
    seg_loff = meta[:, 2, :N_EXPERTS]
    seg_grouped = starts[None, :] + meta[:, 1, :N_EXPERTS]
    tile_start = jnp.arange(n_tiles, dtype=jnp.int32) * TG_MOE
    tile_e = jnp.sum((tile_start[:, None] >= ends[None, :]).astype(jnp.int32), axis=1)
    used = (tile_start < ends[-1]).astype(jnp.int32)
    last_e = jnp.sum((ends[-1] - 1 >= ends).astype(jnp.int32))
    tile_e = jnp.minimum(jnp.where(used == 1, tile_e, last_e), N_EXPERTS - 1)
    flat = lambda a: a.reshape(-1).astype(jnp.int32)
    gaps = jnp.concatenate([flat(jnp.stack([starts + totals, padded - totals], axis=1)),
                            flat(ends[-1:] // TG_MOE)])
    return flat(seg_grouped), flat(seg_loff), flat(seg_cnt), gaps, tile_e, used


def kernel(x, c, positions, norm1_g, norm2_g, ada_w, ada_b, w_in, conv_w, cq_g, w_uq, ckv_g, w_ukv, qn_g, kn_g, gv_g, w_s, b_s, ret_g, w_branch, w_o, ffn_w1, ffn_w3, ffn_w2, router_w, router_b, moe_w1, moe_w3, moe_w2):
    batch, seq, d = x.shape
    depth = ada_w.shape[0]
    n = batch * seq
    assert seq % max(TM_PROJ, TM_PREP, TQ_ATT, TM_MIX, TM_FFN, TM_ROUTE) == 0
    assert d // 2 % LANES == 0

    c_t = jnp.pad(c, ((0, 8 - batch), (0, 0))).T
    ada = _ada(c_t, ada_w, ada_b, batch)[:, :batch].reshape(depth, batch, 6, d)
    cosr, sinr, cm, sm = _rope_tables(positions.astype(F32).reshape(n, 1))

    xt = x.reshape(n, d)
    for l in range(depth):
        mod = ada[l]
        proj = _inproj(xt, mod, norm1_g[l][None, :], _pack_w_in(w_in[l]), seq)
        mla_p, static_shift = _mla_params(cq_g[l], w_uq[l], ckv_g[l], w_ukv[l], qn_g[l], kn_g[l])
        q, k, v = _mla_prep(proj, cm, sm, mla_p)
        y_mla = lax.cond(static_shift,
                         functools.partial(_flash, batch=batch, seq=seq, online_max=False),
                         functools.partial(_flash, batch=batch, seq=seq, online_max=True), q, k, v)
        mp = _mixer_params(conv_w[l], gv_g[l], w_s[l], b_s[l], ret_g[l], w_branch[l], w_o[l])
        xt = _mixers(proj, y_mla, xt, cosr, sinr, mod, mp, seq)
        g2n = norm2_g[l][None, :]
        if l % 2 == 0:
            i = l // 2
            xt = _dense_ffn(xt, mod, g2n, ffn_w1[i].astype(BF16), ffn_w3[i].astype(BF16),
                            ffn_w2[i].astype(BF16), seq)
        else:
            i = l // 2
            rw = jnp.pad(router_w[i], ((0, 0), (0, LANES - N_EXPERTS)))
            rw_hi = rw.astype(BF16)
            rw_pad = jnp.concatenate([rw_hi, (rw - rw_hi.astype(F32)).astype(BF16)], axis=1)
            rb_pad = jnp.pad(router_b[i], (0, LANES - N_EXPERTS), constant_values=-1e30)[None, :]
            hs, ei, pw, meta, tot = _router(xt, mod, g2n, rw_pad, rb_pad, seq)
            max_rows = n * TOP_K + N_EXPERTS * (n // TM_ROUTE) * (ROW_ALIGN - 1)
            n_tiles = -(-max_rows // TG_MOE) + N_EXPERTS
            seg_grouped, seg_loff, seg_cnt, gaps, tile_e, used = _moe_layout(meta, tot, n_tiles)
            tile_rows = jnp.sum(seg_cnt.reshape(-1, N_EXPERTS), axis=1)
            xs = _dispatch(seg_loff, seg_grouped, seg_cnt, tile_rows, gaps, hs, n_tiles)
            y = _expert_ffn(tile_e, used, xs, moe_w1[i], moe_w3[i], moe_w2[i])
            xt = _combine(seg_grouped, seg_loff, seg_cnt, tile_rows, xt, ei, pw, mod, y, seq)
    return xt.reshape(batch, seq, d)
```

```python
import functools

import jax
import jax.numpy as jnp
import numpy as np
from jax import lax
from jax.experimental import pallas as pl
from jax.experimental.pallas import tpu as pltpu

F32 = jnp.float32
BF16 = jnp.bfloat16
HIGHEST = lax.Precision.HIGHEST

HEAD_DIM = 64
N_GROUPS = 4
MIX_W = N_GROUPS * HEAD_DIM
N_BRANCH = 4
CONV_W = 3
Q_LORA = 256
KV_LORA = 128
QK_NOPE = 64
QK_ROPE = 32
QK_HEAD = QK_NOPE + QK_ROPE
V_HEAD = 64
CHUNK = 128
N_EXPERTS = 8
TOP_K = 2
ROPE_THETA = 10000.0
EPS = 1e-6
LOG2_E = 1.4426950408889634
MAX_STATIC_SHIFT = 50.0

LANES = 128
VMEM_LIMIT_BYTES = 56 * 1024 * 1024

COL_GATES = 0
COL_A = 4096
COL_CQ = COL_A + 3 * MIX_W
COL_R = 5120
COL_SU = 6144
COL_CKV = 6656
COL_KRA = 6784
N_IN = 6912

TM_PROJ = 512
TN_PROJ = 768
TM_PREP = 1024
TQ_ATT = 1024
TM_MIX = 512
TM_FFN = 512
TM_ROUTE = 512
ROW_ALIGN = 8
SORT_ROWS = TOP_K * TM_ROUTE + N_EXPERTS * ROW_ALIGN
TG_MOE = 512
DISPATCH_SLOTS = 4
DISPATCH_AHEAD = 2
W_CHUNK = 512
W_SLOTS = 8


def _cparams(*sem):
    return pltpu.CompilerParams(dimension_semantics=sem, vmem_limit_bytes=VMEM_LIMIT_BYTES)


def _sigmoid(x):
    return jnp.tanh(x * 0.5) * 0.5 + 0.5


def _group_mean(x, gmat_bf16):
    hi = x.astype(BF16)
    lo = (x - hi.astype(F32)).astype(BF16)
    return (jnp.dot(hi, gmat_bf16, preferred_element_type=F32)
            + jnp.dot(lo, gmat_bf16, preferred_element_type=F32))


def _pack_bf16_pair(lo, hi):
    lo_bits = lax.bitcast_convert_type(lo.astype(BF16).astype(F32), jnp.uint32)
    hi_bits = lax.bitcast_convert_type(hi.astype(BF16).astype(F32), jnp.uint32)
    return (lo_bits >> 16) | (hi_bits & jnp.uint32(0xFFFF0000))


def _unpack_bf16_pair(p):
    lo = lax.bitcast_convert_type(p << 16, F32)
    hi = lax.bitcast_convert_type(p & jnp.uint32(0xFFFF0000), F32)
    return lo, hi


def _norm_mod(x, g, shift, scale):
    y = x * lax.rsqrt(jnp.mean(x * x, axis=-1, keepdims=True) + EPS)
    return (y * g) * (1.0 + scale) + shift


def _ada_kernel(ct_ref, w_ref, b_ref, o_ref, *, batch):
    ct = ct_ref[...]
    cond = ct * _sigmoid(ct)
    w = w_ref[...]
    o_ref[...] = jnp.zeros(o_ref.shape, F32)
    for b in range(batch):
        o_ref[b:b + 1, :] = jnp.sum(w * cond[:, b:b + 1], axis=0, keepdims=True) + b_ref[...]


def _ada(c_t, ada_w, ada_b, batch):
    n_layer, d, d6 = ada_w.shape
    rows = c_t.shape[1]
    tn = 1024
    return pl.pallas_call(
        functools.partial(_ada_kernel, batch=batch),
        out_shape=jax.ShapeDtypeStruct((n_layer, rows, d6), F32),
        grid=(n_layer, d6 // tn),
        in_specs=[
            pl.BlockSpec((d, rows), lambda l, j: (0, 0)),
            pl.BlockSpec((None, d, tn), lambda l, j: (l, 0, j)),
            pl.BlockSpec((None, 1, tn), lambda l, j: (l, 0, j)),
        ],
        out_specs=pl.BlockSpec((None, rows, tn), lambda l, j: (l, 0, j)),
        compiler_params=_cparams("parallel", "parallel"),
        name="ada_mod",
    )(c_t, ada_w, ada_b.reshape(n_layer, 1, d6))


def _rope_kernel(pos_ref, inv_ref, cr_ref, sr_ref, cm_ref, sm_ref):
    half_r = HEAD_DIM // 2
    half_m = QK_ROPE // 2
    tm = pos_ref.shape[0]
    low = lax.broadcasted_iota(jnp.int32, (tm // 2, LANES), 1) < LANES // 2
    ang = jnp.where(low, pos_ref[0:tm // 2, :], pos_ref[tm // 2:tm, :]) * inv_ref[...]
    c = jnp.cos(ang)
    s = jnp.sin(ang)
    c = jnp.concatenate([c, pltpu.roll(c, LANES // 2, axis=1)], axis=0)
    s = jnp.concatenate([s, pltpu.roll(s, LANES // 2, axis=1)], axis=0)
    lane = lax.broadcasted_iota(jnp.int32, c.shape, 1)

    def tile_r(t):
        t = jnp.where(lane < half_r, t, 0.0)
        out = t
        for k in range(1, LANES // half_r):
            out = out + pltpu.roll(t, k * half_r, axis=1)
        return out

    cr_ref[...] = tile_r(c)
    sr_ref[...] = tile_r(s)
    first = jnp.logical_and(lane >= QK_NOPE, lane < QK_NOPE + half_m)
    second = jnp.logical_and(lane >= QK_NOPE + half_m, lane < QK_HEAD)
    c1, c2 = pltpu.roll(c, QK_NOPE - half_r, axis=1), pltpu.roll(c, QK_NOPE + half_m - half_r, axis=1)
    s1, s2 = pltpu.roll(s, QK_NOPE - half_r, axis=1), pltpu.roll(s, QK_NOPE + half_m - half_r, axis=1)
    cm_ref[...] = jnp.where(first, c1, jnp.where(second, c2, 1.0))
    sm_ref[...] = jnp.where(first, -s1, jnp.where(second, s2, 0.0))


def _rope_tables(pos_f):
    n = pos_f.shape[0]
    tm = 1024
    half_r = HEAD_DIM // 2
    half_m = QK_ROPE // 2
    inv_r = ROPE_THETA ** (-jnp.arange(half_r, dtype=F32) / half_r)
    inv_m = ROPE_THETA ** (-jnp.arange(half_m, dtype=F32) / half_m)
    inv = jnp.concatenate([inv_r, inv_m, jnp.zeros((LANES // 2 - half_r - half_m,), F32)])
    inv = jnp.tile(inv, 2)[None, :]
    tab = pl.BlockSpec((tm, LANES), lambda i: (i, 0))
    shape = jax.ShapeDtypeStruct((n, LANES), F32)
    return pl.pallas_call(
        _rope_kernel,
        out_shape=(shape, shape, shape, shape),
        grid=(n // tm,),
        in_specs=[pl.BlockSpec((tm, 1), lambda i: (i, 0)), pl.BlockSpec((1, LANES), lambda i: (0, 0))],
        out_specs=(tab, tab, tab, tab),
        compiler_params=_cparams("parallel"),
        name="rope_tables",
    )(pos_f, inv)


def _inproj_kernel(x_ref, mod_ref, g_ref, w_ref, o_ref):
    h = _norm_mod(x_ref[...], g_ref[...], mod_ref[0:1, :], mod_ref[1:2, :]).astype(BF16)
    for c in range(N_IN // TN_PROJ):
        cols = slice(c * TN_PROJ, (c + 1) * TN_PROJ)
        o_ref[:, cols] = jnp.dot(h, w_ref[:, cols], preferred_element_type=F32).astype(BF16)


def _resident(shape):
    return pl.BlockSpec(shape, lambda *_: (0,) * len(shape), pipeline_mode=pl.Buffered(1))


def _inproj(x, mod, g, w, seq):
    n, d = x.shape
    tm = TM_PROJ
    tpb = seq // tm
    return pl.pallas_call(
        _inproj_kernel,
        out_shape=jax.ShapeDtypeStruct((n, N_IN), BF16),
        grid=(n // tm,),
        in_specs=[
            pl.BlockSpec((tm, d), lambda i: (i, 0)),
            pl.BlockSpec((None, 6, d), lambda i: (i // tpb, 0, 0)),
            pl.BlockSpec((1, d), lambda i: (0, 0)),
            _resident((d, N_IN)),
        ],
        out_specs=pl.BlockSpec((tm, N_IN), lambda i: (i, 0)),
        compiler_params=_cparams("parallel"),
        name="in_proj",
    )(x, mod, g, w)


def _mla_prep_kernel(cq_ref, ckv_ref, kra_ref, cm_ref, sm_ref, cqg_ref, wqa_ref, wqb_ref,
                     ckvg_ref, wk_ref, wv_ref, qga_ref, qgb_ref, kga_ref, kgb_ref,
                     qaug_ref, kaug_ref, vaug_ref, swap_ref, q_ref, k_ref, v_ref):
    cq = cq_ref[...].astype(F32)
    cqn = (cq * lax.rsqrt(jnp.mean(cq * cq, axis=-1, keepdims=True) + EPS) * cqg_ref[...]).astype(BF16)
    qa = jnp.dot(cqn, wqa_ref[...], preferred_element_type=F32)
    qb = jnp.dot(cqn, wqb_ref[...], preferred_element_type=F32)
    ckv = ckv_ref[...].astype(F32)
    ckvn = (ckv * lax.rsqrt(jnp.mean(ckv * ckv, axis=-1, keepdims=True) + EPS) * ckvg_ref[...]).astype(BF16)
    ka = jnp.dot(ckvn, wk_ref[...], preferred_element_type=F32)
    v_ref[...] = (jnp.dot(ckvn, wv_ref[...], preferred_element_type=F32) + vaug_ref[...]).astype(BF16)
    kra = kra_ref[...].astype(F32)
    krb = jnp.dot(kra_ref[...], swap_ref[...], preferred_element_type=F32)
    cm = cm_ref[...]
    sm = sm_ref[...]
    scale = QK_HEAD ** -0.5 * LOG2_E
    q_cos, q_sin = cm * (qga_ref[...] * scale), sm * (qgb_ref[...] * scale)
    k_cos, k_sin = cm * kga_ref[...], sm * kgb_ref[...]
    for h in range(N_GROUPS):
        sl = slice(h * LANES, (h + 1) * LANES)
        qah, qbh = qa[:, sl], qb[:, sl]
        r = lax.rsqrt(jnp.sum(qah * qah, axis=-1, keepdims=True) * (1.0 / QK_HEAD) + EPS)
        q_ref[:, sl] = ((qah * q_cos + qbh * q_sin) * r + qaug_ref[...]).astype(BF16)
        kah = ka[:, sl] + kra
        kbh = ka[:, sl] + krb
        r = lax.rsqrt(jnp.sum(kah * kah, axis=-1, keepdims=True) * (1.0 / QK_HEAD) + EPS)
        k_ref[:, sl] = ((kah * k_cos + kbh * k_sin) * r + kaug_ref[...]).astype(BF16)


def _mla_prep(proj, cm, sm, p):
    n = proj.shape[0]
    tm = TM_PREP
    hw = N_GROUPS * LANES

    def col(width, offset):
        return pl.BlockSpec((tm, width), lambda i: (i, offset // width))

    def full(a):
        return pl.BlockSpec(a.shape, lambda i: (0,) * a.ndim)

    weights = [p["cq_g"], p["wqa"], p["wqb"], p["ckv_g"], p["wk"], p["wv"],
               p["qga"], p["qgb"], p["kga"], p["kgb"], p["qaug"], p["kaug"], p["vaug"], p["swap"]]
    head_tile = pl.BlockSpec((tm, hw), lambda i: (i, 0))
    out = jax.ShapeDtypeStruct((n, hw), BF16)
    return pl.pallas_call(
        _mla_prep_kernel,
        out_shape=(out, out, out),
        grid=(n // tm,),
        in_specs=[col(Q_LORA, COL_CQ), col(KV_LORA, COL_CKV), col(LANES, COL_KRA),
                  pl.BlockSpec((tm, LANES), lambda i: (i, 0)), pl.BlockSpec((tm, LANES), lambda i: (i, 0))]
                 + [full(w) for w in weights],
        out_specs=(head_tile, head_tile, head_tile),
        compiler_params=_cparams("parallel"),
        name="mla_prep",
    )(proj, proj, proj, cm, sm, *weights)


def _flash_kernel(qi_ref, kj_ref, q_ref, k_ref, v_ref, o_ref, acc_scr, *rest, tq, online_max):
    i = qi_ref[pl.program_id(1)]
    j = kj_ref[pl.program_id(1)]

    @pl.when(j == 0)
    def _():
        acc_scr[...] = jnp.zeros(acc_scr.shape, F32)
        if online_max:
            rest[0][...] = jnp.full(rest[0].shape, -jnp.inf, F32)

    def block(q0, nq, nk, masked):
        rows = slice(q0, q0 + nq)
        if masked:
            row = lax.broadcasted_iota(jnp.int32, (nq, nk), 0) + q0
            col = lax.broadcasted_iota(jnp.int32, (nq, nk), 1)
            keep = col <= row
        for h in range(N_GROUPS):
            sl = slice(h * LANES, (h + 1) * LANES)
            s = lax.dot_general(q_ref[rows, sl], k_ref[0:nk, sl], (((1,), (1,)), ((), ())),
                                preferred_element_type=F32)
            if masked:
                s = jnp.where(keep, s, -jnp.inf)
            if online_max:
                m_scr = rest[0]
                m_prev = m_scr[h, rows]
                m_new = jnp.maximum(m_prev, jnp.max(s, axis=-1, keepdims=True))
                p = jnp.exp2(s - m_new).astype(BF16)
                acc_scr[h, rows] = jnp.exp2(m_prev - m_new) * acc_scr[h, rows] + jnp.dot(
                    p, v_ref[0:nk, sl], preferred_element_type=F32)
                m_scr[h, rows] = m_new
            else:
                acc_scr[h, rows] += jnp.dot(jnp.exp2(s).astype(BF16), v_ref[0:nk, sl],
                                            preferred_element_type=F32)

    @pl.when(j < i)
    def _():
        block(0, tq, tq, False)

    @pl.when(j == i)
    def _():
        block(0, tq // 2, tq // 2, True)
        block(tq // 2, tq // 2, tq, True)
        lane = lax.broadcasted_iota(jnp.int32, (tq, LANES), 1)
        for pr in range(N_GROUPS // 2):
            lo = acc_scr[2 * pr]
            hi = acc_scr[2 * pr + 1]
            lo = lo / lo[:, V_HEAD:V_HEAD + 1]
            hi = hi / hi[:, V_HEAD:V_HEAD + 1]
            both = jnp.where(lane < V_HEAD, lo, pltpu.roll(hi, V_HEAD, axis=1))
            o_ref[:, pr * LANES:(pr + 1) * LANES] = both.astype(BF16)


def _flash(q, k, v, batch, seq, online_max):
    n = q.shape[0]
    tq = TQ_ATT
    nq = seq // tq
    hw = N_GROUPS * LANES
    scratch = [pltpu.VMEM((N_GROUPS, tq, LANES), F32)]
    if online_max:
        scratch.append(pltpu.VMEM((N_GROUPS, tq, 1), F32))
    pairs = [(i, j) for i in range(nq) for j in range(i + 1)]
    qi = jnp.asarray([p[0] for p in pairs], jnp.int32)
    kj = jnp.asarray([p[1] for p in pairs], jnp.int32)
    q_tile = lambda b, s, qi, kj: (b * nq + qi[s], 0)
    k_tile = lambda b, s, qi, kj: (b * nq + kj[s], 0)
    return pl.pallas_call(
        functools.partial(_flash_kernel, tq=tq, online_max=online_max),
        out_shape=jax.ShapeDtypeStruct((n, MIX_W), BF16),
        grid_spec=pltpu.PrefetchScalarGridSpec(
            num_scalar_prefetch=2,
            grid=(batch, len(pairs)),
            in_specs=[pl.BlockSpec((tq, hw), q_tile), pl.BlockSpec((tq, hw), k_tile),
                      pl.BlockSpec((tq, hw), k_tile)],
            out_specs=pl.BlockSpec((tq, MIX_W), q_tile),
            scratch_shapes=scratch,
        ),
        compiler_params=_cparams("parallel", "arbitrary"),
        name="mla_flash_online" if online_max else "mla_flash",
    )(qi, kj, q, k, v)


def _gelu_tanh(x):
    return jax.nn.gelu(x, approximate=True)


def _mix_kernel(gates_ref, a_ref, r_ref, su_ref, ymla_ref, x_ref, cos_ref, sin_ref, mod_ref,
                convw_ref, gvg_ref, wscat_ref, bsmat_ref, retg_ref, dec_ref, kdec_ref, qdec_ref,
                cdec_ref, bd_ref, gmat_ref, mk_ref, mv_ref, wb_ref, wo_ref,
                o_ref, carry_scr, state_scr, ysg_scr, yret_scr, *, tm, tpb):
    i = pl.program_id(0)

    @pl.when(i % tpb == 0)
    def _():
        carry_scr[...] = jnp.zeros(carry_scr.shape, F32)
        state_scr[...] = jnp.zeros(state_scr.shape, F32)

    w = MIX_W
    a_b = a_ref[:, 0:w].astype(F32)
    u = a_ref[:, w:2 * w].astype(F32) * a_ref[:, 2 * w:3 * w].astype(F32)
    rowi = lax.broadcasted_iota(jnp.int32, (tm, w), 0)
    prev1 = carry_scr[0:1, :]
    prev2 = carry_scr[1:2, :]
    u1 = jnp.where(rowi == 0, prev1, pltpu.roll(u, 1, axis=0))
    u2 = jnp.where(rowi == 0, prev2, jnp.where(rowi == 1, prev1, pltpu.roll(u, 2, axis=0)))
    carry_scr[0:1, :] = u[tm - 1:tm, :]
    carry_scr[1:2, :] = u[tm - 2:tm - 1, :]
    y_conv = a_b * (convw_ref[0:1, :] * u2 + convw_ref[1:2, :] * u1 + convw_ref[2:3, :] * u)

    gmat = gmat_ref[...]
    s_u = _gelu_tanh(su_ref[:, 0:w].astype(F32))
    s_v = _gelu_tanh(su_ref[:, w:2 * w].astype(F32))
    ms = _group_mean(s_v * s_v, gmat)
    vn = (s_v * lax.rsqrt(ms + EPS) * gvg_ref[...]).astype(BF16)

    cosr = cos_ref[...]
    sinr = sin_ref[...]

    def rot(t):
        t1, t2 = t[:, 0:LANES], t[:, LANES:2 * LANES]
        return jnp.concatenate([t1 * cosr - t2 * sinr, t2 * cosr + t1 * sinr], axis=-1)

    rq = rot(r_ref[:, 0:w].astype(F32))
    rk = rot(r_ref[:, w:2 * w].astype(F32)) * (HEAD_DIM ** -0.5)

    for c in range(tm // CHUNK):
        rows = slice(c * CHUNK, (c + 1) * CHUNK)
        vc = vn[rows, :]
        vbd = jnp.concatenate([vc * mv_ref[g:g + 1, :].astype(BF16) for g in range(N_GROUPS)], axis=0)
        mixed = jnp.dot(wscat_ref[...], vbd, preferred_element_type=F32) + bsmat_ref[...]
        ysg_scr[rows, :] = s_u[rows, :] * mixed

        qc = rq[rows, :]
        kc = rk[rows, :]
        kcb = kc.astype(BF16)
        vcb = r_ref[rows, 2 * w:3 * w]
        qstack = jnp.concatenate([(qc * mk_ref[h:h + 1, :]).astype(BF16) for h in range(N_GROUPS)], axis=0)
        sc = lax.dot_general(qstack, kcb, (((1,), (1,)), ((), ())), preferred_element_type=F32)
        sc = (sc * dec_ref[...]).astype(BF16)
        scat = jnp.concatenate([sc[h * CHUNK:(h + 1) * CHUNK, :] for h in range(N_GROUPS)], axis=1)
        vstack = jnp.concatenate([vcb * mv_ref[h:h + 1, :].astype(BF16) for h in range(N_GROUPS)], axis=0)
        o_c = jnp.dot(scat, vstack, preferred_element_type=F32)
        state = state_scr[...]
        o_c = o_c + jnp.dot((qc * qdec_ref[...]).astype(BF16), state.astype(BF16),
                            preferred_element_type=F32)
        kd_t = jnp.transpose(kc * kdec_ref[...]).astype(BF16)
        kv = jnp.dot(kd_t, vcb, preferred_element_type=F32)
        state_scr[...] = state * cdec_ref[...] + kv * bd_ref[...]
        yret_scr[rows, :] = o_c

    o_all = yret_scr[...]
    xc = o_all - _group_mean(o_all, gmat)
    var = _group_mean(xc * xc, gmat)
    r_g = r_ref[:, 3 * w:4 * w].astype(F32)
    y_ret = (r_g * _sigmoid(r_g)) * (xc * lax.rsqrt(var + EPS) * retg_ref[...])

    d = x_ref.shape[1]
    ys = (y_conv, ymla_ref[...], ysg_scr[...], y_ret)
    merged = None
    for n in range(N_BRANCH):
        gate = _sigmoid(gates_ref[:, n * d:(n + 1) * d])
        term = gate * jnp.dot(ys[n].astype(BF16), wb_ref[n], preferred_element_type=F32).astype(BF16)
        merged = term if merged is None else merged + term
    out = jnp.dot(merged, wo_ref[...], preferred_element_type=F32)
    o_ref[...] = x_ref[...] + mod_ref[2:3, :] * out


def _mixers(proj, ymla, x, cosr, sinr, mod, p, seq):
    n, d = x.shape
    tm = TM_MIX
    tpb = seq // tm

    def col(width, offset):
        return pl.BlockSpec((tm, width), lambda i: (i, offset // width))

    def full(a):
        return pl.BlockSpec(a.shape, lambda i: (0,) * a.ndim)

    consts = [p["conv_w"], p["gv_g"], p["ws_cat"], p["bs_mat"], p["ret_g"], p["dec"], p["kdec"],
              p["qdec"], p["cdec"], p["bd"], p["gmat"], p["mk"], p["mv"], p["w_branch"], p["w_o"]]
    return pl.pallas_call(
        functools.partial(_mix_kernel, tm=tm, tpb=tpb),
        out_shape=jax.ShapeDtypeStruct((n, d), F32),
        grid=(n // tm,),
        in_specs=[col(N_BRANCH * d, COL_GATES), col(4 * MIX_W, COL_A), col(4 * MIX_W, COL_R),
                  col(2 * MIX_W, COL_SU),
                  pl.BlockSpec((tm, MIX_W), lambda i: (i, 0)),
                  pl.BlockSpec((tm, d), lambda i: (i, 0)),
                  pl.BlockSpec((tm, LANES), lambda i: (i, 0)),
                  pl.BlockSpec((tm, LANES), lambda i: (i, 0)),
                  pl.BlockSpec((None, 6, d), lambda i: (i // tpb, 0, 0))]
                 + [full(c) for c in consts],
        out_specs=pl.BlockSpec((tm, d), lambda i: (i, 0)),
        scratch_shapes=[pltpu.VMEM((8, MIX_W), F32), pltpu.VMEM((MIX_W, MIX_W), F32),
                        pltpu.VMEM((tm, MIX_W), F32), pltpu.VMEM((tm, MIX_W), F32)],
        compiler_params=_cparams("arbitrary"),
        name="mixers_merge",
    )(proj, proj, proj, proj, ymla, x, cosr, sinr, mod, *consts)


def _ffn_kernel(x_ref, mod_ref, g_ref, w1_ref, w3_ref, w2_ref, o_ref):
    x = x_ref[...]
    h = _norm_mod(x, g_ref[...], mod_ref[3:4, :], mod_ref[4:5, :]).astype(BF16)
    a = jnp.dot(h, w1_ref[...], preferred_element_type=F32)
    b = jnp.dot(h, w3_ref[...], preferred_element_type=F32)
    hid = ((a * _sigmoid(a)) * b).astype(BF16)
    o_ref[...] = x + mod_ref[5:6, :] * jnp.dot(hid, w2_ref[...], preferred_element_type=F32)


def _dense_ffn(x, mod, g, w1, w3, w2, seq):
    n, d = x.shape
    dff = w1.shape[1]
    tm = TM_FFN
    tpb = seq // tm
    return pl.pallas_call(
        _ffn_kernel,
        out_shape=jax.ShapeDtypeStruct((n, d), F32),
        grid=(n // tm,),
        in_specs=[
            pl.BlockSpec((tm, d), lambda i: (i, 0)),
            pl.BlockSpec((None, 6, d), lambda i: (i // tpb, 0, 0)),
            pl.BlockSpec((1, d), lambda i: (0, 0)),
            _resident((d, dff)), _resident((d, dff)), _resident((dff, d)),
        ],
        out_specs=pl.BlockSpec((tm, d), lambda i: (i, 0)),
        compiler_params=_cparams("parallel"),
        name="dense_swiglu",
    )(x, mod, g, w1, w3, w2)


def _router_kernel(x_ref, mod_ref, g_ref, rw_ref, rb_ref, hs_ref, ei_ref, pw_ref, meta_ref, tot_ref,
                   carry_scr, *, tm, srows):
    i = pl.program_id(0)

    @pl.when(i == 0)
    def _():
        carry_scr[...] = jnp.zeros(carry_scr.shape, F32)

    h = _norm_mod(x_ref[...], g_ref[...], mod_ref[3:4, :], mod_ref[4:5, :])

    h_hi = h.astype(BF16)
    h_lo = (h - h_hi.astype(F32)).astype(BF16)
    hw = jnp.dot(h_hi, rw_ref[...], preferred_element_type=F32)
    logits = (hw[:, :LANES] + hw[:, LANES:] + jnp.dot(h_lo, rw_ref[:, :LANES], preferred_element_type=F32)
              + rb_ref[...])
    mx = jnp.max(logits, axis=-1, keepdims=True)
    ex = jnp.exp(logits - mx)
    probs = ex / jnp.sum(ex, axis=-1, keepdims=True)
    lane = lax.broadcasted_iota(jnp.int32, (tm, LANES), 1)
    valid = lane < N_EXPERTS
    probs = jnp.where(valid, probs, -1.0)
    m1 = jnp.max(probs, axis=-1, keepdims=True)
    i1 = jnp.min(jnp.where(probs == m1, lane, LANES), axis=-1, keepdims=True)
    rest = jnp.where(lane == i1, -1.0, probs)
    m2 = jnp.max(rest, axis=-1, keepdims=True)
    i2 = jnp.min(jnp.where(rest == m2, lane, LANES), axis=-1, keepdims=True)
    den = m1 + m2
    pw_ref[...] = jnp.where(lane == 0, m1 / den, jnp.where(lane == 1, m2 / den, 0.0))

    sel1 = lane == i1
    sel2 = lane == i2
    onehot = jnp.where(sel1, 1.0, 0.0) + jnp.where(sel2, 1.0, 0.0)
    r_i = lax.broadcasted_iota(jnp.int32, (tm, tm), 0)
    c_i = lax.broadcasted_iota(jnp.int32, (tm, tm), 1)
    tri = jnp.where(c_i < r_i, 1.0, 0.0).astype(BF16)
    before = jnp.dot(tri, onehot.astype(BF16), preferred_element_type=F32)
    cnt = jnp.sum(onehot, axis=0, keepdims=True)
    cnt_al = jnp.floor((cnt + (ROW_ALIGN - 1)) * (1.0 / ROW_ALIGN)) * ROW_ALIGN
    e_r = lax.broadcasted_iota(jnp.int32, (LANES, LANES), 0)
    e_c = lax.broadcasted_iota(jnp.int32, (LANES, LANES), 1)
    upper = jnp.where(e_r < e_c, 1.0, 0.0)
    loff = jnp.dot(jnp.broadcast_to(cnt_al, (8, LANES)), upper, precision=HIGHEST,
                   preferred_element_type=F32)[0:1, :]
    slot = loff + before
    slot1 = jnp.sum(jnp.where(sel1, slot, 0.0), axis=-1, keepdims=True).astype(jnp.int32)
    slot2 = jnp.sum(jnp.where(sel2, slot, 0.0), axis=-1, keepdims=True).astype(jnp.int32)
    ei = jnp.where(lane == 0, i1, jnp.where(lane == 1, i2, 0))
    ei_ref[...] = jnp.where(lane == 2, slot1, jnp.where(lane == 3, slot2, ei))

    r_idx = lax.broadcasted_iota(jnp.int32, (tm, srows), 1)
    place = jnp.where(r_idx == slot1, 1.0, jnp.where(r_idx == slot2, 1.0, 0.0)).astype(BF16)
    hs = lax.dot_general(place, h.astype(BF16), (((0,), (0,)), ((), ())), preferred_element_type=F32)
    half = hs.shape[1] // 2
    hs_ref[...] = _pack_bf16_pair(hs[:, :half], hs[:, half:])

    carry = carry_scr[0:1, :]
    mrow = lax.broadcasted_iota(jnp.int32, (8, LANES), 0)
    meta = jnp.where(mrow == 0, cnt_al, jnp.where(mrow == 1, carry, jnp.where(mrow == 2, loff, 0.0)))
    meta_ref[...] = meta.astype(jnp.int32)
    carry_scr[0:1, :] = carry + cnt_al
    tot_ref[...] = jnp.broadcast_to(carry + cnt_al, tot_ref.shape).astype(jnp.int32)


def _router(x, mod, g, rw_pad, rb_pad, seq):
    n, d = x.shape
    tm = TM_ROUTE
    tpb = seq // tm
    nt = n // tm
    return pl.pallas_call(
        functools.partial(_router_kernel, tm=tm, srows=SORT_ROWS),
        out_shape=(jax.ShapeDtypeStruct((nt * SORT_ROWS, d // 2), jnp.uint32),
                   jax.ShapeDtypeStruct((n, LANES), jnp.int32),
                   jax.ShapeDtypeStruct((n, LANES), F32),
                   jax.ShapeDtypeStruct((nt, 8, LANES), jnp.int32),
                   jax.ShapeDtypeStruct((8, LANES), jnp.int32)),
        grid=(nt,),
        in_specs=[
            pl.BlockSpec((tm, d), lambda i: (i, 0)),
            pl.BlockSpec((None, 6, d), lambda i: (i // tpb, 0, 0)),
            pl.BlockSpec((1, d), lambda i: (0, 0)),
            pl.BlockSpec((d, 2 * LANES), lambda i: (0, 0)),
            pl.BlockSpec((1, LANES), lambda i: (0, 0)),
        ],
        out_specs=(pl.BlockSpec((SORT_ROWS, d // 2), lambda i: (i, 0)),
                   pl.BlockSpec((tm, LANES), lambda i: (i, 0)),
                   pl.BlockSpec((tm, LANES), lambda i: (i, 0)),
                   pl.BlockSpec((None, 8, LANES), lambda i: (i, 0, 0)),
                   pl.BlockSpec((8, LANES), lambda i: (0, 0))),
        scratch_shapes=[pltpu.VMEM((8, LANES), F32)],
        compiler_params=_cparams("arbitrary"),
        name="router_top2",
    )(x, mod, g, rw_pad, rb_pad)


def _segment_copy(src_hbm, dst_hbm, src_row, dst_row, n_rows, sem):
    src_row = pl.multiple_of(src_row, ROW_ALIGN)
    dst_row = pl.multiple_of(dst_row, ROW_ALIGN)
    n_rows = pl.multiple_of(n_rows, ROW_ALIGN)
    return pltpu.make_async_copy(src_hbm.at[pl.ds(src_row, n_rows)], dst_hbm.at[pl.ds(dst_row, n_rows)], sem)


def _dispatch_kernel(loff_ref, dst_ref, cnt_ref, rows_ref, gap_ref, hs_ref, xs_ref, buf, zbuf, sem_in, sem_out,
                     sem_zero, *, n_tiles, srows, n_out_tiles):
    zbuf[...] = jnp.zeros(zbuf.shape, zbuf.dtype)
    tg = zbuf.shape[0]
    first_unused = gap_ref[2 * N_EXPERTS]

    def zero_gap(e):
        return _segment_copy(zbuf, xs_ref, 0, gap_ref[2 * e], gap_ref[2 * e + 1], sem_zero)

    def zero_tile(t):
        return pltpu.make_async_copy(zbuf, xs_ref.at[pl.ds(pl.multiple_of(t * tg, ROW_ALIGN), tg)], sem_zero)

    def for_each_zero_copy(action):
        for e in range(N_EXPERTS):
            @pl.when(gap_ref[2 * e + 1] > 0)
            def _():
                action(zero_gap(e))

        def tail(t, carry):
            action(zero_tile(t))
            return carry

        lax.fori_loop(first_unused, n_out_tiles, tail, 0)

    for_each_zero_copy(lambda copy: copy.start())

    def fetch(t):
        slot = t % DISPATCH_SLOTS
        return pltpu.make_async_copy(hs_ref.at[pl.ds(pl.multiple_of(t * srows, ROW_ALIGN), srows)],
                                     buf.at[slot], sem_in.at[slot])

    def drain(t):
        slot = t % DISPATCH_SLOTS
        n_rows = rows_ref[t]

        @pl.when(n_rows > 0)
        def _():
            _segment_copy(buf.at[slot], xs_ref, 0, 0, n_rows, sem_out.at[slot]).wait()

    for t in range(DISPATCH_AHEAD):
        fetch(t).start()

    def body(t, carry):
        slot = t % DISPATCH_SLOTS
        fetch(t).wait()
        for e in range(N_EXPERTS):
            s = t * N_EXPERTS + e
            n_rows = cnt_ref[s]

            @pl.when(n_rows > 0)
            def _():
                _segment_copy(buf.at[slot], xs_ref, loff_ref[s], dst_ref[s], n_rows, sem_out.at[slot]).start()

        @pl.when(t + DISPATCH_AHEAD < n_tiles)
        def _():
            @pl.when(t + DISPATCH_AHEAD >= DISPATCH_SLOTS)
            def _():
                drain(t + DISPATCH_AHEAD - DISPATCH_SLOTS)

            fetch(t + DISPATCH_AHEAD).start()

        return carry

    lax.fori_loop(0, n_tiles, body, 0)
    for t in range(max(n_tiles - DISPATCH_SLOTS, 0), n_tiles):
        drain(t)
    for_each_zero_copy(lambda copy: copy.wait())


def _dispatch(seg_loff, seg_dst, seg_cnt, tile_rows, gaps, hs, n_out_tiles):
    n_tiles = tile_rows.shape[0]
    srows = hs.shape[0] // n_tiles
    assert n_tiles >= DISPATCH_SLOTS
    return pl.pallas_call(
        functools.partial(_dispatch_kernel, n_tiles=n_tiles, srows=srows, n_out_tiles=n_out_tiles),
        out_shape=jax.ShapeDtypeStruct((n_out_tiles * TG_MOE, hs.shape[1]), hs.dtype),
        grid_spec=pltpu.PrefetchScalarGridSpec(
            num_scalar_prefetch=5,
            grid=(1,),
            in_specs=[pl.BlockSpec(memory_space=pl.ANY)],
            out_specs=pl.BlockSpec(memory_space=pl.ANY),
            scratch_shapes=[pltpu.VMEM((DISPATCH_SLOTS, srows, hs.shape[1]), hs.dtype),
                            pltpu.VMEM((TG_MOE, hs.shape[1]), hs.dtype),
                            pltpu.SemaphoreType.DMA((DISPATCH_SLOTS,)),
                            pltpu.SemaphoreType.DMA((DISPATCH_SLOTS,)),
                            pltpu.SemaphoreType.DMA],
        ),
        compiler_params=_cparams("arbitrary"),
        name="moe_dispatch",
    )(seg_loff, seg_dst, seg_cnt, tile_rows, gaps, hs)


def _expert_kernel(te_ref, used_ref, toff_ref, ilo_ref, ihi_ref, cnt_ref, carry_ref, loff_ref,
                   hs_hbm, w1_hbm, w3_hbm, w2_hbm, y_ref, wb1, wb3, wb2, stage, sem, xbuf, xsem, rows_smem,
                   *, srows):
    t = pl.program_id(0)
    e = te_ref[t]
    tg = xbuf.shape[1]
    slot = t % 2
    first_of_expert = jnp.logical_or(t == 0, e != te_ref[jnp.maximum(t - 1, 0)])

    def fetch(tile, into):
        xbuf[into] = jnp.zeros(xbuf.shape[1:], xbuf.dtype)
        expert = te_ref[tile]
        first_row = toff_ref[tile]

        def piece(i, total):
            s = i * N_EXPERTS + expert
            run_start = carry_ref[s]
            lo = jnp.maximum(run_start, first_row)
            hi = jnp.minimum(run_start + cnt_ref[s], first_row + tg)

            @pl.when(hi > lo)
            def _():
                _segment_copy(hs_hbm, xbuf.at[into], i * srows + loff_ref[s] + (lo - run_start), lo - first_row,
                              hi - lo, xsem.at[into]).start()

            return total + jnp.maximum(hi - lo, 0)

        rows_smem[into] = lax.fori_loop(ilo_ref[tile], ihi_ref[tile], piece, 0)

    @pl.when(t == 0)
    def _():
        fetch(t, slot)

    @pl.when(t + 1 < pl.num_programs(0))
    def _():
        fetch(t + 1, 1 - slot)

    @pl.when(rows_smem[slot] > 0)
    def _():
        _segment_copy(hs_hbm, xbuf.at[slot], 0, 0, rows_smem[slot], xsem.at[slot]).wait()

    @pl.when(jnp.logical_and(used_ref[t] == 1, first_of_expert))
    def _():
        n_slots = stage.shape[0]
        windows = [(src, dst, r, c) for src, dst in ((w1_hbm, wb1), (w3_hbm, wb3), (w2_hbm, wb2))
                   for r in range(dst.shape[0] // W_CHUNK) for c in range(dst.shape[1] // W_CHUNK)]

        def staged_copy(k):
            src, _, r, c = windows[k]
            return pltpu.make_async_copy(src.at[e, pl.ds(r * W_CHUNK, W_CHUNK), pl.ds(c * W_CHUNK, W_CHUNK)],
                                         stage.at[k % n_slots], sem.at[k % n_slots])

        for k in range(n_slots - 1):
            staged_copy(k).start()
        for k, (_, dst, r, c) in enumerate(windows):
            staged_copy(k).wait()
            dst[r * W_CHUNK:(r + 1) * W_CHUNK, c * W_CHUNK:(c + 1) * W_CHUNK] = stage[k % n_slots].astype(BF16)
            if k + n_slots - 1 < len(windows):
                staged_copy(k + n_slots - 1).start()

    @pl.when(used_ref[t] == 1)
    def _():
        lo, hi = _unpack_bf16_pair(xbuf[slot])
        h = jnp.concatenate([lo.astype(BF16), hi.astype(BF16)], axis=1)
        a = jnp.dot(h, wb1[...], preferred_element_type=F32)
        b = jnp.dot(h, wb3[...], preferred_element_type=F32)
        hid = ((a * _sigmoid(a)) * b).astype(BF16)
        acc = jnp.dot(hid, wb2[...], preferred_element_type=F32)
        half = acc.shape[1] // 2
        y_ref[...] = _pack_bf16_pair(acc[:, :half], acc[:, half:])

    @pl.when(used_ref[t] == 0)
    def _():
        y_ref[...] = jnp.zeros(y_ref.shape, y_ref.dtype)


def _expert_ffn(tile_tables, seg_tables, hs, w1, w3, w2, n_tiles):
    half = hs.shape[1]
    d = 2 * half
    dff = w1.shape[2]
    tg = TG_MOE
    assert dff % W_CHUNK == 0 and d % W_CHUNK == 0
    hbm = pl.BlockSpec(memory_space=pl.ANY)
    tables = tuple(tile_tables) + tuple(seg_tables)
    return pl.pallas_call(
        functools.partial(_expert_kernel, srows=SORT_ROWS),
        out_shape=jax.ShapeDtypeStruct((n_tiles * tg, half), jnp.uint32),
        grid_spec=pltpu.PrefetchScalarGridSpec(
            num_scalar_prefetch=len(tables),
            grid=(n_tiles,),
            in_specs=[hbm, hbm, hbm, hbm],
            out_specs=pl.BlockSpec((tg, half), lambda t, *_: (t, 0)),
            scratch_shapes=[pltpu.VMEM((d, dff), BF16), pltpu.VMEM((d, dff), BF16), pltpu.VMEM((dff, d), BF16),
                            pltpu.VMEM((W_SLOTS, W_CHUNK, W_CHUNK), F32), pltpu.SemaphoreType.DMA((W_SLOTS,)),
                            pltpu.VMEM((2, tg, half), jnp.uint32), pltpu.SemaphoreType.DMA((2,)),
                            pltpu.SMEM((2,), jnp.int32)],
        ),
        compiler_params=_cparams("arbitrary"),
        name="expert_swiglu",
    )(*tables, hs, w1, w3, w2)


def _combine_kernel(src_ref, loff_ref, cnt_ref, rows_ref, x_ref, ei_ref, pw_ref, mod_ref, y_ref, o_ref,
                    ybuf, sem, *, tm, srows):
    i = pl.program_id(0)
    slot = i % 2

    def fetch(tile, into):
        ybuf[into] = jnp.zeros(ybuf.shape[1:], ybuf.dtype)
        for e in range(N_EXPERTS):
            s = tile * N_EXPERTS + e
            n_rows = cnt_ref[s]

            @pl.when(n_rows > 0)
            def _():
                _segment_copy(y_ref, ybuf.at[into], src_ref[s], loff_ref[s], n_rows, sem.at[into]).start()

    @pl.when(i == 0)
    def _():
        fetch(i, slot)

    @pl.when(i + 1 < pl.num_programs(0))
    def _():
        fetch(i + 1, 1 - slot)

    @pl.when(rows_ref[i] > 0)
    def _():
        _segment_copy(y_ref, ybuf.at[slot], 0, 0, rows_ref[i], sem.at[slot]).wait()

    lo, hi = _unpack_bf16_pair(ybuf[slot])
    ys = jnp.concatenate([lo.astype(BF16), hi.astype(BF16)], axis=1)
    r_idx = lax.broadcasted_iota(jnp.int32, (tm, srows), 1)
    mix = jnp.zeros(x_ref.shape, F32)
    for k in range(TOP_K):
        pick = jnp.where(r_idx == ei_ref[:, TOP_K + k:TOP_K + k + 1], 1.0, 0.0).astype(BF16)
        mix = mix + pw_ref[:, k:k + 1] * jnp.dot(pick, ys, preferred_element_type=F32)
    o_ref[...] = x_ref[...] + mod_ref[5:6, :] * mix


def _combine(seg_src, seg_loff, seg_cnt, tile_rows, x, ei, pw, mod, y, seq):
    n, d = x.shape
    tm = TM_ROUTE
    tpb = seq // tm
    tok = lambda width: pl.BlockSpec((tm, width), lambda i, *_: (i, 0))
    return pl.pallas_call(
        functools.partial(_combine_kernel, tm=tm, srows=SORT_ROWS),
        out_shape=jax.ShapeDtypeStruct((n, d), F32),
        grid_spec=pltpu.PrefetchScalarGridSpec(
            num_scalar_prefetch=4,
            grid=(n // tm,),
            in_specs=[tok(d), tok(LANES), tok(LANES),
                      pl.BlockSpec((None, 6, d), lambda i, *_: (i // tpb, 0, 0)),
                      pl.BlockSpec(memory_space=pl.ANY)],
            out_specs=tok(d),
            scratch_shapes=[pltpu.VMEM((2, SORT_ROWS, d // 2), jnp.uint32), pltpu.SemaphoreType.DMA((2,))],
        ),
        compiler_params=_cparams("arbitrary"),
        name="moe_combine",
    )(seg_src, seg_loff, seg_cnt, tile_rows, x, ei, pw, mod, y)


def _pack_w_in(w_in):
    d = w_in.shape[0]
    w = MIX_W
    o_ckv = 3 * w + Q_LORA
    o_kr = o_ckv + KV_LORA
    o_su = o_kr + QK_ROPE
    o_rq = o_su + 2 * w
    o_gate = o_rq + 4 * w
    half = HEAD_DIM // 2
    perm = np.array([h * HEAD_DIM + part * half + i
                     for part in range(2) for h in range(N_GROUPS) for i in range(half)])
    kr = w_in[:, o_kr:o_kr + QK_ROPE]
    z = lambda k: jnp.zeros((d, k), w_in.dtype)
    cols = [
        w_in[:, o_gate:o_gate + N_BRANCH * d],
        w_in[:, 0:3 * w + Q_LORA],
        w_in[:, o_rq:o_rq + w][:, perm], w_in[:, o_rq + w:o_rq + 2 * w][:, perm],
        w_in[:, o_rq + 2 * w:o_rq + 4 * w],
        w_in[:, o_su:o_su + 2 * w],
        w_in[:, o_ckv:o_ckv + KV_LORA],
        z(QK_NOPE), kr, z(LANES - QK_HEAD),
    ]
    return jnp.concatenate(cols, axis=1).astype(BF16)


def _swap_rope_halves(a):
    hr = QK_ROPE // 2
    return jnp.concatenate([a[..., :QK_NOPE], a[..., QK_NOPE + hr:QK_HEAD], a[..., QK_NOPE:QK_NOPE + hr],
                            a[..., QK_HEAD:]], axis=-1)


def _mla_params(cq_g, w_uq, ckv_g, w_ukv, qn_g, kn_g):
    pad = LANES - QK_HEAD
    wq = w_uq.reshape(Q_LORA, N_GROUPS, QK_HEAD)
    wq = jnp.pad(wq, ((0, 0), (0, 0), (0, pad)))
    wkv = w_ukv.reshape(KV_LORA, N_GROUPS, QK_NOPE + V_HEAD)
    wk = jnp.pad(wkv[:, :, :QK_NOPE], ((0, 0), (0, 0), (0, LANES - QK_NOPE)))
    wv = jnp.pad(wkv[:, :, QK_NOPE:], ((0, 0), (0, 0), (0, LANES - V_HEAD)))
    qg = jnp.pad(qn_g, (0, pad))[None, :]
    kg = jnp.pad(kn_g, (0, pad))[None, :]
    bound = (QK_HEAD ** 0.5 * LOG2_E) * jnp.max(jnp.abs(qn_g)) * jnp.max(jnp.abs(kn_g))
    static_shift = bound <= MAX_STATIC_SHIFT
    lane = jnp.arange(LANES)
    qaug = (lane == QK_HEAD).astype(F32)[None, :]
    kaug = qaug * jnp.where(static_shift, -bound, 0.0)
    vaug = jnp.tile((lane == V_HEAD).astype(F32), N_GROUPS)[None, :]
    params = {
        "cq_g": cq_g[None, :], "ckv_g": ckv_g[None, :],
        "wqa": wq.reshape(Q_LORA, -1).astype(BF16),
        "wqb": _swap_rope_halves(wq).reshape(Q_LORA, -1).astype(BF16),
        "wk": wk.reshape(KV_LORA, -1).astype(BF16),
        "wv": wv.reshape(KV_LORA, -1).astype(BF16),
        "qga": qg, "qgb": _swap_rope_halves(qg), "kga": kg, "kgb": _swap_rope_halves(kg),
        "qaug": qaug, "kaug": kaug, "vaug": vaug,
        "swap": (_swap_rope_halves(lane[None, :])[0][None, :] == lane[:, None]).astype(BF16),
    }
    return params, static_shift


def _mixer_consts():
    h = jnp.arange(N_GROUPS, dtype=F32)
    log_gamma = jnp.log1p(-(2.0 ** (-5.0 - h)))
    pos = jnp.arange(CHUNK, dtype=F32)
    rel = pos[:, None] - pos[None, :]
    dec = jnp.where(rel >= 0, jnp.exp(log_gamma[:, None, None] * jnp.maximum(rel, 0.0)), 0.0)
    lane = np.arange(MIX_W)
    head_k = (lane % LANES) // (HEAD_DIM // 2)
    head_v = lane // HEAD_DIM
    lg_k = log_gamma[head_k]
    return {
        "dec": dec.reshape(N_GROUPS * CHUNK, CHUNK),
        "kdec": jnp.exp(lg_k[None, :] * (CHUNK - 1.0 - pos)[:, None]),
        "qdec": jnp.exp(lg_k[None, :] * (pos + 1.0)[:, None]),
        "cdec": jnp.broadcast_to(jnp.exp(lg_k * CHUNK)[:, None], (MIX_W, MIX_W)),
        "bd": jnp.asarray((head_k[:, None] == head_v[None, :]).astype(np.float32)),
        "gmat": jnp.asarray((head_v[:, None] == head_v[None, :]).astype(np.float32) / HEAD_DIM).astype(BF16),
        "mk": jnp.asarray((head_k[None, :] == np.arange(N_GROUPS)[:, None]).astype(np.float32)),
        "mv": jnp.asarray((head_v[None, :] == np.arange(N_GROUPS)[:, None]).astype(np.float32)),
    }


def _mixer_params(conv_w, gv_g, w_s, b_s, ret_g, w_branch, w_o):
    p = dict(_mixer_consts())
    ws = jnp.tril(w_s)
    p.update({
        "conv_w": conv_w,
        "gv_g": gv_g.reshape(1, MIX_W),
        "ws_cat": jnp.transpose(ws, (1, 0, 2)).reshape(CHUNK, N_GROUPS * CHUNK).astype(BF16),
        "bs_mat": jnp.repeat(b_s.T, HEAD_DIM, axis=1),
        "ret_g": ret_g.reshape(1, MIX_W),
        "w_branch": w_branch.astype(BF16),
        "w_o": w_o.astype(BF16),
    })
    return p


def _moe_layout(meta, tot, n_tiles):
    totals = tot[0, :N_EXPERTS]
    padded = ((totals + TG_MOE - 1) // TG_MOE) * TG_MOE
    ends = jnp.cumsum(padded)
    starts = ends - padded
    seg_cnt = meta[:, 0, :N_EXPERTS]
    seg_carry = meta[:, 1, :N_EXPERTS]
    seg_loff = meta[:, 2, :N_EXPERTS]
    seg_grouped = starts[None, :] + seg_carry
    tile_start = jnp.arange(n_tiles, dtype=jnp.int32) * TG_MOE
    tile_e = jnp.sum((tile_start[:, None] >= ends[None, :]).astype(jnp.int32), axis=1)
    used = (tile_start < ends[-1]).astype(jnp.int32)
    last_e = jnp.sum((ends[-1] - 1 >= ends).astype(jnp.int32))
    tile_e = jnp.minimum(jnp.where(used == 1, tile_e, last_e), N_EXPERTS - 1)
    tile_off = tile_start - starts[tile_e]
    run_start = seg_carry[:, tile_e]
    run_end = run_start + seg_cnt[:, tile_e]
    tile_ilo = jnp.sum((run_end <= tile_off[None, :]).astype(jnp.int32), axis=0) * used
    tile_ihi = jnp.sum((run_start < tile_off[None, :] + TG_MOE).astype(jnp.int32), axis=0) * used
    flat = lambda a: a.reshape(-1).astype(jnp.int32)
    tile_tables = (flat(tile_e), flat(used), flat(tile_off * used), flat(tile_ilo), flat(tile_ihi))
    seg_tables = (flat(seg_cnt), flat(seg_carry), flat(seg_loff))
    return tile_tables, seg_tables, flat(seg_grouped)


def kernel(x, c, positions, norm1_g, norm2_g, ada_w, ada_b, w_in, conv_w, cq_g, w_uq, ckv_g, w_ukv, qn_g, kn_g, gv_g, w_s, b_s, ret_g, w_branch, w_o, ffn_w1, ffn_w3, ffn_w2, router_w, router_b, moe_w1, moe_w3, moe_w2):
    batch, seq, d = x.shape
    depth = ada_w.shape[0]
    n = batch * seq
    assert seq % max(TM_PROJ, TM_PREP, TQ_ATT, TM_MIX, TM_FFN, TM_ROUTE) == 0
    assert d // 2 % LANES == 0

    c_t = jnp.pad(c, ((0, 8 - batch), (0, 0))).T
    ada = _ada(c_t, ada_w, ada_b, batch)[:, :batch].reshape(depth, batch, 6, d)
    cosr, sinr, cm, sm = _rope_tables(positions.astype(F32).reshape(n, 1))

    xt = x.reshape(n, d)
    for l in range(depth):
        mod = ada[l]
        proj = _inproj(xt, mod, norm1_g[l][None, :], _pack_w_in(w_in[l]), seq)
        mla_p, static_shift = _mla_params(cq_g[l], w_uq[l], ckv_g[l], w_ukv[l], qn_g[l], kn_g[l])
        q, k, v = _mla_prep(proj, cm, sm, mla_p)
        y_mla = lax.cond(static_shift,
                         functools.partial(_flash, batch=batch, seq=seq, online_max=False),
                         functools.partial(_flash, batch=batch, seq=seq, online_max=True), q, k, v)
        mp = _mixer_params(conv_w[l], gv_g[l], w_s[l], b_s[l], ret_g[l], w_branch[l], w_o[l])
        xt = _mixers(proj, y_mla, xt, cosr, sinr, mod, mp, seq)
        g2n = norm2_g[l][None, :]
        if l % 2 == 0:
            i = l // 2
            xt = _dense_ffn(xt, mod, g2n, ffn_w1[i].astype(BF16), ffn_w3[i].astype(BF16),
                            ffn_w2[i].astype(BF16), seq)
        else:
            i = l // 2
            rw = jnp.pad(router_w[i], ((0, 0), (0, LANES - N_EXPERTS)))
            rw_hi = rw.astype(BF16)
            rw_pad = jnp.concatenate([rw_hi, (rw - rw_hi.astype(F32)).astype(BF16)], axis=1)
            rb_pad = jnp.pad(router_b[i], (0, LANES - N_EXPERTS), constant_values=-1e30)[None, :]
            hs, ei, pw, meta, tot = _router(xt, mod, g2n, rw_pad, rb_pad, seq)
            max_rows = n * TOP_K + N_EXPERTS * (n // TM_ROUTE) * (ROW_ALIGN - 1)
            n_tiles = -(-max_rows // TG_MOE) + N_EXPERTS
            tile_tables, seg_tables, seg_grouped = _moe_layout(meta, tot, n_tiles)
            seg_cnt, _, seg_loff = seg_tables
            tile_rows = jnp.sum(seg_cnt.reshape(-1, N_EXPERTS), axis=1)
            y = _expert_ffn(tile_tables, seg_tables, hs, moe_w1[i], moe_w3[i], moe_w2[i], n_tiles)
            xt = _combine(seg_grouped, seg_loff, seg_cnt, tile_rows, xt, ei, pw, mod, y, seq)
    return xt.reshape(batch, seq, d)
```

```python
import functools

import jax
import jax.numpy as jnp
import numpy as np
from jax import lax
from jax.experimental import pallas as pl
from jax.experimental.pallas import tpu as pltpu

F32 = jnp.float32
BF16 = jnp.bfloat16
HIGHEST = lax.Precision.HIGHEST

HEAD_DIM = 64
N_GROUPS = 4
MIX_W = N_GROUPS * HEAD_DIM
N_BRANCH = 4
CONV_W = 3
Q_LORA = 256
KV_LORA = 128
QK_NOPE = 64
QK_ROPE = 32
QK_HEAD = QK_NOPE + QK_ROPE
V_HEAD = 64
CHUNK = 128
N_EXPERTS = 8
TOP_K = 2
ROPE_THETA = 10000.0
EPS = 1e-6
LOG2_E = 1.4426950408889634
MAX_STATIC_SHIFT = 50.0

LANES = 128
VMEM_LIMIT_BYTES = 56 * 1024 * 1024

COL_GATES = 0
COL_A = 4096
COL_CQ = COL_A + 3 * MIX_W
COL_R = 5120
COL_SU = 6144
COL_CKV = 6656
COL_KRA = 6784
N_IN = 6912

TM_PROJ = 512
TN_PROJ = 768
TM_PREP = 1024
TQ_ATT = 1024
TM_MIX = 512
TM_FFN = 512
TM_ROUTE = 512
ROW_ALIGN = 8
SORT_ROWS = TOP_K * TM_ROUTE + N_EXPERTS * ROW_ALIGN
TG_MOE = 512
W_CHUNK = 512
W_SLOTS = 6


def _cparams(*sem):
    return pltpu.CompilerParams(dimension_semantics=sem, vmem_limit_bytes=VMEM_LIMIT_BYTES)


def _sigmoid(x):
    return jnp.tanh(x * 0.5) * 0.5 + 0.5


def _group_mean(x, gmat_bf16):
    hi = x.astype(BF16)
    lo = (x - hi.astype(F32)).astype(BF16)
    return (jnp.dot(hi, gmat_bf16, preferred_element_type=F32)
            + jnp.dot(lo, gmat_bf16, preferred_element_type=F32))


def _pack_bf16_pair(lo, hi):
    lo_bits = lax.bitcast_convert_type(lo.astype(BF16).astype(F32), jnp.uint32)
    hi_bits = lax.bitcast_convert_type(hi.astype(BF16).astype(F32), jnp.uint32)
    return (lo_bits >> 16) | (hi_bits & jnp.uint32(0xFFFF0000))


def _unpack_bf16_pair(p):
    lo = lax.bitcast_convert_type(p << 16, F32)
    hi = lax.bitcast_convert_type(p & jnp.uint32(0xFFFF0000), F32)
    return lo, hi


def _norm_mod(x, g, shift, scale):
    y = x * lax.rsqrt(jnp.mean(x * x, axis=-1, keepdims=True) + EPS)
    return (y * g) * (1.0 + scale) + shift


def _ada_kernel(ct_ref, w_ref, b_ref, o_ref, *, batch):
    ct = ct_ref[...]
    cond = ct * _sigmoid(ct)
    w = w_ref[...]
    o_ref[...] = jnp.zeros(o_ref.shape, F32)
    for b in range(batch):
        o_ref[b:b + 1, :] = jnp.sum(w * cond[:, b:b + 1], axis=0, keepdims=True) + b_ref[...]


def _ada(c_t, ada_w, ada_b, batch):
    n_layer, d, d6 = ada_w.shape
    rows = c_t.shape[1]
    tn = 2048
    return pl.pallas_call(
        functools.partial(_ada_kernel, batch=batch),
        out_shape=jax.ShapeDtypeStruct((n_layer, rows, d6), F32),
        grid=(n_layer, d6 // tn),
        in_specs=[
            pl.BlockSpec((d, rows), lambda l, j: (0, 0)),
            pl.BlockSpec((None, d, tn), lambda l, j: (l, 0, j)),
            pl.BlockSpec((None, 1, tn), lambda l, j: (l, 0, j)),
        ],
        out_specs=pl.BlockSpec((None, rows, tn), lambda l, j: (l, 0, j)),
        compiler_params=_cparams("parallel", "parallel"),
        name="ada_mod",
    )(c_t, ada_w, ada_b.reshape(n_layer, 1, d6))


def _rope_kernel(pos_ref, inv_ref, cr_ref, sr_ref, cm_ref, sm_ref):
    half_r = HEAD_DIM // 2
    half_m = QK_ROPE // 2
    tm = pos_ref.shape[0]
    low = lax.broadcasted_iota(jnp.int32, (tm // 2, LANES), 1) < LANES // 2
    ang = jnp.where(low, pos_ref[0:tm // 2, :], pos_ref[tm // 2:tm, :]) * inv_ref[...]
    c = jnp.cos(ang)
    s = jnp.sin(ang)
    c = jnp.concatenate([c, pltpu.roll(c, LANES // 2, axis=1)], axis=0)
    s = jnp.concatenate([s, pltpu.roll(s, LANES // 2, axis=1)], axis=0)
    lane = lax.broadcasted_iota(jnp.int32, c.shape, 1)

    def tile_r(t):
        t = jnp.where(lane < half_r, t, 0.0)
        out = t
        for k in range(1, LANES // half_r):
            out = out + pltpu.roll(t, k * half_r, axis=1)
        return out

    cr_ref[...] = tile_r(c)
    sr_ref[...] = tile_r(s)
    first = jnp.logical_and(lane >= QK_NOPE, lane < QK_NOPE + half_m)
    second = jnp.logical_and(lane >= QK_NOPE + half_m, lane < QK_HEAD)
    c1, c2 = pltpu.roll(c, QK_NOPE - half_r, axis=1), pltpu.roll(c, QK_NOPE + half_m - half_r, axis=1)
    s1, s2 = pltpu.roll(s, QK_NOPE - half_r, axis=1), pltpu.roll(s, QK_NOPE + half_m - half_r, axis=1)
    cm_ref[...] = jnp.where(first, c1, jnp.where(second, c2, 1.0))
    sm_ref[...] = jnp.where(first, -s1, jnp.where(second, s2, 0.0))


def _rope_tables(pos_f):
    n = pos_f.shape[0]
    tm = 1024
    half_r = HEAD_DIM // 2
    half_m = QK_ROPE // 2
    inv_r = ROPE_THETA ** (-jnp.arange(half_r, dtype=F32) / half_r)
    inv_m = ROPE_THETA ** (-jnp.arange(half_m, dtype=F32) / half_m)
    inv = jnp.concatenate([inv_r, inv_m, jnp.zeros((LANES // 2 - half_r - half_m,), F32)])
    inv = jnp.tile(inv, 2)[None, :]
    tab = pl.BlockSpec((tm, LANES), lambda i: (i, 0))
    shape = jax.ShapeDtypeStruct((n, LANES), F32)
    return pl.pallas_call(
        _rope_kernel,
        out_shape=(shape, shape, shape, shape),
        grid=(n // tm,),
        in_specs=[pl.BlockSpec((tm, 1), lambda i: (i, 0)), pl.BlockSpec((1, LANES), lambda i: (0, 0))],
        out_specs=(tab, tab, tab, tab),
        compiler_params=_cparams("parallel"),
        name="rope_tables",
    )(pos_f, inv)


def _inproj_kernel(x_ref, mod_ref, g_ref, w_ref, o_ref):
    h = _norm_mod(x_ref[...], g_ref[...], mod_ref[0:1, :], mod_ref[1:2, :]).astype(BF16)
    for c in range(N_IN // TN_PROJ):
        cols = slice(c * TN_PROJ, (c + 1) * TN_PROJ)
        o_ref[:, cols] = jnp.dot(h, w_ref[:, cols], preferred_element_type=F32).astype(BF16)


def _resident(shape):
    return pl.BlockSpec(shape, lambda *_: (0,) * len(shape), pipeline_mode=pl.Buffered(1))


def _inproj(x, mod, g, w, seq):
    n, d = x.shape
    tm = TM_PROJ
    tpb = seq // tm
    return pl.pallas_call(
        _inproj_kernel,
        out_shape=jax.ShapeDtypeStruct((n, N_IN), BF16),
        grid=(n // tm,),
        in_specs=[
            pl.BlockSpec((tm, d), lambda i: (i, 0)),
            pl.BlockSpec((None, 6, d), lambda i: (i // tpb, 0, 0)),
            pl.BlockSpec((1, d), lambda i: (0, 0)),
            _resident((d, N_IN)),
        ],
        out_specs=pl.BlockSpec((tm, N_IN), lambda i: (i, 0)),
        compiler_params=_cparams("parallel"),
        name="in_proj",
    )(x, mod, g, w)


def _mla_prep_kernel(cq_ref, ckv_ref, kra_ref, cm_ref, sm_ref, cqg_ref, wqa_ref, wqb_ref,
                     ckvg_ref, wk_ref, wv_ref, qga_ref, qgb_ref, kga_ref, kgb_ref,
                     qaug_ref, kaug_ref, vaug_ref, swap_ref, q_ref, k_ref, v_ref):
    cq = cq_ref[...].astype(F32)
    cqn = (cq * lax.rsqrt(jnp.mean(cq * cq, axis=-1, keepdims=True) + EPS) * cqg_ref[...]).astype(BF16)
    qa = jnp.dot(cqn, wqa_ref[...], preferred_element_type=F32)
    qb = jnp.dot(cqn, wqb_ref[...], preferred_element_type=F32)
    ckv = ckv_ref[...].astype(F32)
    ckvn = (ckv * lax.rsqrt(jnp.mean(ckv * ckv, axis=-1, keepdims=True) + EPS) * ckvg_ref[...]).astype(BF16)
    ka = jnp.dot(ckvn, wk_ref[...], preferred_element_type=F32)
    v_ref[...] = (jnp.dot(ckvn, wv_ref[...], preferred_element_type=F32) + vaug_ref[...]).astype(BF16)
    kra = kra_ref[...].astype(F32)
    krb = jnp.dot(kra_ref[...], swap_ref[...], preferred_element_type=F32)
    cm = cm_ref[...]
    sm = sm_ref[...]
    scale = QK_HEAD ** -0.5 * LOG2_E
    q_cos, q_sin = cm * (qga_ref[...] * scale), sm * (qgb_ref[...] * scale)
    k_cos, k_sin = cm * kga_ref[...], sm * kgb_ref[...]
    for h in range(N_GROUPS):
        sl = slice(h * LANES, (h + 1) * LANES)
        qah, qbh = qa[:, sl], qb[:, sl]
        r = lax.rsqrt(jnp.sum(qah * qah, axis=-1, keepdims=True) * (1.0 / QK_HEAD) + EPS)
        q_ref[:, sl] = ((qah * q_cos + qbh * q_sin) * r + qaug_ref[...]).astype(BF16)
        kah = ka[:, sl] + kra
        kbh = ka[:, sl] + krb
        r = lax.rsqrt(jnp.sum(kah * kah, axis=-1, keepdims=True) * (1.0 / QK_HEAD) + EPS)
        k_ref[:, sl] = ((kah * k_cos + kbh * k_sin) * r + kaug_ref[...]).astype(BF16)


def _mla_prep(proj, cm, sm, p):
    n = proj.shape[0]
    tm = TM_PREP
    hw = N_GROUPS * LANES

    def col(width, offset):
        return pl.BlockSpec((tm, width), lambda i: (i, offset // width))

    def full(a):
        return pl.BlockSpec(a.shape, lambda i: (0,) * a.ndim)

    weights = [p["cq_g"], p["wqa"], p["wqb"], p["ckv_g"], p["wk"], p["wv"],
               p["qga"], p["qgb"], p["kga"], p["kgb"], p["qaug"], p["kaug"], p["vaug"], p["swap"]]
    head_tile = pl.BlockSpec((tm, hw), lambda i: (i, 0))
    out = jax.ShapeDtypeStruct((n, hw), BF16)
    return pl.pallas_call(
        _mla_prep_kernel,
        out_shape=(out, out, out),
        grid=(n // tm,),
        in_specs=[col(Q_LORA, COL_CQ), col(KV_LORA, COL_CKV), col(LANES, COL_KRA),
                  pl.BlockSpec((tm, LANES), lambda i: (i, 0)), pl.BlockSpec((tm, LANES), lambda i: (i, 0))]
                 + [full(w) for w in weights],
        out_specs=(head_tile, head_tile, head_tile),
        compiler_params=_cparams("parallel"),
        name="mla_prep",
    )(proj, proj, proj, cm, sm, *weights)


def _flash_kernel(qi_ref, kj_ref, q_ref, k_ref, v_ref, o_ref, acc_scr, *rest, tq, online_max):
    i = qi_ref[pl.program_id(1)]
    j = kj_ref[pl.program_id(1)]

    @pl.when(j == 0)
    def _():
        acc_scr[...] = jnp.zeros(acc_scr.shape, F32)
        if online_max:
            rest[0][...] = jnp.full(rest[0].shape, -jnp.inf, F32)

    def block(q0, nq, nk, masked):
        rows = slice(q0, q0 + nq)
        if masked:
            row = lax.broadcasted_iota(jnp.int32, (nq, nk), 0) + q0
            col = lax.broadcasted_iota(jnp.int32, (nq, nk), 1)
            keep = col <= row
        for h in range(N_GROUPS):
            sl = slice(h * LANES, (h + 1) * LANES)
            s = lax.dot_general(q_ref[rows, sl], k_ref[0:nk, sl], (((1,), (1,)), ((), ())),
                                preferred_element_type=F32)
            if masked:
                s = jnp.where(keep, s, -jnp.inf)
            if online_max:
                m_scr = rest[0]
                m_prev = m_scr[h, rows]
                m_new = jnp.maximum(m_prev, jnp.max(s, axis=-1, keepdims=True))
                p = jnp.exp2(s - m_new).astype(BF16)
                acc_scr[h, rows] = jnp.exp2(m_prev - m_new) * acc_scr[h, rows] + jnp.dot(
                    p, v_ref[0:nk, sl], preferred_element_type=F32)
                m_scr[h, rows] = m_new
            else:
                acc_scr[h, rows] += jnp.dot(jnp.exp2(s).astype(BF16), v_ref[0:nk, sl],
                                            preferred_element_type=F32)

    @pl.when(j < i)
    def _():
        block(0, tq, tq, False)

    @pl.when(j == i)
    def _():
        block(0, tq // 2, tq // 2, True)
        block(tq // 2, tq // 2, tq, True)
        lane = lax.broadcasted_iota(jnp.int32, (tq, LANES), 1)
        for pr in range(N_GROUPS // 2):
            lo = acc_scr[2 * pr]
            hi = acc_scr[2 * pr + 1]
            lo = lo / lo[:, V_HEAD:V_HEAD + 1]
            hi = hi / hi[:, V_HEAD:V_HEAD + 1]
            both = jnp.where(lane < V_HEAD, lo, pltpu.roll(hi, V_HEAD, axis=1))
            o_ref[:, pr * LANES:(pr + 1) * LANES] = both.astype(BF16)


def _flash(q, k, v, batch, seq, online_max):
    n = q.shape[0]
    tq = TQ_ATT
    nq = seq // tq
    hw = N_GROUPS * LANES
    scratch = [pltpu.VMEM((N_GROUPS, tq, LANES), F32)]
    if online_max:
        scratch.append(pltpu.VMEM((N_GROUPS, tq, 1), F32))
    pairs = [(i, j) for i in range(nq) for j in range(i + 1)]
    qi = jnp.asarray([p[0] for p in pairs], jnp.int32)
    kj = jnp.asarray([p[1] for p in pairs], jnp.int32)
    q_tile = lambda b, s, qi, kj: (b * nq + qi[s], 0)
    k_tile = lambda b, s, qi, kj: (b * nq + kj[s], 0)
    return pl.pallas_call(
        functools.partial(_flash_kernel, tq=tq, online_max=online_max),
        out_shape=jax.ShapeDtypeStruct((n, MIX_W), BF16),
        grid_spec=pltpu.PrefetchScalarGridSpec(
            num_scalar_prefetch=2,
            grid=(batch, len(pairs)),
            in_specs=[pl.BlockSpec((tq, hw), q_tile), pl.BlockSpec((tq, hw), k_tile),
                      pl.BlockSpec((tq, hw), k_tile)],
            out_specs=pl.BlockSpec((tq, MIX_W), q_tile),
            scratch_shapes=scratch,
        ),
        compiler_params=_cparams("parallel", "arbitrary"),
        name="mla_flash_online" if online_max else "mla_flash",
    )(qi, kj, q, k, v)


def _gelu_tanh(x):
    return jax.nn.gelu(x, approximate=True)


def _mix_kernel(gates_ref, a_ref, r_ref, su_ref, ymla_ref, x_ref, cos_ref, sin_ref, mod_ref,
                convw_ref, gvg_ref, wscat_ref, bsmat_ref, retg_ref, dec_ref, kdec_ref, qdec_ref,
                cdec_ref, bd_ref, gmat_ref, mk_ref, mv_ref, wb_ref, wo_ref,
                o_ref, carry_scr, state_scr, ysg_scr, yret_scr, *, tm, tpb):
    i = pl.program_id(0)

    @pl.when(i % tpb == 0)
    def _():
        carry_scr[...] = jnp.zeros(carry_scr.shape, F32)
        state_scr[...] = jnp.zeros(state_scr.shape, F32)

    w = MIX_W
    a_b = a_ref[:, 0:w].astype(F32)
    u = a_ref[:, w:2 * w].astype(F32) * a_ref[:, 2 * w:3 * w].astype(F32)
    rowi = lax.broadcasted_iota(jnp.int32, (tm, w), 0)
    prev1 = carry_scr[0:1, :]
    prev2 = carry_scr[1:2, :]
    u1 = jnp.where(rowi == 0, prev1, pltpu.roll(u, 1, axis=0))
    u2 = jnp.where(rowi == 0, prev2, jnp.where(rowi == 1, prev1, pltpu.roll(u, 2, axis=0)))
    carry_scr[0:1, :] = u[tm - 1:tm, :]
    carry_scr[1:2, :] = u[tm - 2:tm - 1, :]
    y_conv = a_b * (convw_ref[0:1, :] * u2 + convw_ref[1:2, :] * u1 + convw_ref[2:3, :] * u)

    gmat = gmat_ref[...]
    s_u = _gelu_tanh(su_ref[:, 0:w].astype(F32))
    s_v = _gelu_tanh(su_ref[:, w:2 * w].astype(F32))
    ms = _group_mean(s_v * s_v, gmat)
    vn = (s_v * lax.rsqrt(ms + EPS) * gvg_ref[...]).astype(BF16)

    cosr = cos_ref[...]
    sinr = sin_ref[...]

    def rot(t):
        t1, t2 = t[:, 0:LANES], t[:, LANES:2 * LANES]
        return jnp.concatenate([t1 * cosr - t2 * sinr, t2 * cosr + t1 * sinr], axis=-1)

    rq = rot(r_ref[:, 0:w].astype(F32))
    rk = rot(r_ref[:, w:2 * w].astype(F32)) * (HEAD_DIM ** -0.5)

    for c in range(tm // CHUNK):
        rows = slice(c * CHUNK, (c + 1) * CHUNK)
        vc = vn[rows, :]
        vbd = jnp.concatenate([vc * mv_ref[g:g + 1, :].astype(BF16) for g in range(N_GROUPS)], axis=0)
        mixed = jnp.dot(wscat_ref[...], vbd, preferred_element_type=F32) + bsmat_ref[...]
        ysg_scr[rows, :] = s_u[rows, :] * mixed

        qc = rq[rows, :]
        kc = rk[rows, :]
        kcb = kc.astype(BF16)
        vcb = r_ref[rows, 2 * w:3 * w]
        qstack = jnp.concatenate([(qc * mk_ref[h:h + 1, :]).astype(BF16) for h in range(N_GROUPS)], axis=0)
        sc = lax.dot_general(qstack, kcb, (((1,), (1,)), ((), ())), preferred_element_type=F32)
        sc = (sc * dec_ref[...]).astype(BF16)
        scat = jnp.concatenate([sc[h * CHUNK:(h + 1) * CHUNK, :] for h in range(N_GROUPS)], axis=1)
        vstack = jnp.concatenate([vcb * mv_ref[h:h + 1, :].astype(BF16) for h in range(N_GROUPS)], axis=0)
        o_c = jnp.dot(scat, vstack, preferred_element_type=F32)
        state = state_scr[...]
        o_c = o_c + jnp.dot((qc * qdec_ref[...]).astype(BF16), state.astype(BF16),
                            preferred_element_type=F32)
        kd_t = jnp.transpose(kc * kdec_ref[...]).astype(BF16)
        kv = jnp.dot(kd_t, vcb, preferred_element_type=F32)
        state_scr[...] = state * cdec_ref[...] + kv * bd_ref[...]
        yret_scr[rows, :] = o_c

    o_all = yret_scr[...]
    xc = o_all - _group_mean(o_all, gmat)
    var = _group_mean(xc * xc, gmat)
    r_g = r_ref[:, 3 * w:4 * w].astype(F32)
    y_ret = (r_g * _sigmoid(r_g)) * (xc * lax.rsqrt(var + EPS) * retg_ref[...])

    d = x_ref.shape[1]
    ys = (y_conv, ymla_ref[...], ysg_scr[...], y_ret)
    merged = None
    for n in range(N_BRANCH):
        gate = _sigmoid(gates_ref[:, n * d:(n + 1) * d])
        term = gate * jnp.dot(ys[n].astype(BF16), wb_ref[n], preferred_element_type=F32).astype(BF16)
        merged = term if merged is None else merged + term
    out = jnp.dot(merged, wo_ref[...], preferred_element_type=F32)
    o_ref[...] = x_ref[...] + mod_ref[2:3, :] * out


def _mixers(proj, ymla, x, cosr, sinr, mod, p, seq):
    n, d = x.shape
    tm = TM_MIX
    tpb = seq // tm

    def col(width, offset):
        return pl.BlockSpec((tm, width), lambda i: (i, offset // width))

    def full(a):
        return pl.BlockSpec(a.shape, lambda i: (0,) * a.ndim)

    consts = [p["conv_w"], p["gv_g"], p["ws_cat"], p["bs_mat"], p["ret_g"], p["dec"], p["kdec"],
              p["qdec"], p["cdec"], p["bd"], p["gmat"], p["mk"], p["mv"], p["w_branch"], p["w_o"]]
    return pl.pallas_call(
        functools.partial(_mix_kernel, tm=tm, tpb=tpb),
        out_shape=jax.ShapeDtypeStruct((n, d), F32),
        grid=(n // tm,),
        in_specs=[col(N_BRANCH * d, COL_GATES), col(4 * MIX_W, COL_A), col(4 * MIX_W, COL_R),
                  col(2 * MIX_W, COL_SU),
                  pl.BlockSpec((tm, MIX_W), lambda i: (i, 0)),
                  pl.BlockSpec((tm, d), lambda i: (i, 0)),
                  pl.BlockSpec((tm, LANES), lambda i: (i, 0)),
                  pl.BlockSpec((tm, LANES), lambda i: (i, 0)),
                  pl.BlockSpec((None, 6, d), lambda i: (i // tpb, 0, 0))]
                 + [full(c) for c in consts],
        out_specs=pl.BlockSpec((tm, d), lambda i: (i, 0)),
        scratch_shapes=[pltpu.VMEM((8, MIX_W), F32), pltpu.VMEM((MIX_W, MIX_W), F32),
                        pltpu.VMEM((tm, MIX_W), F32), pltpu.VMEM((tm, MIX_W), F32)],
        compiler_params=_cparams("arbitrary"),
        name="mixers_merge",
    )(proj, proj, proj, proj, ymla, x, cosr, sinr, mod, *consts)


def _ffn_kernel(x_ref, mod_ref, g_ref, w1_ref, w3_ref, w2_ref, o_ref):
    x = x_ref[...]
    h = _norm_mod(x, g_ref[...], mod_ref[3:4, :], mod_ref[4:5, :]).astype(BF16)
    a = jnp.dot(h, w1_ref[...], preferred_element_type=F32)
    b = jnp.dot(h, w3_ref[...], preferred_element_type=F32)
    hid = ((a * _sigmoid(a)) * b).astype(BF16)
    o_ref[...] = x + mod_ref[5:6, :] * jnp.dot(hid, w2_ref[...], preferred_element_type=F32)


def _dense_ffn(x, mod, g, w1, w3, w2, seq):
    n, d = x.shape
    dff = w1.shape[1]
    tm = TM_FFN
    tpb = seq // tm
    return pl.pallas_call(
        _ffn_kernel,
        out_shape=jax.ShapeDtypeStruct((n, d), F32),
        grid=(n // tm,),
        in_specs=[
            pl.BlockSpec((tm, d), lambda i: (i, 0)),
            pl.BlockSpec((None, 6, d), lambda i: (i // tpb, 0, 0)),
            pl.BlockSpec((1, d), lambda i: (0, 0)),
            _resident((d, dff)), _resident((d, dff)), _resident((dff, d)),
        ],
        out_specs=pl.BlockSpec((tm, d), lambda i: (i, 0)),
        compiler_params=_cparams("parallel"),
        name="dense_swiglu",
    )(x, mod, g, w1, w3, w2)


def _router_kernel(x_ref, mod_ref, g_ref, rw_ref, rb_ref, hs_ref, ei_ref, pw_ref, meta_ref, tot_ref,
                   carry_scr, *, tm, srows):
    i = pl.program_id(0)

    @pl.when(i == 0)
    def _():
        carry_scr[...] = jnp.zeros(carry_scr.shape, F32)

    h = _norm_mod(x_ref[...], g_ref[...], mod_ref[3:4, :], mod_ref[4:5, :])

    h_hi = h.astype(BF16)
    h_lo = (h - h_hi.astype(F32)).astype(BF16)
    hw = jnp.dot(h_hi, rw_ref[...], preferred_element_type=F32)
    logits = (hw[:, :LANES] + hw[:, LANES:] + jnp.dot(h_lo, rw_ref[:, :LANES], preferred_element_type=F32)
              + rb_ref[...])
    mx = jnp.max(logits, axis=-1, keepdims=True)
    ex = jnp.exp(logits - mx)
    probs = ex / jnp.sum(ex, axis=-1, keepdims=True)
    lane = lax.broadcasted_iota(jnp.int32, (tm, LANES), 1)
    valid = lane < N_EXPERTS
    probs = jnp.where(valid, probs, -1.0)
    m1 = jnp.max(probs, axis=-1, keepdims=True)
    i1 = jnp.min(jnp.where(probs == m1, lane, LANES), axis=-1, keepdims=True)
    rest = jnp.where(lane == i1, -1.0, probs)
    m2 = jnp.max(rest, axis=-1, keepdims=True)
    i2 = jnp.min(jnp.where(rest == m2, lane, LANES), axis=-1, keepdims=True)
    den = m1 + m2
    pw_ref[...] = jnp.where(lane == 0, m1 / den, jnp.where(lane == 1, m2 / den, 0.0))

    sel1 = lane == i1
    sel2 = lane == i2
    onehot = jnp.where(sel1, 1.0, 0.0) + jnp.where(sel2, 1.0, 0.0)
    r_i = lax.broadcasted_iota(jnp.int32, (tm, tm), 0)
    c_i = lax.broadcasted_iota(jnp.int32, (tm, tm), 1)
    tri = jnp.where(c_i < r_i, 1.0, 0.0).astype(BF16)
    before = jnp.dot(tri, onehot.astype(BF16), preferred_element_type=F32)
    cnt = jnp.sum(onehot, axis=0, keepdims=True)
    cnt_al = jnp.floor((cnt + (ROW_ALIGN - 1)) * (1.0 / ROW_ALIGN)) * ROW_ALIGN
    e_r = lax.broadcasted_iota(jnp.int32, (LANES, LANES), 0)
    e_c = lax.broadcasted_iota(jnp.int32, (LANES, LANES), 1)
    upper = jnp.where(e_r < e_c, 1.0, 0.0)
    loff = jnp.dot(jnp.broadcast_to(cnt_al, (8, LANES)), upper, precision=HIGHEST,
                   preferred_element_type=F32)[0:1, :]
    slot = loff + before
    slot1 = jnp.sum(jnp.where(sel1, slot, 0.0), axis=-1, keepdims=True).astype(jnp.int32)
    slot2 = jnp.sum(jnp.where(sel2, slot, 0.0), axis=-1, keepdims=True).astype(jnp.int32)
    ei = jnp.where(lane == 0, i1, jnp.where(lane == 1, i2, 0))
    ei_ref[...] = jnp.where(lane == 2, slot1, jnp.where(lane == 3, slot2, ei))

    r_idx = lax.broadcasted_iota(jnp.int32, (tm, srows), 1)
    place = jnp.where(r_idx == slot1, 1.0, jnp.where(r_idx == slot2, 1.0, 0.0)).astype(BF16)
    hs = lax.dot_general(place, h.astype(BF16), (((0,), (0,)), ((), ())), preferred_element_type=F32)
    half = hs.shape[1] // 2
    hs_ref[...] = _pack_bf16_pair(hs[:, :half], hs[:, half:])

    carry = carry_scr[0:1, :]
    mrow = lax.broadcasted_iota(jnp.int32, (8, LANES), 0)
    meta = jnp.where(mrow == 0, cnt_al, jnp.where(mrow == 1, carry, jnp.where(mrow == 2, loff, 0.0)))
    meta_ref[...] = meta.astype(jnp.int32)
    carry_scr[0:1, :] = carry + cnt_al
    tot_ref[...] = jnp.broadcast_to(carry + cnt_al, tot_ref.shape).astype(jnp.int32)


def _router(x, mod, g, rw_pad, rb_pad, seq):
    n, d = x.shape
    tm = TM_ROUTE
    tpb = seq // tm
    nt = n // tm
    return pl.pallas_call(
        functools.partial(_router_kernel, tm=tm, srows=SORT_ROWS),
        out_shape=(jax.ShapeDtypeStruct((nt * SORT_ROWS, d // 2), jnp.uint32),
                   jax.ShapeDtypeStruct((n, LANES), jnp.int32),
                   jax.ShapeDtypeStruct((n, LANES), F32),
                   jax.ShapeDtypeStruct((nt, 8, LANES), jnp.int32),
                   jax.ShapeDtypeStruct((8, LANES), jnp.int32)),
        grid=(nt,),
        in_specs=[
            pl.BlockSpec((tm, d), lambda i: (i, 0)),
            pl.BlockSpec((None, 6, d), lambda i: (i // tpb, 0, 0)),
            pl.BlockSpec((1, d), lambda i: (0, 0)),
            pl.BlockSpec((d, 2 * LANES), lambda i: (0, 0)),
            pl.BlockSpec((1, LANES), lambda i: (0, 0)),
        ],
        out_specs=(pl.BlockSpec((SORT_ROWS, d // 2), lambda i: (i, 0)),
                   pl.BlockSpec((tm, LANES), lambda i: (i, 0)),
                   pl.BlockSpec((tm, LANES), lambda i: (i, 0)),
                   pl.BlockSpec((None, 8, LANES), lambda i: (i, 0, 0)),
                   pl.BlockSpec((8, LANES), lambda i: (0, 0))),
        scratch_shapes=[pltpu.VMEM((8, LANES), F32)],
        compiler_params=_cparams("arbitrary"),
        name="router_top2",
    )(x, mod, g, rw_pad, rb_pad)


def _segment_copy(src_hbm, dst_hbm, src_row, dst_row, n_rows, sem):
    src_row = pl.multiple_of(src_row, ROW_ALIGN)
    dst_row = pl.multiple_of(dst_row, ROW_ALIGN)
    n_rows = pl.multiple_of(n_rows, ROW_ALIGN)
    return pltpu.make_async_copy(src_hbm.at[pl.ds(src_row, n_rows)], dst_hbm.at[pl.ds(dst_row, n_rows)], sem)


def _expert_kernel(te_ref, used_ref, toff_ref, ilo_ref, ihi_ref, cnt_ref, carry_ref, loff_ref,
                   hs_hbm, w1_hbm, w3_hbm, w2_hbm, y_ref, wb1, wb3, wb2, stage, sem, xbuf, xsem, rows_smem,
                   *, srows):
    t = pl.program_id(0)
    e = te_ref[t]
    tg = xbuf.shape[1]
    slot = t % 2
    first_of_expert = jnp.logical_or(t == 0, e != te_ref[jnp.maximum(t - 1, 0)])

    def fetch(tile, into):
        xbuf[into] = jnp.zeros(xbuf.shape[1:], xbuf.dtype)
        expert = te_ref[tile]
        first_row = toff_ref[tile]

        def piece(i, total):
            s = i * N_EXPERTS + expert
            run_start = carry_ref[s]
            lo = jnp.maximum(run_start, first_row)
            hi = jnp.minimum(run_start + cnt_ref[s], first_row + tg)

            @pl.when(hi > lo)
            def _():
                _segment_copy(hs_hbm, xbuf.at[into], i * srows + loff_ref[s] + (lo - run_start), lo - first_row,
                              hi - lo, xsem.at[into]).start()

            return total + jnp.maximum(hi - lo, 0)

        rows_smem[into] = lax.fori_loop(ilo_ref[tile], ihi_ref[tile], piece, 0)

    @pl.when(t == 0)
    def _():
        fetch(t, slot)

    @pl.when(t + 1 < pl.num_programs(0))
    def _():
        fetch(t + 1, 1 - slot)

    @pl.when(rows_smem[slot] > 0)
    def _():
        _segment_copy(hs_hbm, xbuf.at[slot], 0, 0, rows_smem[slot], xsem.at[slot]).wait()

    @pl.when(jnp.logical_and(used_ref[t] == 1, first_of_expert))
    def _():
        n_slots = stage.shape[0]
        windows = [(src, dst, r, c) for src, dst in ((w1_hbm, wb1), (w3_hbm, wb3), (w2_hbm, wb2))
                   for r in range(dst.shape[0] // W_CHUNK) for c in range(dst.shape[1] // W_CHUNK)]

        def staged_copy(k):
            src, _, r, c = windows[k]
            return pltpu.make_async_copy(src.at[e, pl.ds(r * W_CHUNK, W_CHUNK), pl.ds(c * W_CHUNK, W_CHUNK)],
                                         stage.at[k % n_slots], sem.at[k % n_slots])

        for k in range(n_slots - 1):
            staged_copy(k).start()
        for k, (_, dst, r, c) in enumerate(windows):
            staged_copy(k).wait()
            dst[r * W_CHUNK:(r + 1) * W_CHUNK, c * W_CHUNK:(c + 1) * W_CHUNK] = stage[k % n_slots].astype(BF16)
            if k + n_slots - 1 < len(windows):
                staged_copy(k + n_slots - 1).start()

    @pl.when(used_ref[t] == 1)
    def _():
        lo, hi = _unpack_bf16_pair(xbuf[slot])
        h = jnp.concatenate([lo.astype(BF16), hi.astype(BF16)], axis=1)
        a = jnp.dot(h, wb1[...], preferred_element_type=F32)
        b = jnp.dot(h, wb3[...], preferred_element_type=F32)
        hid = ((a * _sigmoid(a)) * b).astype(BF16)
        acc = jnp.dot(hid, wb2[...], preferred_element_type=F32)
        half = acc.shape[1] // 2
        y_ref[...] = _pack_bf16_pair(acc[:, :half], acc[:, half:])

    @pl.when(used_ref[t] == 0)
    def _():
        y_ref[...] = jnp.zeros(y_ref.shape, y_ref.dtype)


def _expert_ffn(tile_tables, seg_tables, hs, w1, w3, w2, n_tiles):
    half = hs.shape[1]
    d = 2 * half
    dff = w1.shape[2]
    tg = TG_MOE
    assert dff % W_CHUNK == 0 and d % W_CHUNK == 0
    hbm = pl.BlockSpec(memory_space=pl.ANY)
    tables = tuple(tile_tables) + tuple(seg_tables)
    return pl.pallas_call(
        functools.partial(_expert_kernel, srows=SORT_ROWS),
        out_shape=jax.ShapeDtypeStruct((n_tiles * tg, half), jnp.uint32),
        grid_spec=pltpu.PrefetchScalarGridSpec(
            num_scalar_prefetch=len(tables),
            grid=(n_tiles,),
            in_specs=[hbm, hbm, hbm, hbm],
            out_specs=pl.BlockSpec((tg, half), lambda t, *_: (t, 0)),
            scratch_shapes=[pltpu.VMEM((d, dff), BF16), pltpu.VMEM((d, dff), BF16), pltpu.VMEM((dff, d), BF16),
                            pltpu.VMEM((W_SLOTS, W_CHUNK, W_CHUNK), F32), pltpu.SemaphoreType.DMA((W_SLOTS,)),
                            pltpu.VMEM((2, tg, half), jnp.uint32), pltpu.SemaphoreType.DMA((2,)),
                            pltpu.SMEM((2,), jnp.int32)],
        ),
        compiler_params=_cparams("arbitrary"),
        name="expert_swiglu",
    )(*tables, hs, w1, w3, w2)


def _combine_kernel(src_ref, loff_ref, cnt_ref, rows_ref, x_ref, ei_ref, pw_ref, mod_ref, y_ref, o_ref,
                    ybuf, sem, *, tm, srows):
    i = pl.program_id(0)
    slot = i % 2

    def fetch(tile, into):
        ybuf[into] = jnp.zeros(ybuf.shape[1:], ybuf.dtype)
        for e in range(N_EXPERTS):
            s = tile * N_EXPERTS + e
            n_rows = cnt_ref[s]

            @pl.when(n_rows > 0)
            def _():
                _segment_copy(y_ref, ybuf.at[into], src_ref[s], loff_ref[s], n_rows, sem.at[into]).start()

    @pl.when(i == 0)
    def _():
        fetch(i, slot)

    @pl.when(i + 1 < pl.num_programs(0))
    def _():
        fetch(i + 1, 1 - slot)

    @pl.when(rows_ref[i] > 0)
    def _():
        _segment_copy(y_ref, ybuf.at[slot], 0, 0, rows_ref[i], sem.at[slot]).wait()

    lo, hi = _unpack_bf16_pair(ybuf[slot])
    ys = jnp.concatenate([lo.astype(BF16), hi.astype(BF16)], axis=1)
    r_idx = lax.broadcasted_iota(jnp.int32, (tm, srows), 1)
    mix = jnp.zeros(x_ref.shape, F32)
    for k in range(TOP_K):
        pick = jnp.where(r_idx == ei_ref[:, TOP_K + k:TOP_K + k + 1], 1.0, 0.0).astype(BF16)
        mix = mix + pw_ref[:, k:k + 1] * jnp.dot(pick, ys, preferred_element_type=F32)
    o_ref[...] = x_ref[...] + mod_ref[5:6, :] * mix


def _combine(seg_src, seg_loff, seg_cnt, tile_rows, x, ei, pw, mod, y, seq):
    n, d = x.shape
    tm = TM_ROUTE
    tpb = seq // tm
    tok = lambda width: pl.BlockSpec((tm, width), lambda i, *_: (i, 0))
    return pl.pallas_call(
        functools.partial(_combine_kernel, tm=tm, srows=SORT_ROWS),
        out_shape=jax.ShapeDtypeStruct((n, d), F32),
        grid_spec=pltpu.PrefetchScalarGridSpec(
            num_scalar_prefetch=4,
            grid=(n // tm,),
            in_specs=[tok(d), tok(LANES), tok(LANES),
                      pl.BlockSpec((None, 6, d), lambda i, *_: (i // tpb, 0, 0)),
                      pl.BlockSpec(memory_space=pl.ANY)],
            out_specs=tok(d),
            scratch_shapes=[pltpu.VMEM((2, SORT_ROWS, d // 2), jnp.uint32), pltpu.SemaphoreType.DMA((2,))],
        ),
        compiler_params=_cparams("arbitrary"),
        name="moe_combine",
    )(seg_src, seg_loff, seg_cnt, tile_rows, x, ei, pw, mod, y)


def _pack_w_in(w_in):
    d = w_in.shape[0]
    w = MIX_W
    o_ckv = 3 * w + Q_LORA
    o_kr = o_ckv + KV_LORA
    o_su = o_kr + QK_ROPE
    o_rq = o_su + 2 * w
    o_gate = o_rq + 4 * w
    half = HEAD_DIM // 2
    perm = np.array([h * HEAD_DIM + part * half + i
                     for part in range(2) for h in range(N_GROUPS) for i in range(half)])
    kr = w_in[:, o_kr:o_kr + QK_ROPE]
    z = lambda k: jnp.zeros((d, k), w_in.dtype)
    cols = [
        w_in[:, o_gate:o_gate + N_BRANCH * d],
        w_in[:, 0:3 * w + Q_LORA],
        w_in[:, o_rq:o_rq + w][:, perm], w_in[:, o_rq + w:o_rq + 2 * w][:, perm],
        w_in[:, o_rq + 2 * w:o_rq + 4 * w],
        w_in[:, o_su:o_su + 2 * w],
        w_in[:, o_ckv:o_ckv + KV_LORA],
        z(QK_NOPE), kr, z(LANES - QK_HEAD),
    ]
    return jnp.concatenate(cols, axis=1).astype(BF16)


def _swap_rope_halves(a):
    hr = QK_ROPE // 2
    return jnp.concatenate([a[..., :QK_NOPE], a[..., QK_NOPE + hr:QK_HEAD], a[..., QK_NOPE:QK_NOPE + hr],
                            a[..., QK_HEAD:]], axis=-1)


def _mla_params(cq_g, w_uq, ckv_g, w_ukv, qn_g, kn_g):
    pad = LANES - QK_HEAD
    wq = w_uq.reshape(Q_LORA, N_GROUPS, QK_HEAD)
    wq = jnp.pad(wq, ((0, 0), (0, 0), (0, pad)))
    wkv = w_ukv.reshape(KV_LORA, N_GROUPS, QK_NOPE + V_HEAD)
    wk = jnp.pad(wkv[:, :, :QK_NOPE], ((0, 0), (0, 0), (0, LANES - QK_NOPE)))
    wv = jnp.pad(wkv[:, :, QK_NOPE:], ((0, 0), (0, 0), (0, LANES - V_HEAD)))
    qg = jnp.pad(qn_g, (0, pad))[None, :]
    kg = jnp.pad(kn_g, (0, pad))[None, :]
    bound = (QK_HEAD ** 0.5 * LOG2_E) * jnp.max(jnp.abs(qn_g)) * jnp.max(jnp.abs(kn_g))
    static_shift = bound <= MAX_STATIC_SHIFT
    lane = jnp.arange(LANES)
    qaug = (lane == QK_HEAD).astype(F32)[None, :]
    kaug = qaug * jnp.where(static_shift, -bound, 0.0)
    vaug = jnp.tile((lane == V_HEAD).astype(F32), N_GROUPS)[None, :]
    params = {
        "cq_g": cq_g[None, :], "ckv_g": ckv_g[None, :],
        "wqa": wq.reshape(Q_LORA, -1).astype(BF16),
        "wqb": _swap_rope_halves(wq).reshape(Q_LORA, -1).astype(BF16),
        "wk": wk.reshape(KV_LORA, -1).astype(BF16),
        "wv": wv.reshape(KV_LORA, -1).astype(BF16),
        "qga": qg, "qgb": _swap_rope_halves(qg), "kga": kg, "kgb": _swap_rope_halves(kg),
        "qaug": qaug, "kaug": kaug, "vaug": vaug,
        "swap": (_swap_rope_halves(lane[None, :])[0][None, :] == lane[:, None]).astype(BF16),
    }
    return params, static_shift


def _mixer_consts():
    h = jnp.arange(N_GROUPS, dtype=F32)
    log_gamma = jnp.log1p(-(2.0 ** (-5.0 - h)))
    pos = jnp.arange(CHUNK, dtype=F32)
    rel = pos[:, None] - pos[None, :]
    dec = jnp.where(rel >= 0, jnp.exp(log_gamma[:, None, None] * jnp.maximum(rel, 0.0)), 0.0)
    lane = np.arange(MIX_W)
    head_k = (lane % LANES) // (HEAD_DIM // 2)
    head_v = lane // HEAD_DIM
    lg_k = log_gamma[head_k]
    return {
        "dec": dec.reshape(N_GROUPS * CHUNK, CHUNK),
        "kdec": jnp.exp(lg_k[None, :] * (CHUNK - 1.0 - pos)[:, None]),
        "qdec": jnp.exp(lg_k[None, :] * (pos + 1.0)[:, None]),
        "cdec": jnp.broadcast_to(jnp.exp(lg_k * CHUNK)[:, None], (MIX_W, MIX_W)),
        "bd": jnp.asarray((head_k[:, None] == head_v[None, :]).astype(np.float32)),
        "gmat": jnp.asarray((head_v[:, None] == head_v[None, :]).astype(np.float32) / HEAD_DIM).astype(BF16),
        "mk": jnp.asarray((head_k[None, :] == np.arange(N_GROUPS)[:, None]).astype(np.float32)),
        "mv": jnp.asarray((head_v[None, :] == np.arange(N_GROUPS)[:, None]).astype(np.float32)),
    }


def _mixer_params(conv_w, gv_g, w_s, b_s, ret_g, w_branch, w_o):
    p = dict(_mixer_consts())
    ws = jnp.tril(w_s)
    p.update({
        "conv_w": conv_w,
        "gv_g": gv_g.reshape(1, MIX_W),
        "ws_cat": jnp.transpose(ws, (1, 0, 2)).reshape(CHUNK, N_GROUPS * CHUNK).astype(BF16),
        "bs_mat": jnp.repeat(b_s.T, HEAD_DIM, axis=1),
        "ret_g": ret_g.reshape(1, MIX_W),
        "w_branch": w_branch.astype(BF16),
        "w_o": w_o.astype(BF16),
    })
    return p


def _moe_layout(meta, tot, n_tiles):
    totals = tot[0, :N_EXPERTS]
    padded = ((totals + TG_MOE - 1) // TG_MOE) * TG_MOE
    ends = jnp.cumsum(padded)
    starts = ends - padded
    seg_cnt = meta[:, 0, :N_EXPERTS]
    seg_carry = meta[:, 1, :N_EXPERTS]
    seg_loff = meta[:, 2, :N_EXPERTS]
    seg_grouped = starts[None, :] + seg_carry
    tile_start = jnp.arange(n_tiles, dtype=jnp.int32) * TG_MOE
    tile_e = jnp.sum((tile_start[:, None] >= ends[None, :]).astype(jnp.int32), axis=1)
    used = (tile_start < ends[-1]).astype(jnp.int32)
    last_e = jnp.sum((ends[-1] - 1 >= ends).astype(jnp.int32))
    tile_e = jnp.minimum(jnp.where(used == 1, tile_e, last_e), N_EXPERTS - 1)
    tile_off = tile_start - starts[tile_e]
    run_start = seg_carry[:, tile_e]
    run_end = run_start + seg_cnt[:, tile_e]
    tile_ilo = jnp.sum((run_end <= tile_off[None, :]).astype(jnp.int32), axis=0) * used
    tile_ihi = jnp.sum((run_start < tile_off[None, :] + TG_MOE).astype(jnp.int32), axis=0) * used
    flat = lambda a: a.reshape(-1).astype(jnp.int32)
    tile_tables = (flat(tile_e), flat(used), flat(tile_off * used), flat(tile_ilo), flat(tile_ihi))
    seg_tables = (flat(seg_cnt), flat(seg_carry), flat(seg_loff))
    return tile_tables, seg_tables, flat(seg_grouped)


def kernel(x, c, positions, norm1_g, norm2_g, ada_w, ada_b, w_in, conv_w, cq_g, w_uq, ckv_g, w_ukv, qn_g, kn_g, gv_g, w_s, b_s, ret_g, w_branch, w_o, ffn_w1, ffn_w3, ffn_w2, router_w, router_b, moe_w1, moe_w3, moe_w2):
    batch, seq, d = x.shape
    depth = ada_w.shape[0]
    n = batch * seq
    assert seq % max(TM_PROJ, TM_PREP, TQ_ATT, TM_MIX, TM_FFN, TM_ROUTE) == 0
    assert d // 2 % LANES == 0

    c_t = jnp.pad(c, ((0, 8 - batch), (0, 0))).T
    ada = _ada(c_t, ada_w, ada_b, batch)[:, :batch].reshape(depth, batch, 6, d)
    cosr, sinr, cm, sm = _rope_tables(positions.astype(F32).reshape(n, 1))

    xt = x.reshape(n, d)
    for l in range(depth):
        mod = ada[l]
        proj = _inproj(xt, mod, norm1_g[l][None, :], _pack_w_in(w_in[l]), seq)
        mla_p, static_shift = _mla_params(cq_g[l], w_uq[l], ckv_g[l], w_ukv[l], qn_g[l], kn_g[l])
        q, k, v = _mla_prep(proj, cm, sm, mla_p)
        y_mla = lax.cond(static_shift,
                         functools.partial(_flash, batch=batch, seq=seq, online_max=False),
                         functools.partial(_flash, batch=batch, seq=seq, online_max=True), q, k, v)
        mp = _mixer_params(conv_w[l], gv_g[l], w_s[l], b_s[l], ret_g[l], w_branch[l], w_o[l])
        xt = _mixers(proj, y_mla, xt, cosr, sinr, mod, mp, seq)
        g2n = norm2_g[l][None, :]
        if l % 2 == 0:
            i = l // 2
            xt = _dense_ffn(xt, mod, g2n, ffn_w1[i].astype(BF16), ffn_w3[i].astype(BF16),
                            ffn_w2[i].astype(BF16), seq)
        else:
            i = l // 2
            rw = jnp.pad(router_w[i], ((0, 0), (0, LANES - N_EXPERTS)))
            rw_hi = rw.astype(BF16)
            rw_pad = jnp.concatenate([rw_hi, (rw - rw_hi.astype(F32)).astype(BF16)], axis=1)
            rb_pad = jnp.pad(router_b[i], (0, LANES - N_EXPERTS), constant_values=-1e30)[None, :]
            hs, ei, pw, meta, tot = _router(xt, mod, g2n, rw_pad, rb_pad, seq)
            max_rows = n * TOP_K + N_EXPERTS * (n // TM_ROUTE) * (ROW_ALIGN - 1)
            n_tiles = -(-max_rows // TG_MOE) + N_EXPERTS
            tile_tables, seg_tables, seg_grouped = _moe_layout(meta, tot, n_tiles)
            seg_cnt, _, seg_loff = seg_tables
            tile_rows = jnp.sum(seg_cnt.reshape(-1, N_EXPERTS), axis=1)
            y = _expert_ffn(tile_tables, seg_tables, hs, moe_w1[i], moe_w3[i], moe_w2[i], n_tiles)
            xt = _combine(seg_grouped, seg_loff, seg_cnt, tile_rows, xt, ei, pw, mod, y, seq)
    return xt.reshape(batch, seq, d)
```

```python
import functools

import jax
import jax.numpy as jnp
import numpy as np
from jax import lax
from jax.experimental import pallas as pl
from jax.experimental.pallas import tpu as pltpu

F32 = jnp.float32
BF16 = jnp.bfloat16
HIGHEST = lax.Precision.HIGHEST

HEAD_DIM = 64
N_GROUPS = 4
MIX_W = N_GROUPS * HEAD_DIM
N_BRANCH = 4
CONV_W = 3
Q_LORA = 256
KV_LORA = 128
QK_NOPE = 64
QK_ROPE = 32
QK_HEAD = QK_NOPE + QK_ROPE
V_HEAD = 64
CHUNK = 128
N_EXPERTS = 8
TOP_K = 2
ROPE_THETA = 10000.0
EPS = 1e-6
LOG2_E = 1.4426950408889634
MAX_STATIC_SHIFT = 50.0

LANES = 128
VMEM_LIMIT_BYTES = 56 * 1024 * 1024

COL_GATES = 0
COL_A = 4096
COL_CQ = COL_A + 3 * MIX_W
COL_R = 5120
COL_SU = 6144
COL_CKV = 6656
COL_KRA = 6784
N_IN = 6912

TM_PROJ = 512
TN_PROJ = 768
TQ_ATT = 1024
TM_MIX = 512
TM_FFN = 512
TM_ROUTE = 512
ROW_ALIGN = 8
SORT_ROWS = TOP_K * TM_ROUTE + N_EXPERTS * ROW_ALIGN
TG_MOE = 512
W_CHUNK = 512
W_SLOTS = 6


def _cparams(*sem):
    return pltpu.CompilerParams(dimension_semantics=sem, vmem_limit_bytes=VMEM_LIMIT_BYTES)


def _sigmoid(x):
    return jnp.tanh(x * 0.5) * 0.5 + 0.5


def _group_mean(x, gmat_bf16):
    hi = x.astype(BF16)
    lo = (x - hi.astype(F32)).astype(BF16)
    return (jnp.dot(hi, gmat_bf16, preferred_element_type=F32)
            + jnp.dot(lo, gmat_bf16, preferred_element_type=F32))


def _pack_bf16_pair(lo, hi):
    lo_bits = lax.bitcast_convert_type(lo.astype(BF16).astype(F32), jnp.uint32)
    hi_bits = lax.bitcast_convert_type(hi.astype(BF16).astype(F32), jnp.uint32)
    return (lo_bits >> 16) | (hi_bits & jnp.uint32(0xFFFF0000))


def _unpack_bf16_pair(p):
    lo = lax.bitcast_convert_type(p << 16, F32)
    hi = lax.bitcast_convert_type(p & jnp.uint32(0xFFFF0000), F32)
    return lo, hi


def _norm_mod(x, g, shift, scale):
    y = x * lax.rsqrt(jnp.mean(x * x, axis=-1, keepdims=True) + EPS)
    return (y * g) * (1.0 + scale) + shift


def _ada_kernel(ct_ref, w_ref, b_ref, o_ref, *, batch):
    ct = ct_ref[...]
    cond = ct * _sigmoid(ct)
    w = w_ref[...]
    o_ref[...] = jnp.zeros(o_ref.shape, F32)
    for b in range(batch):
        o_ref[b:b + 1, :] = jnp.sum(w * cond[:, b:b + 1], axis=0, keepdims=True) + b_ref[...]


def _ada(c_t, ada_w, ada_b, batch):
    n_layer, d, d6 = ada_w.shape
    rows = c_t.shape[1]
    tn = 2048
    return pl.pallas_call(
        functools.partial(_ada_kernel, batch=batch),
        out_shape=jax.ShapeDtypeStruct((n_layer, rows, d6), F32),
        grid=(n_layer, d6 // tn),
        in_specs=[
            pl.BlockSpec((d, rows), lambda l, j: (0, 0)),
            pl.BlockSpec((None, d, tn), lambda l, j: (l, 0, j)),
            pl.BlockSpec((None, 1, tn), lambda l, j: (l, 0, j)),
        ],
        out_specs=pl.BlockSpec((None, rows, tn), lambda l, j: (l, 0, j)),
        compiler_params=_cparams("parallel", "parallel"),
        name="ada_mod",
    )(c_t, ada_w, ada_b.reshape(n_layer, 1, d6))


def _rope_kernel(pos_ref, inv_ref, cr_ref, sr_ref, cm_ref, sm_ref):
    half_r = HEAD_DIM // 2
    half_m = QK_ROPE // 2
    tm = pos_ref.shape[0]
    low = lax.broadcasted_iota(jnp.int32, (tm // 2, LANES), 1) < LANES // 2
    ang = jnp.where(low, pos_ref[0:tm // 2, :], pos_ref[tm // 2:tm, :]) * inv_ref[...]
    c = jnp.cos(ang)
    s = jnp.sin(ang)
    c = jnp.concatenate([c, pltpu.roll(c, LANES // 2, axis=1)], axis=0)
    s = jnp.concatenate([s, pltpu.roll(s, LANES // 2, axis=1)], axis=0)
    lane = lax.broadcasted_iota(jnp.int32, c.shape, 1)

    def tile_r(t):
        t = jnp.where(lane < half_r, t, 0.0)
        out = t
        for k in range(1, LANES // half_r):
            out = out + pltpu.roll(t, k * half_r, axis=1)
        return out

    cr_ref[...] = tile_r(c)
    sr_ref[...] = tile_r(s)
    first = jnp.logical_and(lane >= QK_NOPE, lane < QK_NOPE + half_m)
    second = jnp.logical_and(lane >= QK_NOPE + half_m, lane < QK_HEAD)
    c1, c2 = pltpu.roll(c, QK_NOPE - half_r, axis=1), pltpu.roll(c, QK_NOPE + half_m - half_r, axis=1)
    s1, s2 = pltpu.roll(s, QK_NOPE - half_r, axis=1), pltpu.roll(s, QK_NOPE + half_m - half_r, axis=1)
    cm_ref[...] = jnp.where(first, c1, jnp.where(second, c2, 1.0))
    sm_ref[...] = jnp.where(first, -s1, jnp.where(second, s2, 0.0))


def _rope_tables(pos_f):
    n = pos_f.shape[0]
    tm = 1024
    half_r = HEAD_DIM // 2
    half_m = QK_ROPE // 2
    inv_r = ROPE_THETA ** (-jnp.arange(half_r, dtype=F32) / half_r)
    inv_m = ROPE_THETA ** (-jnp.arange(half_m, dtype=F32) / half_m)
    inv = jnp.concatenate([inv_r, inv_m, jnp.zeros((LANES // 2 - half_r - half_m,), F32)])
    inv = jnp.tile(inv, 2)[None, :]
    tab = pl.BlockSpec((tm, LANES), lambda i: (i, 0))
    shape = jax.ShapeDtypeStruct((n, LANES), F32)
    return pl.pallas_call(
        _rope_kernel,
        out_shape=(shape, shape, shape, shape),
        grid=(n // tm,),
        in_specs=[pl.BlockSpec((tm, 1), lambda i: (i, 0)), pl.BlockSpec((1, LANES), lambda i: (0, 0))],
        out_specs=(tab, tab, tab, tab),
        compiler_params=_cparams("parallel"),
        name="rope_tables",
    )(pos_f, inv)


def _resident(shape):
    return pl.BlockSpec(shape, lambda *_: (0,) * len(shape), pipeline_mode=pl.Buffered(1))


def _mla_prep(cq_b, ckv_b, kra_b, cm, sm, cqg_ref, wqa_ref, wqb_ref, ckvg_ref, wk_ref, wv_ref, qga_ref, qgb_ref,
              kga_ref, kgb_ref, qaug_ref, kaug_ref, vaug_ref, swap_ref, q_ref, k_ref, v_ref):
    cq = cq_b.astype(F32)
    cqn = (cq * lax.rsqrt(jnp.mean(cq * cq, axis=-1, keepdims=True) + EPS) * cqg_ref[...]).astype(BF16)
    qa = jnp.dot(cqn, wqa_ref[...], preferred_element_type=F32)
    qb = jnp.dot(cqn, wqb_ref[...], preferred_element_type=F32)
    ckv = ckv_b.astype(F32)
    ckvn = (ckv * lax.rsqrt(jnp.mean(ckv * ckv, axis=-1, keepdims=True) + EPS) * ckvg_ref[...]).astype(BF16)
    ka = jnp.dot(ckvn, wk_ref[...], preferred_element_type=F32)
    v_ref[...] = (jnp.dot(ckvn, wv_ref[...], preferred_element_type=F32) + vaug_ref[...]).astype(BF16)
    kra = kra_b.astype(F32)
    krb = jnp.dot(kra_b, swap_ref[...], preferred_element_type=F32)
    scale = QK_HEAD ** -0.5 * LOG2_E
    q_cos, q_sin = cm * (qga_ref[...] * scale), sm * (qgb_ref[...] * scale)
    k_cos, k_sin = cm * kga_ref[...], sm * kgb_ref[...]
    for h in range(N_GROUPS):
        sl = slice(h * LANES, (h + 1) * LANES)
        qah, qbh = qa[:, sl], qb[:, sl]
        r = lax.rsqrt(jnp.sum(qah * qah, axis=-1, keepdims=True) * (1.0 / QK_HEAD) + EPS)
        q_ref[:, sl] = ((qah * q_cos + qbh * q_sin) * r + qaug_ref[...]).astype(BF16)
        kah = ka[:, sl] + kra
        kbh = ka[:, sl] + krb
        r = lax.rsqrt(jnp.sum(kah * kah, axis=-1, keepdims=True) * (1.0 / QK_HEAD) + EPS)
        k_ref[:, sl] = ((kah * k_cos + kbh * k_sin) * r + kaug_ref[...]).astype(BF16)


def _inproj_kernel(x_ref, mod_ref, g_ref, w_ref, cm_ref, sm_ref, *rest):
    mla_refs, (o_ref, q_ref, k_ref, v_ref) = rest[:-4], rest[-4:]
    h = _norm_mod(x_ref[...], g_ref[...], mod_ref[0:1, :], mod_ref[1:2, :]).astype(BF16)

    def chunk(c):
        cols = slice(c * TN_PROJ, (c + 1) * TN_PROJ)
        out = jnp.dot(h, w_ref[:, cols], preferred_element_type=F32).astype(BF16)
        o_ref[:, cols] = out
        return out

    c_q, c_kv = COL_CQ // TN_PROJ, COL_CKV // TN_PROJ
    lat_q = chunk(c_q)
    lat_kv = chunk(c_kv)
    q0, kv0, kr0 = COL_CQ - c_q * TN_PROJ, COL_CKV - c_kv * TN_PROJ, COL_KRA - c_kv * TN_PROJ
    _mla_prep(lat_q[:, q0:q0 + Q_LORA], lat_kv[:, kv0:kv0 + KV_LORA], lat_kv[:, kr0:kr0 + LANES],
              cm_ref[...], sm_ref[...], *mla_refs, q_ref, k_ref, v_ref)
    for c in range(N_IN // TN_PROJ):
        if c not in (c_q, c_kv):
            chunk(c)


def _inproj(x, mod, g, w, cm, sm, p, seq):
    n, d = x.shape
    tm = TM_PROJ
    tpb = seq // tm
    hw = N_GROUPS * LANES
    assert COL_KRA // TN_PROJ == COL_CKV // TN_PROJ and (COL_CQ + Q_LORA - 1) // TN_PROJ == COL_CQ // TN_PROJ

    def full(a):
        return pl.BlockSpec(a.shape, lambda i: (0,) * a.ndim)

    weights = [p["cq_g"], p["wqa"], p["wqb"], p["ckv_g"], p["wk"], p["wv"],
               p["qga"], p["qgb"], p["kga"], p["kgb"], p["qaug"], p["kaug"], p["vaug"], p["swap"]]
    table = pl.BlockSpec((tm, LANES), lambda i: (i, 0))
    head_tile = pl.BlockSpec((tm, hw), lambda i: (i, 0))
    heads = jax.ShapeDtypeStruct((n, hw), BF16)
    return pl.pallas_call(
        _inproj_kernel,
        out_shape=(jax.ShapeDtypeStruct((n, N_IN), BF16), heads, heads, heads),
        grid=(n // tm,),
        in_specs=[
            pl.BlockSpec((tm, d), lambda i: (i, 0)),
            pl.BlockSpec((None, 6, d), lambda i: (i // tpb, 0, 0)),
            pl.BlockSpec((1, d), lambda i: (0, 0)),
            _resident((d, N_IN)), table, table,
        ] + [full(a) for a in weights],
        out_specs=(pl.BlockSpec((tm, N_IN), lambda i: (i, 0)), head_tile, head_tile, head_tile),
        compiler_params=_cparams("parallel"),
        name="in_proj",
    )(x, mod, g, w, cm, sm, *weights)


def _flash_kernel(qi_ref, kj_ref, q_ref, k_ref, v_ref, o_ref, acc_scr, *rest, tq, online_max):
    i = qi_ref[pl.program_id(1)]
    j = kj_ref[pl.program_id(1)]

    @pl.when(j == 0)
    def _():
        acc_scr[...] = jnp.zeros(acc_scr.shape, F32)
        if online_max:
            rest[0][...] = jnp.full(rest[0].shape, -jnp.inf, F32)

    def block(q0, nq, nk, masked):
        rows = slice(q0, q0 + nq)
        if masked:
            row = lax.broadcasted_iota(jnp.int32, (nq, nk), 0) + q0
            col = lax.broadcasted_iota(jnp.int32, (nq, nk), 1)
            keep = col <= row
        for h in range(N_GROUPS):
            sl = slice(h * LANES, (h + 1) * LANES)
            s = lax.dot_general(q_ref[rows, sl], k_ref[0:nk, sl], (((1,), (1,)), ((), ())),
                                preferred_element_type=F32)
            if masked:
                s = jnp.where(keep, s, -jnp.inf)
            if online_max:
                m_scr = rest[0]
                m_prev = m_scr[h, rows]
                m_new = jnp.maximum(m_prev, jnp.max(s, axis=-1, keepdims=True))
                p = jnp.exp2(s - m_new).astype(BF16)
                acc_scr[h, rows] = jnp.exp2(m_prev - m_new) * acc_scr[h, rows] + jnp.dot(
                    p, v_ref[0:nk, sl], preferred_element_type=F32)
                m_scr[h, rows] = m_new
            else:
                acc_scr[h, rows] += jnp.dot(jnp.exp2(s).astype(BF16), v_ref[0:nk, sl],
                                            preferred_element_type=F32)

    @pl.when(j < i)
    def _():
        block(0, tq, tq, False)

    @pl.when(j == i)
    def _():
        block(0, tq // 2, tq // 2, True)
        block(tq // 2, tq // 2, tq, True)
        lane = lax.broadcasted_iota(jnp.int32, (tq, LANES), 1)
        for pr in range(N_GROUPS // 2):
            lo = acc_scr[2 * pr]
            hi = acc_scr[2 * pr + 1]
            lo = lo / lo[:, V_HEAD:V_HEAD + 1]
            hi = hi / hi[:, V_HEAD:V_HEAD + 1]
            both = jnp.where(lane < V_HEAD, lo, pltpu.roll(hi, V_HEAD, axis=1))
            o_ref[:, pr * LANES:(pr + 1) * LANES] = both.astype(BF16)


def _flash(q, k, v, batch, seq, online_max):
    n = q.shape[0]
    tq = TQ_ATT
    nq = seq // tq
    hw = N_GROUPS * LANES
    scratch = [pltpu.VMEM((N_GROUPS, tq, LANES), F32)]
    if online_max:
        scratch.append(pltpu.VMEM((N_GROUPS, tq, 1), F32))
    pairs = [(i, j) for i in range(nq) for j in range(i + 1)]
    qi = jnp.asarray([p[0] for p in pairs], jnp.int32)
    kj = jnp.asarray([p[1] for p in pairs], jnp.int32)
    q_tile = lambda b, s, qi, kj: (b * nq + qi[s], 0)
    k_tile = lambda b, s, qi, kj: (b * nq + kj[s], 0)
    return pl.pallas_call(
        functools.partial(_flash_kernel, tq=tq, online_max=online_max),
        out_shape=jax.ShapeDtypeStruct((n, MIX_W), BF16),
        grid_spec=pltpu.PrefetchScalarGridSpec(
            num_scalar_prefetch=2,
            grid=(batch, len(pairs)),
            in_specs=[pl.BlockSpec((tq, hw), q_tile), pl.BlockSpec((tq, hw), k_tile),
                      pl.BlockSpec((tq, hw), k_tile)],
            out_specs=pl.BlockSpec((tq, MIX_W), q_tile),
            scratch_shapes=scratch,
        ),
        compiler_params=_cparams("parallel", "arbitrary"),
        name="mla_flash_online" if online_max else "mla_flash",
    )(qi, kj, q, k, v)


def _gelu_tanh(x):
    return jax.nn.gelu(x, approximate=True)


def _mix_kernel(gates_ref, a_ref, r_ref, su_ref, ymla_ref, x_ref, cos_ref, sin_ref, mod_ref,
                convw_ref, gvg_ref, wscat_ref, bsmat_ref, retg_ref, dec_ref, kdec_ref, qdec_ref,
                cdec_ref, bd_ref, gmat_ref, mk_ref, mv_ref, wb_ref, wo_ref,
                o_ref, carry_scr, state_scr, ysg_scr, yret_scr, *, tm, tpb):
    i = pl.program_id(0)

    @pl.when(i % tpb == 0)
    def _():
        carry_scr[...] = jnp.zeros(carry_scr.shape, F32)
        state_scr[...] = jnp.zeros(state_scr.shape, F32)

    w = MIX_W
    a_b = a_ref[:, 0:w].astype(F32)
    u = a_ref[:, w:2 * w].astype(F32) * a_ref[:, 2 * w:3 * w].astype(F32)
    rowi = lax.broadcasted_iota(jnp.int32, (tm, w), 0)
    prev1 = carry_scr[0:1, :]
    prev2 = carry_scr[1:2, :]
    u1 = jnp.where(rowi == 0, prev1, pltpu.roll(u, 1, axis=0))
    u2 = jnp.where(rowi == 0, prev2, jnp.where(rowi == 1, prev1, pltpu.roll(u, 2, axis=0)))
    carry_scr[0:1, :] = u[tm - 1:tm, :]
    carry_scr[1:2, :] = u[tm - 2:tm - 1, :]
    y_conv = a_b * (convw_ref[0:1, :] * u2 + convw_ref[1:2, :] * u1 + convw_ref[2:3, :] * u)

    gmat = gmat_ref[...]
    s_u = _gelu_tanh(su_ref[:, 0:w].astype(F32))
    s_v = _gelu_tanh(su_ref[:, w:2 * w].astype(F32))
    ms = _group_mean(s_v * s_v, gmat)
    vn = (s_v * lax.rsqrt(ms + EPS) * gvg_ref[...]).astype(BF16)

    cosr = cos_ref[...]
    sinr = sin_ref[...]

    def rot(t):
        t1, t2 = t[:, 0:LANES], t[:, LANES:2 * LANES]
        return jnp.concatenate([t1 * cosr - t2 * sinr, t2 * cosr + t1 * sinr], axis=-1)

    rq = rot(r_ref[:, 0:w].astype(F32))
    rk = rot(r_ref[:, w:2 * w].astype(F32)) * (HEAD_DIM ** -0.5)

    for c in range(tm // CHUNK):
        rows = slice(c * CHUNK, (c + 1) * CHUNK)
        vc = vn[rows, :]
        vbd = jnp.concatenate([vc * mv_ref[g:g + 1, :].astype(BF16) for g in range(N_GROUPS)], axis=0)
        mixed = jnp.dot(wscat_ref[...], vbd, preferred_element_type=F32) + bsmat_ref[...]
        ysg_scr[rows, :] = s_u[rows, :] * mixed

        qc = rq[rows, :]
        kc = rk[rows, :]
        kcb = kc.astype(BF16)
        vcb = r_ref[rows, 2 * w:3 * w]
        qstack = jnp.concatenate([(qc * mk_ref[h:h + 1, :]).astype(BF16) for h in range(N_GROUPS)], axis=0)
        sc = lax.dot_general(qstack, kcb, (((1,), (1,)), ((), ())), preferred_element_type=F32)
        sc = (sc * dec_ref[...]).astype(BF16)
        scat = jnp.concatenate([sc[h * CHUNK:(h + 1) * CHUNK, :] for h in range(N_GROUPS)], axis=1)
        vstack = jnp.concatenate([vcb * mv_ref[h:h + 1, :].astype(BF16) for h in range(N_GROUPS)], axis=0)
        o_c = jnp.dot(scat, vstack, preferred_element_type=F32)
        state = state_scr[...]
        o_c = o_c + jnp.dot((qc * qdec_ref[...]).astype(BF16), state.astype(BF16),
                            preferred_element_type=F32)
        kd_t = jnp.transpose(kc * kdec_ref[...]).astype(BF16)
        kv = jnp.dot(kd_t, vcb, preferred_element_type=F32)
        state_scr[...] = state * cdec_ref[...] + kv * bd_ref[...]
        yret_scr[rows, :] = o_c

    o_all = yret_scr[...]
    xc = o_all - _group_mean(o_all, gmat)
    var = _group_mean(xc * xc, gmat)
    r_g = r_ref[:, 3 * w:4 * w].astype(F32)
    y_ret = (r_g * _sigmoid(r_g)) * (xc * lax.rsqrt(var + EPS) * retg_ref[...])

    d = x_ref.shape[1]
    ys = (y_conv, ymla_ref[...], ysg_scr[...], y_ret)
    merged = None
    for n in range(N_BRANCH):
        gate = _sigmoid(gates_ref[:, n * d:(n + 1) * d])
        term = gate * jnp.dot(ys[n].astype(BF16), wb_ref[n], preferred_element_type=F32).astype(BF16)
        merged = term if merged is None else merged + term
    out = jnp.dot(merged, wo_ref[...], preferred_element_type=F32)
    o_ref[...] = x_ref[...] + mod_ref[2:3, :] * out


def _mixers(proj, ymla, x, cosr, sinr, mod, p, seq):
    n, d = x.shape
    tm = TM_MIX
    tpb = seq // tm

    def col(width, offset):
        return pl.BlockSpec((tm, width), lambda i: (i, offset // width))

    def full(a):
        return pl.BlockSpec(a.shape, lambda i: (0,) * a.ndim)

    consts = [p["conv_w"], p["gv_g"], p["ws_cat"], p["bs_mat"], p["ret_g"], p["dec"], p["kdec"],
              p["qdec"], p["cdec"], p["bd"], p["gmat"], p["mk"], p["mv"], p["w_branch"], p["w_o"]]
    return pl.pallas_call(
        functools.partial(_mix_kernel, tm=tm, tpb=tpb),
        out_shape=jax.ShapeDtypeStruct((n, d), F32),
        grid=(n // tm,),
        in_specs=[col(N_BRANCH * d, COL_GATES), col(4 * MIX_W, COL_A), col(4 * MIX_W, COL_R),
                  col(2 * MIX_W, COL_SU),
                  pl.BlockSpec((tm, MIX_W), lambda i: (i, 0)),
                  pl.BlockSpec((tm, d), lambda i: (i, 0)),
                  pl.BlockSpec((tm, LANES), lambda i: (i, 0)),
                  pl.BlockSpec((tm, LANES), lambda i: (i, 0)),
                  pl.BlockSpec((None, 6, d), lambda i: (i // tpb, 0, 0))]
                 + [full(c) for c in consts],
        out_specs=pl.BlockSpec((tm, d), lambda i: (i, 0)),
        scratch_shapes=[pltpu.VMEM((8, MIX_W), F32), pltpu.VMEM((MIX_W, MIX_W), F32),
                        pltpu.VMEM((tm, MIX_W), F32), pltpu.VMEM((tm, MIX_W), F32)],
        compiler_params=_cparams("arbitrary"),
        name="mixers_merge",
    )(proj, proj, proj, proj, ymla, x, cosr, sinr, mod, *consts)


def _ffn_kernel(x_ref, mod_ref, g_ref, w1_ref, w3_ref, w2_ref, o_ref):
    x = x_ref[...]
    h = _norm_mod(x, g_ref[...], mod_ref[3:4, :], mod_ref[4:5, :]).astype(BF16)
    a = jnp.dot(h, w1_ref[...], preferred_element_type=F32)
    b = jnp.dot(h, w3_ref[...], preferred_element_type=F32)
    hid = ((a * _sigmoid(a)) * b).astype(BF16)
    o_ref[...] = x + mod_ref[5:6, :] * jnp.dot(hid, w2_ref[...], preferred_element_type=F32)


def _dense_ffn(x, mod, g, w1, w3, w2, seq):
    n, d = x.shape
    dff = w1.shape[1]
    tm = TM_FFN
    tpb = seq // tm
    return pl.pallas_call(
        _ffn_kernel,
        out_shape=jax.ShapeDtypeStruct((n, d), F32),
        grid=(n // tm,),
        in_specs=[
            pl.BlockSpec((tm, d), lambda i: (i, 0)),
            pl.BlockSpec((None, 6, d), lambda i: (i // tpb, 0, 0)),
            pl.BlockSpec((1, d), lambda i: (0, 0)),
            _resident((d, dff)), _resident((d, dff)), _resident((dff, d)),
        ],
        out_specs=pl.BlockSpec((tm, d), lambda i: (i, 0)),
        compiler_params=_cparams("parallel"),
        name="dense_swiglu",
    )(x, mod, g, w1, w3, w2)


def _router_kernel(x_ref, mod_ref, g_ref, rw_ref, rb_ref, hs_ref, ei_ref, pw_ref, meta_ref, tot_ref,
                   carry_scr, *, tm, srows):
    i = pl.program_id(0)

    @pl.when(i == 0)
    def _():
        carry_scr[...] = jnp.zeros(carry_scr.shape, F32)

    h = _norm_mod(x_ref[...], g_ref[...], mod_ref[3:4, :], mod_ref[4:5, :])

    h_hi = h.astype(BF16)
    h_lo = (h - h_hi.astype(F32)).astype(BF16)
    hw = jnp.dot(h_hi, rw_ref[...], preferred_element_type=F32)
    logits = (hw[:, :LANES] + hw[:, LANES:] + jnp.dot(h_lo, rw_ref[:, :LANES], preferred_element_type=F32)
              + rb_ref[...])
    mx = jnp.max(logits, axis=-1, keepdims=True)
    ex = jnp.exp(logits - mx)
    probs = ex / jnp.sum(ex, axis=-1, keepdims=True)
    lane = lax.broadcasted_iota(jnp.int32, (tm, LANES), 1)
    valid = lane < N_EXPERTS
    probs = jnp.where(valid, probs, -1.0)
    m1 = jnp.max(probs, axis=-1, keepdims=True)
    i1 = jnp.min(jnp.where(probs == m1, lane, LANES), axis=-1, keepdims=True)
    rest = jnp.where(lane == i1, -1.0, probs)
    m2 = jnp.max(rest, axis=-1, keepdims=True)
    i2 = jnp.min(jnp.where(rest == m2, lane, LANES), axis=-1, keepdims=True)
    den = m1 + m2
    pw_ref[...] = jnp.where(lane == 0, m1 / den, jnp.where(lane == 1, m2 / den, 0.0))

    sel1 = lane == i1
    sel2 = lane == i2
    onehot = jnp.where(sel1, 1.0, 0.0) + jnp.where(sel2, 1.0, 0.0)
    r_i = lax.broadcasted_iota(jnp.int32, (tm, tm), 0)
    c_i = lax.broadcasted_iota(jnp.int32, (tm, tm), 1)
    tri = jnp.where(c_i < r_i, 1.0, 0.0).astype(BF16)
    before = jnp.dot(tri, onehot.astype(BF16), preferred_element_type=F32)
    cnt = jnp.sum(onehot, axis=0, keepdims=True)
    cnt_al = jnp.floor((cnt + (ROW_ALIGN - 1)) * (1.0 / ROW_ALIGN)) * ROW_ALIGN
    e_r = lax.broadcasted_iota(jnp.int32, (LANES, LANES), 0)
    e_c = lax.broadcasted_iota(jnp.int32, (LANES, LANES), 1)
    upper = jnp.where(e_r < e_c, 1.0, 0.0)
    loff = jnp.dot(jnp.broadcast_to(cnt_al, (8, LANES)), upper, precision=HIGHEST,
                   preferred_element_type=F32)[0:1, :]
    slot = loff + before
    slot1 = jnp.sum(jnp.where(sel1, slot, 0.0), axis=-1, keepdims=True).astype(jnp.int32)
    slot2 = jnp.sum(jnp.where(sel2, slot, 0.0), axis=-1, keepdims=True).astype(jnp.int32)
    ei = jnp.where(lane == 0, i1, jnp.where(lane == 1, i2, 0))
    ei_ref[...] = jnp.where(lane == 2, slot1, jnp.where(lane == 3, slot2, ei))

    r_idx = lax.broadcasted_iota(jnp.int32, (tm, srows), 1)
    place = jnp.where(r_idx == slot1, 1.0, jnp.where(r_idx == slot2, 1.0, 0.0)).astype(BF16)
    hs = lax.dot_general(place, h.astype(BF16), (((0,), (0,)), ((), ())), preferred_element_type=F32)
    half = hs.shape[1] // 2
    hs_ref[...] = _pack_bf16_pair(hs[:, :half], hs[:, half:])

    carry = carry_scr[0:1, :]
    mrow = lax.broadcasted_iota(jnp.int32, (8, LANES), 0)
    meta = jnp.where(mrow == 0, cnt_al, jnp.where(mrow == 1, carry, jnp.where(mrow == 2, loff, 0.0)))
    meta_ref[...] = meta.astype(jnp.int32)
    carry_scr[0:1, :] = carry + cnt_al
    tot_ref[...] = jnp.broadcast_to(carry + cnt_al, tot_ref.shape).astype(jnp.int32)


def _router(x, mod, g, rw_pad, rb_pad, seq):
    n, d = x.shape
    tm = TM_ROUTE
    tpb = seq // tm
    nt = n // tm
    return pl.pallas_call(
        functools.partial(_router_kernel, tm=tm, srows=SORT_ROWS),
        out_shape=(jax.ShapeDtypeStruct((nt * SORT_ROWS, d // 2), jnp.uint32),
                   jax.ShapeDtypeStruct((n, LANES), jnp.int32),
                   jax.ShapeDtypeStruct((n, LANES), F32),
                   jax.ShapeDtypeStruct((nt, 8, LANES), jnp.int32),
                   jax.ShapeDtypeStruct((8, LANES), jnp.int32)),
        grid=(nt,),
        in_specs=[
            pl.BlockSpec((tm, d), lambda i: (i, 0)),
            pl.BlockSpec((None, 6, d), lambda i: (i // tpb, 0, 0)),
            pl.BlockSpec((1, d), lambda i: (0, 0)),
            pl.BlockSpec((d, 2 * LANES), lambda i: (0, 0)),
            pl.BlockSpec((1, LANES), lambda i: (0, 0)),
        ],
        out_specs=(pl.BlockSpec((SORT_ROWS, d // 2), lambda i: (i, 0)),
                   pl.BlockSpec((tm, LANES), lambda i: (i, 0)),
                   pl.BlockSpec((tm, LANES), lambda i: (i, 0)),
                   pl.BlockSpec((None, 8, LANES), lambda i: (i, 0, 0)),
                   pl.BlockSpec((8, LANES), lambda i: (0, 0))),
        scratch_shapes=[pltpu.VMEM((8, LANES), F32)],
        compiler_params=_cparams("arbitrary"),
        name="router_top2",
    )(x, mod, g, rw_pad, rb_pad)


def _segment_copy(src_hbm, dst_hbm, src_row, dst_row, n_rows, sem):
    src_row = pl.multiple_of(src_row, ROW_ALIGN)
    dst_row = pl.multiple_of(dst_row, ROW_ALIGN)
    n_rows = pl.multiple_of(n_rows, ROW_ALIGN)
    return pltpu.make_async_copy(src_hbm.at[pl.ds(src_row, n_rows)], dst_hbm.at[pl.ds(dst_row, n_rows)], sem)


def _expert_kernel(te_ref, used_ref, toff_ref, ilo_ref, ihi_ref, cnt_ref, carry_ref, loff_ref,
                   hs_hbm, w1_hbm, w3_hbm, w2_hbm, y_ref, wb1, wb3, wb2, stage, sem, xbuf, xsem, rows_smem,
                   *, srows):
    t = pl.program_id(0)
    e = te_ref[t]
    tg = xbuf.shape[1]
    slot = t % 2
    first_of_expert = jnp.logical_or(t == 0, e != te_ref[jnp.maximum(t - 1, 0)])

    def fetch(tile, into):
        xbuf[into] = jnp.zeros(xbuf.shape[1:], xbuf.dtype)
        expert = te_ref[tile]
        first_row = toff_ref[tile]

        def piece(i, total):
            s = i * N_EXPERTS + expert
            run_start = carry_ref[s]
            lo = jnp.maximum(run_start, first_row)
            hi = jnp.minimum(run_start + cnt_ref[s], first_row + tg)

            @pl.when(hi > lo)
            def _():
                _segment_copy(hs_hbm, xbuf.at[into], i * srows + loff_ref[s] + (lo - run_start), lo - first_row,
                              hi - lo, xsem.at[into]).start()

            return total + jnp.maximum(hi - lo, 0)

        rows_smem[into] = lax.fori_loop(ilo_ref[tile], ihi_ref[tile], piece, 0)

    @pl.when(t == 0)
    def _():
        fetch(t, slot)

    @pl.when(t + 1 < pl.num_programs(0))
    def _():
        fetch(t + 1, 1 - slot)

    @pl.when(rows_smem[slot] > 0)
    def _():
        _segment_copy(hs_hbm, xbuf.at[slot], 0, 0, rows_smem[slot], xsem.at[slot]).wait()

    @pl.when(jnp.logical_and(used_ref[t] == 1, first_of_expert))
    def _():
        n_slots = stage.shape[0]
        windows = [(src, dst, r, c) for src, dst in ((w1_hbm, wb1), (w3_hbm, wb3), (w2_hbm, wb2))
                   for r in range(dst.shape[0] // W_CHUNK) for c in range(dst.shape[1] // W_CHUNK)]

        def staged_copy(k):
            src, _, r, c = windows[k]
            return pltpu.make_async_copy(src.at[e, pl.ds(r * W_CHUNK, W_CHUNK), pl.ds(c * W_CHUNK, W_CHUNK)],
                                         stage.at[k % n_slots], sem.at[k % n_slots])

        for k in range(n_slots - 1):
            staged_copy(k).start()
        for k, (_, dst, r, c) in enumerate(windows):
            staged_copy(k).wait()
            dst[r * W_CHUNK:(r + 1) * W_CHUNK, c * W_CHUNK:(c + 1) * W_CHUNK] = stage[k % n_slots].astype(BF16)
            if k + n_slots - 1 < len(windows):
                staged_copy(k + n_slots - 1).start()

    @pl.when(used_ref[t] == 1)
    def _():
        lo, hi = _unpack_bf16_pair(xbuf[slot])
        h = jnp.concatenate([lo.astype(BF16), hi.astype(BF16)], axis=1)
        a = jnp.dot(h, wb1[...], preferred_element_type=F32)
        b = jnp.dot(h, wb3[...], preferred_element_type=F32)
        hid = ((a * _sigmoid(a)) * b).astype(BF16)
        acc = jnp.dot(hid, wb2[...], preferred_element_type=F32)
        half = acc.shape[1] // 2
        y_ref[...] = _pack_bf16_pair(acc[:, :half], acc[:, half:])

    @pl.when(used_ref[t] == 0)
    def _():
        y_ref[...] = jnp.zeros(y_ref.shape, y_ref.dtype)


def _expert_ffn(tile_tables, seg_tables, hs, w1, w3, w2, n_tiles):
    half = hs.shape[1]
    d = 2 * half
    dff = w1.shape[2]
    tg = TG_MOE
    assert dff % W_CHUNK == 0 and d % W_CHUNK == 0
    hbm = pl.BlockSpec(memory_space=pl.ANY)
    tables = tuple(tile_tables) + tuple(seg_tables)
    return pl.pallas_call(
        functools.partial(_expert_kernel, srows=SORT_ROWS),
        out_shape=jax.ShapeDtypeStruct((n_tiles * tg, half), jnp.uint32),
        grid_spec=pltpu.PrefetchScalarGridSpec(
            num_scalar_prefetch=len(tables),
            grid=(n_tiles,),
            in_specs=[hbm, hbm, hbm, hbm],
            out_specs=pl.BlockSpec((tg, half), lambda t, *_: (t, 0)),
            scratch_shapes=[pltpu.VMEM((d, dff), BF16), pltpu.VMEM((d, dff), BF16), pltpu.VMEM((dff, d), BF16),
                            pltpu.VMEM((W_SLOTS, W_CHUNK, W_CHUNK), F32), pltpu.SemaphoreType.DMA((W_SLOTS,)),
                            pltpu.VMEM((2, tg, half), jnp.uint32), pltpu.SemaphoreType.DMA((2,)),
                            pltpu.SMEM((2,), jnp.int32)],
        ),
        compiler_params=_cparams("arbitrary"),
        name="expert_swiglu",
    )(*tables, hs, w1, w3, w2)


def _combine_kernel(src_ref, loff_ref, cnt_ref, rows_ref, x_ref, ei_ref, pw_ref, mod_ref, y_ref, o_ref,
                    ybuf, sem, *, tm, srows):
    i = pl.program_id(0)
    slot = i % 2

    def fetch(tile, into):
        ybuf[into] = jnp.zeros(ybuf.shape[1:], ybuf.dtype)
        for e in range(N_EXPERTS):
            s = tile * N_EXPERTS + e
            n_rows = cnt_ref[s]

            @pl.when(n_rows > 0)
            def _():
                _segment_copy(y_ref, ybuf.at[into], src_ref[s], loff_ref[s], n_rows, sem.at[into]).start()

    @pl.when(i == 0)
    def _():
        fetch(i, slot)

    @pl.when(i + 1 < pl.num_programs(0))
    def _():
        fetch(i + 1, 1 - slot)

    @pl.when(rows_ref[i] > 0)
    def _():
        _segment_copy(y_ref, ybuf.at[slot], 0, 0, rows_ref[i], sem.at[slot]).wait()

    lo, hi = _unpack_bf16_pair(ybuf[slot])
    ys = jnp.concatenate([lo.astype(BF16), hi.astype(BF16)], axis=1)
    r_idx = lax.broadcasted_iota(jnp.int32, (tm, srows), 1)
    mix = jnp.zeros(x_ref.shape, F32)
    for k in range(TOP_K):
        pick = jnp.where(r_idx == ei_ref[:, TOP_K + k:TOP_K + k + 1], 1.0, 0.0).astype(BF16)
        mix = mix + pw_ref[:, k:k + 1] * jnp.dot(pick, ys, preferred_element_type=F32)
    o_ref[...] = x_ref[...] + mod_ref[5:6, :] * mix


def _combine(seg_src, seg_loff, seg_cnt, tile_rows, x, ei, pw, mod, y, seq):
    n, d = x.shape
    tm = TM_ROUTE
    tpb = seq // tm
    tok = lambda width: pl.BlockSpec((tm, width), lambda i, *_: (i, 0))
    return pl.pallas_call(
        functools.partial(_combine_kernel, tm=tm, srows=SORT_ROWS),
        out_shape=jax.ShapeDtypeStruct((n, d), F32),
        grid_spec=pltpu.PrefetchScalarGridSpec(
            num_scalar_prefetch=4,
            grid=(n // tm,),
            in_specs=[tok(d), tok(LANES), tok(LANES),
                      pl.BlockSpec((None, 6, d), lambda i, *_: (i // tpb, 0, 0)),
                      pl.BlockSpec(memory_space=pl.ANY)],
            out_specs=tok(d),
            scratch_shapes=[pltpu.VMEM((2, SORT_ROWS, d // 2), jnp.uint32), pltpu.SemaphoreType.DMA((2,))],
        ),
        compiler_params=_cparams("arbitrary"),
        name="moe_combine",
    )(seg_src, seg_loff, seg_cnt, tile_rows, x, ei, pw, mod, y)


def _pack_w_in(w_in):
    d = w_in.shape[0]
    w = MIX_W
    o_ckv = 3 * w + Q_LORA
    o_kr = o_ckv + KV_LORA
    o_su = o_kr + QK_ROPE
    o_rq = o_su + 2 * w
    o_gate = o_rq + 4 * w
    half = HEAD_DIM // 2
    perm = np.array([h * HEAD_DIM + part * half + i
                     for part in range(2) for h in range(N_GROUPS) for i in range(half)])
    kr = w_in[:, o_kr:o_kr + QK_ROPE]
    z = lambda k: jnp.zeros((d, k), w_in.dtype)
    cols = [
        w_in[:, o_gate:o_gate + N_BRANCH * d],
        w_in[:, 0:3 * w + Q_LORA],
        w_in[:, o_rq:o_rq + w][:, perm], w_in[:, o_rq + w:o_rq + 2 * w][:, perm],
        w_in[:, o_rq + 2 * w:o_rq + 4 * w],
        w_in[:, o_su:o_su + 2 * w],
        w_in[:, o_ckv:o_ckv + KV_LORA],
        z(QK_NOPE), kr, z(LANES - QK_HEAD),
    ]
    return jnp.concatenate(cols, axis=1).astype(BF16)


def _swap_rope_halves(a):
    hr = QK_ROPE // 2
    return jnp.concatenate([a[..., :QK_NOPE], a[..., QK_NOPE + hr:QK_HEAD], a[..., QK_NOPE:QK_NOPE + hr],
                            a[..., QK_HEAD:]], axis=-1)


def _mla_params(cq_g, w_uq, ckv_g, w_ukv, qn_g, kn_g):
    pad = LANES - QK_HEAD
    wq = w_uq.reshape(Q_LORA, N_GROUPS, QK_HEAD)
    wq = jnp.pad(wq, ((0, 0), (0, 0), (0, pad)))
    wkv = w_ukv.reshape(KV_LORA, N_GROUPS, QK_NOPE + V_HEAD)
    wk = jnp.pad(wkv[:, :, :QK_NOPE], ((0, 0), (0, 0), (0, LANES - QK_NOPE)))
    wv = jnp.pad(wkv[:, :, QK_NOPE:], ((0, 0), (0, 0), (0, LANES - V_HEAD)))
    qg = jnp.pad(qn_g, (0, pad))[None, :]
    kg = jnp.pad(kn_g, (0, pad))[None, :]
    bound = (QK_HEAD ** 0.5 * LOG2_E) * jnp.max(jnp.abs(qn_g)) * jnp.max(jnp.abs(kn_g))
    static_shift = bound <= MAX_STATIC_SHIFT
    lane = jnp.arange(LANES)
    qaug = (lane == QK_HEAD).astype(F32)[None, :]
    kaug = qaug * jnp.where(static_shift, -bound, 0.0)
    vaug = jnp.tile((lane == V_HEAD).astype(F32), N_GROUPS)[None, :]
    params = {
        "cq_g": cq_g[None, :], "ckv_g": ckv_g[None, :],
        "wqa": wq.reshape(Q_LORA, -1).astype(BF16),
        "wqb": _swap_rope_halves(wq).reshape(Q_LORA, -1).astype(BF16),
        "wk": wk.reshape(KV_LORA, -1).astype(BF16),
        "wv": wv.reshape(KV_LORA, -1).astype(BF16),
        "qga": qg, "qgb": _swap_rope_halves(qg), "kga": kg, "kgb": _swap_rope_halves(kg),
        "qaug": qaug, "kaug": kaug, "vaug": vaug,
        "swap": (_swap_rope_halves(lane[None, :])[0][None, :] == lane[:, None]).astype(BF16),
    }
    return params, static_shift


def _mixer_consts():
    h = jnp.arange(N_GROUPS, dtype=F32)
    log_gamma = jnp.log1p(-(2.0 ** (-5.0 - h)))
    pos = jnp.arange(CHUNK, dtype=F32)
    rel = pos[:, None] - pos[None, :]
    dec = jnp.where(rel >= 0, jnp.exp(log_gamma[:, None, None] * jnp.maximum(rel, 0.0)), 0.0)
    lane = np.arange(MIX_W)
    head_k = (lane % LANES) // (HEAD_DIM // 2)
    head_v = lane // HEAD_DIM
    lg_k = log_gamma[head_k]
    return {
        "dec": dec.reshape(N_GROUPS * CHUNK, CHUNK),
        "kdec": jnp.exp(lg_k[None, :] * (CHUNK - 1.0 - pos)[:, None]),
        "qdec": jnp.exp(lg_k[None, :] * (pos + 1.0)[:, None]),
        "cdec": jnp.broadcast_to(jnp.exp(lg_k * CHUNK)[:, None], (MIX_W, MIX_W)),
        "bd": jnp.asarray((head_k[:, None] == head_v[None, :]).astype(np.float32)),
        "gmat": jnp.asarray((head_v[:, None] == head_v[None, :]).astype(np.float32) / HEAD_DIM).astype(BF16),
        "mk": jnp.asarray((head_k[None, :] == np.arange(N_GROUPS)[:, None]).astype(np.float32)),
        "mv": jnp.asarray((head_v[None, :] == np.arange(N_GROUPS)[:, None]).astype(np.float32)),
    }


def _mixer_params(conv_w, gv_g, w_s, b_s, ret_g, w_branch, w_o):
    p = dict(_mixer_consts())
    ws = jnp.tril(w_s)
    p.update({
        "conv_w": conv_w,
        "gv_g": gv_g.reshape(1, MIX_W),
        "ws_cat": jnp.transpose(ws, (1, 0, 2)).reshape(CHUNK, N_GROUPS * CHUNK).astype(BF16),
        "bs_mat": jnp.repeat(b_s.T, HEAD_DIM, axis=1),
        "ret_g": ret_g.reshape(1, MIX_W),
        "w_branch": w_branch.astype(BF16),
        "w_o": w_o.astype(BF16),
    })
    return p


def _moe_layout(meta, tot, n_tiles):
    totals = tot[0, :N_EXPERTS]
    padded = ((totals + TG_MOE - 1) // TG_MOE) * TG_MOE
    ends = jnp.cumsum(padded)
    starts = ends - padded
    seg_cnt = meta[:, 0, :N_EXPERTS]
    seg_carry = meta[:, 1, :N_EXPERTS]
    seg_loff = meta[:, 2, :N_EXPERTS]
    seg_grouped = starts[None, :] + seg_carry
    tile_start = jnp.arange(n_tiles, dtype=jnp.int32) * TG_MOE
    tile_e = jnp.sum((tile_start[:, None] >= ends[None, :]).astype(jnp.int32), axis=1)
    used = (tile_start < ends[-1]).astype(jnp.int32)
    last_e = jnp.sum((ends[-1] - 1 >= ends).astype(jnp.int32))
    tile_e = jnp.minimum(jnp.where(used == 1, tile_e, last_e), N_EXPERTS - 1)
    tile_off = tile_start - starts[tile_e]
    run_start = seg_carry[:, tile_e]
    run_end = run_start + seg_cnt[:, tile_e]
    tile_ilo = jnp.sum((run_end <= tile_off[None, :]).astype(jnp.int32), axis=0) * used
    tile_ihi = jnp.sum((run_start < tile_off[None, :] + TG_MOE).astype(jnp.int32), axis=0) * used
    flat = lambda a: a.reshape(-1).astype(jnp.int32)
    tile_tables = (flat(tile_e), flat(used), flat(tile_off * used), flat(tile_ilo), flat(tile_ihi))
    seg_tables = (flat(seg_cnt), flat(seg_carry), flat(seg_loff))
    return tile_tables, seg_tables, flat(seg_grouped)


def kernel(x, c, positions, norm1_g, norm2_g, ada_w, ada_b, w_in, conv_w, cq_g, w_uq, ckv_g, w_ukv, qn_g, kn_g, gv_g, w_s, b_s, ret_g, w_branch, w_o, ffn_w1, ffn_w3, ffn_w2, router_w, router_b, moe_w1, moe_w3, moe_w2):
    batch, seq, d = x.shape
    depth = ada_w.shape[0]
    n = batch * seq
    assert seq % max(TM_PROJ, TQ_ATT, TM_MIX, TM_FFN, TM_ROUTE) == 0
    assert d // 2 % LANES == 0

    c_t = jnp.pad(c, ((0, 8 - batch), (0, 0))).T
    ada = _ada(c_t, ada_w, ada_b, batch)[:, :batch].reshape(depth, batch, 6, d)
    cosr, sinr, cm, sm = _rope_tables(positions.astype(F32).reshape(n, 1))

    xt = x.reshape(n, d)
    for l in range(depth):
        mod = ada[l]
        mla_p, static_shift = _mla_params(cq_g[l], w_uq[l], ckv_g[l], w_ukv[l], qn_g[l], kn_g[l])
        proj, q, k, v = _inproj(xt, mod, norm1_g[l][None, :], _pack_w_in(w_in[l]), cm, sm, mla_p, seq)
        y_mla = lax.cond(static_shift,
                         functools.partial(_flash, batch=batch, seq=seq, online_max=False),
                         functools.partial(_flash, batch=batch, seq=seq, online_max=True), q, k, v)
        mp = _mixer_params(conv_w[l], gv_g[l], w_s[l], b_s[l], ret_g[l], w_branch[l], w_o[l])
        xt = _mixers(proj, y_mla, xt, cosr, sinr, mod, mp, seq)
        g2n = norm2_g[l][None, :]
        if l % 2 == 0:
            i = l // 2
            xt = _dense_ffn(xt, mod, g2n, ffn_w1[i].astype(BF16), ffn_w3[i].astype(BF16),
                            ffn_w2[i].astype(BF16), seq)
        else:
            i = l // 2
            rw = jnp.pad(router_w[i], ((0, 0), (0, LANES - N_EXPERTS)))
            rw_hi = rw.astype(BF16)
            rw_pad = jnp.concatenate([rw_hi, (rw - rw_hi.astype(F32)).astype(BF16)], axis=1)
            rb_pad = jnp.pad(router_b[i], (0, LANES - N_EXPERTS), constant_values=-1e30)[None, :]
            hs, ei, pw, meta, tot = _router(xt, mod, g2n, rw_pad, rb_pad, seq)
            max_rows = n * TOP_K + N_EXPERTS * (n // TM_ROUTE) * (ROW_ALIGN - 1)
            n_tiles = -(-max_rows // TG_MOE) + N_EXPERTS
            tile_tables, seg_tables, seg_grouped = _moe_layout(meta, tot, n_tiles)
            seg_cnt, _, seg_loff = seg_tables
            tile_rows = jnp.sum(seg_cnt.reshape(-1, N_EXPERTS), axis=1)
            y = _expert_ffn(tile_tables, seg_tables, hs, moe_w1[i], moe_w3[i], moe_w2[i], n_tiles)
            xt = _combine(seg_grouped, seg_loff, seg_cnt, tile_rows, xt, ei, pw, mod, y, seq)
    return xt.reshape(batch, seq, d)
```

```python
import functools

import jax
import jax.numpy as jnp
import numpy as np
from jax import lax
from jax.experimental import pallas as pl
from jax.experimental.pallas import tpu as pltpu

F32 = jnp.float32
BF16 = jnp.bfloat16
HIGHEST = lax.Precision.HIGHEST

HEAD_DIM = 64
N_GROUPS = 4
MIX_W = N_GROUPS * HEAD_DIM
N_BRANCH = 4
CONV_W = 3
Q_LORA = 256
KV_LORA = 128
QK_NOPE = 64
QK_ROPE = 32
QK_HEAD = QK_NOPE + QK_ROPE
V_HEAD = 64
CHUNK = 128
N_EXPERTS = 8
TOP_K = 2
ROPE_THETA = 10000.0
EPS = 1e-6
LOG2_E = 1.4426950408889634
MAX_STATIC_SHIFT = 50.0

LANES = 128
VMEM_LIMIT_BYTES = 56 * 1024 * 1024

COL_GATES = 0
COL_A = 4096
COL_CQ = COL_A + 3 * MIX_W
COL_R = 5120
COL_SU = 6144
COL_CKV = 6656
COL_KRA = 6784
N_IN = 6912

TM_PROJ = 512
TN_PROJ = 768
TQ_ATT = 1024
TM_MIX = 512
TM_FFN = 512
TM_ROUTE = 512
ROW_ALIGN = 8
SORT_ROWS = TOP_K * TM_ROUTE + N_EXPERTS * ROW_ALIGN
TG_MOE = 512
W_CHUNK = 512
W_SLOTS = 6


def _cparams(*sem):
    return pltpu.CompilerParams(dimension_semantics=sem, vmem_limit_bytes=VMEM_LIMIT_BYTES)


def _sigmoid(x):
    return jnp.tanh(x * 0.5) * 0.5 + 0.5


def _group_mean(x, gmat_bf16):
    hi = x.astype(BF16)
    lo = (x - hi.astype(F32)).astype(BF16)
    return (jnp.dot(hi, gmat_bf16, preferred_element_type=F32)
            + jnp.dot(lo, gmat_bf16, preferred_element_type=F32))


def _pack_bf16_pair(lo, hi):
    lo_bits = lax.bitcast_convert_type(lo.astype(BF16).astype(F32), jnp.uint32)
    hi_bits = lax.bitcast_convert_type(hi.astype(BF16).astype(F32), jnp.uint32)
    return (lo_bits >> 16) | (hi_bits & jnp.uint32(0xFFFF0000))


def _unpack_bf16_pair(p):
    lo = lax.bitcast_convert_type(p << 16, F32)
    hi = lax.bitcast_convert_type(p & jnp.uint32(0xFFFF0000), F32)
    return lo, hi


def _norm_mod(x, g, shift, scale):
    y = x * lax.rsqrt(jnp.mean(x * x, axis=-1, keepdims=True) + EPS)
    return (y * g) * (1.0 + scale) + shift


def _ada_kernel(ct_ref, w_ref, b_ref, o_ref, *, batch):
    ct = ct_ref[...]
    cond = ct * _sigmoid(ct)
    w = w_ref[...]
    o_ref[...] = jnp.zeros(o_ref.shape, F32)
    for b in range(batch):
        o_ref[b:b + 1, :] = jnp.sum(w * cond[:, b:b + 1], axis=0, keepdims=True) + b_ref[...]


def _ada(c_t, ada_w, ada_b, batch):
    n_layer, d, d6 = ada_w.shape
    rows = c_t.shape[1]
    tn = 2048
    return pl.pallas_call(
        functools.partial(_ada_kernel, batch=batch),
        out_shape=jax.ShapeDtypeStruct((n_layer, rows, d6), F32),
        grid=(n_layer, d6 // tn),
        in_specs=[
            pl.BlockSpec((d, rows), lambda l, j: (0, 0)),
            pl.BlockSpec((None, d, tn), lambda l, j: (l, 0, j)),
            pl.BlockSpec((None, 1, tn), lambda l, j: (l, 0, j)),
        ],
        out_specs=pl.BlockSpec((None, rows, tn), lambda l, j: (l, 0, j)),
        compiler_params=_cparams("parallel", "parallel"),
        name="ada_mod",
    )(c_t, ada_w, ada_b.reshape(n_layer, 1, d6))


def _rope_kernel(pos_ref, inv_ref, cr_ref, sr_ref, cm_ref, sm_ref):
    half_r = HEAD_DIM // 2
    half_m = QK_ROPE // 2
    tm = pos_ref.shape[0]
    low = lax.broadcasted_iota(jnp.int32, (tm // 2, LANES), 1) < LANES // 2
    ang = jnp.where(low, pos_ref[0:tm // 2, :], pos_ref[tm // 2:tm, :]) * inv_ref[...]
    c = jnp.cos(ang)
    s = jnp.sin(ang)
    c = jnp.concatenate([c, pltpu.roll(c, LANES // 2, axis=1)], axis=0)
    s = jnp.concatenate([s, pltpu.roll(s, LANES // 2, axis=1)], axis=0)
    lane = lax.broadcasted_iota(jnp.int32, c.shape, 1)

    def tile_r(t):
        t = jnp.where(lane < half_r, t, 0.0)
        out = t
        for k in range(1, LANES // half_r):
            out = out + pltpu.roll(t, k * half_r, axis=1)
        return out

    cr_ref[...] = tile_r(c)
    sr_ref[...] = tile_r(s)
    first = jnp.logical_and(lane >= QK_NOPE, lane < QK_NOPE + half_m)
    second = jnp.logical_and(lane >= QK_NOPE + half_m, lane < QK_HEAD)
    c1, c2 = pltpu.roll(c, QK_NOPE - half_r, axis=1), pltpu.roll(c, QK_NOPE + half_m - half_r, axis=1)
    s1, s2 = pltpu.roll(s, QK_NOPE - half_r, axis=1), pltpu.roll(s, QK_NOPE + half_m - half_r, axis=1)
    cm_ref[...] = jnp.where(first, c1, jnp.where(second, c2, 1.0))
    sm_ref[...] = jnp.where(first, -s1, jnp.where(second, s2, 0.0))


def _rope_tables(pos_f):
    n = pos_f.shape[0]
    tm = 1024
    half_r = HEAD_DIM // 2
    half_m = QK_ROPE // 2
    inv_r = ROPE_THETA ** (-jnp.arange(half_r, dtype=F32) / half_r)
    inv_m = ROPE_THETA ** (-jnp.arange(half_m, dtype=F32) / half_m)
    inv = jnp.concatenate([inv_r, inv_m, jnp.zeros((LANES // 2 - half_r - half_m,), F32)])
    inv = jnp.tile(inv, 2)[None, :]
    tab = pl.BlockSpec((tm, LANES), lambda i: (i, 0))
    shape = jax.ShapeDtypeStruct((n, LANES), F32)
    return pl.pallas_call(
        _rope_kernel,
        out_shape=(shape, shape, shape, shape),
        grid=(n // tm,),
        in_specs=[pl.BlockSpec((tm, 1), lambda i: (i, 0)), pl.BlockSpec((1, LANES), lambda i: (0, 0))],
        out_specs=(tab, tab, tab, tab),
        compiler_params=_cparams("parallel"),
        name="rope_tables",
    )(pos_f, inv)


def _resident(shape):
    return pl.BlockSpec(shape, lambda *_: (0,) * len(shape), pipeline_mode=pl.Buffered(1))


def _mla_prep(cq_b, ckv_b, kra_b, cm, sm, cqg_ref, wqa_ref, wqb_ref, ckvg_ref, wk_ref, wv_ref, qga_ref, qgb_ref,
              kga_ref, kgb_ref, qaug_ref, kaug_ref, vaug_ref, swap_ref, q_ref, k_ref, v_ref):
    cq = cq_b.astype(F32)
    cqn = (cq * lax.rsqrt(jnp.mean(cq * cq, axis=-1, keepdims=True) + EPS) * cqg_ref[...]).astype(BF16)
    qa = jnp.dot(cqn, wqa_ref[...], preferred_element_type=F32)
    qb = jnp.dot(cqn, wqb_ref[...], preferred_element_type=F32)
    ckv = ckv_b.astype(F32)
    ckvn = (ckv * lax.rsqrt(jnp.mean(ckv * ckv, axis=-1, keepdims=True) + EPS) * ckvg_ref[...]).astype(BF16)
    ka = jnp.dot(ckvn, wk_ref[...], preferred_element_type=F32)
    v_ref[...] = (jnp.dot(ckvn, wv_ref[...], preferred_element_type=F32) + vaug_ref[...]).astype(BF16)
    kra = kra_b.astype(F32)
    krb = jnp.dot(kra_b, swap_ref[...], preferred_element_type=F32)
    scale = QK_HEAD ** -0.5 * LOG2_E
    q_cos, q_sin = cm * (qga_ref[...] * scale), sm * (qgb_ref[...] * scale)
    k_cos, k_sin = cm * kga_ref[...], sm * kgb_ref[...]
    for h in range(N_GROUPS):
        sl = slice(h * LANES, (h + 1) * LANES)
        qah, qbh = qa[:, sl], qb[:, sl]
        r = lax.rsqrt(jnp.sum(qah * qah, axis=-1, keepdims=True) * (1.0 / QK_HEAD) + EPS)
        q_ref[:, sl] = ((qah * q_cos + qbh * q_sin) * r + qaug_ref[...]).astype(BF16)
        kah = ka[:, sl] + kra
        kbh = ka[:, sl] + krb
        r = lax.rsqrt(jnp.sum(kah * kah, axis=-1, keepdims=True) * (1.0 / QK_HEAD) + EPS)
        k_ref[:, sl] = ((kah * k_cos + kbh * k_sin) * r + kaug_ref[...]).astype(BF16)


def _inproj_kernel(x_ref, mod_ref, g_ref, w_ref, cm_ref, sm_ref, *rest):
    mla_refs, (o_ref, q_ref, k_ref, v_ref) = rest[:-4], rest[-4:]
    h = _norm_mod(x_ref[...], g_ref[...], mod_ref[0:1, :], mod_ref[1:2, :]).astype(BF16)

    def chunk(c):
        cols = slice(c * TN_PROJ, (c + 1) * TN_PROJ)
        out = jnp.dot(h, w_ref[:, cols], preferred_element_type=F32).astype(BF16)
        o_ref[:, cols] = out
        return out

    c_q, c_kv = COL_CQ // TN_PROJ, COL_CKV // TN_PROJ
    lat_q = chunk(c_q)
    lat_kv = chunk(c_kv)
    q0, kv0, kr0 = COL_CQ - c_q * TN_PROJ, COL_CKV - c_kv * TN_PROJ, COL_KRA - c_kv * TN_PROJ
    _mla_prep(lat_q[:, q0:q0 + Q_LORA], lat_kv[:, kv0:kv0 + KV_LORA], lat_kv[:, kr0:kr0 + LANES],
              cm_ref[...], sm_ref[...], *mla_refs, q_ref, k_ref, v_ref)
    for c in range(N_IN // TN_PROJ):
        if c not in (c_q, c_kv):
            chunk(c)


def _inproj(x, mod, g, w, cm, sm, p, seq):
    n, d = x.shape
    tm = TM_PROJ
    tpb = seq // tm
    hw = N_GROUPS * LANES
    assert COL_KRA // TN_PROJ == COL_CKV // TN_PROJ and (COL_CQ + Q_LORA - 1) // TN_PROJ == COL_CQ // TN_PROJ

    def full(a):
        return pl.BlockSpec(a.shape, lambda i: (0,) * a.ndim)

    weights = [p["cq_g"], p["wqa"], p["wqb"], p["ckv_g"], p["wk"], p["wv"],
               p["qga"], p["qgb"], p["kga"], p["kgb"], p["qaug"], p["kaug"], p["vaug"], p["swap"]]
    table = pl.BlockSpec((tm, LANES), lambda i: (i, 0))
    head_tile = pl.BlockSpec((tm, hw), lambda i: (i, 0))
    heads = jax.ShapeDtypeStruct((n, hw), BF16)
    return pl.pallas_call(
        _inproj_kernel,
        out_shape=(jax.ShapeDtypeStruct((n, N_IN), BF16), heads, heads, heads),
        grid=(n // tm,),
        in_specs=[
            pl.BlockSpec((tm, d), lambda i: (i, 0)),
            pl.BlockSpec((None, 6, d), lambda i: (i // tpb, 0, 0)),
            pl.BlockSpec((1, d), lambda i: (0, 0)),
            _resident((d, N_IN)), table, table,
        ] + [full(a) for a in weights],
        out_specs=(pl.BlockSpec((tm, N_IN), lambda i: (i, 0)), head_tile, head_tile, head_tile),
        compiler_params=_cparams("parallel"),
        name="in_proj",
    )(x, mod, g, w, cm, sm, *weights)


def _flash_kernel(qi_ref, kj_ref, q_ref, k_ref, v_ref, o_ref, acc_scr, *rest, tq, online_max):
    i = qi_ref[pl.program_id(1)]
    j = kj_ref[pl.program_id(1)]

    @pl.when(j == 0)
    def _():
        acc_scr[...] = jnp.zeros(acc_scr.shape, F32)
        if online_max:
            rest[0][...] = jnp.full(rest[0].shape, -jnp.inf, F32)

    def block(q0, nq, nk, masked):
        rows = slice(q0, q0 + nq)
        if masked:
            row = lax.broadcasted_iota(jnp.int32, (nq, nk), 0) + q0
            col = lax.broadcasted_iota(jnp.int32, (nq, nk), 1)
            keep = col <= row
        for h in range(N_GROUPS):
            sl = slice(h * LANES, (h + 1) * LANES)
            s = lax.dot_general(q_ref[rows, sl], k_ref[0:nk, sl], (((1,), (1,)), ((), ())),
                                preferred_element_type=F32)
            if masked:
                s = jnp.where(keep, s, -jnp.inf)
            if online_max:
                m_scr = rest[0]
                m_prev = m_scr[h, rows]
                m_new = jnp.maximum(m_prev, jnp.max(s, axis=-1, keepdims=True))
                p = jnp.exp2(s - m_new).astype(BF16)
                acc_scr[h, rows] = jnp.exp2(m_prev - m_new) * acc_scr[h, rows] + jnp.dot(
                    p, v_ref[0:nk, sl], preferred_element_type=F32)
                m_scr[h, rows] = m_new
            else:
                acc_scr[h, rows] += jnp.dot(jnp.exp2(s).astype(BF16), v_ref[0:nk, sl],
                                            preferred_element_type=F32)

    @pl.when(j < i)
    def _():
        block(0, tq, tq, False)

    @pl.when(j == i)
    def _():
        block(0, tq // 2, tq // 2, True)
        block(tq // 2, tq // 2, tq, True)
        lane = lax.broadcasted_iota(jnp.int32, (tq, LANES), 1)
        for pr in range(N_GROUPS // 2):
            lo = acc_scr[2 * pr]
            hi = acc_scr[2 * pr + 1]
            lo = lo / lo[:, V_HEAD:V_HEAD + 1]
            hi = hi / hi[:, V_HEAD:V_HEAD + 1]
            both = jnp.where(lane < V_HEAD, lo, pltpu.roll(hi, V_HEAD, axis=1))
            o_ref[:, pr * LANES:(pr + 1) * LANES] = both.astype(BF16)


def _flash(q, k, v, batch, seq, online_max):
    n = q.shape[0]
    tq = TQ_ATT
    nq = seq // tq
    hw = N_GROUPS * LANES
    scratch = [pltpu.VMEM((N_GROUPS, tq, LANES), F32)]
    if online_max:
        scratch.append(pltpu.VMEM((N_GROUPS, tq, 1), F32))
    pairs = [(i, j) for i in range(nq) for j in range(i + 1)]
    qi = jnp.asarray([p[0] for p in pairs], jnp.int32)
    kj = jnp.asarray([p[1] for p in pairs], jnp.int32)
    q_tile = lambda b, s, qi, kj: (b * nq + qi[s], 0)
    k_tile = lambda b, s, qi, kj: (b * nq + kj[s], 0)
    return pl.pallas_call(
        functools.partial(_flash_kernel, tq=tq, online_max=online_max),
        out_shape=jax.ShapeDtypeStruct((n, MIX_W), BF16),
        grid_spec=pltpu.PrefetchScalarGridSpec(
            num_scalar_prefetch=2,
            grid=(batch, len(pairs)),
            in_specs=[pl.BlockSpec((tq, hw), q_tile), pl.BlockSpec((tq, hw), k_tile),
                      pl.BlockSpec((tq, hw), k_tile)],
            out_specs=pl.BlockSpec((tq, MIX_W), q_tile),
            scratch_shapes=scratch,
        ),
        compiler_params=_cparams("parallel", "arbitrary"),
        name="mla_flash_online" if online_max else "mla_flash",
    )(qi, kj, q, k, v)


def _gelu_tanh(x):
    return jax.nn.gelu(x, approximate=True)


def _mix_kernel(gates_ref, a_ref, r_ref, su_ref, ymla_ref, x_ref, cos_ref, sin_ref, mod_ref,
                convw_ref, gvg_ref, wscat_ref, bsmat_ref, retg_ref, dec_ref, kdec_ref, qdec_ref,
                cdec_ref, bd_ref, gmat_ref, mk_ref, mv_ref, wb_ref, wo_ref,
                o_ref, carry_scr, state_scr, ysg_scr, yret_scr, *, tm, tpb):
    i = pl.program_id(0)

    @pl.when(i % tpb == 0)
    def _():
        carry_scr[...] = jnp.zeros(carry_scr.shape, F32)
        state_scr[...] = jnp.zeros(state_scr.shape, F32)

    w = MIX_W
    a_b = a_ref[:, 0:w].astype(F32)
    u = a_ref[:, w:2 * w].astype(F32) * a_ref[:, 2 * w:3 * w].astype(F32)
    rowi = lax.broadcasted_iota(jnp.int32, (tm, w), 0)
    prev1 = carry_scr[0:1, :]
    prev2 = carry_scr[1:2, :]
    u1 = jnp.where(rowi == 0, prev1, pltpu.roll(u, 1, axis=0))
    u2 = jnp.where(rowi == 0, prev2, jnp.where(rowi == 1, prev1, pltpu.roll(u, 2, axis=0)))
    carry_scr[0:1, :] = u[tm - 1:tm, :]
    carry_scr[1:2, :] = u[tm - 2:tm - 1, :]
    y_conv = a_b * (convw_ref[0:1, :] * u2 + convw_ref[1:2, :] * u1 + convw_ref[2:3, :] * u)

    gmat = gmat_ref[...]
    s_u = _gelu_tanh(su_ref[:, 0:w].astype(F32))
    s_v = _gelu_tanh(su_ref[:, w:2 * w].astype(F32))
    ms = _group_mean(s_v * s_v, gmat)
    vn = (s_v * lax.rsqrt(ms + EPS) * gvg_ref[...]).astype(BF16)

    cosr = cos_ref[...]
    sinr = sin_ref[...]

    def rot(t):
        t1, t2 = t[:, 0:LANES], t[:, LANES:2 * LANES]
        return jnp.concatenate([t1 * cosr - t2 * sinr, t2 * cosr + t1 * sinr], axis=-1)

    rq = rot(r_ref[:, 0:w].astype(F32))
    rk = rot(r_ref[:, w:2 * w].astype(F32)) * (HEAD_DIM ** -0.5)

    for c in range(tm // CHUNK):
        rows = slice(c * CHUNK, (c + 1) * CHUNK)
        vc = vn[rows, :]
        vbd = jnp.concatenate([vc * mv_ref[g:g + 1, :].astype(BF16) for g in range(N_GROUPS)], axis=0)
        mixed = jnp.dot(wscat_ref[...], vbd, preferred_element_type=F32) + bsmat_ref[...]
        ysg_scr[rows, :] = s_u[rows, :] * mixed

        qc = rq[rows, :]
        kc = rk[rows, :]
        kcb = kc.astype(BF16)
        vcb = r_ref[rows, 2 * w:3 * w]
        qstack = jnp.concatenate([(qc * mk_ref[h:h + 1, :]).astype(BF16) for h in range(N_GROUPS)], axis=0)
        sc = lax.dot_general(qstack, kcb, (((1,), (1,)), ((), ())), preferred_element_type=F32)
        sc = (sc * dec_ref[...]).astype(BF16)
        scat = jnp.concatenate([sc[h * CHUNK:(h + 1) * CHUNK, :] for h in range(N_GROUPS)], axis=1)
        vstack = jnp.concatenate([vcb * mv_ref[h:h + 1, :].astype(BF16) for h in range(N_GROUPS)], axis=0)
        o_c = jnp.dot(scat, vstack, preferred_element_type=F32)
        state = state_scr[...]
        o_c = o_c + jnp.dot((qc * qdec_ref[...]).astype(BF16), state.astype(BF16),
                            preferred_element_type=F32)
        kd_t = jnp.transpose(kc * kdec_ref[...]).astype(BF16)
        kv = jnp.dot(kd_t, vcb, preferred_element_type=F32)
        state_scr[...] = state * cdec_ref[...] + kv * bd_ref[...]
        yret_scr[rows, :] = o_c

    o_all = yret_scr[...]
    xc = o_all - _group_mean(o_all, gmat)
    var = _group_mean(xc * xc, gmat)
    r_g = r_ref[:, 3 * w:4 * w].astype(F32)
    y_ret = (r_g * _sigmoid(r_g)) * (xc * lax.rsqrt(var + EPS) * retg_ref[...])

    d = x_ref.shape[1]
    ys = (y_conv, ymla_ref[...], ysg_scr[...], y_ret)
    merged = None
    for n in range(N_BRANCH):
        gate = _sigmoid(gates_ref[:, n * d:(n + 1) * d])
        term = gate * jnp.dot(ys[n].astype(BF16), wb_ref[n], preferred_element_type=F32).astype(BF16)
        merged = term if merged is None else merged + term
    out = jnp.dot(merged, wo_ref[...], preferred_element_type=F32)
    o_ref[...] = x_ref[...] + mod_ref[2:3, :] * out


def _mixers(proj, ymla, x, cosr, sinr, mod, p, seq):
    n, d = x.shape
    tm = TM_MIX
    tpb = seq // tm

    def col(width, offset):
        return pl.BlockSpec((tm, width), lambda i: (i, offset // width))

    def full(a):
        return pl.BlockSpec(a.shape, lambda i: (0,) * a.ndim)

    consts = [p["conv_w"], p["gv_g"], p["ws_cat"], p["bs_mat"], p["ret_g"], p["dec"], p["kdec"],
              p["qdec"], p["cdec"], p["bd"], p["gmat"], p["mk"], p["mv"], p["w_branch"], p["w_o"]]
    return pl.pallas_call(
        functools.partial(_mix_kernel, tm=tm, tpb=tpb),
        out_shape=jax.ShapeDtypeStruct((n, d), F32),
        grid=(n // tm,),
        in_specs=[col(N_BRANCH * d, COL_GATES), col(4 * MIX_W, COL_A), col(4 * MIX_W, COL_R),
                  col(2 * MIX_W, COL_SU),
                  pl.BlockSpec((tm, MIX_W), lambda i: (i, 0)),
                  pl.BlockSpec((tm, d), lambda i: (i, 0)),
                  pl.BlockSpec((tm, LANES), lambda i: (i, 0)),
                  pl.BlockSpec((tm, LANES), lambda i: (i, 0)),
                  pl.BlockSpec((None, 6, d), lambda i: (i // tpb, 0, 0))]
                 + [full(c) for c in consts],
        out_specs=pl.BlockSpec((tm, d), lambda i: (i, 0)),
        scratch_shapes=[pltpu.VMEM((8, MIX_W), F32), pltpu.VMEM((MIX_W, MIX_W), F32),
                        pltpu.VMEM((tm, MIX_W), F32), pltpu.VMEM((tm, MIX_W), F32)],
        compiler_params=_cparams("arbitrary"),
        name="mixers_merge",
    )(proj, proj, proj, proj, ymla, x, cosr, sinr, mod, *consts)


def _ffn_kernel(x_ref, mod_ref, g_ref, w1_ref, w3_ref, w2_ref, o_ref):
    x = x_ref[...]
    h = _norm_mod(x, g_ref[...], mod_ref[3:4, :], mod_ref[4:5, :]).astype(BF16)
    a = jnp.dot(h, w1_ref[...], preferred_element_type=F32)
    b = jnp.dot(h, w3_ref[...], preferred_element_type=F32)
    hid = ((a * _sigmoid(a)) * b).astype(BF16)
    o_ref[...] = x + mod_ref[5:6, :] * jnp.dot(hid, w2_ref[...], preferred_element_type=F32)


def _dense_ffn(x, mod, g, w1, w3, w2, seq):
    n, d = x.shape
    dff = w1.shape[1]
    tm = TM_FFN
    tpb = seq // tm
    return pl.pallas_call(
        _ffn_kernel,
        out_shape=jax.ShapeDtypeStruct((n, d), F32),
        grid=(n // tm,),
        in_specs=[
            pl.BlockSpec((tm, d), lambda i: (i, 0)),
            pl.BlockSpec((None, 6, d), lambda i: (i // tpb, 0, 0)),
            pl.BlockSpec((1, d), lambda i: (0, 0)),
            _resident((d, dff)), _resident((d, dff)), _resident((dff, d)),
        ],
        out_specs=pl.BlockSpec((tm, d), lambda i: (i, 0)),
        compiler_params=_cparams("parallel"),
        name="dense_swiglu",
    )(x, mod, g, w1, w3, w2)


def _router_kernel(x_ref, mod_ref, g_ref, rw_ref, rb_ref, hs_ref, ei_ref, pw_ref, meta_ref, tot_ref,
                   carry_scr, *, tm, srows):
    i = pl.program_id(0)

    @pl.when(i == 0)
    def _():
        carry_scr[...] = jnp.zeros(carry_scr.shape, F32)

    h = _norm_mod(x_ref[...], g_ref[...], mod_ref[3:4, :], mod_ref[4:5, :])

    h_hi = h.astype(BF16)
    h_lo = (h - h_hi.astype(F32)).astype(BF16)
    hw = jnp.dot(h_hi, rw_ref[...], preferred_element_type=F32)
    logits = (hw[:, :LANES] + hw[:, LANES:] + jnp.dot(h_lo, rw_ref[:, :LANES], preferred_element_type=F32)
              + rb_ref[...])
    mx = jnp.max(logits, axis=-1, keepdims=True)
    ex = jnp.exp(logits - mx)
    probs = ex / jnp.sum(ex, axis=-1, keepdims=True)
    lane = lax.broadcasted_iota(jnp.int32, (tm, LANES), 1)
    valid = lane < N_EXPERTS
    probs = jnp.where(valid, probs, -1.0)
    m1 = jnp.max(probs, axis=-1, keepdims=True)
    i1 = jnp.min(jnp.where(probs == m1, lane, LANES), axis=-1, keepdims=True)
    rest = jnp.where(lane == i1, -1.0, probs)
    m2 = jnp.max(rest, axis=-1, keepdims=True)
    i2 = jnp.min(jnp.where(rest == m2, lane, LANES), axis=-1, keepdims=True)
    den = m1 + m2
    pw_ref[...] = jnp.where(lane == 0, m1 / den, jnp.where(lane == 1, m2 / den, 0.0))

    sel1 = lane == i1
    sel2 = lane == i2
    onehot = jnp.where(sel1, 1.0, 0.0) + jnp.where(sel2, 1.0, 0.0)
    r_i = lax.broadcasted_iota(jnp.int32, (tm, tm), 0)
    c_i = lax.broadcasted_iota(jnp.int32, (tm, tm), 1)
    tri = jnp.where(c_i < r_i, 1.0, 0.0).astype(BF16)
    before = jnp.dot(tri, onehot.astype(BF16), preferred_element_type=F32)
    cnt = jnp.sum(onehot, axis=0, keepdims=True)
    cnt_al = jnp.floor((cnt + (ROW_ALIGN - 1)) * (1.0 / ROW_ALIGN)) * ROW_ALIGN
    e_r = lax.broadcasted_iota(jnp.int32, (LANES, LANES), 0)
    e_c = lax.broadcasted_iota(jnp.int32, (LANES, LANES), 1)
    upper = jnp.where(e_r < e_c, 1.0, 0.0)
    loff = jnp.dot(jnp.broadcast_to(cnt_al, (8, LANES)), upper, precision=HIGHEST,
                   preferred_element_type=F32)[0:1, :]
    slot = loff + before
    slot1 = jnp.sum(jnp.where(sel1, slot, 0.0), axis=-1, keepdims=True).astype(jnp.int32)
    slot2 = jnp.sum(jnp.where(sel2, slot, 0.0), axis=-1, keepdims=True).astype(jnp.int32)
    ei = jnp.where(lane == 0, i1, jnp.where(lane == 1, i2, 0))
    ei_ref[...] = jnp.where(lane == 2, slot1, jnp.where(lane == 3, slot2, ei))

    r_idx = lax.broadcasted_iota(jnp.int32, (tm, srows), 1)
    place = jnp.where(r_idx == slot1, 1.0, jnp.where(r_idx == slot2, 1.0, 0.0)).astype(BF16)
    hs = lax.dot_general(place, h.astype(BF16), (((0,), (0,)), ((), ())), preferred_element_type=F32)
    half = hs.shape[1] // 2
    hs_ref[...] = _pack_bf16_pair(hs[:, :half], hs[:, half:])

    carry = carry_scr[0:1, :]
    mrow = lax.broadcasted_iota(jnp.int32, (8, LANES), 0)
    meta = jnp.where(mrow == 0, cnt_al, jnp.where(mrow == 1, carry, jnp.where(mrow == 2, loff, 0.0)))
    meta_ref[...] = meta.astype(jnp.int32)
    carry_scr[0:1, :] = carry + cnt_al
    tot_ref[...] = jnp.broadcast_to(carry + cnt_al, tot_ref.shape).astype(jnp.int32)


def _router(x, mod, g, rw_pad, rb_pad, seq):
    n, d = x.shape
    tm = TM_ROUTE
    tpb = seq // tm
    nt = n // tm
    return pl.pallas_call(
        functools.partial(_router_kernel, tm=tm, srows=SORT_ROWS),
        out_shape=(jax.ShapeDtypeStruct((nt * SORT_ROWS, d // 2), jnp.uint32),
                   jax.ShapeDtypeStruct((n, LANES), jnp.int32),
                   jax.ShapeDtypeStruct((n, LANES), F32),
                   jax.ShapeDtypeStruct((nt, 8, LANES), jnp.int32),
                   jax.ShapeDtypeStruct((8, LANES), jnp.int32)),
        grid=(nt,),
        in_specs=[
            pl.BlockSpec((tm, d), lambda i: (i, 0)),
            pl.BlockSpec((None, 6, d), lambda i: (i // tpb, 0, 0)),
            pl.BlockSpec((1, d), lambda i: (0, 0)),
            pl.BlockSpec((d, 2 * LANES), lambda i: (0, 0)),
            pl.BlockSpec((1, LANES), lambda i: (0, 0)),
        ],
        out_specs=(pl.BlockSpec((SORT_ROWS, d // 2), lambda i: (i, 0)),
                   pl.BlockSpec((tm, LANES), lambda i: (i, 0)),
                   pl.BlockSpec((tm, LANES), lambda i: (i, 0)),
                   pl.BlockSpec((None, 8, LANES), lambda i: (i, 0, 0)),
                   pl.BlockSpec((8, LANES), lambda i: (0, 0))),
        scratch_shapes=[pltpu.VMEM((8, LANES), F32)],
        compiler_params=_cparams("arbitrary"),
        name="router_top2",
    )(x, mod, g, rw_pad, rb_pad)


def _segment_copy(src_hbm, dst_hbm, src_row, dst_row, n_rows, sem):
    src_row = pl.multiple_of(src_row, ROW_ALIGN)
    dst_row = pl.multiple_of(dst_row, ROW_ALIGN)
    n_rows = pl.multiple_of(n_rows, ROW_ALIGN)
    return pltpu.make_async_copy(src_hbm.at[pl.ds(src_row, n_rows)], dst_hbm.at[pl.ds(dst_row, n_rows)], sem)


def _expert_kernel(te_ref, used_ref, toff_ref, ilo_ref, ihi_ref, cnt_ref, carry_ref, loff_ref,
                   hs_hbm, w1_hbm, w3_hbm, w2_hbm, y_ref, wb1, wb3, wb2, stage, sem, xbuf, xsem, rows_smem,
                   *, srows):
    t = pl.program_id(0)
    e = te_ref[t]
    tg = xbuf.shape[1]
    slot = t % 2
    first_of_expert = jnp.logical_or(t == 0, e != te_ref[jnp.maximum(t - 1, 0)])

    def fetch(tile, into):
        xbuf[into] = jnp.zeros(xbuf.shape[1:], xbuf.dtype)
        expert = te_ref[tile]
        first_row = toff_ref[tile]

        def piece(i, total):
            s = i * N_EXPERTS + expert
            run_start = carry_ref[s]
            lo = jnp.maximum(run_start, first_row)
            hi = jnp.minimum(run_start + cnt_ref[s], first_row + tg)

            @pl.when(hi > lo)
            def _():
                _segment_copy(hs_hbm, xbuf.at[into], i * srows + loff_ref[s] + (lo - run_start), lo - first_row,
                              hi - lo, xsem.at[into]).start()

            return total + jnp.maximum(hi - lo, 0)

        rows_smem[into] = lax.fori_loop(ilo_ref[tile], ihi_ref[tile], piece, 0)

    @pl.when(t == 0)
    def _():
        fetch(t, slot)

    @pl.when(t + 1 < pl.num_programs(0))
    def _():
        fetch(t + 1, 1 - slot)

    @pl.when(rows_smem[slot] > 0)
    def _():
        _segment_copy(hs_hbm, xbuf.at[slot], 0, 0, rows_smem[slot], xsem.at[slot]).wait()

    @pl.when(jnp.logical_and(used_ref[t] == 1, first_of_expert))
    def _():
        n_slots = stage.shape[0]
        windows = [(src, dst, r, c) for src, dst in ((w1_hbm, wb1), (w3_hbm, wb3), (w2_hbm, wb2))
                   for r in range(dst.shape[0] // W_CHUNK) for c in range(dst.shape[1] // W_CHUNK)]

        def staged_copy(k):
            src, _, r, c = windows[k]
            return pltpu.make_async_copy(src.at[e, pl.ds(r * W_CHUNK, W_CHUNK), pl.ds(c * W_CHUNK, W_CHUNK)],
                                         stage.at[k % n_slots], sem.at[k % n_slots])

        for k in range(n_slots - 1):
            staged_copy(k).start()
        for k, (_, dst, r, c) in enumerate(windows):
            staged_copy(k).wait()
            dst[r * W_CHUNK:(r + 1) * W_CHUNK, c * W_CHUNK:(c + 1) * W_CHUNK] = stage[k % n_slots].astype(BF16)
            if k + n_slots - 1 < len(windows):
                staged_copy(k + n_slots - 1).start()

    @pl.when(used_ref[t] == 1)
    def _():
        lo, hi = _unpack_bf16_pair(xbuf[slot])
        h = jnp.concatenate([lo.astype(BF16), hi.astype(BF16)], axis=1)
        a = jnp.dot(h, wb1[...], preferred_element_type=F32)
        b = jnp.dot(h, wb3[...], preferred_element_type=F32)
        hid = ((a * _sigmoid(a)) * b).astype(BF16)
        acc = jnp.dot(hid, wb2[...], preferred_element_type=F32)
        half = acc.shape[1] // 2
        y_ref[...] = _pack_bf16_pair(acc[:, :half], acc[:, half:])

    @pl.when(used_ref[t] == 0)
    def _():
        y_ref[...] = jnp.zeros(y_ref.shape, y_ref.dtype)


def _expert_ffn(tile_tables, seg_tables, hs, w1, w3, w2, n_tiles):
    half = hs.shape[1]
    d = 2 * half
    dff = w1.shape[2]
    tg = TG_MOE
    assert dff % W_CHUNK == 0 and d % W_CHUNK == 0
    hbm = pl.BlockSpec(memory_space=pl.ANY)
    tables = tuple(tile_tables) + tuple(seg_tables)
    return pl.pallas_call(
        functools.partial(_expert_kernel, srows=SORT_ROWS),
        out_shape=jax.ShapeDtypeStruct((n_tiles * tg, half), jnp.uint32),
        grid_spec=pltpu.PrefetchScalarGridSpec(
            num_scalar_prefetch=len(tables),
            grid=(n_tiles,),
            in_specs=[hbm, hbm, hbm, hbm],
            out_specs=pl.BlockSpec((tg, half), lambda t, *_: (t, 0)),
            scratch_shapes=[pltpu.VMEM((d, dff), BF16), pltpu.VMEM((d, dff), BF16), pltpu.VMEM((dff, d), BF16),
                            pltpu.VMEM((W_SLOTS, W_CHUNK, W_CHUNK), F32), pltpu.SemaphoreType.DMA((W_SLOTS,)),
                            pltpu.VMEM((2, tg, half), jnp.uint32), pltpu.SemaphoreType.DMA((2,)),
                            pltpu.SMEM((2,), jnp.int32)],
        ),
        compiler_params=_cparams("arbitrary"),
        name="expert_swiglu",
    )(*tables, hs, w1, w3, w2)


def _combine_kernel(src_ref, loff_ref, cnt_ref, rows_ref, x_ref, ei_ref, pw_ref, mod_ref, y_ref, o_ref,
                    ybuf, sem, *, tm, srows):
    i = pl.program_id(0)
    slot = i % 2

    def fetch(tile, into):
        ybuf[into] = jnp.zeros(ybuf.shape[1:], ybuf.dtype)
        for e in range(N_EXPERTS):
            s = tile * N_EXPERTS + e
            n_rows = cnt_ref[s]

            @pl.when(n_rows > 0)
            def _():
                _segment_copy(y_ref, ybuf.at[into], src_ref[s], loff_ref[s], n_rows, sem.at[into]).start()

    @pl.when(i == 0)
    def _():
        fetch(i, slot)

    @pl.when(i + 1 < pl.num_programs(0))
    def _():
        fetch(i + 1, 1 - slot)

    @pl.when(rows_ref[i] > 0)
    def _():
        _segment_copy(y_ref, ybuf.at[slot], 0, 0, rows_ref[i], sem.at[slot]).wait()

    lo, hi = _unpack_bf16_pair(ybuf[slot])
    ys = jnp.concatenate([lo.astype(BF16), hi.astype(BF16)], axis=1)
    r_idx = lax.broadcasted_iota(jnp.int32, (tm, srows), 1)
    mix = jnp.zeros(x_ref.shape, F32)
    for k in range(TOP_K):
        pick = jnp.where(r_idx == ei_ref[:, TOP_K + k:TOP_K + k + 1], 1.0, 0.0).astype(BF16)
        mix = mix + pw_ref[:, k:k + 1] * jnp.dot(pick, ys, preferred_element_type=F32)
    o_ref[...] = x_ref[...] + mod_ref[5:6, :] * mix


def _combine(seg_src, seg_loff, seg_cnt, tile_rows, x, ei, pw, mod, y, seq):
    n, d = x.shape
    tm = TM_ROUTE
    tpb = seq // tm
    tok = lambda width: pl.BlockSpec((tm, width), lambda i, *_: (i, 0))
    return pl.pallas_call(
        functools.partial(_combine_kernel, tm=tm, srows=SORT_ROWS),
        out_shape=jax.ShapeDtypeStruct((n, d), F32),
        grid_spec=pltpu.PrefetchScalarGridSpec(
            num_scalar_prefetch=4,
            grid=(n // tm,),
            in_specs=[tok(d), tok(LANES), tok(LANES),
                      pl.BlockSpec((None, 6, d), lambda i, *_: (i // tpb, 0, 0)),
                      pl.BlockSpec(memory_space=pl.ANY)],
            out_specs=tok(d),
            scratch_shapes=[pltpu.VMEM((2, SORT_ROWS, d // 2), jnp.uint32), pltpu.SemaphoreType.DMA((2,))],
        ),
        compiler_params=_cparams("arbitrary"),
        name="moe_combine",
    )(seg_src, seg_loff, seg_cnt, tile_rows, x, ei, pw, mod, y)


def _pack_w_in(w_in_all, layer):
    d = w_in_all.shape[1]
    col = lambda start, width: w_in_all[layer, :, start:start + width]
    w = MIX_W
    o_ckv = 3 * w + Q_LORA
    o_kr = o_ckv + KV_LORA
    o_su = o_kr + QK_ROPE
    o_rq = o_su + 2 * w
    o_gate = o_rq + 4 * w
    half = HEAD_DIM // 2
    perm = np.array([h * HEAD_DIM + part * half + i
                     for part in range(2) for h in range(N_GROUPS) for i in range(half)])
    z = lambda k: jnp.zeros((d, k), w_in_all.dtype)
    cols = [
        col(o_gate, N_BRANCH * d),
        col(0, 3 * w + Q_LORA),
        col(o_rq, w)[:, perm], col(o_rq + w, w)[:, perm],
        col(o_rq + 2 * w, 2 * w),
        col(o_su, 2 * w),
        col(o_ckv, KV_LORA),
        z(QK_NOPE), col(o_kr, QK_ROPE), z(LANES - QK_HEAD),
    ]
    return jnp.concatenate(cols, axis=1).astype(BF16)


def _swap_rope_halves(a):
    hr = QK_ROPE // 2
    return jnp.concatenate([a[..., :QK_NOPE], a[..., QK_NOPE + hr:QK_HEAD], a[..., QK_NOPE:QK_NOPE + hr],
                            a[..., QK_HEAD:]], axis=-1)


def _mla_params(cq_g, w_uq, ckv_g, w_ukv, qn_g, kn_g):
    pad = LANES - QK_HEAD
    wq = w_uq.reshape(Q_LORA, N_GROUPS, QK_HEAD)
    wq = jnp.pad(wq, ((0, 0), (0, 0), (0, pad)))
    wkv = w_ukv.reshape(KV_LORA, N_GROUPS, QK_NOPE + V_HEAD)
    wk = jnp.pad(wkv[:, :, :QK_NOPE], ((0, 0), (0, 0), (0, LANES - QK_NOPE)))
    wv = jnp.pad(wkv[:, :, QK_NOPE:], ((0, 0), (0, 0), (0, LANES - V_HEAD)))
    qg = jnp.pad(qn_g, (0, pad))[None, :]
    kg = jnp.pad(kn_g, (0, pad))[None, :]
    bound = (QK_HEAD ** 0.5 * LOG2_E) * jnp.max(jnp.abs(qn_g)) * jnp.max(jnp.abs(kn_g))
    static_shift = bound <= MAX_STATIC_SHIFT
    lane = jnp.arange(LANES)
    qaug = (lane == QK_HEAD).astype(F32)[None, :]
    kaug = qaug * jnp.where(static_shift, -bound, 0.0)
    vaug = jnp.tile((lane == V_HEAD).astype(F32), N_GROUPS)[None, :]
    params = {
        "cq_g": cq_g[None, :], "ckv_g": ckv_g[None, :],
        "wqa": wq.reshape(Q_LORA, -1).astype(BF16),
        "wqb": _swap_rope_halves(wq).reshape(Q_LORA, -1).astype(BF16),
        "wk": wk.reshape(KV_LORA, -1).astype(BF16),
        "wv": wv.reshape(KV_LORA, -1).astype(BF16),
        "qga": qg, "qgb": _swap_rope_halves(qg), "kga": kg, "kgb": _swap_rope_halves(kg),
        "qaug": qaug, "kaug": kaug, "vaug": vaug,
        "swap": (_swap_rope_halves(lane[None, :])[0][None, :] == lane[:, None]).astype(BF16),
    }
    return params, static_shift


def _mixer_consts():
    f32 = np.float32
    h = np.arange(N_GROUPS, dtype=f32)
    log_gamma = np.log1p(-(f32(2.0) ** (f32(-5.0) - h))).astype(f32)
    pos = np.arange(CHUNK, dtype=f32)
    rel = pos[:, None] - pos[None, :]
    dec = np.where(rel >= 0, np.exp(log_gamma[:, None, None] * np.maximum(rel, f32(0.0))), f32(0.0)).astype(f32)
    lane = np.arange(MIX_W)
    head_k = (lane % LANES) // (HEAD_DIM // 2)
    head_v = lane // HEAD_DIM
    lg_k = log_gamma[head_k]
    return {
        "dec": jnp.asarray(dec.reshape(N_GROUPS * CHUNK, CHUNK)),
        "kdec": jnp.asarray(np.exp(lg_k[None, :] * (CHUNK - 1.0 - pos)[:, None]).astype(f32)),
        "qdec": jnp.asarray(np.exp(lg_k[None, :] * (pos + 1.0)[:, None]).astype(f32)),
        "cdec": jnp.asarray(np.broadcast_to(np.exp(lg_k * f32(CHUNK)).astype(f32)[:, None], (MIX_W, MIX_W))),
        "bd": jnp.asarray((head_k[:, None] == head_v[None, :]).astype(f32)),
        "gmat": jnp.asarray((head_v[:, None] == head_v[None, :]).astype(f32) / HEAD_DIM).astype(BF16),
        "mk": jnp.asarray((head_k[None, :] == np.arange(N_GROUPS)[:, None]).astype(f32)),
        "mv": jnp.asarray((head_v[None, :] == np.arange(N_GROUPS)[:, None]).astype(f32)),
    }


def _mixer_params(consts, conv_w, gv_g, w_s, b_s, ret_g, w_branch, w_o):
    p = dict(consts)
    ws = jnp.tril(w_s)
    p.update({
        "conv_w": conv_w,
        "gv_g": gv_g.reshape(1, MIX_W),
        "ws_cat": jnp.transpose(ws, (1, 0, 2)).reshape(CHUNK, N_GROUPS * CHUNK).astype(BF16),
        "bs_mat": jnp.repeat(b_s.T, HEAD_DIM, axis=1),
        "ret_g": ret_g.reshape(1, MIX_W),
        "w_branch": w_branch.astype(BF16),
        "w_o": w_o.astype(BF16),
    })
    return p


def _moe_layout(meta, tot, n_tiles):
    totals = tot[0, :N_EXPERTS]
    padded = ((totals + TG_MOE - 1) // TG_MOE) * TG_MOE
    ends = jnp.cumsum(padded)
    starts = ends - padded
    seg_cnt = meta[:, 0, :N_EXPERTS]
    seg_carry = meta[:, 1, :N_EXPERTS]
    seg_loff = meta[:, 2, :N_EXPERTS]
    seg_grouped = starts[None, :] + seg_carry
    tile_start = jnp.arange(n_tiles, dtype=jnp.int32) * TG_MOE
    tile_e = jnp.sum((tile_start[:, None] >= ends[None, :]).astype(jnp.int32), axis=1)
    used = (tile_start < ends[-1]).astype(jnp.int32)
    last_e = jnp.sum((ends[-1] - 1 >= ends).astype(jnp.int32))
    tile_e = jnp.minimum(jnp.where(used == 1, tile_e, last_e), N_EXPERTS - 1)
    tile_off = tile_start - starts[tile_e]
    run_start = seg_carry[:, tile_e]
    run_end = run_start + seg_cnt[:, tile_e]
    tile_ilo = jnp.sum((run_end <= tile_off[None, :]).astype(jnp.int32), axis=0) * used
    tile_ihi = jnp.sum((run_start < tile_off[None, :] + TG_MOE).astype(jnp.int32), axis=0) * used
    flat = lambda a: a.reshape(-1).astype(jnp.int32)
    tile_tables = (flat(tile_e), flat(used), flat(tile_off * used), flat(tile_ilo), flat(tile_ihi))
    seg_tables = (flat(seg_cnt), flat(seg_carry), flat(seg_loff))
    return tile_tables, seg_tables, flat(seg_grouped)


def kernel(x, c, positions, norm1_g, norm2_g, ada_w, ada_b, w_in, conv_w, cq_g, w_uq, ckv_g, w_ukv, qn_g, kn_g, gv_g, w_s, b_s, ret_g, w_branch, w_o, ffn_w1, ffn_w3, ffn_w2, router_w, router_b, moe_w1, moe_w3, moe_w2):
    batch, seq, d = x.shape
    depth = ada_w.shape[0]
    n = batch * seq
    assert seq % max(TM_PROJ, TQ_ATT, TM_MIX, TM_FFN, TM_ROUTE) == 0
    assert d // 2 % LANES == 0

    c_t = jnp.pad(c, ((0, 8 - batch), (0, 0))).T
    ada = _ada(c_t, ada_w, ada_b, batch)[:, :batch].reshape(depth, batch, 6, d)
    cosr, sinr, cm, sm = _rope_tables(positions.astype(F32).reshape(n, 1))

    mixer_consts = _mixer_consts()
    xt = x.reshape(n, d)
    for l in range(depth):
        mod = ada[l]
        mla_p, static_shift = _mla_params(cq_g[l], w_uq[l], ckv_g[l], w_ukv[l], qn_g[l], kn_g[l])
        proj, q, k, v = _inproj(xt, mod, norm1_g[l][None, :], _pack_w_in(w_in, l), cm, sm, mla_p, seq)
        y_mla = lax.cond(static_shift,
                         functools.partial(_flash, batch=batch, seq=seq, online_max=False),
                         functools.partial(_flash, batch=batch, seq=seq, online_max=True), q, k, v)
        mp = _mixer_params(mixer_consts, conv_w[l], gv_g[l], w_s[l], b_s[l], ret_g[l], w_branch[l], w_o[l])
        xt = _mixers(proj, y_mla, xt, cosr, sinr, mod, mp, seq)
        g2n = norm2_g[l][None, :]
        if l % 2 == 0:
            i = l // 2
            xt = _dense_ffn(xt, mod, g2n, ffn_w1[i].astype(BF16), ffn_w3[i].astype(BF16),
                            ffn_w2[i].astype(BF16), seq)
        else:
            i = l // 2
            rw = jnp.pad(router_w[i], ((0, 0), (0, LANES - N_EXPERTS)))
            rw_hi = rw.astype(BF16)
            rw_pad = jnp.concatenate([rw_hi, (rw - rw_hi.astype(F32)).astype(BF16)], axis=1)
            rb_pad = jnp.pad(router_b[i], (0, LANES - N_EXPERTS), constant_values=-1e30)[None, :]
            hs, ei, pw, meta, tot = _router(xt, mod, g2n, rw_pad, rb_pad, seq)
            max_rows = n * TOP_K + N_EXPERTS * (n // TM_ROUTE) * (ROW_ALIGN - 1)
            n_tiles = -(-max_rows // TG_MOE) + N_EXPERTS
            tile_tables, seg_tables, seg_grouped = _moe_layout(meta, tot, n_tiles)
            seg_cnt, _, seg_loff = seg_tables
            tile_rows = jnp.sum(seg_cnt.reshape(-1, N_EXPERTS), axis=1)
            y = _expert_ffn(tile_tables, seg_tables, hs, moe_w1[i], moe_w3[i], moe_w2[i], n_tiles)
            xt = _combine(seg_grouped, seg_loff, seg_cnt, tile_rows, xt, ei, pw, mod, y, seq)
    return xt.reshape(batch, seq, d)
```

```python
import functools

import jax
import jax.numpy as jnp
import numpy as np
from jax import lax
from jax.experimental import pallas as pl
from jax.experimental.pallas import tpu as pltpu

F32 = jnp.float32
BF16 = jnp.bfloat16
HIGHEST = lax.Precision.HIGHEST

HEAD_DIM = 64
N_GROUPS = 4
MIX_W = N_GROUPS * HEAD_DIM
N_BRANCH = 4
CONV_W = 3
Q_LORA = 256
KV_LORA = 128
QK_NOPE = 64
QK_ROPE = 32
QK_HEAD = QK_NOPE + QK_ROPE
V_HEAD = 64
CHUNK = 128
N_EXPERTS = 8
TOP_K = 2
ROPE_THETA = 10000.0
EPS = 1e-6
LOG2_E = 1.4426950408889634
MAX_STATIC_SHIFT = 50.0

LANES = 128
VMEM_LIMIT_BYTES = 56 * 1024 * 1024

COL_GATES = 0
COL_A = 4096
COL_CQ = COL_A + 3 * MIX_W
COL_R = 5120
COL_SU = 6144
COL_CKV = 6656
COL_KRA = 6784
N_IN = 6912

TM_PROJ = 512
TN_PROJ = 768
TQ_ATT = 1024
TM_MIX = 512
TM_FFN = 512
TM_ROUTE = 512
ROW_ALIGN = 8
SORT_ROWS = TOP_K * TM_ROUTE + N_EXPERTS * ROW_ALIGN
TG_MOE = 512
W_CHUNK = 512
W_SLOTS = 6


def _cparams(*sem):
    return pltpu.CompilerParams(dimension_semantics=sem, vmem_limit_bytes=VMEM_LIMIT_BYTES)


def _sigmoid(x):
    return jnp.tanh(x * 0.5) * 0.5 + 0.5


def _group_mean(x, gmat_bf16):
    hi = x.astype(BF16)
    lo = (x - hi.astype(F32)).astype(BF16)
    return (jnp.dot(hi, gmat_bf16, preferred_element_type=F32)
            + jnp.dot(lo, gmat_bf16, preferred_element_type=F32))


def _pack_bf16_pair(lo, hi):
    lo_bits = lax.bitcast_convert_type(lo.astype(BF16).astype(F32), jnp.uint32)
    hi_bits = lax.bitcast_convert_type(hi.astype(BF16).astype(F32), jnp.uint32)
    return (lo_bits >> 16) | (hi_bits & jnp.uint32(0xFFFF0000))


def _unpack_bf16_pair(p):
    lo = lax.bitcast_convert_type(p << 16, F32)
    hi = lax.bitcast_convert_type(p & jnp.uint32(0xFFFF0000), F32)
    return lo, hi


def _norm_mod(x, g, shift, scale):
    y = x * lax.rsqrt(jnp.mean(x * x, axis=-1, keepdims=True) + EPS)
    return (y * g) * (1.0 + scale) + shift


def _ada_kernel(ct_ref, w_ref, b_ref, o_ref, *, batch):
    ct = ct_ref[...]
    cond = ct * _sigmoid(ct)
    w = w_ref[...]
    o_ref[...] = jnp.zeros(o_ref.shape, F32)
    for b in range(batch):
        o_ref[b:b + 1, :] = jnp.sum(w * cond[:, b:b + 1], axis=0, keepdims=True) + b_ref[...]


def _ada(c_t, ada_w, ada_b, batch):
    n_layer, d, d6 = ada_w.shape
    rows = c_t.shape[1]
    tn = 2048
    return pl.pallas_call(
        functools.partial(_ada_kernel, batch=batch),
        out_shape=jax.ShapeDtypeStruct((n_layer, rows, d6), F32),
        grid=(n_layer, d6 // tn),
        in_specs=[
            pl.BlockSpec((d, rows), lambda l, j: (0, 0)),
            pl.BlockSpec((None, d, tn), lambda l, j: (l, 0, j)),
            pl.BlockSpec((None, 1, tn), lambda l, j: (l, 0, j)),
        ],
        out_specs=pl.BlockSpec((None, rows, tn), lambda l, j: (l, 0, j)),
        compiler_params=_cparams("parallel", "parallel"),
        name="ada_mod",
    )(c_t, ada_w, ada_b.reshape(n_layer, 1, d6))


def _rope_kernel(pos_ref, inv_ref, cr_ref, sr_ref, cm_ref, sm_ref):
    half_r = HEAD_DIM // 2
    half_m = QK_ROPE // 2
    tm = pos_ref.shape[0]
    low = lax.broadcasted_iota(jnp.int32, (tm // 2, LANES), 1) < LANES // 2
    ang = jnp.where(low, pos_ref[0:tm // 2, :], pos_ref[tm // 2:tm, :]) * inv_ref[...]
    c = jnp.cos(ang)
    s = jnp.sin(ang)
    c = jnp.concatenate([c, pltpu.roll(c, LANES // 2, axis=1)], axis=0)
    s = jnp.concatenate([s, pltpu.roll(s, LANES // 2, axis=1)], axis=0)
    lane = lax.broadcasted_iota(jnp.int32, c.shape, 1)

    def tile_r(t):
        t = jnp.where(lane < half_r, t, 0.0)
        out = t
        for k in range(1, LANES // half_r):
            out = out + pltpu.roll(t, k * half_r, axis=1)
        return out

    cr_ref[...] = tile_r(c)
    sr_ref[...] = tile_r(s)
    first = jnp.logical_and(lane >= QK_NOPE, lane < QK_NOPE + half_m)
    second = jnp.logical_and(lane >= QK_NOPE + half_m, lane < QK_HEAD)
    c1, c2 = pltpu.roll(c, QK_NOPE - half_r, axis=1), pltpu.roll(c, QK_NOPE + half_m - half_r, axis=1)
    s1, s2 = pltpu.roll(s, QK_NOPE - half_r, axis=1), pltpu.roll(s, QK_NOPE + half_m - half_r, axis=1)
    cm_ref[...] = jnp.where(first, c1, jnp.where(second, c2, 1.0))
    sm_ref[...] = jnp.where(first, -s1, jnp.where(second, s2, 0.0))


def _rope_tables(pos_f):
    n = pos_f.shape[0]
    tm = 1024
    half_r = HEAD_DIM // 2
    half_m = QK_ROPE // 2
    inv_r = ROPE_THETA ** (-jnp.arange(half_r, dtype=F32) / half_r)
    inv_m = ROPE_THETA ** (-jnp.arange(half_m, dtype=F32) / half_m)
    inv = jnp.concatenate([inv_r, inv_m, jnp.zeros((LANES // 2 - half_r - half_m,), F32)])
    inv = jnp.tile(inv, 2)[None, :]
    tab = pl.BlockSpec((tm, LANES), lambda i: (i, 0))
    shape = jax.ShapeDtypeStruct((n, LANES), F32)
    return pl.pallas_call(
        _rope_kernel,
        out_shape=(shape, shape, shape, shape),
        grid=(n // tm,),
        in_specs=[pl.BlockSpec((tm, 1), lambda i: (i, 0)), pl.BlockSpec((1, LANES), lambda i: (0, 0))],
        out_specs=(tab, tab, tab, tab),
        compiler_params=_cparams("parallel"),
        name="rope_tables",
    )(pos_f, inv)


def _resident(shape):
    return pl.BlockSpec(shape, lambda *_: (0,) * len(shape), pipeline_mode=pl.Buffered(1))


def _mla_prep(cq_b, ckv_b, kra_b, cm, sm, cqg_ref, wqa_ref, wqb_ref, ckvg_ref, wk_ref, wv_ref, qga_ref, qgb_ref,
              kga_ref, kgb_ref, qaug_ref, kaug_ref, vaug_ref, swap_ref, q_ref, k_ref, v_ref):
    cq = cq_b.astype(F32)
    cqn = (cq * lax.rsqrt(jnp.mean(cq * cq, axis=-1, keepdims=True) + EPS) * cqg_ref[...]).astype(BF16)
    qa = jnp.dot(cqn, wqa_ref[...], preferred_element_type=F32)
    qb = jnp.dot(cqn, wqb_ref[...], preferred_element_type=F32)
    ckv = ckv_b.astype(F32)
    ckvn = (ckv * lax.rsqrt(jnp.mean(ckv * ckv, axis=-1, keepdims=True) + EPS) * ckvg_ref[...]).astype(BF16)
    ka = jnp.dot(ckvn, wk_ref[...], preferred_element_type=F32)
    v_ref[...] = (jnp.dot(ckvn, wv_ref[...], preferred_element_type=F32) + vaug_ref[...]).astype(BF16)
    kra = kra_b.astype(F32)
    krb = jnp.dot(kra_b, swap_ref[...], preferred_element_type=F32)
    scale = QK_HEAD ** -0.5 * LOG2_E
    q_cos, q_sin = cm * (qga_ref[...] * scale), sm * (qgb_ref[...] * scale)
    k_cos, k_sin = cm * kga_ref[...], sm * kgb_ref[...]
    for h in range(N_GROUPS):
        sl = slice(h * LANES, (h + 1) * LANES)
        qah, qbh = qa[:, sl], qb[:, sl]
        r = lax.rsqrt(jnp.sum(qah * qah, axis=-1, keepdims=True) * (1.0 / QK_HEAD) + EPS)
        q_ref[:, sl] = ((qah * q_cos + qbh * q_sin) * r + qaug_ref[...]).astype(BF16)
        kah = ka[:, sl] + kra
        kbh = ka[:, sl] + krb
        r = lax.rsqrt(jnp.sum(kah * kah, axis=-1, keepdims=True) * (1.0 / QK_HEAD) + EPS)
        k_ref[:, sl] = ((kah * k_cos + kbh * k_sin) * r + kaug_ref[...]).astype(BF16)


def _inproj_kernel(x_ref, mod_ref, g_ref, w_ref, cm_ref, sm_ref, *rest):
    mla_refs, (o_ref, q_ref, k_ref, v_ref) = rest[:-4], rest[-4:]
    h = _norm_mod(x_ref[...], g_ref[...], mod_ref[0:1, :], mod_ref[1:2, :]).astype(BF16)

    def chunk(c):
        cols = slice(c * TN_PROJ, (c + 1) * TN_PROJ)
        out = jnp.dot(h, w_ref[:, cols], preferred_element_type=F32).astype(BF16)
        o_ref[:, cols] = out
        return out

    c_q, c_kv = COL_CQ // TN_PROJ, COL_CKV // TN_PROJ
    lat_q = chunk(c_q)
    lat_kv = chunk(c_kv)
    q0, kv0, kr0 = COL_CQ - c_q * TN_PROJ, COL_CKV - c_kv * TN_PROJ, COL_KRA - c_kv * TN_PROJ
    _mla_prep(lat_q[:, q0:q0 + Q_LORA], lat_kv[:, kv0:kv0 + KV_LORA], lat_kv[:, kr0:kr0 + LANES],
              cm_ref[...], sm_ref[...], *mla_refs, q_ref, k_ref, v_ref)
    for c in range(N_IN // TN_PROJ):
        if c not in (c_q, c_kv):
            chunk(c)


def _inproj(x, mod, g, w, cm, sm, p, seq):
    n, d = x.shape
    tm = TM_PROJ
    tpb = seq // tm
    hw = N_GROUPS * LANES
    assert COL_KRA // TN_PROJ == COL_CKV // TN_PROJ and (COL_CQ + Q_LORA - 1) // TN_PROJ == COL_CQ // TN_PROJ

    def full(a):
        return pl.BlockSpec(a.shape, lambda i: (0,) * a.ndim)

    weights = [p["cq_g"], p["wqa"], p["wqb"], p["ckv_g"], p["wk"], p["wv"],
               p["qga"], p["qgb"], p["kga"], p["kgb"], p["qaug"], p["kaug"], p["vaug"], p["swap"]]
    table = pl.BlockSpec((tm, LANES), lambda i: (i, 0))
    head_tile = pl.BlockSpec((tm, hw), lambda i: (i, 0))
    heads = jax.ShapeDtypeStruct((n, hw), BF16)
    return pl.pallas_call(
        _inproj_kernel,
        out_shape=(jax.ShapeDtypeStruct((n, N_IN), BF16), heads, heads, heads),
        grid=(n // tm,),
        in_specs=[
            pl.BlockSpec((tm, d), lambda i: (i, 0)),
            pl.BlockSpec((None, 6, d), lambda i: (i // tpb, 0, 0)),
            pl.BlockSpec((1, d), lambda i: (0, 0)),
            _resident((d, N_IN)), table, table,
        ] + [full(a) for a in weights],
        out_specs=(pl.BlockSpec((tm, N_IN), lambda i: (i, 0)), head_tile, head_tile, head_tile),
        compiler_params=_cparams("parallel"),
        name="in_proj",
    )(x, mod, g, w, cm, sm, *weights)


def _flash_kernel(qi_ref, kj_ref, q_ref, k_ref, v_ref, o_ref, acc_scr, *rest, tq, online_max):
    i = qi_ref[pl.program_id(1)]
    j = kj_ref[pl.program_id(1)]

    @pl.when(j == 0)
    def _():
        acc_scr[...] = jnp.zeros(acc_scr.shape, F32)
        if online_max:
            rest[0][...] = jnp.full(rest[0].shape, -jnp.inf, F32)

    def block(q0, nq, nk, masked, k0=0):
        rows = slice(q0, q0 + nq)
        keys = slice(k0, k0 + nk)
        if masked:
            row = lax.broadcasted_iota(jnp.int32, (nq, nk), 0) + q0
            col = lax.broadcasted_iota(jnp.int32, (nq, nk), 1)
            keep = col <= row
        for h in range(N_GROUPS):
            sl = slice(h * LANES, (h + 1) * LANES)
            s = lax.dot_general(q_ref[rows, sl], k_ref[keys, sl], (((1,), (1,)), ((), ())),
                                preferred_element_type=F32)
            if masked:
                s = jnp.where(keep, s, -jnp.inf)
            if online_max:
                m_scr = rest[0]
                m_prev = m_scr[h, rows]
                m_new = jnp.maximum(m_prev, jnp.max(s, axis=-1, keepdims=True))
                p = jnp.exp2(s - m_new).astype(BF16)
                acc_scr[h, rows] = jnp.exp2(m_prev - m_new) * acc_scr[h, rows] + jnp.dot(
                    p, v_ref[keys, sl], preferred_element_type=F32)
                m_scr[h, rows] = m_new
            else:
                acc_scr[h, rows] += jnp.dot(jnp.exp2(s).astype(BF16), v_ref[keys, sl],
                                            preferred_element_type=F32)

    def diagonal(k0):
        block(0, tq // 2, tq // 2, True, k0)
        block(tq // 2, tq // 2, tq, True, k0)

    @pl.when(2 * j + 1 < i)
    def _():
        block(0, tq, tq, False, 0)
        block(0, tq, tq, False, tq)

    @pl.when(2 * j + 1 == i)
    def _():
        block(0, tq, tq, False, 0)
        diagonal(tq)

    @pl.when(2 * j == i)
    def _():
        diagonal(0)

    @pl.when(2 * j + 1 >= i)
    def _():
        lane = lax.broadcasted_iota(jnp.int32, (tq, LANES), 1)
        for pr in range(N_GROUPS // 2):
            lo = acc_scr[2 * pr]
            hi = acc_scr[2 * pr + 1]
            lo = lo / lo[:, V_HEAD:V_HEAD + 1]
            hi = hi / hi[:, V_HEAD:V_HEAD + 1]
            both = jnp.where(lane < V_HEAD, lo, pltpu.roll(hi, V_HEAD, axis=1))
            o_ref[:, pr * LANES:(pr + 1) * LANES] = both.astype(BF16)


def _flash(q, k, v, batch, seq, online_max):
    n = q.shape[0]
    tq = TQ_ATT
    nq = seq // tq
    hw = N_GROUPS * LANES
    scratch = [pltpu.VMEM((N_GROUPS, tq, LANES), F32)]
    if online_max:
        scratch.append(pltpu.VMEM((N_GROUPS, tq, 1), F32))
    assert nq % 2 == 0
    pairs = [(i, j) for i in range(nq) for j in range(i // 2 + 1)]
    qi = jnp.asarray([p[0] for p in pairs], jnp.int32)
    kj = jnp.asarray([p[1] for p in pairs], jnp.int32)
    q_tile = lambda b, s, qi, kj: (b * nq + qi[s], 0)
    k_tile = lambda b, s, qi, kj: (b * (nq // 2) + kj[s], 0)
    return pl.pallas_call(
        functools.partial(_flash_kernel, tq=tq, online_max=online_max),
        out_shape=jax.ShapeDtypeStruct((n, MIX_W), BF16),
        grid_spec=pltpu.PrefetchScalarGridSpec(
            num_scalar_prefetch=2,
            grid=(batch, len(pairs)),
            in_specs=[pl.BlockSpec((tq, hw), q_tile), pl.BlockSpec((2 * tq, hw), k_tile),
                      pl.BlockSpec((2 * tq, hw), k_tile)],
            out_specs=pl.BlockSpec((tq, MIX_W), q_tile),
            scratch_shapes=scratch,
        ),
        compiler_params=_cparams("parallel", "arbitrary"),
        name="mla_flash_online" if online_max else "mla_flash",
    )(qi, kj, q, k, v)


def _gelu_tanh(x):
    return jax.nn.gelu(x, approximate=True)


def _mix_kernel(gates_ref, a_ref, r_ref, su_ref, ymla_ref, x_ref, cos_ref, sin_ref, mod_ref,
                convw_ref, gvg_ref, wscat_ref, bsmat_ref, retg_ref, dec_ref, kdec_ref, qdec_ref,
                cdec_ref, bd_ref, gmat_ref, mk_ref, mv_ref, wb_ref, wo_ref,
                o_ref, carry_scr, state_scr, ysg_scr, yret_scr, *, tm, tpb):
    i = pl.program_id(0)

    @pl.when(i % tpb == 0)
    def _():
        carry_scr[...] = jnp.zeros(carry_scr.shape, F32)
        state_scr[...] = jnp.zeros(state_scr.shape, F32)

    w = MIX_W
    a_b = a_ref[:, 0:w].astype(F32)
    u = a_ref[:, w:2 * w].astype(F32) * a_ref[:, 2 * w:3 * w].astype(F32)
    rowi = lax.broadcasted_iota(jnp.int32, (tm, w), 0)
    prev1 = carry_scr[0:1, :]
    prev2 = carry_scr[1:2, :]
    u1 = jnp.where(rowi == 0, prev1, pltpu.roll(u, 1, axis=0))
    u2 = jnp.where(rowi == 0, prev2, jnp.where(rowi == 1, prev1, pltpu.roll(u, 2, axis=0)))
    carry_scr[0:1, :] = u[tm - 1:tm, :]
    carry_scr[1:2, :] = u[tm - 2:tm - 1, :]
    y_conv = a_b * (convw_ref[0:1, :] * u2 + convw_ref[1:2, :] * u1 + convw_ref[2:3, :] * u)

    gmat = gmat_ref[...]
    s_u = _gelu_tanh(su_ref[:, 0:w].astype(F32))
    s_v = _gelu_tanh(su_ref[:, w:2 * w].astype(F32))
    ms = _group_mean(s_v * s_v, gmat)
    vn = (s_v * lax.rsqrt(ms + EPS) * gvg_ref[...]).astype(BF16)

    cosr = cos_ref[...]
    sinr = sin_ref[...]

    def rot(t):
        t1, t2 = t[:, 0:LANES], t[:, LANES:2 * LANES]
        return jnp.concatenate([t1 * cosr - t2 * sinr, t2 * cosr + t1 * sinr], axis=-1)

    rq = rot(r_ref[:, 0:w].astype(F32))
    rk = rot(r_ref[:, w:2 * w].astype(F32)) * (HEAD_DIM ** -0.5)

    for c in range(tm // CHUNK):
        rows = slice(c * CHUNK, (c + 1) * CHUNK)
        vc = vn[rows, :]
        vbd = jnp.concatenate([vc * mv_ref[g:g + 1, :].astype(BF16) for g in range(N_GROUPS)], axis=0)
        mixed = jnp.dot(wscat_ref[...], vbd, preferred_element_type=F32) + bsmat_ref[...]
        ysg_scr[rows, :] = s_u[rows, :] * mixed

        qc = rq[rows, :]
        kc = rk[rows, :]
        kcb = kc.astype(BF16)
        vcb = r_ref[rows, 2 * w:3 * w]
        qstack = jnp.concatenate([(qc * mk_ref[h:h + 1, :]).astype(BF16) for h in range(N_GROUPS)], axis=0)
        sc = lax.dot_general(qstack, kcb, (((1,), (1,)), ((), ())), preferred_element_type=F32)
        sc = (sc * dec_ref[...]).astype(BF16)
        scat = jnp.concatenate([sc[h * CHUNK:(h + 1) * CHUNK, :] for h in range(N_GROUPS)], axis=1)
        vstack = jnp.concatenate([vcb * mv_ref[h:h + 1, :].astype(BF16) for h in range(N_GROUPS)], axis=0)
        o_c = jnp.dot(scat, vstack, preferred_element_type=F32)
        state = state_scr[...]
        o_c = o_c + jnp.dot((qc * qdec_ref[...]).astype(BF16), state.astype(BF16),
                            preferred_element_type=F32)
        kd_t = jnp.transpose(kc * kdec_ref[...]).astype(BF16)
        kv = jnp.dot(kd_t, vcb, preferred_element_type=F32)
        state_scr[...] = state * cdec_ref[...] + kv * bd_ref[...]
        yret_scr[rows, :] = o_c

    o_all = yret_scr[...]
    xc = o_all - _group_mean(o_all, gmat)
    var = _group_mean(xc * xc, gmat)
    r_g = r_ref[:, 3 * w:4 * w].astype(F32)
    y_ret = (r_g * _sigmoid(r_g)) * (xc * lax.rsqrt(var + EPS) * retg_ref[...])

    d = x_ref.shape[1]
    ys = (y_conv, ymla_ref[...], ysg_scr[...], y_ret)
    merged = None
    for n in range(N_BRANCH):
        gate = _sigmoid(gates_ref[:, n * d:(n + 1) * d])
        term = gate * jnp.dot(ys[n].astype(BF16), wb_ref[n], preferred_element_type=F32).astype(BF16)
        merged = term if merged is None else merged + term
    out = jnp.dot(merged, wo_ref[...], preferred_element_type=F32)
    o_ref[...] = x_ref[...] + mod_ref[2:3, :] * out


def _mixers(proj, ymla, x, cosr, sinr, mod, p, seq):
    n, d = x.shape
    tm = TM_MIX
    tpb = seq // tm

    def col(width, offset):
        return pl.BlockSpec((tm, width), lambda i: (i, offset // width))

    def full(a):
        return pl.BlockSpec(a.shape, lambda i: (0,) * a.ndim)

    consts = [p["conv_w"], p["gv_g"], p["ws_cat"], p["bs_mat"], p["ret_g"], p["dec"], p["kdec"],
              p["qdec"], p["cdec"], p["bd"], p["gmat"], p["mk"], p["mv"], p["w_branch"], p["w_o"]]
    return pl.pallas_call(
        functools.partial(_mix_kernel, tm=tm, tpb=tpb),
        out_shape=jax.ShapeDtypeStruct((n, d), F32),
        grid=(n // tm,),
        in_specs=[col(N_BRANCH * d, COL_GATES), col(4 * MIX_W, COL_A), col(4 * MIX_W, COL_R),
                  col(2 * MIX_W, COL_SU),
                  pl.BlockSpec((tm, MIX_W), lambda i: (i, 0)),
                  pl.BlockSpec((tm, d), lambda i: (i, 0)),
                  pl.BlockSpec((tm, LANES), lambda i: (i, 0)),
                  pl.BlockSpec((tm, LANES), lambda i: (i, 0)),
                  pl.BlockSpec((None, 6, d), lambda i: (i // tpb, 0, 0))]
                 + [full(c) for c in consts],
        out_specs=pl.BlockSpec((tm, d), lambda i: (i, 0)),
        scratch_shapes=[pltpu.VMEM((8, MIX_W), F32), pltpu.VMEM((MIX_W, MIX_W), F32),
                        pltpu.VMEM((tm, MIX_W), F32), pltpu.VMEM((tm, MIX_W), F32)],
        compiler_params=_cparams("arbitrary"),
        name="mixers_merge",
    )(proj, proj, proj, proj, ymla, x, cosr, sinr, mod, *consts)


def _ffn_kernel(x_ref, mod_ref, g_ref, w1_ref, w3_ref, w2_ref, o_ref):
    x = x_ref[...]
    h = _norm_mod(x, g_ref[...], mod_ref[3:4, :], mod_ref[4:5, :]).astype(BF16)
    a = jnp.dot(h, w1_ref[...], preferred_element_type=F32)
    b = jnp.dot(h, w3_ref[...], preferred_element_type=F32)
    hid = ((a * _sigmoid(a)) * b).astype(BF16)
    o_ref[...] = x + mod_ref[5:6, :] * jnp.dot(hid, w2_ref[...], preferred_element_type=F32)


def _dense_ffn(x, mod, g, w1, w3, w2, seq):
    n, d = x.shape
    dff = w1.shape[1]
    tm = TM_FFN
    tpb = seq // tm
    return pl.pallas_call(
        _ffn_kernel,
        out_shape=jax.ShapeDtypeStruct((n, d), F32),
        grid=(n // tm,),
        in_specs=[
            pl.BlockSpec((tm, d), lambda i: (i, 0)),
            pl.BlockSpec((None, 6, d), lambda i: (i // tpb, 0, 0)),
            pl.BlockSpec((1, d), lambda i: (0, 0)),
            _resident((d, dff)), _resident((d, dff)), _resident((dff, d)),
        ],
        out_specs=pl.BlockSpec((tm, d), lambda i: (i, 0)),
        compiler_params=_cparams("parallel"),
        name="dense_swiglu",
    )(x, mod, g, w1, w3, w2)


def _router_kernel(x_ref, mod_ref, g_ref, rw_ref, rb_ref, hs_ref, ei_ref, pw_ref, meta_ref, tot_ref,
                   carry_scr, *, tm, srows):
    i = pl.program_id(0)

    @pl.when(i == 0)
    def _():
        carry_scr[...] = jnp.zeros(carry_scr.shape, F32)

    h = _norm_mod(x_ref[...], g_ref[...], mod_ref[3:4, :], mod_ref[4:5, :])

    h_hi = h.astype(BF16)
    h_lo = (h - h_hi.astype(F32)).astype(BF16)
    hw = jnp.dot(h_hi, rw_ref[...], preferred_element_type=F32)
    logits = (hw[:, :LANES] + hw[:, LANES:] + jnp.dot(h_lo, rw_ref[:, :LANES], preferred_element_type=F32)
              + rb_ref[...])
    mx = jnp.max(logits, axis=-1, keepdims=True)
    ex = jnp.exp(logits - mx)
    probs = ex / jnp.sum(ex, axis=-1, keepdims=True)
    lane = lax.broadcasted_iota(jnp.int32, (tm, LANES), 1)
    valid = lane < N_EXPERTS
    probs = jnp.where(valid, probs, -1.0)
    m1 = jnp.max(probs, axis=-1, keepdims=True)
    i1 = jnp.min(jnp.where(probs == m1, lane, LANES), axis=-1, keepdims=True)
    rest = jnp.where(lane == i1, -1.0, probs)
    m2 = jnp.max(rest, axis=-1, keepdims=True)
    i2 = jnp.min(jnp.where(rest == m2, lane, LANES), axis=-1, keepdims=True)
    den = m1 + m2
    pw_ref[...] = jnp.where(lane == 0, m1 / den, jnp.where(lane == 1, m2 / den, 0.0))

    sel1 = lane == i1
    sel2 = lane == i2
    onehot = jnp.where(sel1, 1.0, 0.0) + jnp.where(sel2, 1.0, 0.0)
    r_i = lax.broadcasted_iota(jnp.int32, (tm, tm), 0)
    c_i = lax.broadcasted_iota(jnp.int32, (tm, tm), 1)
    tri = jnp.where(c_i < r_i, 1.0, 0.0).astype(BF16)
    before = jnp.dot(tri, onehot.astype(BF16), preferred_element_type=F32)
    cnt = jnp.sum(onehot, axis=0, keepdims=True)
    cnt_al = jnp.floor((cnt + (ROW_ALIGN - 1)) * (1.0 / ROW_ALIGN)) * ROW_ALIGN
    e_r = lax.broadcasted_iota(jnp.int32, (LANES, LANES), 0)
    e_c = lax.broadcasted_iota(jnp.int32, (LANES, LANES), 1)
    upper = jnp.where(e_r < e_c, 1.0, 0.0)
    loff = jnp.dot(jnp.broadcast_to(cnt_al, (8, LANES)), upper, precision=HIGHEST,
                   preferred_element_type=F32)[0:1, :]
    slot = loff + before
    slot1 = jnp.sum(jnp.where(sel1, slot, 0.0), axis=-1, keepdims=True).astype(jnp.int32)
    slot2 = jnp.sum(jnp.where(sel2, slot, 0.0), axis=-1, keepdims=True).astype(jnp.int32)
    ei = jnp.where(lane == 0, i1, jnp.where(lane == 1, i2, 0))
    ei_ref[...] = jnp.where(lane == 2, slot1, jnp.where(lane == 3, slot2, ei))

    r_idx = lax.broadcasted_iota(jnp.int32, (tm, srows), 1)
    place = jnp.where(r_idx == slot1, 1.0, jnp.where(r_idx == slot2, 1.0, 0.0)).astype(BF16)
    hs = lax.dot_general(place, h.astype(BF16), (((0,), (0,)), ((), ())), preferred_element_type=F32)
    half = hs.shape[1] // 2
    hs_ref[...] = _pack_bf16_pair(hs[:, :half], hs[:, half:])

    carry = carry_scr[0:1, :]
    mrow = lax.broadcasted_iota(jnp.int32, (8, LANES), 0)
    meta = jnp.where(mrow == 0, cnt_al, jnp.where(mrow == 1, carry, jnp.where(mrow == 2, loff, 0.0)))
    meta_ref[...] = meta.astype(jnp.int32)
    carry_scr[0:1, :] = carry + cnt_al
    tot_ref[...] = jnp.broadcast_to(carry + cnt_al, tot_ref.shape).astype(jnp.int32)


def _router(x, mod, g, rw_pad, rb_pad, seq):
    n, d = x.shape
    tm = TM_ROUTE
    tpb = seq // tm
    nt = n // tm
    return pl.pallas_call(
        functools.partial(_router_kernel, tm=tm, srows=SORT_ROWS),
        out_shape=(jax.ShapeDtypeStruct((nt * SORT_ROWS, d // 2), jnp.uint32),
                   jax.ShapeDtypeStruct((n, LANES), jnp.int32),
                   jax.ShapeDtypeStruct((n, LANES), F32),
                   jax.ShapeDtypeStruct((nt, 8, LANES), jnp.int32),
                   jax.ShapeDtypeStruct((8, LANES), jnp.int32)),
        grid=(nt,),
        in_specs=[
            pl.BlockSpec((tm, d), lambda i: (i, 0)),
            pl.BlockSpec((None, 6, d), lambda i: (i // tpb, 0, 0)),
            pl.BlockSpec((1, d), lambda i: (0, 0)),
            pl.BlockSpec((d, 2 * LANES), lambda i: (0, 0)),
            pl.BlockSpec((1, LANES), lambda i: (0, 0)),
        ],
        out_specs=(pl.BlockSpec((SORT_ROWS, d // 2), lambda i: (i, 0)),
                   pl.BlockSpec((tm, LANES), lambda i: (i, 0)),
                   pl.BlockSpec((tm, LANES), lambda i: (i, 0)),
                   pl.BlockSpec((None, 8, LANES), lambda i: (i, 0, 0)),
                   pl.BlockSpec((8, LANES), lambda i: (0, 0))),
        scratch_shapes=[pltpu.VMEM((8, LANES), F32)],
        compiler_params=_cparams("arbitrary"),
        name="router_top2",
    )(x, mod, g, rw_pad, rb_pad)


def _segment_copy(src_hbm, dst_hbm, src_row, dst_row, n_rows, sem):
    src_row = pl.multiple_of(src_row, ROW_ALIGN)
    dst_row = pl.multiple_of(dst_row, ROW_ALIGN)
    n_rows = pl.multiple_of(n_rows, ROW_ALIGN)
    return pltpu.make_async_copy(src_hbm.at[pl.ds(src_row, n_rows)], dst_hbm.at[pl.ds(dst_row, n_rows)], sem)


def _expert_kernel(te_ref, used_ref, toff_ref, ilo_ref, ihi_ref, cnt_ref, carry_ref, loff_ref,
                   hs_hbm, w1_hbm, w3_hbm, w2_hbm, y_ref, wb1, wb3, wb2, stage, sem, xbuf, xsem, rows_smem,
                   *, srows):
    t = pl.program_id(0)
    e = te_ref[t]
    tg = xbuf.shape[1]
    slot = t % 2
    first_of_expert = jnp.logical_or(t == 0, e != te_ref[jnp.maximum(t - 1, 0)])

    def fetch(tile, into):
        xbuf[into] = jnp.zeros(xbuf.shape[1:], xbuf.dtype)
        expert = te_ref[tile]
        first_row = toff_ref[tile]

        def piece(i, total):
            s = i * N_EXPERTS + expert
            run_start = carry_ref[s]
            lo = jnp.maximum(run_start, first_row)
            hi = jnp.minimum(run_start + cnt_ref[s], first_row + tg)

            @pl.when(hi > lo)
            def _():
                _segment_copy(hs_hbm, xbuf.at[into], i * srows + loff_ref[s] + (lo - run_start), lo - first_row,
                              hi - lo, xsem.at[into]).start()

            return total + jnp.maximum(hi - lo, 0)

        rows_smem[into] = lax.fori_loop(ilo_ref[tile], ihi_ref[tile], piece, 0)

    @pl.when(t == 0)
    def _():
        fetch(t, slot)

    @pl.when(t + 1 < pl.num_programs(0))
    def _():
        fetch(t + 1, 1 - slot)

    @pl.when(rows_smem[slot] > 0)
    def _():
        _segment_copy(hs_hbm, xbuf.at[slot], 0, 0, rows_smem[slot], xsem.at[slot]).wait()

    @pl.when(jnp.logical_and(used_ref[t] == 1, first_of_expert))
    def _():
        n_slots = stage.shape[0]
        windows = [(src, dst, r, c) for src, dst in ((w1_hbm, wb1), (w3_hbm, wb3), (w2_hbm, wb2))
                   for r in range(dst.shape[0] // W_CHUNK) for c in range(dst.shape[1] // W_CHUNK)]

        def staged_copy(k):
            src, _, r, c = windows[k]
            return pltpu.make_async_copy(src.at[e, pl.ds(r * W_CHUNK, W_CHUNK), pl.ds(c * W_CHUNK, W_CHUNK)],
                                         stage.at[k % n_slots], sem.at[k % n_slots])

        for k in range(n_slots - 1):
            staged_copy(k).start()
        for k, (_, dst, r, c) in enumerate(windows):
            staged_copy(k).wait()
            dst[r * W_CHUNK:(r + 1) * W_CHUNK, c * W_CHUNK:(c + 1) * W_CHUNK] = stage[k % n_slots].astype(BF16)
            if k + n_slots - 1 < len(windows):
                staged_copy(k + n_slots - 1).start()

    @pl.when(used_ref[t] == 1)
    def _():
        lo, hi = _unpack_bf16_pair(xbuf[slot])
        h = jnp.concatenate([lo.astype(BF16), hi.astype(BF16)], axis=1)
        a = jnp.dot(h, wb1[...], preferred_element_type=F32)
        b = jnp.dot(h, wb3[...], preferred_element_type=F32)
        hid = ((a * _sigmoid(a)) * b).astype(BF16)
        acc = jnp.dot(hid, wb2[...], preferred_element_type=F32)
        half = acc.shape[1] // 2
        y_ref[...] = _pack_bf16_pair(acc[:, :half], acc[:, half:])

    @pl.when(used_ref[t] == 0)
    def _():
        y_ref[...] = jnp.zeros(y_ref.shape, y_ref.dtype)


def _expert_ffn(tile_tables, seg_tables, hs, w1, w3, w2, n_tiles):
    half = hs.shape[1]
    d = 2 * half
    dff = w1.shape[2]
    tg = TG_MOE
    assert dff % W_CHUNK == 0 and d % W_CHUNK == 0
    hbm = pl.BlockSpec(memory_space=pl.ANY)
    tables = tuple(tile_tables) + tuple(seg_tables)
    return pl.pallas_call(
        functools.partial(_expert_kernel, srows=SORT_ROWS),
        out_shape=jax.ShapeDtypeStruct((n_tiles * tg, half), jnp.uint32),
        grid_spec=pltpu.PrefetchScalarGridSpec(
            num_scalar_prefetch=len(tables),
            grid=(n_tiles,),
            in_specs=[hbm, hbm, hbm, hbm],
            out_specs=pl.BlockSpec((tg, half), lambda t, *_: (t, 0)),
            scratch_shapes=[pltpu.VMEM((d, dff), BF16), pltpu.VMEM((d, dff), BF16), pltpu.VMEM((dff, d), BF16),
                            pltpu.VMEM((W_SLOTS, W_CHUNK, W_CHUNK), F32), pltpu.SemaphoreType.DMA((W_SLOTS,)),
                            pltpu.VMEM((2, tg, half), jnp.uint32), pltpu.SemaphoreType.DMA((2,)),
                            pltpu.SMEM((2,), jnp.int32)],
        ),
        compiler_params=_cparams("arbitrary"),
        name="expert_swiglu",
    )(*tables, hs, w1, w3, w2)


def _combine_kernel(src_ref, loff_ref, cnt_ref, rows_ref, x_ref, ei_ref, pw_ref, mod_ref, y_ref, o_ref,
                    ybuf, sem, *, tm, srows):
    i = pl.program_id(0)
    slot = i % 2

    def fetch(tile, into):
        ybuf[into] = jnp.zeros(ybuf.shape[1:], ybuf.dtype)
        for e in range(N_EXPERTS):
            s = tile * N_EXPERTS + e
            n_rows = cnt_ref[s]

            @pl.when(n_rows > 0)
            def _():
                _segment_copy(y_ref, ybuf.at[into], src_ref[s], loff_ref[s], n_rows, sem.at[into]).start()

    @pl.when(i == 0)
    def _():
        fetch(i, slot)

    @pl.when(i + 1 < pl.num_programs(0))
    def _():
        fetch(i + 1, 1 - slot)

    @pl.when(rows_ref[i] > 0)
    def _():
        _segment_copy(y_ref, ybuf.at[slot], 0, 0, rows_ref[i], sem.at[slot]).wait()

    lo, hi = _unpack_bf16_pair(ybuf[slot])
    ys = jnp.concatenate([lo.astype(BF16), hi.astype(BF16)], axis=1)
    r_idx = lax.broadcasted_iota(jnp.int32, (tm, srows), 1)
    mix = jnp.zeros(x_ref.shape, F32)
    for k in range(TOP_K):
        pick = jnp.where(r_idx == ei_ref[:, TOP_K + k:TOP_K + k + 1], 1.0, 0.0).astype(BF16)
        mix = mix + pw_ref[:, k:k + 1] * jnp.dot(pick, ys, preferred_element_type=F32)
    o_ref[...] = x_ref[...] + mod_ref[5:6, :] * mix


def _combine(seg_src, seg_loff, seg_cnt, tile_rows, x, ei, pw, mod, y, seq):
    n, d = x.shape
    tm = TM_ROUTE
    tpb = seq // tm
    tok = lambda width: pl.BlockSpec((tm, width), lambda i, *_: (i, 0))
    return pl.pallas_call(
        functools.partial(_combine_kernel, tm=tm, srows=SORT_ROWS),
        out_shape=jax.ShapeDtypeStruct((n, d), F32),
        grid_spec=pltpu.PrefetchScalarGridSpec(
            num_scalar_prefetch=4,
            grid=(n // tm,),
            in_specs=[tok(d), tok(LANES), tok(LANES),
                      pl.BlockSpec((None, 6, d), lambda i, *_: (i // tpb, 0, 0)),
                      pl.BlockSpec(memory_space=pl.ANY)],
            out_specs=tok(d),
            scratch_shapes=[pltpu.VMEM((2, SORT_ROWS, d // 2), jnp.uint32), pltpu.SemaphoreType.DMA((2,))],
        ),
        compiler_params=_cparams("arbitrary"),
        name="moe_combine",
    )(seg_src, seg_loff, seg_cnt, tile_rows, x, ei, pw, mod, y)


def _pack_w_in(w_in_all, layer):
    d = w_in_all.shape[1]
    col = lambda start, width: w_in_all[layer, :, start:start + width]
    w = MIX_W
    o_ckv = 3 * w + Q_LORA
    o_kr = o_ckv + KV_LORA
    o_su = o_kr + QK_ROPE
    o_rq = o_su + 2 * w
    o_gate = o_rq + 4 * w
    half = HEAD_DIM // 2
    perm = np.array([h * HEAD_DIM + part * half + i
                     for part in range(2) for h in range(N_GROUPS) for i in range(half)])
    z = lambda k: jnp.zeros((d, k), w_in_all.dtype)
    cols = [
        col(o_gate, N_BRANCH * d),
        col(0, 3 * w + Q_LORA),
        col(o_rq, w)[:, perm], col(o_rq + w, w)[:, perm],
        col(o_rq + 2 * w, 2 * w),
        col(o_su, 2 * w),
        col(o_ckv, KV_LORA),
        z(QK_NOPE), col(o_kr, QK_ROPE), z(LANES - QK_HEAD),
    ]
    return jnp.concatenate(cols, axis=1).astype(BF16)


def _swap_rope_halves(a):
    hr = QK_ROPE // 2
    return jnp.concatenate([a[..., :QK_NOPE], a[..., QK_NOPE + hr:QK_HEAD], a[..., QK_NOPE:QK_NOPE + hr],
                            a[..., QK_HEAD:]], axis=-1)


def _mla_params(cq_g, w_uq, ckv_g, w_ukv, qn_g, kn_g):
    pad = LANES - QK_HEAD
    wq = w_uq.reshape(Q_LORA, N_GROUPS, QK_HEAD)
    wq = jnp.pad(wq, ((0, 0), (0, 0), (0, pad)))
    wkv = w_ukv.reshape(KV_LORA, N_GROUPS, QK_NOPE + V_HEAD)
    wk = jnp.pad(wkv[:, :, :QK_NOPE], ((0, 0), (0, 0), (0, LANES - QK_NOPE)))
    wv = jnp.pad(wkv[:, :, QK_NOPE:], ((0, 0), (0, 0), (0, LANES - V_HEAD)))
    qg = jnp.pad(qn_g, (0, pad))[None, :]
    kg = jnp.pad(kn_g, (0, pad))[None, :]
    bound = (QK_HEAD ** 0.5 * LOG2_E) * jnp.max(jnp.abs(qn_g)) * jnp.max(jnp.abs(kn_g))
    static_shift = bound <= MAX_STATIC_SHIFT
    lane = jnp.arange(LANES)
    qaug = (lane == QK_HEAD).astype(F32)[None, :]
    kaug = qaug * jnp.where(static_shift, -bound, 0.0)
    vaug = jnp.tile((lane == V_HEAD).astype(F32), N_GROUPS)[None, :]
    params = {
        "cq_g": cq_g[None, :], "ckv_g": ckv_g[None, :],
        "wqa": wq.reshape(Q_LORA, -1).astype(BF16),
        "wqb": _swap_rope_halves(wq).reshape(Q_LORA, -1).astype(BF16),
        "wk": wk.reshape(KV_LORA, -1).astype(BF16),
        "wv": wv.reshape(KV_LORA, -1).astype(BF16),
        "qga": qg, "qgb": _swap_rope_halves(qg), "kga": kg, "kgb": _swap_rope_halves(kg),
        "qaug": qaug, "kaug": kaug, "vaug": vaug,
        "swap": (_swap_rope_halves(lane[None, :])[0][None, :] == lane[:, None]).astype(BF16),
    }
    return params, static_shift


def _mixer_consts():
    f32 = np.float32
    h = np.arange(N_GROUPS, dtype=f32)
    log_gamma = np.log1p(-(f32(2.0) ** (f32(-5.0) - h))).astype(f32)
    pos = np.arange(CHUNK, dtype=f32)
    rel = pos[:, None] - pos[None, :]
    dec = np.where(rel >= 0, np.exp(log_gamma[:, None, None] * np.maximum(rel, f32(0.0))), f32(0.0)).astype(f32)
    lane = np.arange(MIX_W)
    head_k = (lane % LANES) // (HEAD_DIM // 2)
    head_v = lane // HEAD_DIM
    lg_k = log_gamma[head_k]
    return {
        "dec": jnp.asarray(dec.reshape(N_GROUPS * CHUNK, CHUNK)),
        "kdec": jnp.asarray(np.exp(lg_k[None, :] * (CHUNK - 1.0 - pos)[:, None]).astype(f32)),
        "qdec": jnp.asarray(np.exp(lg_k[None, :] * (pos + 1.0)[:, None]).astype(f32)),
        "cdec": jnp.asarray(np.broadcast_to(np.exp(lg_k * f32(CHUNK)).astype(f32)[:, None], (MIX_W, MIX_W))),
        "bd": jnp.asarray((head_k[:, None] == head_v[None, :]).astype(f32)),
        "gmat": jnp.asarray((head_v[:, None] == head_v[None, :]).astype(f32) / HEAD_DIM).astype(BF16),
        "mk": jnp.asarray((head_k[None, :] == np.arange(N_GROUPS)[:, None]).astype(f32)),
        "mv": jnp.asarray((head_v[None, :] == np.arange(N_GROUPS)[:, None]).astype(f32)),
    }


def _mixer_params(consts, conv_w, gv_g, w_s, b_s, ret_g, w_branch, w_o):
    p = dict(consts)
    ws = jnp.tril(w_s)
    p.update({
        "conv_w": conv_w,
        "gv_g": gv_g.reshape(1, MIX_W),
        "ws_cat": jnp.transpose(ws, (1, 0, 2)).reshape(CHUNK, N_GROUPS * CHUNK).astype(BF16),
        "bs_mat": jnp.repeat(b_s.T, HEAD_DIM, axis=1),
        "ret_g": ret_g.reshape(1, MIX_W),
        "w_branch": w_branch.astype(BF16),
        "w_o": w_o.astype(BF16),
    })
    return p


def _moe_layout(meta, tot, n_tiles):
    totals = tot[0, :N_EXPERTS]
    padded = ((totals + TG_MOE - 1) // TG_MOE) * TG_MOE
    ends = jnp.cumsum(padded)
    starts = ends - padded
    seg_cnt = meta[:, 0, :N_EXPERTS]
    seg_carry = meta[:, 1, :N_EXPERTS]
    seg_loff = meta[:, 2, :N_EXPERTS]
    seg_grouped = starts[None, :] + seg_carry
    tile_start = jnp.arange(n_tiles, dtype=jnp.int32) * TG_MOE
    tile_e = jnp.sum((tile_start[:, None] >= ends[None, :]).astype(jnp.int32), axis=1)
    used = (tile_start < ends[-1]).astype(jnp.int32)
    last_e = jnp.sum((ends[-1] - 1 >= ends).astype(jnp.int32))
    tile_e = jnp.minimum(jnp.where(used == 1, tile_e, last_e), N_EXPERTS - 1)
    tile_off = tile_start - starts[tile_e]
    run_start = seg_carry[:, tile_e]
    run_end = run_start + seg_cnt[:, tile_e]
    tile_ilo = jnp.sum((run_end <= tile_off[None, :]).astype(jnp.int32), axis=0) * used
    tile_ihi = jnp.sum((run_start < tile_off[None, :] + TG_MOE).astype(jnp.int32), axis=0) * used
    flat = lambda a: a.reshape(-1).astype(jnp.int32)
    tile_tables = (flat(tile_e), flat(used), flat(tile_off * used), flat(tile_ilo), flat(tile_ihi))
    seg_tables = (flat(seg_cnt), flat(seg_carry), flat(seg_loff))
    return tile_tables, seg_tables, flat(seg_grouped)


def kernel(x, c, positions, norm1_g, norm2_g, ada_w, ada_b, w_in, conv_w, cq_g, w_uq, ckv_g, w_ukv, qn_g, kn_g, gv_g, w_s, b_s, ret_g, w_branch, w_o, ffn_w1, ffn_w3, ffn_w2, router_w, router_b, moe_w1, moe_w3, moe_w2):
    batch, seq, d = x.shape
    depth = ada_w.shape[0]
    n = batch * seq
    assert seq % max(TM_PROJ, TQ_ATT, TM_MIX, TM_FFN, TM_ROUTE) == 0
    assert d // 2 % LANES == 0

    c_t = jnp.pad(c, ((0, 8 - batch), (0, 0))).T
    ada = _ada(c_t, ada_w, ada_b, batch)[:, :batch].reshape(depth, batch, 6, d)
    cosr, sinr, cm, sm = _rope_tables(positions.astype(F32).reshape(n, 1))

    mixer_consts = _mixer_consts()
    xt = x.reshape(n, d)
    for l in range(depth):
        mod = ada[l]
        mla_p, static_shift = _mla_params(cq_g[l], w_uq[l], ckv_g[l], w_ukv[l], qn_g[l], kn_g[l])
        proj, q, k, v = _inproj(xt, mod, norm1_g[l][None, :], _pack_w_in(w_in, l), cm, sm, mla_p, seq)
        y_mla = lax.cond(static_shift,
                         functools.partial(_flash, batch=batch, seq=seq, online_max=False),
                         functools.partial(_flash, batch=batch, seq=seq, online_max=True), q, k, v)
        mp = _mixer_params(mixer_consts, conv_w[l], gv_g[l], w_s[l], b_s[l], ret_g[l], w_branch[l], w_o[l])
        xt = _mixers(proj, y_mla, xt, cosr, sinr, mod, mp, seq)
        g2n = norm2_g[l][None, :]
        if l % 2 == 0:
            i = l // 2
            xt = _dense_ffn(xt, mod, g2n, ffn_w1[i].astype(BF16), ffn_w3[i].astype(BF16),
                            ffn_w2[i].astype(BF16), seq)
        else:
            i = l // 2
            rw = jnp.pad(router_w[i], ((0, 0), (0, LANES - N_EXPERTS)))
            rw_hi = rw.astype(BF16)
            rw_pad = jnp.concatenate([rw_hi, (rw - rw_hi.astype(F32)).astype(BF16)], axis=1)
            rb_pad = jnp.pad(router_b[i], (0, LANES - N_EXPERTS), constant_values=-1e30)[None, :]
            hs, ei, pw, meta, tot = _router(xt, mod, g2n, rw_pad, rb_pad, seq)
            max_rows = n * TOP_K + N_EXPERTS * (n // TM_ROUTE) * (ROW_ALIGN - 1)
            n_tiles = -(-max_rows // TG_MOE) + N_EXPERTS
            tile_tables, seg_tables, seg_grouped = _moe_layout(meta, tot, n_tiles)
            seg_cnt, _, seg_loff = seg_tables
            tile_rows = jnp.sum(seg_cnt.reshape(-1, N_EXPERTS), axis=1)
            y = _expert_ffn(tile_tables, seg_tables, hs, moe_w1[i], moe_w3[i], moe_w2[i], n_tiles)
            xt = _combine(seg_grouped, seg_loff, seg_cnt, tile_rows, xt, ei, pw, mod, y, seq)
    return xt.reshape(batch, seq, d)
```

```python
import functools

import jax
import jax.numpy as jnp
import numpy as np
from jax import lax
from jax.experimental import pallas as pl
from jax.experimental.pallas import tpu as pltpu

F32 = jnp.float32
BF16 = jnp.bfloat16
HIGHEST = lax.Precision.HIGHEST

HEAD_DIM = 64
N_GROUPS = 4
MIX_W = N_GROUPS * HEAD_DIM
N_BRANCH = 4
CONV_W = 3
Q_LORA = 256
KV_LORA = 128
QK_NOPE = 64
QK_ROPE = 32
QK_HEAD = QK_NOPE + QK_ROPE
V_HEAD = 64
CHUNK = 128
N_EXPERTS = 8
TOP_K = 2
ROPE_THETA = 10000.0
EPS = 1e-6
LOG2_E = 1.4426950408889634
MAX_STATIC_SHIFT = 50.0

LANES = 128
VMEM_LIMIT_BYTES = 56 * 1024 * 1024

COL_GATES = 0
COL_A = 4096
COL_CQ = COL_A + 3 * MIX_W
COL_R = 5120
COL_SU = 6144
COL_CKV = 6656
COL_KRA = 6784
N_IN = 6912

TM_PROJ = 512
TN_PROJ = 768
TQ_ATT = 1024
TM_MIX = 512
TM_FFN = 512
TM_ROUTE = 512
ROW_ALIGN = 8
SORT_ROWS = TOP_K * TM_ROUTE + N_EXPERTS * ROW_ALIGN
TG_MOE = 512
W_CHUNK = 512
W_SLOTS = 8


def _cparams(*sem):
    return pltpu.CompilerParams(dimension_semantics=sem, vmem_limit_bytes=VMEM_LIMIT_BYTES)


def _sigmoid(x):
    return jnp.tanh(x * 0.5) * 0.5 + 0.5


def _group_mean(x, gmat_bf16):
    hi = x.astype(BF16)
    lo = (x - hi.astype(F32)).astype(BF16)
    return (jnp.dot(hi, gmat_bf16, preferred_element_type=F32)
            + jnp.dot(lo, gmat_bf16, preferred_element_type=F32))


def _pack_bf16_pair(lo, hi):
    lo_bits = lax.bitcast_convert_type(lo.astype(BF16).astype(F32), jnp.uint32)
    hi_bits = lax.bitcast_convert_type(hi.astype(BF16).astype(F32), jnp.uint32)
    return (lo_bits >> 16) | (hi_bits & jnp.uint32(0xFFFF0000))


def _unpack_bf16_pair(p):
    lo = lax.bitcast_convert_type(p << 16, F32)
    hi = lax.bitcast_convert_type(p & jnp.uint32(0xFFFF0000), F32)
    return lo, hi


def _norm_mod(x, g, shift, scale):
    y = x * lax.rsqrt(jnp.mean(x * x, axis=-1, keepdims=True) + EPS)
    return (y * g) * (1.0 + scale) + shift


def _ada_kernel(ct_ref, w_ref, b_ref, o_ref, *, batch):
    ct = ct_ref[...]
    cond = ct * _sigmoid(ct)
    w = w_ref[...]
    o_ref[...] = jnp.zeros(o_ref.shape, F32)
    for b in range(batch):
        o_ref[b:b + 1, :] = jnp.sum(w * cond[:, b:b + 1], axis=0, keepdims=True) + b_ref[...]


def _ada(c_t, ada_w, ada_b, batch):
    n_layer, d, d6 = ada_w.shape
    rows = c_t.shape[1]
    tn = 2048
    return pl.pallas_call(
        functools.partial(_ada_kernel, batch=batch),
        out_shape=jax.ShapeDtypeStruct((n_layer, rows, d6), F32),
        grid=(n_layer, d6 // tn),
        in_specs=[
            pl.BlockSpec((d, rows), lambda l, j: (0, 0)),
            pl.BlockSpec((None, d, tn), lambda l, j: (l, 0, j)),
            pl.BlockSpec((None, 1, tn), lambda l, j: (l, 0, j)),
        ],
        out_specs=pl.BlockSpec((None, rows, tn), lambda l, j: (l, 0, j)),
        compiler_params=_cparams("parallel", "parallel"),
        name="ada_mod",
    )(c_t, ada_w, ada_b.reshape(n_layer, 1, d6))


def _rope_kernel(pos_ref, inv_ref, cr_ref, sr_ref, cm_ref, sm_ref):
    half_r = HEAD_DIM // 2
    half_m = QK_ROPE // 2
    tm = pos_ref.shape[0]
    low = lax.broadcasted_iota(jnp.int32, (tm // 2, LANES), 1) < LANES // 2
    ang = jnp.where(low, pos_ref[0:tm // 2, :], pos_ref[tm // 2:tm, :]) * inv_ref[...]
    c = jnp.cos(ang)
    s = jnp.sin(ang)
    c = jnp.concatenate([c, pltpu.roll(c, LANES // 2, axis=1)], axis=0)
    s = jnp.concatenate([s, pltpu.roll(s, LANES // 2, axis=1)], axis=0)
    lane = lax.broadcasted_iota(jnp.int32, c.shape, 1)

    def tile_r(t):
        t = jnp.where(lane < half_r, t, 0.0)
        out = t
        for k in range(1, LANES // half_r):
            out = out + pltpu.roll(t, k * half_r, axis=1)
        return out

    cr_ref[...] = tile_r(c)
    sr_ref[...] = tile_r(s)
    first = jnp.logical_and(lane >= QK_NOPE, lane < QK_NOPE + half_m)
    second = jnp.logical_and(lane >= QK_NOPE + half_m, lane < QK_HEAD)
    c1, c2 = pltpu.roll(c, QK_NOPE - half_r, axis=1), pltpu.roll(c, QK_NOPE + half_m - half_r, axis=1)
    s1, s2 = pltpu.roll(s, QK_NOPE - half_r, axis=1), pltpu.roll(s, QK_NOPE + half_m - half_r, axis=1)
    cm_ref[...] = jnp.where(first, c1, jnp.where(second, c2, 1.0))
    sm_ref[...] = jnp.where(first, -s1, jnp.where(second, s2, 0.0))


def _rope_tables(pos_f):
    n = pos_f.shape[0]
    tm = 1024
    half_r = HEAD_DIM // 2
    half_m = QK_ROPE // 2
    inv_r = ROPE_THETA ** (-jnp.arange(half_r, dtype=F32) / half_r)
    inv_m = ROPE_THETA ** (-jnp.arange(half_m, dtype=F32) / half_m)
    inv = jnp.concatenate([inv_r, inv_m, jnp.zeros((LANES // 2 - half_r - half_m,), F32)])
    inv = jnp.tile(inv, 2)[None, :]
    tab = pl.BlockSpec((tm, LANES), lambda i: (i, 0))
    shape = jax.ShapeDtypeStruct((n, LANES), F32)
    return pl.pallas_call(
        _rope_kernel,
        out_shape=(shape, shape, shape, shape),
        grid=(n // tm,),
        in_specs=[pl.BlockSpec((tm, 1), lambda i: (i, 0)), pl.BlockSpec((1, LANES), lambda i: (0, 0))],
        out_specs=(tab, tab, tab, tab),
        compiler_params=_cparams("parallel"),
        name="rope_tables",
    )(pos_f, inv)


def _resident(shape):
    return pl.BlockSpec(shape, lambda *_: (0,) * len(shape), pipeline_mode=pl.Buffered(1))


def _mla_prep(cq_b, ckv_b, kra_b, cm, sm, cqg_ref, wqa_ref, wqb_ref, ckvg_ref, wk_ref, wv_ref, qga_ref, qgb_ref,
              kga_ref, kgb_ref, qaug_ref, kaug_ref, vaug_ref, swap_ref, q_ref, k_ref, v_ref):
    cq = cq_b.astype(F32)
    cqn = (cq * lax.rsqrt(jnp.mean(cq * cq, axis=-1, keepdims=True) + EPS) * cqg_ref[...]).astype(BF16)
    qa = jnp.dot(cqn, wqa_ref[...], preferred_element_type=F32)
    qb = jnp.dot(cqn, wqb_ref[...], preferred_element_type=F32)
    ckv = ckv_b.astype(F32)
    ckvn = (ckv * lax.rsqrt(jnp.mean(ckv * ckv, axis=-1, keepdims=True) + EPS) * ckvg_ref[...]).astype(BF16)
    ka = jnp.dot(ckvn, wk_ref[...], preferred_element_type=F32)
    v_ref[...] = (jnp.dot(ckvn, wv_ref[...], preferred_element_type=F32) + vaug_ref[...]).astype(BF16)
    kra = kra_b.astype(F32)
    krb = jnp.dot(kra_b, swap_ref[...], preferred_element_type=F32)
    scale = QK_HEAD ** -0.5 * LOG2_E
    q_cos, q_sin = cm * (qga_ref[...] * scale), sm * (qgb_ref[...] * scale)
    k_cos, k_sin = cm * kga_ref[...], sm * kgb_ref[...]
    for h in range(N_GROUPS):
        sl = slice(h * LANES, (h + 1) * LANES)
        qah, qbh = qa[:, sl], qb[:, sl]
        r = lax.rsqrt(jnp.sum(qah * qah, axis=-1, keepdims=True) * (1.0 / QK_HEAD) + EPS)
        q_ref[:, sl] = ((qah * q_cos + qbh * q_sin) * r + qaug_ref[...]).astype(BF16)
        kah = ka[:, sl] + kra
        kbh = ka[:, sl] + krb
        r = lax.rsqrt(jnp.sum(kah * kah, axis=-1, keepdims=True) * (1.0 / QK_HEAD) + EPS)
        k_ref[:, sl] = ((kah * k_cos + kbh * k_sin) * r + kaug_ref[...]).astype(BF16)


def _inproj_kernel(x_ref, mod_ref, g_ref, w_ref, cm_ref, sm_ref, *rest):
    mla_refs, (o_ref, q_ref, k_ref, v_ref) = rest[:-4], rest[-4:]
    h = _norm_mod(x_ref[...], g_ref[...], mod_ref[0:1, :], mod_ref[1:2, :]).astype(BF16)

    def chunk(c):
        cols = slice(c * TN_PROJ, (c + 1) * TN_PROJ)
        out = jnp.dot(h, w_ref[:, cols], preferred_element_type=F32).astype(BF16)
        o_ref[:, cols] = out
        return out

    c_q, c_kv = COL_CQ // TN_PROJ, COL_CKV // TN_PROJ
    lat_q = chunk(c_q)
    lat_kv = chunk(c_kv)
    q0, kv0, kr0 = COL_CQ - c_q * TN_PROJ, COL_CKV - c_kv * TN_PROJ, COL_KRA - c_kv * TN_PROJ
    _mla_prep(lat_q[:, q0:q0 + Q_LORA], lat_kv[:, kv0:kv0 + KV_LORA], lat_kv[:, kr0:kr0 + LANES],
              cm_ref[...], sm_ref[...], *mla_refs, q_ref, k_ref, v_ref)
    for c in range(N_IN // TN_PROJ):
        if c not in (c_q, c_kv):
            chunk(c)


def _inproj(x, mod, g, w, cm, sm, p, seq):
    n, d = x.shape
    tm = TM_PROJ
    tpb = seq // tm
    hw = N_GROUPS * LANES
    assert COL_KRA // TN_PROJ == COL_CKV // TN_PROJ and (COL_CQ + Q_LORA - 1) // TN_PROJ == COL_CQ // TN_PROJ

    def full(a):
        return pl.BlockSpec(a.shape, lambda i: (0,) * a.ndim)

    weights = [p["cq_g"], p["wqa"], p["wqb"], p["ckv_g"], p["wk"], p["wv"],
               p["qga"], p["qgb"], p["kga"], p["kgb"], p["qaug"], p["kaug"], p["vaug"], p["swap"]]
    table = pl.BlockSpec((tm, LANES), lambda i: (i, 0))
    head_tile = pl.BlockSpec((tm, hw), lambda i: (i, 0))
    heads = jax.ShapeDtypeStruct((n, hw), BF16)
    return pl.pallas_call(
        _inproj_kernel,
        out_shape=(jax.ShapeDtypeStruct((n, N_IN), BF16), heads, heads, heads),
        grid=(n // tm,),
        in_specs=[
            pl.BlockSpec((tm, d), lambda i: (i, 0)),
            pl.BlockSpec((None, 6, d), lambda i: (i // tpb, 0, 0)),
            pl.BlockSpec((1, d), lambda i: (0, 0)),
            _resident((d, N_IN)), table, table,
        ] + [full(a) for a in weights],
        out_specs=(pl.BlockSpec((tm, N_IN), lambda i: (i, 0)), head_tile, head_tile, head_tile),
        compiler_params=_cparams("parallel"),
        name="in_proj",
    )(x, mod, g, w, cm, sm, *weights)


def _flash_kernel(qi_ref, kj_ref, q_ref, k_ref, v_ref, o_ref, acc_scr, *rest, tq, online_max):
    i = qi_ref[pl.program_id(1)]
    j = kj_ref[pl.program_id(1)]

    @pl.when(j == 0)
    def _():
        acc_scr[...] = jnp.zeros(acc_scr.shape, F32)
        if online_max:
            rest[0][...] = jnp.full(rest[0].shape, -jnp.inf, F32)

    def block(q0, nq, nk, masked):
        rows = slice(q0, q0 + nq)
        if masked:
            row = lax.broadcasted_iota(jnp.int32, (nq, nk), 0) + q0
            col = lax.broadcasted_iota(jnp.int32, (nq, nk), 1)
            keep = col <= row
        for h in range(N_GROUPS):
            sl = slice(h * LANES, (h + 1) * LANES)
            s = lax.dot_general(q_ref[rows, sl], k_ref[0:nk, sl], (((1,), (1,)), ((), ())),
                                preferred_element_type=F32)
            if masked:
                s = jnp.where(keep, s, -jnp.inf)
            if online_max:
                m_scr = rest[0]
                m_prev = m_scr[h, rows]
                m_new = jnp.maximum(m_prev, jnp.max(s, axis=-1, keepdims=True))
                p = jnp.exp2(s - m_new).astype(BF16)
                acc_scr[h, rows] = jnp.exp2(m_prev - m_new) * acc_scr[h, rows] + jnp.dot(
                    p, v_ref[0:nk, sl], preferred_element_type=F32)
                m_scr[h, rows] = m_new
            else:
                acc_scr[h, rows] += jnp.dot(jnp.exp2(s).astype(BF16), v_ref[0:nk, sl],
                                            preferred_element_type=F32)

    @pl.when(j < i)
    def _():
        block(0, tq, tq, False)

    @pl.when(j == i)
    def _():
        block(0, tq // 2, tq // 2, True)
        block(tq // 2, tq // 2, tq, True)
        lane = lax.broadcasted_iota(jnp.int32, (tq, LANES), 1)
        for pr in range(N_GROUPS // 2):
            lo = acc_scr[2 * pr]
            hi = acc_scr[2 * pr + 1]
            lo = lo / lo[:, V_HEAD:V_HEAD + 1]
            hi = hi / hi[:, V_HEAD:V_HEAD + 1]
            both = jnp.where(lane < V_HEAD, lo, pltpu.roll(hi, V_HEAD, axis=1))
            o_ref[:, pr * LANES:(pr + 1) * LANES] = both.astype(BF16)


def _flash(q, k, v, batch, seq, online_max):
    n = q.shape[0]
    tq = TQ_ATT
    nq = seq // tq
    hw = N_GROUPS * LANES
    scratch = [pltpu.VMEM((N_GROUPS, tq, LANES), F32)]
    if online_max:
        scratch.append(pltpu.VMEM((N_GROUPS, tq, 1), F32))
    pairs = [(i, j) for i in range(nq) for j in range(i + 1)]
    qi = jnp.asarray([p[0] for p in pairs], jnp.int32)
    kj = jnp.asarray([p[1] for p in pairs], jnp.int32)
    q_tile = lambda b, s, qi, kj: (b * nq + qi[s], 0)
    k_tile = lambda b, s, qi, kj: (b * nq + kj[s], 0)
    return pl.pallas_call(
        functools.partial(_flash_kernel, tq=tq, online_max=online_max),
        out_shape=jax.ShapeDtypeStruct((n, MIX_W), BF16),
        grid_spec=pltpu.PrefetchScalarGridSpec(
            num_scalar_prefetch=2,
            grid=(batch, len(pairs)),
            in_specs=[pl.BlockSpec((tq, hw), q_tile), pl.BlockSpec((tq, hw), k_tile),
                      pl.BlockSpec((tq, hw), k_tile)],
            out_specs=pl.BlockSpec((tq, MIX_W), q_tile),
            scratch_shapes=scratch,
        ),
        compiler_params=_cparams("parallel", "arbitrary"),
        name="mla_flash_online" if online_max else "mla_flash",
    )(qi, kj, q, k, v)


def _gelu_tanh(x):
    return jax.nn.gelu(x, approximate=True)


def _mix_kernel(gates_ref, a_ref, r_ref, su_ref, ymla_ref, x_ref, cos_ref, sin_ref, mod_ref,
                convw_ref, gvg_ref, wscat_ref, bsmat_ref, retg_ref, dec_ref, kdec_ref, qdec_ref,
                cdec_ref, bd_ref, gmat_ref, mk_ref, mv_ref, wb_ref, wo_ref,
                o_ref, carry_scr, state_scr, ysg_scr, yret_scr, *, tm, tpb):
    i = pl.program_id(0)

    @pl.when(i % tpb == 0)
    def _():
        carry_scr[...] = jnp.zeros(carry_scr.shape, F32)
        state_scr[...] = jnp.zeros(state_scr.shape, F32)

    w = MIX_W
    a_b = a_ref[:, 0:w].astype(F32)
    u = a_ref[:, w:2 * w].astype(F32) * a_ref[:, 2 * w:3 * w].astype(F32)
    rowi = lax.broadcasted_iota(jnp.int32, (tm, w), 0)
    prev1 = carry_scr[0:1, :]
    prev2 = carry_scr[1:2, :]
    u1 = jnp.where(rowi == 0, prev1, pltpu.roll(u, 1, axis=0))
    u2 = jnp.where(rowi == 0, prev2, jnp.where(rowi == 1, prev1, pltpu.roll(u, 2, axis=0)))
    carry_scr[0:1, :] = u[tm - 1:tm, :]
    carry_scr[1:2, :] = u[tm - 2:tm - 1, :]
    y_conv = a_b * (convw_ref[0:1, :] * u2 + convw_ref[1:2, :] * u1 + convw_ref[2:3, :] * u)

    gmat = gmat_ref[...]
    s_u = _gelu_tanh(su_ref[:, 0:w].astype(F32))
    s_v = _gelu_tanh(su_ref[:, w:2 * w].astype(F32))
    ms = _group_mean(s_v * s_v, gmat)
    vn = (s_v * lax.rsqrt(ms + EPS) * gvg_ref[...]).astype(BF16)

    cosr = cos_ref[...]
    sinr = sin_ref[...]

    def rot(t):
        t1, t2 = t[:, 0:LANES], t[:, LANES:2 * LANES]
        return jnp.concatenate([t1 * cosr - t2 * sinr, t2 * cosr + t1 * sinr], axis=-1)

    rq = rot(r_ref[:, 0:w].astype(F32))
    rk = rot(r_ref[:, w:2 * w].astype(F32)) * (HEAD_DIM ** -0.5)

    for c in range(tm // CHUNK):
        rows = slice(c * CHUNK, (c + 1) * CHUNK)
        vc = vn[rows, :]
        vbd = jnp.concatenate([vc * mv_ref[g:g + 1, :].astype(BF16) for g in range(N_GROUPS)], axis=0)
        mixed = jnp.dot(wscat_ref[...], vbd, preferred_element_type=F32) + bsmat_ref[...]
        ysg_scr[rows, :] = s_u[rows, :] * mixed

        qc = rq[rows, :]
        kc = rk[rows, :]
        kcb = kc.astype(BF16)
        vcb = r_ref[rows, 2 * w:3 * w]
        qstack = jnp.concatenate([(qc * mk_ref[h:h + 1, :]).astype(BF16) for h in range(N_GROUPS)], axis=0)
        sc = lax.dot_general(qstack, kcb, (((1,), (1,)), ((), ())), preferred_element_type=F32)
        sc = (sc * dec_ref[...]).astype(BF16)
        scat = jnp.concatenate([sc[h * CHUNK:(h + 1) * CHUNK, :] for h in range(N_GROUPS)], axis=1)
        vstack = jnp.concatenate([vcb * mv_ref[h:h + 1, :].astype(BF16) for h in range(N_GROUPS)], axis=0)
        o_c = jnp.dot(scat, vstack, preferred_element_type=F32)
        state = state_scr[...]
        o_c = o_c + jnp.dot((qc * qdec_ref[...]).astype(BF16), state.astype(BF16),
                            preferred_element_type=F32)
        kd_t = jnp.transpose(kc * kdec_ref[...]).astype(BF16)
        kv = jnp.dot(kd_t, vcb, preferred_element_type=F32)
        state_scr[...] = state * cdec_ref[...] + kv * bd_ref[...]
        yret_scr[rows, :] = o_c

    o_all = yret_scr[...]
    xc = o_all - _group_mean(o_all, gmat)
    var = _group_mean(xc * xc, gmat)
    r_g = r_ref[:, 3 * w:4 * w].astype(F32)
    y_ret = (r_g * _sigmoid(r_g)) * (xc * lax.rsqrt(var + EPS) * retg_ref[...])

    d = x_ref.shape[1]
    ys = (y_conv, ymla_ref[...], ysg_scr[...], y_ret)
    merged = None
    for n in range(N_BRANCH):
        gate = _sigmoid(gates_ref[:, n * d:(n + 1) * d])
        term = gate * jnp.dot(ys[n].astype(BF16), wb_ref[n], preferred_element_type=F32).astype(BF16)
        merged = term if merged is None else merged + term
    out = jnp.dot(merged, wo_ref[...], preferred_element_type=F32)
    o_ref[...] = x_ref[...] + mod_ref[2:3, :] * out


def _mixers(proj, ymla, x, cosr, sinr, mod, p, seq):
    n, d = x.shape
    tm = TM_MIX
    tpb = seq // tm

    def col(width, offset):
        return pl.BlockSpec((tm, width), lambda i: (i, offset // width))

    def full(a):
        return pl.BlockSpec(a.shape, lambda i: (0,) * a.ndim)

    consts = [p["conv_w"], p["gv_g"], p["ws_cat"], p["bs_mat"], p["ret_g"], p["dec"], p["kdec"],
              p["qdec"], p["cdec"], p["bd"], p["gmat"], p["mk"], p["mv"], p["w_branch"], p["w_o"]]
    return pl.pallas_call(
        functools.partial(_mix_kernel, tm=tm, tpb=tpb),
        out_shape=jax.ShapeDtypeStruct((n, d), F32),
        grid=(n // tm,),
        in_specs=[col(N_BRANCH * d, COL_GATES), col(4 * MIX_W, COL_A), col(4 * MIX_W, COL_R),
                  col(2 * MIX_W, COL_SU),
                  pl.BlockSpec((tm, MIX_W), lambda i: (i, 0)),
                  pl.BlockSpec((tm, d), lambda i: (i, 0)),
                  pl.BlockSpec((tm, LANES), lambda i: (i, 0)),
                  pl.BlockSpec((tm, LANES), lambda i: (i, 0)),
                  pl.BlockSpec((None, 6, d), lambda i: (i // tpb, 0, 0))]
                 + [full(c) for c in consts],
        out_specs=pl.BlockSpec((tm, d), lambda i: (i, 0)),
        scratch_shapes=[pltpu.VMEM((8, MIX_W), F32), pltpu.VMEM((MIX_W, MIX_W), F32),
                        pltpu.VMEM((tm, MIX_W), F32), pltpu.VMEM((tm, MIX_W), F32)],
        compiler_params=_cparams("arbitrary"),
        name="mixers_merge",
    )(proj, proj, proj, proj, ymla, x, cosr, sinr, mod, *consts)


def _ffn_kernel(x_ref, mod_ref, g_ref, w1_ref, w3_ref, w2_ref, o_ref):
    x = x_ref[...]
    h = _norm_mod(x, g_ref[...], mod_ref[3:4, :], mod_ref[4:5, :]).astype(BF16)
    a = jnp.dot(h, w1_ref[...], preferred_element_type=F32)
    b = jnp.dot(h, w3_ref[...], preferred_element_type=F32)
    hid = ((a * _sigmoid(a)) * b).astype(BF16)
    o_ref[...] = x + mod_ref[5:6, :] * jnp.dot(hid, w2_ref[...], preferred_element_type=F32)


def _dense_ffn(x, mod, g, w1, w3, w2, seq):
    n, d = x.shape
    dff = w1.shape[1]
    tm = TM_FFN
    tpb = seq // tm
    return pl.pallas_call(
        _ffn_kernel,
        out_shape=jax.ShapeDtypeStruct((n, d), F32),
        grid=(n // tm,),
        in_specs=[
            pl.BlockSpec((tm, d), lambda i: (i, 0)),
            pl.BlockSpec((None, 6, d), lambda i: (i // tpb, 0, 0)),
            pl.BlockSpec((1, d), lambda i: (0, 0)),
            _resident((d, dff)), _resident((d, dff)), _resident((dff, d)),
        ],
        out_specs=pl.BlockSpec((tm, d), lambda i: (i, 0)),
        compiler_params=_cparams("parallel"),
        name="dense_swiglu",
    )(x, mod, g, w1, w3, w2)


def _router_kernel(x_ref, mod_ref, g_ref, rw_ref, rb_ref, hs_ref, ei_ref, pw_ref, meta_ref, tot_ref,
                   carry_scr, *, tm, srows):
    i = pl.program_id(0)

    @pl.when(i == 0)
    def _():
        carry_scr[...] = jnp.zeros(carry_scr.shape, F32)

    h = _norm_mod(x_ref[...], g_ref[...], mod_ref[3:4, :], mod_ref[4:5, :])

    h_hi = h.astype(BF16)
    h_lo = (h - h_hi.astype(F32)).astype(BF16)
    hw = jnp.dot(h_hi, rw_ref[...], preferred_element_type=F32)
    logits = (hw[:, :LANES] + hw[:, LANES:] + jnp.dot(h_lo, rw_ref[:, :LANES], preferred_element_type=F32)
              + rb_ref[...])
    mx = jnp.max(logits, axis=-1, keepdims=True)
    ex = jnp.exp(logits - mx)
    probs = ex / jnp.sum(ex, axis=-1, keepdims=True)
    lane = lax.broadcasted_iota(jnp.int32, (tm, LANES), 1)
    valid = lane < N_EXPERTS
    probs = jnp.where(valid, probs, -1.0)
    lane_f = lane.astype(F32)
    m1 = jnp.max(probs, axis=-1, keepdims=True)
    i1 = jnp.min(jnp.where(probs == m1, lane_f, float(LANES)), axis=-1, keepdims=True)
    rest = jnp.where(lane_f == i1, -1.0, probs)
    m2 = jnp.max(rest, axis=-1, keepdims=True)
    i2 = jnp.min(jnp.where(rest == m2, lane_f, float(LANES)), axis=-1, keepdims=True)
    den = m1 + m2
    pw_ref[...] = jnp.where(lane == 0, m1 / den, jnp.where(lane == 1, m2 / den, 0.0))

    sel1 = lane_f == i1
    sel2 = lane_f == i2
    onehot = jnp.where(sel1, 1.0, 0.0) + jnp.where(sel2, 1.0, 0.0)
    r_i = lax.broadcasted_iota(jnp.int32, (tm, tm), 0)
    c_i = lax.broadcasted_iota(jnp.int32, (tm, tm), 1)
    tri = jnp.where(c_i < r_i, 1.0, 0.0).astype(BF16)
    before = jnp.dot(tri, onehot.astype(BF16), preferred_element_type=F32)
    cnt = jnp.sum(onehot, axis=0, keepdims=True)
    cnt_al = jnp.floor((cnt + (ROW_ALIGN - 1)) * (1.0 / ROW_ALIGN)) * ROW_ALIGN
    e_r = lax.broadcasted_iota(jnp.int32, (LANES, LANES), 0)
    e_c = lax.broadcasted_iota(jnp.int32, (LANES, LANES), 1)
    upper = jnp.where(e_r < e_c, 1.0, 0.0)
    loff = jnp.dot(jnp.broadcast_to(cnt_al, (8, LANES)), upper, precision=HIGHEST,
                   preferred_element_type=F32)[0:1, :]
    slot = loff + before
    slot1 = jnp.sum(jnp.where(sel1, slot, 0.0), axis=-1, keepdims=True).astype(jnp.int32)
    slot2 = jnp.sum(jnp.where(sel2, slot, 0.0), axis=-1, keepdims=True).astype(jnp.int32)
    ei = jnp.where(lane == 0, i1, jnp.where(lane == 1, i2, 0.0)).astype(jnp.int32)
    ei_ref[...] = jnp.where(lane == 2, slot1, jnp.where(lane == 3, slot2, ei))

    r_idx = lax.broadcasted_iota(jnp.int32, (tm, srows), 1)
    place = jnp.where(r_idx == slot1, 1.0, jnp.where(r_idx == slot2, 1.0, 0.0)).astype(BF16)
    hs = lax.dot_general(place, h.astype(BF16), (((0,), (0,)), ((), ())), preferred_element_type=F32)
    half = hs.shape[1] // 2
    hs_ref[...] = _pack_bf16_pair(hs[:, :half], hs[:, half:])

    carry = carry_scr[0:1, :]
    mrow = lax.broadcasted_iota(jnp.int32, (8, LANES), 0)
    meta = jnp.where(mrow == 0, cnt_al, jnp.where(mrow == 1, carry, jnp.where(mrow == 2, loff, 0.0)))
    meta_ref[...] = meta.astype(jnp.int32)
    carry_scr[0:1, :] = carry + cnt_al
    tot_ref[...] = jnp.broadcast_to(carry + cnt_al, tot_ref.shape).astype(jnp.int32)


def _router(x, mod, g, rw_pad, rb_pad, seq):
    n, d = x.shape
    tm = TM_ROUTE
    tpb = seq // tm
    nt = n // tm
    return pl.pallas_call(
        functools.partial(_router_kernel, tm=tm, srows=SORT_ROWS),
        out_shape=(jax.ShapeDtypeStruct((nt * SORT_ROWS, d // 2), jnp.uint32),
                   jax.ShapeDtypeStruct((n, LANES), jnp.int32),
                   jax.ShapeDtypeStruct((n, LANES), F32),
                   jax.ShapeDtypeStruct((nt, 8, LANES), jnp.int32),
                   jax.ShapeDtypeStruct((8, LANES), jnp.int32)),
        grid=(nt,),
        in_specs=[
            pl.BlockSpec((tm, d), lambda i: (i, 0)),
            pl.BlockSpec((None, 6, d), lambda i: (i // tpb, 0, 0)),
            pl.BlockSpec((1, d), lambda i: (0, 0)),
            pl.BlockSpec((d, 2 * LANES), lambda i: (0, 0)),
            pl.BlockSpec((1, LANES), lambda i: (0, 0)),
        ],
        out_specs=(pl.BlockSpec((SORT_ROWS, d // 2), lambda i: (i, 0)),
                   pl.BlockSpec((tm, LANES), lambda i: (i, 0)),
                   pl.BlockSpec((tm, LANES), lambda i: (i, 0)),
                   pl.BlockSpec((None, 8, LANES), lambda i: (i, 0, 0)),
                   pl.BlockSpec((8, LANES), lambda i: (0, 0))),
        scratch_shapes=[pltpu.VMEM((8, LANES), F32)],
        compiler_params=_cparams("arbitrary"),
        name="router_top2",
    )(x, mod, g, rw_pad, rb_pad)


def _segment_copy(src_hbm, dst_hbm, src_row, dst_row, n_rows, sem):
    src_row = pl.multiple_of(src_row, ROW_ALIGN)
    dst_row = pl.multiple_of(dst_row, ROW_ALIGN)
    n_rows = pl.multiple_of(n_rows, ROW_ALIGN)
    return pltpu.make_async_copy(src_hbm.at[pl.ds(src_row, n_rows)], dst_hbm.at[pl.ds(dst_row, n_rows)], sem)


def _expert_kernel(te_ref, used_ref, toff_ref, ilo_ref, ihi_ref, cnt_ref, carry_ref, loff_ref,
                   hs_hbm, w1_hbm, w3_hbm, w2_hbm, y_ref, wb1, wb3, wb2, stage, sem, xbuf, xsem, rows_smem,
                   *, srows):
    t = pl.program_id(0)
    e = te_ref[t]
    tg = xbuf.shape[1]
    slot = t % 2
    first_of_expert = jnp.logical_or(t == 0, e != te_ref[jnp.maximum(t - 1, 0)])

    def fetch(tile, into):
        xbuf[into] = jnp.zeros(xbuf.shape[1:], xbuf.dtype)
        expert = te_ref[tile]
        first_row = toff_ref[tile]

        def piece(i, total):
            s = i * N_EXPERTS + expert
            run_start = carry_ref[s]
            lo = jnp.maximum(run_start, first_row)
            hi = jnp.minimum(run_start + cnt_ref[s], first_row + tg)

            @pl.when(hi > lo)
            def _():
                _segment_copy(hs_hbm, xbuf.at[into], i * srows + loff_ref[s] + (lo - run_start), lo - first_row,
                              hi - lo, xsem.at[into]).start()

            return total + jnp.maximum(hi - lo, 0)

        rows_smem[into] = lax.fori_loop(ilo_ref[tile], ihi_ref[tile], piece, 0)

    @pl.when(t == 0)
    def _():
        fetch(t, slot)

    @pl.when(t + 1 < pl.num_programs(0))
    def _():
        fetch(t + 1, 1 - slot)

    @pl.when(rows_smem[slot] > 0)
    def _():
        _segment_copy(hs_hbm, xbuf.at[slot], 0, 0, rows_smem[slot], xsem.at[slot]).wait()

    def rows_bf16():
        lo, hi = _unpack_bf16_pair(xbuf[slot])
        return jnp.concatenate([lo.astype(BF16), hi.astype(BF16)], axis=1)

    def swiglu(h, cols):
        a = jnp.dot(h, wb1[:, cols], preferred_element_type=F32)
        b = jnp.dot(h, wb3[:, cols], preferred_element_type=F32)
        hid = ((a * _sigmoid(a)) * b).astype(BF16)
        return jnp.dot(hid, wb2[cols, :], preferred_element_type=F32)

    def emit(acc):
        half = acc.shape[1] // 2
        y_ref[...] = _pack_bf16_pair(acc[:, :half], acc[:, half:])

    @pl.when(jnp.logical_and(used_ref[t] == 1, first_of_expert))
    def _():
        n_slots = stage.shape[0]
        dff, d = wb1.shape[1], wb1.shape[0]
        windows = [(src, dst, r, c)
                   for ch in range(dff // W_CHUNK)
                   for src, dst, r_list, c_list in ((w1_hbm, wb1, range(d // W_CHUNK), [ch]),
                                                    (w3_hbm, wb3, range(d // W_CHUNK), [ch]),
                                                    (w2_hbm, wb2, [ch], range(d // W_CHUNK)))
                   for r in r_list for c in c_list]
        per_chunk = len(windows) // (dff // W_CHUNK)

        def staged_copy(k):
            src, _, r, c = windows[k]
            return pltpu.make_async_copy(src.at[e, pl.ds(r * W_CHUNK, W_CHUNK), pl.ds(c * W_CHUNK, W_CHUNK)],
                                         stage.at[k % n_slots], sem.at[k % n_slots])

        for k in range(n_slots - 1):
            staged_copy(k).start()
        h = rows_bf16()
        acc = None
        for k, (_, dst, r, c) in enumerate(windows):
            staged_copy(k).wait()
            dst[r * W_CHUNK:(r + 1) * W_CHUNK, c * W_CHUNK:(c + 1) * W_CHUNK] = stage[k % n_slots].astype(BF16)
            if k + n_slots - 1 < len(windows):
                staged_copy(k + n_slots - 1).start()
            if (k + 1) % per_chunk == 0:
                ch = k // per_chunk
                part = swiglu(h, slice(ch * W_CHUNK, (ch + 1) * W_CHUNK))
                acc = part if acc is None else acc + part
        emit(acc)

    @pl.when(jnp.logical_and(used_ref[t] == 1, jnp.logical_not(first_of_expert)))
    def _():
        emit(swiglu(rows_bf16(), slice(None)))

    @pl.when(used_ref[t] == 0)
    def _():
        y_ref[...] = jnp.zeros(y_ref.shape, y_ref.dtype)


def _expert_ffn(tile_tables, seg_tables, hs, w1, w3, w2, n_tiles):
    half = hs.shape[1]
    d = 2 * half
    dff = w1.shape[2]
    tg = TG_MOE
    assert dff % W_CHUNK == 0 and d % W_CHUNK == 0
    hbm = pl.BlockSpec(memory_space=pl.ANY)
    tables = tuple(tile_tables) + tuple(seg_tables)
    return pl.pallas_call(
        functools.partial(_expert_kernel, srows=SORT_ROWS),
        out_shape=jax.ShapeDtypeStruct((n_tiles * tg, half), jnp.uint32),
        grid_spec=pltpu.PrefetchScalarGridSpec(
            num_scalar_prefetch=len(tables),
            grid=(n_tiles,),
            in_specs=[hbm, hbm, hbm, hbm],
            out_specs=pl.BlockSpec((tg, half), lambda t, *_: (t, 0)),
            scratch_shapes=[pltpu.VMEM((d, dff), BF16), pltpu.VMEM((d, dff), BF16), pltpu.VMEM((dff, d), BF16),
                            pltpu.VMEM((W_SLOTS, W_CHUNK, W_CHUNK), F32), pltpu.SemaphoreType.DMA((W_SLOTS,)),
                            pltpu.VMEM((2, tg, half), jnp.uint32), pltpu.SemaphoreType.DMA((2,)),
                            pltpu.SMEM((2,), jnp.int32)],
        ),
        compiler_params=_cparams("arbitrary"),
        name="expert_swiglu",
    )(*tables, hs, w1, w3, w2)


def _combine_kernel(src_ref, loff_ref, cnt_ref, rows_ref, x_ref, ei_ref, pw_ref, mod_ref, y_ref, o_ref,
                    ybuf, sem, *, tm, srows):
    i = pl.program_id(0)
    slot = i % 2

    def fetch(tile, into):
        ybuf[into] = jnp.zeros(ybuf.shape[1:], ybuf.dtype)
        for e in range(N_EXPERTS):
            s = tile * N_EXPERTS + e
            n_rows = cnt_ref[s]

            @pl.when(n_rows > 0)
            def _():
                _segment_copy(y_ref, ybuf.at[into], src_ref[s], loff_ref[s], n_rows, sem.at[into]).start()

    @pl.when(i == 0)
    def _():
        fetch(i, slot)

    @pl.when(i + 1 < pl.num_programs(0))
    def _():
        fetch(i + 1, 1 - slot)

    @pl.when(rows_ref[i] > 0)
    def _():
        _segment_copy(y_ref, ybuf.at[slot], 0, 0, rows_ref[i], sem.at[slot]).wait()

    lo, hi = _unpack_bf16_pair(ybuf[slot])
    ys = jnp.concatenate([lo.astype(BF16), hi.astype(BF16)], axis=1)
    r_idx = lax.broadcasted_iota(jnp.int32, (tm, srows), 1)
    mix = jnp.zeros(x_ref.shape, F32)
    for k in range(TOP_K):
        pick = jnp.where(r_idx == ei_ref[:, TOP_K + k:TOP_K + k + 1], 1.0, 0.0).astype(BF16)
        mix = mix + pw_ref[:, k:k + 1] * jnp.dot(pick, ys, preferred_element_type=F32)
    o_ref[...] = x_ref[...] + mod_ref[5:6, :] * mix


def _combine(seg_src, seg_loff, seg_cnt, tile_rows, x, ei, pw, mod, y, seq):
    n, d = x.shape
    tm = TM_ROUTE
    tpb = seq // tm
    tok = lambda width: pl.BlockSpec((tm, width), lambda i, *_: (i, 0))
    return pl.pallas_call(
        functools.partial(_combine_kernel, tm=tm, srows=SORT_ROWS),
        out_shape=jax.ShapeDtypeStruct((n, d), F32),
        grid_spec=pltpu.PrefetchScalarGridSpec(
            num_scalar_prefetch=4,
            grid=(n // tm,),
            in_specs=[tok(d), tok(LANES), tok(LANES),
                      pl.BlockSpec((None, 6, d), lambda i, *_: (i // tpb, 0, 0)),
                      pl.BlockSpec(memory_space=pl.ANY)],
            out_specs=tok(d),
            scratch_shapes=[pltpu.VMEM((2, SORT_ROWS, d // 2), jnp.uint32), pltpu.SemaphoreType.DMA((2,))],
        ),
        compiler_params=_cparams("arbitrary"),
        name="moe_combine",
    )(seg_src, seg_loff, seg_cnt, tile_rows, x, ei, pw, mod, y)


def _pack_w_in(w_in_all, layer):
    d = w_in_all.shape[1]
    col = lambda start, width: w_in_all[layer, :, start:start + width]
    w = MIX_W
    o_ckv = 3 * w + Q_LORA
    o_kr = o_ckv + KV_LORA
    o_su = o_kr + QK_ROPE
    o_rq = o_su + 2 * w
    o_gate = o_rq + 4 * w
    half = HEAD_DIM // 2
    perm = np.array([h * HEAD_DIM + part * half + i
                     for part in range(2) for h in range(N_GROUPS) for i in range(half)])
    z = lambda k: jnp.zeros((d, k), w_in_all.dtype)
    cols = [
        col(o_gate, N_BRANCH * d),
        col(0, 3 * w + Q_LORA),
        col(o_rq, w)[:, perm], col(o_rq + w, w)[:, perm],
        col(o_rq + 2 * w, 2 * w),
        col(o_su, 2 * w),
        col(o_ckv, KV_LORA),
        z(QK_NOPE), col(o_kr, QK_ROPE), z(LANES - QK_HEAD),
    ]
    return jnp.concatenate(cols, axis=1).astype(BF16)


def _swap_rope_halves(a):
    hr = QK_ROPE // 2
    return jnp.concatenate([a[..., :QK_NOPE], a[..., QK_NOPE + hr:QK_HEAD], a[..., QK_NOPE:QK_NOPE + hr],
                            a[..., QK_HEAD:]], axis=-1)


def _mla_params(cq_g, w_uq, ckv_g, w_ukv, qn_g, kn_g):
    pad = LANES - QK_HEAD
    wq = w_uq.reshape(Q_LORA, N_GROUPS, QK_HEAD)
    wq = jnp.pad(wq, ((0, 0), (0, 0), (0, pad)))
    wkv = w_ukv.reshape(KV_LORA, N_GROUPS, QK_NOPE + V_HEAD)
    wk = jnp.pad(wkv[:, :, :QK_NOPE], ((0, 0), (0, 0), (0, LANES - QK_NOPE)))
    wv = jnp.pad(wkv[:, :, QK_NOPE:], ((0, 0), (0, 0), (0, LANES - V_HEAD)))
    qg = jnp.pad(qn_g, (0, pad))[None, :]
    kg = jnp.pad(kn_g, (0, pad))[None, :]
    bound = (QK_HEAD ** 0.5 * LOG2_E) * jnp.max(jnp.abs(qn_g)) * jnp.max(jnp.abs(kn_g))
    static_shift = bound <= MAX_STATIC_SHIFT
    lane = jnp.arange(LANES)
    qaug = (lane == QK_HEAD).astype(F32)[None, :]
    kaug = qaug * jnp.where(static_shift, -bound, 0.0)
    vaug = jnp.tile((lane == V_HEAD).astype(F32), N_GROUPS)[None, :]
    params = {
        "cq_g": cq_g[None, :], "ckv_g": ckv_g[None, :],
        "wqa": wq.reshape(Q_LORA, -1).astype(BF16),
        "wqb": _swap_rope_halves(wq).reshape(Q_LORA, -1).astype(BF16),
        "wk": wk.reshape(KV_LORA, -1).astype(BF16),
        "wv": wv.reshape(KV_LORA, -1).astype(BF16),
        "qga": qg, "qgb": _swap_rope_halves(qg), "kga": kg, "kgb": _swap_rope_halves(kg),
        "qaug": qaug, "kaug": kaug, "vaug": vaug,
        "swap": (_swap_rope_halves(lane[None, :])[0][None, :] == lane[:, None]).astype(BF16),
    }
    return params, static_shift


def _mixer_consts():
    f32 = np.float32
    h = np.arange(N_GROUPS, dtype=f32)
    log_gamma = np.log1p(-(f32(2.0) ** (f32(-5.0) - h))).astype(f32)
    pos = np.arange(CHUNK, dtype=f32)
    rel = pos[:, None] - pos[None, :]
    dec = np.where(rel >= 0, np.exp(log_gamma[:, None, None] * np.maximum(rel, f32(0.0))), f32(0.0)).astype(f32)
    lane = np.arange(MIX_W)
    head_k = (lane % LANES) // (HEAD_DIM // 2)
    head_v = lane // HEAD_DIM
    lg_k = log_gamma[head_k]
    return {
        "dec": jnp.asarray(dec.reshape(N_GROUPS * CHUNK, CHUNK)),
        "kdec": jnp.asarray(np.exp(lg_k[None, :] * (CHUNK - 1.0 - pos)[:, None]).astype(f32)),
        "qdec": jnp.asarray(np.exp(lg_k[None, :] * (pos + 1.0)[:, None]).astype(f32)),
        "cdec": jnp.asarray(np.broadcast_to(np.exp(lg_k * f32(CHUNK)).astype(f32)[:, None], (MIX_W, MIX_W))),
        "bd": jnp.asarray((head_k[:, None] == head_v[None, :]).astype(f32)),
        "gmat": jnp.asarray((head_v[:, None] == head_v[None, :]).astype(f32) / HEAD_DIM).astype(BF16),
        "mk": jnp.asarray((head_k[None, :] == np.arange(N_GROUPS)[:, None]).astype(f32)),
        "mv": jnp.asarray((head_v[None, :] == np.arange(N_GROUPS)[:, None]).astype(f32)),
    }


def _mixer_params(consts, conv_w, gv_g, w_s, b_s, ret_g, w_branch, w_o):
    p = dict(consts)
    ws = jnp.tril(w_s)
    p.update({
        "conv_w": conv_w,
        "gv_g": gv_g.reshape(1, MIX_W),
        "ws_cat": jnp.transpose(ws, (1, 0, 2)).reshape(CHUNK, N_GROUPS * CHUNK).astype(BF16),
        "bs_mat": jnp.repeat(b_s.T, HEAD_DIM, axis=1),
        "ret_g": ret_g.reshape(1, MIX_W),
        "w_branch": w_branch.astype(BF16),
        "w_o": w_o.astype(BF16),
    })
    return p


def _moe_layout(meta, tot, n_tiles):
    totals = tot[0, :N_EXPERTS]
    padded = ((totals + TG_MOE - 1) // TG_MOE) * TG_MOE
    ends = jnp.cumsum(padded)
    starts = ends - padded
    seg_cnt = meta[:, 0, :N_EXPERTS]
    seg_carry = meta[:, 1, :N_EXPERTS]
    seg_loff = meta[:, 2, :N_EXPERTS]
    seg_grouped = starts[None, :] + seg_carry
    tile_start = jnp.arange(n_tiles, dtype=jnp.int32) * TG_MOE
    tile_e = jnp.sum((tile_start[:, None] >= ends[None, :]).astype(jnp.int32), axis=1)
    used = (tile_start < ends[-1]).astype(jnp.int32)
    last_e = jnp.sum((ends[-1] - 1 >= ends).astype(jnp.int32))
    tile_e = jnp.minimum(jnp.where(used == 1, tile_e, last_e), N_EXPERTS - 1)
    tile_off = tile_start - starts[tile_e]
    run_start = seg_carry[:, tile_e]
    run_end = run_start + seg_cnt[:, tile_e]
    tile_ilo = jnp.sum((run_end <= tile_off[None, :]).astype(jnp.int32), axis=0) * used
    tile_ihi = jnp.sum((run_start < tile_off[None, :] + TG_MOE).astype(jnp.int32), axis=0) * used
    flat = lambda a: a.reshape(-1).astype(jnp.int32)
    tile_tables = (flat(tile_e), flat(used), flat(tile_off * used), flat(tile_ilo), flat(tile_ihi))
    seg_tables = (flat(seg_cnt), flat(seg_carry), flat(seg_loff))
    return tile_tables, seg_tables, flat(seg_grouped)


def kernel(x, c, positions, norm1_g, norm2_g, ada_w, ada_b, w_in, conv_w, cq_g, w_uq, ckv_g, w_ukv, qn_g, kn_g, gv_g, w_s, b_s, ret_g, w_branch, w_o, ffn_w1, ffn_w3, ffn_w2, router_w, router_b, moe_w1, moe_w3, moe_w2):
    batch, seq, d = x.shape
    depth = ada_w.shape[0]
    n = batch * seq
    assert seq % max(TM_PROJ, TQ_ATT, TM_MIX, TM_FFN, TM_ROUTE) == 0
    assert d // 2 % LANES == 0

    c_t = jnp.pad(c, ((0, 8 - batch), (0, 0))).T
    ada = _ada(c_t, ada_w, ada_b, batch)[:, :batch].reshape(depth, batch, 6, d)
    cosr, sinr, cm, sm = _rope_tables(positions.astype(F32).reshape(n, 1))

    mixer_consts = _mixer_consts()
    xt = x.reshape(n, d)
    for l in range(depth):
        mod = ada[l]
        mla_p, static_shift = _mla_params(cq_g[l], w_uq[l], ckv_g[l], w_ukv[l], qn_g[l], kn_g[l])
        proj, q, k, v = _inproj(xt, mod, norm1_g[l][None, :], _pack_w_in(w_in, l), cm, sm, mla_p, seq)
        y_mla = lax.cond(static_shift,
                         functools.partial(_flash, batch=batch, seq=seq, online_max=False),
                         functools.partial(_flash, batch=batch, seq=seq, online_max=True), q, k, v)
        mp = _mixer_params(mixer_consts, conv_w[l], gv_g[l], w_s[l], b_s[l], ret_g[l], w_branch[l], w_o[l])
        xt = _mixers(proj, y_mla, xt, cosr, sinr, mod, mp, seq)
        g2n = norm2_g[l][None, :]
        if l % 2 == 0:
            i = l // 2
            xt = _dense_ffn(xt, mod, g2n, ffn_w1[i].astype(BF16), ffn_w3[i].astype(BF16),
                            ffn_w2[i].astype(BF16), seq)
        else:
            i = l // 2
            rw = jnp.pad(router_w[i], ((0, 0), (0, LANES - N_EXPERTS)))
            rw_hi = rw.astype(BF16)
            rw_pad = jnp.concatenate([rw_hi, (rw - rw_hi.astype(F32)).astype(BF16)], axis=1)
            rb_pad = jnp.pad(router_b[i], (0, LANES - N_EXPERTS), constant_values=-1e30)[None, :]
            hs, ei, pw, meta, tot = _router(xt, mod, g2n, rw_pad, rb_pad, seq)
            max_rows = n * TOP_K + N_EXPERTS * (n // TM_ROUTE) * (ROW_ALIGN - 1)
            n_tiles = -(-max_rows // TG_MOE) + N_EXPERTS
            tile_tables, seg_tables, seg_grouped = _moe_layout(meta, tot, n_tiles)
            seg_cnt, _, seg_loff = seg_tables
            tile_rows = jnp.sum(seg_cnt.reshape(-1, N_EXPERTS), axis=1)
            y = _expert_ffn(tile_tables, seg_tables, hs, moe_w1[i], moe_w3[i], moe_w2[i], n_tiles)
            xt = _combine(seg_grouped, seg_loff, seg_cnt, tile_rows, xt, ei, pw, mod, y, seq)
    return xt.reshape(batch, seq, d)
```

```python
import functools

import jax
import jax.numpy as jnp
import numpy as np
from jax import lax
from jax.experimental import pallas as pl
from jax.experimental.pallas import tpu as pltpu

F32 = jnp.float32
BF16 = jnp.bfloat16
HIGHEST = lax.Precision.HIGHEST

HEAD_DIM = 64
N_GROUPS = 4
MIX_W = N_GROUPS * HEAD_DIM
N_BRANCH = 4
CONV_W = 3
Q_LORA = 256
KV_LORA = 128
QK_NOPE = 64
QK_ROPE = 32
QK_HEAD = QK_NOPE + QK_ROPE
V_HEAD = 64
CHUNK = 128
N_EXPERTS = 8
TOP_K = 2
ROPE_THETA = 10000.0
EPS = 1e-6
LOG2_E = 1.4426950408889634
MAX_STATIC_SHIFT = 50.0

LANES = 128
VMEM_LIMIT_BYTES = 56 * 1024 * 1024

COL_GATES = 0
COL_A = 4096
COL_CQ = COL_A + 3 * MIX_W
COL_R = 5120
COL_SU = 6144
COL_CKV = 6656
COL_KRA = 6784
N_IN = 6912

TM_PROJ = 512
TN_PROJ = 768
TQ_ATT = 1024
TM_MIX = 512
TM_FFN = 512
TM_ROUTE = 512
ROW_ALIGN = 8
SORT_ROWS = TOP_K * TM_ROUTE + N_EXPERTS * ROW_ALIGN
TG_MOE = 512
W_CHUNK = 512
W_SLOTS = 8


def _cparams(*sem):
    return pltpu.CompilerParams(dimension_semantics=sem, vmem_limit_bytes=VMEM_LIMIT_BYTES)


def _sigmoid(x):
    return jnp.tanh(x * 0.5) * 0.5 + 0.5


def _group_mean(x, gmat_bf16):
    hi = x.astype(BF16)
    lo = (x - hi.astype(F32)).astype(BF16)
    return (jnp.dot(hi, gmat_bf16, preferred_element_type=F32)
            + jnp.dot(lo, gmat_bf16, preferred_element_type=F32))


def _pack_bf16_pair(lo, hi):
    lo_bits = lax.bitcast_convert_type(lo.astype(BF16).astype(F32), jnp.uint32)
    hi_bits = lax.bitcast_convert_type(hi.astype(BF16).astype(F32), jnp.uint32)
    return (lo_bits >> 16) | (hi_bits & jnp.uint32(0xFFFF0000))


def _unpack_bf16_pair(p):
    lo = lax.bitcast_convert_type(p << 16, F32)
    hi = lax.bitcast_convert_type(p & jnp.uint32(0xFFFF0000), F32)
    return lo, hi


def _norm_mod(x, g, shift, scale):
    y = x * lax.rsqrt(jnp.mean(x * x, axis=-1, keepdims=True) + EPS)
    return (y * g) * (1.0 + scale) + shift


def _ada_kernel(ct_ref, w_ref, b_ref, o_ref, *, batch):
    ct = ct_ref[...]
    cond = ct * _sigmoid(ct)
    w = w_ref[...]
    o_ref[...] = jnp.zeros(o_ref.shape, F32)
    for b in range(batch):
        o_ref[b:b + 1, :] = jnp.sum(w * cond[:, b:b + 1], axis=0, keepdims=True) + b_ref[...]


def _ada(c_t, ada_w, ada_b, batch):
    n_layer, d, d6 = ada_w.shape
    rows = c_t.shape[1]
    tn = 2048
    return pl.pallas_call(
        functools.partial(_ada_kernel, batch=batch),
        out_shape=jax.ShapeDtypeStruct((n_layer, rows, d6), F32),
        grid=(n_layer, d6 // tn),
        in_specs=[
            pl.BlockSpec((d, rows), lambda l, j: (0, 0)),
            pl.BlockSpec((None, d, tn), lambda l, j: (l, 0, j)),
            pl.BlockSpec((None, 1, tn), lambda l, j: (l, 0, j)),
        ],
        out_specs=pl.BlockSpec((None, rows, tn), lambda l, j: (l, 0, j)),
        compiler_params=_cparams("parallel", "parallel"),
        name="ada_mod",
    )(c_t, ada_w, ada_b.reshape(n_layer, 1, d6))


def _rope_kernel(pos_ref, inv_ref, cr_ref, sr_ref, cm_ref, sm_ref):
    half_r = HEAD_DIM // 2
    half_m = QK_ROPE // 2
    tm = pos_ref.shape[0]
    low = lax.broadcasted_iota(jnp.int32, (tm // 2, LANES), 1) < LANES // 2
    ang = jnp.where(low, pos_ref[0:tm // 2, :], pos_ref[tm // 2:tm, :]) * inv_ref[...]
    c = jnp.cos(ang)
    s = jnp.sin(ang)
    c = jnp.concatenate([c, pltpu.roll(c, LANES // 2, axis=1)], axis=0)
    s = jnp.concatenate([s, pltpu.roll(s, LANES // 2, axis=1)], axis=0)
    lane = lax.broadcasted_iota(jnp.int32, c.shape, 1)

    def tile_r(t):
        t = jnp.where(lane < half_r, t, 0.0)
        out = t
        for k in range(1, LANES // half_r):
            out = out + pltpu.roll(t, k * half_r, axis=1)
        return out

    cr_ref[...] = tile_r(c)
    sr_ref[...] = tile_r(s)
    first = jnp.logical_and(lane >= QK_NOPE, lane < QK_NOPE + half_m)
    second = jnp.logical_and(lane >= QK_NOPE + half_m, lane < QK_HEAD)
    c1, c2 = pltpu.roll(c, QK_NOPE - half_r, axis=1), pltpu.roll(c, QK_NOPE + half_m - half_r, axis=1)
    s1, s2 = pltpu.roll(s, QK_NOPE - half_r, axis=1), pltpu.roll(s, QK_NOPE + half_m - half_r, axis=1)
    cm_ref[...] = jnp.where(first, c1, jnp.where(second, c2, 1.0))
    sm_ref[...] = jnp.where(first, -s1, jnp.where(second, s2, 0.0))


def _rope_tables(pos_f):
    n = pos_f.shape[0]
    tm = 1024
    half_r = HEAD_DIM // 2
    half_m = QK_ROPE // 2
    inv_r = ROPE_THETA ** (-jnp.arange(half_r, dtype=F32) / half_r)
    inv_m = ROPE_THETA ** (-jnp.arange(half_m, dtype=F32) / half_m)
    inv = jnp.concatenate([inv_r, inv_m, jnp.zeros((LANES // 2 - half_r - half_m,), F32)])
    inv = jnp.tile(inv, 2)[None, :]
    tab = pl.BlockSpec((tm, LANES), lambda i: (i, 0))
    shape = jax.ShapeDtypeStruct((n, LANES), F32)
    return pl.pallas_call(
        _rope_kernel,
        out_shape=(shape, shape, shape, shape),
        grid=(n // tm,),
        in_specs=[pl.BlockSpec((tm, 1), lambda i: (i, 0)), pl.BlockSpec((1, LANES), lambda i: (0, 0))],
        out_specs=(tab, tab, tab, tab),
        compiler_params=_cparams("parallel"),
        name="rope_tables",
    )(pos_f, inv)


def _resident(shape):
    return pl.BlockSpec(shape, lambda *_: (0,) * len(shape), pipeline_mode=pl.Buffered(1))


def _mla_prep(cq_b, ckv_b, kra_b, cm, sm, cqg_ref, wqa_ref, wqb_ref, ckvg_ref, wk_ref, wv_ref, qga_ref, qgb_ref,
              kga_ref, kgb_ref, qaug_ref, kaug_ref, vaug_ref, swap_ref, q_ref, k_ref, v_ref):
    cq = cq_b.astype(F32)
    cqn = (cq * lax.rsqrt(jnp.mean(cq * cq, axis=-1, keepdims=True) + EPS) * cqg_ref[...]).astype(BF16)
    qa = jnp.dot(cqn, wqa_ref[...], preferred_element_type=F32)
    qb = jnp.dot(cqn, wqb_ref[...], preferred_element_type=F32)
    ckv = ckv_b.astype(F32)
    ckvn = (ckv * lax.rsqrt(jnp.mean(ckv * ckv, axis=-1, keepdims=True) + EPS) * ckvg_ref[...]).astype(BF16)
    ka = jnp.dot(ckvn, wk_ref[...], preferred_element_type=F32)
    v_ref[...] = (jnp.dot(ckvn, wv_ref[...], preferred_element_type=F32) + vaug_ref[...]).astype(BF16)
    kra = kra_b.astype(F32)
    krb = jnp.dot(kra_b, swap_ref[...], preferred_element_type=F32)
    scale = QK_HEAD ** -0.5 * LOG2_E
    q_cos, q_sin = cm * (qga_ref[...] * scale), sm * (qgb_ref[...] * scale)
    k_cos, k_sin = cm * kga_ref[...], sm * kgb_ref[...]
    for h in range(N_GROUPS):
        sl = slice(h * LANES, (h + 1) * LANES)
        qah, qbh = qa[:, sl], qb[:, sl]
        r = lax.rsqrt(jnp.sum(qah * qah, axis=-1, keepdims=True) * (1.0 / QK_HEAD) + EPS)
        q_ref[:, sl] = ((qah * q_cos + qbh * q_sin) * r + qaug_ref[...]).astype(BF16)
        kah = ka[:, sl] + kra
        kbh = ka[:, sl] + krb
        r = lax.rsqrt(jnp.sum(kah * kah, axis=-1, keepdims=True) * (1.0 / QK_HEAD) + EPS)
        k_ref[:, sl] = ((kah * k_cos + kbh * k_sin) * r + kaug_ref[...]).astype(BF16)


def _inproj_kernel(x_ref, mod_ref, g_ref, w_ref, cm_ref, sm_ref, *rest):
    mla_refs, (o_ref, q_ref, k_ref, v_ref) = rest[:-4], rest[-4:]
    h = _norm_mod(x_ref[...], g_ref[...], mod_ref[0:1, :], mod_ref[1:2, :]).astype(BF16)

    def chunk(c):
        cols = slice(c * TN_PROJ, (c + 1) * TN_PROJ)
        out = jnp.dot(h, w_ref[:, cols], preferred_element_type=F32).astype(BF16)
        o_ref[:, cols] = out
        return out

    c_q, c_kv = COL_CQ // TN_PROJ, COL_CKV // TN_PROJ
    lat_q = chunk(c_q)
    lat_kv = chunk(c_kv)
    q0, kv0, kr0 = COL_CQ - c_q * TN_PROJ, COL_CKV - c_kv * TN_PROJ, COL_KRA - c_kv * TN_PROJ
    _mla_prep(lat_q[:, q0:q0 + Q_LORA], lat_kv[:, kv0:kv0 + KV_LORA], lat_kv[:, kr0:kr0 + LANES],
              cm_ref[...], sm_ref[...], *mla_refs, q_ref, k_ref, v_ref)
    for c in range(N_IN // TN_PROJ):
        if c not in (c_q, c_kv):
            chunk(c)


def _inproj(x, mod, g, w, cm, sm, p, seq):
    n, d = x.shape
    tm = TM_PROJ
    tpb = seq // tm
    hw = N_GROUPS * LANES
    assert COL_KRA // TN_PROJ == COL_CKV // TN_PROJ and (COL_CQ + Q_LORA - 1) // TN_PROJ == COL_CQ // TN_PROJ

    def full(a):
        return pl.BlockSpec(a.shape, lambda i: (0,) * a.ndim)

    weights = [p["cq_g"], p["wqa"], p["wqb"], p["ckv_g"], p["wk"], p["wv"],
               p["qga"], p["qgb"], p["kga"], p["kgb"], p["qaug"], p["kaug"], p["vaug"], p["swap"]]
    table = pl.BlockSpec((tm, LANES), lambda i: (i, 0))
    head_tile = pl.BlockSpec((tm, hw), lambda i: (i, 0))
    heads = jax.ShapeDtypeStruct((n, hw), BF16)
    return pl.pallas_call(
        _inproj_kernel,
        out_shape=(jax.ShapeDtypeStruct((n, N_IN), BF16), heads, heads, heads),
        grid=(n // tm,),
        in_specs=[
            pl.BlockSpec((tm, d), lambda i: (i, 0)),
            pl.BlockSpec((None, 6, d), lambda i: (i // tpb, 0, 0)),
            pl.BlockSpec((1, d), lambda i: (0, 0)),
            _resident((d, N_IN)), table, table,
        ] + [full(a) for a in weights],
        out_specs=(pl.BlockSpec((tm, N_IN), lambda i: (i, 0)), head_tile, head_tile, head_tile),
        compiler_params=_cparams("parallel"),
        name="in_proj",
    )(x, mod, g, w, cm, sm, *weights)


def _flash_kernel(qi_ref, kj_ref, q_ref, k_ref, v_ref, o_ref, acc_scr, *rest, tq, online_max):
    i = qi_ref[pl.program_id(1)]
    j = kj_ref[pl.program_id(1)]

    @pl.when(j == 0)
    def _():
        acc_scr[...] = jnp.zeros(acc_scr.shape, F32)
        if online_max:
            rest[0][...] = jnp.full(rest[0].shape, -jnp.inf, F32)

    def block(q0, nq, nk, masked):
        rows = slice(q0, q0 + nq)
        if masked:
            row = lax.broadcasted_iota(jnp.int32, (nq, nk), 0) + q0
            col = lax.broadcasted_iota(jnp.int32, (nq, nk), 1)
            keep = col <= row
        for h in range(N_GROUPS):
            sl = slice(h * LANES, (h + 1) * LANES)
            s = lax.dot_general(q_ref[rows, sl], k_ref[0:nk, sl], (((1,), (1,)), ((), ())),
                                preferred_element_type=F32)
            if masked:
                s = jnp.where(keep, s, -jnp.inf)
            if online_max:
                m_scr = rest[0]
                m_prev = m_scr[h, rows]
                m_new = jnp.maximum(m_prev, jnp.max(s, axis=-1, keepdims=True))
                p = jnp.exp2(s - m_new).astype(BF16)
                acc_scr[h, rows] = jnp.exp2(m_prev - m_new) * acc_scr[h, rows] + jnp.dot(
                    p, v_ref[0:nk, sl], preferred_element_type=F32)
                m_scr[h, rows] = m_new
            else:
                acc_scr[h, rows] += jnp.dot(jnp.exp2(s).astype(BF16), v_ref[0:nk, sl],
                                            preferred_element_type=F32)

    @pl.when(j < i)
    def _():
        block(0, tq, tq, False)

    @pl.when(j == i)
    def _():
        block(0, tq // 2, tq // 2, True)
        block(tq // 2, tq // 2, tq, True)
        lane = lax.broadcasted_iota(jnp.int32, (tq, LANES), 1)
        for pr in range(N_GROUPS // 2):
            lo = acc_scr[2 * pr]
            hi = acc_scr[2 * pr + 1]
            lo = lo / lo[:, V_HEAD:V_HEAD + 1]
            hi = hi / hi[:, V_HEAD:V_HEAD + 1]
            both = jnp.where(lane < V_HEAD, lo, pltpu.roll(hi, V_HEAD, axis=1))
            o_ref[:, pr * LANES:(pr + 1) * LANES] = both.astype(BF16)


def _flash(q, k, v, batch, seq, online_max):
    n = q.shape[0]
    tq = TQ_ATT
    nq = seq // tq
    hw = N_GROUPS * LANES
    scratch = [pltpu.VMEM((N_GROUPS, tq, LANES), F32)]
    if online_max:
        scratch.append(pltpu.VMEM((N_GROUPS, tq, 1), F32))
    pairs = [(i, j) for i in range(nq) for j in range(i + 1)]
    qi = jnp.asarray([p[0] for p in pairs], jnp.int32)
    kj = jnp.asarray([p[1] for p in pairs], jnp.int32)
    q_tile = lambda b, s, qi, kj: (b * nq + qi[s], 0)
    k_tile = lambda b, s, qi, kj: (b * nq + kj[s], 0)
    return pl.pallas_call(
        functools.partial(_flash_kernel, tq=tq, online_max=online_max),
        out_shape=jax.ShapeDtypeStruct((n, MIX_W), BF16),
        grid_spec=pltpu.PrefetchScalarGridSpec(
            num_scalar_prefetch=2,
            grid=(batch, len(pairs)),
            in_specs=[pl.BlockSpec((tq, hw), q_tile), pl.BlockSpec((tq, hw), k_tile),
                      pl.BlockSpec((tq, hw), k_tile)],
            out_specs=pl.BlockSpec((tq, MIX_W), q_tile),
            scratch_shapes=scratch,
        ),
        compiler_params=_cparams("parallel", "arbitrary"),
        name="mla_flash_online" if online_max else "mla_flash",
    )(qi, kj, q, k, v)


def _gelu_tanh(x):
    return jax.nn.gelu(x, approximate=True)


def _mix_kernel(gates_ref, a_ref, r_ref, su_ref, ymla_ref, x_ref, cos_ref, sin_ref, mod_ref,
                convw_ref, gvg_ref, wscat_ref, bsmat_ref, retg_ref, dec_ref, kdec_ref, qdec_ref,
                cdec_ref, bd_ref, gmat_ref, mk_ref, mv_ref, wb_ref, wo_ref,
                o_ref, carry_scr, state_scr, ysg_scr, yret_scr, *, tm, tpb):
    i = pl.program_id(0)

    @pl.when(i % tpb == 0)
    def _():
        carry_scr[...] = jnp.zeros(carry_scr.shape, F32)
        state_scr[...] = jnp.zeros(state_scr.shape, F32)

    w = MIX_W
    a_b = a_ref[:, 0:w].astype(F32)
    u = a_ref[:, w:2 * w].astype(F32) * a_ref[:, 2 * w:3 * w].astype(F32)
    rowi = lax.broadcasted_iota(jnp.int32, (tm, w), 0)
    prev1 = carry_scr[0:1, :]
    prev2 = carry_scr[1:2, :]
    u1 = jnp.where(rowi == 0, prev1, pltpu.roll(u, 1, axis=0))
    u2 = jnp.where(rowi == 0, prev2, jnp.where(rowi == 1, prev1, pltpu.roll(u, 2, axis=0)))
    carry_scr[0:1, :] = u[tm - 1:tm, :]
    carry_scr[1:2, :] = u[tm - 2:tm - 1, :]
    y_conv = a_b * (convw_ref[0:1, :] * u2 + convw_ref[1:2, :] * u1 + convw_ref[2:3, :] * u)

    gmat = gmat_ref[...]
    s_u = _gelu_tanh(su_ref[:, 0:w].astype(F32))
    s_v = _gelu_tanh(su_ref[:, w:2 * w].astype(F32))
    ms = _group_mean(s_v * s_v, gmat)
    vn = (s_v * lax.rsqrt(ms + EPS) * gvg_ref[...]).astype(BF16)

    cosr = cos_ref[...]
    sinr = sin_ref[...]

    def rot(t):
        t1, t2 = t[:, 0:LANES], t[:, LANES:2 * LANES]
        return jnp.concatenate([t1 * cosr - t2 * sinr, t2 * cosr + t1 * sinr], axis=-1)

    rq = rot(r_ref[:, 0:w].astype(F32))
    rk = rot(r_ref[:, w:2 * w].astype(F32)) * (HEAD_DIM ** -0.5)

    for c in range(tm // CHUNK):
        rows = slice(c * CHUNK, (c + 1) * CHUNK)
        vc = vn[rows, :]
        vbd = jnp.concatenate([vc * mv_ref[g:g + 1, :].astype(BF16) for g in range(N_GROUPS)], axis=0)
        mixed = jnp.dot(wscat_ref[...], vbd, preferred_element_type=F32) + bsmat_ref[...]
        ysg_scr[rows, :] = s_u[rows, :] * mixed

        qc = rq[rows, :]
        kc = rk[rows, :]
        kcb = kc.astype(BF16)
        vcb = r_ref[rows, 2 * w:3 * w]
        qstack = jnp.concatenate([(qc * mk_ref[h:h + 1, :]).astype(BF16) for h in range(N_GROUPS)], axis=0)
        sc = lax.dot_general(qstack, kcb, (((1,), (1,)), ((), ())), preferred_element_type=F32)
        sc = (sc * dec_ref[...]).astype(BF16)
        scat = jnp.concatenate([sc[h * CHUNK:(h + 1) * CHUNK, :] for h in range(N_GROUPS)], axis=1)
        vstack = jnp.concatenate([vcb * mv_ref[h:h + 1, :].astype(BF16) for h in range(N_GROUPS)], axis=0)
        o_c = jnp.dot(scat, vstack, preferred_element_type=F32)
        state = state_scr[...]
        o_c = o_c + jnp.dot((qc * qdec_ref[...]).astype(BF16), state.astype(BF16),
                            preferred_element_type=F32)
        kd_t = jnp.transpose(kc * kdec_ref[...]).astype(BF16)
        kv = jnp.dot(kd_t, vcb, preferred_element_type=F32)
        state_scr[...] = state * cdec_ref[...] + kv * bd_ref[...]
        yret_scr[rows, :] = o_c

    o_all = yret_scr[...]
    xc = o_all - _group_mean(o_all, gmat)
    var = _group_mean(xc * xc, gmat)
    r_g = r_ref[:, 3 * w:4 * w].astype(F32)
    y_ret = (r_g * _sigmoid(r_g)) * (xc * lax.rsqrt(var + EPS) * retg_ref[...])

    d = x_ref.shape[1]
    ys = (y_conv, ymla_ref[...], ysg_scr[...], y_ret)
    merged = None
    for n in range(N_BRANCH):
        gate = _sigmoid(gates_ref[:, n * d:(n + 1) * d])
        term = gate * jnp.dot(ys[n].astype(BF16), wb_ref[n], preferred_element_type=F32).astype(BF16)
        merged = term if merged is None else merged + term
    out = jnp.dot(merged, wo_ref[...], preferred_element_type=F32)
    o_ref[...] = x_ref[...] + mod_ref[2:3, :] * out


def _mixers(proj, ymla, x, cosr, sinr, mod, p, seq):
    n, d = x.shape
    tm = TM_MIX
    tpb = seq // tm

    def col(width, offset):
        return pl.BlockSpec((tm, width), lambda i: (i, offset // width))

    def full(a):
        return pl.BlockSpec(a.shape, lambda i: (0,) * a.ndim)

    consts = [p["conv_w"], p["gv_g"], p["ws_cat"], p["bs_mat"], p["ret_g"], p["dec"], p["kdec"],
              p["qdec"], p["cdec"], p["bd"], p["gmat"], p["mk"], p["mv"], p["w_branch"], p["w_o"]]
    return pl.pallas_call(
        functools.partial(_mix_kernel, tm=tm, tpb=tpb),
        out_shape=jax.ShapeDtypeStruct((n, d), F32),
        grid=(n // tm,),
        in_specs=[col(N_BRANCH * d, COL_GATES), col(4 * MIX_W, COL_A), col(4 * MIX_W, COL_R),
                  col(2 * MIX_W, COL_SU),
                  pl.BlockSpec((tm, MIX_W), lambda i: (i, 0)),
                  pl.BlockSpec((tm, d), lambda i: (i, 0)),
                  pl.BlockSpec((tm, LANES), lambda i: (i, 0)),
                  pl.BlockSpec((tm, LANES), lambda i: (i, 0)),
                  pl.BlockSpec((None, 6, d), lambda i: (i // tpb, 0, 0))]
                 + [full(c) for c in consts],
        out_specs=pl.BlockSpec((tm, d), lambda i: (i, 0)),
        scratch_shapes=[pltpu.VMEM((8, MIX_W), F32), pltpu.VMEM((MIX_W, MIX_W), F32),
                        pltpu.VMEM((tm, MIX_W), F32), pltpu.VMEM((tm, MIX_W), F32)],
        compiler_params=_cparams("arbitrary"),
        name="mixers_merge",
    )(proj, proj, proj, proj, ymla, x, cosr, sinr, mod, *consts)


def _ffn_kernel(x_ref, mod_ref, g_ref, w1_ref, w3_ref, w2_ref, o_ref):
    x = x_ref[...]
    h = _norm_mod(x, g_ref[...], mod_ref[3:4, :], mod_ref[4:5, :]).astype(BF16)
    a = jnp.dot(h, w1_ref[...], preferred_element_type=F32)
    b = jnp.dot(h, w3_ref[...], preferred_element_type=F32)
    hid = ((a * _sigmoid(a)) * b).astype(BF16)
    o_ref[...] = x + mod_ref[5:6, :] * jnp.dot(hid, w2_ref[...], preferred_element_type=F32)


def _dense_ffn(x, mod, g, w1, w3, w2, seq):
    n, d = x.shape
    dff = w1.shape[1]
    tm = TM_FFN
    tpb = seq // tm
    return pl.pallas_call(
        _ffn_kernel,
        out_shape=jax.ShapeDtypeStruct((n, d), F32),
        grid=(n // tm,),
        in_specs=[
            pl.BlockSpec((tm, d), lambda i: (i, 0)),
            pl.BlockSpec((None, 6, d), lambda i: (i // tpb, 0, 0)),
            pl.BlockSpec((1, d), lambda i: (0, 0)),
            _resident((d, dff)), _resident((d, dff)), _resident((dff, d)),
        ],
        out_specs=pl.BlockSpec((tm, d), lambda i: (i, 0)),
        compiler_params=_cparams("parallel"),
        name="dense_swiglu",
    )(x, mod, g, w1, w3, w2)


def _router_kernel(x_ref, mod_ref, g_ref, rw_ref, rb_ref, hs_ref, ei_ref, pw_ref, meta_ref, tot_ref,
                   carry_scr, *, tm, srows):
    i = pl.program_id(0)

    @pl.when(i == 0)
    def _():
        carry_scr[...] = jnp.zeros(carry_scr.shape, F32)

    h = _norm_mod(x_ref[...], g_ref[...], mod_ref[3:4, :], mod_ref[4:5, :])

    h_hi = h.astype(BF16)
    h_lo = (h - h_hi.astype(F32)).astype(BF16)
    hw = jnp.dot(h_hi, rw_ref[...], preferred_element_type=F32)
    logits = (hw[:, :LANES] + hw[:, LANES:] + jnp.dot(h_lo, rw_ref[:, :LANES], preferred_element_type=F32)
              + rb_ref[...])
    mx = jnp.max(logits, axis=-1, keepdims=True)
    ex = jnp.exp(logits - mx)
    probs = ex / jnp.sum(ex, axis=-1, keepdims=True)
    lane = lax.broadcasted_iota(jnp.int32, (tm, LANES), 1)
    valid = lane < N_EXPERTS
    probs = jnp.where(valid, probs, -1.0)
    lane_f = lane.astype(F32)
    m1 = jnp.max(probs, axis=-1, keepdims=True)
    i1 = jnp.min(jnp.where(probs == m1, lane_f, float(LANES)), axis=-1, keepdims=True)
    rest = jnp.where(lane_f == i1, -1.0, probs)
    m2 = jnp.max(rest, axis=-1, keepdims=True)
    i2 = jnp.min(jnp.where(rest == m2, lane_f, float(LANES)), axis=-1, keepdims=True)
    den = m1 + m2
    pw_ref[...] = jnp.where(lane == 0, m1 / den, jnp.where(lane == 1, m2 / den, 0.0))

    sel1 = lane_f == i1
    sel2 = lane_f == i2
    onehot = jnp.where(sel1, 1.0, 0.0) + jnp.where(sel2, 1.0, 0.0)
    r_i = lax.broadcasted_iota(jnp.int32, (tm, tm), 0)
    c_i = lax.broadcasted_iota(jnp.int32, (tm, tm), 1)
    tri = jnp.where(c_i < r_i, 1.0, 0.0).astype(BF16)
    before = jnp.dot(tri, onehot.astype(BF16), preferred_element_type=F32)
    cnt = jnp.sum(onehot, axis=0, keepdims=True)
    cnt_al = jnp.floor((cnt + (ROW_ALIGN - 1)) * (1.0 / ROW_ALIGN)) * ROW_ALIGN
    e_r = lax.broadcasted_iota(jnp.int32, (LANES, LANES), 0)
    e_c = lax.broadcasted_iota(jnp.int32, (LANES, LANES), 1)
    upper = jnp.where(e_r < e_c, 1.0, 0.0)
    loff = jnp.dot(jnp.broadcast_to(cnt_al, (8, LANES)), upper, precision=HIGHEST,
                   preferred_element_type=F32)[0:1, :]
    slot = loff + before
    slot1 = jnp.sum(jnp.where(sel1, slot, 0.0), axis=-1, keepdims=True).astype(jnp.int32)
    slot2 = jnp.sum(jnp.where(sel2, slot, 0.0), axis=-1, keepdims=True).astype(jnp.int32)
    ei = jnp.where(lane == 0, i1, jnp.where(lane == 1, i2, 0.0)).astype(jnp.int32)
    ei_ref[...] = jnp.where(lane == 2, slot1, jnp.where(lane == 3, slot2, ei))

    r_idx = lax.broadcasted_iota(jnp.int32, (tm, srows), 1)
    place = jnp.where(r_idx == slot1, 1.0, jnp.where(r_idx == slot2, 1.0, 0.0)).astype(BF16)
    hs = lax.dot_general(place, h.astype(BF16), (((0,), (0,)), ((), ())), preferred_element_type=F32)
    half = hs.shape[1] // 2
    hs_ref[...] = _pack_bf16_pair(hs[:, :half], hs[:, half:])

    carry = carry_scr[0:1, :]
    mrow = lax.broadcasted_iota(jnp.int32, (8, LANES), 0)
    meta = jnp.where(mrow == 0, cnt_al, jnp.where(mrow == 1, carry, jnp.where(mrow == 2, loff, 0.0)))
    meta_ref[...] = meta.astype(jnp.int32)
    carry_scr[0:1, :] = carry + cnt_al
    tot_ref[...] = jnp.broadcast_to(carry + cnt_al, tot_ref.shape).astype(jnp.int32)


def _router(x, mod, g, rw_pad, rb_pad, seq):
    n, d = x.shape
    tm = TM_ROUTE
    tpb = seq // tm
    nt = n // tm
    return pl.pallas_call(
        functools.partial(_router_kernel, tm=tm, srows=SORT_ROWS),
        out_shape=(jax.ShapeDtypeStruct((nt * SORT_ROWS, d // 2), jnp.uint32),
                   jax.ShapeDtypeStruct((n, LANES), jnp.int32),
                   jax.ShapeDtypeStruct((n, LANES), F32),
                   jax.ShapeDtypeStruct((nt, 8, LANES), jnp.int32),
                   jax.ShapeDtypeStruct((8, LANES), jnp.int32)),
        grid=(nt,),
        in_specs=[
            pl.BlockSpec((tm, d), lambda i: (i, 0)),
            pl.BlockSpec((None, 6, d), lambda i: (i // tpb, 0, 0)),
            pl.BlockSpec((1, d), lambda i: (0, 0)),
            pl.BlockSpec((d, 2 * LANES), lambda i: (0, 0)),
            pl.BlockSpec((1, LANES), lambda i: (0, 0)),
        ],
        out_specs=(pl.BlockSpec((SORT_ROWS, d // 2), lambda i: (i, 0)),
                   pl.BlockSpec((tm, LANES), lambda i: (i, 0)),
                   pl.BlockSpec((tm, LANES), lambda i: (i, 0)),
                   pl.BlockSpec((None, 8, LANES), lambda i: (i, 0, 0)),
                   pl.BlockSpec((8, LANES), lambda i: (0, 0))),
        scratch_shapes=[pltpu.VMEM((8, LANES), F32)],
        compiler_params=_cparams("arbitrary"),
        name="router_top2",
    )(x, mod, g, rw_pad, rb_pad)


def _segment_copy(src_hbm, dst_hbm, src_row, dst_row, n_rows, sem):
    src_row = pl.multiple_of(src_row, ROW_ALIGN)
    dst_row = pl.multiple_of(dst_row, ROW_ALIGN)
    n_rows = pl.multiple_of(n_rows, ROW_ALIGN)
    return pltpu.make_async_copy(src_hbm.at[pl.ds(src_row, n_rows)], dst_hbm.at[pl.ds(dst_row, n_rows)], sem)


def _expert_kernel(te_ref, used_ref, toff_ref, ilo_ref, ihi_ref, cnt_ref, carry_ref, loff_ref,
                   hs_hbm, w1_hbm, w3_hbm, w2_hbm, y_ref, wb1, wb3, wb2, stage, sem, xbuf, xsem, rows_smem,
                   *, srows):
    t = pl.program_id(0)
    e = te_ref[t]
    tg = xbuf.shape[1]
    slot = t % 2
    first_of_expert = jnp.logical_or(t == 0, e != te_ref[jnp.maximum(t - 1, 0)])

    def fetch(tile, into):
        xbuf[into] = jnp.zeros(xbuf.shape[1:], xbuf.dtype)
        expert = te_ref[tile]
        first_row = toff_ref[tile]

        def piece(i, total):
            s = i * N_EXPERTS + expert
            run_start = carry_ref[s]
            lo = jnp.maximum(run_start, first_row)
            hi = jnp.minimum(run_start + cnt_ref[s], first_row + tg)

            @pl.when(hi > lo)
            def _():
                _segment_copy(hs_hbm, xbuf.at[into], i * srows + loff_ref[s] + (lo - run_start), lo - first_row,
                              hi - lo, xsem.at[into]).start()

            return total + jnp.maximum(hi - lo, 0)

        rows_smem[into] = lax.fori_loop(ilo_ref[tile], ihi_ref[tile], piece, 0)

    @pl.when(t == 0)
    def _():
        fetch(t, slot)

    @pl.when(t + 1 < pl.num_programs(0))
    def _():
        fetch(t + 1, 1 - slot)

    @pl.when(rows_smem[slot] > 0)
    def _():
        _segment_copy(hs_hbm, xbuf.at[slot], 0, 0, rows_smem[slot], xsem.at[slot]).wait()

    def rows_bf16():
        lo, hi = _unpack_bf16_pair(xbuf[slot])
        return jnp.concatenate([lo.astype(BF16), hi.astype(BF16)], axis=1)

    def swiglu(h, cols):
        a = jnp.dot(h, wb1[:, cols], preferred_element_type=F32)
        b = jnp.dot(h, wb3[:, cols], preferred_element_type=F32)
        hid = ((a * _sigmoid(a)) * b).astype(BF16)
        return jnp.dot(hid, wb2[cols, :], preferred_element_type=F32)

    def emit(acc):
        half = acc.shape[1] // 2
        y_ref[...] = _pack_bf16_pair(acc[:, :half], acc[:, half:])

    n_slots = stage.shape[0]
    dff, d = wb1.shape[1], wb1.shape[0]
    windows = [(src, dst, r, c)
               for ch in range(dff // W_CHUNK)
               for src, dst, r_list, c_list in ((w1_hbm, wb1, range(d // W_CHUNK), [ch]),
                                                (w3_hbm, wb3, range(d // W_CHUNK), [ch]),
                                                (w2_hbm, wb2, [ch], range(d // W_CHUNK)))
               for r in r_list for c in c_list]
    per_chunk = len(windows) // (dff // W_CHUNK)

    def staged_copy(expert, k):
        src, _, r, c = windows[k]
        return pltpu.make_async_copy(src.at[expert, pl.ds(r * W_CHUNK, W_CHUNK), pl.ds(c * W_CHUNK, W_CHUNK)],
                                     stage.at[k % n_slots], sem.at[k % n_slots])

    def start_stream(expert):
        for k in range(n_slots - 1):
            staged_copy(expert, k).start()

    @pl.when(jnp.logical_and(used_ref[t] == 1, first_of_expert))
    def _():
        @pl.when(t == 0)
        def _():
            start_stream(e)

        h = rows_bf16()
        acc = None
        for k, (_, dst, r, c) in enumerate(windows):
            staged_copy(e, k).wait()
            dst[r * W_CHUNK:(r + 1) * W_CHUNK, c * W_CHUNK:(c + 1) * W_CHUNK] = stage[k % n_slots].astype(BF16)
            if k + n_slots - 1 < len(windows):
                staged_copy(e, k + n_slots - 1).start()
            if (k + 1) % per_chunk == 0:
                ch = k // per_chunk
                part = swiglu(h, slice(ch * W_CHUNK, (ch + 1) * W_CHUNK))
                acc = part if acc is None else acc + part
        emit(acc)

    @pl.when(jnp.logical_and(used_ref[t] == 1, jnp.logical_not(first_of_expert)))
    def _():
        emit(swiglu(rows_bf16(), slice(None)))

    nxt = jnp.minimum(t + 1, pl.num_programs(0) - 1)
    next_is_new_expert = jnp.logical_and(jnp.logical_and(t + 1 < pl.num_programs(0), used_ref[nxt] == 1),
                                         te_ref[nxt] != e)

    @pl.when(jnp.logical_and(used_ref[t] == 1, next_is_new_expert))
    def _():
        start_stream(te_ref[nxt])

    @pl.when(used_ref[t] == 0)
    def _():
        y_ref[...] = jnp.zeros(y_ref.shape, y_ref.dtype)


def _expert_ffn(tile_tables, seg_tables, hs, w1, w3, w2, n_tiles):
    half = hs.shape[1]
    d = 2 * half
    dff = w1.shape[2]
    tg = TG_MOE
    assert dff % W_CHUNK == 0 and d % W_CHUNK == 0
    hbm = pl.BlockSpec(memory_space=pl.ANY)
    tables = tuple(tile_tables) + tuple(seg_tables)
    return pl.pallas_call(
        functools.partial(_expert_kernel, srows=SORT_ROWS),
        out_shape=jax.ShapeDtypeStruct((n_tiles * tg, half), jnp.uint32),
        grid_spec=pltpu.PrefetchScalarGridSpec(
            num_scalar_prefetch=len(tables),
            grid=(n_tiles,),
            in_specs=[hbm, hbm, hbm, hbm],
            out_specs=pl.BlockSpec((tg, half), lambda t, *_: (t, 0)),
            scratch_shapes=[pltpu.VMEM((d, dff), BF16), pltpu.VMEM((d, dff), BF16), pltpu.VMEM((dff, d), BF16),
                            pltpu.VMEM((W_SLOTS, W_CHUNK, W_CHUNK), F32), pltpu.SemaphoreType.DMA((W_SLOTS,)),
                            pltpu.VMEM((2, tg, half), jnp.uint32), pltpu.SemaphoreType.DMA((2,)),
                            pltpu.SMEM((2,), jnp.int32)],
        ),
        compiler_params=_cparams("arbitrary"),
        name="expert_swiglu",
    )(*tables, hs, w1, w3, w2)


def _combine_kernel(src_ref, loff_ref, cnt_ref, rows_ref, x_ref, ei_ref, pw_ref, mod_ref, y_ref, o_ref,
                    ybuf, sem, *, tm, srows):
    i = pl.program_id(0)
    slot = i % 2

    def fetch(tile, into):
        ybuf[into] = jnp.zeros(ybuf.shape[1:], ybuf.dtype)
        for e in range(N_EXPERTS):
            s = tile * N_EXPERTS + e
            n_rows = cnt_ref[s]

            @pl.when(n_rows > 0)
            def _():
                _segment_copy(y_ref, ybuf.at[into], src_ref[s], loff_ref[s], n_rows, sem.at[into]).start()

    @pl.when(i == 0)
    def _():
        fetch(i, slot)

    @pl.when(i + 1 < pl.num_programs(0))
    def _():
        fetch(i + 1, 1 - slot)

    @pl.when(rows_ref[i] > 0)
    def _():
        _segment_copy(y_ref, ybuf.at[slot], 0, 0, rows_ref[i], sem.at[slot]).wait()

    lo, hi = _unpack_bf16_pair(ybuf[slot])
    ys = jnp.concatenate([lo.astype(BF16), hi.astype(BF16)], axis=1)
    r_idx = lax.broadcasted_iota(jnp.int32, (tm, srows), 1)
    mix = jnp.zeros(x_ref.shape, F32)
    for k in range(TOP_K):
        pick = jnp.where(r_idx == ei_ref[:, TOP_K + k:TOP_K + k + 1], 1.0, 0.0).astype(BF16)
        mix = mix + pw_ref[:, k:k + 1] * jnp.dot(pick, ys, preferred_element_type=F32)
    o_ref[...] = x_ref[...] + mod_ref[5:6, :] * mix


def _combine(seg_src, seg_loff, seg_cnt, tile_rows, x, ei, pw, mod, y, seq):
    n, d = x.shape
    tm = TM_ROUTE
    tpb = seq // tm
    tok = lambda width: pl.BlockSpec((tm, width), lambda i, *_: (i, 0))
    return pl.pallas_call(
        functools.partial(_combine_kernel, tm=tm, srows=SORT_ROWS),
        out_shape=jax.ShapeDtypeStruct((n, d), F32),
        grid_spec=pltpu.PrefetchScalarGridSpec(
            num_scalar_prefetch=4,
            grid=(n // tm,),
            in_specs=[tok(d), tok(LANES), tok(LANES),
                      pl.BlockSpec((None, 6, d), lambda i, *_: (i // tpb, 0, 0)),
                      pl.BlockSpec(memory_space=pl.ANY)],
            out_specs=tok(d),
            scratch_shapes=[pltpu.VMEM((2, SORT_ROWS, d // 2), jnp.uint32), pltpu.SemaphoreType.DMA((2,))],
        ),
        compiler_params=_cparams("arbitrary"),
        name="moe_combine",
    )(seg_src, seg_loff, seg_cnt, tile_rows, x, ei, pw, mod, y)


def _pack_w_in(w_in):
    d = w_in.shape[0]
    col = lambda start, width: w_in[:, start:start + width]
    w = MIX_W
    o_ckv = 3 * w + Q_LORA
    o_kr = o_ckv + KV_LORA
    o_su = o_kr + QK_ROPE
    o_rq = o_su + 2 * w
    o_gate = o_rq + 4 * w
    half = HEAD_DIM // 2
    perm = np.array([h * HEAD_DIM + part * half + i
                     for part in range(2) for h in range(N_GROUPS) for i in range(half)])
    z = lambda k: jnp.zeros((d, k), w_in.dtype)
    cols = [
        col(o_gate, N_BRANCH * d),
        col(0, 3 * w + Q_LORA),
        col(o_rq, w)[:, perm], col(o_rq + w, w)[:, perm],
        col(o_rq + 2 * w, 2 * w),
        col(o_su, 2 * w),
        col(o_ckv, KV_LORA),
        z(QK_NOPE), col(o_kr, QK_ROPE), z(LANES - QK_HEAD),
    ]
    return jnp.concatenate(cols, axis=1).astype(BF16)


def _swap_rope_halves(a):
    hr = QK_ROPE // 2
    return jnp.concatenate([a[..., :QK_NOPE], a[..., QK_NOPE + hr:QK_HEAD], a[..., QK_NOPE:QK_NOPE + hr],
                            a[..., QK_HEAD:]], axis=-1)


def _mla_params(cq_g, w_uq, ckv_g, w_ukv, qn_g, kn_g):
    pad = LANES - QK_HEAD
    wq = w_uq.reshape(Q_LORA, N_GROUPS, QK_HEAD)
    wq = jnp.pad(wq, ((0, 0), (0, 0), (0, pad)))
    wkv = w_ukv.reshape(KV_LORA, N_GROUPS, QK_NOPE + V_HEAD)
    wk = jnp.pad(wkv[:, :, :QK_NOPE], ((0, 0), (0, 0), (0, LANES - QK_NOPE)))
    wv = jnp.pad(wkv[:, :, QK_NOPE:], ((0, 0), (0, 0), (0, LANES - V_HEAD)))
    qg = jnp.pad(qn_g, (0, pad))[None, :]
    kg = jnp.pad(kn_g, (0, pad))[None, :]
    bound = (QK_HEAD ** 0.5 * LOG2_E) * jnp.max(jnp.abs(qn_g)) * jnp.max(jnp.abs(kn_g))
    static_shift = bound <= MAX_STATIC_SHIFT
    lane = jnp.arange(LANES)
    qaug = (lane == QK_HEAD).astype(F32)[None, :]
    kaug = qaug * jnp.where(static_shift, -bound, 0.0)
    vaug = jnp.tile((lane == V_HEAD).astype(F32), N_GROUPS)[None, :]
    params = {
        "cq_g": cq_g[None, :], "ckv_g": ckv_g[None, :],
        "wqa": wq.reshape(Q_LORA, -1).astype(BF16),
        "wqb": _swap_rope_halves(wq).reshape(Q_LORA, -1).astype(BF16),
        "wk": wk.reshape(KV_LORA, -1).astype(BF16),
        "wv": wv.reshape(KV_LORA, -1).astype(BF16),
        "qga": qg, "qgb": _swap_rope_halves(qg), "kga": kg, "kgb": _swap_rope_halves(kg),
        "qaug": qaug, "kaug": kaug, "vaug": vaug,
        "swap": (_swap_rope_halves(lane[None, :])[0][None, :] == lane[:, None]).astype(BF16),
    }
    return params, static_shift


def _mixer_consts():
    f32 = np.float32
    h = np.arange(N_GROUPS, dtype=f32)
    log_gamma = np.log1p(-(f32(2.0) ** (f32(-5.0) - h))).astype(f32)
    pos = np.arange(CHUNK, dtype=f32)
    rel = pos[:, None] - pos[None, :]
    dec = np.where(rel >= 0, np.exp(log_gamma[:, None, None] * np.maximum(rel, f32(0.0))), f32(0.0)).astype(f32)
    lane = np.arange(MIX_W)
    head_k = (lane % LANES) // (HEAD_DIM // 2)
    head_v = lane // HEAD_DIM
    lg_k = log_gamma[head_k]
    return {
        "dec": jnp.asarray(dec.reshape(N_GROUPS * CHUNK, CHUNK)),
        "kdec": jnp.asarray(np.exp(lg_k[None, :] * (CHUNK - 1.0 - pos)[:, None]).astype(f32)),
        "qdec": jnp.asarray(np.exp(lg_k[None, :] * (pos + 1.0)[:, None]).astype(f32)),
        "cdec": jnp.asarray(np.broadcast_to(np.exp(lg_k * f32(CHUNK)).astype(f32)[:, None], (MIX_W, MIX_W))),
        "bd": jnp.asarray((head_k[:, None] == head_v[None, :]).astype(f32)),
        "gmat": jnp.asarray((head_v[:, None] == head_v[None, :]).astype(f32) / HEAD_DIM).astype(BF16),
        "mk": jnp.asarray((head_k[None, :] == np.arange(N_GROUPS)[:, None]).astype(f32)),
        "mv": jnp.asarray((head_v[None, :] == np.arange(N_GROUPS)[:, None]).astype(f32)),
    }


def _mixer_params(consts, conv_w, gv_g, w_s, b_s, ret_g, w_branch, w_o):
    p = dict(consts)
    ws = jnp.tril(w_s)
    p.update({
        "conv_w": conv_w,
        "gv_g": gv_g.reshape(1, MIX_W),
        "ws_cat": jnp.transpose(ws, (1, 0, 2)).reshape(CHUNK, N_GROUPS * CHUNK).astype(BF16),
        "bs_mat": jnp.repeat(b_s.T, HEAD_DIM, axis=1),
        "ret_g": ret_g.reshape(1, MIX_W),
        "w_branch": w_branch.astype(BF16),
        "w_o": w_o.astype(BF16),
    })
    return p


def _moe_layout(meta, tot, n_tiles):
    totals = tot[0, :N_EXPERTS]
    padded = ((totals + TG_MOE - 1) // TG_MOE) * TG_MOE
    ends = jnp.cumsum(padded)
    starts = ends - padded
    seg_cnt = meta[:, 0, :N_EXPERTS]
    seg_carry = meta[:, 1, :N_EXPERTS]
    seg_loff = meta[:, 2, :N_EXPERTS]
    seg_grouped = starts[None, :] + seg_carry
    tile_start = jnp.arange(n_tiles, dtype=jnp.int32) * TG_MOE
    tile_e = jnp.sum((tile_start[:, None] >= ends[None, :]).astype(jnp.int32), axis=1)
    used = (tile_start < ends[-1]).astype(jnp.int32)
    last_e = jnp.sum((ends[-1] - 1 >= ends).astype(jnp.int32))
    tile_e = jnp.minimum(jnp.where(used == 1, tile_e, last_e), N_EXPERTS - 1)
    tile_off = tile_start - starts[tile_e]
    run_start = seg_carry[:, tile_e]
    run_end = run_start + seg_cnt[:, tile_e]
    tile_ilo = jnp.sum((run_end <= tile_off[None, :]).astype(jnp.int32), axis=0) * used
    tile_ihi = jnp.sum((run_start < tile_off[None, :] + TG_MOE).astype(jnp.int32), axis=0) * used
    flat = lambda a: a.reshape(-1).astype(jnp.int32)
    tile_tables = (flat(tile_e), flat(used), flat(tile_off * used), flat(tile_ilo), flat(tile_ihi))
    seg_tables = (flat(seg_cnt), flat(seg_carry), flat(seg_loff))
    return tile_tables, seg_tables, flat(seg_grouped)


def kernel(x, c, positions, norm1_g, norm2_g, ada_w, ada_b, w_in, conv_w, cq_g, w_uq, ckv_g, w_ukv, qn_g, kn_g, gv_g, w_s, b_s, ret_g, w_branch, w_o, ffn_w1, ffn_w3, ffn_w2, router_w, router_b, moe_w1, moe_w3, moe_w2):
    batch, seq, d = x.shape
    depth = ada_w.shape[0]
    n = batch * seq
    assert seq % max(TM_PROJ, TQ_ATT, TM_MIX, TM_FFN, TM_ROUTE) == 0
    assert d // 2 % LANES == 0

    c_t = jnp.pad(c, ((0, 8 - batch), (0, 0))).T
    ada = _ada(c_t, ada_w, ada_b, batch)[:, :batch].reshape(depth, batch, 6, d)
    cosr, sinr, cm, sm = _rope_tables(positions.astype(F32).reshape(n, 1))

    mixer_consts = _mixer_consts()
    xt = x.reshape(n, d)
    for l in range(depth):
        mod = ada[l]
        mla_p, static_shift = _mla_params(cq_g[l], w_uq[l], ckv_g[l], w_ukv[l], qn_g[l], kn_g[l])
        proj, q, k, v = _inproj(xt, mod, norm1_g[l][None, :], _pack_w_in(w_in[l]), cm, sm, mla_p, seq)
        y_mla = lax.cond(static_shift,
                         functools.partial(_flash, batch=batch, seq=seq, online_max=False),
                         functools.partial(_flash, batch=batch, seq=seq, online_max=True), q, k, v)
        mp = _mixer_params(mixer_consts, conv_w[l], gv_g[l], w_s[l], b_s[l], ret_g[l], w_branch[l], w_o[l])
        xt = _mixers(proj, y_mla, xt, cosr, sinr, mod, mp, seq)
        g2n = norm2_g[l][None, :]
        if l % 2 == 0:
            i = l // 2
            xt = _dense_ffn(xt, mod, g2n, ffn_w1[i].astype(BF16), ffn_w3[i].astype(BF16),
                            ffn_w2[i].astype(BF16), seq)
        else:
            i = l // 2
            rw = jnp.pad(router_w[i], ((0, 0), (0, LANES - N_EXPERTS)))
            rw_hi = rw.astype(BF16)
            rw_pad = jnp.concatenate([rw_hi, (rw - rw_hi.astype(F32)).astype(BF16)], axis=1)
            rb_pad = jnp.pad(router_b[i], (0, LANES - N_EXPERTS), constant_values=-1e30)[None, :]
            hs, ei, pw, meta, tot = _router(xt, mod, g2n, rw_pad, rb_pad, seq)
            max_rows = n * TOP_K + N_EXPERTS * (n // TM_ROUTE) * (ROW_ALIGN - 1)
            n_tiles = -(-max_rows // TG_MOE) + N_EXPERTS
            tile_tables, seg_tables, seg_grouped = _moe_layout(meta, tot, n_tiles)
            seg_cnt, _, seg_loff = seg_tables
            tile_rows = jnp.sum(seg_cnt.reshape(-1, N_EXPERTS), axis=1)
            y = _expert_ffn(tile_tables, seg_tables, hs, moe_w1[i], moe_w3[i], moe_w2[i], n_tiles)
            xt = _combine(seg_grouped, seg_loff, seg_cnt, tile_rows, xt, ei, pw, mod, y, seq)
    return xt.reshape(batch, seq, d)
```

```python
import functools

import jax
import jax.numpy as jnp
import numpy as np
from jax import lax
from jax.experimental import pallas as pl
from jax.experimental.pallas import tpu as pltpu

F32 = jnp.float32
BF16 = jnp.bfloat16
HIGHEST = lax.Precision.HIGHEST

HEAD_DIM = 64
N_GROUPS = 4
MIX_W = N_GROUPS * HEAD_DIM
N_BRANCH = 4
CONV_W = 3
Q_LORA = 256
KV_LORA = 128
QK_NOPE = 64
QK_ROPE = 32
QK_HEAD = QK_NOPE + QK_ROPE
V_HEAD = 64
CHUNK = 128
N_EXPERTS = 8
TOP_K = 2
ROPE_THETA = 10000.0
EPS = 1e-6
LOG2_E = 1.4426950408889634
MAX_STATIC_SHIFT = 50.0

LANES = 128
VMEM_LIMIT_BYTES = 56 * 1024 * 1024

COL_GATES = 0
COL_A = 4096
COL_CQ = COL_A + 3 * MIX_W
COL_R = 5120
COL_SU = 6144
COL_CKV = 6656
COL_KRA = 6784
N_IN = 6912

TM_PROJ = 512
TN_PROJ = 768
TQ_ATT = 1024
TM_MIX = 512
TM_FFN = 512
TM_ROUTE = 512
ROW_ALIGN = 8
SORT_ROWS = TOP_K * TM_ROUTE + N_EXPERTS * ROW_ALIGN
TG_MOE = 512
W_CHUNK = 512
W_SLOTS = 8


def _cparams(*sem):
    return pltpu.CompilerParams(dimension_semantics=sem, vmem_limit_bytes=VMEM_LIMIT_BYTES)


def _sigmoid(x):
    return jnp.tanh(x * 0.5) * 0.5 + 0.5


def _group_mean(x, gmat_bf16):
    hi = x.astype(BF16)
    lo = (x - hi.astype(F32)).astype(BF16)
    return (jnp.dot(hi, gmat_bf16, preferred_element_type=F32)
            + jnp.dot(lo, gmat_bf16, preferred_element_type=F32))


def _pack_bf16_pair(lo, hi):
    lo_bits = lax.bitcast_convert_type(lo.astype(BF16).astype(F32), jnp.uint32)
    hi_bits = lax.bitcast_convert_type(hi.astype(BF16).astype(F32), jnp.uint32)
    return (lo_bits >> 16) | (hi_bits & jnp.uint32(0xFFFF0000))


def _unpack_bf16_pair(p):
    lo = lax.bitcast_convert_type(p << 16, F32)
    hi = lax.bitcast_convert_type(p & jnp.uint32(0xFFFF0000), F32)
    return lo, hi


def _norm_mod(x, g, shift, scale):
    y = x * lax.rsqrt(jnp.mean(x * x, axis=-1, keepdims=True) + EPS)
    return (y * g) * (1.0 + scale) + shift


def _ada_kernel(ct_ref, w_ref, b_ref, o_ref, *, batch):
    ct = ct_ref[...]
    cond = ct * _sigmoid(ct)
    w = w_ref[...]
    o_ref[...] = jnp.zeros(o_ref.shape, F32)
    for b in range(batch):
        o_ref[b:b + 1, :] = jnp.sum(w * cond[:, b:b + 1], axis=0, keepdims=True) + b_ref[...]


def _ada(c_t, ada_w, ada_b, batch):
    n_layer, d, d6 = ada_w.shape
    rows = c_t.shape[1]
    tn = 2048
    return pl.pallas_call(
        functools.partial(_ada_kernel, batch=batch),
        out_shape=jax.ShapeDtypeStruct((n_layer, rows, d6), F32),
        grid=(n_layer, d6 // tn),
        in_specs=[
            pl.BlockSpec((d, rows), lambda l, j: (0, 0)),
            pl.BlockSpec((None, d, tn), lambda l, j: (l, 0, j)),
            pl.BlockSpec((None, 1, tn), lambda l, j: (l, 0, j)),
        ],
        out_specs=pl.BlockSpec((None, rows, tn), lambda l, j: (l, 0, j)),
        compiler_params=_cparams("parallel", "parallel"),
        name="ada_mod",
    )(c_t, ada_w, ada_b.reshape(n_layer, 1, d6))


def _rope_kernel(pos_ref, inv_ref, cr_ref, sr_ref, cm_ref, sm_ref):
    half_r = HEAD_DIM // 2
    half_m = QK_ROPE // 2
    tm = pos_ref.shape[0]
    low = lax.broadcasted_iota(jnp.int32, (tm // 2, LANES), 1) < LANES // 2
    ang = jnp.where(low, pos_ref[0:tm // 2, :], pos_ref[tm // 2:tm, :]) * inv_ref[...]
    c = jnp.cos(ang)
    s = jnp.sin(ang)
    c = jnp.concatenate([c, pltpu.roll(c, LANES // 2, axis=1)], axis=0)
    s = jnp.concatenate([s, pltpu.roll(s, LANES // 2, axis=1)], axis=0)
    lane = lax.broadcasted_iota(jnp.int32, c.shape, 1)

    def tile_r(t):
        t = jnp.where(lane < half_r, t, 0.0)
        out = t
        for k in range(1, LANES // half_r):
            out = out + pltpu.roll(t, k * half_r, axis=1)
        return out

    cr_ref[...] = tile_r(c)
    sr_ref[...] = tile_r(s)
    first = jnp.logical_and(lane >= QK_NOPE, lane < QK_NOPE + half_m)
    second = jnp.logical_and(lane >= QK_NOPE + half_m, lane < QK_HEAD)
    c1, c2 = pltpu.roll(c, QK_NOPE - half_r, axis=1), pltpu.roll(c, QK_NOPE + half_m - half_r, axis=1)
    s1, s2 = pltpu.roll(s, QK_NOPE - half_r, axis=1), pltpu.roll(s, QK_NOPE + half_m - half_r, axis=1)
    cm_ref[...] = jnp.where(first, c1, jnp.where(second, c2, 1.0))
    sm_ref[...] = jnp.where(first, -s1, jnp.where(second, s2, 0.0))


def _rope_tables(pos_f):
    n = pos_f.shape[0]
    tm = 1024
    half_r = HEAD_DIM // 2
    half_m = QK_ROPE // 2
    inv_r = ROPE_THETA ** (-jnp.arange(half_r, dtype=F32) / half_r)
    inv_m = ROPE_THETA ** (-jnp.arange(half_m, dtype=F32) / half_m)
    inv = jnp.concatenate([inv_r, inv_m, jnp.zeros((LANES // 2 - half_r - half_m,), F32)])
    inv = jnp.tile(inv, 2)[None, :]
    tab = pl.BlockSpec((tm, LANES), lambda i: (i, 0))
    shape = jax.ShapeDtypeStruct((n, LANES), F32)
    return pl.pallas_call(
        _rope_kernel,
        out_shape=(shape, shape, shape, shape),
        grid=(n // tm,),
        in_specs=[pl.BlockSpec((tm, 1), lambda i: (i, 0)), pl.BlockSpec((1, LANES), lambda i: (0, 0))],
        out_specs=(tab, tab, tab, tab),
        compiler_params=_cparams("parallel"),
        name="rope_tables",
    )(pos_f, inv)


def _resident(shape):
    return pl.BlockSpec(shape, lambda *_: (0,) * len(shape), pipeline_mode=pl.Buffered(1))


def _mla_prep(cq_b, ckv_b, kra_b, cm, sm, cqg_ref, wqa_ref, wqb_ref, ckvg_ref, wk_ref, wv_ref, qga_ref, qgb_ref,
              kga_ref, kgb_ref, qaug_ref, kaug_ref, vaug_ref, swap_ref, q_ref, k_ref, v_ref):
    cq = cq_b.astype(F32)
    cqn = (cq * lax.rsqrt(jnp.mean(cq * cq, axis=-1, keepdims=True) + EPS) * cqg_ref[...]).astype(BF16)
    qa = jnp.dot(cqn, wqa_ref[...], preferred_element_type=F32)
    qb = jnp.dot(cqn, wqb_ref[...], preferred_element_type=F32)
    ckv = ckv_b.astype(F32)
    ckvn = (ckv * lax.rsqrt(jnp.mean(ckv * ckv, axis=-1, keepdims=True) + EPS) * ckvg_ref[...]).astype(BF16)
    ka = jnp.dot(ckvn, wk_ref[...], preferred_element_type=F32)
    v_ref[...] = (jnp.dot(ckvn, wv_ref[...], preferred_element_type=F32) + vaug_ref[...]).astype(BF16)
    kra = kra_b.astype(F32)
    krb = jnp.dot(kra_b, swap_ref[...], preferred_element_type=F32)
    scale = QK_HEAD ** -0.5 * LOG2_E
    q_cos, q_sin = cm * (qga_ref[...] * scale), sm * (qgb_ref[...] * scale)
    k_cos, k_sin = cm * kga_ref[...], sm * kgb_ref[...]
    for h in range(N_GROUPS):
        sl = slice(h * LANES, (h + 1) * LANES)
        qah, qbh = qa[:, sl], qb[:, sl]
        r = lax.rsqrt(jnp.sum(qah * qah, axis=-1, keepdims=True) * (1.0 / QK_HEAD) + EPS)
        q_ref[:, sl] = ((qah * q_cos + qbh * q_sin) * r + qaug_ref[...]).astype(BF16)
        kah = ka[:, sl] + kra
        kbh = ka[:, sl] + krb
        r = lax.rsqrt(jnp.sum(kah * kah, axis=-1, keepdims=True) * (1.0 / QK_HEAD) + EPS)
        k_ref[:, sl] = ((kah * k_cos + kbh * k_sin) * r + kaug_ref[...]).astype(BF16)


def _inproj_kernel(x_ref, mod_ref, g_ref, w_ref, cm_ref, sm_ref, *rest):
    mla_refs, (o_ref, q_ref, k_ref, v_ref) = rest[:-4], rest[-4:]
    h = _norm_mod(x_ref[...], g_ref[...], mod_ref[0:1, :], mod_ref[1:2, :]).astype(BF16)

    def chunk(c):
        cols = slice(c * TN_PROJ, (c + 1) * TN_PROJ)
        out = jnp.dot(h, w_ref[:, cols], preferred_element_type=F32).astype(BF16)
        o_ref[:, cols] = out
        return out

    c_q, c_kv = COL_CQ // TN_PROJ, COL_CKV // TN_PROJ
    lat_q = chunk(c_q)
    lat_kv = chunk(c_kv)
    q0, kv0, kr0 = COL_CQ - c_q * TN_PROJ, COL_CKV - c_kv * TN_PROJ, COL_KRA - c_kv * TN_PROJ
    _mla_prep(lat_q[:, q0:q0 + Q_LORA], lat_kv[:, kv0:kv0 + KV_LORA], lat_kv[:, kr0:kr0 + LANES],
              cm_ref[...], sm_ref[...], *mla_refs, q_ref, k_ref, v_ref)
    for c in range(N_IN // TN_PROJ):
        if c not in (c_q, c_kv):
            chunk(c)


def _inproj(x, mod, g, w, cm, sm, p, seq):
    n, d = x.shape
    tm = TM_PROJ
    tpb = seq // tm
    hw = N_GROUPS * LANES
    assert COL_KRA // TN_PROJ == COL_CKV // TN_PROJ and (COL_CQ + Q_LORA - 1) // TN_PROJ == COL_CQ // TN_PROJ

    def full(a):
        return pl.BlockSpec(a.shape, lambda i: (0,) * a.ndim)

    weights = [p["cq_g"], p["wqa"], p["wqb"], p["ckv_g"], p["wk"], p["wv"],
               p["qga"], p["qgb"], p["kga"], p["kgb"], p["qaug"], p["kaug"], p["vaug"], p["swap"]]
    table = pl.BlockSpec((tm, LANES), lambda i: (i, 0))
    head_tile = pl.BlockSpec((tm, hw), lambda i: (i, 0))
    heads = jax.ShapeDtypeStruct((n, hw), BF16)
    return pl.pallas_call(
        _inproj_kernel,
        out_shape=(jax.ShapeDtypeStruct((n, N_IN), BF16), heads, heads, heads),
        grid=(n // tm,),
        in_specs=[
            pl.BlockSpec((tm, d), lambda i: (i, 0)),
            pl.BlockSpec((None, 6, d), lambda i: (i // tpb, 0, 0)),
            pl.BlockSpec((1, d), lambda i: (0, 0)),
            _resident((d, N_IN)), table, table,
        ] + [full(a) for a in weights],
        out_specs=(pl.BlockSpec((tm, N_IN), lambda i: (i, 0)), head_tile, head_tile, head_tile),
        compiler_params=_cparams("parallel"),
        name="in_proj",
    )(x, mod, g, w, cm, sm, *weights)


def _flash_kernel(qi_ref, kj_ref, q_ref, k_ref, v_ref, o_ref, acc_scr, *rest, tq, online_max):
    i = qi_ref[pl.program_id(1)]
    j = kj_ref[pl.program_id(1)]

    @pl.when(j == 0)
    def _():
        acc_scr[...] = jnp.zeros(acc_scr.shape, F32)
        if online_max:
            rest[0][...] = jnp.full(rest[0].shape, -jnp.inf, F32)

    def block(q0, nq, nk, masked):
        rows = slice(q0, q0 + nq)
        if masked:
            row = lax.broadcasted_iota(jnp.int32, (nq, nk), 0) + q0
            col = lax.broadcasted_iota(jnp.int32, (nq, nk), 1)
            keep = col <= row
        for h in range(N_GROUPS):
            sl = slice(h * LANES, (h + 1) * LANES)
            s = lax.dot_general(q_ref[rows, sl], k_ref[0:nk, sl], (((1,), (1,)), ((), ())),
                                preferred_element_type=F32)
            if masked:
                s = jnp.where(keep, s, -jnp.inf)
            if online_max:
                m_scr = rest[0]
                m_prev = m_scr[h, rows]
                m_new = jnp.maximum(m_prev, jnp.max(s, axis=-1, keepdims=True))
                p = jnp.exp2(s - m_new).astype(BF16)
                acc_scr[h, rows] = jnp.exp2(m_prev - m_new) * acc_scr[h, rows] + jnp.dot(
                    p, v_ref[0:nk, sl], preferred_element_type=F32)
                m_scr[h, rows] = m_new
            else:
                acc_scr[h, rows] += jnp.dot(jnp.exp2(s).astype(BF16), v_ref[0:nk, sl],
                                            preferred_element_type=F32)

    @pl.when(j < i)
    def _():
        block(0, tq, tq, False)

    @pl.when(j == i)
    def _():
        block(0, tq // 2, tq // 2, True)
        block(tq // 2, tq // 2, tq, True)
        lane = lax.broadcasted_iota(jnp.int32, (tq, LANES), 1)
        for pr in range(N_GROUPS // 2):
            lo = acc_scr[2 * pr]
            hi = acc_scr[2 * pr + 1]
            lo = lo / lo[:, V_HEAD:V_HEAD + 1]
            hi = hi / hi[:, V_HEAD:V_HEAD + 1]
            both = jnp.where(lane < V_HEAD, lo, pltpu.roll(hi, V_HEAD, axis=1))
            o_ref[:, pr * LANES:(pr + 1) * LANES] = both.astype(BF16)


def _flash(q, k, v, batch, seq, online_max):
    n = q.shape[0]
    tq = TQ_ATT
    nq = seq // tq
    hw = N_GROUPS * LANES
    scratch = [pltpu.VMEM((N_GROUPS, tq, LANES), F32)]
    if online_max:
        scratch.append(pltpu.VMEM((N_GROUPS, tq, 1), F32))
    pairs = [(i, j) for i in range(nq) for j in range(i + 1)]
    qi = jnp.asarray([p[0] for p in pairs], jnp.int32)
    kj = jnp.asarray([p[1] for p in pairs], jnp.int32)
    q_tile = lambda b, s, qi, kj: (b * nq + qi[s], 0)
    k_tile = lambda b, s, qi, kj: (b * nq + kj[s], 0)
    return pl.pallas_call(
        functools.partial(_flash_kernel, tq=tq, online_max=online_max),
        out_shape=jax.ShapeDtypeStruct((n, MIX_W), BF16),
        grid_spec=pltpu.PrefetchScalarGridSpec(
            num_scalar_prefetch=2,
            grid=(batch, len(pairs)),
            in_specs=[pl.BlockSpec((tq, hw), q_tile), pl.BlockSpec((tq, hw), k_tile),
                      pl.BlockSpec((tq, hw), k_tile)],
            out_specs=pl.BlockSpec((tq, MIX_W), q_tile),
            scratch_shapes=scratch,
        ),
        compiler_params=_cparams("parallel", "arbitrary"),
        name="mla_flash_online" if online_max else "mla_flash",
    )(qi, kj, q, k, v)


def _gelu_tanh(x):
    return jax.nn.gelu(x, approximate=True)


def _mix_kernel(gates_ref, a_ref, r_ref, su_ref, ymla_ref, x_ref, cos_ref, sin_ref, mod_ref,
                convw_ref, gvg_ref, wscat_ref, bsmat_ref, retg_ref, dec_ref, kdec_ref, qdec_ref,
                cdec_ref, bd_ref, gmat_ref, mk_ref, mv_ref, wb_ref, wo_ref,
                o_ref, carry_scr, state_scr, ysg_scr, yret_scr, *, tm, tpb):
    i = pl.program_id(0)

    @pl.when(i % tpb == 0)
    def _():
        carry_scr[...] = jnp.zeros(carry_scr.shape, F32)
        state_scr[...] = jnp.zeros(state_scr.shape, F32)

    w = MIX_W
    a_b = a_ref[:, 0:w].astype(F32)
    u = a_ref[:, w:2 * w].astype(F32) * a_ref[:, 2 * w:3 * w].astype(F32)
    rowi = lax.broadcasted_iota(jnp.int32, (tm, w), 0)
    prev1 = carry_scr[0:1, :]
    prev2 = carry_scr[1:2, :]
    u1 = jnp.where(rowi == 0, prev1, pltpu.roll(u, 1, axis=0))
    u2 = jnp.where(rowi == 0, prev2, jnp.where(rowi == 1, prev1, pltpu.roll(u, 2, axis=0)))
    carry_scr[0:1, :] = u[tm - 1:tm, :]
    carry_scr[1:2, :] = u[tm - 2:tm - 1, :]
    y_conv = a_b * (convw_ref[0:1, :] * u2 + convw_ref[1:2, :] * u1 + convw_ref[2:3, :] * u)

    gmat = gmat_ref[...]
    s_u = _gelu_tanh(su_ref[:, 0:w].astype(F32))
    s_v = _gelu_tanh(su_ref[:, w:2 * w].astype(F32))
    ms = _group_mean(s_v * s_v, gmat)
    vn = (s_v * lax.rsqrt(ms + EPS) * gvg_ref[...]).astype(BF16)

    cosr = cos_ref[...]
    sinr = sin_ref[...]

    def rot(t):
        t1, t2 = t[:, 0:LANES], t[:, LANES:2 * LANES]
        return jnp.concatenate([t1 * cosr - t2 * sinr, t2 * cosr + t1 * sinr], axis=-1)

    rq = rot(r_ref[:, 0:w].astype(F32))
    rk = rot(r_ref[:, w:2 * w].astype(F32)) * (HEAD_DIM ** -0.5)

    for c in range(tm // CHUNK):
        rows = slice(c * CHUNK, (c + 1) * CHUNK)
        vc = vn[rows, :]
        vbd = jnp.concatenate([vc * mv_ref[g:g + 1, :].astype(BF16) for g in range(N_GROUPS)], axis=0)
        mixed = jnp.dot(wscat_ref[...], vbd, preferred_element_type=F32) + bsmat_ref[...]
        ysg_scr[rows, :] = s_u[rows, :] * mixed

        qc = rq[rows, :]
        kc = rk[rows, :]
        kcb = kc.astype(BF16)
        vcb = r_ref[rows, 2 * w:3 * w]
        qstack = jnp.concatenate([(qc * mk_ref[h:h + 1, :]).astype(BF16) for h in range(N_GROUPS)], axis=0)
        sc = lax.dot_general(qstack, kcb, (((1,), (1,)), ((), ())), preferred_element_type=F32)
        sc = (sc * dec_ref[...]).astype(BF16)
        scat = jnp.concatenate([sc[h * CHUNK:(h + 1) * CHUNK, :] for h in range(N_GROUPS)], axis=1)
        vstack = jnp.concatenate([vcb * mv_ref[h:h + 1, :].astype(BF16) for h in range(N_GROUPS)], axis=0)
        o_c = jnp.dot(scat, vstack, preferred_element_type=F32)
        state = state_scr[...]
        o_c = o_c + jnp.dot((qc * qdec_ref[...]).astype(BF16), state.astype(BF16),
                            preferred_element_type=F32)
        kd_t = jnp.transpose(kc * kdec_ref[...]).astype(BF16)
        kv = jnp.dot(kd_t, vcb, preferred_element_type=F32)
        state_scr[...] = state * cdec_ref[...] + kv * bd_ref[...]
        yret_scr[rows, :] = o_c

    o_all = yret_scr[...]
    xc = o_all - _group_mean(o_all, gmat)
    var = _group_mean(xc * xc, gmat)
    r_g = r_ref[:, 3 * w:4 * w].astype(F32)
    y_ret = (r_g * _sigmoid(r_g)) * (xc * lax.rsqrt(var + EPS) * retg_ref[...])

    d = x_ref.shape[1]
    ys = (y_conv, ymla_ref[...], ysg_scr[...], y_ret)
    merged = None
    for n in range(N_BRANCH):
        gate = _sigmoid(gates_ref[:, n * d:(n + 1) * d])
        term = gate * jnp.dot(ys[n].astype(BF16), wb_ref[n], preferred_element_type=F32).astype(BF16)
        merged = term if merged is None else merged + term
    out = jnp.dot(merged, wo_ref[...], preferred_element_type=F32)
    o_ref[...] = x_ref[...] + mod_ref[2:3, :] * out


def _mixers(proj, ymla, x, cosr, sinr, mod, p, seq):
    n, d = x.shape
    tm = TM_MIX
    tpb = seq // tm

    def col(width, offset):
        return pl.BlockSpec((tm, width), lambda i: (i, offset // width))

    def full(a):
        return pl.BlockSpec(a.shape, lambda i: (0,) * a.ndim)

    consts = [p["conv_w"], p["gv_g"], p["ws_cat"], p["bs_mat"], p["ret_g"], p["dec"], p["kdec"],
              p["qdec"], p["cdec"], p["bd"], p["gmat"], p["mk"], p["mv"], p["w_branch"], p["w_o"]]
    return pl.pallas_call(
        functools.partial(_mix_kernel, tm=tm, tpb=tpb),
        out_shape=jax.ShapeDtypeStruct((n, d), F32),
        grid=(n // tm,),
        in_specs=[col(N_BRANCH * d, COL_GATES), col(4 * MIX_W, COL_A), col(4 * MIX_W, COL_R),
                  col(2 * MIX_W, COL_SU),
                  pl.BlockSpec((tm, MIX_W), lambda i: (i, 0)),
                  pl.BlockSpec((tm, d), lambda i: (i, 0)),
                  pl.BlockSpec((tm, LANES), lambda i: (i, 0)),
                  pl.BlockSpec((tm, LANES), lambda i: (i, 0)),
                  pl.BlockSpec((None, 6, d), lambda i: (i // tpb, 0, 0))]
                 + [full(c) for c in consts],
        out_specs=pl.BlockSpec((tm, d), lambda i: (i, 0)),
        scratch_shapes=[pltpu.VMEM((8, MIX_W), F32), pltpu.VMEM((MIX_W, MIX_W), F32),
                        pltpu.VMEM((tm, MIX_W), F32), pltpu.VMEM((tm, MIX_W), F32)],
        compiler_params=_cparams("arbitrary"),
        name="mixers_merge",
    )(proj, proj, proj, proj, ymla, x, cosr, sinr, mod, *consts)


def _ffn_kernel(x_ref, mod_ref, g_ref, w1_ref, w3_ref, w2_ref, o_ref):
    x = x_ref[...]
    h = _norm_mod(x, g_ref[...], mod_ref[3:4, :], mod_ref[4:5, :]).astype(BF16)
    a = jnp.dot(h, w1_ref[...], preferred_element_type=F32)
    b = jnp.dot(h, w3_ref[...], preferred_element_type=F32)
    hid = ((a * _sigmoid(a)) * b).astype(BF16)
    o_ref[...] = x + mod_ref[5:6, :] * jnp.dot(hid, w2_ref[...], preferred_element_type=F32)


def _dense_ffn(x, mod, g, w1, w3, w2, seq):
    n, d = x.shape
    dff = w1.shape[1]
    tm = TM_FFN
    tpb = seq // tm
    return pl.pallas_call(
        _ffn_kernel,
        out_shape=jax.ShapeDtypeStruct((n, d), F32),
        grid=(n // tm,),
        in_specs=[
            pl.BlockSpec((tm, d), lambda i: (i, 0)),
            pl.BlockSpec((None, 6, d), lambda i: (i // tpb, 0, 0)),
            pl.BlockSpec((1, d), lambda i: (0, 0)),
            _resident((d, dff)), _resident((d, dff)), _resident((dff, d)),
        ],
        out_specs=pl.BlockSpec((tm, d), lambda i: (i, 0)),
        compiler_params=_cparams("parallel"),
        name="dense_swiglu",
    )(x, mod, g, w1, w3, w2)


def _router_kernel(x_ref, mod_ref, g_ref, rw_ref, rb_ref, hs_ref, ei_ref, pw_ref, meta_ref, tot_ref,
                   carry_scr, *, tm, srows):
    i = pl.program_id(0)

    @pl.when(i == 0)
    def _():
        carry_scr[...] = jnp.zeros(carry_scr.shape, F32)

    h = _norm_mod(x_ref[...], g_ref[...], mod_ref[3:4, :], mod_ref[4:5, :])

    h_hi = h.astype(BF16)
    h_lo = (h - h_hi.astype(F32)).astype(BF16)
    hw = jnp.dot(h_hi, rw_ref[...], preferred_element_type=F32)
    logits = (hw[:, :LANES] + hw[:, LANES:] + jnp.dot(h_lo, rw_ref[:, :LANES], preferred_element_type=F32)
              + rb_ref[...])
    mx = jnp.max(logits, axis=-1, keepdims=True)
    ex = jnp.exp(logits - mx)
    probs = ex / jnp.sum(ex, axis=-1, keepdims=True)
    lane = lax.broadcasted_iota(jnp.int32, (tm, LANES), 1)
    valid = lane < N_EXPERTS
    probs = jnp.where(valid, probs, -1.0)
    lane_f = lane.astype(F32)
    m1 = jnp.max(probs, axis=-1, keepdims=True)
    i1 = jnp.min(jnp.where(probs == m1, lane_f, float(LANES)), axis=-1, keepdims=True)
    rest = jnp.where(lane_f == i1, -1.0, probs)
    m2 = jnp.max(rest, axis=-1, keepdims=True)
    i2 = jnp.min(jnp.where(rest == m2, lane_f, float(LANES)), axis=-1, keepdims=True)
    den = m1 + m2
    pw_ref[...] = jnp.where(lane == 0, m1 / den, jnp.where(lane == 1, m2 / den, 0.0))

    sel1 = lane_f == i1
    sel2 = lane_f == i2
    onehot = jnp.where(sel1, 1.0, 0.0) + jnp.where(sel2, 1.0, 0.0)
    r_i = lax.broadcasted_iota(jnp.int32, (tm, tm), 0)
    c_i = lax.broadcasted_iota(jnp.int32, (tm, tm), 1)
    tri = jnp.where(c_i < r_i, 1.0, 0.0).astype(BF16)
    before = jnp.dot(tri, onehot.astype(BF16), preferred_element_type=F32)
    cnt = jnp.sum(onehot, axis=0, keepdims=True)
    cnt_al = jnp.floor((cnt + (ROW_ALIGN - 1)) * (1.0 / ROW_ALIGN)) * ROW_ALIGN
    e_r = lax.broadcasted_iota(jnp.int32, (LANES, LANES), 0)
    e_c = lax.broadcasted_iota(jnp.int32, (LANES, LANES), 1)
    upper = jnp.where(e_r < e_c, 1.0, 0.0)
    loff = jnp.dot(jnp.broadcast_to(cnt_al, (8, LANES)), upper, precision=HIGHEST,
                   preferred_element_type=F32)[0:1, :]
    slot = loff + before
    slot1 = jnp.sum(jnp.where(sel1, slot, 0.0), axis=-1, keepdims=True).astype(jnp.int32)
    slot2 = jnp.sum(jnp.where(sel2, slot, 0.0), axis=-1, keepdims=True).astype(jnp.int32)
    ei = jnp.where(lane == 0, i1, jnp.where(lane == 1, i2, 0.0)).astype(jnp.int32)
    ei_ref[...] = jnp.where(lane == 2, slot1, jnp.where(lane == 3, slot2, ei))

    r_idx = lax.broadcasted_iota(jnp.int32, (tm, srows), 1)
    place = jnp.where(r_idx == slot1, 1.0, jnp.where(r_idx == slot2, 1.0, 0.0)).astype(BF16)
    hs = lax.dot_general(place, h.astype(BF16), (((0,), (0,)), ((), ())), preferred_element_type=F32)
    half = hs.shape[1] // 2
    hs_ref[...] = _pack_bf16_pair(hs[:, :half], hs[:, half:])

    carry = carry_scr[0:1, :]
    mrow = lax.broadcasted_iota(jnp.int32, (8, LANES), 0)
    meta = jnp.where(mrow == 0, cnt_al, jnp.where(mrow == 1, carry, jnp.where(mrow == 2, loff, 0.0)))
    meta_ref[...] = meta.astype(jnp.int32)
    carry_scr[0:1, :] = carry + cnt_al
    tot_ref[...] = jnp.broadcast_to(carry + cnt_al, tot_ref.shape).astype(jnp.int32)


def _router(x, mod, g, rw_pad, rb_pad, seq):
    n, d = x.shape
    tm = TM_ROUTE
    tpb = seq // tm
    nt = n // tm
    return pl.pallas_call(
        functools.partial(_router_kernel, tm=tm, srows=SORT_ROWS),
        out_shape=(jax.ShapeDtypeStruct((nt * SORT_ROWS, d // 2), jnp.uint32),
                   jax.ShapeDtypeStruct((n, LANES), jnp.int32),
                   jax.ShapeDtypeStruct((n, LANES), F32),
                   jax.ShapeDtypeStruct((nt, 8, LANES), jnp.int32),
                   jax.ShapeDtypeStruct((8, LANES), jnp.int32)),
        grid=(nt,),
        in_specs=[
            pl.BlockSpec((tm, d), lambda i: (i, 0)),
            pl.BlockSpec((None, 6, d), lambda i: (i // tpb, 0, 0)),
            pl.BlockSpec((1, d), lambda i: (0, 0)),
            pl.BlockSpec((d, 2 * LANES), lambda i: (0, 0)),
            pl.BlockSpec((1, LANES), lambda i: (0, 0)),
        ],
        out_specs=(pl.BlockSpec((SORT_ROWS, d // 2), lambda i: (i, 0)),
                   pl.BlockSpec((tm, LANES), lambda i: (i, 0)),
                   pl.BlockSpec((tm, LANES), lambda i: (i, 0)),
                   pl.BlockSpec((None, 8, LANES), lambda i: (i, 0, 0)),
                   pl.BlockSpec((8, LANES), lambda i: (0, 0))),
        scratch_shapes=[pltpu.VMEM((8, LANES), F32)],
        compiler_params=_cparams("arbitrary"),
        name="router_top2",
    )(x, mod, g, rw_pad, rb_pad)


def _segment_copy(src_hbm, dst_hbm, src_row, dst_row, n_rows, sem):
    src_row = pl.multiple_of(src_row, ROW_ALIGN)
    dst_row = pl.multiple_of(dst_row, ROW_ALIGN)
    n_rows = pl.multiple_of(n_rows, ROW_ALIGN)
    return pltpu.make_async_copy(src_hbm.at[pl.ds(src_row, n_rows)], dst_hbm.at[pl.ds(dst_row, n_rows)], sem)


def _expert_kernel(te_ref, used_ref, toff_ref, ilo_ref, ihi_ref, cnt_ref, carry_ref, loff_ref,
                   hs_hbm, w1_hbm, w3_hbm, w2_hbm, y_ref, wb1, wb3, wb2, stage, sem, xbuf, xsem, rows_smem,
                   *, srows):
    t = pl.program_id(0)
    e = te_ref[t]
    tg = xbuf.shape[1]
    slot = t % 2
    first_of_expert = jnp.logical_or(t == 0, e != te_ref[jnp.maximum(t - 1, 0)])

    def fetch(tile, into):
        xbuf[into] = jnp.zeros(xbuf.shape[1:], xbuf.dtype)
        expert = te_ref[tile]
        first_row = toff_ref[tile]

        def piece(i, total):
            s = i * N_EXPERTS + expert
            run_start = carry_ref[s]
            lo = jnp.maximum(run_start, first_row)
            hi = jnp.minimum(run_start + cnt_ref[s], first_row + tg)

            @pl.when(hi > lo)
            def _():
                _segment_copy(hs_hbm, xbuf.at[into], i * srows + loff_ref[s] + (lo - run_start), lo - first_row,
                              hi - lo, xsem.at[into]).start()

            return total + jnp.maximum(hi - lo, 0)

        rows_smem[into] = lax.fori_loop(ilo_ref[tile], ihi_ref[tile], piece, 0)

    @pl.when(t == 0)
    def _():
        fetch(t, slot)

    @pl.when(t + 1 < pl.num_programs(0))
    def _():
        fetch(t + 1, 1 - slot)

    @pl.when(rows_smem[slot] > 0)
    def _():
        _segment_copy(hs_hbm, xbuf.at[slot], 0, 0, rows_smem[slot], xsem.at[slot]).wait()

    def rows_bf16():
        lo, hi = _unpack_bf16_pair(xbuf[slot])
        return jnp.concatenate([lo.astype(BF16), hi.astype(BF16)], axis=1)

    def swiglu(h, cols):
        a = jnp.dot(h, wb1[:, cols], preferred_element_type=F32)
        b = jnp.dot(h, wb3[:, cols], preferred_element_type=F32)
        hid = ((a * _sigmoid(a)) * b).astype(BF16)
        return jnp.dot(hid, wb2[cols, :], preferred_element_type=F32)

    def emit(acc):
        half = acc.shape[1] // 2
        y_ref[...] = _pack_bf16_pair(acc[:, :half], acc[:, half:])

    @pl.when(jnp.logical_and(used_ref[t] == 1, first_of_expert))
    def _():
        n_slots = stage.shape[0]
        dff, d = wb1.shape[1], wb1.shape[0]
        windows = [(src, dst, r, c)
                   for ch in range(dff // W_CHUNK)
                   for src, dst, r_list, c_list in ((w1_hbm, wb1, range(d // W_CHUNK), [ch]),
                                                    (w3_hbm, wb3, range(d // W_CHUNK), [ch]),
                                                    (w2_hbm, wb2, [ch], range(d // W_CHUNK)))
                   for r in r_list for c in c_list]
        per_chunk = len(windows) // (dff // W_CHUNK)

        def staged_copy(k):
            src, _, r, c = windows[k]
            return pltpu.make_async_copy(src.at[e, pl.ds(r * W_CHUNK, W_CHUNK), pl.ds(c * W_CHUNK, W_CHUNK)],
                                         stage.at[k % n_slots], sem.at[k % n_slots])

        for k in range(n_slots - 1):
            staged_copy(k).start()
        h = rows_bf16()
        acc = None
        for k, (_, dst, r, c) in enumerate(windows):
            staged_copy(k).wait()
            dst[r * W_CHUNK:(r + 1) * W_CHUNK, c * W_CHUNK:(c + 1) * W_CHUNK] = stage[k % n_slots].astype(BF16)
            if k + n_slots - 1 < len(windows):
                staged_copy(k + n_slots - 1).start()
            if (k + 1) % per_chunk == 0:
                ch = k // per_chunk
                part = swiglu(h, slice(ch * W_CHUNK, (ch + 1) * W_CHUNK))
                acc = part if acc is None else acc + part
        emit(acc)

    @pl.when(jnp.logical_and(used_ref[t] == 1, jnp.logical_not(first_of_expert)))
    def _():
        emit(swiglu(rows_bf16(), slice(None)))

    @pl.when(used_ref[t] == 0)
    def _():
        y_ref[...] = jnp.zeros(y_ref.shape, y_ref.dtype)


def _expert_ffn(tile_tables, seg_tables, hs, w1, w3, w2, n_tiles):
    half = hs.shape[1]
    d = 2 * half
    dff = w1.shape[2]
    tg = TG_MOE
    assert dff % W_CHUNK == 0 and d % W_CHUNK == 0
    hbm = pl.BlockSpec(memory_space=pl.ANY)
    tables = tuple(tile_tables) + tuple(seg_tables)
    return pl.pallas_call(
        functools.partial(_expert_kernel, srows=SORT_ROWS),
        out_shape=jax.ShapeDtypeStruct((n_tiles * tg, half), jnp.uint32),
        grid_spec=pltpu.PrefetchScalarGridSpec(
            num_scalar_prefetch=len(tables),
            grid=(n_tiles,),
            in_specs=[hbm, hbm, hbm, hbm],
            out_specs=pl.BlockSpec((tg, half), lambda t, *_: (t, 0)),
            scratch_shapes=[pltpu.VMEM((d, dff), BF16), pltpu.VMEM((d, dff), BF16), pltpu.VMEM((dff, d), BF16),
                            pltpu.VMEM((W_SLOTS, W_CHUNK, W_CHUNK), F32), pltpu.SemaphoreType.DMA((W_SLOTS,)),
                            pltpu.VMEM((2, tg, half), jnp.uint32), pltpu.SemaphoreType.DMA((2,)),
                            pltpu.SMEM((2,), jnp.int32)],
        ),
        compiler_params=_cparams("arbitrary"),
        name="expert_swiglu",
    )(*tables, hs, w1, w3, w2)


def _combine_kernel(src_ref, loff_ref, cnt_ref, rows_ref, x_ref, ei_ref, pw_ref, mod_ref, y_ref, o_ref,
                    ybuf, sem, *, tm, srows):
    i = pl.program_id(0)
    slot = i % 2

    def fetch(tile, into):
        ybuf[into] = jnp.zeros(ybuf.shape[1:], ybuf.dtype)
        for e in range(N_EXPERTS):
            s = tile * N_EXPERTS + e
            n_rows = cnt_ref[s]

            @pl.when(n_rows > 0)
            def _():
                _segment_copy(y_ref, ybuf.at[into], src_ref[s], loff_ref[s], n_rows, sem.at[into]).start()

    @pl.when(i == 0)
    def _():
        fetch(i, slot)

    @pl.when(i + 1 < pl.num_programs(0))
    def _():
        fetch(i + 1, 1 - slot)

    @pl.when(rows_ref[i] > 0)
    def _():
        _segment_copy(y_ref, ybuf.at[slot], 0, 0, rows_ref[i], sem.at[slot]).wait()

    lo, hi = _unpack_bf16_pair(ybuf[slot])
    ys = jnp.concatenate([lo.astype(BF16), hi.astype(BF16)], axis=1)
    r_idx = lax.broadcasted_iota(jnp.int32, (tm, srows), 1)
    mix = jnp.zeros(x_ref.shape, F32)
    for k in range(TOP_K):
        pick = jnp.where(r_idx == ei_ref[:, TOP_K + k:TOP_K + k + 1], 1.0, 0.0).astype(BF16)
        mix = mix + pw_ref[:, k:k + 1] * jnp.dot(pick, ys, preferred_element_type=F32)
    o_ref[...] = x_ref[...] + mod_ref[5:6, :] * mix


def _combine(seg_src, seg_loff, seg_cnt, tile_rows, x, ei, pw, mod, y, seq):
    n, d = x.shape
    tm = TM_ROUTE
    tpb = seq // tm
    tok = lambda width: pl.BlockSpec((tm, width), lambda i, *_: (i, 0))
    return pl.pallas_call(
        functools.partial(_combine_kernel, tm=tm, srows=SORT_ROWS),
        out_shape=jax.ShapeDtypeStruct((n, d), F32),
        grid_spec=pltpu.PrefetchScalarGridSpec(
            num_scalar_prefetch=4,
            grid=(n // tm,),
            in_specs=[tok(d), tok(LANES), tok(LANES),
                      pl.BlockSpec((None, 6, d), lambda i, *_: (i // tpb, 0, 0)),
                      pl.BlockSpec(memory_space=pl.ANY)],
            out_specs=tok(d),
            scratch_shapes=[pltpu.VMEM((2, SORT_ROWS, d // 2), jnp.uint32), pltpu.SemaphoreType.DMA((2,))],
        ),
        compiler_params=_cparams("arbitrary"),
        name="moe_combine",
    )(seg_src, seg_loff, seg_cnt, tile_rows, x, ei, pw, mod, y)


def _pack_w_in(w_in_all, layer):
    d = w_in_all.shape[1]
    col = lambda start, width: w_in_all[layer, :, start:start + width]
    w = MIX_W
    o_ckv = 3 * w + Q_LORA
    o_kr = o_ckv + KV_LORA
    o_su = o_kr + QK_ROPE
    o_rq = o_su + 2 * w
    o_gate = o_rq + 4 * w
    half = HEAD_DIM // 2
    perm = np.array([h * HEAD_DIM + part * half + i
                     for part in range(2) for h in range(N_GROUPS) for i in range(half)])
    z = lambda k: jnp.zeros((d, k), w_in_all.dtype)
    cols = [
        col(o_gate, N_BRANCH * d),
        col(0, 3 * w + Q_LORA),
        col(o_rq, w)[:, perm], col(o_rq + w, w)[:, perm],
        col(o_rq + 2 * w, 2 * w),
        col(o_su, 2 * w),
        col(o_ckv, KV_LORA),
        z(QK_NOPE), col(o_kr, QK_ROPE), z(LANES - QK_HEAD),
    ]
    return jnp.concatenate(cols, axis=1)


def _swap_rope_halves(a):
    hr = QK_ROPE // 2
    return jnp.concatenate([a[..., :QK_NOPE], a[..., QK_NOPE + hr:QK_HEAD], a[..., QK_NOPE:QK_NOPE + hr],
                            a[..., QK_HEAD:]], axis=-1)


def _mla_params(cq_g, w_uq, ckv_g, w_ukv, qn_g, kn_g):
    pad = LANES - QK_HEAD
    wq = w_uq.reshape(Q_LORA, N_GROUPS, QK_HEAD)
    wq = jnp.pad(wq, ((0, 0), (0, 0), (0, pad)))
    wkv = w_ukv.reshape(KV_LORA, N_GROUPS, QK_NOPE + V_HEAD)
    wk = jnp.pad(wkv[:, :, :QK_NOPE], ((0, 0), (0, 0), (0, LANES - QK_NOPE)))
    wv = jnp.pad(wkv[:, :, QK_NOPE:], ((0, 0), (0, 0), (0, LANES - V_HEAD)))
    qg = jnp.pad(qn_g, (0, pad))[None, :]
    kg = jnp.pad(kn_g, (0, pad))[None, :]
    bound = (QK_HEAD ** 0.5 * LOG2_E) * jnp.max(jnp.abs(qn_g)) * jnp.max(jnp.abs(kn_g))
    static_shift = bound <= MAX_STATIC_SHIFT
    lane = jnp.arange(LANES)
    qaug = (lane == QK_HEAD).astype(F32)[None, :]
    kaug = qaug * jnp.where(static_shift, -bound, 0.0)
    vaug = jnp.tile((lane == V_HEAD).astype(F32), N_GROUPS)[None, :]
    params = {
        "cq_g": cq_g[None, :], "ckv_g": ckv_g[None, :],
        "wqa": wq.reshape(Q_LORA, -1).astype(BF16),
        "wqb": _swap_rope_halves(wq).reshape(Q_LORA, -1).astype(BF16),
        "wk": wk.reshape(KV_LORA, -1).astype(BF16),
        "wv": wv.reshape(KV_LORA, -1).astype(BF16),
        "qga": qg, "qgb": _swap_rope_halves(qg), "kga": kg, "kgb": _swap_rope_halves(kg),
        "qaug": qaug, "kaug": kaug, "vaug": vaug,
        "swap": (_swap_rope_halves(lane[None, :])[0][None, :] == lane[:, None]).astype(BF16),
    }
    return params, static_shift


def _mixer_consts():
    f32 = np.float32
    h = np.arange(N_GROUPS, dtype=f32)
    log_gamma = np.log1p(-(f32(2.0) ** (f32(-5.0) - h))).astype(f32)
    pos = np.arange(CHUNK, dtype=f32)
    rel = pos[:, None] - pos[None, :]
    dec = np.where(rel >= 0, np.exp(log_gamma[:, None, None] * np.maximum(rel, f32(0.0))), f32(0.0)).astype(f32)
    lane = np.arange(MIX_W)
    head_k = (lane % LANES) // (HEAD_DIM // 2)
    head_v = lane // HEAD_DIM
    lg_k = log_gamma[head_k]
    return {
        "dec": jnp.asarray(dec.reshape(N_GROUPS * CHUNK, CHUNK)),
        "kdec": jnp.asarray(np.exp(lg_k[None, :] * (CHUNK - 1.0 - pos)[:, None]).astype(f32)),
        "qdec": jnp.asarray(np.exp(lg_k[None, :] * (pos + 1.0)[:, None]).astype(f32)),
        "cdec": jnp.asarray(np.broadcast_to(np.exp(lg_k * f32(CHUNK)).astype(f32)[:, None], (MIX_W, MIX_W))),
        "bd": jnp.asarray((head_k[:, None] == head_v[None, :]).astype(f32)),
        "gmat": jnp.asarray((head_v[:, None] == head_v[None, :]).astype(f32) / HEAD_DIM).astype(BF16),
        "mk": jnp.asarray((head_k[None, :] == np.arange(N_GROUPS)[:, None]).astype(f32)),
        "mv": jnp.asarray((head_v[None, :] == np.arange(N_GROUPS)[:, None]).astype(f32)),
    }


def _mixer_params(consts, conv_w, gv_g, w_s, b_s, ret_g, w_branch, w_o):
    p = dict(consts)
    ws = jnp.tril(w_s)
    p.update({
        "conv_w": conv_w,
        "gv_g": gv_g.reshape(1, MIX_W),
        "ws_cat": jnp.transpose(ws, (1, 0, 2)).reshape(CHUNK, N_GROUPS * CHUNK).astype(BF16),
        "bs_mat": jnp.repeat(b_s.T, HEAD_DIM, axis=1),
        "ret_g": ret_g.reshape(1, MIX_W),
        "w_branch": w_branch.astype(BF16),
        "w_o": w_o.astype(BF16),
    })
    return p


def _moe_layout(meta, tot, n_tiles):
    totals = tot[0, :N_EXPERTS]
    padded = ((totals + TG_MOE - 1) // TG_MOE) * TG_MOE
    ends = jnp.cumsum(padded)
    starts = ends - padded
    seg_cnt = meta[:, 0, :N_EXPERTS]
    seg_carry = meta[:, 1, :N_EXPERTS]
    seg_loff = meta[:, 2, :N_EXPERTS]
    seg_grouped = starts[None, :] + seg_carry
    tile_start = jnp.arange(n_tiles, dtype=jnp.int32) * TG_MOE
    tile_e = jnp.sum((tile_start[:, None] >= ends[None, :]).astype(jnp.int32), axis=1)
    used = (tile_start < ends[-1]).astype(jnp.int32)
    last_e = jnp.sum((ends[-1] - 1 >= ends).astype(jnp.int32))
    tile_e = jnp.minimum(jnp.where(used == 1, tile_e, last_e), N_EXPERTS - 1)
    tile_off = tile_start - starts[tile_e]
    run_start = seg_carry[:, tile_e]
    run_end = run_start + seg_cnt[:, tile_e]
    tile_ilo = jnp.sum((run_end <= tile_off[None, :]).astype(jnp.int32), axis=0) * used
    tile_ihi = jnp.sum((run_start < tile_off[None, :] + TG_MOE).astype(jnp.int32), axis=0) * used
    flat = lambda a: a.reshape(-1).astype(jnp.int32)
    tile_tables = (flat(tile_e), flat(used), flat(tile_off * used), flat(tile_ilo), flat(tile_ihi))
    seg_tables = (flat(seg_cnt), flat(seg_carry), flat(seg_loff))
    return tile_tables, seg_tables, flat(seg_grouped)


def kernel(x, c, positions, norm1_g, norm2_g, ada_w, ada_b, w_in, conv_w, cq_g, w_uq, ckv_g, w_ukv, qn_g, kn_g, gv_g, w_s, b_s, ret_g, w_branch, w_o, ffn_w1, ffn_w3, ffn_w2, router_w, router_b, moe_w1, moe_w3, moe_w2):
    batch, seq, d = x.shape
    depth = ada_w.shape[0]
    n = batch * seq
    assert seq % max(TM_PROJ, TQ_ATT, TM_MIX, TM_FFN, TM_ROUTE) == 0
    assert d // 2 % LANES == 0

    c_t = jnp.pad(c, ((0, 8 - batch), (0, 0))).T
    ada = _ada(c_t, ada_w, ada_b, batch)[:, :batch].reshape(depth, batch, 6, d)
    cosr, sinr, cm, sm = _rope_tables(positions.astype(F32).reshape(n, 1))

    mixer_consts = _mixer_consts()
    w_in = w_in.astype(BF16)
    xt = x.reshape(n, d)
    for l in range(depth):
        mod = ada[l]
        mla_p, static_shift = _mla_params(cq_g[l], w_uq[l], ckv_g[l], w_ukv[l], qn_g[l], kn_g[l])
        proj, q, k, v = _inproj(xt, mod, norm1_g[l][None, :], _pack_w_in(w_in, l), cm, sm, mla_p, seq)
        y_mla = lax.cond(static_shift,
                         functools.partial(_flash, batch=batch, seq=seq, online_max=False),
                         functools.partial(_flash, batch=batch, seq=seq, online_max=True), q, k, v)
        mp = _mixer_params(mixer_consts, conv_w[l], gv_g[l], w_s[l], b_s[l], ret_g[l], w_branch[l], w_o[l])
        xt = _mixers(proj, y_mla, xt, cosr, sinr, mod, mp, seq)
        g2n = norm2_g[l][None, :]
        if l % 2 == 0:
            i = l // 2
            xt = _dense_ffn(xt, mod, g2n, ffn_w1[i].astype(BF16), ffn_w3[i].astype(BF16),
                            ffn_w2[i].astype(BF16), seq)
        else:
            i = l // 2
            rw = jnp.pad(router_w[i], ((0, 0), (0, LANES - N_EXPERTS)))
            rw_hi = rw.astype(BF16)
            rw_pad = jnp.concatenate([rw_hi, (rw - rw_hi.astype(F32)).astype(BF16)], axis=1)
            rb_pad = jnp.pad(router_b[i], (0, LANES - N_EXPERTS), constant_values=-1e30)[None, :]
            hs, ei, pw, meta, tot = _router(xt, mod, g2n, rw_pad, rb_pad, seq)
            max_rows = n * TOP_K + N_EXPERTS * (n // TM_ROUTE) * (ROW_ALIGN - 1)
            n_tiles = -(-max_rows // TG_MOE) + N_EXPERTS
            tile_tables, seg_tables, seg_grouped = _moe_layout(meta, tot, n_tiles)
            seg_cnt, _, seg_loff = seg_tables
            tile_rows = jnp.sum(seg_cnt.reshape(-1, N_EXPERTS), axis=1)
            y = _expert_ffn(tile_tables, seg_tables, hs, moe_w1[i], moe_w3[i], moe_w2[i], n_tiles)
            xt = _combine(seg_grouped, seg_loff, seg_cnt, tile_rows, xt, ei, pw, mod, y, seq)
    return xt.reshape(batch, seq, d)
```

```python
import functools

import jax
import jax.numpy as jnp
import numpy as np
from jax import lax
from jax.experimental import pallas as pl
from jax.experimental.pallas import tpu as pltpu

F32 = jnp.float32
BF16 = jnp.bfloat16
HIGHEST = lax.Precision.HIGHEST

HEAD_DIM = 64
N_GROUPS = 4
MIX_W = N_GROUPS * HEAD_DIM
N_BRANCH = 4
CONV_W = 3
Q_LORA = 256
KV_LORA = 128
QK_NOPE = 64
QK_ROPE = 32
QK_HEAD = QK_NOPE + QK_ROPE
V_HEAD = 64
CHUNK = 128
N_EXPERTS = 8
TOP_K = 2
ROPE_THETA = 10000.0
EPS = 1e-6
LOG2_E = 1.4426950408889634
MAX_STATIC_SHIFT = 50.0

LANES = 128
VMEM_LIMIT_BYTES = 56 * 1024 * 1024

COL_GATES = 0
COL_A = 4096
COL_CQ = COL_A + 3 * MIX_W
COL_R = 5120
COL_SU = 6144
COL_CKV = 6656
COL_KRA = 6784
N_IN = 6912

TM_PROJ = 512
TN_PROJ = 768
TQ_ATT = 1024
TM_MIX = 512
TM_FFN = 512
FF_CHUNK = 256
TM_ROUTE = 512
ROW_ALIGN = 8
SORT_ROWS = TOP_K * TM_ROUTE + N_EXPERTS * ROW_ALIGN
TG_MOE = 512
W_CHUNK = 512
W_SLOTS = 16


def _cparams(*sem):
    return pltpu.CompilerParams(dimension_semantics=sem, vmem_limit_bytes=VMEM_LIMIT_BYTES)


def _sigmoid(x):
    return jnp.tanh(x * 0.5) * 0.5 + 0.5


def _group_mean(x, gmat_bf16):
    hi = x.astype(BF16)
    lo = (x - hi.astype(F32)).astype(BF16)
    return (jnp.dot(hi, gmat_bf16, preferred_element_type=F32)
            + jnp.dot(lo, gmat_bf16, preferred_element_type=F32))


def _pack_bf16_pair(lo, hi):
    lo_bits = lax.bitcast_convert_type(lo.astype(BF16).astype(F32), jnp.uint32)
    hi_bits = lax.bitcast_convert_type(hi.astype(BF16).astype(F32), jnp.uint32)
    return (lo_bits >> 16) | (hi_bits & jnp.uint32(0xFFFF0000))


def _unpack_bf16_pair(p):
    lo = lax.bitcast_convert_type(p << 16, F32)
    hi = lax.bitcast_convert_type(p & jnp.uint32(0xFFFF0000), F32)
    return lo, hi


def _norm_mod(x, g, shift, scale):
    y = x * lax.rsqrt(jnp.mean(x * x, axis=-1, keepdims=True) + EPS)
    return (y * g) * (1.0 + scale) + shift


def _ada_kernel(ct_ref, w_ref, b_ref, o_ref, *, batch):
    ct = ct_ref[...]
    cond = ct * _sigmoid(ct)
    w = w_ref[...]
    o_ref[...] = jnp.zeros(o_ref.shape, F32)
    for b in range(batch):
        o_ref[b:b + 1, :] = jnp.sum(w * cond[:, b:b + 1], axis=0, keepdims=True) + b_ref[...]


def _ada(c_t, ada_w, ada_b, batch):
    n_layer, d, d6 = ada_w.shape
    rows = c_t.shape[1]
    tn = 2048
    return pl.pallas_call(
        functools.partial(_ada_kernel, batch=batch),
        out_shape=jax.ShapeDtypeStruct((n_layer, rows, d6), F32),
        grid=(n_layer, d6 // tn),
        in_specs=[
            pl.BlockSpec((d, rows), lambda l, j: (0, 0)),
            pl.BlockSpec((None, d, tn), lambda l, j: (l, 0, j)),
            pl.BlockSpec((None, 1, tn), lambda l, j: (l, 0, j)),
        ],
        out_specs=pl.BlockSpec((None, rows, tn), lambda l, j: (l, 0, j)),
        compiler_params=_cparams("parallel", "parallel"),
        name="ada_mod",
    )(c_t, ada_w, ada_b.reshape(n_layer, 1, d6))


def _rope_kernel(pos_ref, inv_ref, cr_ref, sr_ref, cm_ref, sm_ref):
    half_r = HEAD_DIM // 2
    half_m = QK_ROPE // 2
    tm = pos_ref.shape[0]
    low = lax.broadcasted_iota(jnp.int32, (tm // 2, LANES), 1) < LANES // 2
    ang = jnp.where(low, pos_ref[0:tm // 2, :], pos_ref[tm // 2:tm, :]) * inv_ref[...]
    c = jnp.cos(ang)
    s = jnp.sin(ang)
    c = jnp.concatenate([c, pltpu.roll(c, LANES // 2, axis=1)], axis=0)
    s = jnp.concatenate([s, pltpu.roll(s, LANES // 2, axis=1)], axis=0)
    lane = lax.broadcasted_iota(jnp.int32, c.shape, 1)

    def tile_r(t):
        t = jnp.where(lane < half_r, t, 0.0)
        out = t
        for k in range(1, LANES // half_r):
            out = out + pltpu.roll(t, k * half_r, axis=1)
        return out

    cr_ref[...] = tile_r(c)
    sr_ref[...] = tile_r(s)
    first = jnp.logical_and(lane >= QK_NOPE, lane < QK_NOPE + half_m)
    second = jnp.logical_and(lane >= QK_NOPE + half_m, lane < QK_HEAD)
    c1, c2 = pltpu.roll(c, QK_NOPE - half_r, axis=1), pltpu.roll(c, QK_NOPE + half_m - half_r, axis=1)
    s1, s2 = pltpu.roll(s, QK_NOPE - half_r, axis=1), pltpu.roll(s, QK_NOPE + half_m - half_r, axis=1)
    cm_ref[...] = jnp.where(first, c1, jnp.where(second, c2, 1.0))
    sm_ref[...] = jnp.where(first, -s1, jnp.where(second, s2, 0.0))


def _rope_tables(pos_f):
    n = pos_f.shape[0]
    tm = 1024
    half_r = HEAD_DIM // 2
    half_m = QK_ROPE // 2
    inv_r = ROPE_THETA ** (-jnp.arange(half_r, dtype=F32) / half_r)
    inv_m = ROPE_THETA ** (-jnp.arange(half_m, dtype=F32) / half_m)
    inv = jnp.concatenate([inv_r, inv_m, jnp.zeros((LANES // 2 - half_r - half_m,), F32)])
    inv = jnp.tile(inv, 2)[None, :]
    tab = pl.BlockSpec((tm, LANES), lambda i: (i, 0))
    shape = jax.ShapeDtypeStruct((n, LANES), F32)
    return pl.pallas_call(
        _rope_kernel,
        out_shape=(shape, shape, shape, shape),
        grid=(n // tm,),
        in_specs=[pl.BlockSpec((tm, 1), lambda i: (i, 0)), pl.BlockSpec((1, LANES), lambda i: (0, 0))],
        out_specs=(tab, tab, tab, tab),
        compiler_params=_cparams("parallel"),
        name="rope_tables",
    )(pos_f, inv)


def _resident(shape):
    return pl.BlockSpec(shape, lambda *_: (0,) * len(shape), pipeline_mode=pl.Buffered(1))


def _mla_prep(cq_b, ckv_b, kra_b, cm, sm, cqg_ref, wqa_ref, wqb_ref, ckvg_ref, wk_ref, wv_ref, qga_ref, qgb_ref,
              kga_ref, kgb_ref, qaug_ref, kaug_ref, vaug_ref, swap_ref, q_ref, k_ref, v_ref):
    cq = cq_b.astype(F32)
    cqn = (cq * lax.rsqrt(jnp.mean(cq * cq, axis=-1, keepdims=True) + EPS) * cqg_ref[...]).astype(BF16)
    qa = jnp.dot(cqn, wqa_ref[...], preferred_element_type=F32)
    qb = jnp.dot(cqn, wqb_ref[...], preferred_element_type=F32)
    ckv = ckv_b.astype(F32)
    ckvn = (ckv * lax.rsqrt(jnp.mean(ckv * ckv, axis=-1, keepdims=True) + EPS) * ckvg_ref[...]).astype(BF16)
    ka = jnp.dot(ckvn, wk_ref[...], preferred_element_type=F32)
    v_ref[...] = (jnp.dot(ckvn, wv_ref[...], preferred_element_type=F32) + vaug_ref[...]).astype(BF16)
    kra = kra_b.astype(F32)
    krb = jnp.dot(kra_b, swap_ref[...], preferred_element_type=F32)
    scale = QK_HEAD ** -0.5 * LOG2_E
    q_cos, q_sin = cm * (qga_ref[...] * scale), sm * (qgb_ref[...] * scale)
    k_cos, k_sin = cm * kga_ref[...], sm * kgb_ref[...]
    for h in range(N_GROUPS):
        sl = slice(h * LANES, (h + 1) * LANES)
        qah, qbh = qa[:, sl], qb[:, sl]
        r = lax.rsqrt(jnp.sum(qah * qah, axis=-1, keepdims=True) * (1.0 / QK_HEAD) + EPS)
        q_ref[:, sl] = ((qah * q_cos + qbh * q_sin) * r + qaug_ref[...]).astype(BF16)
        kah = ka[:, sl] + kra
        kbh = ka[:, sl] + krb
        r = lax.rsqrt(jnp.sum(kah * kah, axis=-1, keepdims=True) * (1.0 / QK_HEAD) + EPS)
        k_ref[:, sl] = ((kah * k_cos + kbh * k_sin) * r + kaug_ref[...]).astype(BF16)


def _inproj_kernel(x_ref, mod_ref, g_ref, w_ref, cm_ref, sm_ref, *rest):
    mla_refs, (o_ref, q_ref, k_ref, v_ref) = rest[:-4], rest[-4:]
    h = _norm_mod(x_ref[...], g_ref[...], mod_ref[0:1, :], mod_ref[1:2, :]).astype(BF16)

    def chunk(c):
        cols = slice(c * TN_PROJ, (c + 1) * TN_PROJ)
        out = jnp.dot(h, w_ref[:, cols], preferred_element_type=F32).astype(BF16)
        o_ref[:, cols] = out
        return out

    c_q, c_kv = COL_CQ // TN_PROJ, COL_CKV // TN_PROJ
    lat_q = chunk(c_q)
    lat_kv = chunk(c_kv)
    q0, kv0, kr0 = COL_CQ - c_q * TN_PROJ, COL_CKV - c_kv * TN_PROJ, COL_KRA - c_kv * TN_PROJ
    _mla_prep(lat_q[:, q0:q0 + Q_LORA], lat_kv[:, kv0:kv0 + KV_LORA], lat_kv[:, kr0:kr0 + LANES],
              cm_ref[...], sm_ref[...], *mla_refs, q_ref, k_ref, v_ref)
    for c in range(N_IN // TN_PROJ):
        if c not in (c_q, c_kv):
            chunk(c)


def _inproj(x, mod, g, w, cm, sm, p, seq):
    n, d = x.shape
    tm = TM_PROJ
    tpb = seq // tm
    hw = N_GROUPS * LANES
    assert COL_KRA // TN_PROJ == COL_CKV // TN_PROJ and (COL_CQ + Q_LORA - 1) // TN_PROJ == COL_CQ // TN_PROJ

    def full(a):
        return pl.BlockSpec(a.shape, lambda i: (0,) * a.ndim)

    weights = [p["cq_g"], p["wqa"], p["wqb"], p["ckv_g"], p["wk"], p["wv"],
               p["qga"], p["qgb"], p["kga"], p["kgb"], p["qaug"], p["kaug"], p["vaug"], p["swap"]]
    table = pl.BlockSpec((tm, LANES), lambda i: (i, 0))
    head_tile = pl.BlockSpec((tm, hw), lambda i: (i, 0))
    heads = jax.ShapeDtypeStruct((n, hw), BF16)
    return pl.pallas_call(
        _inproj_kernel,
        out_shape=(jax.ShapeDtypeStruct((n, N_IN), BF16), heads, heads, heads),
        grid=(n // tm,),
        in_specs=[
            pl.BlockSpec((tm, d), lambda i: (i, 0)),
            pl.BlockSpec((None, 6, d), lambda i: (i // tpb, 0, 0)),
            pl.BlockSpec((1, d), lambda i: (0, 0)),
            _resident((d, N_IN)), table, table,
        ] + [full(a) for a in weights],
        out_specs=(pl.BlockSpec((tm, N_IN), lambda i: (i, 0)), head_tile, head_tile, head_tile),
        compiler_params=_cparams("parallel"),
        name="in_proj",
    )(x, mod, g, w, cm, sm, *weights)


def _flash_kernel(qi_ref, kj_ref, q_ref, k_ref, v_ref, o_ref, acc_scr, *rest, tq, online_max):
    i = qi_ref[pl.program_id(1)]
    j = kj_ref[pl.program_id(1)]

    @pl.when(j == 0)
    def _():
        acc_scr[...] = jnp.zeros(acc_scr.shape, F32)
        if online_max:
            rest[0][...] = jnp.full(rest[0].shape, -jnp.inf, F32)

    def block(q0, nq, nk, masked):
        rows = slice(q0, q0 + nq)
        if masked:
            row = lax.broadcasted_iota(jnp.int32, (nq, nk), 0) + q0
            col = lax.broadcasted_iota(jnp.int32, (nq, nk), 1)
            keep = col <= row
        for h in range(N_GROUPS):
            sl = slice(h * LANES, (h + 1) * LANES)
            s = lax.dot_general(q_ref[rows, sl], k_ref[0:nk, sl], (((1,), (1,)), ((), ())),
                                preferred_element_type=F32)
            if masked:
                s = jnp.where(keep, s, -jnp.inf)
            if online_max:
                m_scr = rest[0]
                m_prev = m_scr[h, rows]
                m_new = jnp.maximum(m_prev, jnp.max(s, axis=-1, keepdims=True))
                p = jnp.exp2(s - m_new).astype(BF16)
                acc_scr[h, rows] = jnp.exp2(m_prev - m_new) * acc_scr[h, rows] + jnp.dot(
                    p, v_ref[0:nk, sl], preferred_element_type=F32)
                m_scr[h, rows] = m_new
            else:
                acc_scr[h, rows] += jnp.dot(jnp.exp2(s).astype(BF16), v_ref[0:nk, sl],
                                            preferred_element_type=F32)

    @pl.when(j < i)
    def _():
        block(0, tq, tq, False)

    @pl.when(j == i)
    def _():
        block(0, tq // 2, tq // 2, True)
        block(tq // 2, tq // 2, tq, True)
        lane = lax.broadcasted_iota(jnp.int32, (tq, LANES), 1)
        for pr in range(N_GROUPS // 2):
            lo = acc_scr[2 * pr]
            hi = acc_scr[2 * pr + 1]
            lo = lo / lo[:, V_HEAD:V_HEAD + 1]
            hi = hi / hi[:, V_HEAD:V_HEAD + 1]
            both = jnp.where(lane < V_HEAD, lo, pltpu.roll(hi, V_HEAD, axis=1))
            o_ref[:, pr * LANES:(pr + 1) * LANES] = both.astype(BF16)


def _flash(q, k, v, batch, seq, online_max):
    n = q.shape[0]
    tq = TQ_ATT
    nq = seq // tq
    hw = N_GROUPS * LANES
    scratch = [pltpu.VMEM((N_GROUPS, tq, LANES), F32)]
    if online_max:
        scratch.append(pltpu.VMEM((N_GROUPS, tq, 1), F32))
    pairs = [(i, j) for i in range(nq) for j in range(i + 1)]
    qi = jnp.asarray([p[0] for p in pairs], jnp.int32)
    kj = jnp.asarray([p[1] for p in pairs], jnp.int32)
    q_tile = lambda b, s, qi, kj: (b * nq + qi[s], 0)
    k_tile = lambda b, s, qi, kj: (b * nq + kj[s], 0)
    return pl.pallas_call(
        functools.partial(_flash_kernel, tq=tq, online_max=online_max),
        out_shape=jax.ShapeDtypeStruct((n, MIX_W), BF16),
        grid_spec=pltpu.PrefetchScalarGridSpec(
            num_scalar_prefetch=2,
            grid=(batch, len(pairs)),
            in_specs=[pl.BlockSpec((tq, hw), q_tile), pl.BlockSpec((tq, hw), k_tile),
                      pl.BlockSpec((tq, hw), k_tile)],
            out_specs=pl.BlockSpec((tq, MIX_W), q_tile),
            scratch_shapes=scratch,
        ),
        compiler_params=_cparams("parallel", "arbitrary"),
        name="mla_flash_online" if online_max else "mla_flash",
    )(qi, kj, q, k, v)


def _gelu_tanh(x):
    return jax.nn.gelu(x, approximate=True)


def _mix_kernel(gates_ref, a_ref, r_ref, su_ref, ymla_ref, x_ref, cos_ref, sin_ref, mod_ref,
                convw_ref, gvg_ref, wscat_ref, bsmat_ref, retg_ref, dec_ref, kdec_ref, qdec_ref,
                cdec_ref, bd_ref, gmat_ref, mk_ref, mv_ref, wb_ref, wo_ref,
                o_ref, carry_scr, state_scr, ysg_scr, yret_scr, *, tm, tpb):
    i = pl.program_id(0)

    @pl.when(i % tpb == 0)
    def _():
        carry_scr[...] = jnp.zeros(carry_scr.shape, F32)
        state_scr[...] = jnp.zeros(state_scr.shape, F32)

    w = MIX_W
    a_b = a_ref[:, 0:w].astype(F32)
    u = a_ref[:, w:2 * w].astype(F32) * a_ref[:, 2 * w:3 * w].astype(F32)
    rowi = lax.broadcasted_iota(jnp.int32, (tm, w), 0)
    prev1 = carry_scr[0:1, :]
    prev2 = carry_scr[1:2, :]
    u1 = jnp.where(rowi == 0, prev1, pltpu.roll(u, 1, axis=0))
    u2 = jnp.where(rowi == 0, prev2, jnp.where(rowi == 1, prev1, pltpu.roll(u, 2, axis=0)))
    carry_scr[0:1, :] = u[tm - 1:tm, :]
    carry_scr[1:2, :] = u[tm - 2:tm - 1, :]
    y_conv = a_b * (convw_ref[0:1, :] * u2 + convw_ref[1:2, :] * u1 + convw_ref[2:3, :] * u)

    gmat = gmat_ref[...]
    s_u = _gelu_tanh(su_ref[:, 0:w].astype(F32))
    s_v = _gelu_tanh(su_ref[:, w:2 * w].astype(F32))
    ms = _group_mean(s_v * s_v, gmat)
    vn = (s_v * lax.rsqrt(ms + EPS) * gvg_ref[...]).astype(BF16)

    cosr = cos_ref[...]
    sinr = sin_ref[...]

    def rot(t):
        t1, t2 = t[:, 0:LANES], t[:, LANES:2 * LANES]
        return jnp.concatenate([t1 * cosr - t2 * sinr, t2 * cosr + t1 * sinr], axis=-1)

    rq = rot(r_ref[:, 0:w].astype(F32))
    rk = rot(r_ref[:, w:2 * w].astype(F32)) * (HEAD_DIM ** -0.5)

    for c in range(tm // CHUNK):
        rows = slice(c * CHUNK, (c + 1) * CHUNK)
        vc = vn[rows, :]
        vbd = jnp.concatenate([vc * mv_ref[g:g + 1, :].astype(BF16) for g in range(N_GROUPS)], axis=0)
        mixed = jnp.dot(wscat_ref[...], vbd, preferred_element_type=F32) + bsmat_ref[...]
        ysg_scr[rows, :] = s_u[rows, :] * mixed

        qc = rq[rows, :]
        kc = rk[rows, :]
        kcb = kc.astype(BF16)
        vcb = r_ref[rows, 2 * w:3 * w]
        qstack = jnp.concatenate([(qc * mk_ref[h:h + 1, :]).astype(BF16) for h in range(N_GROUPS)], axis=0)
        sc = lax.dot_general(qstack, kcb, (((1,), (1,)), ((), ())), preferred_element_type=F32)
        sc = (sc * dec_ref[...]).astype(BF16)
        scat = jnp.concatenate([sc[h * CHUNK:(h + 1) * CHUNK, :] for h in range(N_GROUPS)], axis=1)
        vstack = jnp.concatenate([vcb * mv_ref[h:h + 1, :].astype(BF16) for h in range(N_GROUPS)], axis=0)
        o_c = jnp.dot(scat, vstack, preferred_element_type=F32)
        state = state_scr[...]
        o_c = o_c + jnp.dot((qc * qdec_ref[...]).astype(BF16), state.astype(BF16),
                            preferred_element_type=F32)
        kd_t = jnp.transpose(kc * kdec_ref[...]).astype(BF16)
        kv = jnp.dot(kd_t, vcb, preferred_element_type=F32)
        state_scr[...] = state * cdec_ref[...] + kv * bd_ref[...]
        yret_scr[rows, :] = o_c

    o_all = yret_scr[...]
    xc = o_all - _group_mean(o_all, gmat)
    var = _group_mean(xc * xc, gmat)
    r_g = r_ref[:, 3 * w:4 * w].astype(F32)
    y_ret = (r_g * _sigmoid(r_g)) * (xc * lax.rsqrt(var + EPS) * retg_ref[...])

    d = x_ref.shape[1]
    ys = (y_conv, ymla_ref[...], ysg_scr[...], y_ret)
    merged = None
    for n in range(N_BRANCH):
        gate = _sigmoid(gates_ref[:, n * d:(n + 1) * d])
        term = gate * jnp.dot(ys[n].astype(BF16), wb_ref[n], preferred_element_type=F32).astype(BF16)
        merged = term if merged is None else merged + term
    out = jnp.dot(merged, wo_ref[...], preferred_element_type=F32)
    o_ref[...] = x_ref[...] + mod_ref[2:3, :] * out


def _mixers(proj, ymla, x, cosr, sinr, mod, p, seq):
    n, d = x.shape
    tm = TM_MIX
    tpb = seq // tm

    def col(width, offset):
        return pl.BlockSpec((tm, width), lambda i: (i, offset // width))

    def full(a):
        return pl.BlockSpec(a.shape, lambda i: (0,) * a.ndim)

    consts = [p["conv_w"], p["gv_g"], p["ws_cat"], p["bs_mat"], p["ret_g"], p["dec"], p["kdec"],
              p["qdec"], p["cdec"], p["bd"], p["gmat"], p["mk"], p["mv"], p["w_branch"], p["w_o"]]
    return pl.pallas_call(
        functools.partial(_mix_kernel, tm=tm, tpb=tpb),
        out_shape=jax.ShapeDtypeStruct((n, d), F32),
        grid=(n // tm,),
        in_specs=[col(N_BRANCH * d, COL_GATES), col(4 * MIX_W, COL_A), col(4 * MIX_W, COL_R),
                  col(2 * MIX_W, COL_SU),
                  pl.BlockSpec((tm, MIX_W), lambda i: (i, 0)),
                  pl.BlockSpec((tm, d), lambda i: (i, 0)),
                  pl.BlockSpec((tm, LANES), lambda i: (i, 0)),
                  pl.BlockSpec((tm, LANES), lambda i: (i, 0)),
                  pl.BlockSpec((None, 6, d), lambda i: (i // tpb, 0, 0))]
                 + [full(c) for c in consts],
        out_specs=pl.BlockSpec((tm, d), lambda i: (i, 0)),
        scratch_shapes=[pltpu.VMEM((8, MIX_W), F32), pltpu.VMEM((MIX_W, MIX_W), F32),
                        pltpu.VMEM((tm, MIX_W), F32), pltpu.VMEM((tm, MIX_W), F32)],
        compiler_params=_cparams("arbitrary"),
        name="mixers_merge",
    )(proj, proj, proj, proj, ymla, x, cosr, sinr, mod, *consts)


def _ffn_kernel(x_ref, mod_ref, g_ref, w1_ref, w3_ref, w2_ref, o_ref):
    x = x_ref[...]
    h = _norm_mod(x, g_ref[...], mod_ref[3:4, :], mod_ref[4:5, :]).astype(BF16)
    acc = None
    for c0 in range(0, w1_ref.shape[1], FF_CHUNK):
        cols = slice(c0, c0 + FF_CHUNK)
        a = jnp.dot(h, w1_ref[:, cols], preferred_element_type=F32)
        b = jnp.dot(h, w3_ref[:, cols], preferred_element_type=F32)
        hid = ((a * _sigmoid(a)) * b).astype(BF16)
        part = jnp.dot(hid, w2_ref[cols, :], preferred_element_type=F32)
        acc = part if acc is None else acc + part
    o_ref[...] = x + mod_ref[5:6, :] * acc


def _dense_ffn(x, mod, g, w1, w3, w2, seq):
    n, d = x.shape
    dff = w1.shape[1]
    tm = TM_FFN
    tpb = seq // tm
    return pl.pallas_call(
        _ffn_kernel,
        out_shape=jax.ShapeDtypeStruct((n, d), F32),
        grid=(n // tm,),
        in_specs=[
            pl.BlockSpec((tm, d), lambda i: (i, 0)),
            pl.BlockSpec((None, 6, d), lambda i: (i // tpb, 0, 0)),
            pl.BlockSpec((1, d), lambda i: (0, 0)),
            _resident((d, dff)), _resident((d, dff)), _resident((dff, d)),
        ],
        out_specs=pl.BlockSpec((tm, d), lambda i: (i, 0)),
        compiler_params=_cparams("parallel"),
        name="dense_swiglu",
    )(x, mod, g, w1, w3, w2)


def _router_kernel(x_ref, mod_ref, g_ref, rw_ref, rb_ref, hs_ref, ei_ref, pw_ref, meta_ref, tot_ref,
                   carry_scr, *, tm, srows):
    i = pl.program_id(0)

    @pl.when(i == 0)
    def _():
        carry_scr[...] = jnp.zeros(carry_scr.shape, F32)

    h = _norm_mod(x_ref[...], g_ref[...], mod_ref[3:4, :], mod_ref[4:5, :])

    h_hi = h.astype(BF16)
    h_lo = (h - h_hi.astype(F32)).astype(BF16)
    hw = jnp.dot(h_hi, rw_ref[...], preferred_element_type=F32)
    logits = (hw[:, :LANES] + hw[:, LANES:] + jnp.dot(h_lo, rw_ref[:, :LANES], preferred_element_type=F32)
              + rb_ref[...])
    mx = jnp.max(logits, axis=-1, keepdims=True)
    ex = jnp.exp(logits - mx)
    probs = ex / jnp.sum(ex, axis=-1, keepdims=True)
    lane = lax.broadcasted_iota(jnp.int32, (tm, LANES), 1)
    valid = lane < N_EXPERTS
    probs = jnp.where(valid, probs, -1.0)
    lane_f = lane.astype(F32)
    m1 = jnp.max(probs, axis=-1, keepdims=True)
    i1 = jnp.min(jnp.where(probs == m1, lane_f, float(LANES)), axis=-1, keepdims=True)
    rest = jnp.where(lane_f == i1, -1.0, probs)
    m2 = jnp.max(rest, axis=-1, keepdims=True)
    i2 = jnp.min(jnp.where(rest == m2, lane_f, float(LANES)), axis=-1, keepdims=True)
    den = m1 + m2
    pw_ref[...] = jnp.where(lane == 0, m1 / den, jnp.where(lane == 1, m2 / den, 0.0))

    sel1 = lane_f == i1
    sel2 = lane_f == i2
    onehot = jnp.where(sel1, 1.0, 0.0) + jnp.where(sel2, 1.0, 0.0)
    r_i = lax.broadcasted_iota(jnp.int32, (tm, tm), 0)
    c_i = lax.broadcasted_iota(jnp.int32, (tm, tm), 1)
    tri = jnp.where(c_i < r_i, 1.0, 0.0).astype(BF16)
    before = jnp.dot(tri, onehot.astype(BF16), preferred_element_type=F32)
    cnt = jnp.sum(onehot, axis=0, keepdims=True)
    cnt_al = jnp.floor((cnt + (ROW_ALIGN - 1)) * (1.0 / ROW_ALIGN)) * ROW_ALIGN
    e_r = lax.broadcasted_iota(jnp.int32, (LANES, LANES), 0)
    e_c = lax.broadcasted_iota(jnp.int32, (LANES, LANES), 1)
    upper = jnp.where(e_r < e_c, 1.0, 0.0)
    loff = jnp.dot(jnp.broadcast_to(cnt_al, (8, LANES)), upper, precision=HIGHEST,
                   preferred_element_type=F32)[0:1, :]
    slot = loff + before
    slot1 = jnp.sum(jnp.where(sel1, slot, 0.0), axis=-1, keepdims=True).astype(jnp.int32)
    slot2 = jnp.sum(jnp.where(sel2, slot, 0.0), axis=-1, keepdims=True).astype(jnp.int32)
    ei = jnp.where(lane == 0, i1, jnp.where(lane == 1, i2, 0.0)).astype(jnp.int32)
    ei_ref[...] = jnp.where(lane == 2, slot1, jnp.where(lane == 3, slot2, ei))

    r_idx = lax.broadcasted_iota(jnp.int32, (tm, srows), 1)
    place = jnp.where(r_idx == slot1, 1.0, jnp.where(r_idx == slot2, 1.0, 0.0)).astype(BF16)
    hs = lax.dot_general(place, h.astype(BF16), (((0,), (0,)), ((), ())), preferred_element_type=F32)
    half = hs.shape[1] // 2
    hs_ref[...] = _pack_bf16_pair(hs[:, :half], hs[:, half:])

    carry = carry_scr[0:1, :]
    mrow = lax.broadcasted_iota(jnp.int32, (8, LANES), 0)
    meta = jnp.where(mrow == 0, cnt_al, jnp.where(mrow == 1, carry, jnp.where(mrow == 2, loff, 0.0)))
    meta_ref[...] = meta.astype(jnp.int32)
    carry_scr[0:1, :] = carry + cnt_al
    tot_ref[...] = jnp.broadcast_to(carry + cnt_al, tot_ref.shape).astype(jnp.int32)


def _router(x, mod, g, rw_pad, rb_pad, seq):
    n, d = x.shape
    tm = TM_ROUTE
    tpb = seq // tm
    nt = n // tm
    return pl.pallas_call(
        functools.partial(_router_kernel, tm=tm, srows=SORT_ROWS),
        out_shape=(jax.ShapeDtypeStruct((nt * SORT_ROWS, d // 2), jnp.uint32),
                   jax.ShapeDtypeStruct((n, LANES), jnp.int32),
                   jax.ShapeDtypeStruct((n, LANES), F32),
                   jax.ShapeDtypeStruct((nt, 8, LANES), jnp.int32),
                   jax.ShapeDtypeStruct((8, LANES), jnp.int32)),
        grid=(nt,),
        in_specs=[
            pl.BlockSpec((tm, d), lambda i: (i, 0)),
            pl.BlockSpec((None, 6, d), lambda i: (i // tpb, 0, 0)),
            pl.BlockSpec((1, d), lambda i: (0, 0)),
            pl.BlockSpec((d, 2 * LANES), lambda i: (0, 0)),
            pl.BlockSpec((1, LANES), lambda i: (0, 0)),
        ],
        out_specs=(pl.BlockSpec((SORT_ROWS, d // 2), lambda i: (i, 0)),
                   pl.BlockSpec((tm, LANES), lambda i: (i, 0)),
                   pl.BlockSpec((tm, LANES), lambda i: (i, 0)),
                   pl.BlockSpec((None, 8, LANES), lambda i: (i, 0, 0)),
                   pl.BlockSpec((8, LANES), lambda i: (0, 0))),
        scratch_shapes=[pltpu.VMEM((8, LANES), F32)],
        compiler_params=_cparams("arbitrary"),
        name="router_top2",
    )(x, mod, g, rw_pad, rb_pad)


def _segment_copy(src_hbm, dst_hbm, src_row, dst_row, n_rows, sem):
    src_row = pl.multiple_of(src_row, ROW_ALIGN)
    dst_row = pl.multiple_of(dst_row, ROW_ALIGN)
    n_rows = pl.multiple_of(n_rows, ROW_ALIGN)
    return pltpu.make_async_copy(src_hbm.at[pl.ds(src_row, n_rows)], dst_hbm.at[pl.ds(dst_row, n_rows)], sem)


def _expert_kernel(te_ref, used_ref, toff_ref, ilo_ref, ihi_ref, cnt_ref, carry_ref, loff_ref,
                   hs_hbm, w1_hbm, w3_hbm, w2_hbm, y_ref, wb1, wb3, wb2, stage, sem, xbuf, xsem, rows_smem,
                   *, srows):
    t = pl.program_id(0)
    e = te_ref[t]
    tg = xbuf.shape[1]
    slot = t % 2
    first_of_expert = jnp.logical_or(t == 0, e != te_ref[jnp.maximum(t - 1, 0)])

    def fetch(tile, into):
        xbuf[into] = jnp.zeros(xbuf.shape[1:], xbuf.dtype)
        expert = te_ref[tile]
        first_row = toff_ref[tile]

        def piece(i, total):
            s = i * N_EXPERTS + expert
            run_start = carry_ref[s]
            lo = jnp.maximum(run_start, first_row)
            hi = jnp.minimum(run_start + cnt_ref[s], first_row + tg)

            @pl.when(hi > lo)
            def _():
                _segment_copy(hs_hbm, xbuf.at[into], i * srows + loff_ref[s] + (lo - run_start), lo - first_row,
                              hi - lo, xsem.at[into]).start()

            return total + jnp.maximum(hi - lo, 0)

        rows_smem[into] = lax.fori_loop(ilo_ref[tile], ihi_ref[tile], piece, 0)

    @pl.when(t == 0)
    def _():
        fetch(t, slot)

    @pl.when(t + 1 < pl.num_programs(0))
    def _():
        fetch(t + 1, 1 - slot)

    @pl.when(rows_smem[slot] > 0)
    def _():
        _segment_copy(hs_hbm, xbuf.at[slot], 0, 0, rows_smem[slot], xsem.at[slot]).wait()

    def rows_bf16():
        lo, hi = _unpack_bf16_pair(xbuf[slot])
        return jnp.concatenate([lo.astype(BF16), hi.astype(BF16)], axis=1)

    def swiglu(h, cols):
        a = jnp.dot(h, wb1[:, cols], preferred_element_type=F32)
        b = jnp.dot(h, wb3[:, cols], preferred_element_type=F32)
        hid = ((a * _sigmoid(a)) * b).astype(BF16)
        return jnp.dot(hid, wb2[cols, :], preferred_element_type=F32)

    def emit(acc):
        half = acc.shape[1] // 2
        y_ref[...] = _pack_bf16_pair(acc[:, :half], acc[:, half:])

    @pl.when(jnp.logical_and(used_ref[t] == 1, first_of_expert))
    def _():
        n_slots = stage.shape[0]
        dff, d = wb1.shape[1], wb1.shape[0]
        windows = [(src, dst, r, c)
                   for ch in range(dff // W_CHUNK)
                   for src, dst, r_list, c_list in ((w1_hbm, wb1, range(d // W_CHUNK), [ch]),
                                                    (w3_hbm, wb3, range(d // W_CHUNK), [ch]),
                                                    (w2_hbm, wb2, [ch], range(d // W_CHUNK)))
                   for r in r_list for c in c_list]
        per_chunk = len(windows) // (dff // W_CHUNK)

        def staged_copy(k):
            src, _, r, c = windows[k]
            return pltpu.make_async_copy(src.at[e, pl.ds(r * W_CHUNK, W_CHUNK), pl.ds(c * W_CHUNK, W_CHUNK)],
                                         stage.at[k % n_slots], sem.at[k % n_slots])

        for k in range(n_slots - 1):
            staged_copy(k).start()
        h = rows_bf16()
        acc = None
        for k, (_, dst, r, c) in enumerate(windows):
            staged_copy(k).wait()
            dst[r * W_CHUNK:(r + 1) * W_CHUNK, c * W_CHUNK:(c + 1) * W_CHUNK] = stage[k % n_slots].astype(BF16)
            if k + n_slots - 1 < len(windows):
                staged_copy(k + n_slots - 1).start()
            if (k + 1) % per_chunk == 0:
                ch = k // per_chunk
                part = swiglu(h, slice(ch * W_CHUNK, (ch + 1) * W_CHUNK))
                acc = part if acc is None else acc + part
        emit(acc)

    @pl.when(jnp.logical_and(used_ref[t] == 1, jnp.logical_not(first_of_expert)))
    def _():
        h = rows_bf16()
        acc = None
        for ch in range(wb1.shape[1] // W_CHUNK):
            part = swiglu(h, slice(ch * W_CHUNK, (ch + 1) * W_CHUNK))
            acc = part if acc is None else acc + part
        emit(acc)

    @pl.when(used_ref[t] == 0)
    def _():
        y_ref[...] = jnp.zeros(y_ref.shape, y_ref.dtype)


def _expert_ffn(tile_tables, seg_tables, hs, w1, w3, w2, n_tiles):
    half = hs.shape[1]
    d = 2 * half
    dff = w1.shape[2]
    tg = TG_MOE
    assert dff % W_CHUNK == 0 and d % W_CHUNK == 0
    hbm = pl.BlockSpec(memory_space=pl.ANY)
    tables = tuple(tile_tables) + tuple(seg_tables)
    return pl.pallas_call(
        functools.partial(_expert_kernel, srows=SORT_ROWS),
        out_shape=jax.ShapeDtypeStruct((n_tiles * tg, half), jnp.uint32),
        grid_spec=pltpu.PrefetchScalarGridSpec(
            num_scalar_prefetch=len(tables),
            grid=(n_tiles,),
            in_specs=[hbm, hbm, hbm, hbm],
            out_specs=pl.BlockSpec((tg, half), lambda t, *_: (t, 0)),
            scratch_shapes=[pltpu.VMEM((d, dff), BF16), pltpu.VMEM((d, dff), BF16), pltpu.VMEM((dff, d), BF16),
                            pltpu.VMEM((W_SLOTS, W_CHUNK, W_CHUNK), F32), pltpu.SemaphoreType.DMA((W_SLOTS,)),
                            pltpu.VMEM((2, tg, half), jnp.uint32), pltpu.SemaphoreType.DMA((2,)),
                            pltpu.SMEM((2,), jnp.int32)],
        ),
        compiler_params=_cparams("arbitrary"),
        name="expert_swiglu",
    )(*tables, hs, w1, w3, w2)


def _combine_kernel(src_ref, loff_ref, cnt_ref, rows_ref, x_ref, ei_ref, pw_ref, mod_ref, y_ref, o_ref,
                    ybuf, sem, *, tm, srows):
    i = pl.program_id(0)
    slot = i % 2

    def fetch(tile, into):
        ybuf[into] = jnp.zeros(ybuf.shape[1:], ybuf.dtype)
        for e in range(N_EXPERTS):
            s = tile * N_EXPERTS + e
            n_rows = cnt_ref[s]

            @pl.when(n_rows > 0)
            def _():
                _segment_copy(y_ref, ybuf.at[into], src_ref[s], loff_ref[s], n_rows, sem.at[into]).start()

    @pl.when(i == 0)
    def _():
        fetch(i, slot)

    @pl.when(i + 1 < pl.num_programs(0))
    def _():
        fetch(i + 1, 1 - slot)

    @pl.when(rows_ref[i] > 0)
    def _():
        _segment_copy(y_ref, ybuf.at[slot], 0, 0, rows_ref[i], sem.at[slot]).wait()

    lo, hi = _unpack_bf16_pair(ybuf[slot])
    ys = jnp.concatenate([lo.astype(BF16), hi.astype(BF16)], axis=1)
    r_idx = lax.broadcasted_iota(jnp.int32, (tm, srows), 1)
    mix = jnp.zeros(x_ref.shape, F32)
    for k in range(TOP_K):
        pick = jnp.where(r_idx == ei_ref[:, TOP_K + k:TOP_K + k + 1], 1.0, 0.0).astype(BF16)
        mix = mix + pw_ref[:, k:k + 1] * jnp.dot(pick, ys, preferred_element_type=F32)
    o_ref[...] = x_ref[...] + mod_ref[5:6, :] * mix


def _combine(seg_src, seg_loff, seg_cnt, tile_rows, x, ei, pw, mod, y, seq):
    n, d = x.shape
    tm = TM_ROUTE
    tpb = seq // tm
    tok = lambda width: pl.BlockSpec((tm, width), lambda i, *_: (i, 0))
    return pl.pallas_call(
        functools.partial(_combine_kernel, tm=tm, srows=SORT_ROWS),
        out_shape=jax.ShapeDtypeStruct((n, d), F32),
        grid_spec=pltpu.PrefetchScalarGridSpec(
            num_scalar_prefetch=4,
            grid=(n // tm,),
            in_specs=[tok(d), tok(LANES), tok(LANES),
                      pl.BlockSpec((None, 6, d), lambda i, *_: (i // tpb, 0, 0)),
                      pl.BlockSpec(memory_space=pl.ANY)],
            out_specs=tok(d),
            scratch_shapes=[pltpu.VMEM((2, SORT_ROWS, d // 2), jnp.uint32), pltpu.SemaphoreType.DMA((2,))],
        ),
        compiler_params=_cparams("arbitrary"),
        name="moe_combine",
    )(seg_src, seg_loff, seg_cnt, tile_rows, x, ei, pw, mod, y)


def _pack_w_in(w_in_all, layer):
    d = w_in_all.shape[1]
    col = lambda start, width: w_in_all[layer, :, start:start + width]
    w = MIX_W
    o_ckv = 3 * w + Q_LORA
    o_kr = o_ckv + KV_LORA
    o_su = o_kr + QK_ROPE
    o_rq = o_su + 2 * w
    o_gate = o_rq + 4 * w
    half = HEAD_DIM // 2
    perm = np.array([h * HEAD_DIM + part * half + i
                     for part in range(2) for h in range(N_GROUPS) for i in range(half)])
    z = lambda k: jnp.zeros((d, k), w_in_all.dtype)
    cols = [
        col(o_gate, N_BRANCH * d),
        col(0, 3 * w + Q_LORA),
        col(o_rq, w)[:, perm], col(o_rq + w, w)[:, perm],
        col(o_rq + 2 * w, 2 * w),
        col(o_su, 2 * w),
        col(o_ckv, KV_LORA),
        z(QK_NOPE), col(o_kr, QK_ROPE), z(LANES - QK_HEAD),
    ]
    return jnp.concatenate(cols, axis=1).astype(BF16)


def _swap_rope_halves(a):
    hr = QK_ROPE // 2
    return jnp.concatenate([a[..., :QK_NOPE], a[..., QK_NOPE + hr:QK_HEAD], a[..., QK_NOPE:QK_NOPE + hr],
                            a[..., QK_HEAD:]], axis=-1)


def _mla_params(cq_g, w_uq, ckv_g, w_ukv, qn_g, kn_g):
    pad = LANES - QK_HEAD
    wq = w_uq.reshape(Q_LORA, N_GROUPS, QK_HEAD)
    wq = jnp.pad(wq, ((0, 0), (0, 0), (0, pad)))
    wkv = w_ukv.reshape(KV_LORA, N_GROUPS, QK_NOPE + V_HEAD)
    wk = jnp.pad(wkv[:, :, :QK_NOPE], ((0, 0), (0, 0), (0, LANES - QK_NOPE)))
    wv = jnp.pad(wkv[:, :, QK_NOPE:], ((0, 0), (0, 0), (0, LANES - V_HEAD)))
    qg = jnp.pad(qn_g, (0, pad))[None, :]
    kg = jnp.pad(kn_g, (0, pad))[None, :]
    bound = (QK_HEAD ** 0.5 * LOG2_E) * jnp.max(jnp.abs(qn_g)) * jnp.max(jnp.abs(kn_g))
    static_shift = bound <= MAX_STATIC_SHIFT
    lane = jnp.arange(LANES)
    qaug = (lane == QK_HEAD).astype(F32)[None, :]
    kaug = qaug * jnp.where(static_shift, -bound, 0.0)
    vaug = jnp.tile((lane == V_HEAD).astype(F32), N_GROUPS)[None, :]
    params = {
        "cq_g": cq_g[None, :], "ckv_g": ckv_g[None, :],
        "wqa": wq.reshape(Q_LORA, -1).astype(BF16),
        "wqb": _swap_rope_halves(wq).reshape(Q_LORA, -1).astype(BF16),
        "wk": wk.reshape(KV_LORA, -1).astype(BF16),
        "wv": wv.reshape(KV_LORA, -1).astype(BF16),
        "qga": qg, "qgb": _swap_rope_halves(qg), "kga": kg, "kgb": _swap_rope_halves(kg),
        "qaug": qaug, "kaug": kaug, "vaug": vaug,
        "swap": (_swap_rope_halves(lane[None, :])[0][None, :] == lane[:, None]).astype(BF16),
    }
    return params, static_shift


def _mixer_consts():
    f32 = np.float32
    h = np.arange(N_GROUPS, dtype=f32)
    log_gamma = np.log1p(-(f32(2.0) ** (f32(-5.0) - h))).astype(f32)
    pos = np.arange(CHUNK, dtype=f32)
    rel = pos[:, None] - pos[None, :]
    dec = np.where(rel >= 0, np.exp(log_gamma[:, None, None] * np.maximum(rel, f32(0.0))), f32(0.0)).astype(f32)
    lane = np.arange(MIX_W)
    head_k = (lane % LANES) // (HEAD_DIM // 2)
    head_v = lane // HEAD_DIM
    lg_k = log_gamma[head_k]
    return {
        "dec": jnp.asarray(dec.reshape(N_GROUPS * CHUNK, CHUNK)),
        "kdec": jnp.asarray(np.exp(lg_k[None, :] * (CHUNK - 1.0 - pos)[:, None]).astype(f32)),
        "qdec": jnp.asarray(np.exp(lg_k[None, :] * (pos + 1.0)[:, None]).astype(f32)),
        "cdec": jnp.asarray(np.broadcast_to(np.exp(lg_k * f32(CHUNK)).astype(f32)[:, None], (MIX_W, MIX_W))),
        "bd": jnp.asarray((head_k[:, None] == head_v[None, :]).astype(f32)),
        "gmat": jnp.asarray((head_v[:, None] == head_v[None, :]).astype(f32) / HEAD_DIM).astype(BF16),
        "mk": jnp.asarray((head_k[None, :] == np.arange(N_GROUPS)[:, None]).astype(f32)),
        "mv": jnp.asarray((head_v[None, :] == np.arange(N_GROUPS)[:, None]).astype(f32)),
    }


def _mixer_params(consts, conv_w, gv_g, w_s, b_s, ret_g, w_branch, w_o):
    p = dict(consts)
    ws = jnp.tril(w_s)
    p.update({
        "conv_w": conv_w,
        "gv_g": gv_g.reshape(1, MIX_W),
        "ws_cat": jnp.transpose(ws, (1, 0, 2)).reshape(CHUNK, N_GROUPS * CHUNK).astype(BF16),
        "bs_mat": jnp.repeat(b_s.T, HEAD_DIM, axis=1),
        "ret_g": ret_g.reshape(1, MIX_W),
        "w_branch": w_branch.astype(BF16),
        "w_o": w_o.astype(BF16),
    })
    return p


def _moe_layout(meta, tot, n_tiles):
    totals = tot[0, :N_EXPERTS]
    padded = ((totals + TG_MOE - 1) // TG_MOE) * TG_MOE
    ends = jnp.cumsum(padded)
    starts = ends - padded
    seg_cnt = meta[:, 0, :N_EXPERTS]
    seg_carry = meta[:, 1, :N_EXPERTS]
    seg_loff = meta[:, 2, :N_EXPERTS]
    seg_grouped = starts[None, :] + seg_carry
    tile_start = jnp.arange(n_tiles, dtype=jnp.int32) * TG_MOE
    tile_e = jnp.sum((tile_start[:, None] >= ends[None, :]).astype(jnp.int32), axis=1)
    used = (tile_start < ends[-1]).astype(jnp.int32)
    last_e = jnp.sum((ends[-1] - 1 >= ends).astype(jnp.int32))
    tile_e = jnp.minimum(jnp.where(used == 1, tile_e, last_e), N_EXPERTS - 1)
    tile_off = tile_start - starts[tile_e]
    run_start = seg_carry[:, tile_e]
    run_end = run_start + seg_cnt[:, tile_e]
    tile_ilo = jnp.sum((run_end <= tile_off[None, :]).astype(jnp.int32), axis=0) * used
    tile_ihi = jnp.sum((run_start < tile_off[None, :] + TG_MOE).astype(jnp.int32), axis=0) * used
    flat = lambda a: a.reshape(-1).astype(jnp.int32)
    tile_tables = (flat(tile_e), flat(used), flat(tile_off * used), flat(tile_ilo), flat(tile_ihi))
    seg_tables = (flat(seg_cnt), flat(seg_carry), flat(seg_loff))
    return tile_tables, seg_tables, flat(seg_grouped)


def kernel(x, c, positions, norm1_g, norm2_g, ada_w, ada_b, w_in, conv_w, cq_g, w_uq, ckv_g, w_ukv, qn_g, kn_g, gv_g, w_s, b_s, ret_g, w_branch, w_o, ffn_w1, ffn_w3, ffn_w2, router_w, router_b, moe_w1, moe_w3, moe_w2):
    batch, seq, d = x.shape
    depth = ada_w.shape[0]
    n = batch * seq
    assert seq % max(TM_PROJ, TQ_ATT, TM_MIX, TM_FFN, TM_ROUTE) == 0
    assert d // 2 % LANES == 0

    c_t = jnp.pad(c, ((0, 8 - batch), (0, 0))).T
    ada = _ada(c_t, ada_w, ada_b, batch)[:, :batch].reshape(depth, batch, 6, d)
    cosr, sinr, cm, sm = _rope_tables(positions.astype(F32).reshape(n, 1))

    mixer_consts = _mixer_consts()
    xt = x.reshape(n, d)
    for l in range(depth):
        mod = ada[l]
        mla_p, static_shift = _mla_params(cq_g[l], w_uq[l], ckv_g[l], w_ukv[l], qn_g[l], kn_g[l])
        proj, q, k, v = _inproj(xt, mod, norm1_g[l][None, :], _pack_w_in(w_in, l), cm, sm, mla_p, seq)
        y_mla = lax.cond(static_shift,
                         functools.partial(_flash, batch=batch, seq=seq, online_max=False),
                         functools.partial(_flash, batch=batch, seq=seq, online_max=True), q, k, v)
        mp = _mixer_params(mixer_consts, conv_w[l], gv_g[l], w_s[l], b_s[l], ret_g[l], w_branch[l], w_o[l])
        xt = _mixers(proj, y_mla, xt, cosr, sinr, mod, mp, seq)
        g2n = norm2_g[l][None, :]
        if l % 2 == 0:
            i = l // 2
            xt = _dense_ffn(xt, mod, g2n, ffn_w1[i].astype(BF16), ffn_w3[i].astype(BF16),
                            ffn_w2[i].astype(BF16), seq)
        else:
            i = l // 2
            rw = jnp.pad(router_w[i], ((0, 0), (0, LANES - N_EXPERTS)))
            rw_hi = rw.astype(BF16)
            rw_pad = jnp.concatenate([rw_hi, (rw - rw_hi.astype(F32)).astype(BF16)], axis=1)
            rb_pad = jnp.pad(router_b[i], (0, LANES - N_EXPERTS), constant_values=-1e30)[None, :]
            hs, ei, pw, meta, tot = _router(xt, mod, g2n, rw_pad, rb_pad, seq)
            max_rows = n * TOP_K + N_EXPERTS * (n // TM_ROUTE) * (ROW_ALIGN - 1)
            n_tiles = -(-max_rows // TG_MOE) + N_EXPERTS
            tile_tables, seg_tables, seg_grouped = _moe_layout(meta, tot, n_tiles)
            seg_cnt, _, seg_loff = seg_tables
            tile_rows = jnp.sum(seg_cnt.reshape(-1, N_EXPERTS), axis=1)
            y = _expert_ffn(tile_tables, seg_tables, hs, moe_w1[i], moe_w3[i], moe_w2[i], n_tiles)
            xt = _combine(seg_grouped, seg_loff, seg_cnt, tile_rows, xt, ei, pw, mod, y, seq)
    return xt.reshape(batch, seq, d)
```

```python
import functools

import jax
import jax.numpy as jnp
import numpy as np
from jax import lax
from jax.experimental import pallas as pl
from jax.experimental.pallas import tpu as pltpu

F32 = jnp.float32
BF16 = jnp.bfloat16
HIGHEST = lax.Precision.HIGHEST

HEAD_DIM = 64
N_GROUPS = 4
MIX_W = N_GROUPS * HEAD_DIM
N_BRANCH = 4
CONV_W = 3
Q_LORA = 256
KV_LORA = 128
QK_NOPE = 64
QK_ROPE = 32
QK_HEAD = QK_NOPE + QK_ROPE
V_HEAD = 64
CHUNK = 128
N_EXPERTS = 8
TOP_K = 2
ROPE_THETA = 10000.0
EPS = 1e-6
LOG2_E = 1.4426950408889634
MAX_STATIC_SHIFT = 50.0

LANES = 128
VMEM_LIMIT_BYTES = 56 * 1024 * 1024

COL_GATES = 0
COL_A = 4096
COL_CQ = COL_A + 3 * MIX_W
COL_R = 5120
COL_SU = 6144
COL_CKV = 6656
COL_KRA = 6784
N_IN = 6912

TM_PROJ = 512
TN_PROJ = 768
TQ_ATT = 1024
TM_MIX = 512
TM_FFN = 1024
FF_CHUNK = 256
TM_ROUTE = 512
ROW_ALIGN = 8
SORT_ROWS = TOP_K * TM_ROUTE + N_EXPERTS * ROW_ALIGN
TG_MOE = 512
W_CHUNK = 512
W_SLOTS = 16


def _cparams(*sem):
    return pltpu.CompilerParams(dimension_semantics=sem, vmem_limit_bytes=VMEM_LIMIT_BYTES)


def _sigmoid(x):
    return jnp.tanh(x * 0.5) * 0.5 + 0.5


def _group_mean(x, gmat_bf16):
    hi = x.astype(BF16)
    lo = (x - hi.astype(F32)).astype(BF16)
    return (jnp.dot(hi, gmat_bf16, preferred_element_type=F32)
            + jnp.dot(lo, gmat_bf16, preferred_element_type=F32))


def _pack_bf16_pair(lo, hi):
    lo_bits = lax.bitcast_convert_type(lo.astype(BF16).astype(F32), jnp.uint32)
    hi_bits = lax.bitcast_convert_type(hi.astype(BF16).astype(F32), jnp.uint32)
    return (lo_bits >> 16) | (hi_bits & jnp.uint32(0xFFFF0000))


def _unpack_bf16_pair(p):
    lo = lax.bitcast_convert_type(p << 16, F32)
    hi = lax.bitcast_convert_type(p & jnp.uint32(0xFFFF0000), F32)
    return lo, hi


def _norm_mod(x, g, shift, scale):
    y = x * lax.rsqrt(jnp.mean(x * x, axis=-1, keepdims=True) + EPS)
    return (y * g) * (1.0 + scale) + shift


def _ada_kernel(ct_ref, w_ref, b_ref, o_ref, *, batch):
    ct = ct_ref[...]
    cond = ct * _sigmoid(ct)
    w = w_ref[...]
    o_ref[...] = jnp.zeros(o_ref.shape, F32)
    for b in range(batch):
        o_ref[b:b + 1, :] = jnp.sum(w * cond[:, b:b + 1], axis=0, keepdims=True) + b_ref[...]


def _ada(c_t, ada_w, ada_b, batch):
    n_layer, d, d6 = ada_w.shape
    rows = c_t.shape[1]
    tn = 2048
    return pl.pallas_call(
        functools.partial(_ada_kernel, batch=batch),
        out_shape=jax.ShapeDtypeStruct((n_layer, rows, d6), F32),
        grid=(n_layer, d6 // tn),
        in_specs=[
            pl.BlockSpec((d, rows), lambda l, j: (0, 0)),
            pl.BlockSpec((None, d, tn), lambda l, j: (l, 0, j)),
            pl.BlockSpec((None, 1, tn), lambda l, j: (l, 0, j)),
        ],
        out_specs=pl.BlockSpec((None, rows, tn), lambda l, j: (l, 0, j)),
        compiler_params=_cparams("parallel", "parallel"),
        name="ada_mod",
    )(c_t, ada_w, ada_b.reshape(n_layer, 1, d6))


def _rope_kernel(pos_ref, inv_ref, cr_ref, sr_ref, cm_ref, sm_ref):
    half_r = HEAD_DIM // 2
    half_m = QK_ROPE // 2
    tm = pos_ref.shape[0]
    low = lax.broadcasted_iota(jnp.int32, (tm // 2, LANES), 1) < LANES // 2
    ang = jnp.where(low, pos_ref[0:tm // 2, :], pos_ref[tm // 2:tm, :]) * inv_ref[...]
    c = jnp.cos(ang)
    s = jnp.sin(ang)
    c = jnp.concatenate([c, pltpu.roll(c, LANES // 2, axis=1)], axis=0)
    s = jnp.concatenate([s, pltpu.roll(s, LANES // 2, axis=1)], axis=0)
    lane = lax.broadcasted_iota(jnp.int32, c.shape, 1)

    def tile_r(t):
        t = jnp.where(lane < half_r, t, 0.0)
        out = t
        for k in range(1, LANES // half_r):
            out = out + pltpu.roll(t, k * half_r, axis=1)
        return out

    cr_ref[...] = tile_r(c)
    sr_ref[...] = tile_r(s)
    first = jnp.logical_and(lane >= QK_NOPE, lane < QK_NOPE + half_m)
    second = jnp.logical_and(lane >= QK_NOPE + half_m, lane < QK_HEAD)
    c1, c2 = pltpu.roll(c, QK_NOPE - half_r, axis=1), pltpu.roll(c, QK_NOPE + half_m - half_r, axis=1)
    s1, s2 = pltpu.roll(s, QK_NOPE - half_r, axis=1), pltpu.roll(s, QK_NOPE + half_m - half_r, axis=1)
    cm_ref[...] = jnp.where(first, c1, jnp.where(second, c2, 1.0))
    sm_ref[...] = jnp.where(first, -s1, jnp.where(second, s2, 0.0))


def _rope_tables(pos_f):
    n = pos_f.shape[0]
    tm = 1024
    half_r = HEAD_DIM // 2
    half_m = QK_ROPE // 2
    inv_r = ROPE_THETA ** (-jnp.arange(half_r, dtype=F32) / half_r)
    inv_m = ROPE_THETA ** (-jnp.arange(half_m, dtype=F32) / half_m)
    inv = jnp.concatenate([inv_r, inv_m, jnp.zeros((LANES // 2 - half_r - half_m,), F32)])
    inv = jnp.tile(inv, 2)[None, :]
    tab = pl.BlockSpec((tm, LANES), lambda i: (i, 0))
    shape = jax.ShapeDtypeStruct((n, LANES), F32)
    return pl.pallas_call(
        _rope_kernel,
        out_shape=(shape, shape, shape, shape),
        grid=(n // tm,),
        in_specs=[pl.BlockSpec((tm, 1), lambda i: (i, 0)), pl.BlockSpec((1, LANES), lambda i: (0, 0))],
        out_specs=(tab, tab, tab, tab),
        compiler_params=_cparams("parallel"),
        name="rope_tables",
    )(pos_f, inv)


def _resident(shape):
    return pl.BlockSpec(shape, lambda *_: (0,) * len(shape), pipeline_mode=pl.Buffered(1))


def _mla_prep(cq_b, ckv_b, kra_b, cm, sm, cqg_ref, wqa_ref, wqb_ref, ckvg_ref, wk_ref, wv_ref, qga_ref, qgb_ref,
              kga_ref, kgb_ref, qaug_ref, kaug_ref, vaug_ref, swap_ref, q_ref, k_ref, v_ref):
    cq = cq_b.astype(F32)
    cqn = (cq * lax.rsqrt(jnp.mean(cq * cq, axis=-1, keepdims=True) + EPS) * cqg_ref[...]).astype(BF16)
    qa = jnp.dot(cqn, wqa_ref[...], preferred_element_type=F32)
    qb = jnp.dot(cqn, wqb_ref[...], preferred_element_type=F32)
    ckv = ckv_b.astype(F32)
    ckvn = (ckv * lax.rsqrt(jnp.mean(ckv * ckv, axis=-1, keepdims=True) + EPS) * ckvg_ref[...]).astype(BF16)
    ka = jnp.dot(ckvn, wk_ref[...], preferred_element_type=F32)
    v_ref[...] = (jnp.dot(ckvn, wv_ref[...], preferred_element_type=F32) + vaug_ref[...]).astype(BF16)
    kra = kra_b.astype(F32)
    krb = jnp.dot(kra_b, swap_ref[...], preferred_element_type=F32)
    scale = QK_HEAD ** -0.5 * LOG2_E
    q_cos, q_sin = cm * (qga_ref[...] * scale), sm * (qgb_ref[...] * scale)
    k_cos, k_sin = cm * kga_ref[...], sm * kgb_ref[...]
    for h in range(N_GROUPS):
        sl = slice(h * LANES, (h + 1) * LANES)
        qah, qbh = qa[:, sl], qb[:, sl]
        r = lax.rsqrt(jnp.sum(qah * qah, axis=-1, keepdims=True) * (1.0 / QK_HEAD) + EPS)
        q_ref[:, sl] = ((qah * q_cos + qbh * q_sin) * r + qaug_ref[...]).astype(BF16)
        kah = ka[:, sl] + kra
        kbh = ka[:, sl] + krb
        r = lax.rsqrt(jnp.sum(kah * kah, axis=-1, keepdims=True) * (1.0 / QK_HEAD) + EPS)
        k_ref[:, sl] = ((kah * k_cos + kbh * k_sin) * r + kaug_ref[...]).astype(BF16)


def _inproj_kernel(x_ref, mod_ref, g_ref, w_ref, cm_ref, sm_ref, *rest):
    mla_refs, (o_ref, q_ref, k_ref, v_ref) = rest[:-4], rest[-4:]
    h = _norm_mod(x_ref[...], g_ref[...], mod_ref[0:1, :], mod_ref[1:2, :]).astype(BF16)

    def chunk(c):
        cols = slice(c * TN_PROJ, (c + 1) * TN_PROJ)
        out = jnp.dot(h, w_ref[:, cols], preferred_element_type=F32).astype(BF16)
        o_ref[:, cols] = out
        return out

    c_q, c_kv = COL_CQ // TN_PROJ, COL_CKV // TN_PROJ
    lat_q = chunk(c_q)
    lat_kv = chunk(c_kv)
    q0, kv0, kr0 = COL_CQ - c_q * TN_PROJ, COL_CKV - c_kv * TN_PROJ, COL_KRA - c_kv * TN_PROJ
    _mla_prep(lat_q[:, q0:q0 + Q_LORA], lat_kv[:, kv0:kv0 + KV_LORA], lat_kv[:, kr0:kr0 + LANES],
              cm_ref[...], sm_ref[...], *mla_refs, q_ref, k_ref, v_ref)
    for c in range(N_IN // TN_PROJ):
        if c not in (c_q, c_kv):
            chunk(c)


def _inproj(x, mod, g, w, cm, sm, p, seq):
    n, d = x.shape
    tm = TM_PROJ
    tpb = seq // tm
    hw = N_GROUPS * LANES
    assert COL_KRA // TN_PROJ == COL_CKV // TN_PROJ and (COL_CQ + Q_LORA - 1) // TN_PROJ == COL_CQ // TN_PROJ

    def full(a):
        return pl.BlockSpec(a.shape, lambda i: (0,) * a.ndim)

    weights = [p["cq_g"], p["wqa"], p["wqb"], p["ckv_g"], p["wk"], p["wv"],
               p["qga"], p["qgb"], p["kga"], p["kgb"], p["qaug"], p["kaug"], p["vaug"], p["swap"]]
    table = pl.BlockSpec((tm, LANES), lambda i: (i, 0))
    head_tile = pl.BlockSpec((tm, hw), lambda i: (i, 0))
    heads = jax.ShapeDtypeStruct((n, hw), BF16)
    return pl.pallas_call(
        _inproj_kernel,
        out_shape=(jax.ShapeDtypeStruct((n, N_IN), BF16), heads, heads, heads),
        grid=(n // tm,),
        in_specs=[
            pl.BlockSpec((tm, d), lambda i: (i, 0)),
            pl.BlockSpec((None, 6, d), lambda i: (i // tpb, 0, 0)),
            pl.BlockSpec((1, d), lambda i: (0, 0)),
            _resident((d, N_IN)), table, table,
        ] + [full(a) for a in weights],
        out_specs=(pl.BlockSpec((tm, N_IN), lambda i: (i, 0)), head_tile, head_tile, head_tile),
        compiler_params=_cparams("parallel"),
        name="in_proj",
    )(x, mod, g, w, cm, sm, *weights)


def _flash_kernel(qi_ref, kj_ref, q_ref, k_ref, v_ref, o_ref, acc_scr, *rest, tq, online_max):
    i = qi_ref[pl.program_id(1)]
    j = kj_ref[pl.program_id(1)]

    @pl.when(j == 0)
    def _():
        acc_scr[...] = jnp.zeros(acc_scr.shape, F32)
        if online_max:
            rest[0][...] = jnp.full(rest[0].shape, -jnp.inf, F32)

    def block(q0, nq, nk, masked):
        rows = slice(q0, q0 + nq)
        if masked:
            row = lax.broadcasted_iota(jnp.int32, (nq, nk), 0) + q0
            col = lax.broadcasted_iota(jnp.int32, (nq, nk), 1)
            keep = col <= row
        for h in range(N_GROUPS):
            sl = slice(h * LANES, (h + 1) * LANES)
            s = lax.dot_general(q_ref[rows, sl], k_ref[0:nk, sl], (((1,), (1,)), ((), ())),
                                preferred_element_type=F32)
            if masked:
                s = jnp.where(keep, s, -jnp.inf)
            if online_max:
                m_scr = rest[0]
                m_prev = m_scr[h, rows]
                m_new = jnp.maximum(m_prev, jnp.max(s, axis=-1, keepdims=True))
                p = jnp.exp2(s - m_new).astype(BF16)
                acc_scr[h, rows] = jnp.exp2(m_prev - m_new) * acc_scr[h, rows] + jnp.dot(
                    p, v_ref[0:nk, sl], preferred_element_type=F32)
                m_scr[h, rows] = m_new
            else:
                acc_scr[h, rows] += jnp.dot(jnp.exp2(s).astype(BF16), v_ref[0:nk, sl],
                                            preferred_element_type=F32)

    @pl.when(j < i)
    def _():
        block(0, tq, tq, False)

    @pl.when(j == i)
    def _():
        block(0, tq // 2, tq // 2, True)
        block(tq // 2, tq // 2, tq, True)
        lane = lax.broadcasted_iota(jnp.int32, (tq, LANES), 1)
        for pr in range(N_GROUPS // 2):
            lo = acc_scr[2 * pr]
            hi = acc_scr[2 * pr + 1]
            lo = lo / lo[:, V_HEAD:V_HEAD + 1]
            hi = hi / hi[:, V_HEAD:V_HEAD + 1]
            both = jnp.where(lane < V_HEAD, lo, pltpu.roll(hi, V_HEAD, axis=1))
            o_ref[:, pr * LANES:(pr + 1) * LANES] = both.astype(BF16)


def _flash(q, k, v, batch, seq, online_max):
    n = q.shape[0]
    tq = TQ_ATT
    nq = seq // tq
    hw = N_GROUPS * LANES
    scratch = [pltpu.VMEM((N_GROUPS, tq, LANES), F32)]
    if online_max:
        scratch.append(pltpu.VMEM((N_GROUPS, tq, 1), F32))
    pairs = [(i, j) for i in range(nq) for j in range(i + 1)]
    qi = jnp.asarray([p[0] for p in pairs], jnp.int32)
    kj = jnp.asarray([p[1] for p in pairs], jnp.int32)
    q_tile = lambda b, s, qi, kj: (b * nq + qi[s], 0)
    k_tile = lambda b, s, qi, kj: (b * nq + kj[s], 0)
    return pl.pallas_call(
        functools.partial(_flash_kernel, tq=tq, online_max=online_max),
        out_shape=jax.ShapeDtypeStruct((n, MIX_W), BF16),
        grid_spec=pltpu.PrefetchScalarGridSpec(
            num_scalar_prefetch=2,
            grid=(batch, len(pairs)),
            in_specs=[pl.BlockSpec((tq, hw), q_tile), pl.BlockSpec((tq, hw), k_tile),
                      pl.BlockSpec((tq, hw), k_tile)],
            out_specs=pl.BlockSpec((tq, MIX_W), q_tile),
            scratch_shapes=scratch,
        ),
        compiler_params=_cparams("parallel", "arbitrary"),
        name="mla_flash_online" if online_max else "mla_flash",
    )(qi, kj, q, k, v)


def _gelu_tanh(x):
    return jax.nn.gelu(x, approximate=True)


def _mix_kernel(gates_ref, a_ref, r_ref, su_ref, ymla_ref, x_ref, cos_ref, sin_ref, mod_ref,
                convw_ref, gvg_ref, wscat_ref, bsmat_ref, retg_ref, dec_ref, kdec_ref, qdec_ref,
                cdec_ref, bd_ref, gmat_ref, mk_ref, mv_ref, wb_ref, wo_ref,
                o_ref, carry_scr, state_scr, ysg_scr, yret_scr, *, tm, tpb):
    i = pl.program_id(0)

    @pl.when(i % tpb == 0)
    def _():
        carry_scr[...] = jnp.zeros(carry_scr.shape, F32)
        state_scr[...] = jnp.zeros(state_scr.shape, F32)

    w = MIX_W
    a_b = a_ref[:, 0:w].astype(F32)
    u = a_ref[:, w:2 * w].astype(F32) * a_ref[:, 2 * w:3 * w].astype(F32)
    rowi = lax.broadcasted_iota(jnp.int32, (tm, w), 0)
    prev1 = carry_scr[0:1, :]
    prev2 = carry_scr[1:2, :]
    u1 = jnp.where(rowi == 0, prev1, pltpu.roll(u, 1, axis=0))
    u2 = jnp.where(rowi == 0, prev2, jnp.where(rowi == 1, prev1, pltpu.roll(u, 2, axis=0)))
    carry_scr[0:1, :] = u[tm - 1:tm, :]
    carry_scr[1:2, :] = u[tm - 2:tm - 1, :]
    y_conv = a_b * (convw_ref[0:1, :] * u2 + convw_ref[1:2, :] * u1 + convw_ref[2:3, :] * u)

    gmat = gmat_ref[...]
    s_u = _gelu_tanh(su_ref[:, 0:w].astype(F32))
    s_v = _gelu_tanh(su_ref[:, w:2 * w].astype(F32))
    ms = _group_mean(s_v * s_v, gmat)
    vn = (s_v * lax.rsqrt(ms + EPS) * gvg_ref[...]).astype(BF16)

    cosr = cos_ref[...]
    sinr = sin_ref[...]

    def rot(t):
        t1, t2 = t[:, 0:LANES], t[:, LANES:2 * LANES]
        return jnp.concatenate([t1 * cosr - t2 * sinr, t2 * cosr + t1 * sinr], axis=-1)

    rq = rot(r_ref[:, 0:w].astype(F32))
    rk = rot(r_ref[:, w:2 * w].astype(F32)) * (HEAD_DIM ** -0.5)

    for c in range(tm // CHUNK):
        rows = slice(c * CHUNK, (c + 1) * CHUNK)
        vc = vn[rows, :]
        vbd = jnp.concatenate([vc * mv_ref[g:g + 1, :].astype(BF16) for g in range(N_GROUPS)], axis=0)
        mixed = jnp.dot(wscat_ref[...], vbd, preferred_element_type=F32) + bsmat_ref[...]
        ysg_scr[rows, :] = s_u[rows, :] * mixed

        qc = rq[rows, :]
        kc = rk[rows, :]
        kcb = kc.astype(BF16)
        vcb = r_ref[rows, 2 * w:3 * w]
        qstack = jnp.concatenate([(qc * mk_ref[h:h + 1, :]).astype(BF16) for h in range(N_GROUPS)], axis=0)
        sc = lax.dot_general(qstack, kcb, (((1,), (1,)), ((), ())), preferred_element_type=F32)
        sc = (sc * dec_ref[...]).astype(BF16)
        scat = jnp.concatenate([sc[h * CHUNK:(h + 1) * CHUNK, :] for h in range(N_GROUPS)], axis=1)
        vstack = jnp.concatenate([vcb * mv_ref[h:h + 1, :].astype(BF16) for h in range(N_GROUPS)], axis=0)
        o_c = jnp.dot(scat, vstack, preferred_element_type=F32)
        state = state_scr[...]
        o_c = o_c + jnp.dot((qc * qdec_ref[...]).astype(BF16), state.astype(BF16),
                            preferred_element_type=F32)
        kd_t = jnp.transpose(kc * kdec_ref[...]).astype(BF16)
        kv = jnp.dot(kd_t, vcb, preferred_element_type=F32)
        state_scr[...] = state * cdec_ref[...] + kv * bd_ref[...]
        yret_scr[rows, :] = o_c

    o_all = yret_scr[...]
    xc = o_all - _group_mean(o_all, gmat)
    var = _group_mean(xc * xc, gmat)
    r_g = r_ref[:, 3 * w:4 * w].astype(F32)
    y_ret = (r_g * _sigmoid(r_g)) * (xc * lax.rsqrt(var + EPS) * retg_ref[...])

    d = x_ref.shape[1]
    ys = (y_conv, ymla_ref[...], ysg_scr[...], y_ret)
    merged = None
    for n in range(N_BRANCH):
        gate = _sigmoid(gates_ref[:, n * d:(n + 1) * d])
        term = gate * jnp.dot(ys[n].astype(BF16), wb_ref[n], preferred_element_type=F32).astype(BF16)
        merged = term if merged is None else merged + term
    out = jnp.dot(merged, wo_ref[...], preferred_element_type=F32)
    o_ref[...] = x_ref[...] + mod_ref[2:3, :] * out


def _mixers(proj, ymla, x, cosr, sinr, mod, p, seq):
    n, d = x.shape
    tm = TM_MIX
    tpb = seq // tm

    def col(width, offset):
        return pl.BlockSpec((tm, width), lambda i: (i, offset // width))

    def full(a):
        return pl.BlockSpec(a.shape, lambda i: (0,) * a.ndim)

    consts = [p["conv_w"], p["gv_g"], p["ws_cat"], p["bs_mat"], p["ret_g"], p["dec"], p["kdec"],
              p["qdec"], p["cdec"], p["bd"], p["gmat"], p["mk"], p["mv"], p["w_branch"], p["w_o"]]
    return pl.pallas_call(
        functools.partial(_mix_kernel, tm=tm, tpb=tpb),
        out_shape=jax.ShapeDtypeStruct((n, d), F32),
        grid=(n // tm,),
        in_specs=[col(N_BRANCH * d, COL_GATES), col(4 * MIX_W, COL_A), col(4 * MIX_W, COL_R),
                  col(2 * MIX_W, COL_SU),
                  pl.BlockSpec((tm, MIX_W), lambda i: (i, 0)),
                  pl.BlockSpec((tm, d), lambda i: (i, 0)),
                  pl.BlockSpec((tm, LANES), lambda i: (i, 0)),
                  pl.BlockSpec((tm, LANES), lambda i: (i, 0)),
                  pl.BlockSpec((None, 6, d), lambda i: (i // tpb, 0, 0))]
                 + [full(c) for c in consts],
        out_specs=pl.BlockSpec((tm, d), lambda i: (i, 0)),
        scratch_shapes=[pltpu.VMEM((8, MIX_W), F32), pltpu.VMEM((MIX_W, MIX_W), F32),
                        pltpu.VMEM((tm, MIX_W), F32), pltpu.VMEM((tm, MIX_W), F32)],
        compiler_params=_cparams("arbitrary"),
        name="mixers_merge",
    )(proj, proj, proj, proj, ymla, x, cosr, sinr, mod, *consts)


def _ffn_kernel(x_ref, mod_ref, g_ref, w1_ref, w3_ref, w2_ref, o_ref):
    x = x_ref[...]
    h = _norm_mod(x, g_ref[...], mod_ref[3:4, :], mod_ref[4:5, :]).astype(BF16)
    acc = None
    for c0 in range(0, w1_ref.shape[1], FF_CHUNK):
        cols = slice(c0, c0 + FF_CHUNK)
        a = jnp.dot(h, w1_ref[:, cols], preferred_element_type=F32)
        b = jnp.dot(h, w3_ref[:, cols], preferred_element_type=F32)
        hid = ((a * _sigmoid(a)) * b).astype(BF16)
        part = jnp.dot(hid, w2_ref[cols, :], preferred_element_type=F32)
        acc = part if acc is None else acc + part
    o_ref[...] = x + mod_ref[5:6, :] * acc


def _dense_ffn(x, mod, g, w1, w3, w2, seq):
    n, d = x.shape
    dff = w1.shape[1]
    tm = TM_FFN
    tpb = seq // tm
    return pl.pallas_call(
        _ffn_kernel,
        out_shape=jax.ShapeDtypeStruct((n, d), F32),
        grid=(n // tm,),
        in_specs=[
            pl.BlockSpec((tm, d), lambda i: (i, 0)),
            pl.BlockSpec((None, 6, d), lambda i: (i // tpb, 0, 0)),
            pl.BlockSpec((1, d), lambda i: (0, 0)),
            _resident((d, dff)), _resident((d, dff)), _resident((dff, d)),
        ],
        out_specs=pl.BlockSpec((tm, d), lambda i: (i, 0)),
        compiler_params=_cparams("parallel"),
        name="dense_swiglu",
    )(x, mod, g, w1, w3, w2)


def _router_kernel(x_ref, mod_ref, g_ref, rw_ref, rb_ref, hs_ref, ei_ref, pw_ref, meta_ref, tot_ref,
                   carry_scr, *, tm, srows):
    i = pl.program_id(0)

    @pl.when(i == 0)
    def _():
        carry_scr[...] = jnp.zeros(carry_scr.shape, F32)

    h = _norm_mod(x_ref[...], g_ref[...], mod_ref[3:4, :], mod_ref[4:5, :])

    h_hi = h.astype(BF16)
    h_lo = (h - h_hi.astype(F32)).astype(BF16)
    hw = jnp.dot(h_hi, rw_ref[...], preferred_element_type=F32)
    logits = (hw[:, :LANES] + hw[:, LANES:] + jnp.dot(h_lo, rw_ref[:, :LANES], preferred_element_type=F32)
              + rb_ref[...])
    mx = jnp.max(logits, axis=-1, keepdims=True)
    ex = jnp.exp(logits - mx)
    probs = ex / jnp.sum(ex, axis=-1, keepdims=True)
    lane = lax.broadcasted_iota(jnp.int32, (tm, LANES), 1)
    valid = lane < N_EXPERTS
    probs = jnp.where(valid, probs, -1.0)
    lane_f = lane.astype(F32)
    m1 = jnp.max(probs, axis=-1, keepdims=True)
    i1 = jnp.min(jnp.where(probs == m1, lane_f, float(LANES)), axis=-1, keepdims=True)
    rest = jnp.where(lane_f == i1, -1.0, probs)
    m2 = jnp.max(rest, axis=-1, keepdims=True)
    i2 = jnp.min(jnp.where(rest == m2, lane_f, float(LANES)), axis=-1, keepdims=True)
    den = m1 + m2
    pw_ref[...] = jnp.where(lane == 0, m1 / den, jnp.where(lane == 1, m2 / den, 0.0))

    sel1 = lane_f == i1
    sel2 = lane_f == i2
    onehot = jnp.where(sel1, 1.0, 0.0) + jnp.where(sel2, 1.0, 0.0)
    r_i = lax.broadcasted_iota(jnp.int32, (tm, tm), 0)
    c_i = lax.broadcasted_iota(jnp.int32, (tm, tm), 1)
    tri = jnp.where(c_i < r_i, 1.0, 0.0).astype(BF16)
    before = jnp.dot(tri, onehot.astype(BF16), preferred_element_type=F32)
    cnt = jnp.sum(onehot, axis=0, keepdims=True)
    cnt_al = jnp.floor((cnt + (ROW_ALIGN - 1)) * (1.0 / ROW_ALIGN)) * ROW_ALIGN
    e_r = lax.broadcasted_iota(jnp.int32, (LANES, LANES), 0)
    e_c = lax.broadcasted_iota(jnp.int32, (LANES, LANES), 1)
    upper = jnp.where(e_r < e_c, 1.0, 0.0)
    loff = jnp.dot(jnp.broadcast_to(cnt_al, (8, LANES)), upper, precision=HIGHEST,
                   preferred_element_type=F32)[0:1, :]
    slot = loff + before
    slot1 = jnp.sum(jnp.where(sel1, slot, 0.0), axis=-1, keepdims=True).astype(jnp.int32)
    slot2 = jnp.sum(jnp.where(sel2, slot, 0.0), axis=-1, keepdims=True).astype(jnp.int32)
    ei = jnp.where(lane == 0, i1, jnp.where(lane == 1, i2, 0.0)).astype(jnp.int32)
    ei_ref[...] = jnp.where(lane == 2, slot1, jnp.where(lane == 3, slot2, ei))

    r_idx = lax.broadcasted_iota(jnp.int32, (tm, srows), 1)
    place = jnp.where(r_idx == slot1, 1.0, jnp.where(r_idx == slot2, 1.0, 0.0)).astype(BF16)
    hs = lax.dot_general(place, h.astype(BF16), (((0,), (0,)), ((), ())), preferred_element_type=F32)
    half = hs.shape[1] // 2
    hs_ref[...] = _pack_bf16_pair(hs[:, :half], hs[:, half:])

    carry = carry_scr[0:1, :]
    mrow = lax.broadcasted_iota(jnp.int32, (8, LANES), 0)
    meta = jnp.where(mrow == 0, cnt_al, jnp.where(mrow == 1, carry, jnp.where(mrow == 2, loff, 0.0)))
    meta_ref[...] = meta.astype(jnp.int32)
    carry_scr[0:1, :] = carry + cnt_al
    tot_ref[...] = jnp.broadcast_to(carry + cnt_al, tot_ref.shape).astype(jnp.int32)


def _router(x, mod, g, rw_pad, rb_pad, seq):
    n, d = x.shape
    tm = TM_ROUTE
    tpb = seq // tm
    nt = n // tm
    return pl.pallas_call(
        functools.partial(_router_kernel, tm=tm, srows=SORT_ROWS),
        out_shape=(jax.ShapeDtypeStruct((nt * SORT_ROWS, d // 2), jnp.uint32),
                   jax.ShapeDtypeStruct((n, LANES), jnp.int32),
                   jax.ShapeDtypeStruct((n, LANES), F32),
                   jax.ShapeDtypeStruct((nt, 8, LANES), jnp.int32),
                   jax.ShapeDtypeStruct((8, LANES), jnp.int32)),
        grid=(nt,),
        in_specs=[
            pl.BlockSpec((tm, d), lambda i: (i, 0)),
            pl.BlockSpec((None, 6, d), lambda i: (i // tpb, 0, 0)),
            pl.BlockSpec((1, d), lambda i: (0, 0)),
            pl.BlockSpec((d, 2 * LANES), lambda i: (0, 0)),
            pl.BlockSpec((1, LANES), lambda i: (0, 0)),
        ],
        out_specs=(pl.BlockSpec((SORT_ROWS, d // 2), lambda i: (i, 0)),
                   pl.BlockSpec((tm, LANES), lambda i: (i, 0)),
                   pl.BlockSpec((tm, LANES), lambda i: (i, 0)),
                   pl.BlockSpec((None, 8, LANES), lambda i: (i, 0, 0)),
                   pl.BlockSpec((8, LANES), lambda i: (0, 0))),
        scratch_shapes=[pltpu.VMEM((8, LANES), F32)],
        compiler_params=_cparams("arbitrary"),
        name="router_top2",
    )(x, mod, g, rw_pad, rb_pad)


def _segment_copy(src_hbm, dst_hbm, src_row, dst_row, n_rows, sem):
    src_row = pl.multiple_of(src_row, ROW_ALIGN)
    dst_row = pl.multiple_of(dst_row, ROW_ALIGN)
    n_rows = pl.multiple_of(n_rows, ROW_ALIGN)
    return pltpu.make_async_copy(src_hbm.at[pl.ds(src_row, n_rows)], dst_hbm.at[pl.ds(dst_row, n_rows)], sem)


def _expert_kernel(te_ref, used_ref, toff_ref, ilo_ref, ihi_ref, cnt_ref, carry_ref, loff_ref,
                   hs_hbm, w1_hbm, w3_hbm, w2_hbm, y_ref, wb1, wb3, wb2, stage, sem, xbuf, xsem, rows_smem,
                   *, srows):
    t = pl.program_id(0)
    e = te_ref[t]
    tg = xbuf.shape[1]
    slot = t % 2
    first_of_expert = jnp.logical_or(t == 0, e != te_ref[jnp.maximum(t - 1, 0)])

    def fetch(tile, into):
        xbuf[into] = jnp.zeros(xbuf.shape[1:], xbuf.dtype)
        expert = te_ref[tile]
        first_row = toff_ref[tile]

        def piece(i, total):
            s = i * N_EXPERTS + expert
            run_start = carry_ref[s]
            lo = jnp.maximum(run_start, first_row)
            hi = jnp.minimum(run_start + cnt_ref[s], first_row + tg)

            @pl.when(hi > lo)
            def _():
                _segment_copy(hs_hbm, xbuf.at[into], i * srows + loff_ref[s] + (lo - run_start), lo - first_row,
                              hi - lo, xsem.at[into]).start()

            return total + jnp.maximum(hi - lo, 0)

        rows_smem[into] = lax.fori_loop(ilo_ref[tile], ihi_ref[tile], piece, 0)

    @pl.when(t == 0)
    def _():
        fetch(t, slot)

    @pl.when(t + 1 < pl.num_programs(0))
    def _():
        fetch(t + 1, 1 - slot)

    @pl.when(rows_smem[slot] > 0)
    def _():
        _segment_copy(hs_hbm, xbuf.at[slot], 0, 0, rows_smem[slot], xsem.at[slot]).wait()

    def rows_bf16():
        lo, hi = _unpack_bf16_pair(xbuf[slot])
        return jnp.concatenate([lo.astype(BF16), hi.astype(BF16)], axis=1)

    def swiglu(h, cols):
        a = jnp.dot(h, wb1[:, cols], preferred_element_type=F32)
        b = jnp.dot(h, wb3[:, cols], preferred_element_type=F32)
        hid = ((a * _sigmoid(a)) * b).astype(BF16)
        return jnp.dot(hid, wb2[cols, :], preferred_element_type=F32)

    def emit(acc):
        half = acc.shape[1] // 2
        y_ref[...] = _pack_bf16_pair(acc[:, :half], acc[:, half:])

    @pl.when(jnp.logical_and(used_ref[t] == 1, first_of_expert))
    def _():
        n_slots = stage.shape[0]
        dff, d = wb1.shape[1], wb1.shape[0]
        windows = [(src, dst, r, c)
                   for ch in range(dff // W_CHUNK)
                   for src, dst, r_list, c_list in ((w1_hbm, wb1, range(d // W_CHUNK), [ch]),
                                                    (w3_hbm, wb3, range(d // W_CHUNK), [ch]),
                                                    (w2_hbm, wb2, [ch], range(d // W_CHUNK)))
                   for r in r_list for c in c_list]
        per_chunk = len(windows) // (dff // W_CHUNK)

        def staged_copy(k):
            src, _, r, c = windows[k]
            return pltpu.make_async_copy(src.at[e, pl.ds(r * W_CHUNK, W_CHUNK), pl.ds(c * W_CHUNK, W_CHUNK)],
                                         stage.at[k % n_slots], sem.at[k % n_slots])

        for k in range(n_slots - 1):
            staged_copy(k).start()
        h = rows_bf16()
        acc = None
        for k, (_, dst, r, c) in enumerate(windows):
            staged_copy(k).wait()
            dst[r * W_CHUNK:(r + 1) * W_CHUNK, c * W_CHUNK:(c + 1) * W_CHUNK] = stage[k % n_slots].astype(BF16)
            if k + n_slots - 1 < len(windows):
                staged_copy(k + n_slots - 1).start()
            if (k + 1) % per_chunk == 0:
                ch = k // per_chunk
                part = swiglu(h, slice(ch * W_CHUNK, (ch + 1) * W_CHUNK))
                acc = part if acc is None else acc + part
        emit(acc)

    @pl.when(jnp.logical_and(used_ref[t] == 1, jnp.logical_not(first_of_expert)))
    def _():
        h = rows_bf16()
        acc = None
        for ch in range(wb1.shape[1] // W_CHUNK):
            part = swiglu(h, slice(ch * W_CHUNK, (ch + 1) * W_CHUNK))
            acc = part if acc is None else acc + part
        emit(acc)

    @pl.when(used_ref[t] == 0)
    def _():
        y_ref[...] = jnp.zeros(y_ref.shape, y_ref.dtype)


def _expert_ffn(tile_tables, seg_tables, hs, w1, w3, w2, n_tiles):
    half = hs.shape[1]
    d = 2 * half
    dff = w1.shape[2]
    tg = TG_MOE
    assert dff % W_CHUNK == 0 and d % W_CHUNK == 0
    hbm = pl.BlockSpec(memory_space=pl.ANY)
    tables = tuple(tile_tables) + tuple(seg_tables)
    return pl.pallas_call(
        functools.partial(_expert_kernel, srows=SORT_ROWS),
        out_shape=jax.ShapeDtypeStruct((n_tiles * tg, half), jnp.uint32),
        grid_spec=pltpu.PrefetchScalarGridSpec(
            num_scalar_prefetch=len(tables),
            grid=(n_tiles,),
            in_specs=[hbm, hbm, hbm, hbm],
            out_specs=pl.BlockSpec((tg, half), lambda t, *_: (t, 0)),
            scratch_shapes=[pltpu.VMEM((d, dff), BF16), pltpu.VMEM((d, dff), BF16), pltpu.VMEM((dff, d), BF16),
                            pltpu.VMEM((W_SLOTS, W_CHUNK, W_CHUNK), F32), pltpu.SemaphoreType.DMA((W_SLOTS,)),
                            pltpu.VMEM((2, tg, half), jnp.uint32), pltpu.SemaphoreType.DMA((2,)),
                            pltpu.SMEM((2,), jnp.int32)],
        ),
        compiler_params=_cparams("arbitrary"),
        name="expert_swiglu",
    )(*tables, hs, w1, w3, w2)


def _combine_kernel(src_ref, loff_ref, cnt_ref, rows_ref, x_ref, ei_ref, pw_ref, mod_ref, y_ref, o_ref,
                    ybuf, sem, *, tm, srows):
    i = pl.program_id(0)
    slot = i % 2

    def fetch(tile, into):
        ybuf[into] = jnp.zeros(ybuf.shape[1:], ybuf.dtype)
        for e in range(N_EXPERTS):
            s = tile * N_EXPERTS + e
            n_rows = cnt_ref[s]

            @pl.when(n_rows > 0)
            def _():
                _segment_copy(y_ref, ybuf.at[into], src_ref[s], loff_ref[s], n_rows, sem.at[into]).start()

    @pl.when(i == 0)
    def _():
        fetch(i, slot)

    @pl.when(i + 1 < pl.num_programs(0))
    def _():
        fetch(i + 1, 1 - slot)

    @pl.when(rows_ref[i] > 0)
    def _():
        _segment_copy(y_ref, ybuf.at[slot], 0, 0, rows_ref[i], sem.at[slot]).wait()

    lo, hi = _unpack_bf16_pair(ybuf[slot])
    ys = jnp.concatenate([lo.astype(BF16), hi.astype(BF16)], axis=1)
    r_idx = lax.broadcasted_iota(jnp.int32, (tm, srows), 1)
    mix = jnp.zeros(x_ref.shape, F32)
    for k in range(TOP_K):
        pick = jnp.where(r_idx == ei_ref[:, TOP_K + k:TOP_K + k + 1], 1.0, 0.0).astype(BF16)
        mix = mix + pw_ref[:, k:k + 1] * jnp.dot(pick, ys, preferred_element_type=F32)
    o_ref[...] = x_ref[...] + mod_ref[5:6, :] * mix


def _combine(seg_src, seg_loff, seg_cnt, tile_rows, x, ei, pw, mod, y, seq):
    n, d = x.shape
    tm = TM_ROUTE
    tpb = seq // tm
    tok = lambda width: pl.BlockSpec((tm, width), lambda i, *_: (i, 0))
    return pl.pallas_call(
        functools.partial(_combine_kernel, tm=tm, srows=SORT_ROWS),
        out_shape=jax.ShapeDtypeStruct((n, d), F32),
        grid_spec=pltpu.PrefetchScalarGridSpec(
            num_scalar_prefetch=4,
            grid=(n // tm,),
            in_specs=[tok(d), tok(LANES), tok(LANES),
                      pl.BlockSpec((None, 6, d), lambda i, *_: (i // tpb, 0, 0)),
                      pl.BlockSpec(memory_space=pl.ANY)],
            out_specs=tok(d),
            scratch_shapes=[pltpu.VMEM((2, SORT_ROWS, d // 2), jnp.uint32), pltpu.SemaphoreType.DMA((2,))],
        ),
        compiler_params=_cparams("arbitrary"),
        name="moe_combine",
    )(seg_src, seg_loff, seg_cnt, tile_rows, x, ei, pw, mod, y)


def _pack_w_in(w_in_all, layer):
    d = w_in_all.shape[1]
    col = lambda start, width: w_in_all[layer, :, start:start + width]
    w = MIX_W
    o_ckv = 3 * w + Q_LORA
    o_kr = o_ckv + KV_LORA
    o_su = o_kr + QK_ROPE
    o_rq = o_su + 2 * w
    o_gate = o_rq + 4 * w
    half = HEAD_DIM // 2
    perm = np.array([h * HEAD_DIM + part * half + i
                     for part in range(2) for h in range(N_GROUPS) for i in range(half)])
    z = lambda k: jnp.zeros((d, k), w_in_all.dtype)
    cols = [
        col(o_gate, N_BRANCH * d),
        col(0, 3 * w + Q_LORA),
        col(o_rq, w)[:, perm], col(o_rq + w, w)[:, perm],
        col(o_rq + 2 * w, 2 * w),
        col(o_su, 2 * w),
        col(o_ckv, KV_LORA),
        z(QK_NOPE), col(o_kr, QK_ROPE), z(LANES - QK_HEAD),
    ]
    return jnp.concatenate(cols, axis=1).astype(BF16)


def _swap_rope_halves(a):
    hr = QK_ROPE // 2
    return jnp.concatenate([a[..., :QK_NOPE], a[..., QK_NOPE + hr:QK_HEAD], a[..., QK_NOPE:QK_NOPE + hr],
                            a[..., QK_HEAD:]], axis=-1)


def _mla_params(cq_g, w_uq, ckv_g, w_ukv, qn_g, kn_g):
    pad = LANES - QK_HEAD
    wq = w_uq.reshape(Q_LORA, N_GROUPS, QK_HEAD)
    wq = jnp.pad(wq, ((0, 0), (0, 0), (0, pad)))
    wkv = w_ukv.reshape(KV_LORA, N_GROUPS, QK_NOPE + V_HEAD)
    wk = jnp.pad(wkv[:, :, :QK_NOPE], ((0, 0), (0, 0), (0, LANES - QK_NOPE)))
    wv = jnp.pad(wkv[:, :, QK_NOPE:], ((0, 0), (0, 0), (0, LANES - V_HEAD)))
    qg = jnp.pad(qn_g, (0, pad))[None, :]
    kg = jnp.pad(kn_g, (0, pad))[None, :]
    bound = (QK_HEAD ** 0.5 * LOG2_E) * jnp.max(jnp.abs(qn_g)) * jnp.max(jnp.abs(kn_g))
    static_shift = bound <= MAX_STATIC_SHIFT
    lane = jnp.arange(LANES)
    qaug = (lane == QK_HEAD).astype(F32)[None, :]
    kaug = qaug * jnp.where(static_shift, -bound, 0.0)
    vaug = jnp.tile((lane == V_HEAD).astype(F32), N_GROUPS)[None, :]
    params = {
        "cq_g": cq_g[None, :], "ckv_g": ckv_g[None, :],
        "wqa": wq.reshape(Q_LORA, -1).astype(BF16),
        "wqb": _swap_rope_halves(wq).reshape(Q_LORA, -1).astype(BF16),
        "wk": wk.reshape(KV_LORA, -1).astype(BF16),
        "wv": wv.reshape(KV_LORA, -1).astype(BF16),
        "qga": qg, "qgb": _swap_rope_halves(qg), "kga": kg, "kgb": _swap_rope_halves(kg),
        "qaug": qaug, "kaug": kaug, "vaug": vaug,
        "swap": (_swap_rope_halves(lane[None, :])[0][None, :] == lane[:, None]).astype(BF16),
    }
    return params, static_shift


def _mixer_consts():
    f32 = np.float32
    h = np.arange(N_GROUPS, dtype=f32)
    log_gamma = np.log1p(-(f32(2.0) ** (f32(-5.0) - h))).astype(f32)
    pos = np.arange(CHUNK, dtype=f32)
    rel = pos[:, None] - pos[None, :]
    dec = np.where(rel >= 0, np.exp(log_gamma[:, None, None] * np.maximum(rel, f32(0.0))), f32(0.0)).astype(f32)
    lane = np.arange(MIX_W)
    head_k = (lane % LANES) // (HEAD_DIM // 2)
    head_v = lane // HEAD_DIM
    lg_k = log_gamma[head_k]
    return {
        "dec": jnp.asarray(dec.reshape(N_GROUPS * CHUNK, CHUNK)),
        "kdec": jnp.asarray(np.exp(lg_k[None, :] * (CHUNK - 1.0 - pos)[:, None]).astype(f32)),
        "qdec": jnp.asarray(np.exp(lg_k[None, :] * (pos + 1.0)[:, None]).astype(f32)),
        "cdec": jnp.asarray(np.broadcast_to(np.exp(lg_k * f32(CHUNK)).astype(f32)[:, None], (MIX_W, MIX_W))),
        "bd": jnp.asarray((head_k[:, None] == head_v[None, :]).astype(f32)),
        "gmat": jnp.asarray((head_v[:, None] == head_v[None, :]).astype(f32) / HEAD_DIM).astype(BF16),
        "mk": jnp.asarray((head_k[None, :] == np.arange(N_GROUPS)[:, None]).astype(f32)),
        "mv": jnp.asarray((head_v[None, :] == np.arange(N_GROUPS)[:, None]).astype(f32)),
    }


def _mixer_params(consts, conv_w, gv_g, w_s, b_s, ret_g, w_branch, w_o):
    p = dict(consts)
    ws = jnp.tril(w_s)
    p.update({
        "conv_w": conv_w,
        "gv_g": gv_g.reshape(1, MIX_W),
        "ws_cat": jnp.transpose(ws, (1, 0, 2)).reshape(CHUNK, N_GROUPS * CHUNK).astype(BF16),
        "bs_mat": jnp.repeat(b_s.T, HEAD_DIM, axis=1),
        "ret_g": ret_g.reshape(1, MIX_W),
        "w_branch": w_branch.astype(BF16),
        "w_o": w_o.astype(BF16),
    })
    return p


def _moe_layout(meta, tot, n_tiles):
    totals = tot[0, :N_EXPERTS]
    padded = ((totals + TG_MOE - 1) // TG_MOE) * TG_MOE
    ends = jnp.cumsum(padded)
    starts = ends - padded
    seg_cnt = meta[:, 0, :N_EXPERTS]
    seg_carry = meta[:, 1, :N_EXPERTS]
    seg_loff = meta[:, 2, :N_EXPERTS]
    seg_grouped = starts[None, :] + seg_carry
    tile_start = jnp.arange(n_tiles, dtype=jnp.int32) * TG_MOE
    tile_e = jnp.sum((tile_start[:, None] >= ends[None, :]).astype(jnp.int32), axis=1)
    used = (tile_start < ends[-1]).astype(jnp.int32)
    last_e = jnp.sum((ends[-1] - 1 >= ends).astype(jnp.int32))
    tile_e = jnp.minimum(jnp.where(used == 1, tile_e, last_e), N_EXPERTS - 1)
    tile_off = tile_start - starts[tile_e]
    run_start = seg_carry[:, tile_e]
    run_end = run_start + seg_cnt[:, tile_e]
    tile_ilo = jnp.sum((run_end <= tile_off[None, :]).astype(jnp.int32), axis=0) * used
    tile_ihi = jnp.sum((run_start < tile_off[None, :] + TG_MOE).astype(jnp.int32), axis=0) * used
    flat = lambda a: a.reshape(-1).astype(jnp.int32)
    tile_tables = (flat(tile_e), flat(used), flat(tile_off * used), flat(tile_ilo), flat(tile_ihi))
    seg_tables = (flat(seg_cnt), flat(seg_carry), flat(seg_loff))
    return tile_tables, seg_tables, flat(seg_grouped)


def kernel(x, c, positions, norm1_g, norm2_g, ada_w, ada_b, w_in, conv_w, cq_g, w_uq, ckv_g, w_ukv, qn_g, kn_g, gv_g, w_s, b_s, ret_g, w_branch, w_o, ffn_w1, ffn_w3, ffn_w2, router_w, router_b, moe_w1, moe_w3, moe_w2):
    batch, seq, d = x.shape
    depth = ada_w.shape[0]
    n = batch * seq
    assert seq % max(TM_PROJ, TQ_ATT, TM_MIX, TM_FFN, TM_ROUTE) == 0
    assert d // 2 % LANES == 0

    c_t = jnp.pad(c, ((0, 8 - batch), (0, 0))).T
    ada = _ada(c_t, ada_w, ada_b, batch)[:, :batch].reshape(depth, batch, 6, d)
    cosr, sinr, cm, sm = _rope_tables(positions.astype(F32).reshape(n, 1))

    mixer_consts = _mixer_consts()
    xt = x.reshape(n, d)
    for l in range(depth):
        mod = ada[l]
        mla_p, static_shift = _mla_params(cq_g[l], w_uq[l], ckv_g[l], w_ukv[l], qn_g[l], kn_g[l])
        proj, q, k, v = _inproj(xt, mod, norm1_g[l][None, :], _pack_w_in(w_in, l), cm, sm, mla_p, seq)
        y_mla = lax.cond(static_shift,
                         functools.partial(_flash, batch=batch, seq=seq, online_max=False),
                         functools.partial(_flash, batch=batch, seq=seq, online_max=True), q, k, v)
        mp = _mixer_params(mixer_consts, conv_w[l], gv_g[l], w_s[l], b_s[l], ret_g[l], w_branch[l], w_o[l])
        xt = _mixers(proj, y_mla, xt, cosr, sinr, mod, mp, seq)
        g2n = norm2_g[l][None, :]
        if l % 2 == 0:
            i = l // 2
            xt = _dense_ffn(xt, mod, g2n, ffn_w1[i].astype(BF16), ffn_w3[i].astype(BF16),
                            ffn_w2[i].astype(BF16), seq)
        else:
            i = l // 2
            rw = jnp.pad(router_w[i], ((0, 0), (0, LANES - N_EXPERTS)))
            rw_hi = rw.astype(BF16)
            rw_pad = jnp.concatenate([rw_hi, (rw - rw_hi.astype(F32)).astype(BF16)], axis=1)
            rb_pad = jnp.pad(router_b[i], (0, LANES - N_EXPERTS), constant_values=-1e30)[None, :]
            hs, ei, pw, meta, tot = _router(xt, mod, g2n, rw_pad, rb_pad, seq)
            max_rows = n * TOP_K + N_EXPERTS * (n // TM_ROUTE) * (ROW_ALIGN - 1)
            n_tiles = -(-max_rows // TG_MOE) + N_EXPERTS
            tile_tables, seg_tables, seg_grouped = _moe_layout(meta, tot, n_tiles)
            seg_cnt, _, seg_loff = seg_tables
            tile_rows = jnp.sum(seg_cnt.reshape(-1, N_EXPERTS), axis=1)
            y = _expert_ffn(tile_tables, seg_tables, hs, moe_w1[i], moe_w3[i], moe_w2[i], n_tiles)
            xt = _combine(seg_grouped, seg_loff, seg_cnt, tile_rows, xt, ei, pw, mod, y, seq)
    return xt.reshape(batch, seq, d)
```
